```python
import math
import jax
import jax.numpy as jnp
from jax import lax
import numpy as np

D_MODEL = 1024
BATCH = 4
SEQ = 8192
DEPTH = 1
DEC_BATCH = 16
DEC_SEQ = 64
PAST_LEN = 1024

CHUNK = 64
PLE_DIM = 256
D_S5 = D_MODEL // 2
S5_GROUP = 16
S5_GROUPS = D_S5 // S5_GROUP
S5_STATE = 64
D_HG = D_MODEL - D_S5
HG_HEADS = 4
HG_DK = D_HG // HG_HEADS
HG_DV = D_HG // HG_HEADS
D_MIX = D_S5 + D_HG
D_IN = D_S5 + 4 * D_HG
N_EXPERTS = 32
TOP_K = 4
D_FF = D_MODEL
SWIGLU_LIMIT = 7.0
SWIGLU_ALPHA = 1.702
MOE_BLOCK = 128
DEEPNORM_ALPHA = (2 * DEPTH) ** 0.25
DEEPNORM_BETA = (8 * DEPTH) ** -0.25
LN_EPS = 1e-5
RMS_EPS = 1e-6

kernel_name = 'hybrid_s5_hgrn2_moe_stream_step'


def _layernorm(x, g, b):
    xf = x.astype(jnp.float32)
    mu = jnp.mean(xf, axis=-1, keepdims=True)
    var = jnp.mean(jnp.square(xf - mu), axis=-1, keepdims=True)
    y = (xf - mu) * lax.rsqrt(var + LN_EPS) * g.astype(jnp.float32) + b.astype(jnp.float32)
    return y.astype(x.dtype)


def _rmsnorm(x, g):
    xf = x.astype(jnp.float32)
    return xf * lax.rsqrt(jnp.mean(xf * xf, axis=-1, keepdims=True) + RMS_EPS) * g.astype(jnp.float32)


def _complex_affine_combine(e1, e2):
    a1r, a1i, b1r, b1i = e1
    a2r, a2i, b2r, b2i = e2
    return (a2r * a1r - a2i * a1i,
            a2r * a1i + a2i * a1r,
            a2r * b1r - a2i * b1i + b2r,
            a2r * b1i + a2i * b1r + b2i)


def _s5_mixer(u, st_re, st_im, lam_re, lam_im, log_step, b_re, b_im, c_re, c_im, d_skip):
    n, L, _ = u.shape
    f32 = jnp.float32
    uf = u.astype(f32)
    ug = uf.reshape(n, L, S5_GROUPS, S5_GROUP)
    lr = jnp.minimum(lam_re.astype(f32), -1e-4)
    li = lam_im.astype(f32)
    step = jnp.exp(log_step.astype(f32))[:, None]
    dr, di = lr * step, li * step
    mag = jnp.exp(dr)
    a_re, a_im = mag * jnp.cos(di), mag * jnp.sin(di)
    den = lr * lr + li * li
    nr = a_re - 1.0
    fr = (nr * lr + a_im * li) / den
    fi = (a_im * lr - nr * li) / den
    br, bi = b_re.astype(f32), b_im.astype(f32)
    bbar_re = fr[..., None] * br - fi[..., None] * bi
    bbar_im = fr[..., None] * bi + fi[..., None] * br
    cr, ci = c_re.astype(f32), c_im.astype(f32)
    c = min(L, CHUNK)
    nc = L // c
    t = jnp.arange(1, c + 1, dtype=f32)[:, None, None]
    pmag = jnp.exp(dr[None] * t)
    p_re, p_im = pmag * jnp.cos(di[None] * t), pmag * jnp.sin(di[None] * t)
    u_blocks = ug.reshape(n, nc, c, S5_GROUPS, S5_GROUP).transpose(1, 0, 2, 3, 4)

    def block_step(carry, ub):
        sr, si = carry
        bu_re = jnp.einsum('ncgh,gph->ncgp', ub, bbar_re)
        bu_im = jnp.einsum('ncgh,gph->ncgp', ub, bbar_im)
        ar = jnp.broadcast_to(a_re, bu_re.shape)
        ai = jnp.broadcast_to(a_im, bu_re.shape)
        _, _, hr, hi = lax.associative_scan(_complex_affine_combine, (ar, ai, bu_re, bu_im), axis=1)
        xr = hr + p_re * sr[:, None] - p_im * si[:, None]
        xi = hi + p_re * si[:, None] + p_im * sr[:, None]
        yb = jnp.einsum('ncgp,ghp->ncgh', xr, cr) - jnp.einsum('ncgp,ghp->ncgh', xi, ci)
        return (xr[:, -1], xi[:, -1]), yb

    (sr, si), y = lax.scan(block_step, (st_re.astype(f32), st_im.astype(f32)), u_blocks)
    y = y.transpose(1, 0, 2, 3, 4).reshape(n, L, D_S5) + d_skip.astype(f32) * uf
    return y, sr, si


def _hgrn2_mixer(q, f_logit, iv, g, lb, s0, norm_g):
    n, L, _ = q.shape
    f32 = jnp.float32
    qf = jax.nn.silu(q.astype(f32))
    fg = lb + (1.0 - lb) * jax.nn.sigmoid(f_logit.astype(f32))
    lf = jnp.log(fg)
    kf = 1.0 - fg
    vf = iv.astype(f32)
    c = min(L, CHUNK)
    nc = L // c

    def to_blocks(a):
        return a.reshape(n, nc, c, HG_HEADS, a.shape[-1] // HG_HEADS).transpose(1, 0, 3, 2, 4)

    mask = jnp.tril(jnp.ones((c, c), dtype=bool))[:, :, None]

    def block_step(s, blk):
        qb, kb, lfb, vb = blk
        bcum = jnp.cumsum(lfb, axis=2)
        b_last = bcum[:, :, -1:, :]
        o_inter = jnp.einsum('nhtk,nhkv->nhtv', qb * jnp.exp(bcum), s)
        diff = bcum[:, :, :, None, :] - bcum[:, :, None, :, :]
        decay = jnp.exp(jnp.where(mask, diff, -jnp.inf))
        scores = jnp.einsum('nhtk,nhsk,nhtsk->nhts', qb, kb, decay)
        o = o_inter + jnp.einsum('nhts,nhsv->nhtv', scores, vb)
        s_new = jnp.exp(b_last)[:, :, 0, :, None] * s + jnp.einsum('nhsk,nhsv->nhkv', kb * jnp.exp(b_last - bcum), vb)
        return s_new, o

    s_fin, o = lax.scan(block_step, s0.astype(f32), (to_blocks(qf), to_blocks(kf), to_blocks(lf), to_blocks(vf)))
    o = o.transpose(1, 0, 3, 2, 4).reshape(n, L, HG_HEADS, HG_DV)
    o = _rmsnorm(o, norm_g.reshape(HG_HEADS, HG_DV)).reshape(n, L, D_HG)
    o = o * jax.nn.silu(g.astype(f32))
    return o, s_fin


def _moe(x, router_w, router_b, w_gate, b_gate, w_up, b_up, w_down, b_down):
    t = x.shape[0]
    d = x.shape[-1]
    logits = x.astype(jnp.float32) @ router_w.astype(jnp.float32) + router_b.astype(jnp.float32)
    top_val, top_idx = lax.top_k(logits, TOP_K)
    gates = jax.nn.softmax(top_val, axis=-1)
    n_assign = t * TOP_K
    flat_e = top_idx.reshape(-1)
    order = jnp.argsort(flat_e)
    sorted_e = flat_e[order]
    sorted_tok = (order // TOP_K).astype(jnp.int32)
    sorted_gate = gates.reshape(-1)[order]
    counts = jnp.bincount(flat_e, length=N_EXPERTS)
    padded = (counts + MOE_BLOCK - 1) // MOE_BLOCK * MOE_BLOCK
    start = jnp.cumsum(counts) - counts
    pend = jnp.cumsum(padded)
    pstart = pend - padded
    dest = pstart[sorted_e] + jnp.arange(n_assign) - start[sorted_e]
    n_blocks = -(-n_assign // MOE_BLOCK) + N_EXPERTS
    n_rows = n_blocks * MOE_BLOCK
    row_tok = jnp.zeros((n_rows,), jnp.int32).at[dest].set(sorted_tok)
    row_gate = jnp.zeros((n_rows,), jnp.float32).at[dest].set(sorted_gate)
    block_e = jnp.minimum(jnp.searchsorted(pend, jnp.arange(n_blocks) * MOE_BLOCK, side='right'), N_EXPERTS - 1)
    xb = x[row_tok].reshape(n_blocks, MOE_BLOCK, d)

    def expert_rows(args):
        xr, e = args
        gt = xr @ w_gate[e] + b_gate[e]
        up = xr @ w_up[e] + b_up[e]
        gt = jnp.minimum(gt, SWIGLU_LIMIT)
        up = jnp.clip(up, -SWIGLU_LIMIT, SWIGLU_LIMIT)
        hid = (up + 1.0) * (gt * jax.nn.sigmoid(SWIGLU_ALPHA * gt))
        return hid @ w_down[e] + b_down[e]

    yb = lax.map(expert_rows, (xb, block_e)).reshape(n_rows, d)
    y = jax.ops.segment_sum(yb * row_gate[:, None].astype(yb.dtype), row_tok, num_segments=t)
    return y.astype(x.dtype)


def _trunk(x, p, s5_re, s5_im, hg, ln_in_g, ln_in_b, hg_lb, W):
    n, L, _ = x.shape
    f32 = jnp.float32
    lb_all = jnp.cumsum(jax.nn.softmax(hg_lb.astype(f32), axis=0), axis=0)
    h = _layernorm(x, ln_in_g, ln_in_b)
    out_re, out_im, out_hg = [], [], []
    for l in range(DEPTH):
        z = h @ W['w_in'][l]
        u, q, fl, iv, g = jnp.split(z, [D_S5, D_S5 + D_HG, D_S5 + 2 * D_HG, D_S5 + 3 * D_HG], axis=-1)
        y_s5, r_new, i_new = _s5_mixer(u, s5_re[l], s5_im[l], W['s5_lambda_re'][l], W['s5_lambda_im'][l],
                                       W['s5_log_step'][l], W['s5_b_re'][l], W['s5_b_im'][l],
                                       W['s5_c_re'][l], W['s5_c_im'][l], W['s5_d'][l])
        gl = jax.nn.gelu(y_s5, approximate=False)
        s5_out = gl * jax.nn.sigmoid(gl @ W['s5_w_glu'][l].astype(f32) + W['s5_b_glu'][l].astype(f32))
        s5_out = _rmsnorm(s5_out, W['s5_norm_g'][l])
        hg_out, hg_new = _hgrn2_mixer(q, fl, iv, g, lb_all[l], hg[l], W['hg_norm_g'][l])
        mix = jnp.concatenate([s5_out, hg_out], axis=-1).astype(h.dtype) @ W['w_out'][l]
        h = _layernorm(DEEPNORM_ALPHA * h + mix, W['ln1_g'][l], W['ln1_b'][l])
        ffn = _moe(h.reshape(n * L, D_MODEL), W['router_w'][l], W['router_b'][l], W['w_gate'][l], W['b_gate'][l],
                   W['w_up'][l], W['b_up'][l], W['w_down'][l], W['b_down'][l]).reshape(n, L, D_MODEL)
        r = DEEPNORM_ALPHA * h + ffn
        e = (p[l] @ W['ple_w'][l]) * jax.nn.sigmoid(r @ W['ple_gate_w'][l])
        h = _layernorm(r + e, W['ln2_g'][l], W['ln2_b'][l])
        out_re.append(r_new)
        out_im.append(i_new)
        out_hg.append(hg_new)
    return h, jnp.stack(out_re, axis=0), jnp.stack(out_im, axis=0), jnp.stack(out_hg, axis=0)


def setup_inputs(seed: int = 0) -> dict:
    key = jax.random.key(seed)
    ks = iter(jax.random.split(key, 48))
    f32 = jnp.float32
    d = D_MODEL

    def nrm(shape, scale):
        return scale * jax.random.normal(next(ks), shape, f32)

    x_prompt = nrm((BATCH, SEQ, d), 1.0)
    x_sample = nrm((DEC_BATCH, DEC_SEQ, d), 1.0)
    state_s5_re = nrm((DEPTH, DEC_BATCH, S5_GROUPS, S5_STATE), 0.5)
    state_s5_im = nrm((DEPTH, DEC_BATCH, S5_GROUPS, S5_STATE), 0.5)
    state_hgrn = nrm((DEPTH, DEC_BATCH, HG_HEADS, HG_DK, HG_DV), 0.5)
    p_prompt = nrm((DEPTH, BATCH, SEQ, PLE_DIM), 1.0)
    p_sample = nrm((DEPTH, DEC_BATCH, DEC_SEQ, PLE_DIM), 1.0)
    ln_in_g = 1.0 + nrm((d,), 0.01)
    ln_in_b = nrm((d,), 0.01)
    w_in = nrm((DEPTH, d, D_IN), d ** -0.5)
    s5_lambda_re = -0.5 + nrm((DEPTH, S5_GROUPS, S5_STATE), 0.01)
    s5_lambda_im = (jnp.broadcast_to(jnp.pi * jnp.arange(S5_STATE, dtype=f32), (DEPTH, S5_GROUPS, S5_STATE))
                    + nrm((DEPTH, S5_GROUPS, S5_STATE), 0.01))
    s5_log_step = jax.random.uniform(next(ks), (DEPTH, S5_GROUPS), f32, math.log(1e-3), math.log(1e-1))
    s5_b_re = nrm((DEPTH, S5_GROUPS, S5_STATE, S5_GROUP), (2 * S5_GROUP) ** -0.5)
    s5_b_im = nrm((DEPTH, S5_GROUPS, S5_STATE, S5_GROUP), (2 * S5_GROUP) ** -0.5)
    s5_c_re = nrm((DEPTH, S5_GROUPS, S5_GROUP, S5_STATE), (2 * S5_STATE) ** -0.5)
    s5_c_im = nrm((DEPTH, S5_GROUPS, S5_GROUP, S5_STATE), (2 * S5_STATE) ** -0.5)
    s5_d = nrm((DEPTH, D_S5), 0.5)
    s5_w_glu = nrm((DEPTH, D_S5, D_S5), D_S5 ** -0.5)
    s5_b_glu = nrm((DEPTH, D_S5), 0.01)
    s5_norm_g = 1.0 + nrm((DEPTH, D_S5), 0.01)
    hg_lb = nrm((DEPTH + 1, D_HG), 0.5)
    hg_norm_g = 1.0 + nrm((DEPTH, D_HG), 0.01)
    w_out = nrm((DEPTH, D_MIX, d), DEEPNORM_BETA * D_MIX ** -0.5)
    ln1_g = 1.0 + nrm((DEPTH, d), 0.01)
    ln1_b = nrm((DEPTH, d), 0.01)
    router_w = nrm((DEPTH, d, N_EXPERTS), d ** -0.5)
    router_b = nrm((DEPTH, N_EXPERTS), 0.01)
    w_gate = nrm((DEPTH, N_EXPERTS, d, D_FF), d ** -0.5)
    b_gate = nrm((DEPTH, N_EXPERTS, D_FF), 0.01)
    w_up = nrm((DEPTH, N_EXPERTS, d, D_FF), d ** -0.5)
    b_up = nrm((DEPTH, N_EXPERTS, D_FF), 0.01)
    w_down = nrm((DEPTH, N_EXPERTS, D_FF, d), DEEPNORM_BETA * D_FF ** -0.5)
    b_down = nrm((DEPTH, N_EXPERTS, d), 0.01)
    ple_w = nrm((DEPTH, PLE_DIM, d), PLE_DIM ** -0.5)
    ple_gate_w = nrm((DEPTH, d, d), d ** -0.5)
    ln2_g = 1.0 + nrm((DEPTH, d), 0.01)
    ln2_b = nrm((DEPTH, d), 0.01)
    return {'x_prompt': x_prompt, 'x_sample': x_sample,
            'state_s5_re': state_s5_re, 'state_s5_im': state_s5_im, 'state_hgrn': state_hgrn,
            'p_prompt': p_prompt, 'p_sample': p_sample,
            'ln_in_g': ln_in_g, 'ln_in_b': ln_in_b, 'w_in': w_in,
            's5_lambda_re': s5_lambda_re, 's5_lambda_im': s5_lambda_im, 's5_log_step': s5_log_step,
            's5_b_re': s5_b_re, 's5_b_im': s5_b_im, 's5_c_re': s5_c_re, 's5_c_im': s5_c_im,
            's5_d': s5_d, 's5_w_glu': s5_w_glu, 's5_b_glu': s5_b_glu, 's5_norm_g': s5_norm_g,
            'hg_lb': hg_lb, 'hg_norm_g': hg_norm_g, 'w_out': w_out, 'ln1_g': ln1_g, 'ln1_b': ln1_b,
            'router_w': router_w, 'router_b': router_b, 'w_gate': w_gate, 'b_gate': b_gate,
            'w_up': w_up, 'b_up': b_up, 'w_down': w_down, 'b_down': b_down,
            'ple_w': ple_w, 'ple_gate_w': ple_gate_w, 'ln2_g': ln2_g, 'ln2_b': ln2_b}


def reference(x_prompt, x_sample, state_s5_re, state_s5_im, state_hgrn, p_prompt, p_sample,
              ln_in_g, ln_in_b, w_in, s5_lambda_re, s5_lambda_im, s5_log_step,
              s5_b_re, s5_b_im, s5_c_re, s5_c_im, s5_d, s5_w_glu, s5_b_glu, s5_norm_g,
              hg_lb, hg_norm_g, w_out, ln1_g, ln1_b, router_w, router_b, w_gate, b_gate,
              w_up, b_up, w_down, b_down, ple_w, ple_gate_w, ln2_g, ln2_b):
    W = {'w_in': w_in, 's5_lambda_re': s5_lambda_re, 's5_lambda_im': s5_lambda_im,
         's5_log_step': s5_log_step, 's5_b_re': s5_b_re, 's5_b_im': s5_b_im,
         's5_c_re': s5_c_re, 's5_c_im': s5_c_im, 's5_d': s5_d, 's5_w_glu': s5_w_glu,
         's5_b_glu': s5_b_glu, 's5_norm_g': s5_norm_g, 'hg_norm_g': hg_norm_g, 'w_out': w_out,
         'ln1_g': ln1_g, 'ln1_b': ln1_b, 'router_w': router_w, 'router_b': router_b,
         'w_gate': w_gate, 'b_gate': b_gate, 'w_up': w_up, 'b_up': b_up,
         'w_down': w_down, 'b_down': b_down, 'ple_w': ple_w, 'ple_gate_w': ple_gate_w,
         'ln2_g': ln2_g, 'ln2_b': ln2_b}
    nb = x_prompt.shape[0]
    zero_re = jnp.zeros((DEPTH, nb, S5_GROUPS, S5_STATE), jnp.float32)
    zero_im = jnp.zeros((DEPTH, nb, S5_GROUPS, S5_STATE), jnp.float32)
    zero_hg = jnp.zeros((DEPTH, nb, HG_HEADS, HG_DK, HG_DV), jnp.float32)
    y_prompt, s5_re_prompt, s5_im_prompt, hgrn_prompt = _trunk(
        x_prompt, p_prompt, zero_re, zero_im, zero_hg, ln_in_g, ln_in_b, hg_lb, W)
    y_sample, s5_re_sample, s5_im_sample, hgrn_sample = _trunk(
        x_sample, p_sample, state_s5_re, state_s5_im, state_hgrn, ln_in_g, ln_in_b, hg_lb, W)
    return (y_prompt, y_sample, s5_re_prompt, s5_im_prompt, hgrn_prompt, s5_re_sample, s5_im_sample, hgrn_sample)
```

```python
import functools

import jax
import jax.numpy as jnp
from jax import lax
from jax.experimental import pallas as pl
from jax.experimental.pallas import tpu as pltpu

F32 = jnp.float32
BF16 = jnp.bfloat16
HIGHEST = lax.Precision.HIGHEST

D_MODEL = 1024
CHUNK = 64
PLE_DIM = 256
D_S5 = 512
S5_GROUP = 16
S5_GROUPS = 32
S5_STATE = 64
D_HG = 512
HG_HEADS = 4
HG_D = 128
D_IN = D_S5 + 4 * D_HG
N_EXPERTS = 32
TOP_K = 4
SWIGLU_LIMIT = 7.0
SWIGLU_ALPHA = 1.702
DEEPNORM_ALPHA = 2.0 ** 0.25
LN_EPS = 1e-5
RMS_EPS = 1e-6

S5_CONV = CHUNK * S5_GROUP
HG_SUB = 16
MOE_ROWS = 512
VMEM_LIMIT = 56 * 1024 * 1024


def _cparams(n_axes=1):
    return pltpu.CompilerParams(dimension_semantics=("arbitrary",) * n_axes,
                                vmem_limit_bytes=VMEM_LIMIT)


def _dot(a, b, precision=None):
    return jnp.dot(a, b, preferred_element_type=F32, precision=precision)


def _layernorm(x, g, b):
    mu = jnp.mean(x, axis=-1, keepdims=True)
    xc = x - mu
    var = jnp.mean(xc * xc, axis=-1, keepdims=True)
    return xc * lax.rsqrt(var + LN_EPS) * g + b


def _ln_in_proj_kernel(x_ref, g_ref, b_ref, w_ref, h_ref, u_ref, z_ref):
    h = _layernorm(x_ref[...], g_ref[...], b_ref[...])
    h_ref[...] = h
    hb = h.astype(BF16)
    u_ref[...] = _dot(hb, w_ref[:, :D_S5])
    z_ref[...] = _dot(hb, w_ref[:, D_S5:])


def ln_in_proj(x, g, b, w_bf16, tm):
    t = x.shape[0]
    row = lambda i: (i, 0)
    fixed = lambda i: (0, 0)
    return pl.pallas_call(
        _ln_in_proj_kernel,
        grid=(t // tm,),
        in_specs=[pl.BlockSpec((tm, D_MODEL), row),
                  pl.BlockSpec((1, D_MODEL), fixed),
                  pl.BlockSpec((1, D_MODEL), fixed),
                  pl.BlockSpec((D_MODEL, D_IN), fixed)],
        out_specs=[pl.BlockSpec((tm, D_MODEL), row),
                   pl.BlockSpec((tm, D_S5), row),
                   pl.BlockSpec((tm, 4 * D_HG), row)],
        out_shape=[jax.ShapeDtypeStruct((t, D_MODEL), F32),
                   jax.ShapeDtypeStruct((t, D_S5), F32),
                   jax.ShapeDtypeStruct((t, 4 * D_HG), F32)],
        compiler_params=_cparams(),
        name="ln_in_proj",
    )(x, g, b, w_bf16)


def _s5_prep_kernel(lrc_ref, lic_ref, lrr_ref, lir_ref, ls_ref, brt_ref, bit_ref,
                    brtt_ref, bitt_ref, crt_ref, cit_ref,
                    m_ref, w_ref, wc_ref, a_ref):
    step = jnp.exp(ls_ref[0])

    def discretise(lr_raw, li):
        lr = jnp.minimum(lr_raw, -1e-4)
        dr, di = lr * step, li * step
        mag = jnp.exp(dr)
        a_re, a_im = mag * jnp.cos(di), mag * jnp.sin(di)
        den = lr * lr + li * li
        nr = a_re - 1.0
        fr = (nr * lr + a_im * li) / den
        fi = (a_im * lr - nr * li) / den
        return dr, di, fr, fi

    dr_c, di_c, _, _ = discretise(lrc_ref[0], lic_ref[0])
    dr_r, di_r, fr_r, fi_r = discretise(lrr_ref[0], lir_ref[0])

    lane = lax.broadcasted_iota(jnp.int32, (1, S5_CONV), 1)
    lag = (lane // S5_GROUP).astype(F32)

    def c_times_power(tf):
        mag = jnp.exp(dr_c * tf)
        ang = di_c * tf
        pr, pi = mag * jnp.cos(ang), mag * jnp.sin(ang)
        ctr, cti = crt_ref[0], cit_ref[0]
        return ctr * pr - cti * pi, ctr * pi + cti * pr

    cpr, cpi = c_times_power(lag)
    bbr = fr_r * brt_ref[0] - fi_r * bit_ref[0]
    bbi = fr_r * bit_ref[0] + fi_r * brt_ref[0]
    kt = _dot(bbr, cpr, HIGHEST) - _dot(bbi, cpi, HIGHEST)
    for s in range(CHUNK):
        shifted = kt if s == 0 else pltpu.roll(kt, S5_GROUP * s, axis=1)
        m_ref[0, S5_GROUP * s:S5_GROUP * (s + 1), :] = jnp.where(
            lane >= S5_GROUP * s, shifted, 0.0).astype(BF16)

    rowi = lax.broadcasted_iota(jnp.int32, (S5_CONV, 1), 0)
    rem = (CHUNK - 1 - rowi // S5_GROUP).astype(F32)
    magw = jnp.exp(dr_r * rem)
    angw = di_r * rem
    pwr, pwi = magw * jnp.cos(angw), magw * jnp.sin(angw)
    bbtr = fr_r * brtt_ref[0] - fi_r * bitt_ref[0]
    bbti = fr_r * bitt_ref[0] + fi_r * brtt_ref[0]
    w_ref[0, :, :S5_STATE] = pwr * bbtr - pwi * bbti
    w_ref[0, :, S5_STATE:] = pwr * bbti + pwi * bbtr

    c1r, c1i = c_times_power(lag + 1.0)
    wc_ref[0, :S5_STATE, :] = c1r
    wc_ref[0, S5_STATE:, :] = -c1i

    full = float(CHUNK)
    mag_c = jnp.exp(dr_r * full)
    a_ref[0, 0:1, :] = mag_c * jnp.cos(di_r * full)
    a_ref[0, 1:2, :] = mag_c * jnp.sin(di_r * full)


def s5_prep(lam_re, lam_im, log_step, b_re, b_im, c_re, c_im):
    g, p = lam_re.shape
    brt = jnp.swapaxes(b_re, 1, 2)
    bit = jnp.swapaxes(b_im, 1, 2)
    crt = jnp.tile(jnp.swapaxes(c_re, 1, 2), (1, 1, CHUNK))
    cit = jnp.tile(jnp.swapaxes(c_im, 1, 2), (1, 1, CHUNK))
    args = (lam_re.reshape(g, p, 1), lam_im.reshape(g, p, 1),
            lam_re.reshape(g, 1, p), lam_im.reshape(g, 1, p), log_step.reshape(g, 1, 1),
            brt, bit, jnp.tile(brt, (1, CHUNK, 1)), jnp.tile(bit, (1, CHUNK, 1)), crt, cit)
    spec = lambda a: pl.BlockSpec((1,) + a.shape[1:], lambda i: (i, 0, 0))
    out_shape = [jax.ShapeDtypeStruct((g, S5_CONV, S5_CONV), BF16),
                 jax.ShapeDtypeStruct((g, S5_CONV, 2 * S5_STATE), F32),
                 jax.ShapeDtypeStruct((g, 2 * S5_STATE, S5_CONV), F32),
                 jax.ShapeDtypeStruct((g, 2, S5_STATE), F32)]
    return pl.pallas_call(
        _s5_prep_kernel,
        grid=(g,),
        in_specs=[spec(a) for a in args],
        out_specs=[spec(o) for o in out_shape],
        out_shape=out_shape,
        compiler_params=_cparams(),
        name="s5_prep",
    )(*args)


def _s5_main_kernel(u_ref, m_ref, w_ref, wc_ref, a_ref, xsr_ref, xsi_ref,
                    y_ref, fpr_ref, fpi_ref, fsr_ref, fsi_ref,
                    hr_scr, hi_scr, x0r_scr, x0i_scr, *, n_prompt, n_chunks):
    u = u_ref[0]
    y_local = _dot(u.astype(BF16), m_ref[0])
    hend = _dot(u, w_ref[0], HIGHEST)
    hr_scr[...] = hend[:, :S5_STATE]
    hi_scr[...] = hend[:, S5_STATE:]
    ar = a_ref[0, 0:1, :]
    ai = a_ref[0, 1:2, :]
    xr = jnp.zeros((n_prompt, S5_STATE), F32)
    xi = jnp.zeros((n_prompt, S5_STATE), F32)
    for c in range(n_chunks):
        rows = slice(c * n_prompt, (c + 1) * n_prompt)
        x0r_scr[rows, :] = xr
        x0i_scr[rows, :] = xi
        xr, xi = (ar * xr - ai * xi + hr_scr[rows, :],
                  ar * xi + ai * xr + hi_scr[rows, :])
    fpr_ref[0] = xr
    fpi_ref[0] = xi
    base = n_chunks * n_prompt
    sr, si = xsr_ref[0], xsi_ref[0]
    x0r_scr[base:, :] = sr
    x0i_scr[base:, :] = si
    fsr_ref[0] = ar * sr - ai * si + hr_scr[base:, :]
    fsi_ref[0] = ar * si + ai * sr + hi_scr[base:, :]
    y_ref[0] = (y_local + _dot(x0r_scr[...], wc_ref[0, :S5_STATE, :], HIGHEST)
                + _dot(x0i_scr[...], wc_ref[0, S5_STATE:, :], HIGHEST))


def s5_main(u_rows, m, w, wc, a, xs_re, xs_im, n_prompt, n_chunks):
    g, r, _ = u_rows.shape
    n_sample = xs_re.shape[1]
    spec = lambda shape: pl.BlockSpec((1,) + tuple(shape[1:]), lambda i: (i, 0, 0))
    args = (u_rows, m, w, wc, a, xs_re, xs_im)
    out_shape = [jax.ShapeDtypeStruct((g, r, S5_CONV), F32),
                 jax.ShapeDtypeStruct((g, n_prompt, S5_STATE), F32),
                 jax.ShapeDtypeStruct((g, n_prompt, S5_STATE), F32),
                 jax.ShapeDtypeStruct((g, n_sample, S5_STATE), F32),
                 jax.ShapeDtypeStruct((g, n_sample, S5_STATE), F32)]
    return pl.pallas_call(
        functools.partial(_s5_main_kernel, n_prompt=n_prompt, n_chunks=n_chunks),
        grid=(g,),
        in_specs=[spec(x.shape) for x in args],
        out_specs=[spec(o.shape) for o in out_shape],
        out_shape=out_shape,
        scratch_shapes=[pltpu.VMEM((r, S5_STATE), F32)] * 4,
        compiler_params=_cparams(),
        name="s5_main",
    )(*args)


def _hgrn_kernel(z_ref, s0_ref, lb_ref, ng_ref, o_ref, sfin_ref, st_scr, *, n_seq):
    c = pl.program_id(1)

    @pl.when(c == 0)
    def _():
        st_scr[...] = s0_ref[...]

    lbw = lb_ref[...]
    lbe = jnp.exp(lbw - jnp.max(lbw, axis=0, keepdims=True))
    lb_all = lbe[0:1, :] / jnp.sum(lbe, axis=0, keepdims=True)

    rowi = lax.broadcasted_iota(jnp.int32, (CHUNK, CHUNK), 0)
    coli = lax.broadcasted_iota(jnp.int32, (CHUNK, CHUNK), 1)
    tri = (coli <= rowi).astype(F32)
    rowk = lax.broadcasted_iota(jnp.int32, (CHUNK, HG_D), 0)
    row_in_sub = rowk % HG_SUB
    n_sub = CHUNK // HG_SUB

    for hd in range(HG_HEADS):
        cols = slice(hd * HG_D, (hd + 1) * HG_D)
        lb = lb_all[:, cols]
        ng = ng_ref[:, cols]

        def body(n, carry, hd=hd, lb=lb, ng=ng):
            zq = z_ref[n, :, hd * HG_D:(hd + 1) * HG_D]
            zf = z_ref[n, :, D_HG + hd * HG_D:D_HG + (hd + 1) * HG_D]
            v = z_ref[n, :, 2 * D_HG + hd * HG_D:2 * D_HG + (hd + 1) * HG_D]
            zg = z_ref[n, :, 3 * D_HG + hd * HG_D:3 * D_HG + (hd + 1) * HG_D]
            q = zq * jax.nn.sigmoid(zq)
            fg = lb + (1.0 - lb) * jax.nn.sigmoid(zf)
            kk = 1.0 - fg
            bcum = _dot(tri, jnp.log(fg), HIGHEST)
            b_last = bcum[CHUNK - 1:CHUNK, :]
            vb = v.astype(BF16)
            st = st_scr[n, hd]

            blocks = [jnp.zeros((HG_SUB, CHUNK), F32)]
            for i in range(1, n_sub):
                lo = i * HG_SUB
                bref = bcum[lo - 1:lo, :]
                qsc = q[lo:lo + HG_SUB, :] * jnp.exp(bcum[lo:lo + HG_SUB, :] - bref)
                ksc = kk * jnp.exp(jnp.where(rowk < lo, bref - bcum, -jnp.inf))
                blocks.append(lax.dot_general(qsc.astype(BF16), ksc.astype(BF16),
                                              (((1,), (1,)), ((), ())), preferred_element_type=F32))
            scores = jnp.concatenate(blocks, axis=0)

            for d in range(HG_SUB):
                if d == 0:
                    col = jnp.sum(q * kk, axis=-1, keepdims=True)
                else:
                    ks = pltpu.roll(kk, d, axis=0)
                    bs = pltpu.roll(bcum, d, axis=0)
                    dec = jnp.exp(jnp.where(row_in_sub >= d, bcum - bs, -jnp.inf))
                    col = jnp.sum(q * ks * dec, axis=-1, keepdims=True)
                scores = scores + jnp.where(coli == rowi - d, col, 0.0)

            qd = (q * jnp.exp(bcum)).astype(BF16)
            o = lax.dot_general(qd, st.astype(BF16), (((1,), (1,)), ((), ())),
                                preferred_element_type=F32)
            o = o + _dot(scores.astype(BF16), vb)
            kdec = (kk * jnp.exp(b_last - bcum)).astype(BF16)
            st_scr[n, hd] = jnp.exp(b_last) * st + lax.dot_general(
                vb, kdec, (((0,), (0,)), ((), ())), preferred_element_type=F32)

            on = o * lax.rsqrt(jnp.mean(o * o, axis=-1, keepdims=True) + RMS_EPS) * ng
            o_ref[n, :, hd * HG_D:(hd + 1) * HG_D] = on * (zg * jax.nn.sigmoid(zg))
            return carry

        lax.fori_loop(0, n_seq, body, 0)

    @pl.when(c == pl.num_programs(1) - 1)
    def _():
        sfin_ref[...] = st_scr[...]


def hgrn(z, s0_t, hg_lb, norm_g, n_seq):
    n, length, _ = z.shape
    return pl.pallas_call(
        functools.partial(_hgrn_kernel, n_seq=n_seq),
        grid=(n // n_seq, length // CHUNK),
        in_specs=[pl.BlockSpec((n_seq, CHUNK, 4 * D_HG), lambda g, c: (g, c, 0)),
                  pl.BlockSpec((n_seq, HG_HEADS, HG_D, HG_D), lambda g, c: (g, 0, 0, 0)),
                  pl.BlockSpec(hg_lb.shape, lambda g, c: (0, 0)),
                  pl.BlockSpec((1, D_HG), lambda g, c: (0, 0))],
        out_specs=[pl.BlockSpec((n_seq, CHUNK, D_HG), lambda g, c: (g, c, 0)),
                   pl.BlockSpec((n_seq, HG_HEADS, HG_D, HG_D), lambda g, c: (g, 0, 0, 0))],
        out_shape=[jax.ShapeDtypeStruct((n, length, D_HG), F32),
                   jax.ShapeDtypeStruct((n, HG_HEADS, HG_D, HG_D), F32)],
        scratch_shapes=[pltpu.VMEM((n_seq, HG_HEADS, HG_D, HG_D), F32)],
        compiler_params=_cparams(2),
        name="hgrn",
    )(z, s0_t, hg_lb, norm_g)


def _post_mix_kernel(h_ref, u_ref, y_ref, hg_ref, d_ref, wglu_ref, bglu_ref, s5g_ref,
                     wout_ref, g1_ref, b1_ref, rw_ref, rb_ref,
                     h1_ref, idx_ref, gate_ref, rank_ref, cnt_ref, run_scr, *, tm):
    i = pl.program_id(0)

    @pl.when(i == 0)
    def _():
        run_scr[...] = jnp.zeros_like(run_scr)

    ys = y_ref[...] + d_ref[...] * u_ref[...]
    gl = 0.5 * ys * (1.0 + lax.erf(ys * (2.0 ** -0.5)))
    s5o = gl * jax.nn.sigmoid(_dot(gl.astype(BF16), wglu_ref[...]) + bglu_ref[...])
    s5o = s5o * lax.rsqrt(jnp.mean(s5o * s5o, axis=-1, keepdims=True) + RMS_EPS) * s5g_ref[...]
    mix = (_dot(s5o.astype(BF16), wout_ref[:D_S5, :])
           + _dot(hg_ref[...].astype(BF16), wout_ref[D_S5:, :]))
    h1 = _layernorm(DEEPNORM_ALPHA * h_ref[...] + mix, g1_ref[...], b1_ref[...])
    h1_ref[...] = h1

    logits = _dot(h1, rw_ref[...], HIGHEST) + rb_ref[...]
    eid = lax.broadcasted_iota(jnp.int32, (tm, N_EXPERTS), 1)
    vals, idxs = [], []
    for _ in range(TOP_K):
        m = jnp.max(logits, axis=-1, keepdims=True)
        ix = jnp.min(jnp.where(logits == m, eid, N_EXPERTS), axis=-1, keepdims=True)
        vals.append(m)
        idxs.append(ix)
        logits = jnp.where(eid == ix, -jnp.inf, logits)
    exps = [jnp.exp(v - vals[0]) for v in vals]
    den = exps[0] + exps[1] + exps[2] + exps[3]

    onehot = jnp.zeros((tm, N_EXPERTS), F32)
    for ix in idxs:
        onehot = onehot + (eid == ix).astype(F32)
    rowi = lax.broadcasted_iota(jnp.int32, (tm, tm), 0)
    coli = lax.broadcasted_iota(jnp.int32, (tm, tm), 1)
    earlier = (coli < rowi).astype(BF16)
    prefix = _dot(earlier, onehot.astype(BF16)) + run_scr[...]
    for k in range(TOP_K):
        idx_ref[:, k:k + 1] = idxs[k]
        gate_ref[:, k:k + 1] = exps[k] / den
        rank_ref[:, k:k + 1] = jnp.sum(jnp.where(eid == idxs[k], prefix, 0.0),
                                       axis=-1, keepdims=True).astype(jnp.int32)
    run_scr[...] = run_scr[...] + jnp.sum(onehot, axis=0, keepdims=True)
    cnt_ref[...] = run_scr[...]


def post_mix(h0, u, y, hg, d_skip, wglu, bglu, s5g, wout, g1, b1, rw, rb, tm):
    t = h0.shape[0]
    row = lambda i: (i, 0)
    fixed = lambda i: (0, 0)
    full = lambda a: pl.BlockSpec(a.shape, fixed)
    weights = (d_skip, wglu, bglu, s5g, wout, g1, b1, rw, rb)
    return pl.pallas_call(
        functools.partial(_post_mix_kernel, tm=tm),
        grid=(t // tm,),
        in_specs=[pl.BlockSpec((tm, D_MODEL), row), pl.BlockSpec((tm, D_S5), row),
                  pl.BlockSpec((tm, D_S5), row), pl.BlockSpec((tm, D_HG), row)]
                 + [full(a) for a in weights],
        out_specs=[pl.BlockSpec((tm, D_MODEL), row), pl.BlockSpec((tm, TOP_K), row),
                   pl.BlockSpec((tm, TOP_K), row), pl.BlockSpec((tm, TOP_K), row),
                   pl.BlockSpec((1, N_EXPERTS), fixed)],
        out_shape=[jax.ShapeDtypeStruct((t, D_MODEL), F32),
                   jax.ShapeDtypeStruct((t, TOP_K), jnp.int32),
                   jax.ShapeDtypeStruct((t, TOP_K), F32),
                   jax.ShapeDtypeStruct((t, TOP_K), jnp.int32),
                   jax.ShapeDtypeStruct((1, N_EXPERTS), F32)],
        scratch_shapes=[pltpu.VMEM((1, N_EXPERTS), F32)],
        compiler_params=_cparams(),
        name="post_mix",
    )(h0, u, y, hg, *weights)


def _dispatch_kernel(dest_ref, h_ref, xs_in_ref, xs_ref, sem, *, tm):
    del xs_in_ref

    def body(t, carry):
        for k in range(TOP_K):
            pltpu.make_async_copy(h_ref.at[pl.ds(t, 1)],
                                  xs_ref.at[pl.ds(dest_ref[TOP_K * t + k], 1)], sem).start()
        return carry

    lax.fori_loop(0, tm, body, 0)
    for _ in range(TOP_K):
        pltpu.make_async_copy(h_ref, xs_ref.at[pl.ds(0, tm)], sem).wait()


def dispatch(dest_flat, h1, xs_zero, tm):
    t = h1.shape[0]
    return pl.pallas_call(
        functools.partial(_dispatch_kernel, tm=tm),
        grid=(t // tm,),
        in_specs=[pl.BlockSpec((TOP_K * tm,), lambda i: (i,), memory_space=pltpu.SMEM),
                  pl.BlockSpec((tm, D_MODEL), lambda i: (i, 0)),
                  pl.BlockSpec(memory_space=pl.ANY)],
        out_specs=pl.BlockSpec(memory_space=pl.ANY),
        out_shape=jax.ShapeDtypeStruct(xs_zero.shape, xs_zero.dtype),
        scratch_shapes=[pltpu.SemaphoreType.DMA(())],
        input_output_aliases={2: 0},
        compiler_params=_cparams(),
        name="moe_dispatch",
    )(dest_flat, h1, xs_zero)


def _moe_ffn_kernel(be_ref, nu_ref, x_ref, wg_ref, bg_ref, wu_ref, bu_ref, wd_ref, bd_ref, y_ref):
    del be_ref

    @pl.when(pl.program_id(0) < nu_ref[0])
    def _():
        x = x_ref[...].astype(BF16)
        gt = jnp.minimum(_dot(x, wg_ref[0]) + bg_ref[0], SWIGLU_LIMIT)
        up = jnp.clip(_dot(x, wu_ref[0]) + bu_ref[0], -SWIGLU_LIMIT, SWIGLU_LIMIT)
        hid = (up + 1.0) * (gt * jax.nn.sigmoid(SWIGLU_ALPHA * gt))
        y_ref[...] = _dot(hid.astype(BF16), wd_ref[0]) + bd_ref[0]

    @pl.when(pl.program_id(0) >= nu_ref[0])
    def _():
        y_ref[...] = jnp.zeros_like(y_ref)


def moe_ffn(block_e, n_used, xs, wg, bg, wu, bu, wd, bd):
    n_rows = xs.shape[0]
    n_blocks = n_rows // MOE_ROWS
    rows = lambda i, be, nu: (jnp.minimum(i, nu[0] - 1), 0)
    wsel = lambda i, be, nu: (be[i], 0, 0)
    d_ff = wg.shape[-1]
    grid_spec = pltpu.PrefetchScalarGridSpec(
        num_scalar_prefetch=2,
        grid=(n_blocks,),
        in_specs=[pl.BlockSpec((MOE_ROWS, D_MODEL), rows),
                  pl.BlockSpec((1, D_MODEL, d_ff), wsel), pl.BlockSpec((1, 1, d_ff), wsel),
                  pl.BlockSpec((1, D_MODEL, d_ff), wsel), pl.BlockSpec((1, 1, d_ff), wsel),
                  pl.BlockSpec((1, d_ff, D_MODEL), wsel), pl.BlockSpec((1, 1, D_MODEL), wsel)],
        out_specs=pl.BlockSpec((MOE_ROWS, D_MODEL), lambda i, be, nu: (i, 0)),
    )
    return pl.pallas_call(
        _moe_ffn_kernel,
        grid_spec=grid_spec,
        out_shape=jax.ShapeDtypeStruct((n_rows, D_MODEL), F32),
        compiler_params=_cparams(),
        name="moe_ffn",
    )(block_e, n_used, xs, wg, bg, wu, bu, wd, bd)


def _combine_kernel(dest_ref, gate_ref, h_ref, p_ref, yb_ref, plew_ref, plegw_ref, g2_ref, b2_ref,
                    out_ref, buf, sem, *, tm):
    def body(t, carry):
        for k in range(TOP_K):
            pltpu.make_async_copy(yb_ref.at[pl.ds(dest_ref[TOP_K * t + k], 1)],
                                  buf.at[k, pl.ds(t, 1)], sem).start()
        return carry

    lax.fori_loop(0, tm, body, 0)
    for k in range(TOP_K):
        pltpu.make_async_copy(yb_ref.at[pl.ds(0, tm)], buf.at[k], sem).wait()

    gates = gate_ref[...]
    ffn = gates[:, 0:1] * buf[0]
    for k in range(1, TOP_K):
        ffn = ffn + gates[:, k:k + 1] * buf[k]
    r = DEEPNORM_ALPHA * h_ref[...] + ffn
    e = _dot(p_ref[...].astype(BF16), plew_ref[...]) * jax.nn.sigmoid(
        _dot(r.astype(BF16), plegw_ref[...]))
    out_ref[...] = _layernorm(r + e, g2_ref[...], b2_ref[...])


def combine(dest_flat, gates, h1, p, yb, plew, plegw, g2, b2, tm):
    t = h1.shape[0]
    row = lambda i: (i, 0)
    fixed = lambda i: (0, 0)
    return pl.pallas_call(
        functools.partial(_combine_kernel, tm=tm),
        grid=(t // tm,),
        in_specs=[pl.BlockSpec((TOP_K * tm,), lambda i: (i,), memory_space=pltpu.SMEM),
                  pl.BlockSpec((tm, TOP_K), row), pl.BlockSpec((tm, D_MODEL), row),
                  pl.BlockSpec((tm, PLE_DIM), row), pl.BlockSpec(memory_space=pl.ANY),
                  pl.BlockSpec(plew.shape, fixed), pl.BlockSpec(plegw.shape, fixed),
                  pl.BlockSpec((1, D_MODEL), fixed), pl.BlockSpec((1, D_MODEL), fixed)],
        out_specs=pl.BlockSpec((tm, D_MODEL), row),
        out_shape=jax.ShapeDtypeStruct((t, D_MODEL), F32),
        scratch_shapes=[pltpu.VMEM((TOP_K, tm, D_MODEL), F32), pltpu.SemaphoreType.DMA(())],
        compiler_params=_cparams(),
        name="moe_combine",
    )(dest_flat, gates, h1, p, yb, plew, plegw, g2, b2)


def _s5_rows(u_p, u_s, nb, nc, ns):
    up = u_p.reshape(nb, nc, CHUNK, S5_GROUPS, S5_GROUP).transpose(3, 1, 0, 2, 4)
    us = u_s.reshape(ns, CHUNK, S5_GROUPS, S5_GROUP).transpose(2, 0, 1, 3)
    return jnp.concatenate([up.reshape(S5_GROUPS, nc * nb, S5_CONV),
                            us.reshape(S5_GROUPS, ns, S5_CONV)], axis=1)


def _s5_tokens(y_rows, nb, nc, ns):
    yp = y_rows[:, :nc * nb].reshape(S5_GROUPS, nc, nb, CHUNK, S5_GROUP).transpose(2, 1, 3, 0, 4)
    ys = y_rows[:, nc * nb:].reshape(S5_GROUPS, ns, CHUNK, S5_GROUP).transpose(1, 2, 0, 3)
    return jnp.concatenate([yp.reshape(nb * nc * CHUNK, D_S5), ys.reshape(ns * CHUNK, D_S5)], axis=0)


def _row(v):
    return v.reshape(1, -1)


def kernel(x_prompt, x_sample, state_s5_re, state_s5_im, state_hgrn, p_prompt, p_sample, ln_in_g, ln_in_b, w_in, s5_lambda_re, s5_lambda_im, s5_log_step, s5_b_re, s5_b_im, s5_c_re, s5_c_im, s5_d, s5_w_glu, s5_b_glu, s5_norm_g, hg_lb, hg_norm_g, w_out, ln1_g, ln1_b, router_w, router_b, w_gate, b_gate, w_up, b_up, w_down, b_down, ple_w, ple_gate_w, ln2_g, ln2_b):
    nb, seq, _ = x_prompt.shape
    ns, dseq, _ = x_sample.shape
    assert dseq == CHUNK and seq % CHUNK == 0 and w_in.shape[0] == 1
    nc = seq // CHUNK
    tp, ts = nb * seq, ns * dseq
    t = tp + ts
    tm = 512 if t % 512 == 0 else 256
    tg = 256
    assert t % tm == 0 and t % tg == 0 and ns % nb == 0

    x = jnp.concatenate([x_prompt.reshape(tp, D_MODEL), x_sample.reshape(ts, D_MODEL)], axis=0)
    p = jnp.concatenate([p_prompt[0].reshape(tp, PLE_DIM), p_sample[0].reshape(ts, PLE_DIM)], axis=0)

    h0, u, z = ln_in_proj(x, _row(ln_in_g), _row(ln_in_b), w_in[0].astype(BF16), tm)

    m, w, wc, a = s5_prep(s5_lambda_re[0], s5_lambda_im[0], s5_log_step[0],
                          s5_b_re[0], s5_b_im[0], s5_c_re[0], s5_c_im[0])
    u_rows = _s5_rows(u[:tp], u[tp:], nb, nc, ns)
    y_rows, fpr, fpi, fsr, fsi = s5_main(u_rows, m, w, wc, a,
                                         jnp.swapaxes(state_s5_re[0], 0, 1),
                                         jnp.swapaxes(state_s5_im[0], 0, 1), nb, nc)
    y = _s5_tokens(y_rows, nb, nc, ns)

    zero_state = jnp.zeros((nb, HG_HEADS, HG_D, HG_D), F32)
    ng = _row(hg_norm_g[0])
    o_p, st_p = hgrn(z[:tp].reshape(nb, seq, 4 * D_HG), zero_state, hg_lb, ng, nb)
    o_s, st_s = hgrn(z[tp:].reshape(ns, dseq, 4 * D_HG), jnp.swapaxes(state_hgrn[0], 2, 3),
                     hg_lb, ng, nb)
    hg_out = jnp.concatenate([o_p.reshape(tp, D_HG), o_s.reshape(ts, D_HG)], axis=0)

    h1, idx, gates, rank, counts = post_mix(
        h0, u, y, hg_out, _row(s5_d[0]), s5_w_glu[0].astype(BF16), _row(s5_b_glu[0]),
        _row(s5_norm_g[0]), w_out[0].astype(BF16), _row(ln1_g[0]), _row(ln1_b[0]),
        router_w[0], _row(router_b[0]), tm)

    counts = counts[0].astype(jnp.int32)
    padded = (counts + MOE_ROWS - 1) // MOE_ROWS * MOE_ROWS
    pend = jnp.cumsum(padded)
    pstart = pend - padded
    dest_flat = (pstart[idx] + rank).reshape(-1)
    n_blocks = -(-t * TOP_K // MOE_ROWS) + N_EXPERTS
    n_used = (pend[-1] // MOE_ROWS).astype(jnp.int32)
    blk = jnp.arange(n_blocks, dtype=jnp.int32)
    block_e = jnp.minimum(jnp.searchsorted(pend, blk * MOE_ROWS, side='right'), N_EXPERTS - 1)
    block_e = jnp.where(blk < n_used, block_e, block_e[n_used - 1]).astype(jnp.int32)

    xs = dispatch(dest_flat, h1, jnp.zeros((n_blocks * MOE_ROWS, D_MODEL), F32), tg)
    yb = moe_ffn(block_e, n_used.reshape(1), xs,
                 w_gate[0].astype(BF16), b_gate[0][:, None, :], w_up[0].astype(BF16), b_up[0][:, None, :],
                 w_down[0].astype(BF16), b_down[0][:, None, :])
    out = combine(dest_flat, gates, h1, p, yb, ple_w[0].astype(BF16), ple_gate_w[0].astype(BF16),
                  _row(ln2_g[0]), _row(ln2_b[0]), tg)

    def s5_state(f, n):
        return jnp.swapaxes(f, 0, 1).reshape(1, n, S5_GROUPS, S5_STATE)

    return (out[:tp].reshape(nb, seq, D_MODEL), out[tp:].reshape(ns, dseq, D_MODEL),
            s5_state(fpr, nb), s5_state(fpi, nb), jnp.swapaxes(st_p, 2, 3)[None],
            s5_state(fsr, ns), s5_state(fsi, ns), jnp.swapaxes(st_s, 2, 3)[None])
```

```python
import functools

import jax
import jax.numpy as jnp
from jax import lax
from jax.experimental import pallas as pl
from jax.experimental.pallas import tpu as pltpu

F32 = jnp.float32
BF16 = jnp.bfloat16
HIGHEST = lax.Precision.HIGHEST

D_MODEL = 1024
CHUNK = 64
PLE_DIM = 256
D_S5 = 512
S5_GROUP = 16
S5_GROUPS = 32
S5_STATE = 64
D_HG = 512
HG_HEADS = 4
HG_D = 128
D_IN = D_S5 + 4 * D_HG
N_EXPERTS = 32
TOP_K = 4
SWIGLU_LIMIT = 7.0
SWIGLU_ALPHA = 1.702
DEEPNORM_ALPHA = 2.0 ** 0.25
LN_EPS = 1e-5
RMS_EPS = 1e-6

LANES = 128
SUBLANES = 8
ROW_TILES = D_MODEL // LANES
S5_CONV = CHUNK * S5_GROUP
HG_SUB = 16
MOE_ROWS = 512
VMEM_LIMIT = 56 * 1024 * 1024

assert ROW_TILES == SUBLANES


def _cparams(n_axes=1):
    return pltpu.CompilerParams(dimension_semantics=("arbitrary",) * n_axes,
                                vmem_limit_bytes=VMEM_LIMIT)


def _dot(a, b, precision=None):
    return jnp.dot(a, b, preferred_element_type=F32, precision=precision)


def _layernorm(x, g, b):
    mu = jnp.mean(x, axis=-1, keepdims=True)
    xc = x - mu
    var = jnp.mean(xc * xc, axis=-1, keepdims=True)
    return xc * lax.rsqrt(var + LN_EPS) * g + b


def _two_phase_specs(block, n_first):
    nd = len(block)
    first = pl.BlockSpec(block, lambda i: (jnp.minimum(i, n_first - 1),) + (0,) * (nd - 1))
    second = pl.BlockSpec(block, lambda i: (jnp.maximum(i - n_first, 0),) + (0,) * (nd - 1))
    return [first, second]


def _chunk(rows, j):
    return pl.ds(j, rows, stride=ROW_TILES)


def _read_row_tiles(ref, rows):
    return jnp.concatenate([ref[_chunk(rows, j), :] for j in range(ROW_TILES)], axis=1)


def _write_row_tiles(ref, val, rows):
    for j in range(ROW_TILES):
        ref[_chunk(rows, j), :] = val[:, j * LANES:(j + 1) * LANES]


def _row_tile(r):
    return pl.ds(pl.multiple_of(r * ROW_TILES, ROW_TILES), ROW_TILES)


def _ln_in_proj_kernel(xp_ref, xs_ref, g_ref, b_ref, w_ref, h_ref, up_ref, us_ref, zp_ref, zs_ref,
                       *, n_first):
    def phase(x_ref, u_ref, z_ref):
        h = _layernorm(x_ref[...], g_ref[...], b_ref[...])
        h_ref[...] = h
        hb = h.astype(BF16)
        u_ref[...] = _dot(hb, w_ref[:, :D_S5]).astype(BF16)
        z_ref[...] = _dot(hb, w_ref[:, D_S5:])

    i = pl.program_id(0)
    pl.when(i < n_first)(lambda: phase(xp_ref, up_ref, zp_ref))
    pl.when(i >= n_first)(lambda: phase(xs_ref, us_ref, zs_ref))


def ln_in_proj(xp, xs, g, b, w_bf16, tm):
    tp, ts = xp.shape[0], xs.shape[0]
    n_first = tp // tm
    fixed = lambda i: (0, 0)
    return pl.pallas_call(
        functools.partial(_ln_in_proj_kernel, n_first=n_first),
        grid=((tp + ts) // tm,),
        in_specs=_two_phase_specs((tm, D_MODEL), n_first)
                 + [pl.BlockSpec((1, D_MODEL), fixed), pl.BlockSpec((1, D_MODEL), fixed),
                    pl.BlockSpec((D_MODEL, D_IN), fixed)],
        out_specs=[pl.BlockSpec((tm, D_MODEL), lambda i: (i, 0))]
                  + _two_phase_specs((tm, D_S5), n_first)
                  + _two_phase_specs((tm, 4 * D_HG), n_first),
        out_shape=[jax.ShapeDtypeStruct((tp + ts, D_MODEL), F32),
                   jax.ShapeDtypeStruct((tp, D_S5), BF16), jax.ShapeDtypeStruct((ts, D_S5), BF16),
                   jax.ShapeDtypeStruct((tp, 4 * D_HG), F32), jax.ShapeDtypeStruct((ts, 4 * D_HG), F32)],
        compiler_params=_cparams(),
        name="ln_in_proj",
    )(xp, xs, g, b, w_bf16)


def _s5_prep_kernel(lrc_ref, lic_ref, lrr_ref, lir_ref, ls_ref, brt_ref, bit_ref,
                    brtt_ref, bitt_ref, crt_ref, cit_ref,
                    m_ref, w_ref, wc_ref, a_ref):
    step = jnp.exp(ls_ref[0])

    def discretise(lr_raw, li):
        lr = jnp.minimum(lr_raw, -1e-4)
        dr, di = lr * step, li * step
        mag = jnp.exp(dr)
        a_re, a_im = mag * jnp.cos(di), mag * jnp.sin(di)
        den = lr * lr + li * li
        nr = a_re - 1.0
        fr = (nr * lr + a_im * li) / den
        fi = (a_im * lr - nr * li) / den
        return dr, di, fr, fi

    dr_c, di_c, _, _ = discretise(lrc_ref[0], lic_ref[0])
    dr_r, di_r, fr_r, fi_r = discretise(lrr_ref[0], lir_ref[0])

    lane = lax.broadcasted_iota(jnp.int32, (1, S5_CONV), 1)
    lag = (lane // S5_GROUP).astype(F32)

    def c_times_power(tf):
        mag = jnp.exp(dr_c * tf)
        ang = di_c * tf
        pr, pi = mag * jnp.cos(ang), mag * jnp.sin(ang)
        ctr, cti = crt_ref[0], cit_ref[0]
        return ctr * pr - cti * pi, ctr * pi + cti * pr

    cpr, cpi = c_times_power(lag)
    bbr = fr_r * brt_ref[0] - fi_r * bit_ref[0]
    bbi = fr_r * bit_ref[0] + fi_r * brt_ref[0]
    kt = _dot(bbr, cpr, HIGHEST) - _dot(bbi, cpi, HIGHEST)
    for s in range(CHUNK):
        shifted = kt if s == 0 else pltpu.roll(kt, S5_GROUP * s, axis=1)
        m_ref[0, S5_GROUP * s:S5_GROUP * (s + 1), :] = jnp.where(
            lane >= S5_GROUP * s, shifted, 0.0).astype(BF16)

    rowi = lax.broadcasted_iota(jnp.int32, (S5_CONV, 1), 0)
    rem = (CHUNK - 1 - rowi // S5_GROUP).astype(F32)
    magw = jnp.exp(dr_r * rem)
    angw = di_r * rem
    pwr, pwi = magw * jnp.cos(angw), magw * jnp.sin(angw)
    bbtr = fr_r * brtt_ref[0] - fi_r * bitt_ref[0]
    bbti = fr_r * bitt_ref[0] + fi_r * brtt_ref[0]
    w_ref[0, :, :S5_STATE] = pwr * bbtr - pwi * bbti
    w_ref[0, :, S5_STATE:] = pwr * bbti + pwi * bbtr

    c1r, c1i = c_times_power(lag + 1.0)
    wc_ref[0, :S5_STATE, :] = c1r.astype(BF16)
    wc_ref[0, S5_STATE:, :] = (-c1i).astype(BF16)

    full = float(CHUNK)
    mag_c = jnp.exp(dr_r * full)
    a_ref[0, 0:1, :] = mag_c * jnp.cos(di_r * full)
    a_ref[0, 1:2, :] = mag_c * jnp.sin(di_r * full)


def s5_prep(lam_re, lam_im, log_step, b_re, b_im, c_re, c_im):
    g, p = lam_re.shape
    brt = jnp.swapaxes(b_re, 1, 2)
    bit = jnp.swapaxes(b_im, 1, 2)
    crt = jnp.tile(jnp.swapaxes(c_re, 1, 2), (1, 1, CHUNK))
    cit = jnp.tile(jnp.swapaxes(c_im, 1, 2), (1, 1, CHUNK))
    args = (lam_re.reshape(g, p, 1), lam_im.reshape(g, p, 1),
            lam_re.reshape(g, 1, p), lam_im.reshape(g, 1, p), log_step.reshape(g, 1, 1),
            brt, bit, jnp.tile(brt, (1, CHUNK, 1)), jnp.tile(bit, (1, CHUNK, 1)), crt, cit)
    spec = lambda a: pl.BlockSpec((1,) + a.shape[1:], lambda i: (i, 0, 0))
    out_shape = [jax.ShapeDtypeStruct((g, S5_CONV, S5_CONV), BF16),
                 jax.ShapeDtypeStruct((g, S5_CONV, 2 * S5_STATE), F32),
                 jax.ShapeDtypeStruct((g, 2 * S5_STATE, S5_CONV), BF16),
                 jax.ShapeDtypeStruct((g, 2, S5_STATE), F32)]
    return pl.pallas_call(
        _s5_prep_kernel,
        grid=(g,),
        in_specs=[spec(a) for a in args],
        out_specs=[spec(o) for o in out_shape],
        out_shape=out_shape,
        compiler_params=_cparams(),
        name="s5_prep",
    )(*args)


def _split3(w):
    hi = w.astype(BF16)
    r1 = w - hi.astype(F32)
    mid = r1.astype(BF16)
    lo = (r1 - mid.astype(F32)).astype(BF16)
    return hi, mid, lo


def _s5_main_kernel(u_ref, m_ref, w_ref, wc_ref, a_ref, xsr_ref, xsi_ref,
                    y_ref, fpr_ref, fpi_ref, fsr_ref, fsi_ref,
                    hr_scr, hi_scr, x0r_scr, x0i_scr, *, n_prompt, n_chunks):
    u = u_ref[0]
    y_local = _dot(u, m_ref[0])
    whi, wmid, wlo = _split3(w_ref[0])
    hend = _dot(u, whi) + _dot(u, wmid) + _dot(u, wlo)
    hr_scr[...] = hend[:, :S5_STATE]
    hi_scr[...] = hend[:, S5_STATE:]
    ar = a_ref[0, 0:1, :]
    ai = a_ref[0, 1:2, :]
    xr = jnp.zeros((n_prompt, S5_STATE), F32)
    xi = jnp.zeros((n_prompt, S5_STATE), F32)
    for c in range(n_chunks):
        rows = slice(c * n_prompt, (c + 1) * n_prompt)
        x0r_scr[rows, :] = xr
        x0i_scr[rows, :] = xi
        xr, xi = (ar * xr - ai * xi + hr_scr[rows, :],
                  ar * xi + ai * xr + hi_scr[rows, :])
    fpr_ref[0] = xr
    fpi_ref[0] = xi
    base = n_chunks * n_prompt
    sr, si = xsr_ref[0], xsi_ref[0]
    x0r_scr[base:, :] = sr
    x0i_scr[base:, :] = si
    fsr_ref[0] = ar * sr - ai * si + hr_scr[base:, :]
    fsi_ref[0] = ar * si + ai * sr + hi_scr[base:, :]
    y = (y_local + _dot(x0r_scr[...].astype(BF16), wc_ref[0, :S5_STATE, :])
         + _dot(x0i_scr[...].astype(BF16), wc_ref[0, S5_STATE:, :]))
    y_ref[0] = y.astype(BF16)


def s5_main(u_rows, m, w, wc, a, xs_re, xs_im, n_prompt, n_chunks):
    g, r, _ = u_rows.shape
    n_sample = xs_re.shape[1]
    spec = lambda shape: pl.BlockSpec((1,) + tuple(shape[1:]), lambda i: (i, 0, 0))
    args = (u_rows, m, w, wc, a, xs_re, xs_im)
    out_shape = [jax.ShapeDtypeStruct((g, r, S5_CONV), BF16),
                 jax.ShapeDtypeStruct((g, n_prompt, S5_STATE), F32),
                 jax.ShapeDtypeStruct((g, n_prompt, S5_STATE), F32),
                 jax.ShapeDtypeStruct((g, n_sample, S5_STATE), F32),
                 jax.ShapeDtypeStruct((g, n_sample, S5_STATE), F32)]
    return pl.pallas_call(
        functools.partial(_s5_main_kernel, n_prompt=n_prompt, n_chunks=n_chunks),
        grid=(g,),
        in_specs=[spec(x.shape) for x in args],
        out_specs=[spec(o.shape) for o in out_shape],
        out_shape=out_shape,
        scratch_shapes=[pltpu.VMEM((r, S5_STATE), F32)] * 4,
        compiler_params=_cparams(),
        name="s5_main",
    )(*args)


def _hgrn_kernel(z_ref, s0_ref, lb_ref, ng_ref, o_ref, sfin_ref, st_scr, *, n_seq):
    c = pl.program_id(1)

    @pl.when(c == 0)
    def _():
        st_scr[...] = s0_ref[...]

    lbw = lb_ref[...]
    lbe = jnp.exp(lbw - jnp.max(lbw, axis=0, keepdims=True))
    lb_all = lbe[0:1, :] / jnp.sum(lbe, axis=0, keepdims=True)

    rowi = lax.broadcasted_iota(jnp.int32, (CHUNK, CHUNK), 0)
    coli = lax.broadcasted_iota(jnp.int32, (CHUNK, CHUNK), 1)
    tri = (coli <= rowi).astype(F32)
    rowk = lax.broadcasted_iota(jnp.int32, (CHUNK, HG_D), 0)
    row_in_sub = rowk % HG_SUB
    n_sub = CHUNK // HG_SUB

    for hd in range(HG_HEADS):
        cols = slice(hd * HG_D, (hd + 1) * HG_D)
        lb = lb_all[:, cols]
        ng = ng_ref[:, cols]

        def body(n, carry, hd=hd, lb=lb, ng=ng):
            zq = z_ref[n, :, hd * HG_D:(hd + 1) * HG_D]
            zf = z_ref[n, :, D_HG + hd * HG_D:D_HG + (hd + 1) * HG_D]
            v = z_ref[n, :, 2 * D_HG + hd * HG_D:2 * D_HG + (hd + 1) * HG_D]
            zg = z_ref[n, :, 3 * D_HG + hd * HG_D:3 * D_HG + (hd + 1) * HG_D]
            q = zq * jax.nn.sigmoid(zq)
            fg = lb + (1.0 - lb) * jax.nn.sigmoid(zf)
            kk = 1.0 - fg
            bcum = _dot(tri, jnp.log(fg), HIGHEST)
            b_last = bcum[CHUNK - 1:CHUNK, :]
            vb = v.astype(BF16)
            st = st_scr[n, hd]

            blocks = [jnp.zeros((HG_SUB, CHUNK), F32)]
            for i in range(1, n_sub):
                lo = i * HG_SUB
                bref = bcum[lo - 1:lo, :]
                qsc = q[lo:lo + HG_SUB, :] * jnp.exp(bcum[lo:lo + HG_SUB, :] - bref)
                ksc = kk * jnp.exp(jnp.where(rowk < lo, bref - bcum, -jnp.inf))
                blocks.append(lax.dot_general(qsc.astype(BF16), ksc.astype(BF16),
                                              (((1,), (1,)), ((), ())), preferred_element_type=F32))
            scores = jnp.concatenate(blocks, axis=0)

            for d in range(HG_SUB):
                if d == 0:
                    col = jnp.sum(q * kk, axis=-1, keepdims=True)
                else:
                    ks = pltpu.roll(kk, d, axis=0)
                    bs = pltpu.roll(bcum, d, axis=0)
                    dec = jnp.exp(jnp.where(row_in_sub >= d, bcum - bs, -jnp.inf))
                    col = jnp.sum(q * ks * dec, axis=-1, keepdims=True)
                scores = scores + jnp.where(coli == rowi - d, col, 0.0)

            qd = (q * jnp.exp(bcum)).astype(BF16)
            o = lax.dot_general(qd, st.astype(BF16), (((1,), (1,)), ((), ())),
                                preferred_element_type=F32)
            o = o + _dot(scores.astype(BF16), vb)
            kdec = (kk * jnp.exp(b_last - bcum)).astype(BF16)
            st_scr[n, hd] = jnp.exp(b_last) * st + lax.dot_general(
                vb, kdec, (((0,), (0,)), ((), ())), preferred_element_type=F32)

            on = o * lax.rsqrt(jnp.mean(o * o, axis=-1, keepdims=True) + RMS_EPS) * ng
            o_ref[n, :, hd * HG_D:(hd + 1) * HG_D] = on * (zg * jax.nn.sigmoid(zg))
            return carry

        lax.fori_loop(0, n_seq, body, 0)

    @pl.when(c == pl.num_programs(1) - 1)
    def _():
        sfin_ref[...] = st_scr[...]


def hgrn(z, s0_t, hg_lb, norm_g, n_seq):
    n, length, _ = z.shape
    return pl.pallas_call(
        functools.partial(_hgrn_kernel, n_seq=n_seq),
        grid=(n // n_seq, length // CHUNK),
        in_specs=[pl.BlockSpec((n_seq, CHUNK, 4 * D_HG), lambda g, c: (g, c, 0)),
                  pl.BlockSpec((n_seq, HG_HEADS, HG_D, HG_D), lambda g, c: (g, 0, 0, 0)),
                  pl.BlockSpec(hg_lb.shape, lambda g, c: (0, 0)),
                  pl.BlockSpec((1, D_HG), lambda g, c: (0, 0))],
        out_specs=[pl.BlockSpec((n_seq, CHUNK, D_HG), lambda g, c: (g, c, 0)),
                   pl.BlockSpec((n_seq, HG_HEADS, HG_D, HG_D), lambda g, c: (g, 0, 0, 0))],
        out_shape=[jax.ShapeDtypeStruct((n, length, D_HG), F32),
                   jax.ShapeDtypeStruct((n, HG_HEADS, HG_D, HG_D), F32)],
        scratch_shapes=[pltpu.VMEM((n_seq, HG_HEADS, HG_D, HG_D), F32)],
        compiler_params=_cparams(2),
        name="hgrn",
    )(z, s0_t, hg_lb, norm_g)


def _post_mix_kernel(h_ref, up_ref, us_ref, yp_ref, ys_ref, hgp_ref, hgs_ref,
                     d_ref, wglu_ref, bglu_ref, s5g_ref, wout_ref, g1_ref, b1_ref, rw_ref, rb_ref,
                     h1_ref, idx_ref, gate_ref, rank_ref, cnt_ref, run_scr, *, tm, n_first):
    i = pl.program_id(0)

    @pl.when(i == 0)
    def _():
        run_scr[...] = jnp.zeros_like(run_scr)

    def phase(u_ref, y_ref, hg_ref):
        ys = y_ref[...].astype(F32) + d_ref[...] * u_ref[...].astype(F32)
        gl = 0.5 * ys * (1.0 + lax.erf(ys * (2.0 ** -0.5)))
        s5o = gl * jax.nn.sigmoid(_dot(gl.astype(BF16), wglu_ref[...]) + bglu_ref[...])
        s5o = s5o * lax.rsqrt(jnp.mean(s5o * s5o, axis=-1, keepdims=True) + RMS_EPS) * s5g_ref[...]
        mix = (_dot(s5o.astype(BF16), wout_ref[:D_S5, :])
               + _dot(hg_ref[...].astype(BF16), wout_ref[D_S5:, :]))
        h1 = _layernorm(DEEPNORM_ALPHA * h_ref[...] + mix, g1_ref[...], b1_ref[...])
        _write_row_tiles(h1_ref, h1, tm)

        logits = _dot(h1, rw_ref[...], HIGHEST) + rb_ref[...]
        eid = lax.broadcasted_iota(jnp.int32, (tm, N_EXPERTS), 1)
        vals, idxs = [], []
        for _ in range(TOP_K):
            m = jnp.max(logits, axis=-1, keepdims=True)
            ix = jnp.min(jnp.where(logits == m, eid, N_EXPERTS), axis=-1, keepdims=True)
            vals.append(m)
            idxs.append(ix)
            logits = jnp.where(eid == ix, -jnp.inf, logits)
        exps = [jnp.exp(v - vals[0]) for v in vals]
        den = exps[0] + exps[1] + exps[2] + exps[3]

        onehot = jnp.zeros((tm, N_EXPERTS), F32)
        for ix in idxs:
            onehot = onehot + (eid == ix).astype(F32)
        rowi = lax.broadcasted_iota(jnp.int32, (tm, tm), 0)
        coli = lax.broadcasted_iota(jnp.int32, (tm, tm), 1)
        earlier = (coli < rowi).astype(BF16)
        prefix = _dot(earlier, onehot.astype(BF16)) + run_scr[...]
        for k in range(TOP_K):
            idx_ref[:, k:k + 1] = idxs[k]
            gate_ref[:, k:k + 1] = exps[k] / den
            rank_ref[:, k:k + 1] = jnp.sum(jnp.where(eid == idxs[k], prefix, 0.0),
                                           axis=-1, keepdims=True).astype(jnp.int32)
        run_scr[...] = run_scr[...] + jnp.sum(onehot, axis=0, keepdims=True)
        cnt_ref[...] = run_scr[...]

    pl.when(i < n_first)(lambda: phase(up_ref, yp_ref, hgp_ref))
    pl.when(i >= n_first)(lambda: phase(us_ref, ys_ref, hgs_ref))


def post_mix(h0, u_pair, y_pair, hg_pair, d_skip, wglu, bglu, s5g, wout, g1, b1, rw, rb, tm):
    t = h0.shape[0]
    n_first = u_pair[0].shape[0] // tm
    row = lambda i: (i, 0)
    fixed = lambda i: (0, 0)
    full = lambda a: pl.BlockSpec(a.shape, fixed)
    weights = (d_skip, wglu, bglu, s5g, wout, g1, b1, rw, rb)
    return pl.pallas_call(
        functools.partial(_post_mix_kernel, tm=tm, n_first=n_first),
        grid=(t // tm,),
        in_specs=[pl.BlockSpec((tm, D_MODEL), row)]
                 + _two_phase_specs((tm, D_S5), n_first) * 3
                 + [full(a) for a in weights],
        out_specs=[pl.BlockSpec((tm * ROW_TILES, LANES), row),
                   pl.BlockSpec((tm, TOP_K), row), pl.BlockSpec((tm, TOP_K), row),
                   pl.BlockSpec((tm, TOP_K), row), pl.BlockSpec((1, N_EXPERTS), fixed)],
        out_shape=[jax.ShapeDtypeStruct((t * ROW_TILES, LANES), F32),
                   jax.ShapeDtypeStruct((t, TOP_K), jnp.int32),
                   jax.ShapeDtypeStruct((t, TOP_K), F32),
                   jax.ShapeDtypeStruct((t, TOP_K), jnp.int32),
                   jax.ShapeDtypeStruct((1, N_EXPERTS), F32)],
        scratch_shapes=[pltpu.VMEM((1, N_EXPERTS), F32)],
        compiler_params=_cparams(),
        name="post_mix",
    )(h0, *u_pair, *y_pair, *hg_pair, *weights)


def _dispatch_kernel(pend_ref, dest_ref, h_ref, xs_ref, zero_scr, zsem, sem, *, tm):
    n_rows = xs_ref.shape[0] // ROW_TILES

    @pl.when(pl.program_id(0) == 0)
    def _():
        zero_scr[...] = jnp.zeros_like(zero_scr)

        def last_block(e):
            prev = pend_ref[e - 1] if e > 0 else 0
            copy = pltpu.make_async_copy(
                zero_scr, xs_ref.at[pl.ds(pl.multiple_of(jnp.maximum(pend_ref[e] - MOE_ROWS, 0) * ROW_TILES,
                                                         ROW_TILES), MOE_ROWS * ROW_TILES)], zsem)
            return pend_ref[e] > prev, copy

        def tail_block(j):
            row0 = pend_ref[N_EXPERTS - 1] + j * MOE_ROWS
            copy = pltpu.make_async_copy(
                zero_scr, xs_ref.at[pl.ds(pl.multiple_of(jnp.minimum(row0, n_rows - MOE_ROWS) * ROW_TILES,
                                                         ROW_TILES), MOE_ROWS * ROW_TILES)], zsem)
            return row0 < n_rows, copy

        blocks = [last_block(e) for e in range(N_EXPERTS)] + [tail_block(j) for j in range(N_EXPERTS)]
        for used, copy in blocks:
            pl.when(used)(copy.start)
        for used, copy in blocks:
            pl.when(used)(copy.wait)

    def body(t, carry):
        for k in range(TOP_K):
            pltpu.make_async_copy(h_ref.at[_row_tile(t)], xs_ref.at[_row_tile(dest_ref[TOP_K * t + k])],
                                  sem).start(priority=k % 2)
        return carry

    lax.fori_loop(0, tm, body, 0)
    for _ in range(TOP_K):
        pltpu.make_async_copy(h_ref, xs_ref.at[pl.ds(0, tm * ROW_TILES)], sem).wait()


def dispatch(pend, dest_flat, h1, n_rows, tm):
    t = h1.shape[0] // ROW_TILES
    grid_spec = pltpu.PrefetchScalarGridSpec(
        num_scalar_prefetch=1,
        grid=(t // tm,),
        in_specs=[pl.BlockSpec((TOP_K * tm,), lambda i, pend: (i,), memory_space=pltpu.SMEM),
                  pl.BlockSpec((tm * ROW_TILES, LANES), lambda i, pend: (i, 0))],
        out_specs=pl.BlockSpec(memory_space=pl.ANY),
        scratch_shapes=[pltpu.VMEM((MOE_ROWS * ROW_TILES, LANES), F32),
                        pltpu.SemaphoreType.DMA(()), pltpu.SemaphoreType.DMA(())],
    )
    return pl.pallas_call(
        functools.partial(_dispatch_kernel, tm=tm),
        grid_spec=grid_spec,
        out_shape=jax.ShapeDtypeStruct((n_rows * ROW_TILES, LANES), F32),
        compiler_params=_cparams(),
        name="moe_dispatch",
    )(pend, dest_flat, h1)


def _moe_ffn_kernel(be_ref, nu_ref, x_ref, wg_ref, bg_ref, wu_ref, bu_ref, wd_ref, bd_ref, y_ref):
    del be_ref

    @pl.when(pl.program_id(0) < nu_ref[0])
    def _():
        x = _read_row_tiles(x_ref, MOE_ROWS).astype(BF16)
        gt = jnp.minimum(_dot(x, wg_ref[0]) + bg_ref[0], SWIGLU_LIMIT)
        up = jnp.clip(_dot(x, wu_ref[0]) + bu_ref[0], -SWIGLU_LIMIT, SWIGLU_LIMIT)
        hid = (up + 1.0) * (gt * jax.nn.sigmoid(SWIGLU_ALPHA * gt))
        _write_row_tiles(y_ref, _dot(hid.astype(BF16), wd_ref[0]) + bd_ref[0], MOE_ROWS)

    @pl.when(pl.program_id(0) >= nu_ref[0])
    def _():
        y_ref[...] = jnp.zeros_like(y_ref)


def moe_ffn(block_e, n_used, xs, wg, bg, wu, bu, wd, bd):
    n_rows = xs.shape[0] // ROW_TILES
    n_blocks = n_rows // MOE_ROWS
    wsel = lambda i, be, nu: (be[i], 0, 0)
    d_ff = wg.shape[-1]
    grid_spec = pltpu.PrefetchScalarGridSpec(
        num_scalar_prefetch=2,
        grid=(n_blocks,),
        in_specs=[pl.BlockSpec((MOE_ROWS * ROW_TILES, LANES),
                               lambda i, be, nu: (jnp.minimum(i, nu[0] - 1), 0)),
                  pl.BlockSpec((1, D_MODEL, d_ff), wsel), pl.BlockSpec((1, 1, d_ff), wsel),
                  pl.BlockSpec((1, D_MODEL, d_ff), wsel), pl.BlockSpec((1, 1, d_ff), wsel),
                  pl.BlockSpec((1, d_ff, D_MODEL), wsel), pl.BlockSpec((1, 1, D_MODEL), wsel)],
        out_specs=pl.BlockSpec((MOE_ROWS * ROW_TILES, LANES), lambda i, be, nu: (i, 0)),
    )
    return pl.pallas_call(
        _moe_ffn_kernel,
        grid_spec=grid_spec,
        out_shape=jax.ShapeDtypeStruct((n_rows * ROW_TILES, LANES), F32),
        compiler_params=_cparams(),
        name="moe_ffn",
    )(block_e, n_used, xs, wg, bg, wu, bu, wd, bd)


def _combine_kernel(dcur_ref, dnext_ref, gate_ref, h_ref, pp_ref, ps_ref, yb_ref,
                    plew_ref, plegw_ref, g2_ref, b2_ref, outp_ref, outs_ref, buf, sem,
                    *, tm, n_first):
    i = pl.program_id(0)
    n = pl.num_programs(0)
    slot = i % 2

    def gather(dest_ref, slot_id):
        def body(t, carry):
            for k in range(TOP_K):
                pltpu.make_async_copy(yb_ref.at[_row_tile(dest_ref[TOP_K * t + k])],
                                      buf.at[slot_id, k, _row_tile(t)],
                                      sem.at[slot_id]).start(priority=k % 2)
            return carry
        lax.fori_loop(0, tm, body, 0)

    pl.when(i == 0)(lambda: gather(dcur_ref, 0))
    pl.when(i + 1 < n)(lambda: gather(dnext_ref, 1 - slot))
    for k in range(TOP_K):
        pltpu.make_async_copy(yb_ref.at[pl.ds(0, tm * ROW_TILES)], buf.at[slot, k], sem.at[slot]).wait()

    gates = gate_ref[...]
    cols = []
    for j in range(ROW_TILES):
        ffn = gates[:, 0:1] * buf[slot, 0, _chunk(tm, j), :]
        for k in range(1, TOP_K):
            ffn = ffn + gates[:, k:k + 1] * buf[slot, k, _chunk(tm, j), :]
        cols.append(DEEPNORM_ALPHA * h_ref[_chunk(tm, j), :] + ffn)
    r = jnp.concatenate(cols, axis=1)
    gate = jax.nn.sigmoid(_dot(r.astype(BF16), plegw_ref[...]))

    def finish(p_ref, out_ref):
        e = _dot(p_ref[...].astype(BF16), plew_ref[...]) * gate
        out_ref[...] = _layernorm(r + e, g2_ref[...], b2_ref[...])

    pl.when(i < n_first)(lambda: finish(pp_ref, outp_ref))
    pl.when(i >= n_first)(lambda: finish(ps_ref, outs_ref))


def combine(dest_flat, gates, h1, p_pair, yb, plew, plegw, g2, b2, tm):
    t = h1.shape[0] // ROW_TILES
    tp, ts = p_pair[0].shape[0], p_pair[1].shape[0]
    n_first = tp // tm
    n_tiles = t // tm
    row = lambda i: (i, 0)
    fixed = lambda i: (0, 0)
    dest_block = lambda f: pl.BlockSpec((TOP_K * tm,), f, memory_space=pltpu.SMEM)
    return pl.pallas_call(
        functools.partial(_combine_kernel, tm=tm, n_first=n_first),
        grid=(n_tiles,),
        in_specs=[dest_block(lambda i: (i,)),
                  dest_block(lambda i: (jnp.minimum(i + 1, n_tiles - 1),)),
                  pl.BlockSpec((tm, TOP_K), row),
                  pl.BlockSpec((tm * ROW_TILES, LANES), row)]
                 + _two_phase_specs((tm, PLE_DIM), n_first)
                 + [pl.BlockSpec(memory_space=pl.ANY),
                    pl.BlockSpec(plew.shape, fixed), pl.BlockSpec(plegw.shape, fixed),
                    pl.BlockSpec((1, D_MODEL), fixed), pl.BlockSpec((1, D_MODEL), fixed)],
        out_specs=_two_phase_specs((tm, D_MODEL), n_first),
        out_shape=[jax.ShapeDtypeStruct((tp, D_MODEL), F32), jax.ShapeDtypeStruct((ts, D_MODEL), F32)],
        scratch_shapes=[pltpu.VMEM((2, TOP_K, tm * ROW_TILES, LANES), F32),
                        pltpu.SemaphoreType.DMA((2,))],
        compiler_params=_cparams(),
        name="moe_combine",
    )(dest_flat, dest_flat, gates, h1, *p_pair, yb, plew, plegw, g2, b2)


def _s5_rows(u_p, u_s, nb, nc, ns):
    up = u_p.reshape(nb, nc, CHUNK, S5_GROUPS, S5_GROUP).transpose(3, 1, 0, 2, 4)
    us = u_s.reshape(ns, CHUNK, S5_GROUPS, S5_GROUP).transpose(2, 0, 1, 3)
    return jnp.concatenate([up.reshape(S5_GROUPS, nc * nb, S5_CONV),
                            us.reshape(S5_GROUPS, ns, S5_CONV)], axis=1)


def _s5_tokens(y_rows, nb, nc, ns):
    yp = y_rows[:, :nc * nb].reshape(S5_GROUPS, nc, nb, CHUNK, S5_GROUP).transpose(2, 1, 3, 0, 4)
    ys = y_rows[:, nc * nb:].reshape(S5_GROUPS, ns, CHUNK, S5_GROUP).transpose(1, 2, 0, 3)
    return yp.reshape(nb * nc * CHUNK, D_S5), ys.reshape(ns * CHUNK, D_S5)


def _row(v):
    return v.reshape(1, -1)


def kernel(x_prompt, x_sample, state_s5_re, state_s5_im, state_hgrn, p_prompt, p_sample, ln_in_g, ln_in_b, w_in, s5_lambda_re, s5_lambda_im, s5_log_step, s5_b_re, s5_b_im, s5_c_re, s5_c_im, s5_d, s5_w_glu, s5_b_glu, s5_norm_g, hg_lb, hg_norm_g, w_out, ln1_g, ln1_b, router_w, router_b, w_gate, b_gate, w_up, b_up, w_down, b_down, ple_w, ple_gate_w, ln2_g, ln2_b):
    nb, seq, _ = x_prompt.shape
    ns, dseq, _ = x_sample.shape
    assert dseq == CHUNK and seq % CHUNK == 0 and w_in.shape[0] == 1
    nc = seq // CHUNK
    tp, ts = nb * seq, ns * dseq
    t = tp + ts
    tm = 512 if (tp % 512 == 0 and ts % 512 == 0) else 256
    tg = 256
    td = 1024 if (tp % 1024 == 0 and ts % 1024 == 0) else 256
    assert tp % tm == 0 and ts % tm == 0 and tp % tg == 0 and ts % tg == 0 and ns % nb == 0

    h0, u_p, u_s, z_p, z_s = ln_in_proj(
        x_prompt.reshape(tp, D_MODEL), x_sample.reshape(ts, D_MODEL),
        _row(ln_in_g), _row(ln_in_b), w_in[0].astype(BF16), tm)

    m, w, wc, a = s5_prep(s5_lambda_re[0], s5_lambda_im[0], s5_log_step[0],
                          s5_b_re[0], s5_b_im[0], s5_c_re[0], s5_c_im[0])
    y_rows, fpr, fpi, fsr, fsi = s5_main(_s5_rows(u_p, u_s, nb, nc, ns), m, w, wc, a,
                                         jnp.swapaxes(state_s5_re[0], 0, 1),
                                         jnp.swapaxes(state_s5_im[0], 0, 1), nb, nc)
    y_pair = _s5_tokens(y_rows, nb, nc, ns)

    zero_state = jnp.zeros((nb, HG_HEADS, HG_D, HG_D), F32)
    ng = _row(hg_norm_g[0])
    o_p, st_p = hgrn(z_p.reshape(nb, seq, 4 * D_HG), zero_state, hg_lb, ng, nb)
    o_s, st_s = hgrn(z_s.reshape(ns, dseq, 4 * D_HG), jnp.swapaxes(state_hgrn[0], 2, 3),
                     hg_lb, ng, nb)

    h1, idx, gates, rank, counts = post_mix(
        h0, (u_p, u_s), y_pair, (o_p.reshape(tp, D_HG), o_s.reshape(ts, D_HG)),
        _row(s5_d[0]), s5_w_glu[0].astype(BF16), _row(s5_b_glu[0]),
        _row(s5_norm_g[0]), w_out[0].astype(BF16), _row(ln1_g[0]), _row(ln1_b[0]),
        router_w[0], _row(router_b[0]), tm)

    counts = counts[0].astype(jnp.int32)
    padded = (counts + MOE_ROWS - 1) // MOE_ROWS * MOE_ROWS
    pend = jnp.cumsum(padded)
    pstart = pend - padded
    experts = jnp.arange(N_EXPERTS, dtype=jnp.int32)
    dest_flat = (jnp.sum(jnp.where(idx[..., None] == experts, pstart, 0), axis=-1) + rank).reshape(-1)
    n_blocks = -(-t * TOP_K // MOE_ROWS) + N_EXPERTS
    n_used = (pend[-1] // MOE_ROWS).astype(jnp.int32)
    blk = jnp.arange(n_blocks, dtype=jnp.int32)
    blk = jnp.minimum(blk, n_used - 1)
    block_e = jnp.sum((pend[None, :] <= (blk * MOE_ROWS)[:, None]).astype(jnp.int32), axis=1)
    block_e = jnp.minimum(block_e, N_EXPERTS - 1)

    xs = dispatch(pend, dest_flat, h1, n_blocks * MOE_ROWS, td)
    yb = moe_ffn(block_e, n_used.reshape(1), xs,
                 w_gate[0].astype(BF16), b_gate[0][:, None, :], w_up[0].astype(BF16), b_up[0][:, None, :],
                 w_down[0].astype(BF16), b_down[0][:, None, :])
    out_p, out_s = combine(dest_flat, gates, h1,
                           (p_prompt[0].reshape(tp, PLE_DIM), p_sample[0].reshape(ts, PLE_DIM)),
                           yb, ple_w[0].astype(BF16), ple_gate_w[0].astype(BF16),
                           _row(ln2_g[0]), _row(ln2_b[0]), tg)

    def s5_state(f, n):
        return jnp.swapaxes(f, 0, 1).reshape(1, n, S5_GROUPS, S5_STATE)

    return (out_p.reshape(nb, seq, D_MODEL), out_s.reshape(ns, dseq, D_MODEL),
            s5_state(fpr, nb), s5_state(fpi, nb), jnp.swapaxes(st_p, 2, 3)[None],
            s5_state(fsr, ns), s5_state(fsi, ns), jnp.swapaxes(st_s, 2, 3)[None])
```

```python
import functools

import jax
import jax.numpy as jnp
from jax import lax
from jax.experimental import pallas as pl
from jax.experimental.pallas import tpu as pltpu

F32 = jnp.float32
BF16 = jnp.bfloat16
HIGHEST = lax.Precision.HIGHEST

D_MODEL = 1024
CHUNK = 64
PLE_DIM = 256
D_S5 = 512
S5_GROUP = 16
S5_GROUPS = 32
S5_STATE = 64
D_HG = 512
HG_HEADS = 4
HG_D = 128
D_IN = D_S5 + 4 * D_HG
N_EXPERTS = 32
TOP_K = 4
SWIGLU_LIMIT = 7.0
SWIGLU_ALPHA = 1.702
DEEPNORM_ALPHA = 2.0 ** 0.25
LN_EPS = 1e-5
RMS_EPS = 1e-6

LANES = 128
SUBLANES = 8
ROW_TILES = D_MODEL // LANES
S5_CONV = CHUNK * S5_GROUP
HG_LEVELS = (32, 16, 8, 4, 2, 1)
MOE_ROWS = 512
VMEM_LIMIT = 56 * 1024 * 1024

assert ROW_TILES == SUBLANES


def _cparams(n_axes=1):
    return pltpu.CompilerParams(dimension_semantics=("arbitrary",) * n_axes,
                                vmem_limit_bytes=VMEM_LIMIT)


def _dot(a, b, precision=None):
    return jnp.dot(a, b, preferred_element_type=F32, precision=precision)


def _layernorm(x, g, b):
    mu = jnp.mean(x, axis=-1, keepdims=True)
    xc = x - mu
    var = jnp.mean(xc * xc, axis=-1, keepdims=True)
    return xc * lax.rsqrt(var + LN_EPS) * g + b


def _two_phase_specs(block, n_first):
    nd = len(block)
    first = pl.BlockSpec(block, lambda i: (jnp.minimum(i, n_first - 1),) + (0,) * (nd - 1))
    second = pl.BlockSpec(block, lambda i: (jnp.maximum(i - n_first, 0),) + (0,) * (nd - 1))
    return [first, second]


def _chunk(rows, j):
    return pl.ds(j, rows, stride=ROW_TILES)


def _read_row_tiles(ref, rows):
    return jnp.concatenate([ref[_chunk(rows, j), :] for j in range(ROW_TILES)], axis=1)


def _write_row_tiles(ref, val, rows):
    for j in range(ROW_TILES):
        ref[_chunk(rows, j), :] = val[:, j * LANES:(j + 1) * LANES]


def _row_tile(r):
    return pl.ds(pl.multiple_of(r * ROW_TILES, ROW_TILES), ROW_TILES)


def _ln_in_proj_kernel(xp_ref, xs_ref, g_ref, b_ref, w_ref, h_ref, up_ref, us_ref, zp_ref, zs_ref,
                       *, n_first):
    def phase(x_ref, u_ref, z_ref):
        h = _layernorm(x_ref[...], g_ref[...], b_ref[...])
        h_ref[...] = h
        hb = h.astype(BF16)
        u_ref[...] = _dot(hb, w_ref[:, :D_S5]).astype(BF16)
        z_ref[...] = _dot(hb, w_ref[:, D_S5:])

    i = pl.program_id(0)
    pl.when(i < n_first)(lambda: phase(xp_ref, up_ref, zp_ref))
    pl.when(i >= n_first)(lambda: phase(xs_ref, us_ref, zs_ref))


def ln_in_proj(xp, xs, g, b, w_bf16, tm):
    tp, ts = xp.shape[0], xs.shape[0]
    n_first = tp // tm
    fixed = lambda i: (0, 0)
    return pl.pallas_call(
        functools.partial(_ln_in_proj_kernel, n_first=n_first),
        grid=((tp + ts) // tm,),
        in_specs=_two_phase_specs((tm, D_MODEL), n_first)
                 + [pl.BlockSpec((1, D_MODEL), fixed), pl.BlockSpec((1, D_MODEL), fixed),
                    pl.BlockSpec((D_MODEL, D_IN), fixed)],
        out_specs=[pl.BlockSpec((tm, D_MODEL), lambda i: (i, 0))]
                  + _two_phase_specs((tm, D_S5), n_first)
                  + _two_phase_specs((tm, 4 * D_HG), n_first),
        out_shape=[jax.ShapeDtypeStruct((tp + ts, D_MODEL), F32),
                   jax.ShapeDtypeStruct((tp, D_S5), BF16), jax.ShapeDtypeStruct((ts, D_S5), BF16),
                   jax.ShapeDtypeStruct((tp, 4 * D_HG), F32), jax.ShapeDtypeStruct((ts, 4 * D_HG), F32)],
        compiler_params=_cparams(),
        name="ln_in_proj",
    )(xp, xs, g, b, w_bf16)


def _s5_prep_kernel(lrc_ref, lic_ref, lrr_ref, lir_ref, ls_ref, brt_ref, bit_ref,
                    brtt_ref, bitt_ref, crt_ref, cit_ref,
                    m_ref, w_ref, wc_ref, a_ref):
    step = jnp.exp(ls_ref[0])

    def discretise(lr_raw, li):
        lr = jnp.minimum(lr_raw, -1e-4)
        dr, di = lr * step, li * step
        mag = jnp.exp(dr)
        a_re, a_im = mag * jnp.cos(di), mag * jnp.sin(di)
        den = lr * lr + li * li
        nr = a_re - 1.0
        fr = (nr * lr + a_im * li) / den
        fi = (a_im * lr - nr * li) / den
        return dr, di, fr, fi

    dr_c, di_c, _, _ = discretise(lrc_ref[0], lic_ref[0])
    dr_r, di_r, fr_r, fi_r = discretise(lrr_ref[0], lir_ref[0])

    lane = lax.broadcasted_iota(jnp.int32, (1, S5_CONV), 1)
    lag = (lane // S5_GROUP).astype(F32)

    def c_times_power(tf):
        mag = jnp.exp(dr_c * tf)
        ang = di_c * tf
        pr, pi = mag * jnp.cos(ang), mag * jnp.sin(ang)
        ctr, cti = crt_ref[0], cit_ref[0]
        return ctr * pr - cti * pi, ctr * pi + cti * pr

    cpr, cpi = c_times_power(lag)
    bbr = fr_r * brt_ref[0] - fi_r * bit_ref[0]
    bbi = fr_r * bit_ref[0] + fi_r * brt_ref[0]
    kt = _dot(bbr, cpr, HIGHEST) - _dot(bbi, cpi, HIGHEST)
    for s in range(CHUNK):
        shifted = kt if s == 0 else pltpu.roll(kt, S5_GROUP * s, axis=1)
        m_ref[0, S5_GROUP * s:S5_GROUP * (s + 1), :] = jnp.where(
            lane >= S5_GROUP * s, shifted, 0.0).astype(BF16)

    rowi = lax.broadcasted_iota(jnp.int32, (S5_CONV, 1), 0)
    rem = (CHUNK - 1 - rowi // S5_GROUP).astype(F32)
    magw = jnp.exp(dr_r * rem)
    angw = di_r * rem
    pwr, pwi = magw * jnp.cos(angw), magw * jnp.sin(angw)
    bbtr = fr_r * brtt_ref[0] - fi_r * bitt_ref[0]
    bbti = fr_r * bitt_ref[0] + fi_r * brtt_ref[0]
    w_ref[0, :, :S5_STATE] = pwr * bbtr - pwi * bbti
    w_ref[0, :, S5_STATE:] = pwr * bbti + pwi * bbtr

    c1r, c1i = c_times_power(lag + 1.0)
    wc_ref[0, :S5_STATE, :] = c1r.astype(BF16)
    wc_ref[0, S5_STATE:, :] = (-c1i).astype(BF16)

    full = float(CHUNK)
    mag_c = jnp.exp(dr_r * full)
    a_ref[0, 0:1, :] = mag_c * jnp.cos(di_r * full)
    a_ref[0, 1:2, :] = mag_c * jnp.sin(di_r * full)


def s5_prep(lam_re, lam_im, log_step, b_re, b_im, c_re, c_im):
    g, p = lam_re.shape
    brt = jnp.swapaxes(b_re, 1, 2)
    bit = jnp.swapaxes(b_im, 1, 2)
    crt = jnp.tile(jnp.swapaxes(c_re, 1, 2), (1, 1, CHUNK))
    cit = jnp.tile(jnp.swapaxes(c_im, 1, 2), (1, 1, CHUNK))
    args = (lam_re.reshape(g, p, 1), lam_im.reshape(g, p, 1),
            lam_re.reshape(g, 1, p), lam_im.reshape(g, 1, p), log_step.reshape(g, 1, 1),
            brt, bit, jnp.tile(brt, (1, CHUNK, 1)), jnp.tile(bit, (1, CHUNK, 1)), crt, cit)
    spec = lambda a: pl.BlockSpec((1,) + a.shape[1:], lambda i: (i, 0, 0))
    out_shape = [jax.ShapeDtypeStruct((g, S5_CONV, S5_CONV), BF16),
                 jax.ShapeDtypeStruct((g, S5_CONV, 2 * S5_STATE), F32),
                 jax.ShapeDtypeStruct((g, 2 * S5_STATE, S5_CONV), BF16),
                 jax.ShapeDtypeStruct((g, 2, S5_STATE), F32)]
    return pl.pallas_call(
        _s5_prep_kernel,
        grid=(g,),
        in_specs=[spec(a) for a in args],
        out_specs=[spec(o) for o in out_shape],
        out_shape=out_shape,
        compiler_params=_cparams(),
        name="s5_prep",
    )(*args)


def _split3(w):
    hi = w.astype(BF16)
    r1 = w - hi.astype(F32)
    mid = r1.astype(BF16)
    lo = (r1 - mid.astype(F32)).astype(BF16)
    return hi, mid, lo


def _s5_main_kernel(u_ref, m_ref, w_ref, wc_ref, a_ref, xsr_ref, xsi_ref,
                    y_ref, fpr_ref, fpi_ref, fsr_ref, fsi_ref,
                    hr_scr, hi_scr, x0r_scr, x0i_scr, *, n_prompt, n_chunks):
    u = u_ref[0]
    y_local = _dot(u, m_ref[0])
    whi, wmid, wlo = _split3(w_ref[0])
    hend = _dot(u, whi) + _dot(u, wmid) + _dot(u, wlo)
    hr_scr[...] = hend[:, :S5_STATE]
    hi_scr[...] = hend[:, S5_STATE:]
    ar = a_ref[0, 0:1, :]
    ai = a_ref[0, 1:2, :]
    xr = jnp.zeros((n_prompt, S5_STATE), F32)
    xi = jnp.zeros((n_prompt, S5_STATE), F32)
    for c in range(n_chunks):
        rows = slice(c * n_prompt, (c + 1) * n_prompt)
        x0r_scr[rows, :] = xr
        x0i_scr[rows, :] = xi
        xr, xi = (ar * xr - ai * xi + hr_scr[rows, :],
                  ar * xi + ai * xr + hi_scr[rows, :])
    fpr_ref[0] = xr
    fpi_ref[0] = xi
    base = n_chunks * n_prompt
    sr, si = xsr_ref[0], xsi_ref[0]
    x0r_scr[base:, :] = sr
    x0i_scr[base:, :] = si
    fsr_ref[0] = ar * sr - ai * si + hr_scr[base:, :]
    fsi_ref[0] = ar * si + ai * sr + hi_scr[base:, :]
    y = (y_local + _dot(x0r_scr[...].astype(BF16), wc_ref[0, :S5_STATE, :])
         + _dot(x0i_scr[...].astype(BF16), wc_ref[0, S5_STATE:, :]))
    y_ref[0] = y.astype(BF16)


def s5_main(u_rows, m, w, wc, a, xs_re, xs_im, n_prompt, n_chunks):
    g, r, _ = u_rows.shape
    n_sample = xs_re.shape[1]
    spec = lambda shape: pl.BlockSpec((1,) + tuple(shape[1:]), lambda i: (i, 0, 0))
    args = (u_rows, m, w, wc, a, xs_re, xs_im)
    out_shape = [jax.ShapeDtypeStruct((g, r, S5_CONV), BF16),
                 jax.ShapeDtypeStruct((g, n_prompt, S5_STATE), F32),
                 jax.ShapeDtypeStruct((g, n_prompt, S5_STATE), F32),
                 jax.ShapeDtypeStruct((g, n_sample, S5_STATE), F32),
                 jax.ShapeDtypeStruct((g, n_sample, S5_STATE), F32)]
    return pl.pallas_call(
        functools.partial(_s5_main_kernel, n_prompt=n_prompt, n_chunks=n_chunks),
        grid=(g,),
        in_specs=[spec(x.shape) for x in args],
        out_specs=[spec(o.shape) for o in out_shape],
        out_shape=out_shape,
        scratch_shapes=[pltpu.VMEM((r, S5_STATE), F32)] * 4,
        compiler_params=_cparams(),
        name="s5_main",
    )(*args)


def _hgrn_kernel(z_ref, s0_ref, lb_ref, ng_ref, o_ref, sfin_ref, st_scr, *, n_seq):
    c = pl.program_id(1)

    @pl.when(c == 0)
    def _():
        st_scr[...] = s0_ref[...]

    lbw = lb_ref[...]
    lbe = jnp.exp(lbw - jnp.max(lbw, axis=0, keepdims=True))
    lb_all = lbe[0:1, :] / jnp.sum(lbe, axis=0, keepdims=True)

    rowi = lax.broadcasted_iota(jnp.int32, (CHUNK, CHUNK), 0)
    coli = lax.broadcasted_iota(jnp.int32, (CHUNK, CHUNK), 1)
    rowk = lax.broadcasted_iota(jnp.int32, (CHUNK, HG_D), 0)
    cum_rows = [(coli <= rowi).astype(F32)]
    upper, lower, same_block = [], [], []
    for m in HG_LEVELS:
        ref_row = (rowi // (2 * m)) * (2 * m) + (m - 1)
        cum_rows.append((coli <= ref_row).astype(F32))
        upper.append((rowk % (2 * m)) >= m)
        lower.append((rowk % (2 * m)) < m)
        same_block.append((rowi // (2 * m)) == (coli // (2 * m)))
    cum_mat = jnp.concatenate(cum_rows, axis=0).astype(BF16)
    cum_mat3 = jnp.concatenate([cum_mat] * 3, axis=1)
    diag = rowi == coli
    nt = (((1,), (1,)), ((), ()))

    def body(n, carry):
        zf = z_ref[n, :, D_HG:2 * D_HG]
        fg_all = lb_all + (1.0 - lb_all) * jax.nn.sigmoid(zf)
        cums = _dot(cum_mat3, jnp.concatenate(_split3(jnp.log(fg_all)), axis=0))
        for hd in range(HG_HEADS):
            cols = slice(hd * HG_D, (hd + 1) * HG_D)
            zq = z_ref[n, :, hd * HG_D:(hd + 1) * HG_D]
            v = z_ref[n, :, 2 * D_HG + hd * HG_D:2 * D_HG + (hd + 1) * HG_D]
            zg = z_ref[n, :, 3 * D_HG + hd * HG_D:3 * D_HG + (hd + 1) * HG_D]
            q = zq * jax.nn.sigmoid(zq)
            kk = 1.0 - fg_all[:, cols]
            bcum = cums[:CHUNK, cols]
            b_last = bcum[CHUNK - 1:CHUNK, :]
            vb = v.astype(BF16)
            st = st_scr[n, hd]

            scores = jnp.where(diag, lax.dot_general(q.astype(BF16), kk.astype(BF16), nt,
                                                     preferred_element_type=F32), 0.0)
            for lvl in range(len(HG_LEVELS)):
                bref = cums[(lvl + 1) * CHUNK:(lvl + 2) * CHUNK, cols]
                dec = jnp.exp(-jnp.abs(bcum - bref))
                qs = jnp.where(upper[lvl], q * dec, 0.0)
                ks = jnp.where(lower[lvl], kk * dec, 0.0)
                sc = lax.dot_general(qs.astype(BF16), ks.astype(BF16), nt, preferred_element_type=F32)
                scores = scores + jnp.where(same_block[lvl], sc, 0.0)

            qd = (q * jnp.exp(bcum)).astype(BF16)
            o = lax.dot_general(qd, st.astype(BF16), nt, preferred_element_type=F32)
            o = o + _dot(scores.astype(BF16), vb)
            kdec = (kk * jnp.exp(b_last - bcum)).astype(BF16)
            st_scr[n, hd] = jnp.exp(b_last) * st + lax.dot_general(
                vb, kdec, (((0,), (0,)), ((), ())), preferred_element_type=F32)

            on = o * lax.rsqrt(jnp.mean(o * o, axis=-1, keepdims=True) + RMS_EPS) * ng_ref[:, cols]
            o_ref[n, :, hd * HG_D:(hd + 1) * HG_D] = on * (zg * jax.nn.sigmoid(zg))
        return carry

    lax.fori_loop(0, n_seq, body, 0, unroll=True)

    @pl.when(c == pl.num_programs(1) - 1)
    def _():
        sfin_ref[...] = st_scr[...]


def hgrn(z, s0_t, hg_lb, norm_g, n_seq):
    n, length, _ = z.shape
    return pl.pallas_call(
        functools.partial(_hgrn_kernel, n_seq=n_seq),
        grid=(n // n_seq, length // CHUNK),
        in_specs=[pl.BlockSpec((n_seq, CHUNK, 4 * D_HG), lambda g, c: (g, c, 0)),
                  pl.BlockSpec((n_seq, HG_HEADS, HG_D, HG_D), lambda g, c: (g, 0, 0, 0)),
                  pl.BlockSpec(hg_lb.shape, lambda g, c: (0, 0)),
                  pl.BlockSpec((1, D_HG), lambda g, c: (0, 0))],
        out_specs=[pl.BlockSpec((n_seq, CHUNK, D_HG), lambda g, c: (g, c, 0)),
                   pl.BlockSpec((n_seq, HG_HEADS, HG_D, HG_D), lambda g, c: (g, 0, 0, 0))],
        out_shape=[jax.ShapeDtypeStruct((n, length, D_HG), F32),
                   jax.ShapeDtypeStruct((n, HG_HEADS, HG_D, HG_D), F32)],
        scratch_shapes=[pltpu.VMEM((n_seq, HG_HEADS, HG_D, HG_D), F32)],
        compiler_params=_cparams(2),
        name="hgrn",
    )(z, s0_t, hg_lb, norm_g)


def _post_mix_kernel(h_ref, up_ref, us_ref, yp_ref, ys_ref, hgp_ref, hgs_ref,
                     d_ref, wglu_ref, bglu_ref, s5g_ref, wout_ref, g1_ref, b1_ref, rwt_ref, rb_ref,
                     h1_ref, idx_ref, gate_ref, rank_ref, cnt_ref, run_scr, *, tm, n_first):
    i = pl.program_id(0)

    @pl.when(i == 0)
    def _():
        run_scr[...] = jnp.zeros_like(run_scr)

    def phase(u_ref, y_ref, hg_ref):
        ys = y_ref[...].astype(F32) + d_ref[...] * u_ref[...].astype(F32)
        gl = 0.5 * ys * (1.0 + lax.erf(ys * (2.0 ** -0.5)))
        s5o = gl * jax.nn.sigmoid(_dot(gl.astype(BF16), wglu_ref[...]) + bglu_ref[...])
        s5o = s5o * lax.rsqrt(jnp.mean(s5o * s5o, axis=-1, keepdims=True) + RMS_EPS) * s5g_ref[...]
        mix = (_dot(s5o.astype(BF16), wout_ref[:D_S5, :])
               + _dot(hg_ref[...].astype(BF16), wout_ref[D_S5:, :]))
        h1 = _layernorm(DEEPNORM_ALPHA * h_ref[...] + mix, g1_ref[...], b1_ref[...])
        _write_row_tiles(h1_ref, h1, tm)

        h_hi, h_mid, _ = _split3(h1)
        w_hi, w_mid, _ = _split3(rwt_ref[...])
        nt = (((1,), (1,)), ((), ()))
        logits = (lax.dot_general(w_hi, h_hi, nt, preferred_element_type=F32)
                  + lax.dot_general(w_hi, h_mid, nt, preferred_element_type=F32)
                  + lax.dot_general(w_mid, h_hi, nt, preferred_element_type=F32)) + rb_ref[...]
        eid = lax.broadcasted_iota(jnp.int32, (N_EXPERTS, tm), 0)
        vals, idxs = [], []
        for _ in range(TOP_K):
            m = jnp.max(logits, axis=0, keepdims=True)
            ix = jnp.min(jnp.where(logits == m, eid, N_EXPERTS), axis=0, keepdims=True)
            vals.append(m)
            idxs.append(ix)
            logits = jnp.where(eid == ix, -jnp.inf, logits)
        exps = [jnp.exp(v - vals[0]) for v in vals]
        den = exps[0] + exps[1] + exps[2] + exps[3]

        onehot = jnp.zeros((N_EXPERTS, tm), F32)
        for ix in idxs:
            onehot = onehot + (eid == ix).astype(F32)
        rowi = lax.broadcasted_iota(jnp.int32, (tm, tm), 0)
        coli = lax.broadcasted_iota(jnp.int32, (tm, tm), 1)
        earlier = (rowi < coli).astype(BF16)
        prefix = _dot(onehot.astype(BF16), earlier) + run_scr[...]
        for k in range(TOP_K):
            idx_ref[k:k + 1, :] = idxs[k]
            gate_ref[k:k + 1, :] = exps[k] / den
            rank_ref[k:k + 1, :] = jnp.sum(jnp.where(eid == idxs[k], prefix, 0.0),
                                           axis=0, keepdims=True).astype(jnp.int32)
        run_scr[...] = run_scr[...] + jnp.sum(onehot, axis=1, keepdims=True)
        cnt_ref[...] = run_scr[...]

    pl.when(i < n_first)(lambda: phase(up_ref, yp_ref, hgp_ref))
    pl.when(i >= n_first)(lambda: phase(us_ref, ys_ref, hgs_ref))


def post_mix(h0, u_pair, y_pair, hg_pair, d_skip, wglu, bglu, s5g, wout, g1, b1, rw_t, rb_col, tm):
    t = h0.shape[0]
    n_first = u_pair[0].shape[0] // tm
    row = lambda i: (i, 0)
    col = lambda i: (0, i)
    fixed = lambda i: (0, 0)
    full = lambda a: pl.BlockSpec(a.shape, fixed)
    weights = (d_skip, wglu, bglu, s5g, wout, g1, b1, rw_t, rb_col)
    return pl.pallas_call(
        functools.partial(_post_mix_kernel, tm=tm, n_first=n_first),
        grid=(t // tm,),
        in_specs=[pl.BlockSpec((tm, D_MODEL), row)]
                 + _two_phase_specs((tm, D_S5), n_first) * 3
                 + [full(a) for a in weights],
        out_specs=[pl.BlockSpec((tm * ROW_TILES, LANES), row),
                   pl.BlockSpec((TOP_K, tm), col), pl.BlockSpec((TOP_K, tm), col),
                   pl.BlockSpec((TOP_K, tm), col), pl.BlockSpec((N_EXPERTS, 1), fixed)],
        out_shape=[jax.ShapeDtypeStruct((t * ROW_TILES, LANES), F32),
                   jax.ShapeDtypeStruct((TOP_K, t), jnp.int32),
                   jax.ShapeDtypeStruct((TOP_K, t), F32),
                   jax.ShapeDtypeStruct((TOP_K, t), jnp.int32),
                   jax.ShapeDtypeStruct((N_EXPERTS, 1), F32)],
        scratch_shapes=[pltpu.VMEM((N_EXPERTS, 1), F32)],
        compiler_params=_cparams(),
        name="post_mix",
    )(h0, *u_pair, *y_pair, *hg_pair, *weights)


def _dispatch_kernel(pend_ref, dest_ref, h_ref, xs_ref, zero_scr, zsem, sem, *, tm):
    n_rows = xs_ref.shape[0] // ROW_TILES

    @pl.when(pl.program_id(0) == 0)
    def _():
        zero_scr[...] = jnp.zeros_like(zero_scr)

        def last_block(e):
            prev = pend_ref[e - 1] if e > 0 else 0
            copy = pltpu.make_async_copy(
                zero_scr, xs_ref.at[pl.ds(pl.multiple_of(jnp.maximum(pend_ref[e] - MOE_ROWS, 0) * ROW_TILES,
                                                         ROW_TILES), MOE_ROWS * ROW_TILES)], zsem)
            return pend_ref[e] > prev, copy

        def tail_block(j):
            row0 = pend_ref[N_EXPERTS - 1] + j * MOE_ROWS
            copy = pltpu.make_async_copy(
                zero_scr, xs_ref.at[pl.ds(pl.multiple_of(jnp.minimum(row0, n_rows - MOE_ROWS) * ROW_TILES,
                                                         ROW_TILES), MOE_ROWS * ROW_TILES)], zsem)
            return row0 < n_rows, copy

        blocks = [last_block(e) for e in range(N_EXPERTS)] + [tail_block(j) for j in range(N_EXPERTS)]
        for used, copy in blocks:
            pl.when(used)(copy.start)
        for used, copy in blocks:
            pl.when(used)(copy.wait)

    def body(t, carry):
        for k in range(TOP_K):
            pltpu.make_async_copy(h_ref.at[_row_tile(t)], xs_ref.at[_row_tile(dest_ref[k * tm + t])],
                                  sem).start(priority=k % 2)
        return carry

    lax.fori_loop(0, tm, body, 0)
    for _ in range(TOP_K):
        pltpu.make_async_copy(h_ref, xs_ref.at[pl.ds(0, tm * ROW_TILES)], sem).wait()


def dispatch(pend, dest_flat, h1, n_rows, tm):
    t = h1.shape[0] // ROW_TILES
    grid_spec = pltpu.PrefetchScalarGridSpec(
        num_scalar_prefetch=1,
        grid=(t // tm,),
        in_specs=[pl.BlockSpec((TOP_K * tm,), lambda i, pend: (i,), memory_space=pltpu.SMEM),
                  pl.BlockSpec((tm * ROW_TILES, LANES), lambda i, pend: (i, 0))],
        out_specs=pl.BlockSpec(memory_space=pl.ANY),
        scratch_shapes=[pltpu.VMEM((MOE_ROWS * ROW_TILES, LANES), F32),
                        pltpu.SemaphoreType.DMA(()), pltpu.SemaphoreType.DMA(())],
    )
    return pl.pallas_call(
        functools.partial(_dispatch_kernel, tm=tm),
        grid_spec=grid_spec,
        out_shape=jax.ShapeDtypeStruct((n_rows * ROW_TILES, LANES), F32),
        compiler_params=_cparams(),
        name="moe_dispatch",
    )(pend, dest_flat, h1)


def _moe_ffn_kernel(be_ref, nu_ref, x_ref, wg_ref, bg_ref, wu_ref, bu_ref, wd_ref, bd_ref, y_ref,
                    wg_scr, wu_scr, wd_scr):
    i = pl.program_id(0)

    @pl.when((i == 0) | (be_ref[i] != be_ref[jnp.maximum(i - 1, 0)]))
    def _():
        wg_scr[...] = wg_ref[0].astype(BF16)
        wu_scr[...] = wu_ref[0].astype(BF16)
        wd_scr[...] = wd_ref[0].astype(BF16)

    @pl.when(i < nu_ref[0])
    def _():
        x = _read_row_tiles(x_ref, MOE_ROWS).astype(BF16)
        gt = jnp.minimum(_dot(x, wg_scr[...]) + bg_ref[0], SWIGLU_LIMIT)
        up = jnp.clip(_dot(x, wu_scr[...]) + bu_ref[0], -SWIGLU_LIMIT, SWIGLU_LIMIT)
        hid = (up + 1.0) * (gt * jax.nn.sigmoid(SWIGLU_ALPHA * gt))
        _write_row_tiles(y_ref, _dot(hid.astype(BF16), wd_scr[...]) + bd_ref[0], MOE_ROWS)

    @pl.when(i >= nu_ref[0])
    def _():
        y_ref[...] = jnp.zeros_like(y_ref)


def moe_ffn(block_e, n_used, xs, wg, bg, wu, bu, wd, bd):
    n_rows = xs.shape[0] // ROW_TILES
    n_blocks = n_rows // MOE_ROWS
    wsel = lambda i, be, nu: (be[i], 0, 0)
    d_ff = wg.shape[-1]
    grid_spec = pltpu.PrefetchScalarGridSpec(
        num_scalar_prefetch=2,
        grid=(n_blocks,),
        in_specs=[pl.BlockSpec((MOE_ROWS * ROW_TILES, LANES),
                               lambda i, be, nu: (jnp.minimum(i, nu[0] - 1), 0)),
                  pl.BlockSpec((1, D_MODEL, d_ff), wsel), pl.BlockSpec((1, 1, d_ff), wsel),
                  pl.BlockSpec((1, D_MODEL, d_ff), wsel), pl.BlockSpec((1, 1, d_ff), wsel),
                  pl.BlockSpec((1, d_ff, D_MODEL), wsel), pl.BlockSpec((1, 1, D_MODEL), wsel)],
        out_specs=pl.BlockSpec((MOE_ROWS * ROW_TILES, LANES), lambda i, be, nu: (i, 0)),
        scratch_shapes=[pltpu.VMEM((D_MODEL, d_ff), BF16), pltpu.VMEM((D_MODEL, d_ff), BF16),
                        pltpu.VMEM((d_ff, D_MODEL), BF16)],
    )
    return pl.pallas_call(
        _moe_ffn_kernel,
        grid_spec=grid_spec,
        out_shape=jax.ShapeDtypeStruct((n_rows * ROW_TILES, LANES), F32),
        compiler_params=_cparams(),
        name="moe_ffn",
    )(block_e, n_used, xs, wg, bg, wu, bu, wd, bd)


def _combine_kernel(dcur_ref, dnext_ref, gate_ref, h_ref, pp_ref, ps_ref, yb_ref,
                    plew_ref, plegw_ref, g2_ref, b2_ref, outp_ref, outs_ref, buf, sem,
                    *, tm, n_first):
    i = pl.program_id(0)
    n = pl.num_programs(0)
    slot = i % 2

    def gather(dest_ref, slot_id):
        def body(t, carry):
            for k in range(TOP_K):
                pltpu.make_async_copy(yb_ref.at[_row_tile(dest_ref[k * tm + t])],
                                      buf.at[slot_id, k, _row_tile(t)],
                                      sem.at[slot_id]).start(priority=k % 2)
            return carry
        lax.fori_loop(0, tm, body, 0)

    pl.when(i == 0)(lambda: gather(dcur_ref, 0))
    pl.when(i + 1 < n)(lambda: gather(dnext_ref, 1 - slot))
    for k in range(TOP_K):
        pltpu.make_async_copy(yb_ref.at[pl.ds(0, tm * ROW_TILES)], buf.at[slot, k], sem.at[slot]).wait()

    gates = gate_ref[...]
    cols = []
    for j in range(ROW_TILES):
        ffn = gates[:, 0:1] * buf[slot, 0, _chunk(tm, j), :]
        for k in range(1, TOP_K):
            ffn = ffn + gates[:, k:k + 1] * buf[slot, k, _chunk(tm, j), :]
        cols.append(DEEPNORM_ALPHA * h_ref[_chunk(tm, j), :] + ffn)
    r = jnp.concatenate(cols, axis=1)
    gate = jax.nn.sigmoid(_dot(r.astype(BF16), plegw_ref[...]))

    def finish(p_ref, out_ref):
        e = _dot(p_ref[...].astype(BF16), plew_ref[...]) * gate
        out_ref[...] = _layernorm(r + e, g2_ref[...], b2_ref[...])

    pl.when(i < n_first)(lambda: finish(pp_ref, outp_ref))
    pl.when(i >= n_first)(lambda: finish(ps_ref, outs_ref))


def combine(dest_flat, gates, h1, p_pair, yb, plew, plegw, g2, b2, tm):
    t = h1.shape[0] // ROW_TILES
    tp, ts = p_pair[0].shape[0], p_pair[1].shape[0]
    n_first = tp // tm
    n_tiles = t // tm
    row = lambda i: (i, 0)
    fixed = lambda i: (0, 0)
    dest_block = lambda f: pl.BlockSpec((TOP_K * tm,), f, memory_space=pltpu.SMEM)
    return pl.pallas_call(
        functools.partial(_combine_kernel, tm=tm, n_first=n_first),
        grid=(n_tiles,),
        in_specs=[dest_block(lambda i: (i,)),
                  dest_block(lambda i: (jnp.minimum(i + 1, n_tiles - 1),)),
                  pl.BlockSpec((tm, TOP_K), row),
                  pl.BlockSpec((tm * ROW_TILES, LANES), row)]
                 + _two_phase_specs((tm, PLE_DIM), n_first)
                 + [pl.BlockSpec(memory_space=pl.ANY),
                    pl.BlockSpec(plew.shape, fixed), pl.BlockSpec(plegw.shape, fixed),
                    pl.BlockSpec((1, D_MODEL), fixed), pl.BlockSpec((1, D_MODEL), fixed)],
        out_specs=_two_phase_specs((tm, D_MODEL), n_first),
        out_shape=[jax.ShapeDtypeStruct((tp, D_MODEL), F32), jax.ShapeDtypeStruct((ts, D_MODEL), F32)],
        scratch_shapes=[pltpu.VMEM((2, TOP_K, tm * ROW_TILES, LANES), F32),
                        pltpu.SemaphoreType.DMA((2,))],
        compiler_params=_cparams(),
        name="moe_combine",
    )(dest_flat, dest_flat, gates, h1, *p_pair, yb, plew, plegw, g2, b2)


def _s5_rows(u_p, u_s, nb, nc, ns):
    up = u_p.reshape(nb, nc, CHUNK, S5_GROUPS, S5_GROUP).transpose(3, 1, 0, 2, 4)
    us = u_s.reshape(ns, CHUNK, S5_GROUPS, S5_GROUP).transpose(2, 0, 1, 3)
    return jnp.concatenate([up.reshape(S5_GROUPS, nc * nb, S5_CONV),
                            us.reshape(S5_GROUPS, ns, S5_CONV)], axis=1)


def _s5_tokens(y_rows, nb, nc, ns):
    yp = y_rows[:, :nc * nb].reshape(S5_GROUPS, nc, nb, CHUNK, S5_GROUP).transpose(2, 1, 3, 0, 4)
    ys = y_rows[:, nc * nb:].reshape(S5_GROUPS, ns, CHUNK, S5_GROUP).transpose(1, 2, 0, 3)
    return yp.reshape(nb * nc * CHUNK, D_S5), ys.reshape(ns * CHUNK, D_S5)


def _row(v):
    return v.reshape(1, -1)


def kernel(x_prompt, x_sample, state_s5_re, state_s5_im, state_hgrn, p_prompt, p_sample, ln_in_g, ln_in_b, w_in, s5_lambda_re, s5_lambda_im, s5_log_step, s5_b_re, s5_b_im, s5_c_re, s5_c_im, s5_d, s5_w_glu, s5_b_glu, s5_norm_g, hg_lb, hg_norm_g, w_out, ln1_g, ln1_b, router_w, router_b, w_gate, b_gate, w_up, b_up, w_down, b_down, ple_w, ple_gate_w, ln2_g, ln2_b):
    nb, seq, _ = x_prompt.shape
    ns, dseq, _ = x_sample.shape
    assert dseq == CHUNK and seq % CHUNK == 0 and w_in.shape[0] == 1
    nc = seq // CHUNK
    tp, ts = nb * seq, ns * dseq
    t = tp + ts
    tm = 512 if (tp % 512 == 0 and ts % 512 == 0) else 256
    tg = 256
    td = 1024 if (tp % 1024 == 0 and ts % 1024 == 0) else 256
    assert tp % tm == 0 and ts % tm == 0 and tp % tg == 0 and ts % tg == 0 and ns % nb == 0

    h0, u_p, u_s, z_p, z_s = ln_in_proj(
        x_prompt.reshape(tp, D_MODEL), x_sample.reshape(ts, D_MODEL),
        _row(ln_in_g), _row(ln_in_b), w_in[0].astype(BF16), tm)

    m, w, wc, a = s5_prep(s5_lambda_re[0], s5_lambda_im[0], s5_log_step[0],
                          s5_b_re[0], s5_b_im[0], s5_c_re[0], s5_c_im[0])
    y_rows, fpr, fpi, fsr, fsi = s5_main(_s5_rows(u_p, u_s, nb, nc, ns), m, w, wc, a,
                                         jnp.swapaxes(state_s5_re[0], 0, 1),
                                         jnp.swapaxes(state_s5_im[0], 0, 1), nb, nc)
    y_pair = _s5_tokens(y_rows, nb, nc, ns)

    zero_state = jnp.zeros((nb, HG_HEADS, HG_D, HG_D), F32)
    ng = _row(hg_norm_g[0])
    o_p, st_p = hgrn(z_p.reshape(nb, seq, 4 * D_HG), zero_state, hg_lb, ng, nb)
    o_s, st_s = hgrn(z_s.reshape(ns, dseq, 4 * D_HG), jnp.swapaxes(state_hgrn[0], 2, 3),
                     hg_lb, ng, nb)

    h1, idx, gates, rank, counts = post_mix(
        h0, (u_p, u_s), y_pair, (o_p.reshape(tp, D_HG), o_s.reshape(ts, D_HG)),
        _row(s5_d[0]), s5_w_glu[0].astype(BF16), _row(s5_b_glu[0]),
        _row(s5_norm_g[0]), w_out[0].astype(BF16), _row(ln1_g[0]), _row(ln1_b[0]),
        router_w[0].T, router_b[0].reshape(N_EXPERTS, 1), tm)

    counts = counts[:, 0].astype(jnp.int32)
    padded = (counts + MOE_ROWS - 1) // MOE_ROWS * MOE_ROWS
    pend = jnp.cumsum(padded)
    pstart = pend - padded
    experts = jnp.arange(N_EXPERTS, dtype=jnp.int32)
    dest = jnp.sum(jnp.where(idx[..., None] == experts, pstart, 0), axis=-1) + rank

    def per_tile(a, tile):
        return a.reshape(TOP_K, t // tile, tile).transpose(1, 0, 2).reshape(-1)

    n_blocks = -(-t * TOP_K // MOE_ROWS) + N_EXPERTS
    n_used = (pend[-1] // MOE_ROWS).astype(jnp.int32)
    blk = jnp.arange(n_blocks, dtype=jnp.int32)
    blk = jnp.minimum(blk, n_used - 1)
    block_e = jnp.sum((pend[None, :] <= (blk * MOE_ROWS)[:, None]).astype(jnp.int32), axis=1)
    block_e = jnp.minimum(block_e, N_EXPERTS - 1)

    xs = dispatch(pend, per_tile(dest, td), h1, n_blocks * MOE_ROWS, td)
    yb = moe_ffn(block_e, n_used.reshape(1), xs,
                 w_gate[0], b_gate[0][:, None, :], w_up[0], b_up[0][:, None, :],
                 w_down[0], b_down[0][:, None, :])
    out_p, out_s = combine(per_tile(dest, tg), gates.T, h1,
                           (p_prompt[0].reshape(tp, PLE_DIM), p_sample[0].reshape(ts, PLE_DIM)),
                           yb, ple_w[0].astype(BF16), ple_gate_w[0].astype(BF16),
                           _row(ln2_g[0]), _row(ln2_b[0]), tg)

    def s5_state(f, n):
        return jnp.swapaxes(f, 0, 1).reshape(1, n, S5_GROUPS, S5_STATE)

    return (out_p.reshape(nb, seq, D_MODEL), out_s.reshape(ns, dseq, D_MODEL),
            s5_state(fpr, nb), s5_state(fpi, nb), jnp.swapaxes(st_p, 2, 3)[None],
            s5_state(fsr, ns), s5_state(fsi, ns), jnp.swapaxes(st_s, 2, 3)[None])
```

```python
import functools

import jax
import jax.numpy as jnp
from jax import lax
from jax.experimental import pallas as pl
from jax.experimental.pallas import tpu as pltpu

F32 = jnp.float32
BF16 = jnp.bfloat16
HIGHEST = lax.Precision.HIGHEST

D_MODEL = 1024
CHUNK = 64
PLE_DIM = 256
D_S5 = 512
S5_GROUP = 16
S5_GROUPS = 32
S5_STATE = 64
D_HG = 512
HG_HEADS = 4
HG_D = 128
D_IN = D_S5 + 4 * D_HG
N_EXPERTS = 32
TOP_K = 4
SWIGLU_LIMIT = 7.0
SWIGLU_ALPHA = 1.702
DEEPNORM_ALPHA = 2.0 ** 0.25
LN_EPS = 1e-5
RMS_EPS = 1e-6

LANES = 128
SUBLANES = 8
ROW_TILES = D_MODEL // LANES
S5_CONV = CHUNK * S5_GROUP
HG_LEVELS = (32, 16, 8, 4, 2, 1)
MOE_ROWS = 512
VMEM_LIMIT = 56 * 1024 * 1024

assert ROW_TILES == SUBLANES


def _cparams(n_axes=1):
    return pltpu.CompilerParams(dimension_semantics=("arbitrary",) * n_axes,
                                vmem_limit_bytes=VMEM_LIMIT)


def _dot(a, b, precision=None):
    return jnp.dot(a, b, preferred_element_type=F32, precision=precision)


def _layernorm(x, g, b):
    mu = jnp.mean(x, axis=-1, keepdims=True)
    xc = x - mu
    var = jnp.mean(xc * xc, axis=-1, keepdims=True)
    return xc * lax.rsqrt(var + LN_EPS) * g + b


def _two_phase_specs(block, n_first):
    nd = len(block)
    first = pl.BlockSpec(block, lambda i: (jnp.minimum(i, n_first - 1),) + (0,) * (nd - 1))
    second = pl.BlockSpec(block, lambda i: (jnp.maximum(i - n_first, 0),) + (0,) * (nd - 1))
    return [first, second]


def _prompt_spec(tm, seq, width, n_first):
    per_seq = seq // tm

    def index(i):
        ic = jnp.minimum(i, n_first - 1)
        return (ic // per_seq, ic % per_seq, 0)

    return pl.BlockSpec((1, tm, width), index)


def _chunk(rows, j):
    return pl.ds(j, rows, stride=ROW_TILES)


def _read_row_tiles(ref, rows):
    return jnp.concatenate([ref[_chunk(rows, j), :] for j in range(ROW_TILES)], axis=1)


def _write_row_tiles(ref, val, rows):
    for j in range(ROW_TILES):
        ref[_chunk(rows, j), :] = val[:, j * LANES:(j + 1) * LANES]


def _row_tiles(r, n=1):
    return pl.ds(pl.multiple_of(r * ROW_TILES, ROW_TILES), n * ROW_TILES)


def _row_tile(r):
    return _row_tiles(r)


def _ln_in_proj_kernel(xp_ref, xs_ref, g_ref, b_ref, w_ref, h_ref, up_ref, us_ref, zp_ref, zs_ref,
                       *, n_first):
    def phase(x, u_ref, z_ref):
        h = _layernorm(x, g_ref[...], b_ref[...])
        h_ref[...] = h
        hb = h.astype(BF16)
        u_ref[...] = _dot(hb, w_ref[:, :D_S5]).astype(BF16)
        z_ref[...] = _dot(hb, w_ref[:, D_S5:])

    i = pl.program_id(0)
    pl.when(i < n_first)(lambda: phase(xp_ref[0], up_ref, zp_ref))
    pl.when(i >= n_first)(lambda: phase(xs_ref[...], us_ref, zs_ref))


def ln_in_proj(xp, xs, g, b, w_bf16, tm):
    nb, seq, _ = xp.shape
    tp, ts = nb * seq, xs.shape[0]
    n_first = tp // tm
    fixed = lambda i: (0, 0)
    return pl.pallas_call(
        functools.partial(_ln_in_proj_kernel, n_first=n_first),
        grid=((tp + ts) // tm,),
        in_specs=[_prompt_spec(tm, seq, D_MODEL, n_first), _two_phase_specs((tm, D_MODEL), n_first)[1]]
                 + [pl.BlockSpec((1, D_MODEL), fixed), pl.BlockSpec((1, D_MODEL), fixed),
                    pl.BlockSpec((D_MODEL, D_IN), fixed)],
        out_specs=[pl.BlockSpec((tm, D_MODEL), lambda i: (i, 0))]
                  + _two_phase_specs((tm, D_S5), n_first)
                  + _two_phase_specs((tm, 4 * D_HG), n_first),
        out_shape=[jax.ShapeDtypeStruct((tp + ts, D_MODEL), F32),
                   jax.ShapeDtypeStruct((tp, D_S5), BF16), jax.ShapeDtypeStruct((ts, D_S5), BF16),
                   jax.ShapeDtypeStruct((tp, 4 * D_HG), F32), jax.ShapeDtypeStruct((ts, 4 * D_HG), F32)],
        compiler_params=_cparams(),
        name="ln_in_proj",
    )(xp, xs, g, b, w_bf16)


def _s5_prep_kernel(lrc_ref, lic_ref, lrr_ref, lir_ref, ls_ref, brt_ref, bit_ref,
                    brtt_ref, bitt_ref, crt_ref, cit_ref,
                    m_ref, w_ref, wc_ref, a_ref):
    step = jnp.exp(ls_ref[0])

    def discretise(lr_raw, li):
        lr = jnp.minimum(lr_raw, -1e-4)
        dr, di = lr * step, li * step
        mag = jnp.exp(dr)
        a_re, a_im = mag * jnp.cos(di), mag * jnp.sin(di)
        den = lr * lr + li * li
        nr = a_re - 1.0
        fr = (nr * lr + a_im * li) / den
        fi = (a_im * lr - nr * li) / den
        return dr, di, fr, fi

    dr_c, di_c, _, _ = discretise(lrc_ref[0], lic_ref[0])
    dr_r, di_r, fr_r, fi_r = discretise(lrr_ref[0], lir_ref[0])

    lane = lax.broadcasted_iota(jnp.int32, (1, S5_CONV), 1)
    lag = (lane // S5_GROUP).astype(F32)

    def c_times_power(tf):
        mag = jnp.exp(dr_c * tf)
        ang = di_c * tf
        pr, pi = mag * jnp.cos(ang), mag * jnp.sin(ang)
        ctr, cti = crt_ref[0], cit_ref[0]
        return ctr * pr - cti * pi, ctr * pi + cti * pr

    cpr, cpi = c_times_power(lag)
    bbr = fr_r * brt_ref[0] - fi_r * bit_ref[0]
    bbi = fr_r * bit_ref[0] + fi_r * brt_ref[0]
    kt = _dot(bbr, cpr, HIGHEST) - _dot(bbi, cpi, HIGHEST)
    for s in range(CHUNK):
        shifted = kt if s == 0 else pltpu.roll(kt, S5_GROUP * s, axis=1)
        m_ref[0, S5_GROUP * s:S5_GROUP * (s + 1), :] = jnp.where(
            lane >= S5_GROUP * s, shifted, 0.0).astype(BF16)

    rowi = lax.broadcasted_iota(jnp.int32, (S5_CONV, 1), 0)
    rem = (CHUNK - 1 - rowi // S5_GROUP).astype(F32)
    magw = jnp.exp(dr_r * rem)
    angw = di_r * rem
    pwr, pwi = magw * jnp.cos(angw), magw * jnp.sin(angw)
    bbtr = fr_r * brtt_ref[0] - fi_r * bitt_ref[0]
    bbti = fr_r * bitt_ref[0] + fi_r * brtt_ref[0]
    w_ref[0, :, :S5_STATE] = pwr * bbtr - pwi * bbti
    w_ref[0, :, S5_STATE:] = pwr * bbti + pwi * bbtr

    c1r, c1i = c_times_power(lag + 1.0)
    wc_ref[0, :S5_STATE, :] = c1r.astype(BF16)
    wc_ref[0, S5_STATE:, :] = (-c1i).astype(BF16)

    full = float(CHUNK)
    mag_c = jnp.exp(dr_r * full)
    a_ref[0, 0:1, :] = mag_c * jnp.cos(di_r * full)
    a_ref[0, 1:2, :] = mag_c * jnp.sin(di_r * full)


def s5_prep(lam_re, lam_im, log_step, b_re, b_im, c_re, c_im):
    g, p = lam_re.shape
    brt = jnp.swapaxes(b_re, 1, 2)
    bit = jnp.swapaxes(b_im, 1, 2)
    crt = jnp.tile(jnp.swapaxes(c_re, 1, 2), (1, 1, CHUNK))
    cit = jnp.tile(jnp.swapaxes(c_im, 1, 2), (1, 1, CHUNK))
    args = (lam_re.reshape(g, p, 1), lam_im.reshape(g, p, 1),
            lam_re.reshape(g, 1, p), lam_im.reshape(g, 1, p), log_step.reshape(g, 1, 1),
            brt, bit, jnp.tile(brt, (1, CHUNK, 1)), jnp.tile(bit, (1, CHUNK, 1)), crt, cit)
    spec = lambda a: pl.BlockSpec((1,) + a.shape[1:], lambda i: (i, 0, 0))
    out_shape = [jax.ShapeDtypeStruct((g, S5_CONV, S5_CONV), BF16),
                 jax.ShapeDtypeStruct((g, S5_CONV, 2 * S5_STATE), F32),
                 jax.ShapeDtypeStruct((g, 2 * S5_STATE, S5_CONV), BF16),
                 jax.ShapeDtypeStruct((g, 2, S5_STATE), F32)]
    return pl.pallas_call(
        _s5_prep_kernel,
        grid=(g,),
        in_specs=[spec(a) for a in args],
        out_specs=[spec(o) for o in out_shape],
        out_shape=out_shape,
        compiler_params=_cparams(),
        name="s5_prep",
    )(*args)


def _split3(w):
    hi = w.astype(BF16)
    r1 = w - hi.astype(F32)
    mid = r1.astype(BF16)
    lo = (r1 - mid.astype(F32)).astype(BF16)
    return hi, mid, lo


def _s5_main_kernel(u_ref, m_ref, w_ref, wc_ref, a_ref, xsr_ref, xsi_ref,
                    y_ref, fpr_ref, fpi_ref, fsr_ref, fsi_ref,
                    hr_scr, hi_scr, x0r_scr, x0i_scr, *, n_prompt, n_chunks):
    u = u_ref[0]
    y_local = _dot(u, m_ref[0])
    whi, wmid, wlo = _split3(w_ref[0])
    hend = _dot(u, whi) + _dot(u, wmid) + _dot(u, wlo)
    hr_scr[...] = hend[:, :S5_STATE]
    hi_scr[...] = hend[:, S5_STATE:]
    ar = a_ref[0, 0:1, :]
    ai = a_ref[0, 1:2, :]
    xr = jnp.zeros((n_prompt, S5_STATE), F32)
    xi = jnp.zeros((n_prompt, S5_STATE), F32)
    for c in range(n_chunks):
        rows = slice(c * n_prompt, (c + 1) * n_prompt)
        x0r_scr[rows, :] = xr
        x0i_scr[rows, :] = xi
        xr, xi = (ar * xr - ai * xi + hr_scr[rows, :],
                  ar * xi + ai * xr + hi_scr[rows, :])
    fpr_ref[0] = xr
    fpi_ref[0] = xi
    base = n_chunks * n_prompt
    sr, si = xsr_ref[0], xsi_ref[0]
    x0r_scr[base:, :] = sr
    x0i_scr[base:, :] = si
    fsr_ref[0] = ar * sr - ai * si + hr_scr[base:, :]
    fsi_ref[0] = ar * si + ai * sr + hi_scr[base:, :]
    y = (y_local + _dot(x0r_scr[...].astype(BF16), wc_ref[0, :S5_STATE, :])
         + _dot(x0i_scr[...].astype(BF16), wc_ref[0, S5_STATE:, :]))
    y_ref[0] = y.astype(BF16)


def s5_main(u_rows, m, w, wc, a, xs_re, xs_im, n_prompt, n_chunks):
    g, r, _ = u_rows.shape
    n_sample = xs_re.shape[1]
    spec = lambda shape: pl.BlockSpec((1,) + tuple(shape[1:]), lambda i: (i, 0, 0))
    args = (u_rows, m, w, wc, a, xs_re, xs_im)
    out_shape = [jax.ShapeDtypeStruct((g, r, S5_CONV), BF16),
                 jax.ShapeDtypeStruct((g, n_prompt, S5_STATE), F32),
                 jax.ShapeDtypeStruct((g, n_prompt, S5_STATE), F32),
                 jax.ShapeDtypeStruct((g, n_sample, S5_STATE), F32),
                 jax.ShapeDtypeStruct((g, n_sample, S5_STATE), F32)]
    return pl.pallas_call(
        functools.partial(_s5_main_kernel, n_prompt=n_prompt, n_chunks=n_chunks),
        grid=(g,),
        in_specs=[spec(x.shape) for x in args],
        out_specs=[spec(o.shape) for o in out_shape],
        out_shape=out_shape,
        scratch_shapes=[pltpu.VMEM((r, S5_STATE), F32)] * 4,
        compiler_params=_cparams(),
        name="s5_main",
    )(*args)


def _hgrn_kernel(z_ref, s0_ref, lb_ref, ng_ref, o_ref, sfin_ref, st_scr, *, n_seq):
    c = pl.program_id(1)

    @pl.when(c == 0)
    def _():
        st_scr[...] = s0_ref[...]

    lbw = lb_ref[...]
    lbe = jnp.exp(lbw - jnp.max(lbw, axis=0, keepdims=True))
    lb_all = lbe[0:1, :] / jnp.sum(lbe, axis=0, keepdims=True)

    rowi = lax.broadcasted_iota(jnp.int32, (CHUNK, CHUNK), 0)
    coli = lax.broadcasted_iota(jnp.int32, (CHUNK, CHUNK), 1)
    rowk = lax.broadcasted_iota(jnp.int32, (CHUNK, HG_D), 0)
    cum_rows = [(coli <= rowi).astype(F32)]
    upper, lower, same_block = [], [], []
    for m in HG_LEVELS:
        ref_row = (rowi // (2 * m)) * (2 * m) + (m - 1)
        cum_rows.append((coli <= ref_row).astype(F32))
        upper.append((rowk % (2 * m)) >= m)
        lower.append((rowk % (2 * m)) < m)
        same_block.append((rowi // (2 * m)) == (coli // (2 * m)))
    cum_mat = jnp.concatenate(cum_rows, axis=0).astype(BF16)
    cum_mat3 = jnp.concatenate([cum_mat] * 3, axis=1)
    diag = rowi == coli
    nt = (((1,), (1,)), ((), ()))

    def body(n, carry):
        zf = z_ref[n, :, D_HG:2 * D_HG]
        fg_all = lb_all + (1.0 - lb_all) * jax.nn.sigmoid(zf)
        cums = _dot(cum_mat3, jnp.concatenate(_split3(jnp.log(fg_all)), axis=0))
        for hd in range(HG_HEADS):
            cols = slice(hd * HG_D, (hd + 1) * HG_D)
            zq = z_ref[n, :, hd * HG_D:(hd + 1) * HG_D]
            v = z_ref[n, :, 2 * D_HG + hd * HG_D:2 * D_HG + (hd + 1) * HG_D]
            zg = z_ref[n, :, 3 * D_HG + hd * HG_D:3 * D_HG + (hd + 1) * HG_D]
            q = zq * jax.nn.sigmoid(zq)
            kk = 1.0 - fg_all[:, cols]
            bcum = cums[:CHUNK, cols]
            b_last = bcum[CHUNK - 1:CHUNK, :]
            vb = v.astype(BF16)
            st = st_scr[n, hd]

            scores = jnp.where(diag, lax.dot_general(q.astype(BF16), kk.astype(BF16), nt,
                                                     preferred_element_type=F32), 0.0)
            for lvl in range(len(HG_LEVELS)):
                bref = cums[(lvl + 1) * CHUNK:(lvl + 2) * CHUNK, cols]
                dec = jnp.exp(-jnp.abs(bcum - bref))
                qs = jnp.where(upper[lvl], q * dec, 0.0)
                ks = jnp.where(lower[lvl], kk * dec, 0.0)
                sc = lax.dot_general(qs.astype(BF16), ks.astype(BF16), nt, preferred_element_type=F32)
                scores = scores + jnp.where(same_block[lvl], sc, 0.0)

            qd = (q * jnp.exp(bcum)).astype(BF16)
            o = lax.dot_general(qd, st.astype(BF16), nt, preferred_element_type=F32)
            o = o + _dot(scores.astype(BF16), vb)
            kdec = (kk * jnp.exp(b_last - bcum)).astype(BF16)
            st_scr[n, hd] = jnp.exp(b_last) * st + lax.dot_general(
                vb, kdec, (((0,), (0,)), ((), ())), preferred_element_type=F32)

            on = o * lax.rsqrt(jnp.mean(o * o, axis=-1, keepdims=True) + RMS_EPS) * ng_ref[:, cols]
            o_ref[n, :, hd * HG_D:(hd + 1) * HG_D] = on * (zg * jax.nn.sigmoid(zg))
        return carry

    lax.fori_loop(0, n_seq, body, 0, unroll=True)

    @pl.when(c == pl.num_programs(1) - 1)
    def _():
        sfin_ref[...] = st_scr[...]


def hgrn(z, s0_t, hg_lb, norm_g, n_seq):
    n, length, _ = z.shape
    return pl.pallas_call(
        functools.partial(_hgrn_kernel, n_seq=n_seq),
        grid=(n // n_seq, length // CHUNK),
        in_specs=[pl.BlockSpec((n_seq, CHUNK, 4 * D_HG), lambda g, c: (g, c, 0)),
                  pl.BlockSpec((n_seq, HG_HEADS, HG_D, HG_D), lambda g, c: (g, 0, 0, 0)),
                  pl.BlockSpec(hg_lb.shape, lambda g, c: (0, 0)),
                  pl.BlockSpec((1, D_HG), lambda g, c: (0, 0))],
        out_specs=[pl.BlockSpec((n_seq, CHUNK, D_HG), lambda g, c: (g, c, 0)),
                   pl.BlockSpec((n_seq, HG_HEADS, HG_D, HG_D), lambda g, c: (g, 0, 0, 0))],
        out_shape=[jax.ShapeDtypeStruct((n, length, D_HG), F32),
                   jax.ShapeDtypeStruct((n, HG_HEADS, HG_D, HG_D), F32)],
        scratch_shapes=[pltpu.VMEM((n_seq, HG_HEADS, HG_D, HG_D), F32)],
        compiler_params=_cparams(2),
        name="hgrn",
    )(z, s0_t, hg_lb, norm_g)


def _post_mix_kernel(h_ref, up_ref, us_ref, yp_ref, ys_ref, hgp_ref, hgs_ref,
                     d_ref, wglu_ref, bglu_ref, s5g_ref, wout_ref, g1_ref, b1_ref, rwt_ref, rb_ref,
                     h1_ref, idx_ref, gate_ref, rank_ref, before_ref, cnt_ref, run_scr, *, tm, n_first):
    i = pl.program_id(0)

    @pl.when(i == 0)
    def _():
        run_scr[...] = jnp.zeros_like(run_scr)

    def phase(u_ref, y_ref, hg_ref):
        ys = y_ref[...].astype(F32) + d_ref[...] * u_ref[...].astype(F32)
        gl = 0.5 * ys * (1.0 + lax.erf(ys * (2.0 ** -0.5)))
        s5o = gl * jax.nn.sigmoid(_dot(gl.astype(BF16), wglu_ref[...]) + bglu_ref[...])
        s5o = s5o * lax.rsqrt(jnp.mean(s5o * s5o, axis=-1, keepdims=True) + RMS_EPS) * s5g_ref[...]
        mix = (_dot(s5o.astype(BF16), wout_ref[:D_S5, :])
               + _dot(hg_ref[...].astype(BF16), wout_ref[D_S5:, :]))
        h1 = _layernorm(DEEPNORM_ALPHA * h_ref[...] + mix, g1_ref[...], b1_ref[...])
        _write_row_tiles(h1_ref, h1, tm)

        h_hi, h_mid, _ = _split3(h1)
        w_hi, w_mid, _ = _split3(rwt_ref[...])
        nt = (((1,), (1,)), ((), ()))
        logits = (lax.dot_general(w_hi, h_hi, nt, preferred_element_type=F32)
                  + lax.dot_general(w_hi, h_mid, nt, preferred_element_type=F32)
                  + lax.dot_general(w_mid, h_hi, nt, preferred_element_type=F32)) + rb_ref[...]
        eid = lax.broadcasted_iota(jnp.int32, (N_EXPERTS, tm), 0)
        vals, idxs = [], []
        for _ in range(TOP_K):
            m = jnp.max(logits, axis=0, keepdims=True)
            ix = jnp.min(jnp.where(logits == m, eid, N_EXPERTS), axis=0, keepdims=True)
            vals.append(m)
            idxs.append(ix)
            logits = jnp.where(eid == ix, -jnp.inf, logits)
        exps = [jnp.exp(v - vals[0]) for v in vals]
        den = exps[0] + exps[1] + exps[2] + exps[3]

        onehot = jnp.zeros((N_EXPERTS, tm), F32)
        for ix in idxs:
            onehot = onehot + (eid == ix).astype(F32)
        rowi = lax.broadcasted_iota(jnp.int32, (tm, tm), 0)
        coli = lax.broadcasted_iota(jnp.int32, (tm, tm), 1)
        earlier = (rowi < coli).astype(BF16)
        prefix = _dot(onehot.astype(BF16), earlier)
        for k in range(TOP_K):
            idx_ref[k:k + 1, :] = idxs[k]
            gate_ref[k:k + 1, :] = exps[k] / den
            rank_ref[k:k + 1, :] = jnp.sum(jnp.where(eid == idxs[k], prefix, 0.0),
                                           axis=0, keepdims=True).astype(jnp.int32)
        before_ref[0] = run_scr[...]
        run_scr[...] = run_scr[...] + jnp.sum(onehot, axis=1, keepdims=True)
        cnt_ref[...] = run_scr[...]

    pl.when(i < n_first)(lambda: phase(up_ref, yp_ref, hgp_ref))
    pl.when(i >= n_first)(lambda: phase(us_ref, ys_ref, hgs_ref))


def post_mix(h0, u_pair, y_pair, hg_pair, d_skip, wglu, bglu, s5g, wout, g1, b1, rw_t, rb_col, tm):
    t = h0.shape[0]
    n_first = u_pair[0].shape[0] // tm
    row = lambda i: (i, 0)
    col = lambda i: (0, i)
    fixed = lambda i: (0, 0)
    full = lambda a: pl.BlockSpec(a.shape, fixed)
    weights = (d_skip, wglu, bglu, s5g, wout, g1, b1, rw_t, rb_col)
    return pl.pallas_call(
        functools.partial(_post_mix_kernel, tm=tm, n_first=n_first),
        grid=(t // tm,),
        in_specs=[pl.BlockSpec((tm, D_MODEL), row)]
                 + _two_phase_specs((tm, D_S5), n_first) * 3
                 + [full(a) for a in weights],
        out_specs=[pl.BlockSpec((tm * ROW_TILES, LANES), row),
                   pl.BlockSpec((TOP_K, tm), col), pl.BlockSpec((TOP_K, tm), col),
                   pl.BlockSpec((TOP_K, tm), col),
                   pl.BlockSpec((1, N_EXPERTS, 1), lambda i: (i, 0, 0)),
                   pl.BlockSpec((N_EXPERTS, 1), fixed)],
        out_shape=[jax.ShapeDtypeStruct((t * ROW_TILES, LANES), F32),
                   jax.ShapeDtypeStruct((TOP_K, t), jnp.int32),
                   jax.ShapeDtypeStruct((TOP_K, t), F32),
                   jax.ShapeDtypeStruct((TOP_K, t), jnp.int32),
                   jax.ShapeDtypeStruct((t // tm, N_EXPERTS, 1), F32),
                   jax.ShapeDtypeStruct((N_EXPERTS, 1), F32)],
        scratch_shapes=[pltpu.VMEM((N_EXPERTS, 1), F32)],
        compiler_params=_cparams(),
        name="post_mix",
    )(h0, *u_pair, *y_pair, *hg_pair, *weights)


def _segment_copies(meta_ref, tile, tm, make_copy):
    for e in range(N_EXPERTS):
        sorted_row = meta_ref[tile, e]
        cnt = meta_ref[tile, N_EXPERTS + e]
        staged_row = meta_ref[tile, 2 * N_EXPERTS + e]
        for b in range(tm.bit_length()):
            done = cnt & ((1 << b) - 1)

            @pl.when(((cnt >> b) & 1) == 1)
            def _(b=b, done=done, e=e):
                make_copy(staged_row + done, sorted_row + done, 1 << b).start(priority=e % 2)


def _dispatch_kernel(meta_ref, pend_ref, slot_ref, h_ref, xs_ref, stage, zero_scr, zsem, sem, *, tm):
    n_rows = xs_ref.shape[0] // ROW_TILES
    i = pl.program_id(0)
    n = pl.num_programs(0)
    slot = i % 2

    def drain(s):
        pltpu.make_async_copy(stage.at[s], xs_ref.at[pl.ds(0, TOP_K * tm * ROW_TILES)], sem.at[s]).wait()

    @pl.when(i == 0)
    def _():
        zero_scr[...] = jnp.zeros_like(zero_scr)

        def last_block(e):
            prev = pend_ref[e - 1] if e > 0 else 0
            copy = pltpu.make_async_copy(
                zero_scr, xs_ref.at[pl.ds(pl.multiple_of(jnp.maximum(pend_ref[e] - MOE_ROWS, 0) * ROW_TILES,
                                                         ROW_TILES), MOE_ROWS * ROW_TILES)], zsem)
            return pend_ref[e] > prev, copy

        def tail_block(j):
            row0 = pend_ref[N_EXPERTS - 1] + j * MOE_ROWS
            copy = pltpu.make_async_copy(
                zero_scr, xs_ref.at[pl.ds(pl.multiple_of(jnp.minimum(row0, n_rows - MOE_ROWS) * ROW_TILES,
                                                         ROW_TILES), MOE_ROWS * ROW_TILES)], zsem)
            return row0 < n_rows, copy

        blocks = [last_block(e) for e in range(N_EXPERTS)] + [tail_block(j) for j in range(N_EXPERTS)]
        for used, copy in blocks:
            pl.when(used)(copy.start)
        for used, copy in blocks:
            pl.when(used)(copy.wait)

    pl.when(i >= 2)(lambda: drain(slot))

    def body(t, carry):
        row = h_ref[_row_tile(t), :]
        for k in range(TOP_K):
            stage[slot, _row_tile(slot_ref[k * tm + t]), :] = row
        return carry

    lax.fori_loop(0, tm, body, 0)
    _segment_copies(meta_ref, i, tm, lambda staged_row, sorted_row, rows: pltpu.make_async_copy(
        stage.at[slot, _row_tiles(staged_row, rows)], xs_ref.at[_row_tiles(sorted_row, rows)], sem.at[slot]))

    @pl.when(i == n - 1)
    def _():
        pl.when(n >= 2)(lambda: drain(1 - slot))
        drain(slot)


def dispatch(meta, pend, slots_flat, h1, n_rows, tm):
    t = h1.shape[0] // ROW_TILES
    grid_spec = pltpu.PrefetchScalarGridSpec(
        num_scalar_prefetch=2,
        grid=(t // tm,),
        in_specs=[pl.BlockSpec((TOP_K * tm,), lambda i, meta, pend: (i,), memory_space=pltpu.SMEM),
                  pl.BlockSpec((tm * ROW_TILES, LANES), lambda i, meta, pend: (i, 0))],
        out_specs=pl.BlockSpec(memory_space=pl.ANY),
        scratch_shapes=[pltpu.VMEM((2, TOP_K * tm * ROW_TILES, LANES), F32),
                        pltpu.VMEM((MOE_ROWS * ROW_TILES, LANES), F32),
                        pltpu.SemaphoreType.DMA(()), pltpu.SemaphoreType.DMA((2,))],
    )
    return pl.pallas_call(
        functools.partial(_dispatch_kernel, tm=tm),
        grid_spec=grid_spec,
        out_shape=jax.ShapeDtypeStruct((n_rows * ROW_TILES, LANES), F32),
        compiler_params=_cparams(),
        name="moe_dispatch",
    )(meta, pend, slots_flat, h1)


def _moe_ffn_kernel(be_ref, nu_ref, x_ref, wg_ref, bg_ref, wu_ref, bu_ref, wd_ref, bd_ref, y_ref,
                    wg_scr, wu_scr, wd_scr):
    i = pl.program_id(0)

    @pl.when((i == 0) | (be_ref[i] != be_ref[jnp.maximum(i - 1, 0)]))
    def _():
        wg_scr[...] = wg_ref[0].astype(BF16)
        wu_scr[...] = wu_ref[0].astype(BF16)
        wd_scr[...] = wd_ref[0].astype(BF16)

    @pl.when(i < nu_ref[0])
    def _():
        x = _read_row_tiles(x_ref, MOE_ROWS).astype(BF16)
        gt = jnp.minimum(_dot(x, wg_scr[...]) + bg_ref[0], SWIGLU_LIMIT)
        up = jnp.clip(_dot(x, wu_scr[...]) + bu_ref[0], -SWIGLU_LIMIT, SWIGLU_LIMIT)
        hid = (up + 1.0) * (gt * jax.nn.sigmoid(SWIGLU_ALPHA * gt))
        _write_row_tiles(y_ref, _dot(hid.astype(BF16), wd_scr[...]) + bd_ref[0], MOE_ROWS)

    @pl.when(i >= nu_ref[0])
    def _():
        y_ref[...] = jnp.zeros_like(y_ref)


def moe_ffn(block_e, n_used, xs, wg, bg, wu, bu, wd, bd):
    n_rows = xs.shape[0] // ROW_TILES
    n_blocks = n_rows // MOE_ROWS
    wsel = lambda i, be, nu: (be[i], 0, 0)
    d_ff = wg.shape[-1]
    grid_spec = pltpu.PrefetchScalarGridSpec(
        num_scalar_prefetch=2,
        grid=(n_blocks,),
        in_specs=[pl.BlockSpec((MOE_ROWS * ROW_TILES, LANES),
                               lambda i, be, nu: (jnp.minimum(i, nu[0] - 1), 0)),
                  pl.BlockSpec((1, D_MODEL, d_ff), wsel), pl.BlockSpec((1, 1, d_ff), wsel),
                  pl.BlockSpec((1, D_MODEL, d_ff), wsel), pl.BlockSpec((1, 1, d_ff), wsel),
                  pl.BlockSpec((1, d_ff, D_MODEL), wsel), pl.BlockSpec((1, 1, D_MODEL), wsel)],
        out_specs=pl.BlockSpec((MOE_ROWS * ROW_TILES, LANES), lambda i, be, nu: (i, 0)),
        scratch_shapes=[pltpu.VMEM((D_MODEL, d_ff), BF16), pltpu.VMEM((D_MODEL, d_ff), BF16),
                        pltpu.VMEM((d_ff, D_MODEL), BF16)],
    )
    return pl.pallas_call(
        _moe_ffn_kernel,
        grid_spec=grid_spec,
        out_shape=jax.ShapeDtypeStruct((n_rows * ROW_TILES, LANES), F32),
        compiler_params=_cparams(),
        name="moe_ffn",
    )(block_e, n_used, xs, wg, bg, wu, bu, wd, bd)


def _combine_kernel(meta_ref, slot_ref, gate_ref, h_ref, pp_ref, ps_ref, yb_ref,
                    plew_ref, plegw_ref, g2_ref, b2_ref, outp_ref, outs_ref, buf, r_scr, sem,
                    *, tm, n_first):
    i = pl.program_id(0)
    n = pl.num_programs(0)
    slot = i % 2

    def fetch(tile, s):
        _segment_copies(meta_ref, tile, tm, lambda staged_row, sorted_row, rows: pltpu.make_async_copy(
            yb_ref.at[_row_tiles(sorted_row, rows)], buf.at[s, _row_tiles(staged_row, rows)], sem.at[s]))

    pl.when(i == 0)(lambda: fetch(0, 0))
    pl.when(i + 1 < n)(lambda: fetch(i + 1, 1 - slot))
    pltpu.make_async_copy(yb_ref.at[pl.ds(0, TOP_K * tm * ROW_TILES)], buf.at[slot], sem.at[slot]).wait()

    def body(t, carry):
        acc = DEEPNORM_ALPHA * h_ref[_row_tile(t), :]
        for k in range(TOP_K):
            acc = acc + gate_ref[k * tm + t] * buf[slot, _row_tile(slot_ref[k * tm + t]), :]
        r_scr[_row_tile(t), :] = acc
        return carry

    lax.fori_loop(0, tm, body, 0)
    r = _read_row_tiles(r_scr, tm)
    gate = jax.nn.sigmoid(_dot(r.astype(BF16), plegw_ref[...]))

    def finish(p, store):
        e = _dot(p.astype(BF16), plew_ref[...]) * gate
        store(_layernorm(r + e, g2_ref[...], b2_ref[...]))

    def store_prompt(v):
        outp_ref[0] = v

    def store_sample(v):
        outs_ref[...] = v

    pl.when(i < n_first)(lambda: finish(pp_ref[0], store_prompt))
    pl.when(i >= n_first)(lambda: finish(ps_ref[...], store_sample))


def combine(meta, slots_flat, gates_flat, h1, p_prompt, p_sample, yb, plew, plegw, g2, b2, tm):
    t = h1.shape[0] // ROW_TILES
    nb, seq, _ = p_prompt.shape
    ts = p_sample.shape[0]
    n_first = nb * seq // tm
    fixed = lambda i, meta: (0, 0)
    flat = pl.BlockSpec((TOP_K * tm,), lambda i, meta: (i,), memory_space=pltpu.SMEM)
    sample = lambda width: pl.BlockSpec((tm, width), lambda i, meta: (jnp.maximum(i - n_first, 0), 0))
    prompt = lambda width: pl.BlockSpec((1, tm, width), _prompt_spec(tm, seq, width, n_first).index_map)
    with_meta = lambda spec: pl.BlockSpec(spec.block_shape, lambda i, meta: spec.index_map(i))
    grid_spec = pltpu.PrefetchScalarGridSpec(
        num_scalar_prefetch=1,
        grid=(t // tm,),
        in_specs=[flat, flat,
                  pl.BlockSpec((tm * ROW_TILES, LANES), lambda i, meta: (i, 0)),
                  with_meta(prompt(PLE_DIM)), sample(PLE_DIM),
                  pl.BlockSpec(memory_space=pl.ANY),
                  pl.BlockSpec(plew.shape, fixed), pl.BlockSpec(plegw.shape, fixed),
                  pl.BlockSpec((1, D_MODEL), fixed), pl.BlockSpec((1, D_MODEL), fixed)],
        out_specs=[with_meta(prompt(D_MODEL)), sample(D_MODEL)],
        scratch_shapes=[pltpu.VMEM((2, TOP_K * tm * ROW_TILES, LANES), F32),
                        pltpu.VMEM((tm * ROW_TILES, LANES), F32),
                        pltpu.SemaphoreType.DMA((2,))],
    )
    return pl.pallas_call(
        functools.partial(_combine_kernel, tm=tm, n_first=n_first),
        grid_spec=grid_spec,
        out_shape=[jax.ShapeDtypeStruct((nb, seq, D_MODEL), F32), jax.ShapeDtypeStruct((ts, D_MODEL), F32)],
        compiler_params=_cparams(),
        name="moe_combine",
    )(meta, slots_flat, gates_flat, h1, p_prompt, p_sample, yb, plew, plegw, g2, b2)


def _s5_rows(u_p, u_s, nb, nc, ns):
    up = u_p.reshape(nb, nc, CHUNK, S5_GROUPS, S5_GROUP).transpose(3, 1, 0, 2, 4)
    us = u_s.reshape(ns, CHUNK, S5_GROUPS, S5_GROUP).transpose(2, 0, 1, 3)
    return jnp.concatenate([up.reshape(S5_GROUPS, nc * nb, S5_CONV),
                            us.reshape(S5_GROUPS, ns, S5_CONV)], axis=1)


def _s5_tokens(y_rows, nb, nc, ns):
    yp = y_rows[:, :nc * nb].reshape(S5_GROUPS, nc, nb, CHUNK, S5_GROUP).transpose(2, 1, 3, 0, 4)
    ys = y_rows[:, nc * nb:].reshape(S5_GROUPS, ns, CHUNK, S5_GROUP).transpose(1, 2, 0, 3)
    return yp.reshape(nb * nc * CHUNK, D_S5), ys.reshape(ns * CHUNK, D_S5)


def _row(v):
    return v.reshape(1, -1)


def kernel(x_prompt, x_sample, state_s5_re, state_s5_im, state_hgrn, p_prompt, p_sample, ln_in_g, ln_in_b, w_in, s5_lambda_re, s5_lambda_im, s5_log_step, s5_b_re, s5_b_im, s5_c_re, s5_c_im, s5_d, s5_w_glu, s5_b_glu, s5_norm_g, hg_lb, hg_norm_g, w_out, ln1_g, ln1_b, router_w, router_b, w_gate, b_gate, w_up, b_up, w_down, b_down, ple_w, ple_gate_w, ln2_g, ln2_b):
    nb, seq, _ = x_prompt.shape
    ns, dseq, _ = x_sample.shape
    assert dseq == CHUNK and seq % CHUNK == 0 and w_in.shape[0] == 1
    nc = seq // CHUNK
    tp, ts = nb * seq, ns * dseq
    t = tp + ts
    tm = 512 if (tp % 512 == 0 and ts % 512 == 0) else 256
    assert tp % tm == 0 and ts % tm == 0 and seq % tm == 0 and ns % nb == 0

    h0, u_p, u_s, z_p, z_s = ln_in_proj(
        x_prompt, x_sample.reshape(ts, D_MODEL),
        _row(ln_in_g), _row(ln_in_b), w_in[0].astype(BF16), tm)

    m, w, wc, a = s5_prep(s5_lambda_re[0], s5_lambda_im[0], s5_log_step[0],
                          s5_b_re[0], s5_b_im[0], s5_c_re[0], s5_c_im[0])
    y_rows, fpr, fpi, fsr, fsi = s5_main(_s5_rows(u_p, u_s, nb, nc, ns), m, w, wc, a,
                                         jnp.swapaxes(state_s5_re[0], 0, 1),
                                         jnp.swapaxes(state_s5_im[0], 0, 1), nb, nc)
    y_pair = _s5_tokens(y_rows, nb, nc, ns)

    zero_state = jnp.zeros((nb, HG_HEADS, HG_D, HG_D), F32)
    ng = _row(hg_norm_g[0])
    o_p, st_p = hgrn(z_p.reshape(nb, seq, 4 * D_HG), zero_state, hg_lb, ng, nb)
    o_s, st_s = hgrn(z_s.reshape(ns, dseq, 4 * D_HG), jnp.swapaxes(state_hgrn[0], 2, 3),
                     hg_lb, ng, nb)

    h1, idx, gates, rank, before, counts = post_mix(
        h0, (u_p, u_s), y_pair, (o_p.reshape(tp, D_HG), o_s.reshape(ts, D_HG)),
        _row(s5_d[0]), s5_w_glu[0].astype(BF16), _row(s5_b_glu[0]),
        _row(s5_norm_g[0]), w_out[0].astype(BF16), _row(ln1_g[0]), _row(ln1_b[0]),
        router_w[0].T, router_b[0].reshape(N_EXPERTS, 1), tm)

    n_tiles = t // tm
    counts = counts[:, 0].astype(jnp.int32)
    before = before[:, :, 0].astype(jnp.int32)
    cnt = jnp.concatenate([before[1:], counts[None]], axis=0) - before
    padded = (counts + MOE_ROWS - 1) // MOE_ROWS * MOE_ROWS
    pend = jnp.cumsum(padded)
    staged = jnp.cumsum(cnt, axis=1) - cnt
    meta = jnp.concatenate([pend - padded + before, cnt, staged, jnp.zeros_like(cnt)], axis=1)
    experts = jnp.arange(N_EXPERTS, dtype=jnp.int32)
    staged_tok = jnp.repeat(staged, tm, axis=0)
    slots = jnp.sum(jnp.where(idx[..., None] == experts, staged_tok, 0), axis=-1) + rank

    def per_tile(a):
        return a.reshape(TOP_K, n_tiles, tm).transpose(1, 0, 2).reshape(-1)

    n_blocks = -(-t * TOP_K // MOE_ROWS) + N_EXPERTS
    n_used = (pend[-1] // MOE_ROWS).astype(jnp.int32)
    blk = jnp.arange(n_blocks, dtype=jnp.int32)
    blk = jnp.minimum(blk, n_used - 1)
    block_e = jnp.sum((pend[None, :] <= (blk * MOE_ROWS)[:, None]).astype(jnp.int32), axis=1)
    block_e = jnp.minimum(block_e, N_EXPERTS - 1)

    slots_flat = per_tile(slots)
    xs = dispatch(meta, pend, slots_flat, h1, n_blocks * MOE_ROWS, tm)
    yb = moe_ffn(block_e, n_used.reshape(1), xs,
                 w_gate[0], b_gate[0][:, None, :], w_up[0], b_up[0][:, None, :],
                 w_down[0], b_down[0][:, None, :])
    out_p, out_s = combine(meta, slots_flat, per_tile(gates), h1, p_prompt[0], p_sample[0].reshape(ts, PLE_DIM),
                           yb, ple_w[0].astype(BF16), ple_gate_w[0].astype(BF16),
                           _row(ln2_g[0]), _row(ln2_b[0]), tm)

    def s5_state(f, n):
        return jnp.swapaxes(f, 0, 1).reshape(1, n, S5_GROUPS, S5_STATE)

    return (out_p, out_s.reshape(ns, dseq, D_MODEL),
            s5_state(fpr, nb), s5_state(fpi, nb), jnp.swapaxes(st_p, 2, 3)[None],
            s5_state(fsr, ns), s5_state(fsi, ns), jnp.swapaxes(st_s, 2, 3)[None])
```

```python
import functools

import jax
import jax.numpy as jnp
from jax import lax
from jax.experimental import pallas as pl
from jax.experimental.pallas import tpu as pltpu

F32 = jnp.float32
BF16 = jnp.bfloat16
HIGHEST = lax.Precision.HIGHEST

D_MODEL = 1024
CHUNK = 64
PLE_DIM = 256
D_S5 = 512
S5_GROUP = 16
S5_GROUPS = 32
S5_STATE = 64
D_HG = 512
HG_HEADS = 4
HG_D = 128
D_IN = D_S5 + 4 * D_HG
N_EXPERTS = 32
TOP_K = 4
SWIGLU_LIMIT = 7.0
SWIGLU_ALPHA = 1.702
DEEPNORM_ALPHA = 2.0 ** 0.25
LN_EPS = 1e-5
RMS_EPS = 1e-6

LANES = 128
SUBLANES = 8
ROW_TILES = D_MODEL // LANES
S5_CONV = CHUNK * S5_GROUP
HG_LEVELS = (32, 16, 8, 4, 2, 1)
MOE_ROWS = 512
VMEM_LIMIT = 56 * 1024 * 1024

assert ROW_TILES == SUBLANES


def _cparams(n_axes=1):
    return pltpu.CompilerParams(dimension_semantics=("arbitrary",) * n_axes,
                                vmem_limit_bytes=VMEM_LIMIT)


def _dot(a, b, precision=None):
    return jnp.dot(a, b, preferred_element_type=F32, precision=precision)


def _layernorm(x, g, b):
    mu = jnp.mean(x, axis=-1, keepdims=True)
    xc = x - mu
    var = jnp.mean(xc * xc, axis=-1, keepdims=True)
    return xc * lax.rsqrt(var + LN_EPS) * g + b


def _two_phase_specs(block, n_first):
    nd = len(block)
    first = pl.BlockSpec(block, lambda i: (jnp.minimum(i, n_first - 1),) + (0,) * (nd - 1))
    second = pl.BlockSpec(block, lambda i: (jnp.maximum(i - n_first, 0),) + (0,) * (nd - 1))
    return [first, second]


def _prompt_spec(tm, seq, width, n_first):
    per_seq = seq // tm

    def index(i):
        ic = jnp.minimum(i, n_first - 1)
        return (ic // per_seq, ic % per_seq, 0)

    return pl.BlockSpec((1, tm, width), index)


def _chunk(rows, j):
    return pl.ds(j, rows, stride=ROW_TILES)


def _read_row_tiles(ref, rows):
    return jnp.concatenate([ref[_chunk(rows, j), :] for j in range(ROW_TILES)], axis=1)


def _write_row_tiles(ref, val, rows):
    for j in range(ROW_TILES):
        ref[_chunk(rows, j), :] = val[:, j * LANES:(j + 1) * LANES]


def _row_tiles(r, n=1):
    return pl.ds(pl.multiple_of(r * ROW_TILES, ROW_TILES), n * ROW_TILES)


def _row_tile(r):
    return _row_tiles(r)


def _ln_in_proj_kernel(xp_ref, xs_ref, g_ref, b_ref, w_ref, h_ref, up_ref, us_ref, zp_ref, zs_ref,
                       *, n_first):
    def phase(x, u_ref, z_ref):
        h = _layernorm(x, g_ref[...], b_ref[...])
        h_ref[...] = h
        hb = h.astype(BF16)
        u_ref[...] = _dot(hb, w_ref[:, :D_S5]).astype(BF16)
        z_ref[...] = _dot(hb, w_ref[:, D_S5:])

    i = pl.program_id(0)
    pl.when(i < n_first)(lambda: phase(xp_ref[0], up_ref, zp_ref))
    pl.when(i >= n_first)(lambda: phase(xs_ref[...], us_ref, zs_ref))


def ln_in_proj(xp, xs, g, b, w_bf16, tm):
    nb, seq, _ = xp.shape
    tp, ts = nb * seq, xs.shape[0]
    n_first = tp // tm
    fixed = lambda i: (0, 0)
    return pl.pallas_call(
        functools.partial(_ln_in_proj_kernel, n_first=n_first),
        grid=((tp + ts) // tm,),
        in_specs=[_prompt_spec(tm, seq, D_MODEL, n_first), _two_phase_specs((tm, D_MODEL), n_first)[1]]
                 + [pl.BlockSpec((1, D_MODEL), fixed), pl.BlockSpec((1, D_MODEL), fixed),
                    pl.BlockSpec((D_MODEL, D_IN), fixed)],
        out_specs=[pl.BlockSpec((tm, D_MODEL), lambda i: (i, 0))]
                  + _two_phase_specs((tm, D_S5), n_first)
                  + _two_phase_specs((tm, 4 * D_HG), n_first),
        out_shape=[jax.ShapeDtypeStruct((tp + ts, D_MODEL), F32),
                   jax.ShapeDtypeStruct((tp, D_S5), BF16), jax.ShapeDtypeStruct((ts, D_S5), BF16),
                   jax.ShapeDtypeStruct((tp, 4 * D_HG), F32), jax.ShapeDtypeStruct((ts, 4 * D_HG), F32)],
        compiler_params=_cparams(),
        name="ln_in_proj",
    )(xp, xs, g, b, w_bf16)


def _s5_prep_kernel(lrc_ref, lic_ref, lrr_ref, lir_ref, ls_ref, brt_ref, bit_ref,
                    brtt_ref, bitt_ref, crt_ref, cit_ref,
                    m_ref, w_ref, wc_ref, a_ref):
    step = jnp.exp(ls_ref[0])

    def discretise(lr_raw, li):
        lr = jnp.minimum(lr_raw, -1e-4)
        dr, di = lr * step, li * step
        mag = jnp.exp(dr)
        a_re, a_im = mag * jnp.cos(di), mag * jnp.sin(di)
        den = lr * lr + li * li
        nr = a_re - 1.0
        fr = (nr * lr + a_im * li) / den
        fi = (a_im * lr - nr * li) / den
        return dr, di, fr, fi

    dr_c, di_c, _, _ = discretise(lrc_ref[0], lic_ref[0])
    dr_r, di_r, fr_r, fi_r = discretise(lrr_ref[0], lir_ref[0])

    lane = lax.broadcasted_iota(jnp.int32, (1, S5_CONV), 1)
    lag = (lane // S5_GROUP).astype(F32)

    def c_times_power(tf):
        mag = jnp.exp(dr_c * tf)
        ang = di_c * tf
        pr, pi = mag * jnp.cos(ang), mag * jnp.sin(ang)
        ctr, cti = crt_ref[0], cit_ref[0]
        return ctr * pr - cti * pi, ctr * pi + cti * pr

    cpr, cpi = c_times_power(lag)
    bbr = fr_r * brt_ref[0] - fi_r * bit_ref[0]
    bbi = fr_r * bit_ref[0] + fi_r * brt_ref[0]
    kt = _dot(bbr, cpr, HIGHEST) - _dot(bbi, cpi, HIGHEST)
    for s in range(CHUNK):
        shifted = kt if s == 0 else pltpu.roll(kt, S5_GROUP * s, axis=1)
        m_ref[0, S5_GROUP * s:S5_GROUP * (s + 1), :] = jnp.where(
            lane >= S5_GROUP * s, shifted, 0.0).astype(BF16)

    rowi = lax.broadcasted_iota(jnp.int32, (S5_CONV, 1), 0)
    rem = (CHUNK - 1 - rowi // S5_GROUP).astype(F32)
    magw = jnp.exp(dr_r * rem)
    angw = di_r * rem
    pwr, pwi = magw * jnp.cos(angw), magw * jnp.sin(angw)
    bbtr = fr_r * brtt_ref[0] - fi_r * bitt_ref[0]
    bbti = fr_r * bitt_ref[0] + fi_r * brtt_ref[0]
    w_ref[0, :, :S5_STATE] = pwr * bbtr - pwi * bbti
    w_ref[0, :, S5_STATE:] = pwr * bbti + pwi * bbtr

    c1r, c1i = c_times_power(lag + 1.0)
    wc_ref[0, :S5_STATE, :] = c1r.astype(BF16)
    wc_ref[0, S5_STATE:, :] = (-c1i).astype(BF16)

    full = float(CHUNK)
    mag_c = jnp.exp(dr_r * full)
    a_ref[0, 0:1, :] = mag_c * jnp.cos(di_r * full)
    a_ref[0, 1:2, :] = mag_c * jnp.sin(di_r * full)


def s5_prep(lam_re, lam_im, log_step, b_re, b_im, c_re, c_im):
    g, p = lam_re.shape
    brt = jnp.swapaxes(b_re, 1, 2)
    bit = jnp.swapaxes(b_im, 1, 2)
    crt = jnp.tile(jnp.swapaxes(c_re, 1, 2), (1, 1, CHUNK))
    cit = jnp.tile(jnp.swapaxes(c_im, 1, 2), (1, 1, CHUNK))
    args = (lam_re.reshape(g, p, 1), lam_im.reshape(g, p, 1),
            lam_re.reshape(g, 1, p), lam_im.reshape(g, 1, p), log_step.reshape(g, 1, 1),
            brt, bit, jnp.tile(brt, (1, CHUNK, 1)), jnp.tile(bit, (1, CHUNK, 1)), crt, cit)
    spec = lambda a: pl.BlockSpec((1,) + a.shape[1:], lambda i: (i, 0, 0))
    out_shape = [jax.ShapeDtypeStruct((g, S5_CONV, S5_CONV), BF16),
                 jax.ShapeDtypeStruct((g, S5_CONV, 2 * S5_STATE), F32),
                 jax.ShapeDtypeStruct((g, 2 * S5_STATE, S5_CONV), BF16),
                 jax.ShapeDtypeStruct((g, 2, S5_STATE), F32)]
    return pl.pallas_call(
        _s5_prep_kernel,
        grid=(g,),
        in_specs=[spec(a) for a in args],
        out_specs=[spec(o) for o in out_shape],
        out_shape=out_shape,
        compiler_params=_cparams(),
        name="s5_prep",
    )(*args)


def _split3(w):
    hi = w.astype(BF16)
    r1 = w - hi.astype(F32)
    mid = r1.astype(BF16)
    lo = (r1 - mid.astype(F32)).astype(BF16)
    return hi, mid, lo


def _s5_main_kernel(u_ref, m_ref, w_ref, wc_ref, a_ref, xsr_ref, xsi_ref,
                    y_ref, fpr_ref, fpi_ref, fsr_ref, fsi_ref,
                    hr_scr, hi_scr, x0r_scr, x0i_scr, *, n_prompt, n_chunks):
    u = u_ref[0]
    y_local = _dot(u, m_ref[0])
    whi, wmid, wlo = _split3(w_ref[0])
    hend = _dot(u, whi) + _dot(u, wmid) + _dot(u, wlo)
    hr_scr[...] = hend[:, :S5_STATE]
    hi_scr[...] = hend[:, S5_STATE:]
    ar = a_ref[0, 0:1, :]
    ai = a_ref[0, 1:2, :]
    xr = jnp.zeros((n_prompt, S5_STATE), F32)
    xi = jnp.zeros((n_prompt, S5_STATE), F32)
    for c in range(n_chunks):
        rows = slice(c * n_prompt, (c + 1) * n_prompt)
        x0r_scr[rows, :] = xr
        x0i_scr[rows, :] = xi
        xr, xi = (ar * xr - ai * xi + hr_scr[rows, :],
                  ar * xi + ai * xr + hi_scr[rows, :])
    fpr_ref[0] = xr
    fpi_ref[0] = xi
    base = n_chunks * n_prompt
    sr, si = xsr_ref[0], xsi_ref[0]
    x0r_scr[base:, :] = sr
    x0i_scr[base:, :] = si
    fsr_ref[0] = ar * sr - ai * si + hr_scr[base:, :]
    fsi_ref[0] = ar * si + ai * sr + hi_scr[base:, :]
    y = (y_local + _dot(x0r_scr[...].astype(BF16), wc_ref[0, :S5_STATE, :])
         + _dot(x0i_scr[...].astype(BF16), wc_ref[0, S5_STATE:, :]))
    y_ref[0] = y.astype(BF16)


def s5_main(u_rows, m, w, wc, a, xs_re, xs_im, n_prompt, n_chunks):
    g, r, _ = u_rows.shape
    n_sample = xs_re.shape[1]
    spec = lambda shape: pl.BlockSpec((1,) + tuple(shape[1:]), lambda i: (i, 0, 0))
    args = (u_rows, m, w, wc, a, xs_re, xs_im)
    out_shape = [jax.ShapeDtypeStruct((g, r, S5_CONV), BF16),
                 jax.ShapeDtypeStruct((g, n_prompt, S5_STATE), F32),
                 jax.ShapeDtypeStruct((g, n_prompt, S5_STATE), F32),
                 jax.ShapeDtypeStruct((g, n_sample, S5_STATE), F32),
                 jax.ShapeDtypeStruct((g, n_sample, S5_STATE), F32)]
    return pl.pallas_call(
        functools.partial(_s5_main_kernel, n_prompt=n_prompt, n_chunks=n_chunks),
        grid=(g,),
        in_specs=[spec(x.shape) for x in args],
        out_specs=[spec(o.shape) for o in out_shape],
        out_shape=out_shape,
        scratch_shapes=[pltpu.VMEM((r, S5_STATE), F32)] * 4,
        compiler_params=_cparams(),
        name="s5_main",
    )(*args)


def _hgrn_kernel(z_ref, s0_ref, lb_ref, ng_ref, o_ref, sfin_ref, st_scr, *, n_seq):
    c = pl.program_id(1)

    @pl.when(c == 0)
    def _():
        st_scr[...] = s0_ref[...]

    lbw = lb_ref[...]
    lbe = jnp.exp(lbw - jnp.max(lbw, axis=0, keepdims=True))
    lb_all = lbe[0:1, :] / jnp.sum(lbe, axis=0, keepdims=True)

    rowi = lax.broadcasted_iota(jnp.int32, (CHUNK, CHUNK), 0)
    coli = lax.broadcasted_iota(jnp.int32, (CHUNK, CHUNK), 1)
    rowk = lax.broadcasted_iota(jnp.int32, (CHUNK, HG_D), 0)
    cum_rows = [(coli <= rowi).astype(F32)]
    upper, lower, same_block = [], [], []
    for m in HG_LEVELS:
        ref_row = (rowi // (2 * m)) * (2 * m) + (m - 1)
        cum_rows.append((coli <= ref_row).astype(F32))
        upper.append((rowk % (2 * m)) >= m)
        lower.append((rowk % (2 * m)) < m)
        same_block.append((rowi // (2 * m)) == (coli // (2 * m)))
    cum_mat = jnp.concatenate(cum_rows, axis=0).astype(BF16)
    cum_mat3 = jnp.concatenate([cum_mat] * 3, axis=1)
    diag = rowi == coli
    nt = (((1,), (1,)), ((), ()))

    def body(n, carry):
        zf = z_ref[n, :, D_HG:2 * D_HG]
        fg_all = lb_all + (1.0 - lb_all) * jax.nn.sigmoid(zf)
        cums = _dot(cum_mat3, jnp.concatenate(_split3(jnp.log(fg_all)), axis=0))
        for hd in range(HG_HEADS):
            cols = slice(hd * HG_D, (hd + 1) * HG_D)
            zq = z_ref[n, :, hd * HG_D:(hd + 1) * HG_D]
            v = z_ref[n, :, 2 * D_HG + hd * HG_D:2 * D_HG + (hd + 1) * HG_D]
            zg = z_ref[n, :, 3 * D_HG + hd * HG_D:3 * D_HG + (hd + 1) * HG_D]
            q = zq * jax.nn.sigmoid(zq)
            kk = 1.0 - fg_all[:, cols]
            bcum = cums[:CHUNK, cols]
            b_last = bcum[CHUNK - 1:CHUNK, :]
            vb = v.astype(BF16)
            st = st_scr[n, hd]

            scores = jnp.where(diag, lax.dot_general(q.astype(BF16), kk.astype(BF16), nt,
                                                     preferred_element_type=F32), 0.0)
            for lvl in range(len(HG_LEVELS)):
                bref = cums[(lvl + 1) * CHUNK:(lvl + 2) * CHUNK, cols]
                dec = jnp.exp(-jnp.abs(bcum - bref))
                qs = jnp.where(upper[lvl], q * dec, 0.0)
                ks = jnp.where(lower[lvl], kk * dec, 0.0)
                sc = lax.dot_general(qs.astype(BF16), ks.astype(BF16), nt, preferred_element_type=F32)
                scores = scores + jnp.where(same_block[lvl], sc, 0.0)

            qd = (q * jnp.exp(bcum)).astype(BF16)
            o = lax.dot_general(qd, st.astype(BF16), nt, preferred_element_type=F32)
            o = o + _dot(scores.astype(BF16), vb)
            kdec = (kk * jnp.exp(b_last - bcum)).astype(BF16)
            st_scr[n, hd] = jnp.exp(b_last) * st + lax.dot_general(
                vb, kdec, (((0,), (0,)), ((), ())), preferred_element_type=F32)

            on = o * lax.rsqrt(jnp.mean(o * o, axis=-1, keepdims=True) + RMS_EPS) * ng_ref[:, cols]
            o_ref[n, :, hd * HG_D:(hd + 1) * HG_D] = on * (zg * jax.nn.sigmoid(zg))
        return carry

    lax.fori_loop(0, n_seq, body, 0, unroll=True)

    @pl.when(c == pl.num_programs(1) - 1)
    def _():
        sfin_ref[...] = st_scr[...]


def hgrn(z, s0_t, hg_lb, norm_g, n_seq):
    n, length, _ = z.shape
    return pl.pallas_call(
        functools.partial(_hgrn_kernel, n_seq=n_seq),
        grid=(n // n_seq, length // CHUNK),
        in_specs=[pl.BlockSpec((n_seq, CHUNK, 4 * D_HG), lambda g, c: (g, c, 0)),
                  pl.BlockSpec((n_seq, HG_HEADS, HG_D, HG_D), lambda g, c: (g, 0, 0, 0)),
                  pl.BlockSpec(hg_lb.shape, lambda g, c: (0, 0)),
                  pl.BlockSpec((1, D_HG), lambda g, c: (0, 0))],
        out_specs=[pl.BlockSpec((n_seq, CHUNK, D_HG), lambda g, c: (g, c, 0)),
                   pl.BlockSpec((n_seq, HG_HEADS, HG_D, HG_D), lambda g, c: (g, 0, 0, 0))],
        out_shape=[jax.ShapeDtypeStruct((n, length, D_HG), F32),
                   jax.ShapeDtypeStruct((n, HG_HEADS, HG_D, HG_D), F32)],
        scratch_shapes=[pltpu.VMEM((n_seq, HG_HEADS, HG_D, HG_D), F32)],
        compiler_params=_cparams(2),
        name="hgrn",
    )(z, s0_t, hg_lb, norm_g)


def _post_mix_kernel(h_ref, up_ref, us_ref, yp_ref, ys_ref, hgp_ref, hgs_ref,
                     d_ref, wglu_ref, bglu_ref, s5g_ref, wout_ref, g1_ref, b1_ref, rwt_ref, rb_ref,
                     h1_ref, idx_ref, gate_ref, rank_ref, before_ref, cnt_ref, run_scr, *, tm, n_first):
    i = pl.program_id(0)

    @pl.when(i == 0)
    def _():
        run_scr[...] = jnp.zeros_like(run_scr)

    def phase(u_ref, y_ref, hg_ref):
        ys = y_ref[...].astype(F32) + d_ref[...] * u_ref[...].astype(F32)
        gl = 0.5 * ys * (1.0 + lax.erf(ys * (2.0 ** -0.5)))
        s5o = gl * jax.nn.sigmoid(_dot(gl.astype(BF16), wglu_ref[...]) + bglu_ref[...])
        s5o = s5o * lax.rsqrt(jnp.mean(s5o * s5o, axis=-1, keepdims=True) + RMS_EPS) * s5g_ref[...]
        mix = (_dot(s5o.astype(BF16), wout_ref[:D_S5, :])
               + _dot(hg_ref[...].astype(BF16), wout_ref[D_S5:, :]))
        h1 = _layernorm(DEEPNORM_ALPHA * h_ref[...] + mix, g1_ref[...], b1_ref[...])
        _write_row_tiles(h1_ref, h1, tm)

        h_hi, h_mid, _ = _split3(h1)
        w_hi, w_mid, _ = _split3(rwt_ref[...])
        nt = (((1,), (1,)), ((), ()))
        logits = (lax.dot_general(w_hi, h_hi, nt, preferred_element_type=F32)
                  + lax.dot_general(w_hi, h_mid, nt, preferred_element_type=F32)
                  + lax.dot_general(w_mid, h_hi, nt, preferred_element_type=F32)) + rb_ref[...]
        eid = lax.broadcasted_iota(jnp.int32, (N_EXPERTS, tm), 0)
        vals, idxs = [], []
        for _ in range(TOP_K):
            m = jnp.max(logits, axis=0, keepdims=True)
            ix = jnp.min(jnp.where(logits == m, eid, N_EXPERTS), axis=0, keepdims=True)
            vals.append(m)
            idxs.append(ix)
            logits = jnp.where(eid == ix, -jnp.inf, logits)
        exps = [jnp.exp(v - vals[0]) for v in vals]
        den = exps[0] + exps[1] + exps[2] + exps[3]

        onehot = jnp.zeros((N_EXPERTS, tm), F32)
        for ix in idxs:
            onehot = onehot + (eid == ix).astype(F32)
        rowi = lax.broadcasted_iota(jnp.int32, (tm, tm), 0)
        coli = lax.broadcasted_iota(jnp.int32, (tm, tm), 1)
        earlier = (rowi < coli).astype(BF16)
        prefix = _dot(onehot.astype(BF16), earlier)
        for k in range(TOP_K):
            idx_ref[k:k + 1, :] = idxs[k]
            gate_ref[k:k + 1, :] = exps[k] / den
            rank_ref[k:k + 1, :] = jnp.sum(jnp.where(eid == idxs[k], prefix, 0.0),
                                           axis=0, keepdims=True).astype(jnp.int32)
        before_ref[0] = run_scr[...]
        run_scr[...] = run_scr[...] + jnp.sum(onehot, axis=1, keepdims=True)
        cnt_ref[...] = run_scr[...]

    pl.when(i < n_first)(lambda: phase(up_ref, yp_ref, hgp_ref))
    pl.when(i >= n_first)(lambda: phase(us_ref, ys_ref, hgs_ref))


def post_mix(h0, u_pair, y_pair, hg_pair, d_skip, wglu, bglu, s5g, wout, g1, b1, rw_t, rb_col, tm):
    t = h0.shape[0]
    n_first = u_pair[0].shape[0] // tm
    row = lambda i: (i, 0)
    col = lambda i: (0, i)
    fixed = lambda i: (0, 0)
    full = lambda a: pl.BlockSpec(a.shape, fixed)
    weights = (d_skip, wglu, bglu, s5g, wout, g1, b1, rw_t, rb_col)
    return pl.pallas_call(
        functools.partial(_post_mix_kernel, tm=tm, n_first=n_first),
        grid=(t // tm,),
        in_specs=[pl.BlockSpec((tm, D_MODEL), row)]
                 + _two_phase_specs((tm, D_S5), n_first) * 3
                 + [full(a) for a in weights],
        out_specs=[pl.BlockSpec((tm * ROW_TILES, LANES), row),
                   pl.BlockSpec((TOP_K, tm), col), pl.BlockSpec((TOP_K, tm), col),
                   pl.BlockSpec((TOP_K, tm), col),
                   pl.BlockSpec((1, N_EXPERTS, 1), lambda i: (i, 0, 0)),
                   pl.BlockSpec((N_EXPERTS, 1), fixed)],
        out_shape=[jax.ShapeDtypeStruct((t * ROW_TILES, LANES), F32),
                   jax.ShapeDtypeStruct((TOP_K, t), jnp.int32),
                   jax.ShapeDtypeStruct((TOP_K, t), F32),
                   jax.ShapeDtypeStruct((TOP_K, t), jnp.int32),
                   jax.ShapeDtypeStruct((t // tm, N_EXPERTS, 1), F32),
                   jax.ShapeDtypeStruct((N_EXPERTS, 1), F32)],
        scratch_shapes=[pltpu.VMEM((N_EXPERTS, 1), F32)],
        compiler_params=_cparams(),
        name="post_mix",
    )(h0, *u_pair, *y_pair, *hg_pair, *weights)


def _segment_copies(meta_ref, tile, tm, make_copy):
    for e in range(N_EXPERTS):
        sorted_row = meta_ref[tile, e]
        cnt = meta_ref[tile, N_EXPERTS + e]
        staged_row = meta_ref[tile, 2 * N_EXPERTS + e]
        for b in range(tm.bit_length()):
            done = cnt & ((1 << b) - 1)

            @pl.when(((cnt >> b) & 1) == 1)
            def _(b=b, done=done, e=e):
                make_copy(staged_row + done, sorted_row + done, 1 << b).start(priority=e % 2)


def _dispatch_kernel(meta_ref, pend_ref, slot_ref, h_ref, xs_ref, stage, zero_scr, zsem, sem, *, tm):
    n_rows = xs_ref.shape[0] // ROW_TILES
    i = pl.program_id(0)
    n = pl.num_programs(0)
    slot = i % 2

    def drain(s):
        pltpu.make_async_copy(stage.at[s], xs_ref.at[pl.ds(0, TOP_K * tm * ROW_TILES)], sem.at[s]).wait()

    @pl.when(i == 0)
    def _():
        zero_scr[...] = jnp.zeros_like(zero_scr)

        def last_block(e):
            prev = pend_ref[e - 1] if e > 0 else 0
            copy = pltpu.make_async_copy(
                zero_scr, xs_ref.at[pl.ds(pl.multiple_of(jnp.maximum(pend_ref[e] - MOE_ROWS, 0) * ROW_TILES,
                                                         ROW_TILES), MOE_ROWS * ROW_TILES)], zsem)
            return pend_ref[e] > prev, copy

        def tail_block(j):
            row0 = pend_ref[N_EXPERTS - 1] + j * MOE_ROWS
            copy = pltpu.make_async_copy(
                zero_scr, xs_ref.at[pl.ds(pl.multiple_of(jnp.minimum(row0, n_rows - MOE_ROWS) * ROW_TILES,
                                                         ROW_TILES), MOE_ROWS * ROW_TILES)], zsem)
            return row0 < n_rows, copy

        blocks = [last_block(e) for e in range(N_EXPERTS)] + [tail_block(j) for j in range(N_EXPERTS)]
        for used, copy in blocks:
            pl.when(used)(copy.start)
        for used, copy in blocks:
            pl.when(used)(copy.wait)

    pl.when(i >= 2)(lambda: drain(slot))

    def body(t, carry):
        row = h_ref[_row_tile(t), :]
        for k in range(TOP_K):
            stage[slot, _row_tile(slot_ref[k * tm + t]), :] = row
        return carry

    lax.fori_loop(0, tm, body, 0, unroll=4)
    _segment_copies(meta_ref, i, tm, lambda staged_row, sorted_row, rows: pltpu.make_async_copy(
        stage.at[slot, _row_tiles(staged_row, rows)], xs_ref.at[_row_tiles(sorted_row, rows)], sem.at[slot]))

    @pl.when(i == n - 1)
    def _():
        pl.when(n >= 2)(lambda: drain(1 - slot))
        drain(slot)


def dispatch(meta, pend, slots_flat, h1, n_rows, tm):
    t = h1.shape[0] // ROW_TILES
    grid_spec = pltpu.PrefetchScalarGridSpec(
        num_scalar_prefetch=2,
        grid=(t // tm,),
        in_specs=[pl.BlockSpec((TOP_K * tm,), lambda i, meta, pend: (i,), memory_space=pltpu.SMEM),
                  pl.BlockSpec((tm * ROW_TILES, LANES), lambda i, meta, pend: (i, 0))],
        out_specs=pl.BlockSpec(memory_space=pl.ANY),
        scratch_shapes=[pltpu.VMEM((2, TOP_K * tm * ROW_TILES, LANES), F32),
                        pltpu.VMEM((MOE_ROWS * ROW_TILES, LANES), F32),
                        pltpu.SemaphoreType.DMA(()), pltpu.SemaphoreType.DMA((2,))],
    )
    return pl.pallas_call(
        functools.partial(_dispatch_kernel, tm=tm),
        grid_spec=grid_spec,
        out_shape=jax.ShapeDtypeStruct((n_rows * ROW_TILES, LANES), F32),
        compiler_params=_cparams(),
        name="moe_dispatch",
    )(meta, pend, slots_flat, h1)


def _moe_ffn_kernel(be_ref, nu_ref, seg_ref, nxt_ref, x_ref, wg_ref, bg_ref, wu_ref, bu_ref, wd_ref, bd_ref,
                    y_ref, wbuf, wbf, sem):
    i = pl.program_id(0)
    hbm = (wg_ref, wu_ref, wd_ref)

    def weight_copies(expert, s):
        return [pltpu.make_async_copy(hbm[j].at[expert], wbuf.at[s, j], sem.at[s, j]) for j in range(3)]

    @pl.when((i == 0) | (be_ref[i] != be_ref[jnp.maximum(i - 1, 0)]))
    def _():
        s = seg_ref[i] % 2

        @pl.when(i == 0)
        def _():
            for c in weight_copies(be_ref[0], 0):
                c.start()

        for j, c in enumerate(weight_copies(be_ref[i], s)):
            c.wait()
            wbf[j] = wbuf[s, j].astype(BF16)

        @pl.when(nxt_ref[i] >= 0)
        def _():
            for c in weight_copies(nxt_ref[i], 1 - s):
                c.start()

    @pl.when(i < nu_ref[0])
    def _():
        x = _read_row_tiles(x_ref, MOE_ROWS).astype(BF16)
        gt = jnp.minimum(_dot(x, wbf[0]) + bg_ref[0], SWIGLU_LIMIT)
        up = jnp.clip(_dot(x, wbf[1]) + bu_ref[0], -SWIGLU_LIMIT, SWIGLU_LIMIT)
        hid = (up + 1.0) * (gt * jax.nn.sigmoid(SWIGLU_ALPHA * gt))
        _write_row_tiles(y_ref, _dot(hid.astype(BF16), wbf[2]) + bd_ref[0], MOE_ROWS)

    @pl.when(i >= nu_ref[0])
    def _():
        y_ref[...] = jnp.zeros_like(y_ref)


def moe_ffn(block_e, n_used, segment, next_e, xs, wg, bg, wu, bu, wd, bd):
    n_rows = xs.shape[0] // ROW_TILES
    n_blocks = n_rows // MOE_ROWS
    wsel = lambda i, be, nu, seg, nxt: (be[i], 0, 0)
    d_ff = wg.shape[-1]
    assert wg.shape[1:] == wu.shape[1:] == wd.shape[1:] == (D_MODEL, D_MODEL)
    anywhere = pl.BlockSpec(memory_space=pl.ANY)
    grid_spec = pltpu.PrefetchScalarGridSpec(
        num_scalar_prefetch=4,
        grid=(n_blocks,),
        in_specs=[pl.BlockSpec((MOE_ROWS * ROW_TILES, LANES),
                               lambda i, be, nu, seg, nxt: (jnp.minimum(i, nu[0] - 1), 0)),
                  anywhere, pl.BlockSpec((1, 1, d_ff), wsel),
                  anywhere, pl.BlockSpec((1, 1, d_ff), wsel),
                  anywhere, pl.BlockSpec((1, 1, D_MODEL), wsel)],
        out_specs=pl.BlockSpec((MOE_ROWS * ROW_TILES, LANES), lambda i, be, nu, seg, nxt: (i, 0)),
        scratch_shapes=[pltpu.VMEM((2, 3, D_MODEL, D_MODEL), F32), pltpu.VMEM((3, D_MODEL, D_MODEL), BF16),
                        pltpu.SemaphoreType.DMA((2, 3))],
    )
    return pl.pallas_call(
        _moe_ffn_kernel,
        grid_spec=grid_spec,
        out_shape=jax.ShapeDtypeStruct((n_rows * ROW_TILES, LANES), F32),
        compiler_params=_cparams(),
        name="moe_ffn",
    )(block_e, n_used, segment, next_e, xs, wg, bg, wu, bu, wd, bd)


def _combine_kernel(meta_ref, slot_ref, gate_ref, h_ref, pp_ref, ps_ref, yb_ref,
                    plew_ref, plegw_ref, g2_ref, b2_ref, outp_ref, outs_ref, buf, r_scr, sem,
                    *, tm, n_first):
    i = pl.program_id(0)
    n = pl.num_programs(0)
    slot = i % 2

    def fetch(tile, s):
        _segment_copies(meta_ref, tile, tm, lambda staged_row, sorted_row, rows: pltpu.make_async_copy(
            yb_ref.at[_row_tiles(sorted_row, rows)], buf.at[s, _row_tiles(staged_row, rows)], sem.at[s]))

    pl.when(i == 0)(lambda: fetch(0, 0))
    pl.when(i + 1 < n)(lambda: fetch(i + 1, 1 - slot))
    pltpu.make_async_copy(yb_ref.at[pl.ds(0, TOP_K * tm * ROW_TILES)], buf.at[slot], sem.at[slot]).wait()

    def body(t, carry):
        acc = DEEPNORM_ALPHA * h_ref[_row_tile(t), :]
        for k in range(TOP_K):
            acc = acc + gate_ref[k * tm + t] * buf[slot, _row_tile(slot_ref[k * tm + t]), :]
        r_scr[_row_tile(t), :] = acc
        return carry

    lax.fori_loop(0, tm, body, 0, unroll=4)
    r = _read_row_tiles(r_scr, tm)
    gate = jax.nn.sigmoid(_dot(r.astype(BF16), plegw_ref[...]))

    def finish(p, store):
        e = _dot(p.astype(BF16), plew_ref[...]) * gate
        store(_layernorm(r + e, g2_ref[...], b2_ref[...]))

    def store_prompt(v):
        outp_ref[0] = v

    def store_sample(v):
        outs_ref[...] = v

    pl.when(i < n_first)(lambda: finish(pp_ref[0], store_prompt))
    pl.when(i >= n_first)(lambda: finish(ps_ref[...], store_sample))


def combine(meta, slots_flat, gates_flat, h1, p_prompt, p_sample, yb, plew, plegw, g2, b2, tm):
    t = h1.shape[0] // ROW_TILES
    nb, seq, _ = p_prompt.shape
    ts = p_sample.shape[0]
    n_first = nb * seq // tm
    fixed = lambda i, meta: (0, 0)
    flat = pl.BlockSpec((TOP_K * tm,), lambda i, meta: (i,), memory_space=pltpu.SMEM)
    sample = lambda width: pl.BlockSpec((tm, width), lambda i, meta: (jnp.maximum(i - n_first, 0), 0))
    prompt = lambda width: pl.BlockSpec((1, tm, width), _prompt_spec(tm, seq, width, n_first).index_map)
    with_meta = lambda spec: pl.BlockSpec(spec.block_shape, lambda i, meta: spec.index_map(i))
    grid_spec = pltpu.PrefetchScalarGridSpec(
        num_scalar_prefetch=1,
        grid=(t // tm,),
        in_specs=[flat, flat,
                  pl.BlockSpec((tm * ROW_TILES, LANES), lambda i, meta: (i, 0)),
                  with_meta(prompt(PLE_DIM)), sample(PLE_DIM),
                  pl.BlockSpec(memory_space=pl.ANY),
                  pl.BlockSpec(plew.shape, fixed), pl.BlockSpec(plegw.shape, fixed),
                  pl.BlockSpec((1, D_MODEL), fixed), pl.BlockSpec((1, D_MODEL), fixed)],
        out_specs=[with_meta(prompt(D_MODEL)), sample(D_MODEL)],
        scratch_shapes=[pltpu.VMEM((2, TOP_K * tm * ROW_TILES, LANES), F32),
                        pltpu.VMEM((tm * ROW_TILES, LANES), F32),
                        pltpu.SemaphoreType.DMA((2,))],
    )
    return pl.pallas_call(
        functools.partial(_combine_kernel, tm=tm, n_first=n_first),
        grid_spec=grid_spec,
        out_shape=[jax.ShapeDtypeStruct((nb, seq, D_MODEL), F32), jax.ShapeDtypeStruct((ts, D_MODEL), F32)],
        compiler_params=_cparams(),
        name="moe_combine",
    )(meta, slots_flat, gates_flat, h1, p_prompt, p_sample, yb, plew, plegw, g2, b2)


def _s5_rows(u_p, u_s, nb, nc, ns):
    up = u_p.reshape(nb, nc, CHUNK, S5_GROUPS, S5_GROUP).transpose(3, 1, 0, 2, 4)
    us = u_s.reshape(ns, CHUNK, S5_GROUPS, S5_GROUP).transpose(2, 0, 1, 3)
    return jnp.concatenate([up.reshape(S5_GROUPS, nc * nb, S5_CONV),
                            us.reshape(S5_GROUPS, ns, S5_CONV)], axis=1)


def _s5_tokens(y_rows, nb, nc, ns):
    yp = y_rows[:, :nc * nb].reshape(S5_GROUPS, nc, nb, CHUNK, S5_GROUP).transpose(2, 1, 3, 0, 4)
    ys = y_rows[:, nc * nb:].reshape(S5_GROUPS, ns, CHUNK, S5_GROUP).transpose(1, 2, 0, 3)
    return yp.reshape(nb * nc * CHUNK, D_S5), ys.reshape(ns * CHUNK, D_S5)


def _row(v):
    return v.reshape(1, -1)


def kernel(x_prompt, x_sample, state_s5_re, state_s5_im, state_hgrn, p_prompt, p_sample, ln_in_g, ln_in_b, w_in, s5_lambda_re, s5_lambda_im, s5_log_step, s5_b_re, s5_b_im, s5_c_re, s5_c_im, s5_d, s5_w_glu, s5_b_glu, s5_norm_g, hg_lb, hg_norm_g, w_out, ln1_g, ln1_b, router_w, router_b, w_gate, b_gate, w_up, b_up, w_down, b_down, ple_w, ple_gate_w, ln2_g, ln2_b):
    nb, seq, _ = x_prompt.shape
    ns, dseq, _ = x_sample.shape
    assert dseq == CHUNK and seq % CHUNK == 0 and w_in.shape[0] == 1
    nc = seq // CHUNK
    tp, ts = nb * seq, ns * dseq
    t = tp + ts
    tm = 512 if (tp % 512 == 0 and ts % 512 == 0) else 256
    assert tp % tm == 0 and ts % tm == 0 and seq % tm == 0 and ns % nb == 0

    h0, u_p, u_s, z_p, z_s = ln_in_proj(
        x_prompt, x_sample.reshape(ts, D_MODEL),
        _row(ln_in_g), _row(ln_in_b), w_in[0].astype(BF16), tm)

    m, w, wc, a = s5_prep(s5_lambda_re[0], s5_lambda_im[0], s5_log_step[0],
                          s5_b_re[0], s5_b_im[0], s5_c_re[0], s5_c_im[0])
    y_rows, fpr, fpi, fsr, fsi = s5_main(_s5_rows(u_p, u_s, nb, nc, ns), m, w, wc, a,
                                         jnp.swapaxes(state_s5_re[0], 0, 1),
                                         jnp.swapaxes(state_s5_im[0], 0, 1), nb, nc)
    y_pair = _s5_tokens(y_rows, nb, nc, ns)

    zero_state = jnp.zeros((nb, HG_HEADS, HG_D, HG_D), F32)
    ng = _row(hg_norm_g[0])
    o_p, st_p = hgrn(z_p.reshape(nb, seq, 4 * D_HG), zero_state, hg_lb, ng, nb)
    o_s, st_s = hgrn(z_s.reshape(ns, dseq, 4 * D_HG), jnp.swapaxes(state_hgrn[0], 2, 3),
                     hg_lb, ng, nb)

    h1, idx, gates, rank, before, counts = post_mix(
        h0, (u_p, u_s), y_pair, (o_p.reshape(tp, D_HG), o_s.reshape(ts, D_HG)),
        _row(s5_d[0]), s5_w_glu[0].astype(BF16), _row(s5_b_glu[0]),
        _row(s5_norm_g[0]), w_out[0].astype(BF16), _row(ln1_g[0]), _row(ln1_b[0]),
        router_w[0].T, router_b[0].reshape(N_EXPERTS, 1), tm)

    n_tiles = t // tm
    counts = counts[:, 0].astype(jnp.int32)
    before = before[:, :, 0].astype(jnp.int32)
    cnt = jnp.concatenate([before[1:], counts[None]], axis=0) - before
    padded = (counts + MOE_ROWS - 1) // MOE_ROWS * MOE_ROWS
    pend = jnp.cumsum(padded)
    staged = jnp.cumsum(cnt, axis=1) - cnt
    meta = jnp.concatenate([pend - padded + before, cnt, staged, jnp.zeros_like(cnt)], axis=1)
    experts = jnp.arange(N_EXPERTS, dtype=jnp.int32)
    staged_tok = jnp.repeat(staged, tm, axis=0)
    slots = jnp.sum(jnp.where(idx[..., None] == experts, staged_tok, 0), axis=-1) + rank

    def per_tile(a):
        return a.reshape(TOP_K, n_tiles, tm).transpose(1, 0, 2).reshape(-1)

    n_blocks = -(-t * TOP_K // MOE_ROWS) + N_EXPERTS
    n_used = (pend[-1] // MOE_ROWS).astype(jnp.int32)
    blk = jnp.arange(n_blocks, dtype=jnp.int32)
    blk = jnp.minimum(blk, n_used - 1)
    block_e = jnp.sum((pend[None, :] <= (blk * MOE_ROWS)[:, None]).astype(jnp.int32), axis=1)
    block_e = jnp.minimum(block_e, N_EXPERTS - 1)
    owns_rows = padded > 0
    segment = (jnp.cumsum(owns_rows.astype(jnp.int32)) - 1)[block_e]
    later = owns_rows[None, :] & (experts[None, :] > experts[:, None])
    next_owner = jnp.where(jnp.any(later, axis=1), jnp.argmax(later, axis=1), -1).astype(jnp.int32)
    next_e = next_owner[block_e]

    slots_flat = per_tile(slots)
    xs = dispatch(meta, pend, slots_flat, h1, n_blocks * MOE_ROWS, tm)
    yb = moe_ffn(block_e, n_used.reshape(1), segment, next_e, xs,
                 w_gate[0], b_gate[0][:, None, :], w_up[0], b_up[0][:, None, :],
                 w_down[0], b_down[0][:, None, :])
    out_p, out_s = combine(meta, slots_flat, per_tile(gates), h1, p_prompt[0], p_sample[0].reshape(ts, PLE_DIM),
                           yb, ple_w[0].astype(BF16), ple_gate_w[0].astype(BF16),
                           _row(ln2_g[0]), _row(ln2_b[0]), tm)

    def s5_state(f, n):
        return jnp.swapaxes(f, 0, 1).reshape(1, n, S5_GROUPS, S5_STATE)

    return (out_p, out_s.reshape(ns, dseq, D_MODEL),
            s5_state(fpr, nb), s5_state(fpi, nb), jnp.swapaxes(st_p, 2, 3)[None],
            s5_state(fsr, ns), s5_state(fsi, ns), jnp.swapaxes(st_s, 2, 3)[None])
```

```python
import functools

import jax
import jax.numpy as jnp
from jax import lax
from jax.experimental import pallas as pl
from jax.experimental.pallas import tpu as pltpu

F32 = jnp.float32
BF16 = jnp.bfloat16
HIGHEST = lax.Precision.HIGHEST

D_MODEL = 1024
CHUNK = 64
PLE_DIM = 256
D_S5 = 512
S5_GROUP = 16
S5_GROUPS = 32
S5_STATE = 64
D_HG = 512
HG_HEADS = 4
HG_D = 128
D_IN = D_S5 + 4 * D_HG
N_EXPERTS = 32
TOP_K = 4
SWIGLU_LIMIT = 7.0
SWIGLU_ALPHA = 1.702
DEEPNORM_ALPHA = 2.0 ** 0.25
LN_EPS = 1e-5
RMS_EPS = 1e-6

LANES = 128
SUBLANES = 8
ROW_TILES = D_MODEL // LANES
S5_CONV = CHUNK * S5_GROUP
HG_LEVELS = (32, 16, 8, 4, 2, 1)
MOE_ROWS = 512
VMEM_LIMIT = 56 * 1024 * 1024

assert ROW_TILES == SUBLANES


def _cparams(n_axes=1, flags=None):
    return pltpu.CompilerParams(dimension_semantics=("arbitrary",) * n_axes,
                                vmem_limit_bytes=VMEM_LIMIT, flags=flags)


def _dot(a, b, precision=None):
    return jnp.dot(a, b, preferred_element_type=F32, precision=precision)


def _layernorm(x, g, b):
    mu = jnp.mean(x, axis=-1, keepdims=True)
    xc = x - mu
    var = jnp.mean(xc * xc, axis=-1, keepdims=True)
    return xc * lax.rsqrt(var + LN_EPS) * g + b


def _two_phase_specs(block, n_first):
    nd = len(block)
    first = pl.BlockSpec(block, lambda i: (jnp.minimum(i, n_first - 1),) + (0,) * (nd - 1))
    second = pl.BlockSpec(block, lambda i: (jnp.maximum(i - n_first, 0),) + (0,) * (nd - 1))
    return [first, second]


def _prompt_spec(tm, seq, width, n_first):
    per_seq = seq // tm

    def index(i):
        ic = jnp.minimum(i, n_first - 1)
        return (ic // per_seq, ic % per_seq, 0)

    return pl.BlockSpec((1, tm, width), index)


def _chunk(rows, j):
    return pl.ds(j, rows, stride=ROW_TILES)


def _read_row_tiles(ref, rows):
    return jnp.concatenate([ref[_chunk(rows, j), :] for j in range(ROW_TILES)], axis=1)


def _write_row_tiles(ref, val, rows):
    for j in range(ROW_TILES):
        ref[_chunk(rows, j), :] = val[:, j * LANES:(j + 1) * LANES]


def _row_tiles(r, n=1):
    return pl.ds(pl.multiple_of(r * ROW_TILES, ROW_TILES), n * ROW_TILES)


def _row_tile(r):
    return _row_tiles(r)


def _ln_in_proj_kernel(xp_ref, xs_ref, g_ref, b_ref, w_ref, h_ref, up_ref, us_ref, zp_ref, zs_ref,
                       *, n_first):
    def phase(x, u_ref, z_ref):
        h = _layernorm(x, g_ref[...], b_ref[...])
        h_ref[...] = h
        hb = h.astype(BF16)
        u_ref[...] = _dot(hb, w_ref[:, :D_S5]).astype(BF16)
        z_ref[...] = _dot(hb, w_ref[:, D_S5:])

    i = pl.program_id(0)
    pl.when(i < n_first)(lambda: phase(xp_ref[0], up_ref, zp_ref))
    pl.when(i >= n_first)(lambda: phase(xs_ref[...], us_ref, zs_ref))


def ln_in_proj(xp, xs, g, b, w_bf16, tm):
    nb, seq, _ = xp.shape
    tp, ts = nb * seq, xs.shape[0]
    n_first = tp // tm
    fixed = lambda i: (0, 0)
    return pl.pallas_call(
        functools.partial(_ln_in_proj_kernel, n_first=n_first),
        grid=((tp + ts) // tm,),
        in_specs=[_prompt_spec(tm, seq, D_MODEL, n_first), _two_phase_specs((tm, D_MODEL), n_first)[1]]
                 + [pl.BlockSpec((1, D_MODEL), fixed), pl.BlockSpec((1, D_MODEL), fixed),
                    pl.BlockSpec((D_MODEL, D_IN), fixed)],
        out_specs=[pl.BlockSpec((tm, D_MODEL), lambda i: (i, 0))]
                  + _two_phase_specs((tm, D_S5), n_first)
                  + _two_phase_specs((tm, 4 * D_HG), n_first),
        out_shape=[jax.ShapeDtypeStruct((tp + ts, D_MODEL), F32),
                   jax.ShapeDtypeStruct((tp, D_S5), BF16), jax.ShapeDtypeStruct((ts, D_S5), BF16),
                   jax.ShapeDtypeStruct((tp, 4 * D_HG), F32), jax.ShapeDtypeStruct((ts, 4 * D_HG), F32)],
        compiler_params=_cparams(),
        name="ln_in_proj",
    )(xp, xs, g, b, w_bf16)


def _s5_prep_kernel(lrc_ref, lic_ref, lrr_ref, lir_ref, ls_ref, brt_ref, bit_ref,
                    brtt_ref, bitt_ref, crt_ref, cit_ref,
                    m_ref, w_ref, wc_ref, a_ref):
    step = jnp.exp(ls_ref[0])

    def discretise(lr_raw, li):
        lr = jnp.minimum(lr_raw, -1e-4)
        dr, di = lr * step, li * step
        mag = jnp.exp(dr)
        a_re, a_im = mag * jnp.cos(di), mag * jnp.sin(di)
        den = lr * lr + li * li
        nr = a_re - 1.0
        fr = (nr * lr + a_im * li) / den
        fi = (a_im * lr - nr * li) / den
        return dr, di, fr, fi

    dr_c, di_c, _, _ = discretise(lrc_ref[0], lic_ref[0])
    dr_r, di_r, fr_r, fi_r = discretise(lrr_ref[0], lir_ref[0])

    lane = lax.broadcasted_iota(jnp.int32, (1, S5_CONV), 1)
    t_row = lax.broadcasted_iota(jnp.int32, (1, CHUNK), 1).astype(F32)
    t_col = lax.broadcasted_iota(jnp.int32, (CHUNK, 1), 0).astype(F32)
    lag_of_lane = (lax.broadcasted_iota(jnp.int32, (CHUNK, S5_CONV), 1) // S5_GROUP
                   == lax.broadcasted_iota(jnp.int32, (CHUNK, S5_CONV), 0)).astype(F32)
    time_of_row = (lax.broadcasted_iota(jnp.int32, (S5_CONV, CHUNK), 0) // S5_GROUP
                   == lax.broadcasted_iota(jnp.int32, (S5_CONV, CHUNK), 1)).astype(F32)

    def c_times_power(tf):
        mag = jnp.exp(dr_c * tf)
        ang = di_c * tf
        pr = _dot(mag * jnp.cos(ang), lag_of_lane, HIGHEST)
        pi = _dot(mag * jnp.sin(ang), lag_of_lane, HIGHEST)
        ctr, cti = crt_ref[0], cit_ref[0]
        return ctr * pr - cti * pi, ctr * pi + cti * pr

    cpr, cpi = c_times_power(t_row)
    bbr = fr_r * brt_ref[0] - fi_r * bit_ref[0]
    bbi = fr_r * bit_ref[0] + fi_r * brt_ref[0]
    kt = _dot(bbr, cpr, HIGHEST) - _dot(bbi, cpi, HIGHEST)
    for s in range(CHUNK):
        shifted = kt if s == 0 else pltpu.roll(kt, S5_GROUP * s, axis=1)
        m_ref[0, S5_GROUP * s:S5_GROUP * (s + 1), :] = jnp.where(
            lane >= S5_GROUP * s, shifted, 0.0).astype(BF16)

    rem = CHUNK - 1.0 - t_col
    magw = jnp.exp(dr_r * rem)
    angw = di_r * rem
    pwr = _dot(time_of_row, magw * jnp.cos(angw), HIGHEST)
    pwi = _dot(time_of_row, magw * jnp.sin(angw), HIGHEST)
    bbtr = fr_r * brtt_ref[0] - fi_r * bitt_ref[0]
    bbti = fr_r * bitt_ref[0] + fi_r * brtt_ref[0]
    w_ref[0, :, :S5_STATE] = pwr * bbtr - pwi * bbti
    w_ref[0, :, S5_STATE:] = pwr * bbti + pwi * bbtr

    c1r, c1i = c_times_power(t_row + 1.0)
    wc_ref[0, :S5_STATE, :] = c1r.astype(BF16)
    wc_ref[0, S5_STATE:, :] = (-c1i).astype(BF16)

    full = float(CHUNK)
    mag_c = jnp.exp(dr_r * full)
    a_ref[0, 0:1, :] = mag_c * jnp.cos(di_r * full)
    a_ref[0, 1:2, :] = mag_c * jnp.sin(di_r * full)


def s5_prep(lam_re, lam_im, log_step, b_re, b_im, c_re, c_im):
    g, p = lam_re.shape
    brt = jnp.swapaxes(b_re, 1, 2)
    bit = jnp.swapaxes(b_im, 1, 2)
    crt = jnp.tile(jnp.swapaxes(c_re, 1, 2), (1, 1, CHUNK))
    cit = jnp.tile(jnp.swapaxes(c_im, 1, 2), (1, 1, CHUNK))
    args = (lam_re.reshape(g, p, 1), lam_im.reshape(g, p, 1),
            lam_re.reshape(g, 1, p), lam_im.reshape(g, 1, p), log_step.reshape(g, 1, 1),
            brt, bit, jnp.tile(brt, (1, CHUNK, 1)), jnp.tile(bit, (1, CHUNK, 1)), crt, cit)
    spec = lambda a: pl.BlockSpec((1,) + a.shape[1:], lambda i: (i, 0, 0))
    out_shape = [jax.ShapeDtypeStruct((g, S5_CONV, S5_CONV), BF16),
                 jax.ShapeDtypeStruct((g, S5_CONV, 2 * S5_STATE), F32),
                 jax.ShapeDtypeStruct((g, 2 * S5_STATE, S5_CONV), BF16),
                 jax.ShapeDtypeStruct((g, 2, S5_STATE), F32)]
    return pl.pallas_call(
        _s5_prep_kernel,
        grid=(g,),
        in_specs=[spec(a) for a in args],
        out_specs=[spec(o) for o in out_shape],
        out_shape=out_shape,
        compiler_params=_cparams(),
        name="s5_prep",
    )(*args)


def _split3(w):
    hi = w.astype(BF16)
    r1 = w - hi.astype(F32)
    mid = r1.astype(BF16)
    lo = (r1 - mid.astype(F32)).astype(BF16)
    return hi, mid, lo


def _s5_main_kernel(up_ref, us_ref, m_ref, w_ref, wc_ref, a_ref, xsr_ref, xsi_ref,
                    yp_ref, ys_ref, fpr_ref, fpi_ref, fsr_ref, fsi_ref,
                    hr_scr, hi_scr, x0r_scr, x0i_scr, *, n_prompt, n_chunks):
    w3 = _split3(w_ref[0])
    ar = a_ref[0, 0:1, :]
    ai = a_ref[0, 1:2, :]

    def local(u):
        hend = _dot(u, w3[0]) + _dot(u, w3[1]) + _dot(u, w3[2])
        return _dot(u, m_ref[0]), hend[:, :S5_STATE], hend[:, S5_STATE:]

    def carry_in(x0r, x0i):
        return (_dot(x0r.astype(BF16), wc_ref[0, :S5_STATE, :])
                + _dot(x0i.astype(BF16), wc_ref[0, S5_STATE:, :]))

    y_local, hr, hi = local(up_ref[0])
    hr_scr[...] = hr
    hi_scr[...] = hi
    xr = jnp.zeros((n_prompt, S5_STATE), F32)
    xi = jnp.zeros((n_prompt, S5_STATE), F32)
    for c in range(n_chunks):
        rows = slice(c * n_prompt, (c + 1) * n_prompt)
        x0r_scr[rows, :] = xr
        x0i_scr[rows, :] = xi
        xr, xi = (ar * xr - ai * xi + hr_scr[rows, :],
                  ar * xi + ai * xr + hi_scr[rows, :])
    fpr_ref[0] = xr
    fpi_ref[0] = xi
    yp_ref[0] = (y_local + carry_in(x0r_scr[...], x0i_scr[...])).astype(BF16)

    y_local, hr, hi = local(us_ref[0])
    sr, si = xsr_ref[0], xsi_ref[0]
    fsr_ref[0] = ar * sr - ai * si + hr
    fsi_ref[0] = ar * si + ai * sr + hi
    ys_ref[0] = (y_local + carry_in(sr, si)).astype(BF16)


def s5_main(up_rows, us_rows, m, w, wc, a, xs_re, xs_im, n_prompt, n_chunks):
    g, r, _ = up_rows.shape
    n_sample = xs_re.shape[1]
    spec = lambda shape: pl.BlockSpec((1,) + tuple(shape[1:]), lambda i: (i, 0, 0))
    args = (up_rows, us_rows, m, w, wc, a, xs_re, xs_im)
    out_shape = [jax.ShapeDtypeStruct((g, r, S5_CONV), BF16),
                 jax.ShapeDtypeStruct((g, n_sample, S5_CONV), BF16),
                 jax.ShapeDtypeStruct((g, n_prompt, S5_STATE), F32),
                 jax.ShapeDtypeStruct((g, n_prompt, S5_STATE), F32),
                 jax.ShapeDtypeStruct((g, n_sample, S5_STATE), F32),
                 jax.ShapeDtypeStruct((g, n_sample, S5_STATE), F32)]
    return pl.pallas_call(
        functools.partial(_s5_main_kernel, n_prompt=n_prompt, n_chunks=n_chunks),
        grid=(g,),
        in_specs=[spec(x.shape) for x in args],
        out_specs=[spec(o.shape) for o in out_shape],
        out_shape=out_shape,
        scratch_shapes=[pltpu.VMEM((r, S5_STATE), F32)] * 4,
        compiler_params=_cparams(),
        name="s5_main",
    )(*args)


def _hgrn_kernel(z_ref, s0_ref, lb_ref, ng_ref, o_ref, sfin_ref, st_scr, *, n_seq):
    c = pl.program_id(1)

    @pl.when(c == 0)
    def _():
        st_scr[...] = s0_ref[...]

    lbw = lb_ref[...]
    lbe = jnp.exp(lbw - jnp.max(lbw, axis=0, keepdims=True))
    lb_all = lbe[0:1, :] / jnp.sum(lbe, axis=0, keepdims=True)

    rowi = lax.broadcasted_iota(jnp.int32, (CHUNK, CHUNK), 0)
    coli = lax.broadcasted_iota(jnp.int32, (CHUNK, CHUNK), 1)
    rowk = lax.broadcasted_iota(jnp.int32, (CHUNK, HG_D), 0)
    cum_rows = [(coli <= rowi).astype(F32)]
    upper, lower, sign, same_block = [], [], [], []
    for m in HG_LEVELS:
        ref_row = (rowi // (2 * m)) * (2 * m) + (m - 1)
        cum_rows.append((coli <= ref_row).astype(F32))
        in_upper = (rowk % (2 * m)) >= m
        upper.append(in_upper.astype(F32).astype(BF16))
        lower.append(1.0 - upper[-1])
        sign.append(jnp.where(in_upper, 1.0, -1.0))
        same_block.append((rowi // (2 * m)) == (coli // (2 * m)))
    cum_mat = jnp.concatenate(cum_rows, axis=0).astype(BF16)
    cum_mat3 = jnp.concatenate([cum_mat] * 3, axis=1)
    diag = rowi == coli
    nt = (((1,), (1,)), ((), ()))

    def body(n, carry):
        zf = z_ref[n, :, D_HG:2 * D_HG]
        fg_all = lb_all + (1.0 - lb_all) * jax.nn.sigmoid(zf)
        cums = _dot(cum_mat3, jnp.concatenate(_split3(jnp.log2(fg_all)), axis=0))
        for hd in range(HG_HEADS):
            cols = slice(hd * HG_D, (hd + 1) * HG_D)
            zq = z_ref[n, :, hd * HG_D:(hd + 1) * HG_D]
            v = z_ref[n, :, 2 * D_HG + hd * HG_D:2 * D_HG + (hd + 1) * HG_D]
            zg = z_ref[n, :, 3 * D_HG + hd * HG_D:3 * D_HG + (hd + 1) * HG_D]
            q = zq * jax.nn.sigmoid(zq)
            kk = 1.0 - fg_all[:, cols]
            bcum = cums[:CHUNK, cols]
            b_last = bcum[CHUNK - 1:CHUNK, :]
            vb = v.astype(BF16)
            qb = q.astype(BF16)
            kb = kk.astype(BF16)
            st = st_scr[n, hd]

            scores = jnp.where(diag, lax.dot_general(qb, kb, nt, preferred_element_type=F32), 0.0)
            for lvl in range(len(HG_LEVELS)):
                bref = cums[(lvl + 1) * CHUNK:(lvl + 2) * CHUNK, cols]
                dec = jnp.exp2((bcum - bref) * sign[lvl]).astype(BF16)
                sc = lax.dot_general(qb * upper[lvl] * dec, kb * lower[lvl] * dec, nt,
                                     preferred_element_type=F32)
                scores = scores + jnp.where(same_block[lvl], sc, 0.0)

            qd = (q * jnp.exp2(bcum)).astype(BF16)
            o = lax.dot_general(qd, st.astype(BF16), nt, preferred_element_type=F32)
            o = o + _dot(scores.astype(BF16), vb)
            kdec = (kk * jnp.exp2(b_last - bcum)).astype(BF16)
            st_scr[n, hd] = jnp.exp2(b_last) * st + lax.dot_general(
                vb, kdec, (((0,), (0,)), ((), ())), preferred_element_type=F32)

            on = o * lax.rsqrt(jnp.mean(o * o, axis=-1, keepdims=True) + RMS_EPS) * ng_ref[:, cols]
            o_ref[n, :, hd * HG_D:(hd + 1) * HG_D] = on * (zg * jax.nn.sigmoid(zg))
        return carry

    lax.fori_loop(0, n_seq, body, 0, unroll=True)

    @pl.when(c == pl.num_programs(1) - 1)
    def _():
        sfin_ref[...] = st_scr[...]


def hgrn(z, s0_t, hg_lb, norm_g, n_seq):
    n, length, _ = z.shape
    return pl.pallas_call(
        functools.partial(_hgrn_kernel, n_seq=n_seq),
        grid=(n // n_seq, length // CHUNK),
        in_specs=[pl.BlockSpec((n_seq, CHUNK, 4 * D_HG), lambda g, c: (g, c, 0)),
                  pl.BlockSpec((n_seq, HG_HEADS, HG_D, HG_D), lambda g, c: (g, 0, 0, 0)),
                  pl.BlockSpec(hg_lb.shape, lambda g, c: (0, 0)),
                  pl.BlockSpec((1, D_HG), lambda g, c: (0, 0))],
        out_specs=[pl.BlockSpec((n_seq, CHUNK, D_HG), lambda g, c: (g, c, 0)),
                   pl.BlockSpec((n_seq, HG_HEADS, HG_D, HG_D), lambda g, c: (g, 0, 0, 0))],
        out_shape=[jax.ShapeDtypeStruct((n, length, D_HG), F32),
                   jax.ShapeDtypeStruct((n, HG_HEADS, HG_D, HG_D), F32)],
        scratch_shapes=[pltpu.VMEM((n_seq, HG_HEADS, HG_D, HG_D), F32)],
        compiler_params=_cparams(2),
        name="hgrn",
    )(z, s0_t, hg_lb, norm_g)


def _post_mix_kernel(h_ref, up_ref, us_ref, yp_ref, ys_ref, hgp_ref, hgs_ref,
                     d_ref, wglu_ref, bglu_ref, s5g_ref, wout_ref, g1_ref, b1_ref, rwt_ref, rb_ref,
                     h1_ref, idx_ref, gate_ref, rank_ref, before_ref, cnt_ref, run_scr, *, tm, n_first):
    i = pl.program_id(0)

    @pl.when(i == 0)
    def _():
        run_scr[...] = jnp.zeros_like(run_scr)

    def phase(u_ref, y_ref, hg_ref):
        ys = y_ref[...].astype(F32) + d_ref[...] * u_ref[...].astype(F32)
        gl = 0.5 * ys * (1.0 + lax.erf(ys * (2.0 ** -0.5)))
        s5o = gl * jax.nn.sigmoid(_dot(gl.astype(BF16), wglu_ref[...]) + bglu_ref[...])
        s5o = s5o * lax.rsqrt(jnp.mean(s5o * s5o, axis=-1, keepdims=True) + RMS_EPS) * s5g_ref[...]
        mix = (_dot(s5o.astype(BF16), wout_ref[:D_S5, :])
               + _dot(hg_ref[...].astype(BF16), wout_ref[D_S5:, :]))
        h1 = _layernorm(DEEPNORM_ALPHA * h_ref[...] + mix, g1_ref[...], b1_ref[...])
        _write_row_tiles(h1_ref, h1, tm)

        h_hi, h_mid, _ = _split3(h1)
        w_hi, w_mid, _ = _split3(rwt_ref[...])
        nt = (((1,), (1,)), ((), ()))
        logits = (lax.dot_general(w_hi, h_hi, nt, preferred_element_type=F32)
                  + lax.dot_general(w_hi, h_mid, nt, preferred_element_type=F32)
                  + lax.dot_general(w_mid, h_hi, nt, preferred_element_type=F32)) + rb_ref[...]
        eid = lax.broadcasted_iota(jnp.int32, (N_EXPERTS, tm), 0)
        vals, idxs = [], []
        for _ in range(TOP_K):
            m = jnp.max(logits, axis=0, keepdims=True)
            ix = jnp.min(jnp.where(logits == m, eid, N_EXPERTS), axis=0, keepdims=True)
            vals.append(m)
            idxs.append(ix)
            logits = jnp.where(eid == ix, -jnp.inf, logits)
        exps = [jnp.exp(v - vals[0]) for v in vals]
        den = exps[0] + exps[1] + exps[2] + exps[3]

        onehot = jnp.zeros((N_EXPERTS, tm), F32)
        for ix in idxs:
            onehot = onehot + (eid == ix).astype(F32)
        rowi = lax.broadcasted_iota(jnp.int32, (tm, tm), 0)
        coli = lax.broadcasted_iota(jnp.int32, (tm, tm), 1)
        earlier = (rowi < coli).astype(BF16)
        prefix = _dot(onehot.astype(BF16), earlier)
        for k in range(TOP_K):
            idx_ref[k:k + 1, :] = idxs[k]
            gate_ref[k:k + 1, :] = exps[k] / den
            rank_ref[k:k + 1, :] = jnp.sum(jnp.where(eid == idxs[k], prefix, 0.0),
                                           axis=0, keepdims=True).astype(jnp.int32)
        before_ref[0] = run_scr[...]
        run_scr[...] = run_scr[...] + jnp.sum(onehot, axis=1, keepdims=True)
        cnt_ref[...] = run_scr[...]

    pl.when(i < n_first)(lambda: phase(up_ref, yp_ref, hgp_ref))
    pl.when(i >= n_first)(lambda: phase(us_ref, ys_ref, hgs_ref))


def post_mix(h0, u_pair, y_pair, hg_pair, d_skip, wglu, bglu, s5g, wout, g1, b1, rw_t, rb_col, tm):
    t = h0.shape[0]
    n_first = u_pair[0].shape[0] // tm
    row = lambda i: (i, 0)
    col = lambda i: (0, i)
    fixed = lambda i: (0, 0)
    full = lambda a: pl.BlockSpec(a.shape, fixed)
    weights = (d_skip, wglu, bglu, s5g, wout, g1, b1, rw_t, rb_col)
    return pl.pallas_call(
        functools.partial(_post_mix_kernel, tm=tm, n_first=n_first),
        grid=(t // tm,),
        in_specs=[pl.BlockSpec((tm, D_MODEL), row)]
                 + _two_phase_specs((tm, D_S5), n_first) * 3
                 + [full(a) for a in weights],
        out_specs=[pl.BlockSpec((tm * ROW_TILES, LANES), row),
                   pl.BlockSpec((TOP_K, tm), col), pl.BlockSpec((TOP_K, tm), col),
                   pl.BlockSpec((TOP_K, tm), col),
                   pl.BlockSpec((1, N_EXPERTS, 1), lambda i: (i, 0, 0)),
                   pl.BlockSpec((N_EXPERTS, 1), fixed)],
        out_shape=[jax.ShapeDtypeStruct((t * ROW_TILES, LANES), F32),
                   jax.ShapeDtypeStruct((TOP_K, t), jnp.int32),
                   jax.ShapeDtypeStruct((TOP_K, t), F32),
                   jax.ShapeDtypeStruct((TOP_K, t), jnp.int32),
                   jax.ShapeDtypeStruct((t // tm, N_EXPERTS, 1), F32),
                   jax.ShapeDtypeStruct((N_EXPERTS, 1), F32)],
        scratch_shapes=[pltpu.VMEM((N_EXPERTS, 1), F32)],
        compiler_params=_cparams(),
        name="post_mix",
    )(h0, *u_pair, *y_pair, *hg_pair, *weights)


def _segment_copies(meta_ref, tile, tm, make_copy):
    for e in range(N_EXPERTS):
        sorted_row = meta_ref[tile, e]
        cnt = meta_ref[tile, N_EXPERTS + e]
        staged_row = meta_ref[tile, 2 * N_EXPERTS + e]
        for b in range(tm.bit_length()):
            done = cnt & ((1 << b) - 1)

            @pl.when(((cnt >> b) & 1) == 1)
            def _(b=b, done=done, e=e):
                make_copy(staged_row + done, sorted_row + done, 1 << b).start(priority=e % 2)


def _dispatch_kernel(meta_ref, pend_ref, slot_ref, h_ref, xs_ref, stage, zero_scr, zsem, sem, *, tm):
    n_rows = xs_ref.shape[0] // ROW_TILES
    i = pl.program_id(0)
    n = pl.num_programs(0)
    slot = i % 2

    def drain(s):
        pltpu.make_async_copy(stage.at[s], xs_ref.at[pl.ds(0, TOP_K * tm * ROW_TILES)], sem.at[s]).wait()

    @pl.when(i == 0)
    def _():
        zero_scr[...] = jnp.zeros_like(zero_scr)

        def last_block(e):
            prev = pend_ref[e - 1] if e > 0 else 0
            copy = pltpu.make_async_copy(
                zero_scr, xs_ref.at[pl.ds(pl.multiple_of(jnp.maximum(pend_ref[e] - MOE_ROWS, 0) * ROW_TILES,
                                                         ROW_TILES), MOE_ROWS * ROW_TILES)], zsem)
            return pend_ref[e] > prev, copy

        def tail_block(j):
            row0 = pend_ref[N_EXPERTS - 1] + j * MOE_ROWS
            copy = pltpu.make_async_copy(
                zero_scr, xs_ref.at[pl.ds(pl.multiple_of(jnp.minimum(row0, n_rows - MOE_ROWS) * ROW_TILES,
                                                         ROW_TILES), MOE_ROWS * ROW_TILES)], zsem)
            return row0 < n_rows, copy

        blocks = [last_block(e) for e in range(N_EXPERTS)] + [tail_block(j) for j in range(N_EXPERTS)]
        for used, copy in blocks:
            pl.when(used)(copy.start)
        for used, copy in blocks:
            pl.when(used)(copy.wait)

    pl.when(i >= 2)(lambda: drain(slot))

    def body(t, carry):
        row = h_ref[_row_tile(t), :]
        for k in range(TOP_K):
            stage[slot, _row_tile(slot_ref[k * tm + t]), :] = row
        return carry

    lax.fori_loop(0, tm, body, 0, unroll=4)
    _segment_copies(meta_ref, i, tm, lambda staged_row, sorted_row, rows: pltpu.make_async_copy(
        stage.at[slot, _row_tiles(staged_row, rows)], xs_ref.at[_row_tiles(sorted_row, rows)], sem.at[slot]))

    @pl.when(i == n - 1)
    def _():
        pl.when(n >= 2)(lambda: drain(1 - slot))
        drain(slot)


def dispatch(meta, pend, slots_flat, h1, n_rows, tm):
    t = h1.shape[0] // ROW_TILES
    grid_spec = pltpu.PrefetchScalarGridSpec(
        num_scalar_prefetch=2,
        grid=(t // tm,),
        in_specs=[pl.BlockSpec((TOP_K * tm,), lambda i, meta, pend: (i,), memory_space=pltpu.SMEM),
                  pl.BlockSpec((tm * ROW_TILES, LANES), lambda i, meta, pend: (i, 0))],
        out_specs=pl.BlockSpec(memory_space=pl.ANY),
        scratch_shapes=[pltpu.VMEM((2, TOP_K * tm * ROW_TILES, LANES), F32),
                        pltpu.VMEM((MOE_ROWS * ROW_TILES, LANES), F32),
                        pltpu.SemaphoreType.DMA(()), pltpu.SemaphoreType.DMA((2,))],
    )
    return pl.pallas_call(
        functools.partial(_dispatch_kernel, tm=tm),
        grid_spec=grid_spec,
        out_shape=jax.ShapeDtypeStruct((n_rows * ROW_TILES, LANES), F32),
        compiler_params=_cparams(),
        name="moe_dispatch",
    )(meta, pend, slots_flat, h1)


def _moe_ffn_kernel(be_ref, nu_ref, seg_ref, nxt_ref, x_ref, wg_ref, bg_ref, wu_ref, bu_ref, wd_ref, bd_ref,
                    y_ref, wbuf, wbf, sem):
    i = pl.program_id(0)
    hbm = (wg_ref, wu_ref, wd_ref)

    def weight_copies(expert, s):
        return [pltpu.make_async_copy(hbm[j].at[expert], wbuf.at[s, j], sem.at[s, j]) for j in range(3)]

    @pl.when((i == 0) | (be_ref[i] != be_ref[jnp.maximum(i - 1, 0)]))
    def _():
        s = seg_ref[i] % 2

        @pl.when(i == 0)
        def _():
            for c in weight_copies(be_ref[0], 0):
                c.start()

        for j, c in enumerate(weight_copies(be_ref[i], s)):
            c.wait()
            wbf[j] = wbuf[s, j].astype(BF16)

        @pl.when(nxt_ref[i] >= 0)
        def _():
            for c in weight_copies(nxt_ref[i], 1 - s):
                c.start()

    @pl.when(i < nu_ref[0])
    def _():
        x = _read_row_tiles(x_ref, MOE_ROWS).astype(BF16)
        gt = jnp.minimum(_dot(x, wbf[0]) + bg_ref[0], SWIGLU_LIMIT)
        up = jnp.clip(_dot(x, wbf[1]) + bu_ref[0], -SWIGLU_LIMIT, SWIGLU_LIMIT)
        hid = (up + 1.0) * (gt * jax.nn.sigmoid(SWIGLU_ALPHA * gt))
        _write_row_tiles(y_ref, _dot(hid.astype(BF16), wbf[2]) + bd_ref[0], MOE_ROWS)

    @pl.when(i >= nu_ref[0])
    def _():
        y_ref[...] = jnp.zeros_like(y_ref)


def moe_ffn(block_e, n_used, segment, next_e, xs, wg, bg, wu, bu, wd, bd):
    n_rows = xs.shape[0] // ROW_TILES
    n_blocks = n_rows // MOE_ROWS
    wsel = lambda i, be, nu, seg, nxt: (be[i], 0, 0)
    d_ff = wg.shape[-1]
    assert wg.shape[1:] == wu.shape[1:] == wd.shape[1:] == (D_MODEL, D_MODEL)
    anywhere = pl.BlockSpec(memory_space=pl.ANY)
    grid_spec = pltpu.PrefetchScalarGridSpec(
        num_scalar_prefetch=4,
        grid=(n_blocks,),
        in_specs=[pl.BlockSpec((MOE_ROWS * ROW_TILES, LANES),
                               lambda i, be, nu, seg, nxt: (jnp.minimum(i, nu[0] - 1), 0)),
                  anywhere, pl.BlockSpec((1, 1, d_ff), wsel),
                  anywhere, pl.BlockSpec((1, 1, d_ff), wsel),
                  anywhere, pl.BlockSpec((1, 1, D_MODEL), wsel)],
        out_specs=pl.BlockSpec((MOE_ROWS * ROW_TILES, LANES), lambda i, be, nu, seg, nxt: (i, 0)),
        scratch_shapes=[pltpu.VMEM((2, 3, D_MODEL, D_MODEL), F32), pltpu.VMEM((3, D_MODEL, D_MODEL), BF16),
                        pltpu.SemaphoreType.DMA((2, 3))],
    )
    return pl.pallas_call(
        _moe_ffn_kernel,
        grid_spec=grid_spec,
        out_shape=jax.ShapeDtypeStruct((n_rows * ROW_TILES, LANES), F32),
        compiler_params=_cparams(),
        name="moe_ffn",
    )(block_e, n_used, segment, next_e, xs, wg, bg, wu, bu, wd, bd)


def _combine_kernel(meta_ref, slot_ref, gate_ref, h_ref, pp_ref, ps_ref, yb_ref,
                    plew_ref, plegw_ref, g2_ref, b2_ref, outp_ref, outs_ref, buf, r_scr, sem,
                    *, tm, n_first):
    i = pl.program_id(0)
    n = pl.num_programs(0)
    slot = i % 2

    def fetch(tile, s):
        _segment_copies(meta_ref, tile, tm, lambda staged_row, sorted_row, rows: pltpu.make_async_copy(
            yb_ref.at[_row_tiles(sorted_row, rows)], buf.at[s, _row_tiles(staged_row, rows)], sem.at[s]))

    pl.when(i == 0)(lambda: fetch(0, 0))
    pl.when(i + 1 < n)(lambda: fetch(i + 1, 1 - slot))
    pltpu.make_async_copy(yb_ref.at[pl.ds(0, TOP_K * tm * ROW_TILES)], buf.at[slot], sem.at[slot]).wait()

    def body(t, carry):
        acc = DEEPNORM_ALPHA * h_ref[_row_tile(t), :]
        for k in range(TOP_K):
            acc = acc + gate_ref[k * tm + t] * buf[slot, _row_tile(slot_ref[k * tm + t]), :]
        r_scr[_row_tile(t), :] = acc
        return carry

    lax.fori_loop(0, tm, body, 0, unroll=4)
    r = _read_row_tiles(r_scr, tm)
    gate = jax.nn.sigmoid(_dot(r.astype(BF16), plegw_ref[...]))

    def finish(p, store):
        e = _dot(p.astype(BF16), plew_ref[...]) * gate
        store(_layernorm(r + e, g2_ref[...], b2_ref[...]))

    def store_prompt(v):
        outp_ref[0] = v

    def store_sample(v):
        outs_ref[...] = v

    pl.when(i < n_first)(lambda: finish(pp_ref[0], store_prompt))
    pl.when(i >= n_first)(lambda: finish(ps_ref[...], store_sample))


def combine(meta, slots_flat, gates_flat, h1, p_prompt, p_sample, yb, plew, plegw, g2, b2, tm):
    t = h1.shape[0] // ROW_TILES
    nb, seq, _ = p_prompt.shape
    ts = p_sample.shape[0]
    n_first = nb * seq // tm
    fixed = lambda i, meta: (0, 0)
    flat = pl.BlockSpec((TOP_K * tm,), lambda i, meta: (i,), memory_space=pltpu.SMEM)
    sample = lambda width: pl.BlockSpec((tm, width), lambda i, meta: (jnp.maximum(i - n_first, 0), 0))
    prompt = lambda width: pl.BlockSpec((1, tm, width), _prompt_spec(tm, seq, width, n_first).index_map)
    with_meta = lambda spec: pl.BlockSpec(spec.block_shape, lambda i, meta: spec.index_map(i))
    grid_spec = pltpu.PrefetchScalarGridSpec(
        num_scalar_prefetch=1,
        grid=(t // tm,),
        in_specs=[flat, flat,
                  pl.BlockSpec((tm * ROW_TILES, LANES), lambda i, meta: (i, 0)),
                  with_meta(prompt(PLE_DIM)), sample(PLE_DIM),
                  pl.BlockSpec(memory_space=pl.ANY),
                  pl.BlockSpec(plew.shape, fixed), pl.BlockSpec(plegw.shape, fixed),
                  pl.BlockSpec((1, D_MODEL), fixed), pl.BlockSpec((1, D_MODEL), fixed)],
        out_specs=[with_meta(prompt(D_MODEL)), sample(D_MODEL)],
        scratch_shapes=[pltpu.VMEM((2, TOP_K * tm * ROW_TILES, LANES), F32),
                        pltpu.VMEM((tm * ROW_TILES, LANES), F32),
                        pltpu.SemaphoreType.DMA((2,))],
    )
    return pl.pallas_call(
        functools.partial(_combine_kernel, tm=tm, n_first=n_first),
        grid_spec=grid_spec,
        out_shape=[jax.ShapeDtypeStruct((nb, seq, D_MODEL), F32), jax.ShapeDtypeStruct((ts, D_MODEL), F32)],
        compiler_params=_cparams(),
        name="moe_combine",
    )(meta, slots_flat, gates_flat, h1, p_prompt, p_sample, yb, plew, plegw, g2, b2)


def _s5_rows(u_p, u_s, nb, nc, ns):
    up = u_p.reshape(nb, nc, CHUNK, S5_GROUPS, S5_GROUP).transpose(3, 1, 0, 2, 4)
    us = u_s.reshape(ns, CHUNK, S5_GROUPS, S5_GROUP).transpose(2, 0, 1, 3)
    return up.reshape(S5_GROUPS, nc * nb, S5_CONV), us.reshape(S5_GROUPS, ns, S5_CONV)


def _s5_tokens(yp_rows, ys_rows, nb, nc, ns):
    yp = yp_rows.reshape(S5_GROUPS, nc, nb, CHUNK, S5_GROUP).transpose(2, 1, 3, 0, 4)
    ys = ys_rows.reshape(S5_GROUPS, ns, CHUNK, S5_GROUP).transpose(1, 2, 0, 3)
    return yp.reshape(nb * nc * CHUNK, D_S5), ys.reshape(ns * CHUNK, D_S5)


def _row(v):
    return v.reshape(1, -1)


def kernel(x_prompt, x_sample, state_s5_re, state_s5_im, state_hgrn, p_prompt, p_sample, ln_in_g, ln_in_b, w_in, s5_lambda_re, s5_lambda_im, s5_log_step, s5_b_re, s5_b_im, s5_c_re, s5_c_im, s5_d, s5_w_glu, s5_b_glu, s5_norm_g, hg_lb, hg_norm_g, w_out, ln1_g, ln1_b, router_w, router_b, w_gate, b_gate, w_up, b_up, w_down, b_down, ple_w, ple_gate_w, ln2_g, ln2_b):
    nb, seq, _ = x_prompt.shape
    ns, dseq, _ = x_sample.shape
    assert dseq == CHUNK and seq % CHUNK == 0 and w_in.shape[0] == 1
    nc = seq // CHUNK
    tp, ts = nb * seq, ns * dseq
    t = tp + ts
    tm = 512 if (tp % 512 == 0 and ts % 512 == 0) else 256
    assert tp % tm == 0 and ts % tm == 0 and seq % tm == 0 and ns % nb == 0

    h0, u_p, u_s, z_p, z_s = ln_in_proj(
        x_prompt, x_sample.reshape(ts, D_MODEL),
        _row(ln_in_g), _row(ln_in_b), w_in[0].astype(BF16), tm)

    m, w, wc, a = s5_prep(s5_lambda_re[0], s5_lambda_im[0], s5_log_step[0],
                          s5_b_re[0], s5_b_im[0], s5_c_re[0], s5_c_im[0])
    yp_rows, ys_rows, fpr, fpi, fsr, fsi = s5_main(*_s5_rows(u_p, u_s, nb, nc, ns), m, w, wc, a,
                                                   jnp.swapaxes(state_s5_re[0], 0, 1),
                                                   jnp.swapaxes(state_s5_im[0], 0, 1), nb, nc)
    y_pair = _s5_tokens(yp_rows, ys_rows, nb, nc, ns)

    zero_state = jnp.zeros((nb, HG_HEADS, HG_D, HG_D), F32)
    ng = _row(hg_norm_g[0])
    o_p, st_p = hgrn(z_p.reshape(nb, seq, 4 * D_HG), zero_state, hg_lb, ng, nb)
    o_s, st_s = hgrn(z_s.reshape(ns, dseq, 4 * D_HG), jnp.swapaxes(state_hgrn[0], 2, 3),
                     hg_lb, ng, nb)

    h1, idx, gates, rank, before, counts = post_mix(
        h0, (u_p, u_s), y_pair, (o_p.reshape(tp, D_HG), o_s.reshape(ts, D_HG)),
        _row(s5_d[0]), s5_w_glu[0].astype(BF16), _row(s5_b_glu[0]),
        _row(s5_norm_g[0]), w_out[0].astype(BF16), _row(ln1_g[0]), _row(ln1_b[0]),
        router_w[0].T, router_b[0].reshape(N_EXPERTS, 1), tm)

    n_tiles = t // tm
    counts = counts[:, 0].astype(jnp.int32)
    before = before[:, :, 0].astype(jnp.int32)
    cnt = jnp.concatenate([before[1:], counts[None]], axis=0) - before
    padded = (counts + MOE_ROWS - 1) // MOE_ROWS * MOE_ROWS
    pend = jnp.cumsum(padded)
    staged = jnp.cumsum(cnt, axis=1) - cnt
    meta = jnp.concatenate([pend - padded + before, cnt, staged, jnp.zeros_like(cnt)], axis=1)
    experts = jnp.arange(N_EXPERTS, dtype=jnp.int32)
    staged_tok = jnp.repeat(staged, tm, axis=0)
    slots = jnp.sum(jnp.where(idx[..., None] == experts, staged_tok, 0), axis=-1) + rank

    def per_tile(a):
        return a.reshape(TOP_K, n_tiles, tm).transpose(1, 0, 2).reshape(-1)

    n_blocks = -(-t * TOP_K // MOE_ROWS) + N_EXPERTS
    n_used = (pend[-1] // MOE_ROWS).astype(jnp.int32)
    blk = jnp.arange(n_blocks, dtype=jnp.int32)
    blk = jnp.minimum(blk, n_used - 1)
    block_e = jnp.sum((pend[None, :] <= (blk * MOE_ROWS)[:, None]).astype(jnp.int32), axis=1)
    block_e = jnp.minimum(block_e, N_EXPERTS - 1)
    owns_rows = padded > 0
    segment = (jnp.cumsum(owns_rows.astype(jnp.int32)) - 1)[block_e]
    later = owns_rows[None, :] & (experts[None, :] > experts[:, None])
    next_owner = jnp.where(jnp.any(later, axis=1), jnp.argmax(later, axis=1), -1).astype(jnp.int32)
    next_e = next_owner[block_e]

    slots_flat = per_tile(slots)
    xs = dispatch(meta, pend, slots_flat, h1, n_blocks * MOE_ROWS, tm)
    yb = moe_ffn(block_e, n_used.reshape(1), segment, next_e, xs,
                 w_gate[0], b_gate[0][:, None, :], w_up[0], b_up[0][:, None, :],
                 w_down[0], b_down[0][:, None, :])
    out_p, out_s = combine(meta, slots_flat, per_tile(gates), h1, p_prompt[0], p_sample[0].reshape(ts, PLE_DIM),
                           yb, ple_w[0].astype(BF16), ple_gate_w[0].astype(BF16),
                           _row(ln2_g[0]), _row(ln2_b[0]), tm)

    def s5_state(f, n):
        return jnp.swapaxes(f, 0, 1).reshape(1, n, S5_GROUPS, S5_STATE)

    return (out_p, out_s.reshape(ns, dseq, D_MODEL),
            s5_state(fpr, nb), s5_state(fpi, nb), jnp.swapaxes(st_p, 2, 3)[None],
            s5_state(fsr, ns), s5_state(fsi, ns), jnp.swapaxes(st_s, 2, 3)[None])
```

```python
import functools

import jax
import jax.numpy as jnp
from jax import lax
from jax.experimental import pallas as pl
from jax.experimental.pallas import tpu as pltpu

F32 = jnp.float32
BF16 = jnp.bfloat16
HIGHEST = lax.Precision.HIGHEST

D_MODEL = 1024
CHUNK = 64
PLE_DIM = 256
D_S5 = 512
S5_GROUP = 16
S5_GROUPS = 32
S5_STATE = 64
D_HG = 512
HG_HEADS = 4
HG_D = 128
D_IN = D_S5 + 4 * D_HG
N_EXPERTS = 32
TOP_K = 4
SWIGLU_LIMIT = 7.0
SWIGLU_ALPHA = 1.702
DEEPNORM_ALPHA = 2.0 ** 0.25
LN_EPS = 1e-5
RMS_EPS = 1e-6

LANES = 128
SUBLANES = 8
ROW_TILES = D_MODEL // LANES
S5_CONV = CHUNK * S5_GROUP
HG_CHUNK = 128
MOE_ROWS = 512
VMEM_LIMIT = 56 * 1024 * 1024

assert ROW_TILES == SUBLANES


def _cparams(n_axes=1, flags=None):
    return pltpu.CompilerParams(dimension_semantics=("arbitrary",) * n_axes,
                                vmem_limit_bytes=VMEM_LIMIT, flags=flags)


def _dot(a, b, precision=None):
    return jnp.dot(a, b, preferred_element_type=F32, precision=precision)


def _layernorm(x, g, b):
    mu = jnp.mean(x, axis=-1, keepdims=True)
    xc = x - mu
    var = jnp.mean(xc * xc, axis=-1, keepdims=True)
    return xc * lax.rsqrt(var + LN_EPS) * g + b


def _two_phase_specs(block, n_first):
    nd = len(block)
    first = pl.BlockSpec(block, lambda i: (jnp.minimum(i, n_first - 1),) + (0,) * (nd - 1))
    second = pl.BlockSpec(block, lambda i: (jnp.maximum(i - n_first, 0),) + (0,) * (nd - 1))
    return [first, second]


def _prompt_spec(tm, seq, width, n_first):
    per_seq = seq // tm

    def index(i):
        ic = jnp.minimum(i, n_first - 1)
        return (ic // per_seq, ic % per_seq, 0)

    return pl.BlockSpec((1, tm, width), index)


def _chunk(rows, j):
    return pl.ds(j, rows, stride=ROW_TILES)


def _read_row_tiles(ref, rows):
    return jnp.concatenate([ref[_chunk(rows, j), :] for j in range(ROW_TILES)], axis=1)


def _write_row_tiles(ref, val, rows):
    for j in range(ROW_TILES):
        ref[_chunk(rows, j), :] = val[:, j * LANES:(j + 1) * LANES]


def _row_tiles(r, n=1):
    return pl.ds(pl.multiple_of(r * ROW_TILES, ROW_TILES), n * ROW_TILES)


def _row_tile(r):
    return _row_tiles(r)


def _ln_in_proj_kernel(xp_ref, xs_ref, g_ref, b_ref, w_ref, h_ref, up_ref, us_ref, zp_ref, zs_ref,
                       *, n_first):
    def phase(x, u_ref, z_ref):
        h = _layernorm(x, g_ref[...], b_ref[...])
        h_ref[...] = h
        hb = h.astype(BF16)
        u_ref[...] = _dot(hb, w_ref[:, :D_S5]).astype(BF16)
        z_ref[...] = _dot(hb, w_ref[:, D_S5:])

    i = pl.program_id(0)
    pl.when(i < n_first)(lambda: phase(xp_ref[0], up_ref, zp_ref))
    pl.when(i >= n_first)(lambda: phase(xs_ref[...], us_ref, zs_ref))


def ln_in_proj(xp, xs, g, b, w_bf16, tm):
    nb, seq, _ = xp.shape
    tp, ts = nb * seq, xs.shape[0]
    n_first = tp // tm
    fixed = lambda i: (0, 0)
    return pl.pallas_call(
        functools.partial(_ln_in_proj_kernel, n_first=n_first),
        grid=((tp + ts) // tm,),
        in_specs=[_prompt_spec(tm, seq, D_MODEL, n_first), _two_phase_specs((tm, D_MODEL), n_first)[1]]
                 + [pl.BlockSpec((1, D_MODEL), fixed), pl.BlockSpec((1, D_MODEL), fixed),
                    pl.BlockSpec((D_MODEL, D_IN), fixed)],
        out_specs=[pl.BlockSpec((tm, D_MODEL), lambda i: (i, 0))]
                  + _two_phase_specs((tm, D_S5), n_first)
                  + _two_phase_specs((tm, 4 * D_HG), n_first),
        out_shape=[jax.ShapeDtypeStruct((tp + ts, D_MODEL), F32),
                   jax.ShapeDtypeStruct((tp, D_S5), BF16), jax.ShapeDtypeStruct((ts, D_S5), BF16),
                   jax.ShapeDtypeStruct((tp, 4 * D_HG), F32), jax.ShapeDtypeStruct((ts, 4 * D_HG), F32)],
        compiler_params=_cparams(),
        name="ln_in_proj",
    )(xp, xs, g, b, w_bf16)


def _s5_prep_kernel(lrc_ref, lic_ref, lrr_ref, lir_ref, ls_ref, brt_ref, bit_ref,
                    brtt_ref, bitt_ref, crt_ref, cit_ref,
                    m_ref, w_ref, wc_ref, a_ref):
    step = jnp.exp(ls_ref[0])

    def discretise(lr_raw, li):
        lr = jnp.minimum(lr_raw, -1e-4)
        dr, di = lr * step, li * step
        mag = jnp.exp(dr)
        a_re, a_im = mag * jnp.cos(di), mag * jnp.sin(di)
        den = lr * lr + li * li
        nr = a_re - 1.0
        fr = (nr * lr + a_im * li) / den
        fi = (a_im * lr - nr * li) / den
        return dr, di, fr, fi

    dr_c, di_c, _, _ = discretise(lrc_ref[0], lic_ref[0])
    dr_r, di_r, fr_r, fi_r = discretise(lrr_ref[0], lir_ref[0])

    lane = lax.broadcasted_iota(jnp.int32, (1, S5_CONV), 1)
    t_row = lax.broadcasted_iota(jnp.int32, (1, CHUNK), 1).astype(F32)
    t_col = lax.broadcasted_iota(jnp.int32, (CHUNK, 1), 0).astype(F32)
    lag_of_lane = (lax.broadcasted_iota(jnp.int32, (CHUNK, S5_CONV), 1) // S5_GROUP
                   == lax.broadcasted_iota(jnp.int32, (CHUNK, S5_CONV), 0)).astype(F32)
    time_of_row = (lax.broadcasted_iota(jnp.int32, (S5_CONV, CHUNK), 0) // S5_GROUP
                   == lax.broadcasted_iota(jnp.int32, (S5_CONV, CHUNK), 1)).astype(F32)

    def c_times_power(tf):
        mag = jnp.exp(dr_c * tf)
        ang = di_c * tf
        pr = _dot(mag * jnp.cos(ang), lag_of_lane, HIGHEST)
        pi = _dot(mag * jnp.sin(ang), lag_of_lane, HIGHEST)
        ctr, cti = crt_ref[0], cit_ref[0]
        return ctr * pr - cti * pi, ctr * pi + cti * pr

    cpr, cpi = c_times_power(t_row)
    bbr = fr_r * brt_ref[0] - fi_r * bit_ref[0]
    bbi = fr_r * bit_ref[0] + fi_r * brt_ref[0]
    kt = _dot(bbr, cpr, HIGHEST) - _dot(bbi, cpi, HIGHEST)
    for s in range(CHUNK):
        shifted = kt if s == 0 else pltpu.roll(kt, S5_GROUP * s, axis=1)
        m_ref[0, S5_GROUP * s:S5_GROUP * (s + 1), :] = jnp.where(
            lane >= S5_GROUP * s, shifted, 0.0).astype(BF16)

    rem = CHUNK - 1.0 - t_col
    magw = jnp.exp(dr_r * rem)
    angw = di_r * rem
    pwr = _dot(time_of_row, magw * jnp.cos(angw), HIGHEST)
    pwi = _dot(time_of_row, magw * jnp.sin(angw), HIGHEST)
    bbtr = fr_r * brtt_ref[0] - fi_r * bitt_ref[0]
    bbti = fr_r * bitt_ref[0] + fi_r * brtt_ref[0]
    w_ref[0, :, :S5_STATE] = pwr * bbtr - pwi * bbti
    w_ref[0, :, S5_STATE:] = pwr * bbti + pwi * bbtr

    c1r, c1i = c_times_power(t_row + 1.0)
    wc_ref[0, :S5_STATE, :] = c1r.astype(BF16)
    wc_ref[0, S5_STATE:, :] = (-c1i).astype(BF16)

    full = float(CHUNK)
    mag_c = jnp.exp(dr_r * full)
    a_ref[0, 0:1, :] = mag_c * jnp.cos(di_r * full)
    a_ref[0, 1:2, :] = mag_c * jnp.sin(di_r * full)


def s5_prep(lam_re, lam_im, log_step, b_re, b_im, c_re, c_im):
    g, p = lam_re.shape
    brt = jnp.swapaxes(b_re, 1, 2)
    bit = jnp.swapaxes(b_im, 1, 2)
    crt = jnp.tile(jnp.swapaxes(c_re, 1, 2), (1, 1, CHUNK))
    cit = jnp.tile(jnp.swapaxes(c_im, 1, 2), (1, 1, CHUNK))
    args = (lam_re.reshape(g, p, 1), lam_im.reshape(g, p, 1),
            lam_re.reshape(g, 1, p), lam_im.reshape(g, 1, p), log_step.reshape(g, 1, 1),
            brt, bit, jnp.tile(brt, (1, CHUNK, 1)), jnp.tile(bit, (1, CHUNK, 1)), crt, cit)
    spec = lambda a: pl.BlockSpec((1,) + a.shape[1:], lambda i: (i, 0, 0))
    out_shape = [jax.ShapeDtypeStruct((g, S5_CONV, S5_CONV), BF16),
                 jax.ShapeDtypeStruct((g, S5_CONV, 2 * S5_STATE), F32),
                 jax.ShapeDtypeStruct((g, 2 * S5_STATE, S5_CONV), BF16),
                 jax.ShapeDtypeStruct((g, 2, S5_STATE), F32)]
    return pl.pallas_call(
        _s5_prep_kernel,
        grid=(g,),
        in_specs=[spec(a) for a in args],
        out_specs=[spec(o) for o in out_shape],
        out_shape=out_shape,
        compiler_params=_cparams(),
        name="s5_prep",
    )(*args)


def _split3(w):
    hi = w.astype(BF16)
    r1 = w - hi.astype(F32)
    mid = r1.astype(BF16)
    lo = (r1 - mid.astype(F32)).astype(BF16)
    return hi, mid, lo


def _s5_main_kernel(up_ref, us_ref, m_ref, w_ref, wc_ref, a_ref, xsr_ref, xsi_ref,
                    yp_ref, ys_ref, fpr_ref, fpi_ref, fsr_ref, fsi_ref,
                    hr_scr, hi_scr, x0r_scr, x0i_scr, *, n_prompt, n_chunks):
    w3 = _split3(w_ref[0])
    ar = a_ref[0, 0:1, :]
    ai = a_ref[0, 1:2, :]

    def local(u):
        hend = _dot(u, w3[0]) + _dot(u, w3[1]) + _dot(u, w3[2])
        return _dot(u, m_ref[0]), hend[:, :S5_STATE], hend[:, S5_STATE:]

    def carry_in(x0r, x0i):
        return (_dot(x0r.astype(BF16), wc_ref[0, :S5_STATE, :])
                + _dot(x0i.astype(BF16), wc_ref[0, S5_STATE:, :]))

    y_local, hr, hi = local(up_ref[0])
    hr_scr[...] = hr
    hi_scr[...] = hi
    xr = jnp.zeros((n_prompt, S5_STATE), F32)
    xi = jnp.zeros((n_prompt, S5_STATE), F32)
    for c in range(n_chunks):
        rows = slice(c * n_prompt, (c + 1) * n_prompt)
        x0r_scr[rows, :] = xr
        x0i_scr[rows, :] = xi
        xr, xi = (ar * xr - ai * xi + hr_scr[rows, :],
                  ar * xi + ai * xr + hi_scr[rows, :])
    fpr_ref[0] = xr
    fpi_ref[0] = xi
    yp_ref[0] = (y_local + carry_in(x0r_scr[...], x0i_scr[...])).astype(BF16)

    y_local, hr, hi = local(us_ref[0])
    sr, si = xsr_ref[0], xsi_ref[0]
    fsr_ref[0] = ar * sr - ai * si + hr
    fsi_ref[0] = ar * si + ai * sr + hi
    ys_ref[0] = (y_local + carry_in(sr, si)).astype(BF16)


def s5_main(up_rows, us_rows, m, w, wc, a, xs_re, xs_im, n_prompt, n_chunks):
    g, r, _ = up_rows.shape
    n_sample = xs_re.shape[1]
    spec = lambda shape: pl.BlockSpec((1,) + tuple(shape[1:]), lambda i: (i, 0, 0))
    args = (up_rows, us_rows, m, w, wc, a, xs_re, xs_im)
    out_shape = [jax.ShapeDtypeStruct((g, r, S5_CONV), BF16),
                 jax.ShapeDtypeStruct((g, n_sample, S5_CONV), BF16),
                 jax.ShapeDtypeStruct((g, n_prompt, S5_STATE), F32),
                 jax.ShapeDtypeStruct((g, n_prompt, S5_STATE), F32),
                 jax.ShapeDtypeStruct((g, n_sample, S5_STATE), F32),
                 jax.ShapeDtypeStruct((g, n_sample, S5_STATE), F32)]
    return pl.pallas_call(
        functools.partial(_s5_main_kernel, n_prompt=n_prompt, n_chunks=n_chunks),
        grid=(g,),
        in_specs=[spec(x.shape) for x in args],
        out_specs=[spec(o.shape) for o in out_shape],
        out_shape=out_shape,
        scratch_shapes=[pltpu.VMEM((r, S5_STATE), F32)] * 4,
        compiler_params=_cparams(),
        name="s5_main",
    )(*args)


def _hgrn_kernel(z_ref, s0_ref, lb_ref, ng_ref, o_ref, sfin_ref, st_scr, *, n_seq, chunk):
    c = pl.program_id(1)

    @pl.when(c == 0)
    def _():
        st_scr[...] = s0_ref[...]

    lbw = lb_ref[...]
    lbe = jnp.exp(lbw - jnp.max(lbw, axis=0, keepdims=True))
    lb_all = lbe[0:1, :] / jnp.sum(lbe, axis=0, keepdims=True)

    levels = [chunk >> (i + 1) for i in range(chunk.bit_length() - 1)]
    rowi = lax.broadcasted_iota(jnp.int32, (chunk, chunk), 0)
    coli = lax.broadcasted_iota(jnp.int32, (chunk, chunk), 1)
    rowk = lax.broadcasted_iota(jnp.int32, (chunk, HG_D), 0)
    cum_rows = [(coli <= rowi).astype(F32)]
    upper, lower, sign, same_block = [], [], [], []
    for m in levels:
        ref_row = (rowi // (2 * m)) * (2 * m) + (m - 1)
        cum_rows.append((coli <= ref_row).astype(F32))
        in_upper = (rowk % (2 * m)) >= m
        upper.append(in_upper.astype(F32).astype(BF16))
        lower.append(1.0 - upper[-1])
        sign.append(jnp.where(in_upper, 1.0, -1.0))
        same_block.append((rowi // (2 * m)) == (coli // (2 * m)))
    cum_mat = jnp.concatenate(cum_rows, axis=0).astype(BF16)
    cum_mat3 = jnp.concatenate([cum_mat] * 3, axis=1)
    diag = rowi == coli
    nt = (((1,), (1,)), ((), ()))

    def body(n, carry):
        zf = z_ref[n, :, D_HG:2 * D_HG]
        fg_all = lb_all + (1.0 - lb_all) * jax.nn.sigmoid(zf)
        cums = _dot(cum_mat3, jnp.concatenate(_split3(jnp.log2(fg_all)), axis=0))
        for hd in range(HG_HEADS):
            cols = slice(hd * HG_D, (hd + 1) * HG_D)
            zq = z_ref[n, :, hd * HG_D:(hd + 1) * HG_D]
            v = z_ref[n, :, 2 * D_HG + hd * HG_D:2 * D_HG + (hd + 1) * HG_D]
            zg = z_ref[n, :, 3 * D_HG + hd * HG_D:3 * D_HG + (hd + 1) * HG_D]
            q = zq * jax.nn.sigmoid(zq)
            kk = 1.0 - fg_all[:, cols]
            bcum = cums[:chunk, cols]
            b_last = bcum[chunk - 1:chunk, :]
            vb = v.astype(BF16)
            qb = q.astype(BF16)
            kb = kk.astype(BF16)
            st = st_scr[n, hd]

            scores = jnp.where(diag, lax.dot_general(qb, kb, nt, preferred_element_type=F32), 0.0)
            for lvl in range(len(levels)):
                bref = cums[(lvl + 1) * chunk:(lvl + 2) * chunk, cols]
                dec = jnp.exp2((bcum - bref) * sign[lvl]).astype(BF16)
                sc = lax.dot_general(qb * upper[lvl] * dec, kb * lower[lvl] * dec, nt,
                                     preferred_element_type=F32)
                scores = scores + jnp.where(same_block[lvl], sc, 0.0)

            qd = (q * jnp.exp2(bcum)).astype(BF16)
            o = lax.dot_general(qd, st.astype(BF16), nt, preferred_element_type=F32)
            o = o + _dot(scores.astype(BF16), vb)
            kdec = (kk * jnp.exp2(b_last - bcum)).astype(BF16)
            st_scr[n, hd] = jnp.exp2(b_last) * st + lax.dot_general(
                vb, kdec, (((0,), (0,)), ((), ())), preferred_element_type=F32)

            on = o * lax.rsqrt(jnp.mean(o * o, axis=-1, keepdims=True) + RMS_EPS) * ng_ref[:, cols]
            o_ref[n, :, hd * HG_D:(hd + 1) * HG_D] = on * (zg * jax.nn.sigmoid(zg))
        return carry

    lax.fori_loop(0, n_seq, body, 0, unroll=True)

    @pl.when(c == pl.num_programs(1) - 1)
    def _():
        sfin_ref[...] = st_scr[...]


def hgrn(z, s0_t, hg_lb, norm_g, n_seq, chunk):
    n, length, _ = z.shape
    return pl.pallas_call(
        functools.partial(_hgrn_kernel, n_seq=n_seq, chunk=chunk),
        grid=(n // n_seq, length // chunk),
        in_specs=[pl.BlockSpec((n_seq, chunk, 4 * D_HG), lambda g, c: (g, c, 0)),
                  pl.BlockSpec((n_seq, HG_HEADS, HG_D, HG_D), lambda g, c: (g, 0, 0, 0)),
                  pl.BlockSpec(hg_lb.shape, lambda g, c: (0, 0)),
                  pl.BlockSpec((1, D_HG), lambda g, c: (0, 0))],
        out_specs=[pl.BlockSpec((n_seq, chunk, D_HG), lambda g, c: (g, c, 0)),
                   pl.BlockSpec((n_seq, HG_HEADS, HG_D, HG_D), lambda g, c: (g, 0, 0, 0))],
        out_shape=[jax.ShapeDtypeStruct((n, length, D_HG), F32),
                   jax.ShapeDtypeStruct((n, HG_HEADS, HG_D, HG_D), F32)],
        scratch_shapes=[pltpu.VMEM((n_seq, HG_HEADS, HG_D, HG_D), F32)],
        compiler_params=_cparams(2),
        name="hgrn",
    )(z, s0_t, hg_lb, norm_g)


def _post_mix_kernel(h_ref, up_ref, us_ref, yp_ref, ys_ref, hgp_ref, hgs_ref,
                     d_ref, wglu_ref, bglu_ref, s5g_ref, wout_ref, g1_ref, b1_ref, rwt_ref, rb_ref,
                     h1_ref, idx_ref, gate_ref, rank_ref, before_ref, cnt_ref, run_scr, *, tm, n_first):
    i = pl.program_id(0)

    @pl.when(i == 0)
    def _():
        run_scr[...] = jnp.zeros_like(run_scr)

    def phase(u_ref, y_ref, hg_ref):
        ys = y_ref[...].astype(F32) + d_ref[...] * u_ref[...].astype(F32)
        gl = 0.5 * ys * (1.0 + lax.erf(ys * (2.0 ** -0.5)))
        s5o = gl * jax.nn.sigmoid(_dot(gl.astype(BF16), wglu_ref[...]) + bglu_ref[...])
        s5o = s5o * lax.rsqrt(jnp.mean(s5o * s5o, axis=-1, keepdims=True) + RMS_EPS) * s5g_ref[...]
        mix = (_dot(s5o.astype(BF16), wout_ref[:D_S5, :])
               + _dot(hg_ref[...].astype(BF16), wout_ref[D_S5:, :]))
        h1 = _layernorm(DEEPNORM_ALPHA * h_ref[...] + mix, g1_ref[...], b1_ref[...])
        _write_row_tiles(h1_ref, h1, tm)

        h_hi, h_mid, _ = _split3(h1)
        w_hi, w_mid, _ = _split3(rwt_ref[...])
        nt = (((1,), (1,)), ((), ()))
        logits = (lax.dot_general(w_hi, h_hi, nt, preferred_element_type=F32)
                  + lax.dot_general(w_hi, h_mid, nt, preferred_element_type=F32)
                  + lax.dot_general(w_mid, h_hi, nt, preferred_element_type=F32)) + rb_ref[...]
        eid = lax.broadcasted_iota(jnp.int32, (N_EXPERTS, tm), 0)
        vals, idxs = [], []
        for _ in range(TOP_K):
            m = jnp.max(logits, axis=0, keepdims=True)
            ix = jnp.min(jnp.where(logits == m, eid, N_EXPERTS), axis=0, keepdims=True)
            vals.append(m)
            idxs.append(ix)
            logits = jnp.where(eid == ix, -jnp.inf, logits)
        exps = [jnp.exp(v - vals[0]) for v in vals]
        den = exps[0] + exps[1] + exps[2] + exps[3]

        onehot = jnp.zeros((N_EXPERTS, tm), F32)
        for ix in idxs:
            onehot = onehot + (eid == ix).astype(F32)
        rowi = lax.broadcasted_iota(jnp.int32, (tm, tm), 0)
        coli = lax.broadcasted_iota(jnp.int32, (tm, tm), 1)
        earlier = (rowi < coli).astype(BF16)
        prefix = _dot(onehot.astype(BF16), earlier)
        for k in range(TOP_K):
            idx_ref[k:k + 1, :] = idxs[k]
            gate_ref[k:k + 1, :] = exps[k] / den
            rank_ref[k:k + 1, :] = jnp.sum(jnp.where(eid == idxs[k], prefix, 0.0),
                                           axis=0, keepdims=True).astype(jnp.int32)
        before_ref[0] = run_scr[...]
        run_scr[...] = run_scr[...] + jnp.sum(onehot, axis=1, keepdims=True)
        cnt_ref[...] = run_scr[...]

    pl.when(i < n_first)(lambda: phase(up_ref, yp_ref, hgp_ref))
    pl.when(i >= n_first)(lambda: phase(us_ref, ys_ref, hgs_ref))


def post_mix(h0, u_pair, y_pair, hg_pair, d_skip, wglu, bglu, s5g, wout, g1, b1, rw_t, rb_col, tm):
    t = h0.shape[0]
    n_first = u_pair[0].shape[0] // tm
    row = lambda i: (i, 0)
    col = lambda i: (0, i)
    fixed = lambda i: (0, 0)
    full = lambda a: pl.BlockSpec(a.shape, fixed)
    weights = (d_skip, wglu, bglu, s5g, wout, g1, b1, rw_t, rb_col)
    return pl.pallas_call(
        functools.partial(_post_mix_kernel, tm=tm, n_first=n_first),
        grid=(t // tm,),
        in_specs=[pl.BlockSpec((tm, D_MODEL), row)]
                 + _two_phase_specs((tm, D_S5), n_first) * 3
                 + [full(a) for a in weights],
        out_specs=[pl.BlockSpec((tm * ROW_TILES, LANES), row),
                   pl.BlockSpec((TOP_K, tm), col), pl.BlockSpec((TOP_K, tm), col),
                   pl.BlockSpec((TOP_K, tm), col),
                   pl.BlockSpec((1, N_EXPERTS, 1), lambda i: (i, 0, 0)),
                   pl.BlockSpec((N_EXPERTS, 1), fixed)],
        out_shape=[jax.ShapeDtypeStruct((t * ROW_TILES, LANES), F32),
                   jax.ShapeDtypeStruct((TOP_K, t), jnp.int32),
                   jax.ShapeDtypeStruct((TOP_K, t), F32),
                   jax.ShapeDtypeStruct((TOP_K, t), jnp.int32),
                   jax.ShapeDtypeStruct((t // tm, N_EXPERTS, 1), F32),
                   jax.ShapeDtypeStruct((N_EXPERTS, 1), F32)],
        scratch_shapes=[pltpu.VMEM((N_EXPERTS, 1), F32)],
        compiler_params=_cparams(),
        name="post_mix",
    )(h0, *u_pair, *y_pair, *hg_pair, *weights)


def _segment_copies(meta_ref, tile, tm, make_copy):
    for e in range(N_EXPERTS):
        sorted_row = meta_ref[tile, e]
        cnt = meta_ref[tile, N_EXPERTS + e]
        staged_row = meta_ref[tile, 2 * N_EXPERTS + e]
        for b in range(tm.bit_length()):
            done = cnt & ((1 << b) - 1)

            @pl.when(((cnt >> b) & 1) == 1)
            def _(b=b, done=done, e=e):
                make_copy(staged_row + done, sorted_row + done, 1 << b).start(priority=e % 2)


def _dispatch_kernel(meta_ref, pend_ref, slot_ref, h_ref, xs_ref, stage, zero_scr, zsem, sem, *, tm):
    n_rows = xs_ref.shape[0] // ROW_TILES
    i = pl.program_id(0)
    n = pl.num_programs(0)
    slot = i % 2

    def drain(s):
        pltpu.make_async_copy(stage.at[s], xs_ref.at[pl.ds(0, TOP_K * tm * ROW_TILES)], sem.at[s]).wait()

    @pl.when(i == 0)
    def _():
        zero_scr[...] = jnp.zeros_like(zero_scr)

        def last_block(e):
            prev = pend_ref[e - 1] if e > 0 else 0
            copy = pltpu.make_async_copy(
                zero_scr, xs_ref.at[pl.ds(pl.multiple_of(jnp.maximum(pend_ref[e] - MOE_ROWS, 0) * ROW_TILES,
                                                         ROW_TILES), MOE_ROWS * ROW_TILES)], zsem)
            return pend_ref[e] > prev, copy

        def tail_block(j):
            row0 = pend_ref[N_EXPERTS - 1] + j * MOE_ROWS
            copy = pltpu.make_async_copy(
                zero_scr, xs_ref.at[pl.ds(pl.multiple_of(jnp.minimum(row0, n_rows - MOE_ROWS) * ROW_TILES,
                                                         ROW_TILES), MOE_ROWS * ROW_TILES)], zsem)
            return row0 < n_rows, copy

        blocks = [last_block(e) for e in range(N_EXPERTS)] + [tail_block(j) for j in range(N_EXPERTS)]
        for used, copy in blocks:
            pl.when(used)(copy.start)
        for used, copy in blocks:
            pl.when(used)(copy.wait)

    pl.when(i >= 2)(lambda: drain(slot))

    def body(t, carry):
        row = h_ref[_row_tile(t), :]
        for k in range(TOP_K):
            stage[slot, _row_tile(slot_ref[k * tm + t]), :] = row
        return carry

    lax.fori_loop(0, tm, body, 0, unroll=8)
    _segment_copies(meta_ref, i, tm, lambda staged_row, sorted_row, rows: pltpu.make_async_copy(
        stage.at[slot, _row_tiles(staged_row, rows)], xs_ref.at[_row_tiles(sorted_row, rows)], sem.at[slot]))

    @pl.when(i == n - 1)
    def _():
        pl.when(n >= 2)(lambda: drain(1 - slot))
        drain(slot)


def dispatch(meta, pend, slots_flat, h1, n_rows, tm):
    t = h1.shape[0] // ROW_TILES
    grid_spec = pltpu.PrefetchScalarGridSpec(
        num_scalar_prefetch=2,
        grid=(t // tm,),
        in_specs=[pl.BlockSpec((TOP_K * tm,), lambda i, meta, pend: (i,), memory_space=pltpu.SMEM),
                  pl.BlockSpec((tm * ROW_TILES, LANES), lambda i, meta, pend: (i, 0))],
        out_specs=pl.BlockSpec(memory_space=pl.ANY),
        scratch_shapes=[pltpu.VMEM((2, TOP_K * tm * ROW_TILES, LANES), F32),
                        pltpu.VMEM((MOE_ROWS * ROW_TILES, LANES), F32),
                        pltpu.SemaphoreType.DMA(()), pltpu.SemaphoreType.DMA((2,))],
    )
    return pl.pallas_call(
        functools.partial(_dispatch_kernel, tm=tm),
        grid_spec=grid_spec,
        out_shape=jax.ShapeDtypeStruct((n_rows * ROW_TILES, LANES), F32),
        compiler_params=_cparams(),
        name="moe_dispatch",
    )(meta, pend, slots_flat, h1)


def _moe_ffn_kernel(be_ref, nu_ref, seg_ref, nxt_ref, x_ref, wg_ref, bg_ref, wu_ref, bu_ref, wd_ref, bd_ref,
                    y_ref, wbuf, wbf, sem):
    i = pl.program_id(0)
    hbm = (wg_ref, wu_ref, wd_ref)

    def weight_copies(expert, s):
        return [pltpu.make_async_copy(hbm[j].at[expert], wbuf.at[s, j], sem.at[s, j]) for j in range(3)]

    @pl.when((i == 0) | (be_ref[i] != be_ref[jnp.maximum(i - 1, 0)]))
    def _():
        s = seg_ref[i] % 2

        @pl.when(i == 0)
        def _():
            for c in weight_copies(be_ref[0], 0):
                c.start()

        for j, c in enumerate(weight_copies(be_ref[i], s)):
            c.wait()
            wbf[j] = wbuf[s, j].astype(BF16)

        @pl.when(nxt_ref[i] >= 0)
        def _():
            for c in weight_copies(nxt_ref[i], 1 - s):
                c.start()

    @pl.when(i < nu_ref[0])
    def _():
        x = _read_row_tiles(x_ref, MOE_ROWS).astype(BF16)
        gt = jnp.minimum(_dot(x, wbf[0]) + bg_ref[0], SWIGLU_LIMIT)
        up = jnp.clip(_dot(x, wbf[1]) + bu_ref[0], -SWIGLU_LIMIT, SWIGLU_LIMIT)
        hid = (up + 1.0) * (gt * jax.nn.sigmoid(SWIGLU_ALPHA * gt))
        _write_row_tiles(y_ref, _dot(hid.astype(BF16), wbf[2]) + bd_ref[0], MOE_ROWS)

    @pl.when(i >= nu_ref[0])
    def _():
        y_ref[...] = jnp.zeros_like(y_ref)


def moe_ffn(block_e, n_used, segment, next_e, xs, wg, bg, wu, bu, wd, bd):
    n_rows = xs.shape[0] // ROW_TILES
    n_blocks = n_rows // MOE_ROWS
    wsel = lambda i, be, nu, seg, nxt: (be[i], 0, 0)
    d_ff = wg.shape[-1]
    assert wg.shape[1:] == wu.shape[1:] == wd.shape[1:] == (D_MODEL, D_MODEL)
    anywhere = pl.BlockSpec(memory_space=pl.ANY)
    grid_spec = pltpu.PrefetchScalarGridSpec(
        num_scalar_prefetch=4,
        grid=(n_blocks,),
        in_specs=[pl.BlockSpec((MOE_ROWS * ROW_TILES, LANES),
                               lambda i, be, nu, seg, nxt: (jnp.minimum(i, nu[0] - 1), 0)),
                  anywhere, pl.BlockSpec((1, 1, d_ff), wsel),
                  anywhere, pl.BlockSpec((1, 1, d_ff), wsel),
                  anywhere, pl.BlockSpec((1, 1, D_MODEL), wsel)],
        out_specs=pl.BlockSpec((MOE_ROWS * ROW_TILES, LANES), lambda i, be, nu, seg, nxt: (i, 0)),
        scratch_shapes=[pltpu.VMEM((2, 3, D_MODEL, D_MODEL), F32), pltpu.VMEM((3, D_MODEL, D_MODEL), BF16),
                        pltpu.SemaphoreType.DMA((2, 3))],
    )
    return pl.pallas_call(
        _moe_ffn_kernel,
        grid_spec=grid_spec,
        out_shape=jax.ShapeDtypeStruct((n_rows * ROW_TILES, LANES), F32),
        compiler_params=_cparams(),
        name="moe_ffn",
    )(block_e, n_used, segment, next_e, xs, wg, bg, wu, bu, wd, bd)


def _combine_kernel(meta_ref, slot_ref, gate_ref, h_ref, pp_ref, ps_ref, yb_ref,
                    plew_ref, plegw_ref, g2_ref, b2_ref, outp_ref, outs_ref, buf, r_scr, sem,
                    *, tm, n_first):
    i = pl.program_id(0)
    n = pl.num_programs(0)
    slot = i % 2

    def fetch(tile, s):
        _segment_copies(meta_ref, tile, tm, lambda staged_row, sorted_row, rows: pltpu.make_async_copy(
            yb_ref.at[_row_tiles(sorted_row, rows)], buf.at[s, _row_tiles(staged_row, rows)], sem.at[s]))

    pl.when(i == 0)(lambda: fetch(0, 0))
    pl.when(i + 1 < n)(lambda: fetch(i + 1, 1 - slot))
    pltpu.make_async_copy(yb_ref.at[pl.ds(0, TOP_K * tm * ROW_TILES)], buf.at[slot], sem.at[slot]).wait()

    def body(t, carry):
        acc = DEEPNORM_ALPHA * h_ref[_row_tile(t), :]
        for k in range(TOP_K):
            acc = acc + gate_ref[k * tm + t] * buf[slot, _row_tile(slot_ref[k * tm + t]), :]
        r_scr[_row_tile(t), :] = acc
        return carry

    lax.fori_loop(0, tm, body, 0, unroll=8)
    r = _read_row_tiles(r_scr, tm)
    gate = jax.nn.sigmoid(_dot(r.astype(BF16), plegw_ref[...]))

    def finish(p, store):
        e = _dot(p.astype(BF16), plew_ref[...]) * gate
        store(_layernorm(r + e, g2_ref[...], b2_ref[...]))

    def store_prompt(v):
        outp_ref[0] = v

    def store_sample(v):
        outs_ref[...] = v

    pl.when(i < n_first)(lambda: finish(pp_ref[0], store_prompt))
    pl.when(i >= n_first)(lambda: finish(ps_ref[...], store_sample))


def combine(meta, slots_flat, gates_flat, h1, p_prompt, p_sample, yb, plew, plegw, g2, b2, tm):
    t = h1.shape[0] // ROW_TILES
    nb, seq, _ = p_prompt.shape
    ts = p_sample.shape[0]
    n_first = nb * seq // tm
    fixed = lambda i, meta: (0, 0)
    flat = pl.BlockSpec((TOP_K * tm,), lambda i, meta: (i,), memory_space=pltpu.SMEM)
    sample = lambda width: pl.BlockSpec((tm, width), lambda i, meta: (jnp.maximum(i - n_first, 0), 0))
    prompt = lambda width: pl.BlockSpec((1, tm, width), _prompt_spec(tm, seq, width, n_first).index_map)
    with_meta = lambda spec: pl.BlockSpec(spec.block_shape, lambda i, meta: spec.index_map(i))
    grid_spec = pltpu.PrefetchScalarGridSpec(
        num_scalar_prefetch=1,
        grid=(t // tm,),
        in_specs=[flat, flat,
                  pl.BlockSpec((tm * ROW_TILES, LANES), lambda i, meta: (i, 0)),
                  with_meta(prompt(PLE_DIM)), sample(PLE_DIM),
                  pl.BlockSpec(memory_space=pl.ANY),
                  pl.BlockSpec(plew.shape, fixed), pl.BlockSpec(plegw.shape, fixed),
                  pl.BlockSpec((1, D_MODEL), fixed), pl.BlockSpec((1, D_MODEL), fixed)],
        out_specs=[with_meta(prompt(D_MODEL)), sample(D_MODEL)],
        scratch_shapes=[pltpu.VMEM((2, TOP_K * tm * ROW_TILES, LANES), F32),
                        pltpu.VMEM((tm * ROW_TILES, LANES), F32),
                        pltpu.SemaphoreType.DMA((2,))],
    )
    return pl.pallas_call(
        functools.partial(_combine_kernel, tm=tm, n_first=n_first),
        grid_spec=grid_spec,
        out_shape=[jax.ShapeDtypeStruct((nb, seq, D_MODEL), F32), jax.ShapeDtypeStruct((ts, D_MODEL), F32)],
        compiler_params=_cparams(),
        name="moe_combine",
    )(meta, slots_flat, gates_flat, h1, p_prompt, p_sample, yb, plew, plegw, g2, b2)


def _s5_rows(u_p, u_s, nb, nc, ns):
    up = u_p.reshape(nb, nc, CHUNK, S5_GROUPS, S5_GROUP).transpose(3, 1, 0, 2, 4)
    us = u_s.reshape(ns, CHUNK, S5_GROUPS, S5_GROUP).transpose(2, 0, 1, 3)
    return up.reshape(S5_GROUPS, nc * nb, S5_CONV), us.reshape(S5_GROUPS, ns, S5_CONV)


def _s5_tokens(yp_rows, ys_rows, nb, nc, ns):
    yp = yp_rows.reshape(S5_GROUPS, nc, nb, CHUNK, S5_GROUP).transpose(2, 1, 3, 0, 4)
    ys = ys_rows.reshape(S5_GROUPS, ns, CHUNK, S5_GROUP).transpose(1, 2, 0, 3)
    return yp.reshape(nb * nc * CHUNK, D_S5), ys.reshape(ns * CHUNK, D_S5)


def _row(v):
    return v.reshape(1, -1)


def kernel(x_prompt, x_sample, state_s5_re, state_s5_im, state_hgrn, p_prompt, p_sample, ln_in_g, ln_in_b, w_in, s5_lambda_re, s5_lambda_im, s5_log_step, s5_b_re, s5_b_im, s5_c_re, s5_c_im, s5_d, s5_w_glu, s5_b_glu, s5_norm_g, hg_lb, hg_norm_g, w_out, ln1_g, ln1_b, router_w, router_b, w_gate, b_gate, w_up, b_up, w_down, b_down, ple_w, ple_gate_w, ln2_g, ln2_b):
    nb, seq, _ = x_prompt.shape
    ns, dseq, _ = x_sample.shape
    assert dseq == CHUNK and seq % CHUNK == 0 and w_in.shape[0] == 1
    nc = seq // CHUNK
    tp, ts = nb * seq, ns * dseq
    t = tp + ts
    tm = 512 if (tp % 512 == 0 and ts % 512 == 0) else 256
    assert tp % tm == 0 and ts % tm == 0 and seq % tm == 0 and ns % nb == 0

    h0, u_p, u_s, z_p, z_s = ln_in_proj(
        x_prompt, x_sample.reshape(ts, D_MODEL),
        _row(ln_in_g), _row(ln_in_b), w_in[0].astype(BF16), tm)

    m, w, wc, a = s5_prep(s5_lambda_re[0], s5_lambda_im[0], s5_log_step[0],
                          s5_b_re[0], s5_b_im[0], s5_c_re[0], s5_c_im[0])
    yp_rows, ys_rows, fpr, fpi, fsr, fsi = s5_main(*_s5_rows(u_p, u_s, nb, nc, ns), m, w, wc, a,
                                                   jnp.swapaxes(state_s5_re[0], 0, 1),
                                                   jnp.swapaxes(state_s5_im[0], 0, 1), nb, nc)
    y_pair = _s5_tokens(yp_rows, ys_rows, nb, nc, ns)

    zero_state = jnp.zeros((nb, HG_HEADS, HG_D, HG_D), F32)
    ng = _row(hg_norm_g[0])
    o_p, st_p = hgrn(z_p.reshape(nb, seq, 4 * D_HG), zero_state, hg_lb, ng, nb,
                     HG_CHUNK if seq % HG_CHUNK == 0 else CHUNK)
    o_s, st_s = hgrn(z_s.reshape(ns, dseq, 4 * D_HG), jnp.swapaxes(state_hgrn[0], 2, 3),
                     hg_lb, ng, nb, dseq)

    h1, idx, gates, rank, before, counts = post_mix(
        h0, (u_p, u_s), y_pair, (o_p.reshape(tp, D_HG), o_s.reshape(ts, D_HG)),
        _row(s5_d[0]), s5_w_glu[0].astype(BF16), _row(s5_b_glu[0]),
        _row(s5_norm_g[0]), w_out[0].astype(BF16), _row(ln1_g[0]), _row(ln1_b[0]),
        router_w[0].T, router_b[0].reshape(N_EXPERTS, 1), tm)

    n_tiles = t // tm
    counts = counts[:, 0].astype(jnp.int32)
    before = before[:, :, 0].astype(jnp.int32)
    cnt = jnp.concatenate([before[1:], counts[None]], axis=0) - before
    padded = (counts + MOE_ROWS - 1) // MOE_ROWS * MOE_ROWS
    pend = jnp.cumsum(padded)
    staged = jnp.cumsum(cnt, axis=1) - cnt
    meta = jnp.concatenate([pend - padded + before, cnt, staged, jnp.zeros_like(cnt)], axis=1)
    experts = jnp.arange(N_EXPERTS, dtype=jnp.int32)
    staged_tok = jnp.repeat(staged, tm, axis=0)
    slots = jnp.sum(jnp.where(idx[..., None] == experts, staged_tok, 0), axis=-1) + rank

    def per_tile(a):
        return a.reshape(TOP_K, n_tiles, tm).transpose(1, 0, 2).reshape(-1)

    n_blocks = -(-t * TOP_K // MOE_ROWS) + N_EXPERTS
    n_used = (pend[-1] // MOE_ROWS).astype(jnp.int32)
    blk = jnp.arange(n_blocks, dtype=jnp.int32)
    blk = jnp.minimum(blk, n_used - 1)
    block_e = jnp.sum((pend[None, :] <= (blk * MOE_ROWS)[:, None]).astype(jnp.int32), axis=1)
    block_e = jnp.minimum(block_e, N_EXPERTS - 1)
    owns_rows = padded > 0
    segment = (jnp.cumsum(owns_rows.astype(jnp.int32)) - 1)[block_e]
    later = owns_rows[None, :] & (experts[None, :] > experts[:, None])
    next_owner = jnp.where(jnp.any(later, axis=1), jnp.argmax(later, axis=1), -1).astype(jnp.int32)
    next_e = next_owner[block_e]

    slots_flat = per_tile(slots)
    xs = dispatch(meta, pend, slots_flat, h1, n_blocks * MOE_ROWS, tm)
    yb = moe_ffn(block_e, n_used.reshape(1), segment, next_e, xs,
                 w_gate[0], b_gate[0][:, None, :], w_up[0], b_up[0][:, None, :],
                 w_down[0], b_down[0][:, None, :])
    out_p, out_s = combine(meta, slots_flat, per_tile(gates), h1, p_prompt[0], p_sample[0].reshape(ts, PLE_DIM),
                           yb, ple_w[0].astype(BF16), ple_gate_w[0].astype(BF16),
                           _row(ln2_g[0]), _row(ln2_b[0]), tm)

    def s5_state(f, n):
        return jnp.swapaxes(f, 0, 1).reshape(1, n, S5_GROUPS, S5_STATE)

    return (out_p, out_s.reshape(ns, dseq, D_MODEL),
            s5_state(fpr, nb), s5_state(fpi, nb), jnp.swapaxes(st_p, 2, 3)[None],
            s5_state(fsr, ns), s5_state(fsi, ns), jnp.swapaxes(st_s, 2, 3)[None])
```

```python
import functools

import jax
import jax.numpy as jnp
from jax import lax
from jax.experimental import pallas as pl
from jax.experimental.pallas import tpu as pltpu

F32 = jnp.float32
BF16 = jnp.bfloat16
HIGHEST = lax.Precision.HIGHEST

D_MODEL = 1024
CHUNK = 64
PLE_DIM = 256
D_S5 = 512
S5_GROUP = 16
S5_GROUPS = 32
S5_STATE = 64
D_HG = 512
HG_HEADS = 4
HG_D = 128
D_IN = D_S5 + 4 * D_HG
N_EXPERTS = 32
TOP_K = 4
SWIGLU_LIMIT = 7.0
SWIGLU_ALPHA = 1.702
DEEPNORM_ALPHA = 2.0 ** 0.25
LN_EPS = 1e-5
RMS_EPS = 1e-6

LANES = 128
SUBLANES = 8
ROW_TILES = D_MODEL // LANES
S5_CONV = CHUNK * S5_GROUP
HG_CHUNK = 128
MOE_ROWS = 512
VMEM_LIMIT = 56 * 1024 * 1024

assert ROW_TILES == SUBLANES


def _cparams(n_axes=1, flags=None):
    return pltpu.CompilerParams(dimension_semantics=("arbitrary",) * n_axes,
                                vmem_limit_bytes=VMEM_LIMIT, flags=flags)


def _dot(a, b, precision=None):
    return jnp.dot(a, b, preferred_element_type=F32, precision=precision)


def _layernorm(x, g, b):
    mu = jnp.mean(x, axis=-1, keepdims=True)
    xc = x - mu
    var = jnp.mean(xc * xc, axis=-1, keepdims=True)
    return xc * lax.rsqrt(var + LN_EPS) * g + b


def _two_phase_specs(block, n_first):
    nd = len(block)
    first = pl.BlockSpec(block, lambda i: (jnp.minimum(i, n_first - 1),) + (0,) * (nd - 1))
    second = pl.BlockSpec(block, lambda i: (jnp.maximum(i - n_first, 0),) + (0,) * (nd - 1))
    return [first, second]


def _two_phase_lane_block_specs(tm, n_first):
    nblk = D_S5 // LANES
    first = pl.BlockSpec((nblk, tm, LANES), lambda i: (0, jnp.minimum(i, n_first - 1), 0))
    second = pl.BlockSpec((nblk, tm, LANES), lambda i: (0, jnp.maximum(i - n_first, 0), 0))
    return [first, second]


def _read_lane_blocks(ref):
    return jnp.concatenate([ref[j] for j in range(D_S5 // LANES)], axis=1)


def _prompt_spec(tm, seq, width, n_first):
    per_seq = seq // tm

    def index(i):
        ic = jnp.minimum(i, n_first - 1)
        return (ic // per_seq, ic % per_seq, 0)

    return pl.BlockSpec((1, tm, width), index)


def _chunk(rows, j):
    return pl.ds(j, rows, stride=ROW_TILES)


def _read_row_tiles(ref, rows):
    return jnp.concatenate([ref[_chunk(rows, j), :] for j in range(ROW_TILES)], axis=1)


def _write_row_tiles(ref, val, rows):
    for j in range(ROW_TILES):
        ref[_chunk(rows, j), :] = val[:, j * LANES:(j + 1) * LANES]


def _row_tiles(r, n=1):
    return pl.ds(pl.multiple_of(r * ROW_TILES, ROW_TILES), n * ROW_TILES)


def _row_tile(r):
    return _row_tiles(r)


def _ln_in_proj_kernel(xp_ref, xs_ref, g_ref, b_ref, w_ref, h_ref, up_ref, us_ref, zp_ref, zs_ref,
                       *, n_first):
    def phase(x, u_ref, z_ref):
        h = _layernorm(x, g_ref[...], b_ref[...])
        h_ref[...] = h
        hb = h.astype(BF16)
        u = _dot(hb, w_ref[:, :D_S5])
        for j in range(D_S5 // LANES):
            u_ref[j] = u[:, j * LANES:(j + 1) * LANES]
        z_ref[...] = _dot(hb, w_ref[:, D_S5:])

    i = pl.program_id(0)
    pl.when(i < n_first)(lambda: phase(xp_ref[0], up_ref, zp_ref))
    pl.when(i >= n_first)(lambda: phase(xs_ref[...], us_ref, zs_ref))


def ln_in_proj(xp, xs, g, b, w_bf16, tm):
    nb, seq, _ = xp.shape
    tp, ts = nb * seq, xs.shape[0]
    n_first = tp // tm
    fixed = lambda i: (0, 0)
    return pl.pallas_call(
        functools.partial(_ln_in_proj_kernel, n_first=n_first),
        grid=((tp + ts) // tm,),
        in_specs=[_prompt_spec(tm, seq, D_MODEL, n_first), _two_phase_specs((tm, D_MODEL), n_first)[1]]
                 + [pl.BlockSpec((1, D_MODEL), fixed), pl.BlockSpec((1, D_MODEL), fixed),
                    pl.BlockSpec((D_MODEL, D_IN), fixed)],
        out_specs=[pl.BlockSpec((tm, D_MODEL), lambda i: (i, 0))]
                  + _two_phase_lane_block_specs(tm, n_first)
                  + _two_phase_specs((tm, 4 * D_HG), n_first),
        out_shape=[jax.ShapeDtypeStruct((tp + ts, D_MODEL), F32),
                   jax.ShapeDtypeStruct((D_S5 // LANES, tp, LANES), F32),
                   jax.ShapeDtypeStruct((D_S5 // LANES, ts, LANES), F32),
                   jax.ShapeDtypeStruct((tp, 4 * D_HG), F32), jax.ShapeDtypeStruct((ts, 4 * D_HG), F32)],
        compiler_params=_cparams(),
        name="ln_in_proj",
    )(xp, xs, g, b, w_bf16)


def _lane_block(j):
    return slice(j * LANES, (j + 1) * LANES)


def _s5_rows_kernel(u_ref, o_ref, *, chunks):
    lane = lax.broadcasted_iota(jnp.int32, (1, LANES), 1)
    per_block = LANES // S5_GROUP
    for gcol in range(D_S5 // LANES):
        for j in range(S5_CONV // LANES):
            slabs = [u_ref[gcol, pl.ds(per_block * j + sl, chunks, stride=CHUNK), :] for sl in range(per_block)]
            for gl in range(per_block):
                acc = jnp.zeros((chunks, LANES), F32)
                for sl in range(per_block):
                    shift = ((sl - gl) * S5_GROUP) % LANES
                    moved = slabs[sl] if shift == 0 else pltpu.roll(slabs[sl], shift, axis=1)
                    acc = jnp.where((lane >= S5_GROUP * sl) & (lane < S5_GROUP * (sl + 1)), moved, acc)
                o_ref[gcol * per_block + gl, :, _lane_block(j)] = acc.astype(BF16)


def s5_rows(u, chunks):
    nblk, t, _ = u.shape
    r = t // CHUNK
    return pl.pallas_call(
        functools.partial(_s5_rows_kernel, chunks=chunks),
        grid=(r // chunks,),
        in_specs=[pl.BlockSpec((nblk, chunks * CHUNK, LANES), lambda i: (0, i, 0))],
        out_specs=pl.BlockSpec((S5_GROUPS, chunks, S5_CONV), lambda i: (0, i, 0)),
        out_shape=jax.ShapeDtypeStruct((S5_GROUPS, r, S5_CONV), BF16),
        compiler_params=_cparams(),
        name="s5_rows",
    )(u)


def _s5_tokens_kernel(y_ref, o_ref, *, chunks):
    lane = lax.broadcasted_iota(jnp.int32, (1, LANES), 1)
    per_block = LANES // S5_GROUP
    for gcol in range(D_S5 // LANES):
        for j in range(S5_CONV // LANES):
            blocks = [y_ref[gcol * per_block + gl, :, _lane_block(j)].astype(F32) for gl in range(per_block)]
            for sl in range(per_block):
                acc = jnp.zeros((chunks, LANES), F32)
                for gl in range(per_block):
                    shift = ((gl - sl) * S5_GROUP) % LANES
                    moved = blocks[gl] if shift == 0 else pltpu.roll(blocks[gl], shift, axis=1)
                    acc = jnp.where((lane >= S5_GROUP * gl) & (lane < S5_GROUP * (gl + 1)), moved, acc)
                o_ref[gcol, pl.ds(per_block * j + sl, chunks, stride=CHUNK), :] = acc


def s5_tokens(y_rows, chunks):
    _, r, _ = y_rows.shape
    return pl.pallas_call(
        functools.partial(_s5_tokens_kernel, chunks=chunks),
        grid=(r // chunks,),
        in_specs=[pl.BlockSpec((S5_GROUPS, chunks, S5_CONV), lambda i: (0, i, 0))],
        out_specs=pl.BlockSpec((D_S5 // LANES, chunks * CHUNK, LANES), lambda i: (0, i, 0)),
        out_shape=jax.ShapeDtypeStruct((D_S5 // LANES, r * CHUNK, LANES), F32),
        compiler_params=_cparams(),
        name="s5_tokens",
    )(y_rows)


def _s5_prep_kernel(lrc_ref, lic_ref, lrr_ref, lir_ref, ls_ref, brt_ref, bit_ref,
                    brtt_ref, bitt_ref, crt_ref, cit_ref,
                    m_ref, w_ref, wc_ref, a_ref):
    step = jnp.exp(ls_ref[0])

    def discretise(lr_raw, li):
        lr = jnp.minimum(lr_raw, -1e-4)
        dr, di = lr * step, li * step
        mag = jnp.exp(dr)
        a_re, a_im = mag * jnp.cos(di), mag * jnp.sin(di)
        den = lr * lr + li * li
        nr = a_re - 1.0
        fr = (nr * lr + a_im * li) / den
        fi = (a_im * lr - nr * li) / den
        return dr, di, fr, fi

    dr_c, di_c, _, _ = discretise(lrc_ref[0], lic_ref[0])
    dr_r, di_r, fr_r, fi_r = discretise(lrr_ref[0], lir_ref[0])

    lane = lax.broadcasted_iota(jnp.int32, (1, S5_CONV), 1)
    t_row = lax.broadcasted_iota(jnp.int32, (1, CHUNK), 1).astype(F32)
    t_col = lax.broadcasted_iota(jnp.int32, (CHUNK, 1), 0).astype(F32)
    lag_of_lane = (lax.broadcasted_iota(jnp.int32, (CHUNK, S5_CONV), 1) // S5_GROUP
                   == lax.broadcasted_iota(jnp.int32, (CHUNK, S5_CONV), 0)).astype(F32)
    time_of_row = (lax.broadcasted_iota(jnp.int32, (S5_CONV, CHUNK), 0) // S5_GROUP
                   == lax.broadcasted_iota(jnp.int32, (S5_CONV, CHUNK), 1)).astype(F32)

    def c_times_power(tf):
        mag = jnp.exp(dr_c * tf)
        ang = di_c * tf
        pr = _dot(mag * jnp.cos(ang), lag_of_lane, HIGHEST)
        pi = _dot(mag * jnp.sin(ang), lag_of_lane, HIGHEST)
        ctr, cti = crt_ref[0], cit_ref[0]
        return ctr * pr - cti * pi, ctr * pi + cti * pr

    cpr, cpi = c_times_power(t_row)
    bbr = fr_r * brt_ref[0] - fi_r * bit_ref[0]
    bbi = fr_r * bit_ref[0] + fi_r * brt_ref[0]
    kt = _dot(bbr, cpr, HIGHEST) - _dot(bbi, cpi, HIGHEST)
    for s in range(CHUNK):
        shifted = kt if s == 0 else pltpu.roll(kt, S5_GROUP * s, axis=1)
        m_ref[0, S5_GROUP * s:S5_GROUP * (s + 1), :] = jnp.where(
            lane >= S5_GROUP * s, shifted, 0.0).astype(BF16)

    rem = CHUNK - 1.0 - t_col
    magw = jnp.exp(dr_r * rem)
    angw = di_r * rem
    pwr = _dot(time_of_row, magw * jnp.cos(angw), HIGHEST)
    pwi = _dot(time_of_row, magw * jnp.sin(angw), HIGHEST)
    bbtr = fr_r * brtt_ref[0] - fi_r * bitt_ref[0]
    bbti = fr_r * bitt_ref[0] + fi_r * brtt_ref[0]
    w_ref[0, :, :S5_STATE] = pwr * bbtr - pwi * bbti
    w_ref[0, :, S5_STATE:] = pwr * bbti + pwi * bbtr

    c1r, c1i = c_times_power(t_row + 1.0)
    wc_ref[0, :S5_STATE, :] = c1r.astype(BF16)
    wc_ref[0, S5_STATE:, :] = (-c1i).astype(BF16)

    full = float(CHUNK)
    mag_c = jnp.exp(dr_r * full)
    a_ref[0, 0:1, :] = mag_c * jnp.cos(di_r * full)
    a_ref[0, 1:2, :] = mag_c * jnp.sin(di_r * full)


def s5_prep(lam_re, lam_im, log_step, b_re, b_im, c_re, c_im):
    g, p = lam_re.shape
    brt = jnp.swapaxes(b_re, 1, 2)
    bit = jnp.swapaxes(b_im, 1, 2)
    crt = jnp.tile(jnp.swapaxes(c_re, 1, 2), (1, 1, CHUNK))
    cit = jnp.tile(jnp.swapaxes(c_im, 1, 2), (1, 1, CHUNK))
    args = (lam_re.reshape(g, p, 1), lam_im.reshape(g, p, 1),
            lam_re.reshape(g, 1, p), lam_im.reshape(g, 1, p), log_step.reshape(g, 1, 1),
            brt, bit, jnp.tile(brt, (1, CHUNK, 1)), jnp.tile(bit, (1, CHUNK, 1)), crt, cit)
    spec = lambda a: pl.BlockSpec((1,) + a.shape[1:], lambda i: (i, 0, 0))
    out_shape = [jax.ShapeDtypeStruct((g, S5_CONV, S5_CONV), BF16),
                 jax.ShapeDtypeStruct((g, S5_CONV, 2 * S5_STATE), F32),
                 jax.ShapeDtypeStruct((g, 2 * S5_STATE, S5_CONV), BF16),
                 jax.ShapeDtypeStruct((g, 2, S5_STATE), F32)]
    return pl.pallas_call(
        _s5_prep_kernel,
        grid=(g,),
        in_specs=[spec(a) for a in args],
        out_specs=[spec(o) for o in out_shape],
        out_shape=out_shape,
        compiler_params=_cparams(),
        name="s5_prep",
    )(*args)


def _split3(w):
    hi = w.astype(BF16)
    r1 = w - hi.astype(F32)
    mid = r1.astype(BF16)
    lo = (r1 - mid.astype(F32)).astype(BF16)
    return hi, mid, lo


def _s5_main_kernel(up_ref, us_ref, m_ref, w_ref, wc_ref, a_ref, xsr_ref, xsi_ref,
                    yp_ref, ys_ref, fpr_ref, fpi_ref, fsr_ref, fsi_ref,
                    hr_scr, hi_scr, x0r_scr, x0i_scr, *, n_prompt, n_chunks):
    w3 = _split3(w_ref[0])
    ar = a_ref[0, 0:1, :]
    ai = a_ref[0, 1:2, :]

    def local(u):
        hend = _dot(u, w3[0]) + _dot(u, w3[1]) + _dot(u, w3[2])
        return _dot(u, m_ref[0]), hend[:, :S5_STATE], hend[:, S5_STATE:]

    def carry_in(x0r, x0i):
        return (_dot(x0r.astype(BF16), wc_ref[0, :S5_STATE, :])
                + _dot(x0i.astype(BF16), wc_ref[0, S5_STATE:, :]))

    y_local, hr, hi = local(up_ref[0])
    hr_scr[...] = hr
    hi_scr[...] = hi
    xr = jnp.zeros((n_prompt, S5_STATE), F32)
    xi = jnp.zeros((n_prompt, S5_STATE), F32)
    for c in range(n_chunks):
        rows = pl.ds(c, n_prompt, stride=n_chunks)
        x0r_scr[rows, :] = xr
        x0i_scr[rows, :] = xi
        xr, xi = (ar * xr - ai * xi + hr_scr[rows, :],
                  ar * xi + ai * xr + hi_scr[rows, :])
    fpr_ref[0] = xr
    fpi_ref[0] = xi
    yp_ref[0] = (y_local + carry_in(x0r_scr[...], x0i_scr[...])).astype(BF16)

    y_local, hr, hi = local(us_ref[0])
    sr, si = xsr_ref[0], xsi_ref[0]
    fsr_ref[0] = ar * sr - ai * si + hr
    fsi_ref[0] = ar * si + ai * sr + hi
    ys_ref[0] = (y_local + carry_in(sr, si)).astype(BF16)


def s5_main(up_rows, us_rows, m, w, wc, a, xs_re, xs_im, n_prompt, n_chunks):
    g, r, _ = up_rows.shape
    n_sample = xs_re.shape[1]
    spec = lambda shape: pl.BlockSpec((1,) + tuple(shape[1:]), lambda i: (i, 0, 0))
    args = (up_rows, us_rows, m, w, wc, a, xs_re, xs_im)
    out_shape = [jax.ShapeDtypeStruct((g, r, S5_CONV), BF16),
                 jax.ShapeDtypeStruct((g, n_sample, S5_CONV), BF16),
                 jax.ShapeDtypeStruct((g, n_prompt, S5_STATE), F32),
                 jax.ShapeDtypeStruct((g, n_prompt, S5_STATE), F32),
                 jax.ShapeDtypeStruct((g, n_sample, S5_STATE), F32),
                 jax.ShapeDtypeStruct((g, n_sample, S5_STATE), F32)]
    return pl.pallas_call(
        functools.partial(_s5_main_kernel, n_prompt=n_prompt, n_chunks=n_chunks),
        grid=(g,),
        in_specs=[spec(x.shape) for x in args],
        out_specs=[spec(o.shape) for o in out_shape],
        out_shape=out_shape,
        scratch_shapes=[pltpu.VMEM((r, S5_STATE), F32)] * 4,
        compiler_params=_cparams(),
        name="s5_main",
    )(*args)


def _hgrn_kernel(z_ref, s0_ref, lb_ref, ng_ref, o_ref, sfin_ref, st_scr, *, n_seq, chunk):
    c = pl.program_id(1)

    @pl.when(c == 0)
    def _():
        st_scr[...] = s0_ref[...]

    lbw = lb_ref[...]
    lbe = jnp.exp(lbw - jnp.max(lbw, axis=0, keepdims=True))
    lb_all = lbe[0:1, :] / jnp.sum(lbe, axis=0, keepdims=True)

    levels = [chunk >> (i + 1) for i in range(chunk.bit_length() - 1)]
    rowi = lax.broadcasted_iota(jnp.int32, (chunk, chunk), 0)
    coli = lax.broadcasted_iota(jnp.int32, (chunk, chunk), 1)
    rowk = lax.broadcasted_iota(jnp.int32, (chunk, HG_D), 0)
    cum_rows = [(coli <= rowi).astype(F32)]
    upper, lower, sign, same_block = [], [], [], []
    for m in levels:
        ref_row = (rowi // (2 * m)) * (2 * m) + (m - 1)
        cum_rows.append((coli <= ref_row).astype(F32))
        in_upper = (rowk % (2 * m)) >= m
        upper.append(in_upper.astype(F32).astype(BF16))
        lower.append(1.0 - upper[-1])
        sign.append(jnp.where(in_upper, 1.0, -1.0))
        same_block.append((rowi // (2 * m)) == (coli // (2 * m)))
    cum_mat = jnp.concatenate(cum_rows, axis=0).astype(BF16)
    cum_mat3 = jnp.concatenate([cum_mat] * 3, axis=1)
    diag = rowi == coli
    nt = (((1,), (1,)), ((), ()))

    def body(n, carry):
        zf = z_ref[n, :, D_HG:2 * D_HG]
        fg_all = lb_all + (1.0 - lb_all) * jax.nn.sigmoid(zf)
        cums = _dot(cum_mat3, jnp.concatenate(_split3(jnp.log2(fg_all)), axis=0))
        for hd in range(HG_HEADS):
            cols = slice(hd * HG_D, (hd + 1) * HG_D)
            zq = z_ref[n, :, hd * HG_D:(hd + 1) * HG_D]
            v = z_ref[n, :, 2 * D_HG + hd * HG_D:2 * D_HG + (hd + 1) * HG_D]
            zg = z_ref[n, :, 3 * D_HG + hd * HG_D:3 * D_HG + (hd + 1) * HG_D]
            q = zq * jax.nn.sigmoid(zq)
            kk = 1.0 - fg_all[:, cols]
            bcum = cums[:chunk, cols]
            b_last = bcum[chunk - 1:chunk, :]
            vb = v.astype(BF16)
            qb = q.astype(BF16)
            kb = kk.astype(BF16)
            st = st_scr[n, hd]

            scores = jnp.where(diag, lax.dot_general(qb, kb, nt, preferred_element_type=F32), 0.0)
            for lvl in range(len(levels)):
                bref = cums[(lvl + 1) * chunk:(lvl + 2) * chunk, cols]
                dec = jnp.exp2((bcum - bref) * sign[lvl]).astype(BF16)
                sc = lax.dot_general(qb * upper[lvl] * dec, kb * lower[lvl] * dec, nt,
                                     preferred_element_type=F32)
                scores = scores + jnp.where(same_block[lvl], sc, 0.0)

            qd = (q * jnp.exp2(bcum)).astype(BF16)
            o = lax.dot_general(qd, st.astype(BF16), nt, preferred_element_type=F32)
            o = o + _dot(scores.astype(BF16), vb)
            kdec = (kk * jnp.exp2(b_last - bcum)).astype(BF16)
            st_scr[n, hd] = jnp.exp2(b_last) * st + lax.dot_general(
                vb, kdec, (((0,), (0,)), ((), ())), preferred_element_type=F32)

            on = o * lax.rsqrt(jnp.mean(o * o, axis=-1, keepdims=True) + RMS_EPS) * ng_ref[:, cols]
            o_ref[n, :, hd * HG_D:(hd + 1) * HG_D] = on * (zg * jax.nn.sigmoid(zg))
        return carry

    lax.fori_loop(0, n_seq, body, 0, unroll=True)

    @pl.when(c == pl.num_programs(1) - 1)
    def _():
        sfin_ref[...] = st_scr[...]


def hgrn(z, s0_t, hg_lb, norm_g, n_seq, chunk):
    n, length, _ = z.shape
    return pl.pallas_call(
        functools.partial(_hgrn_kernel, n_seq=n_seq, chunk=chunk),
        grid=(n // n_seq, length // chunk),
        in_specs=[pl.BlockSpec((n_seq, chunk, 4 * D_HG), lambda g, c: (g, c, 0)),
                  pl.BlockSpec((n_seq, HG_HEADS, HG_D, HG_D), lambda g, c: (g, 0, 0, 0)),
                  pl.BlockSpec(hg_lb.shape, lambda g, c: (0, 0)),
                  pl.BlockSpec((1, D_HG), lambda g, c: (0, 0))],
        out_specs=[pl.BlockSpec((n_seq, chunk, D_HG), lambda g, c: (g, c, 0)),
                   pl.BlockSpec((n_seq, HG_HEADS, HG_D, HG_D), lambda g, c: (g, 0, 0, 0))],
        out_shape=[jax.ShapeDtypeStruct((n, length, D_HG), F32),
                   jax.ShapeDtypeStruct((n, HG_HEADS, HG_D, HG_D), F32)],
        scratch_shapes=[pltpu.VMEM((n_seq, HG_HEADS, HG_D, HG_D), F32)],
        compiler_params=_cparams(2),
        name="hgrn",
    )(z, s0_t, hg_lb, norm_g)


def _post_mix_kernel(h_ref, up_ref, us_ref, yp_ref, ys_ref, hgp_ref, hgs_ref,
                     d_ref, wglu_ref, bglu_ref, s5g_ref, wout_ref, g1_ref, b1_ref, rwt_ref, rb_ref,
                     h1_ref, idx_ref, gate_ref, rank_ref, before_ref, cnt_ref, run_scr, *, tm, n_first):
    i = pl.program_id(0)

    @pl.when(i == 0)
    def _():
        run_scr[...] = jnp.zeros_like(run_scr)

    def phase(u_ref, y_ref, hg_ref):
        ys = _read_lane_blocks(y_ref) + d_ref[...] * _read_lane_blocks(u_ref)
        gl = 0.5 * ys * (1.0 + lax.erf(ys * (2.0 ** -0.5)))
        s5o = gl * jax.nn.sigmoid(_dot(gl.astype(BF16), wglu_ref[...]) + bglu_ref[...])
        s5o = s5o * lax.rsqrt(jnp.mean(s5o * s5o, axis=-1, keepdims=True) + RMS_EPS) * s5g_ref[...]
        mix = (_dot(s5o.astype(BF16), wout_ref[:D_S5, :])
               + _dot(hg_ref[...].astype(BF16), wout_ref[D_S5:, :]))
        h1 = _layernorm(DEEPNORM_ALPHA * h_ref[...] + mix, g1_ref[...], b1_ref[...])
        _write_row_tiles(h1_ref, h1, tm)

        h_hi, h_mid, _ = _split3(h1)
        w_hi, w_mid, _ = _split3(rwt_ref[...])
        nt = (((1,), (1,)), ((), ()))
        logits = (lax.dot_general(w_hi, h_hi, nt, preferred_element_type=F32)
                  + lax.dot_general(w_hi, h_mid, nt, preferred_element_type=F32)
                  + lax.dot_general(w_mid, h_hi, nt, preferred_element_type=F32)) + rb_ref[...]
        eid = lax.broadcasted_iota(jnp.int32, (N_EXPERTS, tm), 0)
        vals, idxs = [], []
        for _ in range(TOP_K):
            m = jnp.max(logits, axis=0, keepdims=True)
            ix = jnp.min(jnp.where(logits == m, eid, N_EXPERTS), axis=0, keepdims=True)
            vals.append(m)
            idxs.append(ix)
            logits = jnp.where(eid == ix, -jnp.inf, logits)
        exps = [jnp.exp(v - vals[0]) for v in vals]
        den = exps[0] + exps[1] + exps[2] + exps[3]

        onehot = jnp.zeros((N_EXPERTS, tm), F32)
        for ix in idxs:
            onehot = onehot + (eid == ix).astype(F32)
        rowi = lax.broadcasted_iota(jnp.int32, (tm, tm), 0)
        coli = lax.broadcasted_iota(jnp.int32, (tm, tm), 1)
        earlier = (rowi < coli).astype(BF16)
        prefix = _dot(onehot.astype(BF16), earlier)
        for k in range(TOP_K):
            idx_ref[k:k + 1, :] = idxs[k]
            gate_ref[k:k + 1, :] = exps[k] / den
            rank_ref[k:k + 1, :] = jnp.sum(jnp.where(eid == idxs[k], prefix, 0.0),
                                           axis=0, keepdims=True).astype(jnp.int32)
        before_ref[0] = run_scr[...]
        run_scr[...] = run_scr[...] + jnp.sum(onehot, axis=1, keepdims=True)
        cnt_ref[...] = run_scr[...]

    pl.when(i < n_first)(lambda: phase(up_ref, yp_ref, hgp_ref))
    pl.when(i >= n_first)(lambda: phase(us_ref, ys_ref, hgs_ref))


def post_mix(h0, u_pair, y_pair, hg_pair, d_skip, wglu, bglu, s5g, wout, g1, b1, rw_t, rb_col, tm):
    t = h0.shape[0]
    n_first = u_pair[0].shape[1] // tm
    row = lambda i: (i, 0)
    col = lambda i: (0, i)
    fixed = lambda i: (0, 0)
    full = lambda a: pl.BlockSpec(a.shape, fixed)
    weights = (d_skip, wglu, bglu, s5g, wout, g1, b1, rw_t, rb_col)
    return pl.pallas_call(
        functools.partial(_post_mix_kernel, tm=tm, n_first=n_first),
        grid=(t // tm,),
        in_specs=[pl.BlockSpec((tm, D_MODEL), row)]
                 + _two_phase_lane_block_specs(tm, n_first) * 2 + _two_phase_specs((tm, D_HG), n_first)
                 + [full(a) for a in weights],
        out_specs=[pl.BlockSpec((tm * ROW_TILES, LANES), row),
                   pl.BlockSpec((TOP_K, tm), col), pl.BlockSpec((TOP_K, tm), col),
                   pl.BlockSpec((TOP_K, tm), col),
                   pl.BlockSpec((1, N_EXPERTS, 1), lambda i: (i, 0, 0)),
                   pl.BlockSpec((N_EXPERTS, 1), fixed)],
        out_shape=[jax.ShapeDtypeStruct((t * ROW_TILES, LANES), F32),
                   jax.ShapeDtypeStruct((TOP_K, t), jnp.int32),
                   jax.ShapeDtypeStruct((TOP_K, t), F32),
                   jax.ShapeDtypeStruct((TOP_K, t), jnp.int32),
                   jax.ShapeDtypeStruct((t // tm, N_EXPERTS, 1), F32),
                   jax.ShapeDtypeStruct((N_EXPERTS, 1), F32)],
        scratch_shapes=[pltpu.VMEM((N_EXPERTS, 1), F32)],
        compiler_params=_cparams(),
        name="post_mix",
    )(h0, *u_pair, *y_pair, *hg_pair, *weights)


def _segment_copies(meta_ref, tile, tm, make_copy):
    for e in range(N_EXPERTS):
        sorted_row = meta_ref[tile, e]
        cnt = meta_ref[tile, N_EXPERTS + e]
        staged_row = meta_ref[tile, 2 * N_EXPERTS + e]
        for b in range(tm.bit_length()):
            done = cnt & ((1 << b) - 1)

            @pl.when(((cnt >> b) & 1) == 1)
            def _(b=b, done=done, e=e):
                make_copy(staged_row + done, sorted_row + done, 1 << b).start(priority=e % 2)


def _dispatch_kernel(meta_ref, pend_ref, slot_ref, h_ref, xs_ref, stage, zero_scr, zsem, sem, *, tm):
    n_rows = xs_ref.shape[0] // ROW_TILES
    i = pl.program_id(0)
    n = pl.num_programs(0)
    slot = i % 2

    def drain(s):
        pltpu.make_async_copy(stage.at[s], xs_ref.at[pl.ds(0, TOP_K * tm * ROW_TILES)], sem.at[s]).wait()

    @pl.when(i == 0)
    def _():
        zero_scr[...] = jnp.zeros_like(zero_scr)

        def last_block(e):
            prev = pend_ref[e - 1] if e > 0 else 0
            copy = pltpu.make_async_copy(
                zero_scr, xs_ref.at[pl.ds(pl.multiple_of(jnp.maximum(pend_ref[e] - MOE_ROWS, 0) * ROW_TILES,
                                                         ROW_TILES), MOE_ROWS * ROW_TILES)], zsem)
            return pend_ref[e] > prev, copy

        def tail_block(j):
            row0 = pend_ref[N_EXPERTS - 1] + j * MOE_ROWS
            copy = pltpu.make_async_copy(
                zero_scr, xs_ref.at[pl.ds(pl.multiple_of(jnp.minimum(row0, n_rows - MOE_ROWS) * ROW_TILES,
                                                         ROW_TILES), MOE_ROWS * ROW_TILES)], zsem)
            return row0 < n_rows, copy

        blocks = [last_block(e) for e in range(N_EXPERTS)] + [tail_block(j) for j in range(N_EXPERTS)]
        for used, copy in blocks:
            pl.when(used)(copy.start)
        for used, copy in blocks:
            pl.when(used)(copy.wait)

    pl.when(i >= 2)(lambda: drain(slot))

    def body(t, carry):
        row = h_ref[_row_tile(t), :]
        for k in range(TOP_K):
            stage[slot, _row_tile(slot_ref[k * tm + t]), :] = row
        return carry

    lax.fori_loop(0, tm, body, 0, unroll=8)
    _segment_copies(meta_ref, i, tm, lambda staged_row, sorted_row, rows: pltpu.make_async_copy(
        stage.at[slot, _row_tiles(staged_row, rows)], xs_ref.at[_row_tiles(sorted_row, rows)], sem.at[slot]))

    @pl.when(i == n - 1)
    def _():
        pl.when(n >= 2)(lambda: drain(1 - slot))
        drain(slot)


def dispatch(meta, pend, slots_flat, h1, n_rows, tm):
    t = h1.shape[0] // ROW_TILES
    grid_spec = pltpu.PrefetchScalarGridSpec(
        num_scalar_prefetch=2,
        grid=(t // tm,),
        in_specs=[pl.BlockSpec((TOP_K * tm,), lambda i, meta, pend: (i,), memory_space=pltpu.SMEM),
                  pl.BlockSpec((tm * ROW_TILES, LANES), lambda i, meta, pend: (i, 0))],
        out_specs=pl.BlockSpec(memory_space=pl.ANY),
        scratch_shapes=[pltpu.VMEM((2, TOP_K * tm * ROW_TILES, LANES), F32),
                        pltpu.VMEM((MOE_ROWS * ROW_TILES, LANES), F32),
                        pltpu.SemaphoreType.DMA(()), pltpu.SemaphoreType.DMA((2,))],
    )
    return pl.pallas_call(
        functools.partial(_dispatch_kernel, tm=tm),
        grid_spec=grid_spec,
        out_shape=jax.ShapeDtypeStruct((n_rows * ROW_TILES, LANES), F32),
        compiler_params=_cparams(),
        name="moe_dispatch",
    )(meta, pend, slots_flat, h1)


def _moe_ffn_kernel(be_ref, nu_ref, seg_ref, nxt_ref, x_ref, wg_ref, bg_ref, wu_ref, bu_ref, wd_ref, bd_ref,
                    y_ref, wbuf, wbf, sem):
    i = pl.program_id(0)
    hbm = (wg_ref, wu_ref, wd_ref)

    def weight_copies(expert, s):
        return [pltpu.make_async_copy(hbm[j].at[expert], wbuf.at[s, j], sem.at[s, j]) for j in range(3)]

    @pl.when((i == 0) | (be_ref[i] != be_ref[jnp.maximum(i - 1, 0)]))
    def _():
        s = seg_ref[i] % 2

        @pl.when(i == 0)
        def _():
            for c in weight_copies(be_ref[0], 0):
                c.start()

        for j, c in enumerate(weight_copies(be_ref[i], s)):
            c.wait()
            wbf[j] = wbuf[s, j].astype(BF16)

        @pl.when(nxt_ref[i] >= 0)
        def _():
            for c in weight_copies(nxt_ref[i], 1 - s):
                c.start()

    @pl.when(i < nu_ref[0])
    def _():
        x = _read_row_tiles(x_ref, MOE_ROWS).astype(BF16)
        gt = jnp.minimum(_dot(x, wbf[0]) + bg_ref[0], SWIGLU_LIMIT)
        up = jnp.clip(_dot(x, wbf[1]) + bu_ref[0], -SWIGLU_LIMIT, SWIGLU_LIMIT)
        hid = (up + 1.0) * (gt * jax.nn.sigmoid(SWIGLU_ALPHA * gt))
        _write_row_tiles(y_ref, _dot(hid.astype(BF16), wbf[2]) + bd_ref[0], MOE_ROWS)

    @pl.when(i >= nu_ref[0])
    def _():
        y_ref[...] = jnp.zeros_like(y_ref)


def moe_ffn(block_e, n_used, segment, next_e, xs, wg, bg, wu, bu, wd, bd):
    n_rows = xs.shape[0] // ROW_TILES
    n_blocks = n_rows // MOE_ROWS
    wsel = lambda i, be, nu, seg, nxt: (be[i], 0, 0)
    d_ff = wg.shape[-1]
    assert wg.shape[1:] == wu.shape[1:] == wd.shape[1:] == (D_MODEL, D_MODEL)
    anywhere = pl.BlockSpec(memory_space=pl.ANY)
    grid_spec = pltpu.PrefetchScalarGridSpec(
        num_scalar_prefetch=4,
        grid=(n_blocks,),
        in_specs=[pl.BlockSpec((MOE_ROWS * ROW_TILES, LANES),
                               lambda i, be, nu, seg, nxt: (jnp.minimum(i, nu[0] - 1), 0)),
                  anywhere, pl.BlockSpec((1, 1, d_ff), wsel),
                  anywhere, pl.BlockSpec((1, 1, d_ff), wsel),
                  anywhere, pl.BlockSpec((1, 1, D_MODEL), wsel)],
        out_specs=pl.BlockSpec((MOE_ROWS * ROW_TILES, LANES), lambda i, be, nu, seg, nxt: (i, 0)),
        scratch_shapes=[pltpu.VMEM((2, 3, D_MODEL, D_MODEL), F32), pltpu.VMEM((3, D_MODEL, D_MODEL), BF16),
                        pltpu.SemaphoreType.DMA((2, 3))],
    )
    return pl.pallas_call(
        _moe_ffn_kernel,
        grid_spec=grid_spec,
        out_shape=jax.ShapeDtypeStruct((n_rows * ROW_TILES, LANES), F32),
        compiler_params=_cparams(),
        name="moe_ffn",
    )(block_e, n_used, segment, next_e, xs, wg, bg, wu, bu, wd, bd)


def _combine_kernel(meta_ref, slot_ref, gate_ref, h_ref, pp_ref, ps_ref, yb_ref,
                    plew_ref, plegw_ref, g2_ref, b2_ref, outp_ref, outs_ref, buf, r_scr, sem,
                    *, tm, n_first):
    i = pl.program_id(0)
    n = pl.num_programs(0)
    slot = i % 2

    def fetch(tile, s):
        _segment_copies(meta_ref, tile, tm, lambda staged_row, sorted_row, rows: pltpu.make_async_copy(
            yb_ref.at[_row_tiles(sorted_row, rows)], buf.at[s, _row_tiles(staged_row, rows)], sem.at[s]))

    pl.when(i == 0)(lambda: fetch(0, 0))
    pl.when(i + 1 < n)(lambda: fetch(i + 1, 1 - slot))
    pltpu.make_async_copy(yb_ref.at[pl.ds(0, TOP_K * tm * ROW_TILES)], buf.at[slot], sem.at[slot]).wait()

    def body(t, carry):
        acc = DEEPNORM_ALPHA * h_ref[_row_tile(t), :]
        for k in range(TOP_K):
            acc = acc + gate_ref[k * tm + t] * buf[slot, _row_tile(slot_ref[k * tm + t]), :]
        r_scr[_row_tile(t), :] = acc
        return carry

    lax.fori_loop(0, tm, body, 0, unroll=8)
    r = _read_row_tiles(r_scr, tm)
    gate = jax.nn.sigmoid(_dot(r.astype(BF16), plegw_ref[...]))

    def finish(p, store):
        e = _dot(p.astype(BF16), plew_ref[...]) * gate
        store(_layernorm(r + e, g2_ref[...], b2_ref[...]))

    def store_prompt(v):
        outp_ref[0] = v

    def store_sample(v):
        outs_ref[...] = v

    pl.when(i < n_first)(lambda: finish(pp_ref[0], store_prompt))
    pl.when(i >= n_first)(lambda: finish(ps_ref[...], store_sample))


def combine(meta, slots_flat, gates_flat, h1, p_prompt, p_sample, yb, plew, plegw, g2, b2, tm):
    t = h1.shape[0] // ROW_TILES
    nb, seq, _ = p_prompt.shape
    ts = p_sample.shape[0]
    n_first = nb * seq // tm
    fixed = lambda i, meta: (0, 0)
    flat = pl.BlockSpec((TOP_K * tm,), lambda i, meta: (i,), memory_space=pltpu.SMEM)
    sample = lambda width: pl.BlockSpec((tm, width), lambda i, meta: (jnp.maximum(i - n_first, 0), 0))
    prompt = lambda width: pl.BlockSpec((1, tm, width), _prompt_spec(tm, seq, width, n_first).index_map)
    with_meta = lambda spec: pl.BlockSpec(spec.block_shape, lambda i, meta: spec.index_map(i))
    grid_spec = pltpu.PrefetchScalarGridSpec(
        num_scalar_prefetch=1,
        grid=(t // tm,),
        in_specs=[flat, flat,
                  pl.BlockSpec((tm * ROW_TILES, LANES), lambda i, meta: (i, 0)),
                  with_meta(prompt(PLE_DIM)), sample(PLE_DIM),
                  pl.BlockSpec(memory_space=pl.ANY),
                  pl.BlockSpec(plew.shape, fixed), pl.BlockSpec(plegw.shape, fixed),
                  pl.BlockSpec((1, D_MODEL), fixed), pl.BlockSpec((1, D_MODEL), fixed)],
        out_specs=[with_meta(prompt(D_MODEL)), sample(D_MODEL)],
        scratch_shapes=[pltpu.VMEM((2, TOP_K * tm * ROW_TILES, LANES), F32),
                        pltpu.VMEM((tm * ROW_TILES, LANES), F32),
                        pltpu.SemaphoreType.DMA((2,))],
    )
    return pl.pallas_call(
        functools.partial(_combine_kernel, tm=tm, n_first=n_first),
        grid_spec=grid_spec,
        out_shape=[jax.ShapeDtypeStruct((nb, seq, D_MODEL), F32), jax.ShapeDtypeStruct((ts, D_MODEL), F32)],
        compiler_params=_cparams(),
        name="moe_combine",
    )(meta, slots_flat, gates_flat, h1, p_prompt, p_sample, yb, plew, plegw, g2, b2)


def _row(v):
    return v.reshape(1, -1)


def kernel(x_prompt, x_sample, state_s5_re, state_s5_im, state_hgrn, p_prompt, p_sample, ln_in_g, ln_in_b, w_in, s5_lambda_re, s5_lambda_im, s5_log_step, s5_b_re, s5_b_im, s5_c_re, s5_c_im, s5_d, s5_w_glu, s5_b_glu, s5_norm_g, hg_lb, hg_norm_g, w_out, ln1_g, ln1_b, router_w, router_b, w_gate, b_gate, w_up, b_up, w_down, b_down, ple_w, ple_gate_w, ln2_g, ln2_b):
    nb, seq, _ = x_prompt.shape
    ns, dseq, _ = x_sample.shape
    assert dseq == CHUNK and seq % CHUNK == 0 and w_in.shape[0] == 1
    nc = seq // CHUNK
    tp, ts = nb * seq, ns * dseq
    t = tp + ts
    tm = 512 if (tp % 512 == 0 and ts % 512 == 0) else 256
    assert tp % tm == 0 and ts % tm == 0 and seq % tm == 0 and ns % nb == 0

    h0, u_p, u_s, z_p, z_s = ln_in_proj(
        x_prompt, x_sample.reshape(ts, D_MODEL),
        _row(ln_in_g), _row(ln_in_b), w_in[0].astype(BF16), tm)

    m, w, wc, a = s5_prep(s5_lambda_re[0], s5_lambda_im[0], s5_log_step[0],
                          s5_b_re[0], s5_b_im[0], s5_c_re[0], s5_c_im[0])
    rb = 64 if (nb * nc) % 64 == 0 else nc
    yp_rows, ys_rows, fpr, fpi, fsr, fsi = s5_main(s5_rows(u_p, rb), s5_rows(u_s, ns), m, w, wc, a,
                                                   jnp.swapaxes(state_s5_re[0], 0, 1),
                                                   jnp.swapaxes(state_s5_im[0], 0, 1), nb, nc)
    y_pair = (s5_tokens(yp_rows, rb), s5_tokens(ys_rows, ns))

    zero_state = jnp.zeros((nb, HG_HEADS, HG_D, HG_D), F32)
    ng = _row(hg_norm_g[0])
    o_p, st_p = hgrn(z_p.reshape(nb, seq, 4 * D_HG), zero_state, hg_lb, ng, nb,
                     HG_CHUNK if seq % HG_CHUNK == 0 else CHUNK)
    o_s, st_s = hgrn(z_s.reshape(ns, dseq, 4 * D_HG), jnp.swapaxes(state_hgrn[0], 2, 3),
                     hg_lb, ng, nb, dseq)

    h1, idx, gates, rank, before, counts = post_mix(
        h0, (u_p, u_s), y_pair, (o_p.reshape(tp, D_HG), o_s.reshape(ts, D_HG)),
        _row(s5_d[0]), s5_w_glu[0].astype(BF16), _row(s5_b_glu[0]),
        _row(s5_norm_g[0]), w_out[0].astype(BF16), _row(ln1_g[0]), _row(ln1_b[0]),
        router_w[0].T, router_b[0].reshape(N_EXPERTS, 1), tm)

    n_tiles = t // tm
    counts = counts[:, 0].astype(jnp.int32)
    before = before[:, :, 0].astype(jnp.int32)
    cnt = jnp.concatenate([before[1:], counts[None]], axis=0) - before
    padded = (counts + MOE_ROWS - 1) // MOE_ROWS * MOE_ROWS
    pend = jnp.cumsum(padded)
    staged = jnp.cumsum(cnt, axis=1) - cnt
    meta = jnp.concatenate([pend - padded + before, cnt, staged, jnp.zeros_like(cnt)], axis=1)
    experts = jnp.arange(N_EXPERTS, dtype=jnp.int32)
    staged_tok = jnp.repeat(staged, tm, axis=0)
    slots = jnp.sum(jnp.where(idx[..., None] == experts, staged_tok, 0), axis=-1) + rank

    def per_tile(a):
        return a.reshape(TOP_K, n_tiles, tm).transpose(1, 0, 2).reshape(-1)

    n_blocks = -(-t * TOP_K // MOE_ROWS) + N_EXPERTS
    n_used = (pend[-1] // MOE_ROWS).astype(jnp.int32)
    blk = jnp.arange(n_blocks, dtype=jnp.int32)
    blk = jnp.minimum(blk, n_used - 1)
    block_e = jnp.sum((pend[None, :] <= (blk * MOE_ROWS)[:, None]).astype(jnp.int32), axis=1)
    block_e = jnp.minimum(block_e, N_EXPERTS - 1)
    owns_rows = padded > 0
    segment = (jnp.cumsum(owns_rows.astype(jnp.int32)) - 1)[block_e]
    later = owns_rows[None, :] & (experts[None, :] > experts[:, None])
    next_owner = jnp.where(jnp.any(later, axis=1), jnp.argmax(later, axis=1), -1).astype(jnp.int32)
    next_e = next_owner[block_e]

    slots_flat = per_tile(slots)
    xs = dispatch(meta, pend, slots_flat, h1, n_blocks * MOE_ROWS, tm)
    yb = moe_ffn(block_e, n_used.reshape(1), segment, next_e, xs,
                 w_gate[0], b_gate[0][:, None, :], w_up[0], b_up[0][:, None, :],
                 w_down[0], b_down[0][:, None, :])
    out_p, out_s = combine(meta, slots_flat, per_tile(gates), h1, p_prompt[0], p_sample[0].reshape(ts, PLE_DIM),
                           yb, ple_w[0].astype(BF16), ple_gate_w[0].astype(BF16),
                           _row(ln2_g[0]), _row(ln2_b[0]), tm)

    def s5_state(f, n):
        return jnp.swapaxes(f, 0, 1).reshape(1, n, S5_GROUPS, S5_STATE)

    return (out_p, out_s.reshape(ns, dseq, D_MODEL),
            s5_state(fpr, nb), s5_state(fpi, nb), jnp.swapaxes(st_p, 2, 3)[None],
            s5_state(fsr, ns), s5_state(fsi, ns), jnp.swapaxes(st_s, 2, 3)[None])
```

```python
import functools

import jax
import jax.numpy as jnp
from jax import lax
from jax.experimental import pallas as pl
from jax.experimental.pallas import tpu as pltpu

F32 = jnp.float32
BF16 = jnp.bfloat16
HIGHEST = lax.Precision.HIGHEST

D_MODEL = 1024
CHUNK = 64
PLE_DIM = 256
D_S5 = 512
S5_GROUP = 16
S5_GROUPS = 32
S5_STATE = 64
D_HG = 512
HG_HEADS = 4
HG_D = 128
D_IN = D_S5 + 4 * D_HG
N_EXPERTS = 32
TOP_K = 4
SWIGLU_LIMIT = 7.0
SWIGLU_ALPHA = 1.702
DEEPNORM_ALPHA = 2.0 ** 0.25
LN_EPS = 1e-5
RMS_EPS = 1e-6

LANES = 128
SUBLANES = 8
ROW_TILES = D_MODEL // LANES
S5_CONV = CHUNK * S5_GROUP
HG_CHUNK = 128
MOE_ROWS = 512
VMEM_LIMIT = 56 * 1024 * 1024

assert ROW_TILES == SUBLANES


def _cparams(n_axes=1, flags=None):
    return pltpu.CompilerParams(dimension_semantics=("arbitrary",) * n_axes,
                                vmem_limit_bytes=VMEM_LIMIT, flags=flags)


def _dot(a, b, precision=None):
    return jnp.dot(a, b, preferred_element_type=F32, precision=precision)


def _layernorm(x, g, b):
    mu = jnp.mean(x, axis=-1, keepdims=True)
    xc = x - mu
    var = jnp.mean(xc * xc, axis=-1, keepdims=True)
    return xc * lax.rsqrt(var + LN_EPS) * g + b


def _two_phase_specs(block, n_first):
    nd = len(block)
    first = pl.BlockSpec(block, lambda i: (jnp.minimum(i, n_first - 1),) + (0,) * (nd - 1))
    second = pl.BlockSpec(block, lambda i: (jnp.maximum(i - n_first, 0),) + (0,) * (nd - 1))
    return [first, second]


def _two_phase_lane_block_specs(tm, n_first):
    nblk = D_S5 // LANES
    first = pl.BlockSpec((nblk, tm, LANES), lambda i: (0, jnp.minimum(i, n_first - 1), 0))
    second = pl.BlockSpec((nblk, tm, LANES), lambda i: (0, jnp.maximum(i - n_first, 0), 0))
    return [first, second]


def _read_lane_blocks(ref):
    return jnp.concatenate([ref[j] for j in range(D_S5 // LANES)], axis=1)


def _prompt_spec(tm, seq, width, n_first):
    per_seq = seq // tm

    def index(i):
        ic = jnp.minimum(i, n_first - 1)
        return (ic // per_seq, ic % per_seq, 0)

    return pl.BlockSpec((1, tm, width), index)


def _chunk(rows, j):
    return pl.ds(j, rows, stride=ROW_TILES)


def _read_row_tiles(ref, rows):
    return jnp.concatenate([ref[_chunk(rows, j), :] for j in range(ROW_TILES)], axis=1)


def _write_row_tiles(ref, val, rows):
    for j in range(ROW_TILES):
        ref[_chunk(rows, j), :] = val[:, j * LANES:(j + 1) * LANES]


def _row_tiles(r, n=1):
    return pl.ds(pl.multiple_of(r * ROW_TILES, ROW_TILES), n * ROW_TILES)


def _row_tile(r):
    return _row_tiles(r)


def _ln_in_proj_kernel(xp_ref, xs_ref, g_ref, b_ref, w_ref, h_ref, up_ref, us_ref, zp_ref, zs_ref,
                       *, n_first):
    def phase(x, u_ref, z_ref):
        h = _layernorm(x, g_ref[...], b_ref[...])
        h_ref[...] = h
        hb = h.astype(BF16)
        u = _dot(hb, w_ref[:, :D_S5])
        for j in range(D_S5 // LANES):
            u_ref[j] = u[:, j * LANES:(j + 1) * LANES]
        z_ref[...] = _dot(hb, w_ref[:, D_S5:])

    i = pl.program_id(0)
    pl.when(i < n_first)(lambda: phase(xp_ref[0], up_ref, zp_ref))
    pl.when(i >= n_first)(lambda: phase(xs_ref[...], us_ref, zs_ref))


def ln_in_proj(xp, xs, g, b, w_bf16, tm):
    nb, seq, _ = xp.shape
    tp, ts = nb * seq, xs.shape[0]
    n_first = tp // tm
    fixed = lambda i: (0, 0)
    return pl.pallas_call(
        functools.partial(_ln_in_proj_kernel, n_first=n_first),
        grid=((tp + ts) // tm,),
        in_specs=[_prompt_spec(tm, seq, D_MODEL, n_first), _two_phase_specs((tm, D_MODEL), n_first)[1]]
                 + [pl.BlockSpec((1, D_MODEL), fixed), pl.BlockSpec((1, D_MODEL), fixed),
                    pl.BlockSpec((D_MODEL, D_IN), fixed)],
        out_specs=[pl.BlockSpec((tm, D_MODEL), lambda i: (i, 0))]
                  + _two_phase_lane_block_specs(tm, n_first)
                  + _two_phase_specs((tm, 4 * D_HG), n_first),
        out_shape=[jax.ShapeDtypeStruct((tp + ts, D_MODEL), F32),
                   jax.ShapeDtypeStruct((D_S5 // LANES, tp, LANES), F32),
                   jax.ShapeDtypeStruct((D_S5 // LANES, ts, LANES), F32),
                   jax.ShapeDtypeStruct((tp, 4 * D_HG), F32), jax.ShapeDtypeStruct((ts, 4 * D_HG), F32)],
        compiler_params=_cparams(),
        name="ln_in_proj",
    )(xp, xs, g, b, w_bf16)


def _lane_block(j):
    return slice(j * LANES, (j + 1) * LANES)


def _granule_transpose(slabs):
    per_block = LANES // S5_GROUP
    granule = lax.broadcasted_iota(jnp.int32, (1, LANES), 1) // S5_GROUP
    x = list(slabs)
    for d in (4, 2, 1):
        keep = (granule & d) == 0
        y = [None] * per_block
        for i in range(per_block):
            if i & d == 0:
                y[i] = jnp.where(keep, x[i], pltpu.roll(x[i + d], d * S5_GROUP, axis=1))
                y[i + d] = jnp.where(keep, pltpu.roll(x[i], LANES - d * S5_GROUP, axis=1), x[i + d])
        x = y
    return x


def _s5_rows_kernel(u_ref, o_ref, *, chunks):
    per_block = LANES // S5_GROUP
    for gcol in range(D_S5 // LANES):
        for j in range(S5_CONV // LANES):
            by_time = [u_ref[gcol, pl.ds(per_block * j + sl, chunks, stride=CHUNK), :] for sl in range(per_block)]
            for gl, rows in enumerate(_granule_transpose(by_time)):
                o_ref[gcol * per_block + gl, :, _lane_block(j)] = rows.astype(BF16)


def s5_rows(u, chunks):
    nblk, t, _ = u.shape
    r = t // CHUNK
    return pl.pallas_call(
        functools.partial(_s5_rows_kernel, chunks=chunks),
        grid=(r // chunks,),
        in_specs=[pl.BlockSpec((nblk, chunks * CHUNK, LANES), lambda i: (0, i, 0))],
        out_specs=pl.BlockSpec((S5_GROUPS, chunks, S5_CONV), lambda i: (0, i, 0)),
        out_shape=jax.ShapeDtypeStruct((S5_GROUPS, r, S5_CONV), BF16),
        compiler_params=_cparams(),
        name="s5_rows",
    )(u)


def _s5_tokens_kernel(y_ref, o_ref, *, chunks):
    per_block = LANES // S5_GROUP
    for gcol in range(D_S5 // LANES):
        for j in range(S5_CONV // LANES):
            by_group = [y_ref[gcol * per_block + gl, :, _lane_block(j)].astype(F32) for gl in range(per_block)]
            for sl, rows in enumerate(_granule_transpose(by_group)):
                o_ref[gcol, pl.ds(per_block * j + sl, chunks, stride=CHUNK), :] = rows


def s5_tokens(y_rows, chunks):
    _, r, _ = y_rows.shape
    return pl.pallas_call(
        functools.partial(_s5_tokens_kernel, chunks=chunks),
        grid=(r // chunks,),
        in_specs=[pl.BlockSpec((S5_GROUPS, chunks, S5_CONV), lambda i: (0, i, 0))],
        out_specs=pl.BlockSpec((D_S5 // LANES, chunks * CHUNK, LANES), lambda i: (0, i, 0)),
        out_shape=jax.ShapeDtypeStruct((D_S5 // LANES, r * CHUNK, LANES), F32),
        compiler_params=_cparams(),
        name="s5_tokens",
    )(y_rows)


def _s5_prep_kernel(lrc_ref, lic_ref, lrr_ref, lir_ref, ls_ref, brt_ref, bit_ref,
                    brtt_ref, bitt_ref, crt_ref, cit_ref,
                    m_ref, w_ref, wc_ref, a_ref):
    step = jnp.exp(ls_ref[0])

    def discretise(lr_raw, li):
        lr = jnp.minimum(lr_raw, -1e-4)
        dr, di = lr * step, li * step
        mag = jnp.exp(dr)
        a_re, a_im = mag * jnp.cos(di), mag * jnp.sin(di)
        den = lr * lr + li * li
        nr = a_re - 1.0
        fr = (nr * lr + a_im * li) / den
        fi = (a_im * lr - nr * li) / den
        return dr, di, fr, fi

    dr_c, di_c, _, _ = discretise(lrc_ref[0], lic_ref[0])
    dr_r, di_r, fr_r, fi_r = discretise(lrr_ref[0], lir_ref[0])

    lane = lax.broadcasted_iota(jnp.int32, (1, S5_CONV), 1)
    t_row = lax.broadcasted_iota(jnp.int32, (1, CHUNK), 1).astype(F32)
    t_col = lax.broadcasted_iota(jnp.int32, (CHUNK, 1), 0).astype(F32)
    lag_of_lane = (lax.broadcasted_iota(jnp.int32, (CHUNK, S5_CONV), 1) // S5_GROUP
                   == lax.broadcasted_iota(jnp.int32, (CHUNK, S5_CONV), 0)).astype(F32)
    time_of_row = (lax.broadcasted_iota(jnp.int32, (S5_CONV, CHUNK), 0) // S5_GROUP
                   == lax.broadcasted_iota(jnp.int32, (S5_CONV, CHUNK), 1)).astype(F32)

    def c_times_power(tf):
        mag = jnp.exp(dr_c * tf)
        ang = di_c * tf
        pr = _dot(mag * jnp.cos(ang), lag_of_lane, HIGHEST)
        pi = _dot(mag * jnp.sin(ang), lag_of_lane, HIGHEST)
        ctr, cti = crt_ref[0], cit_ref[0]
        return ctr * pr - cti * pi, ctr * pi + cti * pr

    cpr, cpi = c_times_power(t_row)
    bbr = fr_r * brt_ref[0] - fi_r * bit_ref[0]
    bbi = fr_r * bit_ref[0] + fi_r * brt_ref[0]
    kt = _dot(bbr, cpr, HIGHEST) - _dot(bbi, cpi, HIGHEST)
    for s in range(CHUNK):
        shifted = kt if s == 0 else pltpu.roll(kt, S5_GROUP * s, axis=1)
        m_ref[0, S5_GROUP * s:S5_GROUP * (s + 1), :] = jnp.where(
            lane >= S5_GROUP * s, shifted, 0.0).astype(BF16)

    rem = CHUNK - 1.0 - t_col
    magw = jnp.exp(dr_r * rem)
    angw = di_r * rem
    pwr = _dot(time_of_row, magw * jnp.cos(angw), HIGHEST)
    pwi = _dot(time_of_row, magw * jnp.sin(angw), HIGHEST)
    bbtr = fr_r * brtt_ref[0] - fi_r * bitt_ref[0]
    bbti = fr_r * bitt_ref[0] + fi_r * brtt_ref[0]
    w_ref[0, :, :S5_STATE] = pwr * bbtr - pwi * bbti
    w_ref[0, :, S5_STATE:] = pwr * bbti + pwi * bbtr

    c1r, c1i = c_times_power(t_row + 1.0)
    wc_ref[0, :S5_STATE, :] = c1r.astype(BF16)
    wc_ref[0, S5_STATE:, :] = (-c1i).astype(BF16)

    full = float(CHUNK)
    mag_c = jnp.exp(dr_r * full)
    a_ref[0, 0:1, :] = mag_c * jnp.cos(di_r * full)
    a_ref[0, 1:2, :] = mag_c * jnp.sin(di_r * full)


def s5_prep(lam_re, lam_im, log_step, b_re, b_im, c_re, c_im):
    g, p = lam_re.shape
    brt = jnp.swapaxes(b_re, 1, 2)
    bit = jnp.swapaxes(b_im, 1, 2)
    crt = jnp.tile(jnp.swapaxes(c_re, 1, 2), (1, 1, CHUNK))
    cit = jnp.tile(jnp.swapaxes(c_im, 1, 2), (1, 1, CHUNK))
    args = (lam_re.reshape(g, p, 1), lam_im.reshape(g, p, 1),
            lam_re.reshape(g, 1, p), lam_im.reshape(g, 1, p), log_step.reshape(g, 1, 1),
            brt, bit, jnp.tile(brt, (1, CHUNK, 1)), jnp.tile(bit, (1, CHUNK, 1)), crt, cit)
    spec = lambda a: pl.BlockSpec((1,) + a.shape[1:], lambda i: (i, 0, 0))
    out_shape = [jax.ShapeDtypeStruct((g, S5_CONV, S5_CONV), BF16),
                 jax.ShapeDtypeStruct((g, S5_CONV, 2 * S5_STATE), F32),
                 jax.ShapeDtypeStruct((g, 2 * S5_STATE, S5_CONV), BF16),
                 jax.ShapeDtypeStruct((g, 2, S5_STATE), F32)]
    return pl.pallas_call(
        _s5_prep_kernel,
        grid=(g,),
        in_specs=[spec(a) for a in args],
        out_specs=[spec(o) for o in out_shape],
        out_shape=out_shape,
        compiler_params=_cparams(),
        name="s5_prep",
    )(*args)


def _split3(w):
    hi = w.astype(BF16)
    r1 = w - hi.astype(F32)
    mid = r1.astype(BF16)
    lo = (r1 - mid.astype(F32)).astype(BF16)
    return hi, mid, lo


def _s5_main_kernel(up_ref, us_ref, m_ref, w_ref, wc_ref, a_ref, xsr_ref, xsi_ref,
                    yp_ref, ys_ref, fpr_ref, fpi_ref, fsr_ref, fsi_ref,
                    hr_scr, hi_scr, x0r_scr, x0i_scr, *, n_prompt, n_chunks):
    w3 = _split3(w_ref[0])
    ar = a_ref[0, 0:1, :]
    ai = a_ref[0, 1:2, :]

    def local(u):
        hend = _dot(u, w3[0]) + _dot(u, w3[1]) + _dot(u, w3[2])
        return _dot(u, m_ref[0]), hend[:, :S5_STATE], hend[:, S5_STATE:]

    def carry_in(x0r, x0i):
        return (_dot(x0r.astype(BF16), wc_ref[0, :S5_STATE, :])
                + _dot(x0i.astype(BF16), wc_ref[0, S5_STATE:, :]))

    y_local, hr, hi = local(up_ref[0])
    hr_scr[...] = hr
    hi_scr[...] = hi
    xr = jnp.zeros((n_prompt, S5_STATE), F32)
    xi = jnp.zeros((n_prompt, S5_STATE), F32)
    for c in range(n_chunks):
        rows = pl.ds(c, n_prompt, stride=n_chunks)
        x0r_scr[rows, :] = xr
        x0i_scr[rows, :] = xi
        xr, xi = (ar * xr - ai * xi + hr_scr[rows, :],
                  ar * xi + ai * xr + hi_scr[rows, :])
    fpr_ref[0] = xr
    fpi_ref[0] = xi
    yp_ref[0] = (y_local + carry_in(x0r_scr[...], x0i_scr[...])).astype(BF16)

    y_local, hr, hi = local(us_ref[0])
    sr, si = xsr_ref[0], xsi_ref[0]
    fsr_ref[0] = ar * sr - ai * si + hr
    fsi_ref[0] = ar * si + ai * sr + hi
    ys_ref[0] = (y_local + carry_in(sr, si)).astype(BF16)


def s5_main(up_rows, us_rows, m, w, wc, a, xs_re, xs_im, n_prompt, n_chunks):
    g, r, _ = up_rows.shape
    n_sample = xs_re.shape[1]
    spec = lambda shape: pl.BlockSpec((1,) + tuple(shape[1:]), lambda i: (i, 0, 0))
    args = (up_rows, us_rows, m, w, wc, a, xs_re, xs_im)
    out_shape = [jax.ShapeDtypeStruct((g, r, S5_CONV), BF16),
                 jax.ShapeDtypeStruct((g, n_sample, S5_CONV), BF16),
                 jax.ShapeDtypeStruct((g, n_prompt, S5_STATE), F32),
                 jax.ShapeDtypeStruct((g, n_prompt, S5_STATE), F32),
                 jax.ShapeDtypeStruct((g, n_sample, S5_STATE), F32),
                 jax.ShapeDtypeStruct((g, n_sample, S5_STATE), F32)]
    return pl.pallas_call(
        functools.partial(_s5_main_kernel, n_prompt=n_prompt, n_chunks=n_chunks),
        grid=(g,),
        in_specs=[spec(x.shape) for x in args],
        out_specs=[spec(o.shape) for o in out_shape],
        out_shape=out_shape,
        scratch_shapes=[pltpu.VMEM((r, S5_STATE), F32)] * 4,
        compiler_params=_cparams(),
        name="s5_main",
    )(*args)


def _hgrn_kernel(z_ref, s0_ref, lb_ref, ng_ref, o_ref, sfin_ref, st_scr, *, n_seq, chunk):
    c = pl.program_id(1)

    @pl.when(c == 0)
    def _():
        st_scr[...] = s0_ref[...]

    lbw = lb_ref[...]
    lbe = jnp.exp(lbw - jnp.max(lbw, axis=0, keepdims=True))
    lb_all = lbe[0:1, :] / jnp.sum(lbe, axis=0, keepdims=True)

    levels = [chunk >> (i + 1) for i in range(chunk.bit_length() - 1)]
    rowi = lax.broadcasted_iota(jnp.int32, (chunk, chunk), 0)
    coli = lax.broadcasted_iota(jnp.int32, (chunk, chunk), 1)
    rowk = lax.broadcasted_iota(jnp.int32, (chunk, HG_D), 0)
    cum_rows = [(coli <= rowi).astype(F32)]
    upper, lower, sign, same_block = [], [], [], []
    for m in levels:
        ref_row = (rowi // (2 * m)) * (2 * m) + (m - 1)
        cum_rows.append((coli <= ref_row).astype(F32))
        in_upper = (rowk % (2 * m)) >= m
        upper.append(in_upper.astype(F32).astype(BF16))
        lower.append(1.0 - upper[-1])
        sign.append(jnp.where(in_upper, 1.0, -1.0))
        same_block.append((rowi // (2 * m)) == (coli // (2 * m)))
    cum_mat = jnp.concatenate(cum_rows, axis=0).astype(BF16)
    cum_mat3 = jnp.concatenate([cum_mat] * 3, axis=1)
    diag = rowi == coli
    nt = (((1,), (1,)), ((), ()))

    def body(n, carry):
        zf = z_ref[n, :, D_HG:2 * D_HG]
        fg_all = lb_all + (1.0 - lb_all) * jax.nn.sigmoid(zf)
        cums = _dot(cum_mat3, jnp.concatenate(_split3(jnp.log2(fg_all)), axis=0))
        for hd in range(HG_HEADS):
            cols = slice(hd * HG_D, (hd + 1) * HG_D)
            zq = z_ref[n, :, hd * HG_D:(hd + 1) * HG_D]
            v = z_ref[n, :, 2 * D_HG + hd * HG_D:2 * D_HG + (hd + 1) * HG_D]
            zg = z_ref[n, :, 3 * D_HG + hd * HG_D:3 * D_HG + (hd + 1) * HG_D]
            q = zq * jax.nn.sigmoid(zq)
            kk = 1.0 - fg_all[:, cols]
            bcum = cums[:chunk, cols]
            b_last = bcum[chunk - 1:chunk, :]
            vb = v.astype(BF16)
            qb = q.astype(BF16)
            kb = kk.astype(BF16)
            st = st_scr[n, hd]

            scores = jnp.where(diag, lax.dot_general(qb, kb, nt, preferred_element_type=F32), 0.0)
            for lvl in range(len(levels)):
                bref = cums[(lvl + 1) * chunk:(lvl + 2) * chunk, cols]
                dec = jnp.exp2((bcum - bref) * sign[lvl]).astype(BF16)
                sc = lax.dot_general(qb * upper[lvl] * dec, kb * lower[lvl] * dec, nt,
                                     preferred_element_type=F32)
                scores = scores + jnp.where(same_block[lvl], sc, 0.0)

            qd = (q * jnp.exp2(bcum)).astype(BF16)
            o = lax.dot_general(qd, st.astype(BF16), nt, preferred_element_type=F32)
            o = o + _dot(scores.astype(BF16), vb)
            kdec = (kk * jnp.exp2(b_last - bcum)).astype(BF16)
            st_scr[n, hd] = jnp.exp2(b_last) * st + lax.dot_general(
                vb, kdec, (((0,), (0,)), ((), ())), preferred_element_type=F32)

            on = o * lax.rsqrt(jnp.mean(o * o, axis=-1, keepdims=True) + RMS_EPS) * ng_ref[:, cols]
            o_ref[n, :, hd * HG_D:(hd + 1) * HG_D] = on * (zg * jax.nn.sigmoid(zg))
        return carry

    lax.fori_loop(0, n_seq, body, 0, unroll=True)

    @pl.when(c == pl.num_programs(1) - 1)
    def _():
        sfin_ref[...] = st_scr[...]


def hgrn(z, s0_t, hg_lb, norm_g, n_seq, chunk):
    n, length, _ = z.shape
    return pl.pallas_call(
        functools.partial(_hgrn_kernel, n_seq=n_seq, chunk=chunk),
        grid=(n // n_seq, length // chunk),
        in_specs=[pl.BlockSpec((n_seq, chunk, 4 * D_HG), lambda g, c: (g, c, 0)),
                  pl.BlockSpec((n_seq, HG_HEADS, HG_D, HG_D), lambda g, c: (g, 0, 0, 0)),
                  pl.BlockSpec(hg_lb.shape, lambda g, c: (0, 0)),
                  pl.BlockSpec((1, D_HG), lambda g, c: (0, 0))],
        out_specs=[pl.BlockSpec((n_seq, chunk, D_HG), lambda g, c: (g, c, 0)),
                   pl.BlockSpec((n_seq, HG_HEADS, HG_D, HG_D), lambda g, c: (g, 0, 0, 0))],
        out_shape=[jax.ShapeDtypeStruct((n, length, D_HG), F32),
                   jax.ShapeDtypeStruct((n, HG_HEADS, HG_D, HG_D), F32)],
        scratch_shapes=[pltpu.VMEM((n_seq, HG_HEADS, HG_D, HG_D), F32)],
        compiler_params=_cparams(2),
        name="hgrn",
    )(z, s0_t, hg_lb, norm_g)


def _post_mix_kernel(h_ref, up_ref, us_ref, yp_ref, ys_ref, hgp_ref, hgs_ref,
                     d_ref, wglu_ref, bglu_ref, s5g_ref, wout_ref, g1_ref, b1_ref, rwt_ref, rb_ref,
                     h1_ref, idx_ref, gate_ref, rank_ref, before_ref, cnt_ref, run_scr, *, tm, n_first):
    i = pl.program_id(0)

    @pl.when(i == 0)
    def _():
        run_scr[...] = jnp.zeros_like(run_scr)

    def phase(u_ref, y_ref, hg_ref):
        ys = _read_lane_blocks(y_ref) + d_ref[...] * _read_lane_blocks(u_ref)
        gl = 0.5 * ys * (1.0 + lax.erf(ys * (2.0 ** -0.5)))
        s5o = gl * jax.nn.sigmoid(_dot(gl.astype(BF16), wglu_ref[...]) + bglu_ref[...])
        s5o = s5o * lax.rsqrt(jnp.mean(s5o * s5o, axis=-1, keepdims=True) + RMS_EPS) * s5g_ref[...]
        mix = (_dot(s5o.astype(BF16), wout_ref[:D_S5, :])
               + _dot(hg_ref[...].astype(BF16), wout_ref[D_S5:, :]))
        h1 = _layernorm(DEEPNORM_ALPHA * h_ref[...] + mix, g1_ref[...], b1_ref[...])
        _write_row_tiles(h1_ref, h1, tm)

        h_hi, h_mid, _ = _split3(h1)
        w_hi, w_mid, _ = _split3(rwt_ref[...])
        nt = (((1,), (1,)), ((), ()))
        logits = (lax.dot_general(w_hi, h_hi, nt, preferred_element_type=F32)
                  + lax.dot_general(w_hi, h_mid, nt, preferred_element_type=F32)
                  + lax.dot_general(w_mid, h_hi, nt, preferred_element_type=F32)) + rb_ref[...]
        eid = lax.broadcasted_iota(jnp.int32, (N_EXPERTS, tm), 0)
        vals, idxs = [], []
        for _ in range(TOP_K):
            m = jnp.max(logits, axis=0, keepdims=True)
            ix = jnp.min(jnp.where(logits == m, eid, N_EXPERTS), axis=0, keepdims=True)
            vals.append(m)
            idxs.append(ix)
            logits = jnp.where(eid == ix, -jnp.inf, logits)
        exps = [jnp.exp(v - vals[0]) for v in vals]
        den = exps[0] + exps[1] + exps[2] + exps[3]

        onehot = jnp.zeros((N_EXPERTS, tm), F32)
        for ix in idxs:
            onehot = onehot + (eid == ix).astype(F32)
        rowi = lax.broadcasted_iota(jnp.int32, (tm, tm), 0)
        coli = lax.broadcasted_iota(jnp.int32, (tm, tm), 1)
        earlier = (rowi < coli).astype(BF16)
        prefix = _dot(onehot.astype(BF16), earlier)
        for k in range(TOP_K):
            idx_ref[k:k + 1, :] = idxs[k]
            gate_ref[k:k + 1, :] = exps[k] / den
            rank_ref[k:k + 1, :] = jnp.sum(jnp.where(eid == idxs[k], prefix, 0.0),
                                           axis=0, keepdims=True).astype(jnp.int32)
        before_ref[0] = run_scr[...]
        run_scr[...] = run_scr[...] + jnp.sum(onehot, axis=1, keepdims=True)
        cnt_ref[...] = run_scr[...]

    pl.when(i < n_first)(lambda: phase(up_ref, yp_ref, hgp_ref))
    pl.when(i >= n_first)(lambda: phase(us_ref, ys_ref, hgs_ref))


def post_mix(h0, u_pair, y_pair, hg_pair, d_skip, wglu, bglu, s5g, wout, g1, b1, rw_t, rb_col, tm):
    t = h0.shape[0]
    n_first = u_pair[0].shape[1] // tm
    row = lambda i: (i, 0)
    col = lambda i: (0, i)
    fixed = lambda i: (0, 0)
    full = lambda a: pl.BlockSpec(a.shape, fixed)
    weights = (d_skip, wglu, bglu, s5g, wout, g1, b1, rw_t, rb_col)
    return pl.pallas_call(
        functools.partial(_post_mix_kernel, tm=tm, n_first=n_first),
        grid=(t // tm,),
        in_specs=[pl.BlockSpec((tm, D_MODEL), row)]
                 + _two_phase_lane_block_specs(tm, n_first) * 2 + _two_phase_specs((tm, D_HG), n_first)
                 + [full(a) for a in weights],
        out_specs=[pl.BlockSpec((tm * ROW_TILES, LANES), row),
                   pl.BlockSpec((TOP_K, tm), col), pl.BlockSpec((TOP_K, tm), col),
                   pl.BlockSpec((TOP_K, tm), col),
                   pl.BlockSpec((1, N_EXPERTS, 1), lambda i: (i, 0, 0)),
                   pl.BlockSpec((N_EXPERTS, 1), fixed)],
        out_shape=[jax.ShapeDtypeStruct((t * ROW_TILES, LANES), F32),
                   jax.ShapeDtypeStruct((TOP_K, t), jnp.int32),
                   jax.ShapeDtypeStruct((TOP_K, t), F32),
                   jax.ShapeDtypeStruct((TOP_K, t), jnp.int32),
                   jax.ShapeDtypeStruct((t // tm, N_EXPERTS, 1), F32),
                   jax.ShapeDtypeStruct((N_EXPERTS, 1), F32)],
        scratch_shapes=[pltpu.VMEM((N_EXPERTS, 1), F32)],
        compiler_params=_cparams(),
        name="post_mix",
    )(h0, *u_pair, *y_pair, *hg_pair, *weights)


def _segment_copies(meta_ref, tile, tm, make_copy):
    for e in range(N_EXPERTS):
        sorted_row = meta_ref[tile, e]
        cnt = meta_ref[tile, N_EXPERTS + e]
        staged_row = meta_ref[tile, 2 * N_EXPERTS + e]
        for b in range(tm.bit_length()):
            done = cnt & ((1 << b) - 1)

            @pl.when(((cnt >> b) & 1) == 1)
            def _(b=b, done=done, e=e):
                make_copy(staged_row + done, sorted_row + done, 1 << b).start(priority=e % 2)


def _dispatch_kernel(meta_ref, pend_ref, slot_ref, h_ref, xs_ref, stage, zero_scr, zsem, sem, *, tm):
    n_rows = xs_ref.shape[0] // ROW_TILES
    i = pl.program_id(0)
    n = pl.num_programs(0)
    slot = i % 2

    def drain(s):
        pltpu.make_async_copy(stage.at[s], xs_ref.at[pl.ds(0, TOP_K * tm * ROW_TILES)], sem.at[s]).wait()

    @pl.when(i == 0)
    def _():
        zero_scr[...] = jnp.zeros_like(zero_scr)

        def last_block(e):
            prev = pend_ref[e - 1] if e > 0 else 0
            copy = pltpu.make_async_copy(
                zero_scr, xs_ref.at[pl.ds(pl.multiple_of(jnp.maximum(pend_ref[e] - MOE_ROWS, 0) * ROW_TILES,
                                                         ROW_TILES), MOE_ROWS * ROW_TILES)], zsem)
            return pend_ref[e] > prev, copy

        def tail_block(j):
            row0 = pend_ref[N_EXPERTS - 1] + j * MOE_ROWS
            copy = pltpu.make_async_copy(
                zero_scr, xs_ref.at[pl.ds(pl.multiple_of(jnp.minimum(row0, n_rows - MOE_ROWS) * ROW_TILES,
                                                         ROW_TILES), MOE_ROWS * ROW_TILES)], zsem)
            return row0 < n_rows, copy

        blocks = [last_block(e) for e in range(N_EXPERTS)] + [tail_block(j) for j in range(N_EXPERTS)]
        for used, copy in blocks:
            pl.when(used)(copy.start)
        for used, copy in blocks:
            pl.when(used)(copy.wait)

    pl.when(i >= 2)(lambda: drain(slot))

    def body(t, carry):
        row = h_ref[_row_tile(t), :]
        for k in range(TOP_K):
            stage[slot, _row_tile(slot_ref[k * tm + t]), :] = row
        return carry

    lax.fori_loop(0, tm, body, 0, unroll=8)
    _segment_copies(meta_ref, i, tm, lambda staged_row, sorted_row, rows: pltpu.make_async_copy(
        stage.at[slot, _row_tiles(staged_row, rows)], xs_ref.at[_row_tiles(sorted_row, rows)], sem.at[slot]))

    @pl.when(i == n - 1)
    def _():
        pl.when(n >= 2)(lambda: drain(1 - slot))
        drain(slot)


def dispatch(meta, pend, slots_flat, h1, n_rows, tm):
    t = h1.shape[0] // ROW_TILES
    grid_spec = pltpu.PrefetchScalarGridSpec(
        num_scalar_prefetch=2,
        grid=(t // tm,),
        in_specs=[pl.BlockSpec((TOP_K * tm,), lambda i, meta, pend: (i,), memory_space=pltpu.SMEM),
                  pl.BlockSpec((tm * ROW_TILES, LANES), lambda i, meta, pend: (i, 0))],
        out_specs=pl.BlockSpec(memory_space=pl.ANY),
        scratch_shapes=[pltpu.VMEM((2, TOP_K * tm * ROW_TILES, LANES), F32),
                        pltpu.VMEM((MOE_ROWS * ROW_TILES, LANES), F32),
                        pltpu.SemaphoreType.DMA(()), pltpu.SemaphoreType.DMA((2,))],
    )
    return pl.pallas_call(
        functools.partial(_dispatch_kernel, tm=tm),
        grid_spec=grid_spec,
        out_shape=jax.ShapeDtypeStruct((n_rows * ROW_TILES, LANES), F32),
        compiler_params=_cparams(),
        name="moe_dispatch",
    )(meta, pend, slots_flat, h1)


def _moe_ffn_kernel(be_ref, nu_ref, seg_ref, nxt_ref, x_ref, wg_ref, bg_ref, wu_ref, bu_ref, wd_ref, bd_ref,
                    y_ref, wbuf, wbf, sem):
    i = pl.program_id(0)
    hbm = (wg_ref, wu_ref, wd_ref)

    def weight_copies(expert, s):
        return [pltpu.make_async_copy(hbm[j].at[expert], wbuf.at[s, j], sem.at[s, j]) for j in range(3)]

    @pl.when((i == 0) | (be_ref[i] != be_ref[jnp.maximum(i - 1, 0)]))
    def _():
        s = seg_ref[i] % 2

        @pl.when(i == 0)
        def _():
            for c in weight_copies(be_ref[0], 0):
                c.start()

        for j, c in enumerate(weight_copies(be_ref[i], s)):
            c.wait()
            wbf[j] = wbuf[s, j].astype(BF16)

        @pl.when(nxt_ref[i] >= 0)
        def _():
            for c in weight_copies(nxt_ref[i], 1 - s):
                c.start()

    @pl.when(i < nu_ref[0])
    def _():
        x = _read_row_tiles(x_ref, MOE_ROWS).astype(BF16)
        gt = jnp.minimum(_dot(x, wbf[0]) + bg_ref[0], SWIGLU_LIMIT)
        up = jnp.clip(_dot(x, wbf[1]) + bu_ref[0], -SWIGLU_LIMIT, SWIGLU_LIMIT)
        hid = (up + 1.0) * (gt * jax.nn.sigmoid(SWIGLU_ALPHA * gt))
        _write_row_tiles(y_ref, _dot(hid.astype(BF16), wbf[2]) + bd_ref[0], MOE_ROWS)

    @pl.when(i >= nu_ref[0])
    def _():
        y_ref[...] = jnp.zeros_like(y_ref)


def moe_ffn(block_e, n_used, segment, next_e, xs, wg, bg, wu, bu, wd, bd):
    n_rows = xs.shape[0] // ROW_TILES
    n_blocks = n_rows // MOE_ROWS
    wsel = lambda i, be, nu, seg, nxt: (be[i], 0, 0)
    d_ff = wg.shape[-1]
    assert wg.shape[1:] == wu.shape[1:] == wd.shape[1:] == (D_MODEL, D_MODEL)
    anywhere = pl.BlockSpec(memory_space=pl.ANY)
    grid_spec = pltpu.PrefetchScalarGridSpec(
        num_scalar_prefetch=4,
        grid=(n_blocks,),
        in_specs=[pl.BlockSpec((MOE_ROWS * ROW_TILES, LANES),
                               lambda i, be, nu, seg, nxt: (jnp.minimum(i, nu[0] - 1), 0)),
                  anywhere, pl.BlockSpec((1, 1, d_ff), wsel),
                  anywhere, pl.BlockSpec((1, 1, d_ff), wsel),
                  anywhere, pl.BlockSpec((1, 1, D_MODEL), wsel)],
        out_specs=pl.BlockSpec((MOE_ROWS * ROW_TILES, LANES), lambda i, be, nu, seg, nxt: (i, 0)),
        scratch_shapes=[pltpu.VMEM((2, 3, D_MODEL, D_MODEL), F32), pltpu.VMEM((3, D_MODEL, D_MODEL), BF16),
                        pltpu.SemaphoreType.DMA((2, 3))],
    )
    return pl.pallas_call(
        _moe_ffn_kernel,
        grid_spec=grid_spec,
        out_shape=jax.ShapeDtypeStruct((n_rows * ROW_TILES, LANES), F32),
        compiler_params=_cparams(),
        name="moe_ffn",
    )(block_e, n_used, segment, next_e, xs, wg, bg, wu, bu, wd, bd)


def _combine_kernel(meta_ref, slot_ref, gate_ref, h_ref, pp_ref, ps_ref, yb_ref,
                    plew_ref, plegw_ref, g2_ref, b2_ref, outp_ref, outs_ref, buf, r_scr, sem,
                    *, tm, n_first):
    i = pl.program_id(0)
    n = pl.num_programs(0)
    slot = i % 2

    def fetch(tile, s):
        _segment_copies(meta_ref, tile, tm, lambda staged_row, sorted_row, rows: pltpu.make_async_copy(
            yb_ref.at[_row_tiles(sorted_row, rows)], buf.at[s, _row_tiles(staged_row, rows)], sem.at[s]))

    pl.when(i == 0)(lambda: fetch(0, 0))
    pl.when(i + 1 < n)(lambda: fetch(i + 1, 1 - slot))
    pltpu.make_async_copy(yb_ref.at[pl.ds(0, TOP_K * tm * ROW_TILES)], buf.at[slot], sem.at[slot]).wait()

    def body(t, carry):
        acc = DEEPNORM_ALPHA * h_ref[_row_tile(t), :]
        for k in range(TOP_K):
            acc = acc + gate_ref[k * tm + t] * buf[slot, _row_tile(slot_ref[k * tm + t]), :]
        r_scr[_row_tile(t), :] = acc
        return carry

    lax.fori_loop(0, tm, body, 0, unroll=8)
    r = _read_row_tiles(r_scr, tm)
    gate = jax.nn.sigmoid(_dot(r.astype(BF16), plegw_ref[...]))

    def finish(p, store):
        e = _dot(p.astype(BF16), plew_ref[...]) * gate
        store(_layernorm(r + e, g2_ref[...], b2_ref[...]))

    def store_prompt(v):
        outp_ref[0] = v

    def store_sample(v):
        outs_ref[...] = v

    pl.when(i < n_first)(lambda: finish(pp_ref[0], store_prompt))
    pl.when(i >= n_first)(lambda: finish(ps_ref[...], store_sample))


def combine(meta, slots_flat, gates_flat, h1, p_prompt, p_sample, yb, plew, plegw, g2, b2, tm):
    t = h1.shape[0] // ROW_TILES
    nb, seq, _ = p_prompt.shape
    ts = p_sample.shape[0]
    n_first = nb * seq // tm
    fixed = lambda i, meta: (0, 0)
    flat = pl.BlockSpec((TOP_K * tm,), lambda i, meta: (i,), memory_space=pltpu.SMEM)
    sample = lambda width: pl.BlockSpec((tm, width), lambda i, meta: (jnp.maximum(i - n_first, 0), 0))
    prompt = lambda width: pl.BlockSpec((1, tm, width), _prompt_spec(tm, seq, width, n_first).index_map)
    with_meta = lambda spec: pl.BlockSpec(spec.block_shape, lambda i, meta: spec.index_map(i))
    grid_spec = pltpu.PrefetchScalarGridSpec(
        num_scalar_prefetch=1,
        grid=(t // tm,),
        in_specs=[flat, flat,
                  pl.BlockSpec((tm * ROW_TILES, LANES), lambda i, meta: (i, 0)),
                  with_meta(prompt(PLE_DIM)), sample(PLE_DIM),
                  pl.BlockSpec(memory_space=pl.ANY),
                  pl.BlockSpec(plew.shape, fixed), pl.BlockSpec(plegw.shape, fixed),
                  pl.BlockSpec((1, D_MODEL), fixed), pl.BlockSpec((1, D_MODEL), fixed)],
        out_specs=[with_meta(prompt(D_MODEL)), sample(D_MODEL)],
        scratch_shapes=[pltpu.VMEM((2, TOP_K * tm * ROW_TILES, LANES), F32),
                        pltpu.VMEM((tm * ROW_TILES, LANES), F32),
                        pltpu.SemaphoreType.DMA((2,))],
    )
    return pl.pallas_call(
        functools.partial(_combine_kernel, tm=tm, n_first=n_first),
        grid_spec=grid_spec,
        out_shape=[jax.ShapeDtypeStruct((nb, seq, D_MODEL), F32), jax.ShapeDtypeStruct((ts, D_MODEL), F32)],
        compiler_params=_cparams(),
        name="moe_combine",
    )(meta, slots_flat, gates_flat, h1, p_prompt, p_sample, yb, plew, plegw, g2, b2)


def _row(v):
    return v.reshape(1, -1)


def kernel(x_prompt, x_sample, state_s5_re, state_s5_im, state_hgrn, p_prompt, p_sample, ln_in_g, ln_in_b, w_in, s5_lambda_re, s5_lambda_im, s5_log_step, s5_b_re, s5_b_im, s5_c_re, s5_c_im, s5_d, s5_w_glu, s5_b_glu, s5_norm_g, hg_lb, hg_norm_g, w_out, ln1_g, ln1_b, router_w, router_b, w_gate, b_gate, w_up, b_up, w_down, b_down, ple_w, ple_gate_w, ln2_g, ln2_b):
    nb, seq, _ = x_prompt.shape
    ns, dseq, _ = x_sample.shape
    assert dseq == CHUNK and seq % CHUNK == 0 and w_in.shape[0] == 1
    nc = seq // CHUNK
    tp, ts = nb * seq, ns * dseq
    t = tp + ts
    tm = 512 if (tp % 512 == 0 and ts % 512 == 0) else 256
    assert tp % tm == 0 and ts % tm == 0 and seq % tm == 0 and ns % nb == 0

    h0, u_p, u_s, z_p, z_s = ln_in_proj(
        x_prompt, x_sample.reshape(ts, D_MODEL),
        _row(ln_in_g), _row(ln_in_b), w_in[0].astype(BF16), tm)

    m, w, wc, a = s5_prep(s5_lambda_re[0], s5_lambda_im[0], s5_log_step[0],
                          s5_b_re[0], s5_b_im[0], s5_c_re[0], s5_c_im[0])
    rb = 64 if (nb * nc) % 64 == 0 else nc
    yp_rows, ys_rows, fpr, fpi, fsr, fsi = s5_main(s5_rows(u_p, rb), s5_rows(u_s, ns), m, w, wc, a,
                                                   jnp.swapaxes(state_s5_re[0], 0, 1),
                                                   jnp.swapaxes(state_s5_im[0], 0, 1), nb, nc)
    y_pair = (s5_tokens(yp_rows, rb), s5_tokens(ys_rows, ns))

    zero_state = jnp.zeros((nb, HG_HEADS, HG_D, HG_D), F32)
    ng = _row(hg_norm_g[0])
    o_p, st_p = hgrn(z_p.reshape(nb, seq, 4 * D_HG), zero_state, hg_lb, ng, nb,
                     HG_CHUNK if seq % HG_CHUNK == 0 else CHUNK)
    o_s, st_s = hgrn(z_s.reshape(ns, dseq, 4 * D_HG), jnp.swapaxes(state_hgrn[0], 2, 3),
                     hg_lb, ng, nb, dseq)

    h1, idx, gates, rank, before, counts = post_mix(
        h0, (u_p, u_s), y_pair, (o_p.reshape(tp, D_HG), o_s.reshape(ts, D_HG)),
        _row(s5_d[0]), s5_w_glu[0].astype(BF16), _row(s5_b_glu[0]),
        _row(s5_norm_g[0]), w_out[0].astype(BF16), _row(ln1_g[0]), _row(ln1_b[0]),
        router_w[0].T, router_b[0].reshape(N_EXPERTS, 1), tm)

    n_tiles = t // tm
    counts = counts[:, 0].astype(jnp.int32)
    before = before[:, :, 0].astype(jnp.int32)
    cnt = jnp.concatenate([before[1:], counts[None]], axis=0) - before
    padded = (counts + MOE_ROWS - 1) // MOE_ROWS * MOE_ROWS
    pend = jnp.cumsum(padded)
    staged = jnp.cumsum(cnt, axis=1) - cnt
    meta = jnp.concatenate([pend - padded + before, cnt, staged, jnp.zeros_like(cnt)], axis=1)
    experts = jnp.arange(N_EXPERTS, dtype=jnp.int32)
    staged_tok = jnp.repeat(staged, tm, axis=0)
    slots = jnp.sum(jnp.where(idx[..., None] == experts, staged_tok, 0), axis=-1) + rank

    def per_tile(a):
        return a.reshape(TOP_K, n_tiles, tm).transpose(1, 0, 2).reshape(-1)

    n_blocks = -(-t * TOP_K // MOE_ROWS) + N_EXPERTS
    n_used = (pend[-1] // MOE_ROWS).astype(jnp.int32)
    blk = jnp.arange(n_blocks, dtype=jnp.int32)
    blk = jnp.minimum(blk, n_used - 1)
    block_e = jnp.sum((pend[None, :] <= (blk * MOE_ROWS)[:, None]).astype(jnp.int32), axis=1)
    block_e = jnp.minimum(block_e, N_EXPERTS - 1)
    owns_rows = padded > 0
    segment = (jnp.cumsum(owns_rows.astype(jnp.int32)) - 1)[block_e]
    later = owns_rows[None, :] & (experts[None, :] > experts[:, None])
    next_owner = jnp.where(jnp.any(later, axis=1), jnp.argmax(later, axis=1), -1).astype(jnp.int32)
    next_e = next_owner[block_e]

    slots_flat = per_tile(slots)
    xs = dispatch(meta, pend, slots_flat, h1, n_blocks * MOE_ROWS, tm)
    yb = moe_ffn(block_e, n_used.reshape(1), segment, next_e, xs,
                 w_gate[0], b_gate[0][:, None, :], w_up[0], b_up[0][:, None, :],
                 w_down[0], b_down[0][:, None, :])
    out_p, out_s = combine(meta, slots_flat, per_tile(gates), h1, p_prompt[0], p_sample[0].reshape(ts, PLE_DIM),
                           yb, ple_w[0].astype(BF16), ple_gate_w[0].astype(BF16),
                           _row(ln2_g[0]), _row(ln2_b[0]), tm)

    def s5_state(f, n):
        return jnp.swapaxes(f, 0, 1).reshape(1, n, S5_GROUPS, S5_STATE)

    return (out_p, out_s.reshape(ns, dseq, D_MODEL),
            s5_state(fpr, nb), s5_state(fpi, nb), jnp.swapaxes(st_p, 2, 3)[None],
            s5_state(fsr, ns), s5_state(fsi, ns), jnp.swapaxes(st_s, 2, 3)[None])
```

```python
import functools

import jax
import jax.numpy as jnp
from jax import lax
from jax.experimental import pallas as pl
from jax.experimental.pallas import tpu as pltpu

F32 = jnp.float32
BF16 = jnp.bfloat16
HIGHEST = lax.Precision.HIGHEST

D_MODEL = 1024
CHUNK = 64
PLE_DIM = 256
D_S5 = 512
S5_GROUP = 16
S5_GROUPS = 32
S5_STATE = 64
D_HG = 512
HG_HEADS = 4
HG_D = 128
D_IN = D_S5 + 4 * D_HG
N_EXPERTS = 32
TOP_K = 4
SWIGLU_LIMIT = 7.0
SWIGLU_ALPHA = 1.702
DEEPNORM_ALPHA = 2.0 ** 0.25
LN_EPS = 1e-5
RMS_EPS = 1e-6

LANES = 128
SUBLANES = 8
ROW_TILES = D_MODEL // LANES
S5_CONV = CHUNK * S5_GROUP
HG_CHUNK = 128
MOE_ROWS = 512
VMEM_LIMIT = 56 * 1024 * 1024

assert ROW_TILES == SUBLANES


def _cparams(n_axes=1, flags=None):
    return pltpu.CompilerParams(dimension_semantics=("arbitrary",) * n_axes,
                                vmem_limit_bytes=VMEM_LIMIT, flags=flags)


def _dot(a, b, precision=None):
    return jnp.dot(a, b, preferred_element_type=F32, precision=precision)


def _layernorm(x, g, b):
    mu = jnp.mean(x, axis=-1, keepdims=True)
    xc = x - mu
    var = jnp.mean(xc * xc, axis=-1, keepdims=True)
    return xc * lax.rsqrt(var + LN_EPS) * g + b


def _two_phase_specs(block, n_first):
    nd = len(block)
    first = pl.BlockSpec(block, lambda i: (jnp.minimum(i, n_first - 1),) + (0,) * (nd - 1))
    second = pl.BlockSpec(block, lambda i: (jnp.maximum(i - n_first, 0),) + (0,) * (nd - 1))
    return [first, second]


def _two_phase_lane_block_specs(tm, n_first):
    nblk = D_S5 // LANES
    first = pl.BlockSpec((nblk, tm, LANES), lambda i: (0, jnp.minimum(i, n_first - 1), 0))
    second = pl.BlockSpec((nblk, tm, LANES), lambda i: (0, jnp.maximum(i - n_first, 0), 0))
    return [first, second]


def _read_lane_blocks(ref):
    return jnp.concatenate([ref[j] for j in range(D_S5 // LANES)], axis=1)


def _prompt_spec(tm, seq, width, n_first):
    per_seq = seq // tm

    def index(i):
        ic = jnp.minimum(i, n_first - 1)
        return (ic // per_seq, ic % per_seq, 0)

    return pl.BlockSpec((1, tm, width), index)


def _chunk(rows, j):
    return pl.ds(j, rows, stride=ROW_TILES)


def _read_row_tiles(ref, rows):
    return jnp.concatenate([ref[_chunk(rows, j), :] for j in range(ROW_TILES)], axis=1)


def _write_row_tiles(ref, val, rows):
    for j in range(ROW_TILES):
        ref[_chunk(rows, j), :] = val[:, j * LANES:(j + 1) * LANES]


def _row_tiles(r, n=1):
    return pl.ds(pl.multiple_of(r * ROW_TILES, ROW_TILES), n * ROW_TILES)


def _row_tile(r):
    return _row_tiles(r)


def _ln_in_proj_kernel(xp_ref, xs_ref, g_ref, b_ref, w_ref, h_ref, up_ref, us_ref, fp_ref, fs_ref, zp_ref, zs_ref,
                       *, n_first):
    def phase(x, u_ref, f_ref, z_ref):
        h = _layernorm(x, g_ref[...], b_ref[...])
        h_ref[...] = h
        hb = h.astype(BF16)
        u = _dot(hb, w_ref[:, :D_S5])
        for j in range(D_S5 // LANES):
            u_ref[j] = u[:, j * LANES:(j + 1) * LANES]
        f_ref[...] = _dot(hb, w_ref[:, D_S5:D_S5 + D_HG])
        z_ref[...] = _dot(hb, w_ref[:, D_S5 + D_HG:]).astype(BF16)

    i = pl.program_id(0)
    pl.when(i < n_first)(lambda: phase(xp_ref[0], up_ref, fp_ref, zp_ref))
    pl.when(i >= n_first)(lambda: phase(xs_ref[...], us_ref, fs_ref, zs_ref))


def ln_in_proj(xp, xs, g, b, w_bf16, tm):
    nb, seq, _ = xp.shape
    tp, ts = nb * seq, xs.shape[0]
    n_first = tp // tm
    fixed = lambda i: (0, 0)
    return pl.pallas_call(
        functools.partial(_ln_in_proj_kernel, n_first=n_first),
        grid=((tp + ts) // tm,),
        in_specs=[_prompt_spec(tm, seq, D_MODEL, n_first), _two_phase_specs((tm, D_MODEL), n_first)[1]]
                 + [pl.BlockSpec((1, D_MODEL), fixed), pl.BlockSpec((1, D_MODEL), fixed),
                    pl.BlockSpec((D_MODEL, D_IN), fixed)],
        out_specs=[pl.BlockSpec((tm, D_MODEL), lambda i: (i, 0))]
                  + _two_phase_lane_block_specs(tm, n_first)
                  + _two_phase_specs((tm, D_HG), n_first)
                  + _two_phase_specs((tm, 3 * D_HG), n_first),
        out_shape=[jax.ShapeDtypeStruct((tp + ts, D_MODEL), F32),
                   jax.ShapeDtypeStruct((D_S5 // LANES, tp, LANES), F32),
                   jax.ShapeDtypeStruct((D_S5 // LANES, ts, LANES), F32),
                   jax.ShapeDtypeStruct((tp, D_HG), F32), jax.ShapeDtypeStruct((ts, D_HG), F32),
                   jax.ShapeDtypeStruct((tp, 3 * D_HG), BF16), jax.ShapeDtypeStruct((ts, 3 * D_HG), BF16)],
        compiler_params=_cparams(),
        name="ln_in_proj",
    )(xp, xs, g, b, w_bf16)


def _lane_block(j):
    return slice(j * LANES, (j + 1) * LANES)


def _granule_transpose(slabs):
    per_block = LANES // S5_GROUP
    granule = lax.broadcasted_iota(jnp.int32, (1, LANES), 1) // S5_GROUP
    x = list(slabs)
    for d in (4, 2, 1):
        keep = (granule & d) == 0
        y = [None] * per_block
        for i in range(per_block):
            if i & d == 0:
                y[i] = jnp.where(keep, x[i], pltpu.roll(x[i + d], d * S5_GROUP, axis=1))
                y[i + d] = jnp.where(keep, pltpu.roll(x[i], LANES - d * S5_GROUP, axis=1), x[i + d])
        x = y
    return x


def _s5_rows_kernel(u_ref, o_ref, *, chunks):
    per_block = LANES // S5_GROUP
    for gcol in range(D_S5 // LANES):
        for j in range(S5_CONV // LANES):
            by_time = [u_ref[gcol, pl.ds(per_block * j + sl, chunks, stride=CHUNK), :] for sl in range(per_block)]
            for gl, rows in enumerate(_granule_transpose(by_time)):
                o_ref[gcol * per_block + gl, :, _lane_block(j)] = rows.astype(BF16)


def s5_rows(u, chunks):
    nblk, t, _ = u.shape
    r = t // CHUNK
    return pl.pallas_call(
        functools.partial(_s5_rows_kernel, chunks=chunks),
        grid=(r // chunks,),
        in_specs=[pl.BlockSpec((nblk, chunks * CHUNK, LANES), lambda i: (0, i, 0))],
        out_specs=pl.BlockSpec((S5_GROUPS, chunks, S5_CONV), lambda i: (0, i, 0)),
        out_shape=jax.ShapeDtypeStruct((S5_GROUPS, r, S5_CONV), BF16),
        compiler_params=_cparams(),
        name="s5_rows",
    )(u)


def _s5_tokens_kernel(y_ref, o_ref, *, chunks):
    per_block = LANES // S5_GROUP
    for gcol in range(D_S5 // LANES):
        for j in range(S5_CONV // LANES):
            by_group = [y_ref[gcol * per_block + gl, :, _lane_block(j)].astype(F32) for gl in range(per_block)]
            for sl, rows in enumerate(_granule_transpose(by_group)):
                o_ref[gcol, pl.ds(per_block * j + sl, chunks, stride=CHUNK), :] = rows


def s5_tokens(y_rows, chunks):
    _, r, _ = y_rows.shape
    return pl.pallas_call(
        functools.partial(_s5_tokens_kernel, chunks=chunks),
        grid=(r // chunks,),
        in_specs=[pl.BlockSpec((S5_GROUPS, chunks, S5_CONV), lambda i: (0, i, 0))],
        out_specs=pl.BlockSpec((D_S5 // LANES, chunks * CHUNK, LANES), lambda i: (0, i, 0)),
        out_shape=jax.ShapeDtypeStruct((D_S5 // LANES, r * CHUNK, LANES), F32),
        compiler_params=_cparams(),
        name="s5_tokens",
    )(y_rows)


def _s5_prep_kernel(lrc_ref, lic_ref, lrr_ref, lir_ref, ls_ref, brt_ref, bit_ref,
                    brtt_ref, bitt_ref, crt_ref, cit_ref,
                    m_ref, w_ref, wc_ref, a_ref):
    step = jnp.exp(ls_ref[0])

    def discretise(lr_raw, li):
        lr = jnp.minimum(lr_raw, -1e-4)
        dr, di = lr * step, li * step
        mag = jnp.exp(dr)
        a_re, a_im = mag * jnp.cos(di), mag * jnp.sin(di)
        den = lr * lr + li * li
        nr = a_re - 1.0
        fr = (nr * lr + a_im * li) / den
        fi = (a_im * lr - nr * li) / den
        return dr, di, fr, fi

    dr_c, di_c, _, _ = discretise(lrc_ref[0], lic_ref[0])
    dr_r, di_r, fr_r, fi_r = discretise(lrr_ref[0], lir_ref[0])

    lane = lax.broadcasted_iota(jnp.int32, (1, S5_CONV), 1)
    t_row = lax.broadcasted_iota(jnp.int32, (1, CHUNK), 1).astype(F32)
    t_col = lax.broadcasted_iota(jnp.int32, (CHUNK, 1), 0).astype(F32)
    lag_of_lane = (lax.broadcasted_iota(jnp.int32, (CHUNK, S5_CONV), 1) // S5_GROUP
                   == lax.broadcasted_iota(jnp.int32, (CHUNK, S5_CONV), 0)).astype(F32)
    time_of_row = (lax.broadcasted_iota(jnp.int32, (S5_CONV, CHUNK), 0) // S5_GROUP
                   == lax.broadcasted_iota(jnp.int32, (S5_CONV, CHUNK), 1)).astype(F32)

    def c_times_power(tf):
        mag = jnp.exp(dr_c * tf)
        ang = di_c * tf
        pr = _dot(mag * jnp.cos(ang), lag_of_lane, HIGHEST)
        pi = _dot(mag * jnp.sin(ang), lag_of_lane, HIGHEST)
        ctr, cti = crt_ref[0], cit_ref[0]
        return ctr * pr - cti * pi, ctr * pi + cti * pr

    cpr, cpi = c_times_power(t_row)
    bbr = fr_r * brt_ref[0] - fi_r * bit_ref[0]
    bbi = fr_r * bit_ref[0] + fi_r * brt_ref[0]
    kt = _dot(bbr, cpr, HIGHEST) - _dot(bbi, cpi, HIGHEST)
    for s in range(CHUNK):
        shifted = kt if s == 0 else pltpu.roll(kt, S5_GROUP * s, axis=1)
        m_ref[0, S5_GROUP * s:S5_GROUP * (s + 1), :] = jnp.where(
            lane >= S5_GROUP * s, shifted, 0.0).astype(BF16)

    rem = CHUNK - 1.0 - t_col
    magw = jnp.exp(dr_r * rem)
    angw = di_r * rem
    pwr = _dot(time_of_row, magw * jnp.cos(angw), HIGHEST)
    pwi = _dot(time_of_row, magw * jnp.sin(angw), HIGHEST)
    bbtr = fr_r * brtt_ref[0] - fi_r * bitt_ref[0]
    bbti = fr_r * bitt_ref[0] + fi_r * brtt_ref[0]
    w_ref[0, :, :S5_STATE] = pwr * bbtr - pwi * bbti
    w_ref[0, :, S5_STATE:] = pwr * bbti + pwi * bbtr

    c1r, c1i = c_times_power(t_row + 1.0)
    wc_ref[0, :S5_STATE, :] = c1r.astype(BF16)
    wc_ref[0, S5_STATE:, :] = (-c1i).astype(BF16)

    full = float(CHUNK)
    mag_c = jnp.exp(dr_r * full)
    a_ref[0, 0:1, :] = mag_c * jnp.cos(di_r * full)
    a_ref[0, 1:2, :] = mag_c * jnp.sin(di_r * full)


def s5_prep(lam_re, lam_im, log_step, b_re, b_im, c_re, c_im):
    g, p = lam_re.shape
    brt = jnp.swapaxes(b_re, 1, 2)
    bit = jnp.swapaxes(b_im, 1, 2)
    crt = jnp.tile(jnp.swapaxes(c_re, 1, 2), (1, 1, CHUNK))
    cit = jnp.tile(jnp.swapaxes(c_im, 1, 2), (1, 1, CHUNK))
    args = (lam_re.reshape(g, p, 1), lam_im.reshape(g, p, 1),
            lam_re.reshape(g, 1, p), lam_im.reshape(g, 1, p), log_step.reshape(g, 1, 1),
            brt, bit, jnp.tile(brt, (1, CHUNK, 1)), jnp.tile(bit, (1, CHUNK, 1)), crt, cit)
    spec = lambda a: pl.BlockSpec((1,) + a.shape[1:], lambda i: (i, 0, 0))
    out_shape = [jax.ShapeDtypeStruct((g, S5_CONV, S5_CONV), BF16),
                 jax.ShapeDtypeStruct((g, S5_CONV, 2 * S5_STATE), F32),
                 jax.ShapeDtypeStruct((g, 2 * S5_STATE, S5_CONV), BF16),
                 jax.ShapeDtypeStruct((g, 2, S5_STATE), F32)]
    return pl.pallas_call(
        _s5_prep_kernel,
        grid=(g,),
        in_specs=[spec(a) for a in args],
        out_specs=[spec(o) for o in out_shape],
        out_shape=out_shape,
        compiler_params=_cparams(),
        name="s5_prep",
    )(*args)


def _split3(w):
    hi = w.astype(BF16)
    r1 = w - hi.astype(F32)
    mid = r1.astype(BF16)
    lo = (r1 - mid.astype(F32)).astype(BF16)
    return hi, mid, lo


def _s5_main_kernel(up_ref, us_ref, m_ref, w_ref, wc_ref, a_ref, xsr_ref, xsi_ref,
                    yp_ref, ys_ref, fpr_ref, fpi_ref, fsr_ref, fsi_ref,
                    hr_scr, hi_scr, x0r_scr, x0i_scr, *, n_prompt, n_chunks):
    w3 = _split3(w_ref[0])
    ar = a_ref[0, 0:1, :]
    ai = a_ref[0, 1:2, :]

    def local(u):
        hend = _dot(u, w3[0]) + _dot(u, w3[1]) + _dot(u, w3[2])
        return _dot(u, m_ref[0]), hend[:, :S5_STATE], hend[:, S5_STATE:]

    def carry_in(x0r, x0i):
        return (_dot(x0r.astype(BF16), wc_ref[0, :S5_STATE, :])
                + _dot(x0i.astype(BF16), wc_ref[0, S5_STATE:, :]))

    y_local, hr, hi = local(up_ref[0])
    hr_scr[...] = hr
    hi_scr[...] = hi
    xr = jnp.zeros((n_prompt, S5_STATE), F32)
    xi = jnp.zeros((n_prompt, S5_STATE), F32)
    for c in range(n_chunks):
        rows = pl.ds(c, n_prompt, stride=n_chunks)
        x0r_scr[rows, :] = xr
        x0i_scr[rows, :] = xi
        xr, xi = (ar * xr - ai * xi + hr_scr[rows, :],
                  ar * xi + ai * xr + hi_scr[rows, :])
    fpr_ref[0] = xr
    fpi_ref[0] = xi
    yp_ref[0] = (y_local + carry_in(x0r_scr[...], x0i_scr[...])).astype(BF16)

    y_local, hr, hi = local(us_ref[0])
    sr, si = xsr_ref[0], xsi_ref[0]
    fsr_ref[0] = ar * sr - ai * si + hr
    fsi_ref[0] = ar * si + ai * sr + hi
    ys_ref[0] = (y_local + carry_in(sr, si)).astype(BF16)


def s5_main(up_rows, us_rows, m, w, wc, a, xs_re, xs_im, n_prompt, n_chunks):
    g, r, _ = up_rows.shape
    n_sample = xs_re.shape[1]
    spec = lambda shape: pl.BlockSpec((1,) + tuple(shape[1:]), lambda i: (i, 0, 0))
    args = (up_rows, us_rows, m, w, wc, a, xs_re, xs_im)
    out_shape = [jax.ShapeDtypeStruct((g, r, S5_CONV), BF16),
                 jax.ShapeDtypeStruct((g, n_sample, S5_CONV), BF16),
                 jax.ShapeDtypeStruct((g, n_prompt, S5_STATE), F32),
                 jax.ShapeDtypeStruct((g, n_prompt, S5_STATE), F32),
                 jax.ShapeDtypeStruct((g, n_sample, S5_STATE), F32),
                 jax.ShapeDtypeStruct((g, n_sample, S5_STATE), F32)]
    return pl.pallas_call(
        functools.partial(_s5_main_kernel, n_prompt=n_prompt, n_chunks=n_chunks),
        grid=(g,),
        in_specs=[spec(x.shape) for x in args],
        out_specs=[spec(o.shape) for o in out_shape],
        out_shape=out_shape,
        scratch_shapes=[pltpu.VMEM((r, S5_STATE), F32)] * 4,
        compiler_params=_cparams(),
        name="s5_main",
    )(*args)


def _hgrn_kernel(f_ref, z_ref, s0_ref, lb_ref, ng_ref, o_ref, sfin_ref, st_scr, *, n_seq, chunk):
    c = pl.program_id(1)

    @pl.when(c == 0)
    def _():
        st_scr[...] = s0_ref[...]

    lbw = lb_ref[...]
    lbe = jnp.exp(lbw - jnp.max(lbw, axis=0, keepdims=True))
    lb_all = lbe[0:1, :] / jnp.sum(lbe, axis=0, keepdims=True)

    levels = [chunk >> (i + 1) for i in range(chunk.bit_length() - 1)]
    rowi = lax.broadcasted_iota(jnp.int32, (chunk, chunk), 0)
    coli = lax.broadcasted_iota(jnp.int32, (chunk, chunk), 1)
    rowk = lax.broadcasted_iota(jnp.int32, (chunk, HG_D), 0)
    cum_rows = [(coli <= rowi).astype(F32)]
    upper, lower, sign, same_block = [], [], [], []
    for m in levels:
        ref_row = (rowi // (2 * m)) * (2 * m) + (m - 1)
        cum_rows.append((coli <= ref_row).astype(F32))
        in_upper = (rowk % (2 * m)) >= m
        upper.append(in_upper.astype(F32).astype(BF16))
        lower.append(1.0 - upper[-1])
        sign.append(jnp.where(in_upper, 1.0, -1.0))
        same_block.append((rowi // (2 * m)) == (coli // (2 * m)))
    cum_mat = jnp.concatenate(cum_rows, axis=0).astype(BF16)
    cum_mat3 = jnp.concatenate([cum_mat] * 3, axis=1)
    diag = rowi == coli
    nt = (((1,), (1,)), ((), ()))

    def body(n, carry):
        zf = f_ref[n]
        fg_all = lb_all + (1.0 - lb_all) * jax.nn.sigmoid(zf)
        cums = _dot(cum_mat3, jnp.concatenate(_split3(jnp.log2(fg_all)), axis=0))
        for hd in range(HG_HEADS):
            cols = slice(hd * HG_D, (hd + 1) * HG_D)
            zq = z_ref[n, :, hd * HG_D:(hd + 1) * HG_D].astype(F32)
            vb = z_ref[n, :, D_HG + hd * HG_D:D_HG + (hd + 1) * HG_D]
            zg = z_ref[n, :, 2 * D_HG + hd * HG_D:2 * D_HG + (hd + 1) * HG_D].astype(F32)
            q = zq * jax.nn.sigmoid(zq)
            kk = 1.0 - fg_all[:, cols]
            bcum = cums[:chunk, cols]
            b_last = bcum[chunk - 1:chunk, :]
            qb = q.astype(BF16)
            kb = kk.astype(BF16)
            st = st_scr[n, hd]

            scores = jnp.where(diag, lax.dot_general(qb, kb, nt, preferred_element_type=F32), 0.0)
            for lvl in range(len(levels)):
                bref = cums[(lvl + 1) * chunk:(lvl + 2) * chunk, cols]
                dec = jnp.exp2((bcum - bref) * sign[lvl]).astype(BF16)
                sc = lax.dot_general(qb * upper[lvl] * dec, kb * lower[lvl] * dec, nt,
                                     preferred_element_type=F32)
                scores = scores + jnp.where(same_block[lvl], sc, 0.0)

            qd = (q * jnp.exp2(bcum)).astype(BF16)
            o = lax.dot_general(qd, st.astype(BF16), nt, preferred_element_type=F32)
            o = o + _dot(scores.astype(BF16), vb)
            kdec = (kk * jnp.exp2(b_last - bcum)).astype(BF16)
            st_scr[n, hd] = jnp.exp2(b_last) * st + lax.dot_general(
                vb, kdec, (((0,), (0,)), ((), ())), preferred_element_type=F32)

            on = o * lax.rsqrt(jnp.mean(o * o, axis=-1, keepdims=True) + RMS_EPS) * ng_ref[:, cols]
            o_ref[n, :, hd * HG_D:(hd + 1) * HG_D] = on * (zg * jax.nn.sigmoid(zg))
        return carry

    lax.fori_loop(0, n_seq, body, 0, unroll=True)

    @pl.when(c == pl.num_programs(1) - 1)
    def _():
        sfin_ref[...] = st_scr[...]


def hgrn(f, z, s0_t, hg_lb, norm_g, n_seq, chunk):
    n, length, _ = z.shape
    return pl.pallas_call(
        functools.partial(_hgrn_kernel, n_seq=n_seq, chunk=chunk),
        grid=(n // n_seq, length // chunk),
        in_specs=[pl.BlockSpec((n_seq, chunk, D_HG), lambda g, c: (g, c, 0)),
                  pl.BlockSpec((n_seq, chunk, 3 * D_HG), lambda g, c: (g, c, 0)),
                  pl.BlockSpec((n_seq, HG_HEADS, HG_D, HG_D), lambda g, c: (g, 0, 0, 0)),
                  pl.BlockSpec(hg_lb.shape, lambda g, c: (0, 0)),
                  pl.BlockSpec((1, D_HG), lambda g, c: (0, 0))],
        out_specs=[pl.BlockSpec((n_seq, chunk, D_HG), lambda g, c: (g, c, 0)),
                   pl.BlockSpec((n_seq, HG_HEADS, HG_D, HG_D), lambda g, c: (g, 0, 0, 0))],
        out_shape=[jax.ShapeDtypeStruct((n, length, D_HG), F32),
                   jax.ShapeDtypeStruct((n, HG_HEADS, HG_D, HG_D), F32)],
        scratch_shapes=[pltpu.VMEM((n_seq, HG_HEADS, HG_D, HG_D), F32)],
        compiler_params=_cparams(2),
        name="hgrn",
    )(f, z, s0_t, hg_lb, norm_g)


def _post_mix_kernel(h_ref, up_ref, us_ref, yp_ref, ys_ref, hgp_ref, hgs_ref,
                     d_ref, wglu_ref, bglu_ref, s5g_ref, wout_ref, g1_ref, b1_ref, rwt_ref, rb_ref,
                     h1_ref, slot_ref, gate_ref, before_ref, cnt_ref, run_scr, *, tm, n_first):
    i = pl.program_id(0)

    @pl.when(i == 0)
    def _():
        run_scr[...] = jnp.zeros_like(run_scr)

    def phase(u_ref, y_ref, hg_ref):
        ys = _read_lane_blocks(y_ref) + d_ref[...] * _read_lane_blocks(u_ref)
        gl = 0.5 * ys * (1.0 + lax.erf(ys * (2.0 ** -0.5)))
        s5o = gl * jax.nn.sigmoid(_dot(gl.astype(BF16), wglu_ref[...]) + bglu_ref[...])
        s5o = s5o * lax.rsqrt(jnp.mean(s5o * s5o, axis=-1, keepdims=True) + RMS_EPS) * s5g_ref[...]
        mix = (_dot(s5o.astype(BF16), wout_ref[:D_S5, :])
               + _dot(hg_ref[...].astype(BF16), wout_ref[D_S5:, :]))
        h1 = _layernorm(DEEPNORM_ALPHA * h_ref[...] + mix, g1_ref[...], b1_ref[...])
        _write_row_tiles(h1_ref, h1, tm)

        h_hi, h_mid, _ = _split3(h1)
        w_hi, w_mid, _ = _split3(rwt_ref[...])
        nt = (((1,), (1,)), ((), ()))
        logits = (lax.dot_general(w_hi, h_hi, nt, preferred_element_type=F32)
                  + lax.dot_general(w_hi, h_mid, nt, preferred_element_type=F32)
                  + lax.dot_general(w_mid, h_hi, nt, preferred_element_type=F32)) + rb_ref[...]
        eid = lax.broadcasted_iota(jnp.int32, (N_EXPERTS, tm), 0)
        vals, idxs = [], []
        for _ in range(TOP_K):
            m = jnp.max(logits, axis=0, keepdims=True)
            ix = jnp.min(jnp.where(logits == m, eid, N_EXPERTS), axis=0, keepdims=True)
            vals.append(m)
            idxs.append(ix)
            logits = jnp.where(eid == ix, -jnp.inf, logits)
        exps = [jnp.exp(v - vals[0]) for v in vals]
        den = exps[0] + exps[1] + exps[2] + exps[3]

        onehot = jnp.zeros((N_EXPERTS, tm), F32)
        for ix in idxs:
            onehot = onehot + (eid == ix).astype(F32)
        rowi = lax.broadcasted_iota(jnp.int32, (tm, tm), 0)
        coli = lax.broadcasted_iota(jnp.int32, (tm, tm), 1)
        earlier = (rowi < coli).astype(BF16)
        prefix = _dot(onehot.astype(BF16), earlier)
        tile_cnt = jnp.sum(onehot, axis=1, keepdims=True)
        for k in range(TOP_K):
            lower_experts = jnp.sum(jnp.where(eid < idxs[k], tile_cnt, 0.0), axis=0, keepdims=True)
            rank = jnp.sum(jnp.where(eid == idxs[k], prefix, 0.0), axis=0, keepdims=True)
            slot_ref[0, :, k * tm:(k + 1) * tm] = (lower_experts + rank).astype(jnp.int32)
            gate_ref[0, :, k * tm:(k + 1) * tm] = exps[k] / den
        before_ref[0] = run_scr[...]
        run_scr[...] = run_scr[...] + tile_cnt
        cnt_ref[...] = run_scr[...]

    pl.when(i < n_first)(lambda: phase(up_ref, yp_ref, hgp_ref))
    pl.when(i >= n_first)(lambda: phase(us_ref, ys_ref, hgs_ref))


def post_mix(h0, u_pair, y_pair, hg_pair, d_skip, wglu, bglu, s5g, wout, g1, b1, rw_t, rb_col, tm):
    t = h0.shape[0]
    n_first = u_pair[0].shape[1] // tm
    row = lambda i: (i, 0)
    fixed = lambda i: (0, 0)
    full = lambda a: pl.BlockSpec(a.shape, fixed)
    weights = (d_skip, wglu, bglu, s5g, wout, g1, b1, rw_t, rb_col)
    return pl.pallas_call(
        functools.partial(_post_mix_kernel, tm=tm, n_first=n_first),
        grid=(t // tm,),
        in_specs=[pl.BlockSpec((tm, D_MODEL), row)]
                 + _two_phase_lane_block_specs(tm, n_first) * 2 + _two_phase_specs((tm, D_HG), n_first)
                 + [full(a) for a in weights],
        out_specs=[pl.BlockSpec((tm * ROW_TILES, LANES), row),
                   pl.BlockSpec((1, 1, TOP_K * tm), lambda i: (i, 0, 0)),
                   pl.BlockSpec((1, 1, TOP_K * tm), lambda i: (i, 0, 0)),
                   pl.BlockSpec((1, N_EXPERTS, 1), lambda i: (i, 0, 0)),
                   pl.BlockSpec((N_EXPERTS, 1), fixed)],
        out_shape=[jax.ShapeDtypeStruct((t * ROW_TILES, LANES), F32),
                   jax.ShapeDtypeStruct((t // tm, 1, TOP_K * tm), jnp.int32),
                   jax.ShapeDtypeStruct((t // tm, 1, TOP_K * tm), F32),
                   jax.ShapeDtypeStruct((t // tm, N_EXPERTS, 1), F32),
                   jax.ShapeDtypeStruct((N_EXPERTS, 1), F32)],
        scratch_shapes=[pltpu.VMEM((N_EXPERTS, 1), F32)],
        compiler_params=_cparams(),
        name="post_mix",
    )(h0, *u_pair, *y_pair, *hg_pair, *weights)


def _segment_copies(meta_ref, tile, tm, make_copy):
    for e in range(N_EXPERTS):
        sorted_row = meta_ref[tile, e]
        cnt = meta_ref[tile, N_EXPERTS + e]
        staged_row = meta_ref[tile, 2 * N_EXPERTS + e]
        for b in range(tm.bit_length()):
            done = cnt & ((1 << b) - 1)

            @pl.when(((cnt >> b) & 1) == 1)
            def _(b=b, done=done, e=e):
                make_copy(staged_row + done, sorted_row + done, 1 << b).start(priority=e % 2)


def _dispatch_kernel(meta_ref, pend_ref, slot_ref, h_ref, xs_ref, stage, zero_scr, zsem, sem, *, tm):
    n_rows = xs_ref.shape[0] // ROW_TILES
    i = pl.program_id(0)
    n = pl.num_programs(0)
    slot = i % 2

    def drain(s):
        pltpu.make_async_copy(stage.at[s], xs_ref.at[pl.ds(0, TOP_K * tm * ROW_TILES)], sem.at[s]).wait()

    @pl.when(i == 0)
    def _():
        zero_scr[...] = jnp.zeros_like(zero_scr)

        def last_block(e):
            prev = pend_ref[e - 1] if e > 0 else 0
            copy = pltpu.make_async_copy(
                zero_scr, xs_ref.at[pl.ds(pl.multiple_of(jnp.maximum(pend_ref[e] - MOE_ROWS, 0) * ROW_TILES,
                                                         ROW_TILES), MOE_ROWS * ROW_TILES)], zsem)
            return pend_ref[e] > prev, copy

        def tail_block(j):
            row0 = pend_ref[N_EXPERTS - 1] + j * MOE_ROWS
            copy = pltpu.make_async_copy(
                zero_scr, xs_ref.at[pl.ds(pl.multiple_of(jnp.minimum(row0, n_rows - MOE_ROWS) * ROW_TILES,
                                                         ROW_TILES), MOE_ROWS * ROW_TILES)], zsem)
            return row0 < n_rows, copy

        blocks = [last_block(e) for e in range(N_EXPERTS)] + [tail_block(j) for j in range(N_EXPERTS)]
        for used, copy in blocks:
            pl.when(used)(copy.start)
        for used, copy in blocks:
            pl.when(used)(copy.wait)

    pl.when(i >= 2)(lambda: drain(slot))

    def body(t, carry):
        row = h_ref[_row_tile(t), :]
        for k in range(TOP_K):
            stage[slot, _row_tile(slot_ref[k * tm + t]), :] = row
        return carry

    lax.fori_loop(0, tm, body, 0, unroll=8)
    _segment_copies(meta_ref, i, tm, lambda staged_row, sorted_row, rows: pltpu.make_async_copy(
        stage.at[slot, _row_tiles(staged_row, rows)], xs_ref.at[_row_tiles(sorted_row, rows)], sem.at[slot]))

    @pl.when(i == n - 1)
    def _():
        pl.when(n >= 2)(lambda: drain(1 - slot))
        drain(slot)


def dispatch(meta, pend, slots_flat, h1, n_rows, tm):
    t = h1.shape[0] // ROW_TILES
    grid_spec = pltpu.PrefetchScalarGridSpec(
        num_scalar_prefetch=2,
        grid=(t // tm,),
        in_specs=[pl.BlockSpec((TOP_K * tm,), lambda i, meta, pend: (i,), memory_space=pltpu.SMEM),
                  pl.BlockSpec((tm * ROW_TILES, LANES), lambda i, meta, pend: (i, 0))],
        out_specs=pl.BlockSpec(memory_space=pl.ANY),
        scratch_shapes=[pltpu.VMEM((2, TOP_K * tm * ROW_TILES, LANES), F32),
                        pltpu.VMEM((MOE_ROWS * ROW_TILES, LANES), F32),
                        pltpu.SemaphoreType.DMA(()), pltpu.SemaphoreType.DMA((2,))],
    )
    return pl.pallas_call(
        functools.partial(_dispatch_kernel, tm=tm),
        grid_spec=grid_spec,
        out_shape=jax.ShapeDtypeStruct((n_rows * ROW_TILES, LANES), F32),
        compiler_params=_cparams(),
        name="moe_dispatch",
    )(meta, pend, slots_flat, h1)


def _moe_ffn_kernel(be_ref, nu_ref, seg_ref, nxt_ref, x_ref, wg_ref, bg_ref, wu_ref, bu_ref, wd_ref, bd_ref,
                    y_ref, wbuf, wbf, sem):
    i = pl.program_id(0)
    hbm = (wg_ref, wu_ref, wd_ref)

    def weight_copies(expert, s):
        return [pltpu.make_async_copy(hbm[j].at[expert], wbuf.at[s, j], sem.at[s, j]) for j in range(3)]

    @pl.when((i == 0) | (be_ref[i] != be_ref[jnp.maximum(i - 1, 0)]))
    def _():
        s = seg_ref[i] % 2

        @pl.when(i == 0)
        def _():
            for c in weight_copies(be_ref[0], 0):
                c.start()

        for j, c in enumerate(weight_copies(be_ref[i], s)):
            c.wait()
            wbf[j] = wbuf[s, j].astype(BF16)

        @pl.when(nxt_ref[i] >= 0)
        def _():
            for c in weight_copies(nxt_ref[i], 1 - s):
                c.start()

    @pl.when(i < nu_ref[0])
    def _():
        x = _read_row_tiles(x_ref, MOE_ROWS).astype(BF16)
        gt = jnp.minimum(_dot(x, wbf[0]) + bg_ref[0], SWIGLU_LIMIT)
        up = jnp.clip(_dot(x, wbf[1]) + bu_ref[0], -SWIGLU_LIMIT, SWIGLU_LIMIT)
        hid = (up + 1.0) * (gt * jax.nn.sigmoid(SWIGLU_ALPHA * gt))
        _write_row_tiles(y_ref, _dot(hid.astype(BF16), wbf[2]) + bd_ref[0], MOE_ROWS)

    @pl.when(i >= nu_ref[0])
    def _():
        y_ref[...] = jnp.zeros_like(y_ref)


def moe_ffn(block_e, n_used, segment, next_e, xs, wg, bg, wu, bu, wd, bd):
    n_rows = xs.shape[0] // ROW_TILES
    n_blocks = n_rows // MOE_ROWS
    wsel = lambda i, be, nu, seg, nxt: (be[i], 0, 0)
    d_ff = wg.shape[-1]
    assert wg.shape[1:] == wu.shape[1:] == wd.shape[1:] == (D_MODEL, D_MODEL)
    anywhere = pl.BlockSpec(memory_space=pl.ANY)
    grid_spec = pltpu.PrefetchScalarGridSpec(
        num_scalar_prefetch=4,
        grid=(n_blocks,),
        in_specs=[pl.BlockSpec((MOE_ROWS * ROW_TILES, LANES),
                               lambda i, be, nu, seg, nxt: (jnp.minimum(i, nu[0] - 1), 0)),
                  anywhere, pl.BlockSpec((1, 1, d_ff), wsel),
                  anywhere, pl.BlockSpec((1, 1, d_ff), wsel),
                  anywhere, pl.BlockSpec((1, 1, D_MODEL), wsel)],
        out_specs=pl.BlockSpec((MOE_ROWS * ROW_TILES, LANES), lambda i, be, nu, seg, nxt: (i, 0)),
        scratch_shapes=[pltpu.VMEM((2, 3, D_MODEL, D_MODEL), F32), pltpu.VMEM((3, D_MODEL, D_MODEL), BF16),
                        pltpu.SemaphoreType.DMA((2, 3))],
    )
    return pl.pallas_call(
        _moe_ffn_kernel,
        grid_spec=grid_spec,
        out_shape=jax.ShapeDtypeStruct((n_rows * ROW_TILES, LANES), F32),
        compiler_params=_cparams(),
        name="moe_ffn",
    )(block_e, n_used, segment, next_e, xs, wg, bg, wu, bu, wd, bd)


def _combine_kernel(meta_ref, slot_ref, gate_ref, h_ref, pp_ref, ps_ref, yb_ref,
                    plew_ref, plegw_ref, g2_ref, b2_ref, outp_ref, outs_ref, buf, r_scr, sem,
                    *, tm, n_first):
    i = pl.program_id(0)
    n = pl.num_programs(0)
    slot = i % 2

    def fetch(tile, s):
        _segment_copies(meta_ref, tile, tm, lambda staged_row, sorted_row, rows: pltpu.make_async_copy(
            yb_ref.at[_row_tiles(sorted_row, rows)], buf.at[s, _row_tiles(staged_row, rows)], sem.at[s]))

    pl.when(i == 0)(lambda: fetch(0, 0))
    pl.when(i + 1 < n)(lambda: fetch(i + 1, 1 - slot))
    pltpu.make_async_copy(yb_ref.at[pl.ds(0, TOP_K * tm * ROW_TILES)], buf.at[slot], sem.at[slot]).wait()

    def body(t, carry):
        acc = DEEPNORM_ALPHA * h_ref[_row_tile(t), :]
        for k in range(TOP_K):
            acc = acc + gate_ref[k * tm + t] * buf[slot, _row_tile(slot_ref[k * tm + t]), :]
        r_scr[_row_tile(t), :] = acc
        return carry

    lax.fori_loop(0, tm, body, 0, unroll=8)
    r = _read_row_tiles(r_scr, tm)
    gate = jax.nn.sigmoid(_dot(r.astype(BF16), plegw_ref[...]))

    def finish(p, store):
        e = _dot(p.astype(BF16), plew_ref[...]) * gate
        store(_layernorm(r + e, g2_ref[...], b2_ref[...]))

    def store_prompt(v):
        outp_ref[0] = v

    def store_sample(v):
        outs_ref[...] = v

    pl.when(i < n_first)(lambda: finish(pp_ref[0], store_prompt))
    pl.when(i >= n_first)(lambda: finish(ps_ref[...], store_sample))


def combine(meta, slots_flat, gates_flat, h1, p_prompt, p_sample, yb, plew, plegw, g2, b2, tm):
    t = h1.shape[0] // ROW_TILES
    nb, seq, _ = p_prompt.shape
    ts = p_sample.shape[0]
    n_first = nb * seq // tm
    fixed = lambda i, meta: (0, 0)
    flat = pl.BlockSpec((TOP_K * tm,), lambda i, meta: (i,), memory_space=pltpu.SMEM)
    sample = lambda width: pl.BlockSpec((tm, width), lambda i, meta: (jnp.maximum(i - n_first, 0), 0))
    prompt = lambda width: pl.BlockSpec((1, tm, width), _prompt_spec(tm, seq, width, n_first).index_map)
    with_meta = lambda spec: pl.BlockSpec(spec.block_shape, lambda i, meta: spec.index_map(i))
    grid_spec = pltpu.PrefetchScalarGridSpec(
        num_scalar_prefetch=1,
        grid=(t // tm,),
        in_specs=[flat, flat,
                  pl.BlockSpec((tm * ROW_TILES, LANES), lambda i, meta: (i, 0)),
                  with_meta(prompt(PLE_DIM)), sample(PLE_DIM),
                  pl.BlockSpec(memory_space=pl.ANY),
                  pl.BlockSpec(plew.shape, fixed), pl.BlockSpec(plegw.shape, fixed),
                  pl.BlockSpec((1, D_MODEL), fixed), pl.BlockSpec((1, D_MODEL), fixed)],
        out_specs=[with_meta(prompt(D_MODEL)), sample(D_MODEL)],
        scratch_shapes=[pltpu.VMEM((2, TOP_K * tm * ROW_TILES, LANES), F32),
                        pltpu.VMEM((tm * ROW_TILES, LANES), F32),
                        pltpu.SemaphoreType.DMA((2,))],
    )
    return pl.pallas_call(
        functools.partial(_combine_kernel, tm=tm, n_first=n_first),
        grid_spec=grid_spec,
        out_shape=[jax.ShapeDtypeStruct((nb, seq, D_MODEL), F32), jax.ShapeDtypeStruct((ts, D_MODEL), F32)],
        compiler_params=_cparams(),
        name="moe_combine",
    )(meta, slots_flat, gates_flat, h1, p_prompt, p_sample, yb, plew, plegw, g2, b2)


def _row(v):
    return v.reshape(1, -1)


def kernel(x_prompt, x_sample, state_s5_re, state_s5_im, state_hgrn, p_prompt, p_sample, ln_in_g, ln_in_b, w_in, s5_lambda_re, s5_lambda_im, s5_log_step, s5_b_re, s5_b_im, s5_c_re, s5_c_im, s5_d, s5_w_glu, s5_b_glu, s5_norm_g, hg_lb, hg_norm_g, w_out, ln1_g, ln1_b, router_w, router_b, w_gate, b_gate, w_up, b_up, w_down, b_down, ple_w, ple_gate_w, ln2_g, ln2_b):
    nb, seq, _ = x_prompt.shape
    ns, dseq, _ = x_sample.shape
    assert dseq == CHUNK and seq % CHUNK == 0 and w_in.shape[0] == 1
    nc = seq // CHUNK
    tp, ts = nb * seq, ns * dseq
    t = tp + ts
    tm = 512 if (tp % 512 == 0 and ts % 512 == 0) else 256
    assert tp % tm == 0 and ts % tm == 0 and seq % tm == 0 and ns % nb == 0

    w_cols = jnp.split(w_in[0], [D_S5, D_S5 + D_HG, D_S5 + 2 * D_HG], axis=1)
    h0, u_p, u_s, f_p, f_s, z_p, z_s = ln_in_proj(
        x_prompt, x_sample.reshape(ts, D_MODEL), _row(ln_in_g), _row(ln_in_b),
        jnp.concatenate([w_cols[0], w_cols[2], w_cols[1], w_cols[3]], axis=1).astype(BF16), tm)

    m, w, wc, a = s5_prep(s5_lambda_re[0], s5_lambda_im[0], s5_log_step[0],
                          s5_b_re[0], s5_b_im[0], s5_c_re[0], s5_c_im[0])
    rb = 64 if (nb * nc) % 64 == 0 else nc
    yp_rows, ys_rows, fpr, fpi, fsr, fsi = s5_main(s5_rows(u_p, rb), s5_rows(u_s, ns), m, w, wc, a,
                                                   jnp.swapaxes(state_s5_re[0], 0, 1),
                                                   jnp.swapaxes(state_s5_im[0], 0, 1), nb, nc)
    y_pair = (s5_tokens(yp_rows, rb), s5_tokens(ys_rows, ns))

    zero_state = jnp.zeros((nb, HG_HEADS, HG_D, HG_D), F32)
    ng = _row(hg_norm_g[0])
    o_p, st_p = hgrn(f_p.reshape(nb, seq, D_HG), z_p.reshape(nb, seq, 3 * D_HG), zero_state, hg_lb, ng, nb,
                     HG_CHUNK if seq % HG_CHUNK == 0 else CHUNK)
    o_s, st_s = hgrn(f_s.reshape(ns, dseq, D_HG), z_s.reshape(ns, dseq, 3 * D_HG), jnp.swapaxes(state_hgrn[0], 2, 3),
                     hg_lb, ng, nb, dseq)

    h1, slots, gates, before, counts = post_mix(
        h0, (u_p, u_s), y_pair, (o_p.reshape(tp, D_HG), o_s.reshape(ts, D_HG)),
        _row(s5_d[0]), s5_w_glu[0].astype(BF16), _row(s5_b_glu[0]),
        _row(s5_norm_g[0]), w_out[0].astype(BF16), _row(ln1_g[0]), _row(ln1_b[0]),
        router_w[0].T, router_b[0].reshape(N_EXPERTS, 1), tm)

    counts = counts[:, 0].astype(jnp.int32)
    before = before[:, :, 0].astype(jnp.int32)
    cnt = jnp.concatenate([before[1:], counts[None]], axis=0) - before
    padded = (counts + MOE_ROWS - 1) // MOE_ROWS * MOE_ROWS
    pend = jnp.cumsum(padded)
    staged = jnp.cumsum(cnt, axis=1) - cnt
    meta = jnp.concatenate([pend - padded + before, cnt, staged, jnp.zeros_like(cnt)], axis=1)
    experts = jnp.arange(N_EXPERTS, dtype=jnp.int32)

    n_blocks = -(-t * TOP_K // MOE_ROWS) + N_EXPERTS
    n_used = (pend[-1] // MOE_ROWS).astype(jnp.int32)
    blk = jnp.arange(n_blocks, dtype=jnp.int32)
    blk = jnp.minimum(blk, n_used - 1)
    block_e = jnp.sum((pend[None, :] <= (blk * MOE_ROWS)[:, None]).astype(jnp.int32), axis=1)
    block_e = jnp.minimum(block_e, N_EXPERTS - 1)
    owns_rows = padded > 0
    segment = (jnp.cumsum(owns_rows.astype(jnp.int32)) - 1)[block_e]
    later = owns_rows[None, :] & (experts[None, :] > experts[:, None])
    next_owner = jnp.where(jnp.any(later, axis=1), jnp.argmax(later, axis=1), -1).astype(jnp.int32)
    next_e = next_owner[block_e]

    slots_flat = slots.reshape(-1)
    xs = dispatch(meta, pend, slots_flat, h1, n_blocks * MOE_ROWS, tm)
    yb = moe_ffn(block_e, n_used.reshape(1), segment, next_e, xs,
                 w_gate[0], b_gate[0][:, None, :], w_up[0], b_up[0][:, None, :],
                 w_down[0], b_down[0][:, None, :])
    out_p, out_s = combine(meta, slots_flat, gates.reshape(-1), h1, p_prompt[0], p_sample[0].reshape(ts, PLE_DIM),
                           yb, ple_w[0].astype(BF16), ple_gate_w[0].astype(BF16),
                           _row(ln2_g[0]), _row(ln2_b[0]), tm)

    def s5_state(f, n):
        return jnp.swapaxes(f, 0, 1).reshape(1, n, S5_GROUPS, S5_STATE)

    return (out_p, out_s.reshape(ns, dseq, D_MODEL),
            s5_state(fpr, nb), s5_state(fpi, nb), jnp.swapaxes(st_p, 2, 3)[None],
            s5_state(fsr, ns), s5_state(fsi, ns), jnp.swapaxes(st_s, 2, 3)[None])
```

```python
import functools

import jax
import jax.numpy as jnp
from jax import lax
from jax.experimental import pallas as pl
from jax.experimental.pallas import tpu as pltpu

F32 = jnp.float32
BF16 = jnp.bfloat16
HIGHEST = lax.Precision.HIGHEST

D_MODEL = 1024
CHUNK = 64
PLE_DIM = 256
D_S5 = 512
S5_GROUP = 16
S5_GROUPS = 32
S5_STATE = 64
D_HG = 512
HG_HEADS = 4
HG_D = 128
D_IN = D_S5 + 4 * D_HG
N_EXPERTS = 32
TOP_K = 4
SWIGLU_LIMIT = 7.0
SWIGLU_ALPHA = 1.702
DEEPNORM_ALPHA = 2.0 ** 0.25
LN_EPS = 1e-5
RMS_EPS = 1e-6

LANES = 128
SUBLANES = 8
ROW_TILES = D_MODEL // LANES
S5_CONV = CHUNK * S5_GROUP
HG_CHUNK = 128
MOE_ROWS = 512
VMEM_LIMIT = 56 * 1024 * 1024

assert ROW_TILES == SUBLANES


def _cparams(n_axes=1, flags=None):
    return pltpu.CompilerParams(dimension_semantics=("arbitrary",) * n_axes,
                                vmem_limit_bytes=VMEM_LIMIT, flags=flags)


def _dot(a, b, precision=None):
    return jnp.dot(a, b, preferred_element_type=F32, precision=precision)


def _layernorm(x, g, b):
    mu = jnp.mean(x, axis=-1, keepdims=True)
    xc = x - mu
    var = jnp.mean(xc * xc, axis=-1, keepdims=True)
    return xc * lax.rsqrt(var + LN_EPS) * g + b


def _two_phase_specs(block, n_first):
    nd = len(block)
    first = pl.BlockSpec(block, lambda i: (jnp.minimum(i, n_first - 1),) + (0,) * (nd - 1))
    second = pl.BlockSpec(block, lambda i: (jnp.maximum(i - n_first, 0),) + (0,) * (nd - 1))
    return [first, second]


def _two_phase_lane_block_specs(tm, n_first):
    nblk = D_S5 // LANES
    first = pl.BlockSpec((nblk, tm, LANES), lambda i: (0, jnp.minimum(i, n_first - 1), 0))
    second = pl.BlockSpec((nblk, tm, LANES), lambda i: (0, jnp.maximum(i - n_first, 0), 0))
    return [first, second]


def _read_lane_blocks(ref):
    return jnp.concatenate([ref[j] for j in range(D_S5 // LANES)], axis=1)


def _prompt_spec(tm, seq, width, n_first):
    per_seq = seq // tm

    def index(i):
        ic = jnp.minimum(i, n_first - 1)
        return (ic // per_seq, ic % per_seq, 0)

    return pl.BlockSpec((1, tm, width), index)


def _chunk(rows, j):
    return pl.ds(j, rows, stride=ROW_TILES)


def _read_row_tiles(ref, rows):
    return jnp.concatenate([ref[_chunk(rows, j), :] for j in range(ROW_TILES)], axis=1)


def _write_row_tiles(ref, val, rows):
    for j in range(ROW_TILES):
        ref[_chunk(rows, j), :] = val[:, j * LANES:(j + 1) * LANES]


def _row_tiles(r, n=1):
    return pl.ds(pl.multiple_of(r * ROW_TILES, ROW_TILES), n * ROW_TILES)


def _row_tile(r):
    return _row_tiles(r)


def _ln_in_proj_kernel(xp_ref, xs_ref, g_ref, b_ref, w_ref, h_ref, up_ref, us_ref, fp_ref, fs_ref, zp_ref, zs_ref,
                       *, n_first):
    def phase(x, u_ref, f_ref, z_ref):
        h = _layernorm(x, g_ref[...], b_ref[...])
        h_ref[...] = h
        hb = h.astype(BF16)
        u = _dot(hb, w_ref[:, :D_S5])
        for j in range(D_S5 // LANES):
            u_ref[j] = u[:, j * LANES:(j + 1) * LANES]
        f_ref[...] = _dot(hb, w_ref[:, D_S5:D_S5 + D_HG])
        z_ref[...] = _dot(hb, w_ref[:, D_S5 + D_HG:]).astype(BF16)

    i = pl.program_id(0)
    pl.when(i < n_first)(lambda: phase(xp_ref[0], up_ref, fp_ref, zp_ref))
    pl.when(i >= n_first)(lambda: phase(xs_ref[...], us_ref, fs_ref, zs_ref))


def ln_in_proj(xp, xs, g, b, w_bf16, tm):
    nb, seq, _ = xp.shape
    tp, ts = nb * seq, xs.shape[0]
    n_first = tp // tm
    fixed = lambda i: (0, 0)
    return pl.pallas_call(
        functools.partial(_ln_in_proj_kernel, n_first=n_first),
        grid=((tp + ts) // tm,),
        in_specs=[_prompt_spec(tm, seq, D_MODEL, n_first), _two_phase_specs((tm, D_MODEL), n_first)[1]]
                 + [pl.BlockSpec((1, D_MODEL), fixed), pl.BlockSpec((1, D_MODEL), fixed),
                    pl.BlockSpec((D_MODEL, D_IN), fixed)],
        out_specs=[pl.BlockSpec((tm, D_MODEL), lambda i: (i, 0))]
                  + _two_phase_lane_block_specs(tm, n_first)
                  + _two_phase_specs((tm, D_HG), n_first)
                  + _two_phase_specs((tm, 3 * D_HG), n_first),
        out_shape=[jax.ShapeDtypeStruct((tp + ts, D_MODEL), F32),
                   jax.ShapeDtypeStruct((D_S5 // LANES, tp, LANES), F32),
                   jax.ShapeDtypeStruct((D_S5 // LANES, ts, LANES), F32),
                   jax.ShapeDtypeStruct((tp, D_HG), F32), jax.ShapeDtypeStruct((ts, D_HG), F32),
                   jax.ShapeDtypeStruct((tp, 3 * D_HG), BF16), jax.ShapeDtypeStruct((ts, 3 * D_HG), BF16)],
        compiler_params=_cparams(),
        name="ln_in_proj",
    )(xp, xs, g, b, w_bf16)


def _lane_block(j):
    return slice(j * LANES, (j + 1) * LANES)


def _granule_transpose(slabs):
    per_block = LANES // S5_GROUP
    granule = lax.broadcasted_iota(jnp.int32, (1, LANES), 1) // S5_GROUP
    x = list(slabs)
    for d in (4, 2, 1):
        keep = (granule & d) == 0
        y = [None] * per_block
        for i in range(per_block):
            if i & d == 0:
                y[i] = jnp.where(keep, x[i], pltpu.roll(x[i + d], d * S5_GROUP, axis=1))
                y[i + d] = jnp.where(keep, pltpu.roll(x[i], LANES - d * S5_GROUP, axis=1), x[i + d])
        x = y
    return x


def _s5_rows_kernel(u_ref, o_ref, *, chunks):
    per_block = LANES // S5_GROUP
    for gcol in range(D_S5 // LANES):
        for j in range(S5_CONV // LANES):
            by_time = [u_ref[gcol, pl.ds(per_block * j + sl, chunks, stride=CHUNK), :] for sl in range(per_block)]
            for gl, rows in enumerate(_granule_transpose(by_time)):
                o_ref[gcol * per_block + gl, :, _lane_block(j)] = rows.astype(BF16)


def s5_rows(u, chunks):
    nblk, t, _ = u.shape
    r = t // CHUNK
    return pl.pallas_call(
        functools.partial(_s5_rows_kernel, chunks=chunks),
        grid=(r // chunks,),
        in_specs=[pl.BlockSpec((nblk, chunks * CHUNK, LANES), lambda i: (0, i, 0))],
        out_specs=pl.BlockSpec((S5_GROUPS, chunks, S5_CONV), lambda i: (0, i, 0)),
        out_shape=jax.ShapeDtypeStruct((S5_GROUPS, r, S5_CONV), BF16),
        compiler_params=_cparams(),
        name="s5_rows",
    )(u)


def _s5_tokens_kernel(y_ref, o_ref, *, chunks):
    per_block = LANES // S5_GROUP
    for gcol in range(D_S5 // LANES):
        for j in range(S5_CONV // LANES):
            by_group = [y_ref[gcol * per_block + gl, :, _lane_block(j)].astype(F32) for gl in range(per_block)]
            for sl, rows in enumerate(_granule_transpose(by_group)):
                o_ref[gcol, pl.ds(per_block * j + sl, chunks, stride=CHUNK), :] = rows


def s5_tokens(y_rows, chunks):
    _, r, _ = y_rows.shape
    return pl.pallas_call(
        functools.partial(_s5_tokens_kernel, chunks=chunks),
        grid=(r // chunks,),
        in_specs=[pl.BlockSpec((S5_GROUPS, chunks, S5_CONV), lambda i: (0, i, 0))],
        out_specs=pl.BlockSpec((D_S5 // LANES, chunks * CHUNK, LANES), lambda i: (0, i, 0)),
        out_shape=jax.ShapeDtypeStruct((D_S5 // LANES, r * CHUNK, LANES), F32),
        compiler_params=_cparams(),
        name="s5_tokens",
    )(y_rows)


def _s5_prep_kernel(lrc_ref, lic_ref, lrr_ref, lir_ref, ls_ref, brt_ref, bit_ref,
                    brtt_ref, bitt_ref, crt_ref, cit_ref,
                    m_ref, w_ref, wc_ref, a_ref):
    step = jnp.exp(ls_ref[0])

    def discretise(lr_raw, li):
        lr = jnp.minimum(lr_raw, -1e-4)
        dr, di = lr * step, li * step
        mag = jnp.exp(dr)
        a_re, a_im = mag * jnp.cos(di), mag * jnp.sin(di)
        den = lr * lr + li * li
        nr = a_re - 1.0
        fr = (nr * lr + a_im * li) / den
        fi = (a_im * lr - nr * li) / den
        return dr, di, fr, fi

    dr_c, di_c, _, _ = discretise(lrc_ref[0], lic_ref[0])
    dr_r, di_r, fr_r, fi_r = discretise(lrr_ref[0], lir_ref[0])

    lane = lax.broadcasted_iota(jnp.int32, (1, S5_CONV), 1)
    t_row = lax.broadcasted_iota(jnp.int32, (1, CHUNK), 1).astype(F32)
    t_col = lax.broadcasted_iota(jnp.int32, (CHUNK, 1), 0).astype(F32)
    lag_of_lane = (lax.broadcasted_iota(jnp.int32, (CHUNK, S5_CONV), 1) // S5_GROUP
                   == lax.broadcasted_iota(jnp.int32, (CHUNK, S5_CONV), 0)).astype(F32)
    time_of_row = (lax.broadcasted_iota(jnp.int32, (S5_CONV, CHUNK), 0) // S5_GROUP
                   == lax.broadcasted_iota(jnp.int32, (S5_CONV, CHUNK), 1)).astype(F32)

    def c_times_power(tf):
        mag = jnp.exp(dr_c * tf)
        ang = di_c * tf
        pr = _dot(mag * jnp.cos(ang), lag_of_lane, HIGHEST)
        pi = _dot(mag * jnp.sin(ang), lag_of_lane, HIGHEST)
        ctr, cti = crt_ref[0], cit_ref[0]
        return ctr * pr - cti * pi, ctr * pi + cti * pr

    cpr, cpi = c_times_power(t_row)
    bbr = fr_r * brt_ref[0] - fi_r * bit_ref[0]
    bbi = fr_r * bit_ref[0] + fi_r * brt_ref[0]
    kt = _dot(bbr, cpr, HIGHEST) - _dot(bbi, cpi, HIGHEST)
    for s in range(CHUNK):
        shifted = kt if s == 0 else pltpu.roll(kt, S5_GROUP * s, axis=1)
        m_ref[0, S5_GROUP * s:S5_GROUP * (s + 1), :] = jnp.where(
            lane >= S5_GROUP * s, shifted, 0.0).astype(BF16)

    rem = CHUNK - 1.0 - t_col
    magw = jnp.exp(dr_r * rem)
    angw = di_r * rem
    pwr = _dot(time_of_row, magw * jnp.cos(angw), HIGHEST)
    pwi = _dot(time_of_row, magw * jnp.sin(angw), HIGHEST)
    bbtr = fr_r * brtt_ref[0] - fi_r * bitt_ref[0]
    bbti = fr_r * bitt_ref[0] + fi_r * brtt_ref[0]
    w_ref[0, :, :S5_STATE] = pwr * bbtr - pwi * bbti
    w_ref[0, :, S5_STATE:] = pwr * bbti + pwi * bbtr

    c1r, c1i = c_times_power(t_row + 1.0)
    wc_ref[0, :S5_STATE, :] = c1r.astype(BF16)
    wc_ref[0, S5_STATE:, :] = (-c1i).astype(BF16)

    full = float(CHUNK)
    mag_c = jnp.exp(dr_r * full)
    a_ref[0, 0:1, :] = mag_c * jnp.cos(di_r * full)
    a_ref[0, 1:2, :] = mag_c * jnp.sin(di_r * full)


def s5_prep(lam_re, lam_im, log_step, b_re, b_im, c_re, c_im):
    g, p = lam_re.shape
    brt = jnp.swapaxes(b_re, 1, 2)
    bit = jnp.swapaxes(b_im, 1, 2)
    crt = jnp.tile(jnp.swapaxes(c_re, 1, 2), (1, 1, CHUNK))
    cit = jnp.tile(jnp.swapaxes(c_im, 1, 2), (1, 1, CHUNK))
    args = (lam_re.reshape(g, p, 1), lam_im.reshape(g, p, 1),
            lam_re.reshape(g, 1, p), lam_im.reshape(g, 1, p), log_step.reshape(g, 1, 1),
            brt, bit, jnp.tile(brt, (1, CHUNK, 1)), jnp.tile(bit, (1, CHUNK, 1)), crt, cit)
    spec = lambda a: pl.BlockSpec((1,) + a.shape[1:], lambda i: (i, 0, 0))
    out_shape = [jax.ShapeDtypeStruct((g, S5_CONV, S5_CONV), BF16),
                 jax.ShapeDtypeStruct((g, S5_CONV, 2 * S5_STATE), F32),
                 jax.ShapeDtypeStruct((g, 2 * S5_STATE, S5_CONV), BF16),
                 jax.ShapeDtypeStruct((g, 2, S5_STATE), F32)]
    return pl.pallas_call(
        _s5_prep_kernel,
        grid=(g,),
        in_specs=[spec(a) for a in args],
        out_specs=[spec(o) for o in out_shape],
        out_shape=out_shape,
        compiler_params=_cparams(),
        name="s5_prep",
    )(*args)


def _split3(w):
    hi = w.astype(BF16)
    r1 = w - hi.astype(F32)
    mid = r1.astype(BF16)
    lo = (r1 - mid.astype(F32)).astype(BF16)
    return hi, mid, lo


def _s5_main_kernel(up_ref, us_ref, m_ref, w_ref, wc_ref, a_ref, xsr_ref, xsi_ref,
                    yp_ref, ys_ref, fpr_ref, fpi_ref, fsr_ref, fsi_ref, fin_scr, *, n_prompt, n_chunks):
    w3 = jnp.concatenate(_split3(w_ref[0]), axis=1)
    ar = a_ref[0, 0:1, :]
    ai = a_ref[0, 1:2, :]
    width = 2 * S5_STATE

    def local(u):
        h3 = _dot(u, w3)
        return _dot(u, m_ref[0]), h3[:, :width] + h3[:, width:2 * width] + h3[:, 2 * width:]

    def times(pr, pi, x):
        return (jnp.concatenate([pr, pr], axis=1) * x
                + jnp.concatenate([-pi, pi], axis=1) * pltpu.roll(x, S5_STATE, axis=1))

    y_local, x = local(up_ref[0])
    chunk_of_row = lax.broadcasted_iota(jnp.int32, (n_prompt * n_chunks, 1), 0) % n_chunks
    pr, pi = ar, ai
    d = 1
    while d < n_chunks:
        x = x + times(pr, pi, jnp.where(chunk_of_row >= d, pltpu.roll(x, d, axis=0), 0.0))
        pr, pi = pr * pr - pi * pi, 2.0 * pr * pi
        d *= 2
    fin_scr[...] = x
    last = fin_scr[pl.ds(n_chunks - 1, n_prompt, stride=n_chunks), :]
    fpr_ref[0] = last[:, :S5_STATE]
    fpi_ref[0] = last[:, S5_STATE:]
    x0 = jnp.where(chunk_of_row >= 1, pltpu.roll(x, 1, axis=0), 0.0)
    yp_ref[0] = (y_local + _dot(x0.astype(BF16), wc_ref[0])).astype(BF16)

    y_local, hend = local(us_ref[0])
    x0 = jnp.concatenate([xsr_ref[0], xsi_ref[0]], axis=1)
    fin = times(ar, ai, x0) + hend
    fsr_ref[0] = fin[:, :S5_STATE]
    fsi_ref[0] = fin[:, S5_STATE:]
    ys_ref[0] = (y_local + _dot(x0.astype(BF16), wc_ref[0])).astype(BF16)


def s5_main(up_rows, us_rows, m, w, wc, a, xs_re, xs_im, n_prompt, n_chunks):
    g, r, _ = up_rows.shape
    n_sample = xs_re.shape[1]
    spec = lambda shape: pl.BlockSpec((1,) + tuple(shape[1:]), lambda i: (i, 0, 0))
    args = (up_rows, us_rows, m, w, wc, a, xs_re, xs_im)
    out_shape = [jax.ShapeDtypeStruct((g, r, S5_CONV), BF16),
                 jax.ShapeDtypeStruct((g, n_sample, S5_CONV), BF16),
                 jax.ShapeDtypeStruct((g, n_prompt, S5_STATE), F32),
                 jax.ShapeDtypeStruct((g, n_prompt, S5_STATE), F32),
                 jax.ShapeDtypeStruct((g, n_sample, S5_STATE), F32),
                 jax.ShapeDtypeStruct((g, n_sample, S5_STATE), F32)]
    return pl.pallas_call(
        functools.partial(_s5_main_kernel, n_prompt=n_prompt, n_chunks=n_chunks),
        grid=(g,),
        in_specs=[spec(x.shape) for x in args],
        out_specs=[spec(o.shape) for o in out_shape],
        out_shape=out_shape,
        scratch_shapes=[pltpu.VMEM((r, 2 * S5_STATE), F32)],
        compiler_params=_cparams(),
        name="s5_main",
    )(*args)


def _hgrn_kernel(f_ref, z_ref, s0_ref, lb_ref, ng_ref, o_ref, sfin_ref, st_scr, *, n_seq, chunk):
    c = pl.program_id(1)

    @pl.when(c == 0)
    def _():
        st_scr[...] = s0_ref[...]

    lbw = lb_ref[...]
    lbe = jnp.exp(lbw - jnp.max(lbw, axis=0, keepdims=True))
    lb_all = lbe[0:1, :] / jnp.sum(lbe, axis=0, keepdims=True)

    levels = [chunk >> (i + 1) for i in range(chunk.bit_length() - 1)]
    rowi = lax.broadcasted_iota(jnp.int32, (chunk, chunk), 0)
    coli = lax.broadcasted_iota(jnp.int32, (chunk, chunk), 1)
    rowk = lax.broadcasted_iota(jnp.int32, (chunk, HG_D), 0)
    cum_rows = [(coli <= rowi).astype(F32)]
    upper, lower, sign, same_block = [], [], [], []
    for m in levels:
        ref_row = (rowi // (2 * m)) * (2 * m) + (m - 1)
        if 2 * m < 2 * SUBLANES:
            cum_rows.append((coli <= ref_row).astype(F32))
        in_upper = (rowk % (2 * m)) >= m
        upper.append(in_upper.astype(F32).astype(BF16))
        lower.append(1.0 - upper[-1])
        sign.append(jnp.where(in_upper, 1.0, -1.0))
        same_block.append((rowi // (2 * m)) == (coli // (2 * m)))
    cum_mat = jnp.concatenate(cum_rows, axis=0).astype(BF16)
    cum_mat3 = jnp.concatenate([cum_mat] * 3, axis=1)
    diag = rowi == coli
    nt = (((1,), (1,)), ((), ()))

    def body(n, carry):
        zf = f_ref[n]
        fg_all = lb_all + (1.0 - lb_all) * jax.nn.sigmoid(zf)
        cums = _dot(cum_mat3, jnp.concatenate(_split3(jnp.log2(fg_all)), axis=0))
        for hd in range(HG_HEADS):
            cols = slice(hd * HG_D, (hd + 1) * HG_D)
            zq = z_ref[n, :, hd * HG_D:(hd + 1) * HG_D].astype(F32)
            vb = z_ref[n, :, D_HG + hd * HG_D:D_HG + (hd + 1) * HG_D]
            zg = z_ref[n, :, 2 * D_HG + hd * HG_D:2 * D_HG + (hd + 1) * HG_D].astype(F32)
            q = zq * jax.nn.sigmoid(zq)
            kk = 1.0 - fg_all[:, cols]
            bcum = cums[:chunk, cols]
            b_last = bcum[chunk - 1:chunk, :]
            qb = q.astype(BF16)
            kb = kk.astype(BF16)
            st = st_scr[n, hd]

            scores = jnp.where(diag, lax.dot_general(qb, kb, nt, preferred_element_type=F32), 0.0)
            n_wide = sum(1 for m in levels if 2 * m >= 2 * SUBLANES)
            for lvl, m in enumerate(levels):
                if lvl < n_wide:
                    bref = jnp.concatenate(
                        [jnp.broadcast_to(bcum[b * 2 * m + m - 1:b * 2 * m + m, :], (2 * m, HG_D))
                         for b in range(chunk // (2 * m))], axis=0)
                else:
                    row0 = (lvl - n_wide + 1) * chunk
                    bref = cums[row0:row0 + chunk, cols]
                dec = jnp.exp2((bcum - bref) * sign[lvl]).astype(BF16)
                sc = lax.dot_general(qb * upper[lvl] * dec, kb * lower[lvl] * dec, nt,
                                     preferred_element_type=F32)
                scores = scores + jnp.where(same_block[lvl], sc, 0.0)

            qd = (q * jnp.exp2(bcum)).astype(BF16)
            o = lax.dot_general(qd, st.astype(BF16), nt, preferred_element_type=F32)
            o = o + _dot(scores.astype(BF16), vb)
            kdec = (kk * jnp.exp2(b_last - bcum)).astype(BF16)
            st_scr[n, hd] = jnp.exp2(b_last) * st + lax.dot_general(
                vb, kdec, (((0,), (0,)), ((), ())), preferred_element_type=F32)

            on = o * lax.rsqrt(jnp.mean(o * o, axis=-1, keepdims=True) + RMS_EPS) * ng_ref[:, cols]
            o_ref[n, :, hd * HG_D:(hd + 1) * HG_D] = on * (zg * jax.nn.sigmoid(zg))
        return carry

    lax.fori_loop(0, n_seq, body, 0, unroll=True)

    @pl.when(c == pl.num_programs(1) - 1)
    def _():
        sfin_ref[...] = st_scr[...]


def hgrn(f, z, s0_t, hg_lb, norm_g, n_seq, chunk):
    n, length, _ = z.shape
    return pl.pallas_call(
        functools.partial(_hgrn_kernel, n_seq=n_seq, chunk=chunk),
        grid=(n // n_seq, length // chunk),
        in_specs=[pl.BlockSpec((n_seq, chunk, D_HG), lambda g, c: (g, c, 0)),
                  pl.BlockSpec((n_seq, chunk, 3 * D_HG), lambda g, c: (g, c, 0)),
                  pl.BlockSpec((n_seq, HG_HEADS, HG_D, HG_D), lambda g, c: (g, 0, 0, 0)),
                  pl.BlockSpec(hg_lb.shape, lambda g, c: (0, 0)),
                  pl.BlockSpec((1, D_HG), lambda g, c: (0, 0))],
        out_specs=[pl.BlockSpec((n_seq, chunk, D_HG), lambda g, c: (g, c, 0)),
                   pl.BlockSpec((n_seq, HG_HEADS, HG_D, HG_D), lambda g, c: (g, 0, 0, 0))],
        out_shape=[jax.ShapeDtypeStruct((n, length, D_HG), F32),
                   jax.ShapeDtypeStruct((n, HG_HEADS, HG_D, HG_D), F32)],
        scratch_shapes=[pltpu.VMEM((n_seq, HG_HEADS, HG_D, HG_D), F32)],
        compiler_params=_cparams(2),
        name="hgrn",
    )(f, z, s0_t, hg_lb, norm_g)


def _post_mix_kernel(h_ref, up_ref, us_ref, yp_ref, ys_ref, hgp_ref, hgs_ref,
                     d_ref, wglu_ref, bglu_ref, s5g_ref, wout_ref, g1_ref, b1_ref, rwt_ref, rb_ref,
                     h1_ref, slot_ref, gate_ref, before_ref, cnt_ref, run_scr, *, tm, n_first):
    i = pl.program_id(0)

    @pl.when(i == 0)
    def _():
        run_scr[...] = jnp.zeros_like(run_scr)

    def phase(u_ref, y_ref, hg_ref):
        ys = _read_lane_blocks(y_ref) + d_ref[...] * _read_lane_blocks(u_ref)
        gl = 0.5 * ys * (1.0 + lax.erf(ys * (2.0 ** -0.5)))
        s5o = gl * jax.nn.sigmoid(_dot(gl.astype(BF16), wglu_ref[...]) + bglu_ref[...])
        s5o = s5o * lax.rsqrt(jnp.mean(s5o * s5o, axis=-1, keepdims=True) + RMS_EPS) * s5g_ref[...]
        mix = (_dot(s5o.astype(BF16), wout_ref[:D_S5, :])
               + _dot(hg_ref[...].astype(BF16), wout_ref[D_S5:, :]))
        h1 = _layernorm(DEEPNORM_ALPHA * h_ref[...] + mix, g1_ref[...], b1_ref[...])
        _write_row_tiles(h1_ref, h1, tm)

        h_hi, h_mid, _ = _split3(h1)
        w_hi, w_mid, _ = _split3(rwt_ref[...])
        nt = (((1,), (1,)), ((), ()))
        logits = (lax.dot_general(w_hi, h_hi, nt, preferred_element_type=F32)
                  + lax.dot_general(w_hi, h_mid, nt, preferred_element_type=F32)
                  + lax.dot_general(w_mid, h_hi, nt, preferred_element_type=F32)) + rb_ref[...]
        eid = lax.broadcasted_iota(jnp.int32, (N_EXPERTS, tm), 0)
        vals, idxs = [], []
        for _ in range(TOP_K):
            m = jnp.max(logits, axis=0, keepdims=True)
            ix = jnp.min(jnp.where(logits == m, eid, N_EXPERTS), axis=0, keepdims=True)
            vals.append(m)
            idxs.append(ix)
            logits = jnp.where(eid == ix, -jnp.inf, logits)
        exps = [jnp.exp(v - vals[0]) for v in vals]
        den = exps[0] + exps[1] + exps[2] + exps[3]

        onehot = jnp.zeros((N_EXPERTS, tm), F32)
        for ix in idxs:
            onehot = onehot + (eid == ix).astype(F32)
        rowi = lax.broadcasted_iota(jnp.int32, (tm, tm), 0)
        coli = lax.broadcasted_iota(jnp.int32, (tm, tm), 1)
        earlier = (rowi < coli).astype(BF16)
        prefix = _dot(onehot.astype(BF16), earlier)
        tile_cnt = jnp.sum(onehot, axis=1, keepdims=True)
        for k in range(TOP_K):
            lower_experts = jnp.sum(jnp.where(eid < idxs[k], tile_cnt, 0.0), axis=0, keepdims=True)
            rank = jnp.sum(jnp.where(eid == idxs[k], prefix, 0.0), axis=0, keepdims=True)
            slot_ref[0, :, k * tm:(k + 1) * tm] = (lower_experts + rank).astype(jnp.int32)
            gate_ref[0, :, k * tm:(k + 1) * tm] = exps[k] / den
        before_ref[0] = run_scr[...]
        run_scr[...] = run_scr[...] + tile_cnt
        cnt_ref[...] = run_scr[...]

    pl.when(i < n_first)(lambda: phase(up_ref, yp_ref, hgp_ref))
    pl.when(i >= n_first)(lambda: phase(us_ref, ys_ref, hgs_ref))


def post_mix(h0, u_pair, y_pair, hg_pair, d_skip, wglu, bglu, s5g, wout, g1, b1, rw_t, rb_col, tm):
    t = h0.shape[0]
    n_first = u_pair[0].shape[1] // tm
    row = lambda i: (i, 0)
    fixed = lambda i: (0, 0)
    full = lambda a: pl.BlockSpec(a.shape, fixed)
    weights = (d_skip, wglu, bglu, s5g, wout, g1, b1, rw_t, rb_col)
    return pl.pallas_call(
        functools.partial(_post_mix_kernel, tm=tm, n_first=n_first),
        grid=(t // tm,),
        in_specs=[pl.BlockSpec((tm, D_MODEL), row)]
                 + _two_phase_lane_block_specs(tm, n_first) * 2 + _two_phase_specs((tm, D_HG), n_first)
                 + [full(a) for a in weights],
        out_specs=[pl.BlockSpec((tm * ROW_TILES, LANES), row),
                   pl.BlockSpec((1, 1, TOP_K * tm), lambda i: (i, 0, 0)),
                   pl.BlockSpec((1, 1, TOP_K * tm), lambda i: (i, 0, 0)),
                   pl.BlockSpec((1, N_EXPERTS, 1), lambda i: (i, 0, 0)),
                   pl.BlockSpec((N_EXPERTS, 1), fixed)],
        out_shape=[jax.ShapeDtypeStruct((t * ROW_TILES, LANES), F32),
                   jax.ShapeDtypeStruct((t // tm, 1, TOP_K * tm), jnp.int32),
                   jax.ShapeDtypeStruct((t // tm, 1, TOP_K * tm), F32),
                   jax.ShapeDtypeStruct((t // tm, N_EXPERTS, 1), F32),
                   jax.ShapeDtypeStruct((N_EXPERTS, 1), F32)],
        scratch_shapes=[pltpu.VMEM((N_EXPERTS, 1), F32)],
        compiler_params=_cparams(),
        name="post_mix",
    )(h0, *u_pair, *y_pair, *hg_pair, *weights)


def _segment_copies(meta_ref, tile, tm, make_copy):
    for e in range(N_EXPERTS):
        sorted_row = meta_ref[tile, e]
        cnt = meta_ref[tile, N_EXPERTS + e]
        staged_row = meta_ref[tile, 2 * N_EXPERTS + e]
        for b in range(tm.bit_length()):
            done = cnt & ((1 << b) - 1)

            @pl.when(((cnt >> b) & 1) == 1)
            def _(b=b, done=done, e=e):
                make_copy(staged_row + done, sorted_row + done, 1 << b).start(priority=e % 2)


def _dispatch_kernel(meta_ref, pend_ref, slot_ref, h_ref, xs_ref, stage, zero_scr, zsem, sem, *, tm):
    n_rows = xs_ref.shape[0] // ROW_TILES
    i = pl.program_id(0)
    n = pl.num_programs(0)
    slot = i % 2

    def drain(s):
        pltpu.make_async_copy(stage.at[s], xs_ref.at[pl.ds(0, TOP_K * tm * ROW_TILES)], sem.at[s]).wait()

    @pl.when(i == 0)
    def _():
        zero_scr[...] = jnp.zeros_like(zero_scr)

        def last_block(e):
            prev = pend_ref[e - 1] if e > 0 else 0
            copy = pltpu.make_async_copy(
                zero_scr, xs_ref.at[pl.ds(pl.multiple_of(jnp.maximum(pend_ref[e] - MOE_ROWS, 0) * ROW_TILES,
                                                         ROW_TILES), MOE_ROWS * ROW_TILES)], zsem)
            return pend_ref[e] > prev, copy

        def tail_block(j):
            row0 = pend_ref[N_EXPERTS - 1] + j * MOE_ROWS
            copy = pltpu.make_async_copy(
                zero_scr, xs_ref.at[pl.ds(pl.multiple_of(jnp.minimum(row0, n_rows - MOE_ROWS) * ROW_TILES,
                                                         ROW_TILES), MOE_ROWS * ROW_TILES)], zsem)
            return row0 < n_rows, copy

        blocks = [last_block(e) for e in range(N_EXPERTS)] + [tail_block(j) for j in range(N_EXPERTS)]
        for used, copy in blocks:
            pl.when(used)(copy.start)
        for used, copy in blocks:
            pl.when(used)(copy.wait)

    pl.when(i >= 2)(lambda: drain(slot))

    def body(t, carry):
        row = h_ref[_row_tile(t), :]
        for k in range(TOP_K):
            stage[slot, _row_tile(slot_ref[k * tm + t]), :] = row
        return carry

    lax.fori_loop(0, tm, body, 0, unroll=8)
    _segment_copies(meta_ref, i, tm, lambda staged_row, sorted_row, rows: pltpu.make_async_copy(
        stage.at[slot, _row_tiles(staged_row, rows)], xs_ref.at[_row_tiles(sorted_row, rows)], sem.at[slot]))

    @pl.when(i == n - 1)
    def _():
        pl.when(n >= 2)(lambda: drain(1 - slot))
        drain(slot)


def dispatch(meta, pend, slots_flat, h1, n_rows, tm):
    t = h1.shape[0] // ROW_TILES
    grid_spec = pltpu.PrefetchScalarGridSpec(
        num_scalar_prefetch=2,
        grid=(t // tm,),
        in_specs=[pl.BlockSpec((TOP_K * tm,), lambda i, meta, pend: (i,), memory_space=pltpu.SMEM),
                  pl.BlockSpec((tm * ROW_TILES, LANES), lambda i, meta, pend: (i, 0))],
        out_specs=pl.BlockSpec(memory_space=pl.ANY),
        scratch_shapes=[pltpu.VMEM((2, TOP_K * tm * ROW_TILES, LANES), F32),
                        pltpu.VMEM((MOE_ROWS * ROW_TILES, LANES), F32),
                        pltpu.SemaphoreType.DMA(()), pltpu.SemaphoreType.DMA((2,))],
    )
    return pl.pallas_call(
        functools.partial(_dispatch_kernel, tm=tm),
        grid_spec=grid_spec,
        out_shape=jax.ShapeDtypeStruct((n_rows * ROW_TILES, LANES), F32),
        compiler_params=_cparams(),
        name="moe_dispatch",
    )(meta, pend, slots_flat, h1)


def _moe_ffn_kernel(be_ref, nu_ref, seg_ref, nxt_ref, x_ref, wg_ref, bg_ref, wu_ref, bu_ref, wd_ref, bd_ref,
                    y_ref, wbuf, wbf, sem):
    i = pl.program_id(0)
    hbm = (wg_ref, wu_ref, wd_ref)

    def weight_copies(expert, s):
        return [pltpu.make_async_copy(hbm[j].at[expert], wbuf.at[s, j], sem.at[s, j]) for j in range(3)]

    @pl.when((i == 0) | (be_ref[i] != be_ref[jnp.maximum(i - 1, 0)]))
    def _():
        s = seg_ref[i] % 2

        @pl.when(i == 0)
        def _():
            for c in weight_copies(be_ref[0], 0):
                c.start()

        for j, c in enumerate(weight_copies(be_ref[i], s)):
            c.wait()
            wbf[j] = wbuf[s, j].astype(BF16)

        @pl.when(nxt_ref[i] >= 0)
        def _():
            for c in weight_copies(nxt_ref[i], 1 - s):
                c.start()

    @pl.when(i < nu_ref[0])
    def _():
        x = _read_row_tiles(x_ref, MOE_ROWS).astype(BF16)
        gt = jnp.minimum(_dot(x, wbf[0]) + bg_ref[0], SWIGLU_LIMIT)
        up = jnp.clip(_dot(x, wbf[1]) + bu_ref[0], -SWIGLU_LIMIT, SWIGLU_LIMIT)
        hid = (up + 1.0) * (gt * jax.nn.sigmoid(SWIGLU_ALPHA * gt))
        _write_row_tiles(y_ref, _dot(hid.astype(BF16), wbf[2]) + bd_ref[0], MOE_ROWS)

    @pl.when(i >= nu_ref[0])
    def _():
        y_ref[...] = jnp.zeros_like(y_ref)


def moe_ffn(block_e, n_used, segment, next_e, xs, wg, bg, wu, bu, wd, bd):
    n_rows = xs.shape[0] // ROW_TILES
    n_blocks = n_rows // MOE_ROWS
    wsel = lambda i, be, nu, seg, nxt: (be[i], 0, 0)
    d_ff = wg.shape[-1]
    assert wg.shape[1:] == wu.shape[1:] == wd.shape[1:] == (D_MODEL, D_MODEL)
    anywhere = pl.BlockSpec(memory_space=pl.ANY)
    grid_spec = pltpu.PrefetchScalarGridSpec(
        num_scalar_prefetch=4,
        grid=(n_blocks,),
        in_specs=[pl.BlockSpec((MOE_ROWS * ROW_TILES, LANES),
                               lambda i, be, nu, seg, nxt: (jnp.minimum(i, nu[0] - 1), 0)),
                  anywhere, pl.BlockSpec((1, 1, d_ff), wsel),
                  anywhere, pl.BlockSpec((1, 1, d_ff), wsel),
                  anywhere, pl.BlockSpec((1, 1, D_MODEL), wsel)],
        out_specs=pl.BlockSpec((MOE_ROWS * ROW_TILES, LANES), lambda i, be, nu, seg, nxt: (i, 0)),
        scratch_shapes=[pltpu.VMEM((2, 3, D_MODEL, D_MODEL), F32), pltpu.VMEM((3, D_MODEL, D_MODEL), BF16),
                        pltpu.SemaphoreType.DMA((2, 3))],
    )
    return pl.pallas_call(
        _moe_ffn_kernel,
        grid_spec=grid_spec,
        out_shape=jax.ShapeDtypeStruct((n_rows * ROW_TILES, LANES), F32),
        compiler_params=_cparams(),
        name="moe_ffn",
    )(block_e, n_used, segment, next_e, xs, wg, bg, wu, bu, wd, bd)


def _combine_kernel(meta_ref, slot_ref, gate_ref, h_ref, pp_ref, ps_ref, yb_ref,
                    plew_ref, plegw_ref, g2_ref, b2_ref, outp_ref, outs_ref, buf, r_scr, sem,
                    *, tm, n_first):
    i = pl.program_id(0)
    n = pl.num_programs(0)
    slot = i % 2

    def fetch(tile, s):
        _segment_copies(meta_ref, tile, tm, lambda staged_row, sorted_row, rows: pltpu.make_async_copy(
            yb_ref.at[_row_tiles(sorted_row, rows)], buf.at[s, _row_tiles(staged_row, rows)], sem.at[s]))

    pl.when(i == 0)(lambda: fetch(0, 0))
    pl.when(i + 1 < n)(lambda: fetch(i + 1, 1 - slot))
    pltpu.make_async_copy(yb_ref.at[pl.ds(0, TOP_K * tm * ROW_TILES)], buf.at[slot], sem.at[slot]).wait()

    def body(t, carry):
        acc = DEEPNORM_ALPHA * h_ref[_row_tile(t), :]
        for k in range(TOP_K):
            acc = acc + gate_ref[k * tm + t] * buf[slot, _row_tile(slot_ref[k * tm + t]), :]
        r_scr[_row_tile(t), :] = acc
        return carry

    lax.fori_loop(0, tm, body, 0, unroll=8)
    r = _read_row_tiles(r_scr, tm)
    gate = jax.nn.sigmoid(_dot(r.astype(BF16), plegw_ref[...]))

    def finish(p, store):
        e = _dot(p.astype(BF16), plew_ref[...]) * gate
        store(_layernorm(r + e, g2_ref[...], b2_ref[...]))

    def store_prompt(v):
        outp_ref[0] = v

    def store_sample(v):
        outs_ref[...] = v

    pl.when(i < n_first)(lambda: finish(pp_ref[0], store_prompt))
    pl.when(i >= n_first)(lambda: finish(ps_ref[...], store_sample))


def combine(meta, slots_flat, gates_flat, h1, p_prompt, p_sample, yb, plew, plegw, g2, b2, tm):
    t = h1.shape[0] // ROW_TILES
    nb, seq, _ = p_prompt.shape
    ts = p_sample.shape[0]
    n_first = nb * seq // tm
    fixed = lambda i, meta: (0, 0)
    flat = pl.BlockSpec((TOP_K * tm,), lambda i, meta: (i,), memory_space=pltpu.SMEM)
    sample = lambda width: pl.BlockSpec((tm, width), lambda i, meta: (jnp.maximum(i - n_first, 0), 0))
    prompt = lambda width: pl.BlockSpec((1, tm, width), _prompt_spec(tm, seq, width, n_first).index_map)
    with_meta = lambda spec: pl.BlockSpec(spec.block_shape, lambda i, meta: spec.index_map(i))
    grid_spec = pltpu.PrefetchScalarGridSpec(
        num_scalar_prefetch=1,
        grid=(t // tm,),
        in_specs=[flat, flat,
                  pl.BlockSpec((tm * ROW_TILES, LANES), lambda i, meta: (i, 0)),
                  with_meta(prompt(PLE_DIM)), sample(PLE_DIM),
                  pl.BlockSpec(memory_space=pl.ANY),
                  pl.BlockSpec(plew.shape, fixed), pl.BlockSpec(plegw.shape, fixed),
                  pl.BlockSpec((1, D_MODEL), fixed), pl.BlockSpec((1, D_MODEL), fixed)],
        out_specs=[with_meta(prompt(D_MODEL)), sample(D_MODEL)],
        scratch_shapes=[pltpu.VMEM((2, TOP_K * tm * ROW_TILES, LANES), F32),
                        pltpu.VMEM((tm * ROW_TILES, LANES), F32),
                        pltpu.SemaphoreType.DMA((2,))],
    )
    return pl.pallas_call(
        functools.partial(_combine_kernel, tm=tm, n_first=n_first),
        grid_spec=grid_spec,
        out_shape=[jax.ShapeDtypeStruct((nb, seq, D_MODEL), F32), jax.ShapeDtypeStruct((ts, D_MODEL), F32)],
        compiler_params=_cparams(),
        name="moe_combine",
    )(meta, slots_flat, gates_flat, h1, p_prompt, p_sample, yb, plew, plegw, g2, b2)


def _row(v):
    return v.reshape(1, -1)


def kernel(x_prompt, x_sample, state_s5_re, state_s5_im, state_hgrn, p_prompt, p_sample, ln_in_g, ln_in_b, w_in, s5_lambda_re, s5_lambda_im, s5_log_step, s5_b_re, s5_b_im, s5_c_re, s5_c_im, s5_d, s5_w_glu, s5_b_glu, s5_norm_g, hg_lb, hg_norm_g, w_out, ln1_g, ln1_b, router_w, router_b, w_gate, b_gate, w_up, b_up, w_down, b_down, ple_w, ple_gate_w, ln2_g, ln2_b):
    nb, seq, _ = x_prompt.shape
    ns, dseq, _ = x_sample.shape
    assert dseq == CHUNK and seq % CHUNK == 0 and w_in.shape[0] == 1
    nc = seq // CHUNK
    tp, ts = nb * seq, ns * dseq
    t = tp + ts
    tm = 512 if (tp % 512 == 0 and ts % 512 == 0) else 256
    assert tp % tm == 0 and ts % tm == 0 and seq % tm == 0 and ns % nb == 0

    w_cols = jnp.split(w_in[0], [D_S5, D_S5 + D_HG, D_S5 + 2 * D_HG], axis=1)
    h0, u_p, u_s, f_p, f_s, z_p, z_s = ln_in_proj(
        x_prompt, x_sample.reshape(ts, D_MODEL), _row(ln_in_g), _row(ln_in_b),
        jnp.concatenate([w_cols[0], w_cols[2], w_cols[1], w_cols[3]], axis=1).astype(BF16), tm)

    m, w, wc, a = s5_prep(s5_lambda_re[0], s5_lambda_im[0], s5_log_step[0],
                          s5_b_re[0], s5_b_im[0], s5_c_re[0], s5_c_im[0])
    rb = 64 if (nb * nc) % 64 == 0 else nc
    yp_rows, ys_rows, fpr, fpi, fsr, fsi = s5_main(s5_rows(u_p, rb), s5_rows(u_s, ns), m, w, wc, a,
                                                   jnp.swapaxes(state_s5_re[0], 0, 1),
                                                   jnp.swapaxes(state_s5_im[0], 0, 1), nb, nc)
    y_pair = (s5_tokens(yp_rows, rb), s5_tokens(ys_rows, ns))

    zero_state = jnp.zeros((nb, HG_HEADS, HG_D, HG_D), F32)
    ng = _row(hg_norm_g[0])
    o_p, st_p = hgrn(f_p.reshape(nb, seq, D_HG), z_p.reshape(nb, seq, 3 * D_HG), zero_state, hg_lb, ng, nb,
                     HG_CHUNK if seq % HG_CHUNK == 0 else CHUNK)
    o_s, st_s = hgrn(f_s.reshape(ns, dseq, D_HG), z_s.reshape(ns, dseq, 3 * D_HG), jnp.swapaxes(state_hgrn[0], 2, 3),
                     hg_lb, ng, nb, dseq)

    h1, slots, gates, before, counts = post_mix(
        h0, (u_p, u_s), y_pair, (o_p.reshape(tp, D_HG), o_s.reshape(ts, D_HG)),
        _row(s5_d[0]), s5_w_glu[0].astype(BF16), _row(s5_b_glu[0]),
        _row(s5_norm_g[0]), w_out[0].astype(BF16), _row(ln1_g[0]), _row(ln1_b[0]),
        router_w[0].T, router_b[0].reshape(N_EXPERTS, 1), tm)

    counts = counts[:, 0].astype(jnp.int32)
    before = before[:, :, 0].astype(jnp.int32)
    cnt = jnp.concatenate([before[1:], counts[None]], axis=0) - before
    padded = (counts + MOE_ROWS - 1) // MOE_ROWS * MOE_ROWS
    pend = jnp.cumsum(padded)
    staged = jnp.cumsum(cnt, axis=1) - cnt
    meta = jnp.concatenate([pend - padded + before, cnt, staged, jnp.zeros_like(cnt)], axis=1)
    experts = jnp.arange(N_EXPERTS, dtype=jnp.int32)

    n_blocks = -(-t * TOP_K // MOE_ROWS) + N_EXPERTS
    n_used = (pend[-1] // MOE_ROWS).astype(jnp.int32)
    blk = jnp.arange(n_blocks, dtype=jnp.int32)
    blk = jnp.minimum(blk, n_used - 1)
    block_e = jnp.sum((pend[None, :] <= (blk * MOE_ROWS)[:, None]).astype(jnp.int32), axis=1)
    block_e = jnp.minimum(block_e, N_EXPERTS - 1)
    owns_rows = padded > 0
    segment = (jnp.cumsum(owns_rows.astype(jnp.int32)) - 1)[block_e]
    later = owns_rows[None, :] & (experts[None, :] > experts[:, None])
    next_owner = jnp.where(jnp.any(later, axis=1), jnp.argmax(later, axis=1), -1).astype(jnp.int32)
    next_e = next_owner[block_e]

    slots_flat = slots.reshape(-1)
    xs = dispatch(meta, pend, slots_flat, h1, n_blocks * MOE_ROWS, tm)
    yb = moe_ffn(block_e, n_used.reshape(1), segment, next_e, xs,
                 w_gate[0], b_gate[0][:, None, :], w_up[0], b_up[0][:, None, :],
                 w_down[0], b_down[0][:, None, :])
    out_p, out_s = combine(meta, slots_flat, gates.reshape(-1), h1, p_prompt[0], p_sample[0].reshape(ts, PLE_DIM),
                           yb, ple_w[0].astype(BF16), ple_gate_w[0].astype(BF16),
                           _row(ln2_g[0]), _row(ln2_b[0]), tm)

    def s5_state(f, n):
        return jnp.swapaxes(f, 0, 1).reshape(1, n, S5_GROUPS, S5_STATE)

    return (out_p, out_s.reshape(ns, dseq, D_MODEL),
            s5_state(fpr, nb), s5_state(fpi, nb), jnp.swapaxes(st_p, 2, 3)[None],
            s5_state(fsr, ns), s5_state(fsi, ns), jnp.swapaxes(st_s, 2, 3)[None])
```

```python
import functools

import jax
import jax.numpy as jnp
from jax import lax
from jax.experimental import pallas as pl
from jax.experimental.pallas import tpu as pltpu

F32 = jnp.float32
BF16 = jnp.bfloat16
HIGHEST = lax.Precision.HIGHEST

D_MODEL = 1024
CHUNK = 64
PLE_DIM = 256
D_S5 = 512
S5_GROUP = 16
S5_GROUPS = 32
S5_STATE = 64
D_HG = 512
HG_HEADS = 4
HG_D = 128
D_IN = D_S5 + 4 * D_HG
N_EXPERTS = 32
TOP_K = 4
SWIGLU_LIMIT = 7.0
SWIGLU_ALPHA = 1.702
DEEPNORM_ALPHA = 2.0 ** 0.25
LN_EPS = 1e-5
RMS_EPS = 1e-6

LANES = 128
SUBLANES = 8
ROW_TILES = D_MODEL // LANES
S5_CONV = CHUNK * S5_GROUP
HG_CHUNK = 128
MOE_ROWS = 512
VMEM_LIMIT = 56 * 1024 * 1024

assert ROW_TILES == SUBLANES


def _cparams(n_axes=1, flags=None):
    return pltpu.CompilerParams(dimension_semantics=("arbitrary",) * n_axes,
                                vmem_limit_bytes=VMEM_LIMIT, flags=flags)


def _dot(a, b, precision=None):
    return jnp.dot(a, b, preferred_element_type=F32, precision=precision)


def _layernorm(x, g, b):
    mu = jnp.mean(x, axis=-1, keepdims=True)
    xc = x - mu
    var = jnp.mean(xc * xc, axis=-1, keepdims=True)
    return xc * lax.rsqrt(var + LN_EPS) * g + b


def _two_phase_specs(block, n_first):
    nd = len(block)
    first = pl.BlockSpec(block, lambda i: (jnp.minimum(i, n_first - 1),) + (0,) * (nd - 1))
    second = pl.BlockSpec(block, lambda i: (jnp.maximum(i - n_first, 0),) + (0,) * (nd - 1))
    return [first, second]


def _two_phase_lane_block_specs(tm, n_first):
    nblk = D_S5 // LANES
    first = pl.BlockSpec((nblk, tm, LANES), lambda i: (0, jnp.minimum(i, n_first - 1), 0))
    second = pl.BlockSpec((nblk, tm, LANES), lambda i: (0, jnp.maximum(i - n_first, 0), 0))
    return [first, second]


def _read_lane_blocks(ref):
    return jnp.concatenate([ref[j] for j in range(D_S5 // LANES)], axis=1)


def _prompt_spec(tm, seq, width, n_first):
    per_seq = seq // tm

    def index(i):
        ic = jnp.minimum(i, n_first - 1)
        return (ic // per_seq, ic % per_seq, 0)

    return pl.BlockSpec((1, tm, width), index)


def _chunk(rows, j):
    return pl.ds(j, rows, stride=ROW_TILES)


def _read_row_tiles(ref, rows):
    return jnp.concatenate([ref[_chunk(rows, j), :] for j in range(ROW_TILES)], axis=1)


def _write_row_tiles(ref, val, rows):
    for j in range(ROW_TILES):
        ref[_chunk(rows, j), :] = val[:, j * LANES:(j + 1) * LANES]


def _row_tiles(r, n=1):
    return pl.ds(pl.multiple_of(r * ROW_TILES, ROW_TILES), n * ROW_TILES)


def _row_tile(r):
    return _row_tiles(r)


def _ln_in_proj_kernel(xp_ref, xs_ref, g_ref, b_ref, w_ref, h_ref, up_ref, us_ref, fp_ref, fs_ref, zp_ref, zs_ref,
                       *, n_first):
    def phase(x, u_ref, f_ref, z_ref):
        h = _layernorm(x, g_ref[...], b_ref[...])
        h_ref[...] = h
        hb = h.astype(BF16)
        u = _dot(hb, w_ref[:, :D_S5])
        for j in range(D_S5 // LANES):
            u_ref[j] = u[:, j * LANES:(j + 1) * LANES]
        f_ref[...] = _dot(hb, w_ref[:, D_S5:D_S5 + D_HG])
        z_ref[...] = _dot(hb, w_ref[:, D_S5 + D_HG:]).astype(BF16)

    i = pl.program_id(0)
    pl.when(i < n_first)(lambda: phase(xp_ref[0], up_ref, fp_ref, zp_ref))
    pl.when(i >= n_first)(lambda: phase(xs_ref[...], us_ref, fs_ref, zs_ref))


def ln_in_proj(xp, xs, g, b, w_bf16, tm):
    nb, seq, _ = xp.shape
    tp, ts = nb * seq, xs.shape[0]
    n_first = tp // tm
    fixed = lambda i: (0, 0)
    return pl.pallas_call(
        functools.partial(_ln_in_proj_kernel, n_first=n_first),
        grid=((tp + ts) // tm,),
        in_specs=[_prompt_spec(tm, seq, D_MODEL, n_first), _two_phase_specs((tm, D_MODEL), n_first)[1]]
                 + [pl.BlockSpec((1, D_MODEL), fixed), pl.BlockSpec((1, D_MODEL), fixed),
                    pl.BlockSpec((D_MODEL, D_IN), fixed)],
        out_specs=[pl.BlockSpec((tm, D_MODEL), lambda i: (i, 0))]
                  + _two_phase_lane_block_specs(tm, n_first)
                  + _two_phase_specs((tm, D_HG), n_first)
                  + _two_phase_specs((tm, 3 * D_HG), n_first),
        out_shape=[jax.ShapeDtypeStruct((tp + ts, D_MODEL), F32),
                   jax.ShapeDtypeStruct((D_S5 // LANES, tp, LANES), F32),
                   jax.ShapeDtypeStruct((D_S5 // LANES, ts, LANES), F32),
                   jax.ShapeDtypeStruct((tp, D_HG), F32), jax.ShapeDtypeStruct((ts, D_HG), F32),
                   jax.ShapeDtypeStruct((tp, 3 * D_HG), BF16), jax.ShapeDtypeStruct((ts, 3 * D_HG), BF16)],
        compiler_params=_cparams(),
        name="ln_in_proj",
    )(xp, xs, g, b, w_bf16)


def _lane_block(j):
    return slice(j * LANES, (j + 1) * LANES)


def _granule_transpose(slabs):
    per_block = LANES // S5_GROUP
    granule = lax.broadcasted_iota(jnp.int32, (1, LANES), 1) // S5_GROUP
    x = list(slabs)
    for d in (4, 2, 1):
        keep = (granule & d) == 0
        y = [None] * per_block
        for i in range(per_block):
            if i & d == 0:
                y[i] = jnp.where(keep, x[i], pltpu.roll(x[i + d], d * S5_GROUP, axis=1))
                y[i + d] = jnp.where(keep, pltpu.roll(x[i], LANES - d * S5_GROUP, axis=1), x[i + d])
        x = y
    return x


def _s5_rows_kernel(u_ref, o_ref, *, chunks):
    per_block = LANES // S5_GROUP
    for gcol in range(D_S5 // LANES):
        for j in range(S5_CONV // LANES):
            by_time = [u_ref[gcol, pl.ds(per_block * j + sl, chunks, stride=CHUNK), :] for sl in range(per_block)]
            for gl, rows in enumerate(_granule_transpose(by_time)):
                o_ref[gcol * per_block + gl, :, _lane_block(j)] = rows.astype(BF16)


def s5_rows(u, chunks):
    nblk, t, _ = u.shape
    r = t // CHUNK
    return pl.pallas_call(
        functools.partial(_s5_rows_kernel, chunks=chunks),
        grid=(r // chunks,),
        in_specs=[pl.BlockSpec((nblk, chunks * CHUNK, LANES), lambda i: (0, i, 0))],
        out_specs=pl.BlockSpec((S5_GROUPS, chunks, S5_CONV), lambda i: (0, i, 0)),
        out_shape=jax.ShapeDtypeStruct((S5_GROUPS, r, S5_CONV), BF16),
        compiler_params=_cparams(),
        name="s5_rows",
    )(u)


def _s5_tokens_kernel(y_ref, o_ref, *, chunks):
    per_block = LANES // S5_GROUP
    for gcol in range(D_S5 // LANES):
        for j in range(S5_CONV // LANES):
            by_group = [y_ref[gcol * per_block + gl, :, _lane_block(j)].astype(F32) for gl in range(per_block)]
            for sl, rows in enumerate(_granule_transpose(by_group)):
                o_ref[gcol, pl.ds(per_block * j + sl, chunks, stride=CHUNK), :] = rows


def s5_tokens(y_rows, chunks):
    _, r, _ = y_rows.shape
    return pl.pallas_call(
        functools.partial(_s5_tokens_kernel, chunks=chunks),
        grid=(r // chunks,),
        in_specs=[pl.BlockSpec((S5_GROUPS, chunks, S5_CONV), lambda i: (0, i, 0))],
        out_specs=pl.BlockSpec((D_S5 // LANES, chunks * CHUNK, LANES), lambda i: (0, i, 0)),
        out_shape=jax.ShapeDtypeStruct((D_S5 // LANES, r * CHUNK, LANES), F32),
        compiler_params=_cparams(),
        name="s5_tokens",
    )(y_rows)


def _s5_prep_kernel(lrc_ref, lic_ref, lrr_ref, lir_ref, ls_ref, brt_ref, bit_ref,
                    brtt_ref, bitt_ref, crt_ref, cit_ref,
                    m_ref, w_ref, wc_ref, a_ref):
    step = jnp.exp(ls_ref[0])

    def discretise(lr_raw, li):
        lr = jnp.minimum(lr_raw, -1e-4)
        dr, di = lr * step, li * step
        mag = jnp.exp(dr)
        a_re, a_im = mag * jnp.cos(di), mag * jnp.sin(di)
        den = lr * lr + li * li
        nr = a_re - 1.0
        fr = (nr * lr + a_im * li) / den
        fi = (a_im * lr - nr * li) / den
        return dr, di, fr, fi

    dr_c, di_c, _, _ = discretise(lrc_ref[0], lic_ref[0])
    dr_r, di_r, fr_r, fi_r = discretise(lrr_ref[0], lir_ref[0])

    lane = lax.broadcasted_iota(jnp.int32, (1, S5_CONV), 1)
    t_row = lax.broadcasted_iota(jnp.int32, (1, CHUNK), 1).astype(F32)
    t_col = lax.broadcasted_iota(jnp.int32, (CHUNK, 1), 0).astype(F32)
    lag_of_lane = (lax.broadcasted_iota(jnp.int32, (CHUNK, S5_CONV), 1) // S5_GROUP
                   == lax.broadcasted_iota(jnp.int32, (CHUNK, S5_CONV), 0)).astype(F32)
    time_of_row = (lax.broadcasted_iota(jnp.int32, (S5_CONV, CHUNK), 0) // S5_GROUP
                   == lax.broadcasted_iota(jnp.int32, (S5_CONV, CHUNK), 1)).astype(F32)

    def c_times_power(tf):
        mag = jnp.exp(dr_c * tf)
        ang = di_c * tf
        pr = _dot(mag * jnp.cos(ang), lag_of_lane, HIGHEST)
        pi = _dot(mag * jnp.sin(ang), lag_of_lane, HIGHEST)
        ctr, cti = crt_ref[0], cit_ref[0]
        return ctr * pr - cti * pi, ctr * pi + cti * pr

    cpr, cpi = c_times_power(t_row)
    bbr = fr_r * brt_ref[0] - fi_r * bit_ref[0]
    bbi = fr_r * bit_ref[0] + fi_r * brt_ref[0]
    kt = _dot(bbr, cpr, HIGHEST) - _dot(bbi, cpi, HIGHEST)
    for s in range(CHUNK):
        shifted = kt if s == 0 else pltpu.roll(kt, S5_GROUP * s, axis=1)
        m_ref[0, S5_GROUP * s:S5_GROUP * (s + 1), :] = jnp.where(
            lane >= S5_GROUP * s, shifted, 0.0).astype(BF16)

    rem = CHUNK - 1.0 - t_col
    magw = jnp.exp(dr_r * rem)
    angw = di_r * rem
    pwr = _dot(time_of_row, magw * jnp.cos(angw), HIGHEST)
    pwi = _dot(time_of_row, magw * jnp.sin(angw), HIGHEST)
    bbtr = fr_r * brtt_ref[0] - fi_r * bitt_ref[0]
    bbti = fr_r * bitt_ref[0] + fi_r * brtt_ref[0]
    w_ref[0, :, :S5_STATE] = pwr * bbtr - pwi * bbti
    w_ref[0, :, S5_STATE:] = pwr * bbti + pwi * bbtr

    c1r, c1i = c_times_power(t_row + 1.0)
    wc_ref[0, :S5_STATE, :] = c1r.astype(BF16)
    wc_ref[0, S5_STATE:, :] = (-c1i).astype(BF16)

    full = float(CHUNK)
    mag_c = jnp.exp(dr_r * full)
    a_ref[0, 0:1, :] = mag_c * jnp.cos(di_r * full)
    a_ref[0, 1:2, :] = mag_c * jnp.sin(di_r * full)


def s5_prep(lam_re, lam_im, log_step, b_re, b_im, c_re, c_im):
    g, p = lam_re.shape
    brt = jnp.swapaxes(b_re, 1, 2)
    bit = jnp.swapaxes(b_im, 1, 2)
    crt = jnp.tile(jnp.swapaxes(c_re, 1, 2), (1, 1, CHUNK))
    cit = jnp.tile(jnp.swapaxes(c_im, 1, 2), (1, 1, CHUNK))
    args = (lam_re.reshape(g, p, 1), lam_im.reshape(g, p, 1),
            lam_re.reshape(g, 1, p), lam_im.reshape(g, 1, p), log_step.reshape(g, 1, 1),
            brt, bit, jnp.tile(brt, (1, CHUNK, 1)), jnp.tile(bit, (1, CHUNK, 1)), crt, cit)
    spec = lambda a: pl.BlockSpec((1,) + a.shape[1:], lambda i: (i, 0, 0))
    out_shape = [jax.ShapeDtypeStruct((g, S5_CONV, S5_CONV), BF16),
                 jax.ShapeDtypeStruct((g, S5_CONV, 2 * S5_STATE), F32),
                 jax.ShapeDtypeStruct((g, 2 * S5_STATE, S5_CONV), BF16),
                 jax.ShapeDtypeStruct((g, 2, S5_STATE), F32)]
    return pl.pallas_call(
        _s5_prep_kernel,
        grid=(g,),
        in_specs=[spec(a) for a in args],
        out_specs=[spec(o) for o in out_shape],
        out_shape=out_shape,
        compiler_params=_cparams(),
        name="s5_prep",
    )(*args)


def _split3(w):
    hi = w.astype(BF16)
    r1 = w - hi.astype(F32)
    mid = r1.astype(BF16)
    lo = (r1 - mid.astype(F32)).astype(BF16)
    return hi, mid, lo


def _s5_main_kernel(up_ref, us_ref, m_ref, w_ref, wc_ref, a_ref, xsr_ref, xsi_ref,
                    yp_ref, ys_ref, fpr_ref, fpi_ref, fsr_ref, fsi_ref, fin_scr, *, n_prompt, n_chunks):
    w3 = jnp.concatenate(_split3(w_ref[0]), axis=1)
    ar = a_ref[0, 0:1, :]
    ai = a_ref[0, 1:2, :]
    width = 2 * S5_STATE

    def local(u):
        h3 = _dot(u, w3)
        return _dot(u, m_ref[0]), h3[:, :width] + h3[:, width:2 * width] + h3[:, 2 * width:]

    def times(pr, pi, x):
        return (jnp.concatenate([pr, pr], axis=1) * x
                + jnp.concatenate([-pi, pi], axis=1) * pltpu.roll(x, S5_STATE, axis=1))

    y_local, x = local(up_ref[0])
    chunk_of_row = lax.broadcasted_iota(jnp.int32, (n_prompt * n_chunks, 1), 0) % n_chunks
    pr, pi = ar, ai
    d = 1
    while d < n_chunks:
        x = x + times(pr, pi, jnp.where(chunk_of_row >= d, pltpu.roll(x, d, axis=0), 0.0))
        pr, pi = pr * pr - pi * pi, 2.0 * pr * pi
        d *= 2
    fin_scr[...] = x
    last = fin_scr[pl.ds(n_chunks - 1, n_prompt, stride=n_chunks), :]
    fpr_ref[0] = last[:, :S5_STATE]
    fpi_ref[0] = last[:, S5_STATE:]
    x0 = jnp.where(chunk_of_row >= 1, pltpu.roll(x, 1, axis=0), 0.0)
    yp_ref[0] = (y_local + _dot(x0.astype(BF16), wc_ref[0])).astype(BF16)

    y_local, hend = local(us_ref[0])
    x0 = jnp.concatenate([xsr_ref[0], xsi_ref[0]], axis=1)
    fin = times(ar, ai, x0) + hend
    fsr_ref[0] = fin[:, :S5_STATE]
    fsi_ref[0] = fin[:, S5_STATE:]
    ys_ref[0] = (y_local + _dot(x0.astype(BF16), wc_ref[0])).astype(BF16)


def s5_main(up_rows, us_rows, m, w, wc, a, xs_re, xs_im, n_prompt, n_chunks):
    g, r, _ = up_rows.shape
    n_sample = xs_re.shape[1]
    spec = lambda shape: pl.BlockSpec((1,) + tuple(shape[1:]), lambda i: (i, 0, 0))
    args = (up_rows, us_rows, m, w, wc, a, xs_re, xs_im)
    out_shape = [jax.ShapeDtypeStruct((g, r, S5_CONV), BF16),
                 jax.ShapeDtypeStruct((g, n_sample, S5_CONV), BF16),
                 jax.ShapeDtypeStruct((g, n_prompt, S5_STATE), F32),
                 jax.ShapeDtypeStruct((g, n_prompt, S5_STATE), F32),
                 jax.ShapeDtypeStruct((g, n_sample, S5_STATE), F32),
                 jax.ShapeDtypeStruct((g, n_sample, S5_STATE), F32)]
    return pl.pallas_call(
        functools.partial(_s5_main_kernel, n_prompt=n_prompt, n_chunks=n_chunks),
        grid=(g,),
        in_specs=[spec(x.shape) for x in args],
        out_specs=[spec(o.shape) for o in out_shape],
        out_shape=out_shape,
        scratch_shapes=[pltpu.VMEM((r, 2 * S5_STATE), F32)],
        compiler_params=_cparams(),
        name="s5_main",
    )(*args)


def _hgrn_kernel(f_ref, z_ref, s0_ref, lb_ref, ng_ref, o_ref, sfin_ref, st_scr, *, n_seq, chunk):
    c = pl.program_id(1)

    @pl.when(c == 0)
    def _():
        st_scr[...] = s0_ref[...]

    lbw = lb_ref[...]
    lbe = jnp.exp(lbw - jnp.max(lbw, axis=0, keepdims=True))
    lb_all = lbe[0:1, :] / jnp.sum(lbe, axis=0, keepdims=True)

    levels = [chunk >> (i + 1) for i in range(chunk.bit_length() - 1)]
    rowi = lax.broadcasted_iota(jnp.int32, (chunk, chunk), 0)
    coli = lax.broadcasted_iota(jnp.int32, (chunk, chunk), 1)
    rowk = lax.broadcasted_iota(jnp.int32, (chunk, HG_D), 0)
    sign, valid = [], []
    for m in levels:
        sign.append(jnp.where((rowk % (2 * m)) >= m, 1.0, -1.0))
        valid.append(((rowi // (2 * m)) == (coli // (2 * m)))
                     & ((rowi % (2 * m)) >= m) & ((coli % (2 * m)) < m))
    cum_mat = (coli <= rowi).astype(BF16)
    cum_mat3 = jnp.concatenate([cum_mat] * 3, axis=1)
    diag = rowi == coli
    nt = (((1,), (1,)), ((), ()))

    def body(n, carry):
        zf = f_ref[n]
        fg_all = lb_all + (1.0 - lb_all) * jax.nn.sigmoid(zf)
        cums = _dot(cum_mat3, jnp.concatenate(_split3(jnp.log2(fg_all)), axis=0))
        for hd in range(HG_HEADS):
            cols = slice(hd * HG_D, (hd + 1) * HG_D)
            zq = z_ref[n, :, hd * HG_D:(hd + 1) * HG_D].astype(F32)
            vb = z_ref[n, :, D_HG + hd * HG_D:D_HG + (hd + 1) * HG_D]
            zg = z_ref[n, :, 2 * D_HG + hd * HG_D:2 * D_HG + (hd + 1) * HG_D].astype(F32)
            q = zq * jax.nn.sigmoid(zq)
            kk = 1.0 - fg_all[:, cols]
            bcum = cums[:chunk, cols]
            b_last = bcum[chunk - 1:chunk, :]
            qb = q.astype(BF16)
            kb = kk.astype(BF16)
            st = st_scr[n, hd]

            scores = jnp.where(diag, lax.dot_general(qb, kb, nt, preferred_element_type=F32), 0.0)
            for lvl, m in enumerate(levels):
                if 2 * m >= SUBLANES:
                    bref = jnp.concatenate(
                        [jnp.broadcast_to(bcum[b * 2 * m + m - 1:b * 2 * m + m, :], (2 * m, HG_D))
                         for b in range(chunk // (2 * m))], axis=0)
                else:
                    offs = rowk % (2 * m) - (m - 1)
                    bref = bcum
                    for o in range(-(m - 1), m + 1):
                        if o != 0:
                            bref = jnp.where(offs == o, pltpu.roll(bcum, o % chunk, axis=0), bref)
                dec = jnp.exp2((bcum - bref) * sign[lvl]).astype(BF16)
                sc = lax.dot_general(qb * dec, kb * dec, nt, preferred_element_type=F32)
                scores = jnp.where(valid[lvl], sc, scores)

            qd = (q * jnp.exp2(bcum)).astype(BF16)
            o = lax.dot_general(qd, st.astype(BF16), nt, preferred_element_type=F32)
            o = o + _dot(scores.astype(BF16), vb)
            kdec = (kk * jnp.exp2(b_last - bcum)).astype(BF16)
            st_scr[n, hd] = jnp.exp2(b_last) * st + lax.dot_general(
                vb, kdec, (((0,), (0,)), ((), ())), preferred_element_type=F32)

            on = o * lax.rsqrt(jnp.mean(o * o, axis=-1, keepdims=True) + RMS_EPS) * ng_ref[:, cols]
            o_ref[n, :, hd * HG_D:(hd + 1) * HG_D] = on * (zg * jax.nn.sigmoid(zg))
        return carry

    lax.fori_loop(0, n_seq, body, 0, unroll=True)

    @pl.when(c == pl.num_programs(1) - 1)
    def _():
        sfin_ref[...] = st_scr[...]


def hgrn(f, z, s0_t, hg_lb, norm_g, n_seq, chunk):
    n, length, _ = z.shape
    return pl.pallas_call(
        functools.partial(_hgrn_kernel, n_seq=n_seq, chunk=chunk),
        grid=(n // n_seq, length // chunk),
        in_specs=[pl.BlockSpec((n_seq, chunk, D_HG), lambda g, c: (g, c, 0)),
                  pl.BlockSpec((n_seq, chunk, 3 * D_HG), lambda g, c: (g, c, 0)),
                  pl.BlockSpec((n_seq, HG_HEADS, HG_D, HG_D), lambda g, c: (g, 0, 0, 0)),
                  pl.BlockSpec(hg_lb.shape, lambda g, c: (0, 0)),
                  pl.BlockSpec((1, D_HG), lambda g, c: (0, 0))],
        out_specs=[pl.BlockSpec((n_seq, chunk, D_HG), lambda g, c: (g, c, 0)),
                   pl.BlockSpec((n_seq, HG_HEADS, HG_D, HG_D), lambda g, c: (g, 0, 0, 0))],
        out_shape=[jax.ShapeDtypeStruct((n, length, D_HG), F32),
                   jax.ShapeDtypeStruct((n, HG_HEADS, HG_D, HG_D), F32)],
        scratch_shapes=[pltpu.VMEM((n_seq, HG_HEADS, HG_D, HG_D), F32)],
        compiler_params=_cparams(2),
        name="hgrn",
    )(f, z, s0_t, hg_lb, norm_g)


def _post_mix_kernel(h_ref, up_ref, us_ref, yp_ref, ys_ref, hgp_ref, hgs_ref,
                     d_ref, wglu_ref, bglu_ref, s5g_ref, wout_ref, g1_ref, b1_ref, rwt_ref, rb_ref,
                     h1_ref, slot_ref, gate_ref, before_ref, cnt_ref, run_scr, *, tm, n_first):
    i = pl.program_id(0)

    @pl.when(i == 0)
    def _():
        run_scr[...] = jnp.zeros_like(run_scr)

    def phase(u_ref, y_ref, hg_ref):
        ys = _read_lane_blocks(y_ref) + d_ref[...] * _read_lane_blocks(u_ref)
        gl = 0.5 * ys * (1.0 + lax.erf(ys * (2.0 ** -0.5)))
        s5o = gl * jax.nn.sigmoid(_dot(gl.astype(BF16), wglu_ref[...]) + bglu_ref[...])
        s5o = s5o * lax.rsqrt(jnp.mean(s5o * s5o, axis=-1, keepdims=True) + RMS_EPS) * s5g_ref[...]
        mix = (_dot(s5o.astype(BF16), wout_ref[:D_S5, :])
               + _dot(hg_ref[...].astype(BF16), wout_ref[D_S5:, :]))
        h1 = _layernorm(DEEPNORM_ALPHA * h_ref[...] + mix, g1_ref[...], b1_ref[...])
        _write_row_tiles(h1_ref, h1, tm)

        h_hi, h_mid, _ = _split3(h1)
        w_hi, w_mid, _ = _split3(rwt_ref[...])
        nt = (((1,), (1,)), ((), ()))
        logits = (lax.dot_general(w_hi, h_hi, nt, preferred_element_type=F32)
                  + lax.dot_general(w_hi, h_mid, nt, preferred_element_type=F32)
                  + lax.dot_general(w_mid, h_hi, nt, preferred_element_type=F32)) + rb_ref[...]
        eid = lax.broadcasted_iota(jnp.int32, (N_EXPERTS, tm), 0)
        vals, idxs = [], []
        for _ in range(TOP_K):
            m = jnp.max(logits, axis=0, keepdims=True)
            ix = jnp.min(jnp.where(logits == m, eid, N_EXPERTS), axis=0, keepdims=True)
            vals.append(m)
            idxs.append(ix)
            logits = jnp.where(eid == ix, -jnp.inf, logits)
        exps = [jnp.exp(v - vals[0]) for v in vals]
        den = exps[0] + exps[1] + exps[2] + exps[3]

        onehot = jnp.zeros((N_EXPERTS, tm), F32)
        for ix in idxs:
            onehot = onehot + (eid == ix).astype(F32)
        rowi = lax.broadcasted_iota(jnp.int32, (tm, tm), 0)
        coli = lax.broadcasted_iota(jnp.int32, (tm, tm), 1)
        earlier = (rowi < coli).astype(BF16)
        prefix = _dot(onehot.astype(BF16), earlier)
        tile_cnt = jnp.sum(onehot, axis=1, keepdims=True)
        for k in range(TOP_K):
            lower_experts = jnp.sum(jnp.where(eid < idxs[k], tile_cnt, 0.0), axis=0, keepdims=True)
            rank = jnp.sum(jnp.where(eid == idxs[k], prefix, 0.0), axis=0, keepdims=True)
            slot_ref[0, :, k * tm:(k + 1) * tm] = (lower_experts + rank).astype(jnp.int32)
            gate_ref[0, :, k * tm:(k + 1) * tm] = exps[k] / den
        before_ref[0] = run_scr[...]
        run_scr[...] = run_scr[...] + tile_cnt
        cnt_ref[...] = run_scr[...]

    pl.when(i < n_first)(lambda: phase(up_ref, yp_ref, hgp_ref))
    pl.when(i >= n_first)(lambda: phase(us_ref, ys_ref, hgs_ref))


def post_mix(h0, u_pair, y_pair, hg_pair, d_skip, wglu, bglu, s5g, wout, g1, b1, rw_t, rb_col, tm):
    t = h0.shape[0]
    n_first = u_pair[0].shape[1] // tm
    row = lambda i: (i, 0)
    fixed = lambda i: (0, 0)
    full = lambda a: pl.BlockSpec(a.shape, fixed)
    weights = (d_skip, wglu, bglu, s5g, wout, g1, b1, rw_t, rb_col)
    return pl.pallas_call(
        functools.partial(_post_mix_kernel, tm=tm, n_first=n_first),
        grid=(t // tm,),
        in_specs=[pl.BlockSpec((tm, D_MODEL), row)]
                 + _two_phase_lane_block_specs(tm, n_first) * 2 + _two_phase_specs((tm, D_HG), n_first)
                 + [full(a) for a in weights],
        out_specs=[pl.BlockSpec((tm * ROW_TILES, LANES), row),
                   pl.BlockSpec((1, 1, TOP_K * tm), lambda i: (i, 0, 0)),
                   pl.BlockSpec((1, 1, TOP_K * tm), lambda i: (i, 0, 0)),
                   pl.BlockSpec((1, N_EXPERTS, 1), lambda i: (i, 0, 0)),
                   pl.BlockSpec((N_EXPERTS, 1), fixed)],
        out_shape=[jax.ShapeDtypeStruct((t * ROW_TILES, LANES), F32),
                   jax.ShapeDtypeStruct((t // tm, 1, TOP_K * tm), jnp.int32),
                   jax.ShapeDtypeStruct((t // tm, 1, TOP_K * tm), F32),
                   jax.ShapeDtypeStruct((t // tm, N_EXPERTS, 1), F32),
                   jax.ShapeDtypeStruct((N_EXPERTS, 1), F32)],
        scratch_shapes=[pltpu.VMEM((N_EXPERTS, 1), F32)],
        compiler_params=_cparams(),
        name="post_mix",
    )(h0, *u_pair, *y_pair, *hg_pair, *weights)


def _segment_copies(meta_ref, tile, tm, make_copy):
    for e in range(N_EXPERTS):
        sorted_row = meta_ref[tile, e]
        cnt = meta_ref[tile, N_EXPERTS + e]
        staged_row = meta_ref[tile, 2 * N_EXPERTS + e]
        for b in range(tm.bit_length()):
            done = cnt & ((1 << b) - 1)

            @pl.when(((cnt >> b) & 1) == 1)
            def _(b=b, done=done, e=e):
                make_copy(staged_row + done, sorted_row + done, 1 << b).start(priority=e % 2)


def _dispatch_kernel(meta_ref, pend_ref, slot_ref, h_ref, xs_ref, stage, zero_scr, zsem, sem, *, tm):
    n_rows = xs_ref.shape[0] // ROW_TILES
    i = pl.program_id(0)
    n = pl.num_programs(0)
    slot = i % 2

    def drain(s):
        pltpu.make_async_copy(stage.at[s], xs_ref.at[pl.ds(0, TOP_K * tm * ROW_TILES)], sem.at[s]).wait()

    @pl.when(i == 0)
    def _():
        zero_scr[...] = jnp.zeros_like(zero_scr)

        def last_block(e):
            prev = pend_ref[e - 1] if e > 0 else 0
            copy = pltpu.make_async_copy(
                zero_scr, xs_ref.at[pl.ds(pl.multiple_of(jnp.maximum(pend_ref[e] - MOE_ROWS, 0) * ROW_TILES,
                                                         ROW_TILES), MOE_ROWS * ROW_TILES)], zsem)
            return pend_ref[e] > prev, copy

        def tail_block(j):
            row0 = pend_ref[N_EXPERTS - 1] + j * MOE_ROWS
            copy = pltpu.make_async_copy(
                zero_scr, xs_ref.at[pl.ds(pl.multiple_of(jnp.minimum(row0, n_rows - MOE_ROWS) * ROW_TILES,
                                                         ROW_TILES), MOE_ROWS * ROW_TILES)], zsem)
            return row0 < n_rows, copy

        blocks = [last_block(e) for e in range(N_EXPERTS)] + [tail_block(j) for j in range(N_EXPERTS)]
        for used, copy in blocks:
            pl.when(used)(copy.start)
        for used, copy in blocks:
            pl.when(used)(copy.wait)

    pl.when(i >= 2)(lambda: drain(slot))

    def body(t, carry):
        row = h_ref[_row_tile(t), :]
        for k in range(TOP_K):
            stage[slot, _row_tile(slot_ref[k * tm + t]), :] = row
        return carry

    lax.fori_loop(0, tm, body, 0, unroll=8)
    _segment_copies(meta_ref, i, tm, lambda staged_row, sorted_row, rows: pltpu.make_async_copy(
        stage.at[slot, _row_tiles(staged_row, rows)], xs_ref.at[_row_tiles(sorted_row, rows)], sem.at[slot]))

    @pl.when(i == n - 1)
    def _():
        pl.when(n >= 2)(lambda: drain(1 - slot))
        drain(slot)


def dispatch(meta, pend, slots_flat, h1, n_rows, tm):
    t = h1.shape[0] // ROW_TILES
    grid_spec = pltpu.PrefetchScalarGridSpec(
        num_scalar_prefetch=2,
        grid=(t // tm,),
        in_specs=[pl.BlockSpec((TOP_K * tm,), lambda i, meta, pend: (i,), memory_space=pltpu.SMEM),
                  pl.BlockSpec((tm * ROW_TILES, LANES), lambda i, meta, pend: (i, 0))],
        out_specs=pl.BlockSpec(memory_space=pl.ANY),
        scratch_shapes=[pltpu.VMEM((2, TOP_K * tm * ROW_TILES, LANES), F32),
                        pltpu.VMEM((MOE_ROWS * ROW_TILES, LANES), F32),
                        pltpu.SemaphoreType.DMA(()), pltpu.SemaphoreType.DMA((2,))],
    )
    return pl.pallas_call(
        functools.partial(_dispatch_kernel, tm=tm),
        grid_spec=grid_spec,
        out_shape=jax.ShapeDtypeStruct((n_rows * ROW_TILES, LANES), F32),
        compiler_params=_cparams(),
        name="moe_dispatch",
    )(meta, pend, slots_flat, h1)


def _moe_ffn_kernel(be_ref, nu_ref, seg_ref, nxt_ref, x_ref, wg_ref, bg_ref, wu_ref, bu_ref, wd_ref, bd_ref,
                    y_ref, wbuf, wbf, sem):
    i = pl.program_id(0)
    hbm = (wg_ref, wu_ref, wd_ref)

    def weight_copies(expert, s):
        return [pltpu.make_async_copy(hbm[j].at[expert], wbuf.at[s, j], sem.at[s, j]) for j in range(3)]

    @pl.when((i == 0) | (be_ref[i] != be_ref[jnp.maximum(i - 1, 0)]))
    def _():
        s = seg_ref[i] % 2

        @pl.when(i == 0)
        def _():
            for c in weight_copies(be_ref[0], 0):
                c.start()

        for j, c in enumerate(weight_copies(be_ref[i], s)):
            c.wait()
            wbf[j] = wbuf[s, j].astype(BF16)

        @pl.when(nxt_ref[i] >= 0)
        def _():
            for c in weight_copies(nxt_ref[i], 1 - s):
                c.start()

    @pl.when(i < nu_ref[0])
    def _():
        x = _read_row_tiles(x_ref, MOE_ROWS).astype(BF16)
        gt = jnp.minimum(_dot(x, wbf[0]) + bg_ref[0], SWIGLU_LIMIT)
        up = jnp.clip(_dot(x, wbf[1]) + bu_ref[0], -SWIGLU_LIMIT, SWIGLU_LIMIT)
        hid = (up + 1.0) * (gt * jax.nn.sigmoid(SWIGLU_ALPHA * gt))
        _write_row_tiles(y_ref, _dot(hid.astype(BF16), wbf[2]) + bd_ref[0], MOE_ROWS)

    @pl.when(i >= nu_ref[0])
    def _():
        y_ref[...] = jnp.zeros_like(y_ref)


def moe_ffn(block_e, n_used, segment, next_e, xs, wg, bg, wu, bu, wd, bd):
    n_rows = xs.shape[0] // ROW_TILES
    n_blocks = n_rows // MOE_ROWS
    wsel = lambda i, be, nu, seg, nxt: (be[i], 0, 0)
    d_ff = wg.shape[-1]
    assert wg.shape[1:] == wu.shape[1:] == wd.shape[1:] == (D_MODEL, D_MODEL)
    anywhere = pl.BlockSpec(memory_space=pl.ANY)
    grid_spec = pltpu.PrefetchScalarGridSpec(
        num_scalar_prefetch=4,
        grid=(n_blocks,),
        in_specs=[pl.BlockSpec((MOE_ROWS * ROW_TILES, LANES),
                               lambda i, be, nu, seg, nxt: (jnp.minimum(i, nu[0] - 1), 0)),
                  anywhere, pl.BlockSpec((1, 1, d_ff), wsel),
                  anywhere, pl.BlockSpec((1, 1, d_ff), wsel),
                  anywhere, pl.BlockSpec((1, 1, D_MODEL), wsel)],
        out_specs=pl.BlockSpec((MOE_ROWS * ROW_TILES, LANES), lambda i, be, nu, seg, nxt: (i, 0)),
        scratch_shapes=[pltpu.VMEM((2, 3, D_MODEL, D_MODEL), F32), pltpu.VMEM((3, D_MODEL, D_MODEL), BF16),
                        pltpu.SemaphoreType.DMA((2, 3))],
    )
    return pl.pallas_call(
        _moe_ffn_kernel,
        grid_spec=grid_spec,
        out_shape=jax.ShapeDtypeStruct((n_rows * ROW_TILES, LANES), F32),
        compiler_params=_cparams(),
        name="moe_ffn",
    )(block_e, n_used, segment, next_e, xs, wg, bg, wu, bu, wd, bd)


def _combine_kernel(meta_ref, slot_ref, gate_ref, h_ref, pp_ref, ps_ref, yb_ref,
                    plew_ref, plegw_ref, g2_ref, b2_ref, outp_ref, outs_ref, buf, r_scr, sem,
                    *, tm, n_first):
    i = pl.program_id(0)
    n = pl.num_programs(0)
    slot = i % 2

    def fetch(tile, s):
        _segment_copies(meta_ref, tile, tm, lambda staged_row, sorted_row, rows: pltpu.make_async_copy(
            yb_ref.at[_row_tiles(sorted_row, rows)], buf.at[s, _row_tiles(staged_row, rows)], sem.at[s]))

    pl.when(i == 0)(lambda: fetch(0, 0))
    pl.when(i + 1 < n)(lambda: fetch(i + 1, 1 - slot))
    pltpu.make_async_copy(yb_ref.at[pl.ds(0, TOP_K * tm * ROW_TILES)], buf.at[slot], sem.at[slot]).wait()

    def body(t, carry):
        acc = DEEPNORM_ALPHA * h_ref[_row_tile(t), :]
        for k in range(TOP_K):
            acc = acc + gate_ref[k * tm + t] * buf[slot, _row_tile(slot_ref[k * tm + t]), :]
        r_scr[_row_tile(t), :] = acc
        return carry

    lax.fori_loop(0, tm, body, 0, unroll=8)
    r = _read_row_tiles(r_scr, tm)
    gate = jax.nn.sigmoid(_dot(r.astype(BF16), plegw_ref[...]))

    def finish(p, store):
        e = _dot(p.astype(BF16), plew_ref[...]) * gate
        store(_layernorm(r + e, g2_ref[...], b2_ref[...]))

    def store_prompt(v):
        outp_ref[0] = v

    def store_sample(v):
        outs_ref[...] = v

    pl.when(i < n_first)(lambda: finish(pp_ref[0], store_prompt))
    pl.when(i >= n_first)(lambda: finish(ps_ref[...], store_sample))


def combine(meta, slots_flat, gates_flat, h1, p_prompt, p_sample, yb, plew, plegw, g2, b2, tm):
    t = h1.shape[0] // ROW_TILES
    nb, seq, _ = p_prompt.shape
    ts = p_sample.shape[0]
    n_first = nb * seq // tm
    fixed = lambda i, meta: (0, 0)
    flat = pl.BlockSpec((TOP_K * tm,), lambda i, meta: (i,), memory_space=pltpu.SMEM)
    sample = lambda width: pl.BlockSpec((tm, width), lambda i, meta: (jnp.maximum(i - n_first, 0), 0))
    prompt = lambda width: pl.BlockSpec((1, tm, width), _prompt_spec(tm, seq, width, n_first).index_map)
    with_meta = lambda spec: pl.BlockSpec(spec.block_shape, lambda i, meta: spec.index_map(i))
    grid_spec = pltpu.PrefetchScalarGridSpec(
        num_scalar_prefetch=1,
        grid=(t // tm,),
        in_specs=[flat, flat,
                  pl.BlockSpec((tm * ROW_TILES, LANES), lambda i, meta: (i, 0)),
                  with_meta(prompt(PLE_DIM)), sample(PLE_DIM),
                  pl.BlockSpec(memory_space=pl.ANY),
                  pl.BlockSpec(plew.shape, fixed), pl.BlockSpec(plegw.shape, fixed),
                  pl.BlockSpec((1, D_MODEL), fixed), pl.BlockSpec((1, D_MODEL), fixed)],
        out_specs=[with_meta(prompt(D_MODEL)), sample(D_MODEL)],
        scratch_shapes=[pltpu.VMEM((2, TOP_K * tm * ROW_TILES, LANES), F32),
                        pltpu.VMEM((tm * ROW_TILES, LANES), F32),
                        pltpu.SemaphoreType.DMA((2,))],
    )
    return pl.pallas_call(
        functools.partial(_combine_kernel, tm=tm, n_first=n_first),
        grid_spec=grid_spec,
        out_shape=[jax.ShapeDtypeStruct((nb, seq, D_MODEL), F32), jax.ShapeDtypeStruct((ts, D_MODEL), F32)],
        compiler_params=_cparams(),
        name="moe_combine",
    )(meta, slots_flat, gates_flat, h1, p_prompt, p_sample, yb, plew, plegw, g2, b2)


def _row(v):
    return v.reshape(1, -1)


def kernel(x_prompt, x_sample, state_s5_re, state_s5_im, state_hgrn, p_prompt, p_sample, ln_in_g, ln_in_b, w_in, s5_lambda_re, s5_lambda_im, s5_log_step, s5_b_re, s5_b_im, s5_c_re, s5_c_im, s5_d, s5_w_glu, s5_b_glu, s5_norm_g, hg_lb, hg_norm_g, w_out, ln1_g, ln1_b, router_w, router_b, w_gate, b_gate, w_up, b_up, w_down, b_down, ple_w, ple_gate_w, ln2_g, ln2_b):
    nb, seq, _ = x_prompt.shape
    ns, dseq, _ = x_sample.shape
    assert dseq == CHUNK and seq % CHUNK == 0 and w_in.shape[0] == 1
    nc = seq // CHUNK
    tp, ts = nb * seq, ns * dseq
    t = tp + ts
    tm = 512 if (tp % 512 == 0 and ts % 512 == 0) else 256
    assert tp % tm == 0 and ts % tm == 0 and seq % tm == 0 and ns % nb == 0

    w_cols = jnp.split(w_in[0], [D_S5, D_S5 + D_HG, D_S5 + 2 * D_HG], axis=1)
    h0, u_p, u_s, f_p, f_s, z_p, z_s = ln_in_proj(
        x_prompt, x_sample.reshape(ts, D_MODEL), _row(ln_in_g), _row(ln_in_b),
        jnp.concatenate([w_cols[0], w_cols[2], w_cols[1], w_cols[3]], axis=1).astype(BF16), tm)

    m, w, wc, a = s5_prep(s5_lambda_re[0], s5_lambda_im[0], s5_log_step[0],
                          s5_b_re[0], s5_b_im[0], s5_c_re[0], s5_c_im[0])
    rb = 64 if (nb * nc) % 64 == 0 else nc
    yp_rows, ys_rows, fpr, fpi, fsr, fsi = s5_main(s5_rows(u_p, rb), s5_rows(u_s, ns), m, w, wc, a,
                                                   jnp.swapaxes(state_s5_re[0], 0, 1),
                                                   jnp.swapaxes(state_s5_im[0], 0, 1), nb, nc)
    y_pair = (s5_tokens(yp_rows, rb), s5_tokens(ys_rows, ns))

    zero_state = jnp.zeros((nb, HG_HEADS, HG_D, HG_D), F32)
    ng = _row(hg_norm_g[0])
    o_p, st_p = hgrn(f_p.reshape(nb, seq, D_HG), z_p.reshape(nb, seq, 3 * D_HG), zero_state, hg_lb, ng, nb,
                     HG_CHUNK if seq % HG_CHUNK == 0 else CHUNK)
    o_s, st_s = hgrn(f_s.reshape(ns, dseq, D_HG), z_s.reshape(ns, dseq, 3 * D_HG), jnp.swapaxes(state_hgrn[0], 2, 3),
                     hg_lb, ng, nb, dseq)

    h1, slots, gates, before, counts = post_mix(
        h0, (u_p, u_s), y_pair, (o_p.reshape(tp, D_HG), o_s.reshape(ts, D_HG)),
        _row(s5_d[0]), s5_w_glu[0].astype(BF16), _row(s5_b_glu[0]),
        _row(s5_norm_g[0]), w_out[0].astype(BF16), _row(ln1_g[0]), _row(ln1_b[0]),
        router_w[0].T, router_b[0].reshape(N_EXPERTS, 1), tm)

    counts = counts[:, 0].astype(jnp.int32)
    before = before[:, :, 0].astype(jnp.int32)
    cnt = jnp.concatenate([before[1:], counts[None]], axis=0) - before
    padded = (counts + MOE_ROWS - 1) // MOE_ROWS * MOE_ROWS
    pend = jnp.cumsum(padded)
    staged = jnp.cumsum(cnt, axis=1) - cnt
    meta = jnp.concatenate([pend - padded + before, cnt, staged, jnp.zeros_like(cnt)], axis=1)
    experts = jnp.arange(N_EXPERTS, dtype=jnp.int32)

    n_blocks = -(-t * TOP_K // MOE_ROWS) + N_EXPERTS
    n_used = (pend[-1] // MOE_ROWS).astype(jnp.int32)
    blk = jnp.arange(n_blocks, dtype=jnp.int32)
    blk = jnp.minimum(blk, n_used - 1)
    block_e = jnp.sum((pend[None, :] <= (blk * MOE_ROWS)[:, None]).astype(jnp.int32), axis=1)
    block_e = jnp.minimum(block_e, N_EXPERTS - 1)
    owns_rows = padded > 0
    segment = (jnp.cumsum(owns_rows.astype(jnp.int32)) - 1)[block_e]
    later = owns_rows[None, :] & (experts[None, :] > experts[:, None])
    next_owner = jnp.where(jnp.any(later, axis=1), jnp.argmax(later, axis=1), -1).astype(jnp.int32)
    next_e = next_owner[block_e]

    slots_flat = slots.reshape(-1)
    xs = dispatch(meta, pend, slots_flat, h1, n_blocks * MOE_ROWS, tm)
    yb = moe_ffn(block_e, n_used.reshape(1), segment, next_e, xs,
                 w_gate[0], b_gate[0][:, None, :], w_up[0], b_up[0][:, None, :],
                 w_down[0], b_down[0][:, None, :])
    out_p, out_s = combine(meta, slots_flat, gates.reshape(-1), h1, p_prompt[0], p_sample[0].reshape(ts, PLE_DIM),
                           yb, ple_w[0].astype(BF16), ple_gate_w[0].astype(BF16),
                           _row(ln2_g[0]), _row(ln2_b[0]), tm)

    def s5_state(f, n):
        return jnp.swapaxes(f, 0, 1).reshape(1, n, S5_GROUPS, S5_STATE)

    return (out_p, out_s.reshape(ns, dseq, D_MODEL),
            s5_state(fpr, nb), s5_state(fpi, nb), jnp.swapaxes(st_p, 2, 3)[None],
            s5_state(fsr, ns), s5_state(fsi, ns), jnp.swapaxes(st_s, 2, 3)[None])
```

```python
import functools

import jax
import jax.numpy as jnp
from jax import lax
from jax.experimental import pallas as pl
from jax.experimental.pallas import tpu as pltpu

F32 = jnp.float32
BF16 = jnp.bfloat16
HIGHEST = lax.Precision.HIGHEST

D_MODEL = 1024
CHUNK = 64
PLE_DIM = 256
D_S5 = 512
S5_GROUP = 16
S5_GROUPS = 32
S5_STATE = 64
D_HG = 512
HG_HEADS = 4
HG_D = 128
D_IN = D_S5 + 4 * D_HG
N_EXPERTS = 32
TOP_K = 4
SWIGLU_LIMIT = 7.0
SWIGLU_ALPHA = 1.702
DEEPNORM_ALPHA = 2.0 ** 0.25
LN_EPS = 1e-5
RMS_EPS = 1e-6

LANES = 128
SUBLANES = 8
ROW_TILES = D_MODEL // LANES
S5_CONV = CHUNK * S5_GROUP
HG_CHUNK = 128
MOE_ROWS = 512
VMEM_LIMIT = 56 * 1024 * 1024

assert ROW_TILES == SUBLANES


def _cparams(n_axes=1, flags=None):
    return pltpu.CompilerParams(dimension_semantics=("arbitrary",) * n_axes,
                                vmem_limit_bytes=VMEM_LIMIT, flags=flags)


def _dot(a, b, precision=None):
    return jnp.dot(a, b, preferred_element_type=F32, precision=precision)


def _layernorm(x, g, b):
    mu = jnp.mean(x, axis=-1, keepdims=True)
    xc = x - mu
    var = jnp.mean(xc * xc, axis=-1, keepdims=True)
    return xc * lax.rsqrt(var + LN_EPS) * g + b


def _two_phase_specs(block, n_first):
    nd = len(block)
    first = pl.BlockSpec(block, lambda i: (jnp.minimum(i, n_first - 1),) + (0,) * (nd - 1))
    second = pl.BlockSpec(block, lambda i: (jnp.maximum(i - n_first, 0),) + (0,) * (nd - 1))
    return [first, second]


def _two_phase_lane_block_specs(tm, n_first):
    nblk = D_S5 // LANES
    first = pl.BlockSpec((nblk, tm, LANES), lambda i: (0, jnp.minimum(i, n_first - 1), 0))
    second = pl.BlockSpec((nblk, tm, LANES), lambda i: (0, jnp.maximum(i - n_first, 0), 0))
    return [first, second]


def _read_lane_blocks(ref):
    return jnp.concatenate([ref[j] for j in range(D_S5 // LANES)], axis=1)


def _prompt_spec(tm, seq, width, n_first):
    per_seq = seq // tm

    def index(i):
        ic = jnp.minimum(i, n_first - 1)
        return (ic // per_seq, ic % per_seq, 0)

    return pl.BlockSpec((1, tm, width), index)


def _chunk(rows, j):
    return pl.ds(j, rows, stride=ROW_TILES)


def _read_row_tiles(ref, rows):
    return jnp.concatenate([ref[_chunk(rows, j), :] for j in range(ROW_TILES)], axis=1)


def _write_row_tiles(ref, val, rows):
    for j in range(ROW_TILES):
        ref[_chunk(rows, j), :] = val[:, j * LANES:(j + 1) * LANES]


def _row_tiles(r, n=1):
    return pl.ds(pl.multiple_of(r * ROW_TILES, ROW_TILES), n * ROW_TILES)


def _row_tile(r):
    return _row_tiles(r)


def _ln_in_proj_kernel(xp_ref, xs_ref, g_ref, b_ref, w_ref, h_ref, up_ref, us_ref, fp_ref, fs_ref, zp_ref, zs_ref,
                       *, n_first):
    def phase(x, u_ref, f_ref, z_ref):
        h = _layernorm(x, g_ref[...], b_ref[...])
        h_ref[...] = h
        hb = h.astype(BF16)
        u = _dot(hb, w_ref[:, :D_S5])
        for j in range(D_S5 // LANES):
            u_ref[j] = u[:, j * LANES:(j + 1) * LANES]
        f_ref[...] = _dot(hb, w_ref[:, D_S5:D_S5 + D_HG])
        z_ref[...] = _dot(hb, w_ref[:, D_S5 + D_HG:]).astype(BF16)

    i = pl.program_id(0)
    pl.when(i < n_first)(lambda: phase(xp_ref[0], up_ref, fp_ref, zp_ref))
    pl.when(i >= n_first)(lambda: phase(xs_ref[...], us_ref, fs_ref, zs_ref))


def ln_in_proj(xp, xs, g, b, w_bf16, tm):
    nb, seq, _ = xp.shape
    tp, ts = nb * seq, xs.shape[0]
    n_first = tp // tm
    fixed = lambda i: (0, 0)
    return pl.pallas_call(
        functools.partial(_ln_in_proj_kernel, n_first=n_first),
        grid=((tp + ts) // tm,),
        in_specs=[_prompt_spec(tm, seq, D_MODEL, n_first), _two_phase_specs((tm, D_MODEL), n_first)[1]]
                 + [pl.BlockSpec((1, D_MODEL), fixed), pl.BlockSpec((1, D_MODEL), fixed),
                    pl.BlockSpec((D_MODEL, D_IN), fixed)],
        out_specs=[pl.BlockSpec((tm, D_MODEL), lambda i: (i, 0))]
                  + _two_phase_lane_block_specs(tm, n_first)
                  + _two_phase_specs((tm, D_HG), n_first)
                  + _two_phase_specs((tm, 3 * D_HG), n_first),
        out_shape=[jax.ShapeDtypeStruct((tp + ts, D_MODEL), F32),
                   jax.ShapeDtypeStruct((D_S5 // LANES, tp, LANES), F32),
                   jax.ShapeDtypeStruct((D_S5 // LANES, ts, LANES), F32),
                   jax.ShapeDtypeStruct((tp, D_HG), F32), jax.ShapeDtypeStruct((ts, D_HG), F32),
                   jax.ShapeDtypeStruct((tp, 3 * D_HG), BF16), jax.ShapeDtypeStruct((ts, 3 * D_HG), BF16)],
        compiler_params=_cparams(),
        name="ln_in_proj",
    )(xp, xs, g, b, w_bf16)


def _lane_block(j):
    return slice(j * LANES, (j + 1) * LANES)


def _granule_transpose(slabs):
    per_block = LANES // S5_GROUP
    granule = lax.broadcasted_iota(jnp.int32, (1, LANES), 1) // S5_GROUP
    x = list(slabs)
    for d in (4, 2, 1):
        keep = (granule & d) == 0
        y = [None] * per_block
        for i in range(per_block):
            if i & d == 0:
                y[i] = jnp.where(keep, x[i], pltpu.roll(x[i + d], d * S5_GROUP, axis=1))
                y[i + d] = jnp.where(keep, pltpu.roll(x[i], LANES - d * S5_GROUP, axis=1), x[i + d])
        x = y
    return x


def _s5_rows_kernel(u_ref, o_ref, *, chunks):
    per_block = LANES // S5_GROUP
    for gcol in range(D_S5 // LANES):
        for j in range(S5_CONV // LANES):
            by_time = [u_ref[gcol, pl.ds(per_block * j + sl, chunks, stride=CHUNK), :] for sl in range(per_block)]
            for gl, rows in enumerate(_granule_transpose(by_time)):
                o_ref[gcol * per_block + gl, :, _lane_block(j)] = rows.astype(BF16)


def s5_rows(u, chunks):
    nblk, t, _ = u.shape
    r = t // CHUNK
    return pl.pallas_call(
        functools.partial(_s5_rows_kernel, chunks=chunks),
        grid=(r // chunks,),
        in_specs=[pl.BlockSpec((nblk, chunks * CHUNK, LANES), lambda i: (0, i, 0))],
        out_specs=pl.BlockSpec((S5_GROUPS, chunks, S5_CONV), lambda i: (0, i, 0)),
        out_shape=jax.ShapeDtypeStruct((S5_GROUPS, r, S5_CONV), BF16),
        compiler_params=_cparams(),
        name="s5_rows",
    )(u)


def _s5_tokens_kernel(y_ref, o_ref, *, chunks):
    per_block = LANES // S5_GROUP
    for gcol in range(D_S5 // LANES):
        for j in range(S5_CONV // LANES):
            by_group = [y_ref[gcol * per_block + gl, :, _lane_block(j)].astype(F32) for gl in range(per_block)]
            for sl, rows in enumerate(_granule_transpose(by_group)):
                o_ref[gcol, pl.ds(per_block * j + sl, chunks, stride=CHUNK), :] = rows


def s5_tokens(y_rows, chunks):
    _, r, _ = y_rows.shape
    return pl.pallas_call(
        functools.partial(_s5_tokens_kernel, chunks=chunks),
        grid=(r // chunks,),
        in_specs=[pl.BlockSpec((S5_GROUPS, chunks, S5_CONV), lambda i: (0, i, 0))],
        out_specs=pl.BlockSpec((D_S5 // LANES, chunks * CHUNK, LANES), lambda i: (0, i, 0)),
        out_shape=jax.ShapeDtypeStruct((D_S5 // LANES, r * CHUNK, LANES), F32),
        compiler_params=_cparams(),
        name="s5_tokens",
    )(y_rows)


def _s5_prep_kernel(lrc_ref, lic_ref, lrr_ref, lir_ref, ls_ref, brt_ref, bit_ref, ctr_ref, cti_ref,
                    m_ref, w_ref, wc_ref, a_ref):
    step = jnp.exp(ls_ref[0])

    def discretise(lr_raw, li):
        lr = jnp.minimum(lr_raw, -1e-4)
        dr, di = lr * step, li * step
        mag = jnp.exp(dr)
        a_re, a_im = mag * jnp.cos(di), mag * jnp.sin(di)
        den = lr * lr + li * li
        nr = a_re - 1.0
        fr = (nr * lr + a_im * li) / den
        fi = (a_im * lr - nr * li) / den
        return dr, di, fr, fi

    dr_c, di_c, _, _ = discretise(lrc_ref[0], lic_ref[0])
    dr_r, di_r, fr_r, fi_r = discretise(lrr_ref[0], lir_ref[0])

    lane = lax.broadcasted_iota(jnp.int32, (1, S5_CONV), 1)
    t_row = lax.broadcasted_iota(jnp.int32, (1, CHUNK), 1).astype(F32)
    t_col = lax.broadcasted_iota(jnp.int32, (CHUNK, 1), 0).astype(F32)
    lag_of_lane = (lax.broadcasted_iota(jnp.int32, (CHUNK, S5_CONV), 1) // S5_GROUP
                   == lax.broadcasted_iota(jnp.int32, (CHUNK, S5_CONV), 0)).astype(F32)
    time_of_row = (lax.broadcasted_iota(jnp.int32, (S5_CONV, CHUNK), 0) // S5_GROUP
                   == lax.broadcasted_iota(jnp.int32, (S5_CONV, CHUNK), 1)).astype(F32)
    chan_of_lane = (lax.broadcasted_iota(jnp.int32, (S5_GROUP, S5_CONV), 1) % S5_GROUP
                    == lax.broadcasted_iota(jnp.int32, (S5_GROUP, S5_CONV), 0)).astype(F32)
    chan_of_row = (lax.broadcasted_iota(jnp.int32, (S5_CONV, S5_GROUP), 0) % S5_GROUP
                   == lax.broadcasted_iota(jnp.int32, (S5_CONV, S5_GROUP), 1)).astype(F32)
    ctr = _dot(ctr_ref[0], chan_of_lane, HIGHEST)
    cti = _dot(cti_ref[0], chan_of_lane, HIGHEST)

    def c_times_power(tf):
        mag = jnp.exp(dr_c * tf)
        ang = di_c * tf
        pr = _dot(mag * jnp.cos(ang), lag_of_lane, HIGHEST)
        pi = _dot(mag * jnp.sin(ang), lag_of_lane, HIGHEST)
        return ctr * pr - cti * pi, ctr * pi + cti * pr

    cpr, cpi = c_times_power(t_row)
    bbr = fr_r * brt_ref[0] - fi_r * bit_ref[0]
    bbi = fr_r * bit_ref[0] + fi_r * brt_ref[0]
    kt = _dot(bbr, cpr, HIGHEST) - _dot(bbi, cpi, HIGHEST)
    for s in range(CHUNK):
        shifted = kt if s == 0 else pltpu.roll(kt, S5_GROUP * s, axis=1)
        m_ref[0, S5_GROUP * s:S5_GROUP * (s + 1), :] = jnp.where(
            lane >= S5_GROUP * s, shifted, 0.0).astype(BF16)

    rem = CHUNK - 1.0 - t_col
    magw = jnp.exp(dr_r * rem)
    angw = di_r * rem
    pwr = _dot(time_of_row, magw * jnp.cos(angw), HIGHEST)
    pwi = _dot(time_of_row, magw * jnp.sin(angw), HIGHEST)
    bbtr = _dot(chan_of_row, bbr, HIGHEST)
    bbti = _dot(chan_of_row, bbi, HIGHEST)
    w_ref[0, :, :S5_STATE] = pwr * bbtr - pwi * bbti
    w_ref[0, :, S5_STATE:] = pwr * bbti + pwi * bbtr

    c1r, c1i = c_times_power(t_row + 1.0)
    wc_ref[0, :S5_STATE, :] = c1r.astype(BF16)
    wc_ref[0, S5_STATE:, :] = (-c1i).astype(BF16)

    full = float(CHUNK)
    mag_c = jnp.exp(dr_r * full)
    a_ref[0, 0:1, :] = mag_c * jnp.cos(di_r * full)
    a_ref[0, 1:2, :] = mag_c * jnp.sin(di_r * full)


def s5_prep(lam_re, lam_im, log_step, b_re, b_im, c_re, c_im):
    g, p = lam_re.shape
    brt = jnp.swapaxes(b_re, 1, 2)
    bit = jnp.swapaxes(b_im, 1, 2)
    args = (lam_re.reshape(g, p, 1), lam_im.reshape(g, p, 1),
            lam_re.reshape(g, 1, p), lam_im.reshape(g, 1, p), log_step.reshape(g, 1, 1),
            brt, bit, jnp.swapaxes(c_re, 1, 2), jnp.swapaxes(c_im, 1, 2))
    spec = lambda a: pl.BlockSpec((1,) + a.shape[1:], lambda i: (i, 0, 0))
    out_shape = [jax.ShapeDtypeStruct((g, S5_CONV, S5_CONV), BF16),
                 jax.ShapeDtypeStruct((g, S5_CONV, 2 * S5_STATE), F32),
                 jax.ShapeDtypeStruct((g, 2 * S5_STATE, S5_CONV), BF16),
                 jax.ShapeDtypeStruct((g, 2, S5_STATE), F32)]
    return pl.pallas_call(
        _s5_prep_kernel,
        grid=(g,),
        in_specs=[spec(a) for a in args],
        out_specs=[spec(o) for o in out_shape],
        out_shape=out_shape,
        compiler_params=_cparams(),
        name="s5_prep",
    )(*args)


def _split3(w):
    hi = w.astype(BF16)
    r1 = w - hi.astype(F32)
    mid = r1.astype(BF16)
    lo = (r1 - mid.astype(F32)).astype(BF16)
    return hi, mid, lo


def _s5_main_kernel(up_ref, us_ref, m_ref, w_ref, wc_ref, a_ref, xsr_ref, xsi_ref,
                    yp_ref, ys_ref, fpr_ref, fpi_ref, fsr_ref, fsi_ref, fin_scr, *, n_prompt, n_chunks):
    w3 = jnp.concatenate(_split3(w_ref[0]), axis=1)
    ar = a_ref[0, 0:1, :]
    ai = a_ref[0, 1:2, :]
    width = 2 * S5_STATE

    def local(u):
        h3 = _dot(u, w3)
        return _dot(u, m_ref[0]), h3[:, :width] + h3[:, width:2 * width] + h3[:, 2 * width:]

    def times(pr, pi, x):
        return (jnp.concatenate([pr, pr], axis=1) * x
                + jnp.concatenate([-pi, pi], axis=1) * pltpu.roll(x, S5_STATE, axis=1))

    y_local, x = local(up_ref[0])
    chunk_of_row = lax.broadcasted_iota(jnp.int32, (n_prompt * n_chunks, 1), 0) % n_chunks
    pr, pi = ar, ai
    d = 1
    while d < n_chunks:
        x = x + times(pr, pi, jnp.where(chunk_of_row >= d, pltpu.roll(x, d, axis=0), 0.0))
        pr, pi = pr * pr - pi * pi, 2.0 * pr * pi
        d *= 2
    fin_scr[...] = x
    last = fin_scr[pl.ds(n_chunks - 1, n_prompt, stride=n_chunks), :]
    fpr_ref[0] = last[:, :S5_STATE]
    fpi_ref[0] = last[:, S5_STATE:]
    x0 = jnp.where(chunk_of_row >= 1, pltpu.roll(x, 1, axis=0), 0.0)
    yp_ref[0] = (y_local + _dot(x0.astype(BF16), wc_ref[0])).astype(BF16)

    y_local, hend = local(us_ref[0])
    x0 = jnp.concatenate([xsr_ref[0], xsi_ref[0]], axis=1)
    fin = times(ar, ai, x0) + hend
    fsr_ref[0] = fin[:, :S5_STATE]
    fsi_ref[0] = fin[:, S5_STATE:]
    ys_ref[0] = (y_local + _dot(x0.astype(BF16), wc_ref[0])).astype(BF16)


def s5_main(up_rows, us_rows, m, w, wc, a, xs_re, xs_im, n_prompt, n_chunks):
    g, r, _ = up_rows.shape
    n_sample = xs_re.shape[1]
    spec = lambda shape: pl.BlockSpec((1,) + tuple(shape[1:]), lambda i: (i, 0, 0))
    args = (up_rows, us_rows, m, w, wc, a, xs_re, xs_im)
    out_shape = [jax.ShapeDtypeStruct((g, r, S5_CONV), BF16),
                 jax.ShapeDtypeStruct((g, n_sample, S5_CONV), BF16),
                 jax.ShapeDtypeStruct((g, n_prompt, S5_STATE), F32),
                 jax.ShapeDtypeStruct((g, n_prompt, S5_STATE), F32),
                 jax.ShapeDtypeStruct((g, n_sample, S5_STATE), F32),
                 jax.ShapeDtypeStruct((g, n_sample, S5_STATE), F32)]
    return pl.pallas_call(
        functools.partial(_s5_main_kernel, n_prompt=n_prompt, n_chunks=n_chunks),
        grid=(g,),
        in_specs=[spec(x.shape) for x in args],
        out_specs=[spec(o.shape) for o in out_shape],
        out_shape=out_shape,
        scratch_shapes=[pltpu.VMEM((r, 2 * S5_STATE), F32)],
        compiler_params=_cparams(),
        name="s5_main",
    )(*args)


def _hgrn_kernel(f_ref, z_ref, s0_ref, lb_ref, ng_ref, o_ref, sfin_ref, st_scr, *, n_seq, chunk):
    c = pl.program_id(1)

    @pl.when(c == 0)
    def _():
        st_scr[...] = s0_ref[...]

    lbw = lb_ref[...]
    lbe = jnp.exp(lbw - jnp.max(lbw, axis=0, keepdims=True))
    lb_all = lbe[0:1, :] / jnp.sum(lbe, axis=0, keepdims=True)

    levels = [chunk >> (i + 1) for i in range(chunk.bit_length() - 1)]
    rowi = lax.broadcasted_iota(jnp.int32, (chunk, chunk), 0)
    coli = lax.broadcasted_iota(jnp.int32, (chunk, chunk), 1)
    rowk = lax.broadcasted_iota(jnp.int32, (chunk, HG_D), 0)
    sign, valid = [], []
    for m in levels:
        sign.append(jnp.where((rowk % (2 * m)) >= m, 1.0, -1.0))
        valid.append(((rowi // (2 * m)) == (coli // (2 * m)))
                     & ((rowi % (2 * m)) >= m) & ((coli % (2 * m)) < m))
    cum_mat = (coli <= rowi).astype(BF16)
    cum_mat3 = jnp.concatenate([cum_mat] * 3, axis=1)
    diag = rowi == coli
    nt = (((1,), (1,)), ((), ()))

    def body(n, carry):
        zf = f_ref[n]
        fg_all = lb_all + (1.0 - lb_all) * jax.nn.sigmoid(zf)
        cums = _dot(cum_mat3, jnp.concatenate(_split3(jnp.log2(fg_all)), axis=0))
        for hd in range(HG_HEADS):
            cols = slice(hd * HG_D, (hd + 1) * HG_D)
            zq = z_ref[n, :, hd * HG_D:(hd + 1) * HG_D].astype(F32)
            vb = z_ref[n, :, D_HG + hd * HG_D:D_HG + (hd + 1) * HG_D]
            zg = z_ref[n, :, 2 * D_HG + hd * HG_D:2 * D_HG + (hd + 1) * HG_D].astype(F32)
            q = zq * jax.nn.sigmoid(zq)
            kk = 1.0 - fg_all[:, cols]
            bcum = cums[:chunk, cols]
            b_last = bcum[chunk - 1:chunk, :]
            qb = q.astype(BF16)
            kb = kk.astype(BF16)
            st = st_scr[n, hd]

            scores = jnp.where(diag, lax.dot_general(qb, kb, nt, preferred_element_type=F32), 0.0)
            for lvl, m in enumerate(levels):
                if 2 * m >= SUBLANES:
                    bref = jnp.concatenate(
                        [jnp.broadcast_to(bcum[b * 2 * m + m - 1:b * 2 * m + m, :], (2 * m, HG_D))
                         for b in range(chunk // (2 * m))], axis=0)
                else:
                    offs = rowk % (2 * m) - (m - 1)
                    bref = bcum
                    for o in range(-(m - 1), m + 1):
                        if o != 0:
                            bref = jnp.where(offs == o, pltpu.roll(bcum, o % chunk, axis=0), bref)
                dec = jnp.exp2((bcum - bref) * sign[lvl]).astype(BF16)
                sc = lax.dot_general(qb * dec, kb * dec, nt, preferred_element_type=F32)
                scores = jnp.where(valid[lvl], sc, scores)

            qd = (q * jnp.exp2(bcum)).astype(BF16)
            o = lax.dot_general(qd, st.astype(BF16), nt, preferred_element_type=F32)
            o = o + _dot(scores.astype(BF16), vb)
            kdec = (kk * jnp.exp2(b_last - bcum)).astype(BF16)
            st_scr[n, hd] = jnp.exp2(b_last) * st + lax.dot_general(
                vb, kdec, (((0,), (0,)), ((), ())), preferred_element_type=F32)

            on = o * lax.rsqrt(jnp.mean(o * o, axis=-1, keepdims=True) + RMS_EPS) * ng_ref[:, cols]
            o_ref[n, :, hd * HG_D:(hd + 1) * HG_D] = on * (zg * jax.nn.sigmoid(zg))
        return carry

    lax.fori_loop(0, n_seq, body, 0, unroll=True)

    @pl.when(c == pl.num_programs(1) - 1)
    def _():
        sfin_ref[...] = st_scr[...]


def hgrn(f, z, s0_t, hg_lb, norm_g, n_seq, chunk):
    n, length, _ = z.shape
    return pl.pallas_call(
        functools.partial(_hgrn_kernel, n_seq=n_seq, chunk=chunk),
        grid=(n // n_seq, length // chunk),
        in_specs=[pl.BlockSpec((n_seq, chunk, D_HG), lambda g, c: (g, c, 0)),
                  pl.BlockSpec((n_seq, chunk, 3 * D_HG), lambda g, c: (g, c, 0)),
                  pl.BlockSpec((n_seq, HG_HEADS, HG_D, HG_D), lambda g, c: (g, 0, 0, 0)),
                  pl.BlockSpec(hg_lb.shape, lambda g, c: (0, 0)),
                  pl.BlockSpec((1, D_HG), lambda g, c: (0, 0))],
        out_specs=[pl.BlockSpec((n_seq, chunk, D_HG), lambda g, c: (g, c, 0)),
                   pl.BlockSpec((n_seq, HG_HEADS, HG_D, HG_D), lambda g, c: (g, 0, 0, 0))],
        out_shape=[jax.ShapeDtypeStruct((n, length, D_HG), F32),
                   jax.ShapeDtypeStruct((n, HG_HEADS, HG_D, HG_D), F32)],
        scratch_shapes=[pltpu.VMEM((n_seq, HG_HEADS, HG_D, HG_D), F32)],
        compiler_params=_cparams(2),
        name="hgrn",
    )(f, z, s0_t, hg_lb, norm_g)


def _post_mix_kernel(h_ref, up_ref, us_ref, yp_ref, ys_ref, hgp_ref, hgs_ref,
                     d_ref, wglu_ref, bglu_ref, s5g_ref, wout_ref, g1_ref, b1_ref, rwt_ref, rb_ref,
                     h1_ref, slot_ref, gate_ref, before_ref, cnt_ref, run_scr, *, tm, n_first):
    i = pl.program_id(0)

    @pl.when(i == 0)
    def _():
        run_scr[...] = jnp.zeros_like(run_scr)

    def phase(u_ref, y_ref, hg_ref):
        ys = _read_lane_blocks(y_ref) + d_ref[...] * _read_lane_blocks(u_ref)
        gl = 0.5 * ys * (1.0 + lax.erf(ys * (2.0 ** -0.5)))
        s5o = gl * jax.nn.sigmoid(_dot(gl.astype(BF16), wglu_ref[...]) + bglu_ref[...])
        s5o = s5o * lax.rsqrt(jnp.mean(s5o * s5o, axis=-1, keepdims=True) + RMS_EPS) * s5g_ref[...]
        mix = (_dot(s5o.astype(BF16), wout_ref[:D_S5, :])
               + _dot(hg_ref[...].astype(BF16), wout_ref[D_S5:, :]))
        h1 = _layernorm(DEEPNORM_ALPHA * h_ref[...] + mix, g1_ref[...], b1_ref[...])
        _write_row_tiles(h1_ref, h1, tm)

        h_hi, h_mid, _ = _split3(h1)
        w_hi, w_mid, _ = _split3(rwt_ref[...])
        nt = (((1,), (1,)), ((), ()))
        logits = (lax.dot_general(w_hi, h_hi, nt, preferred_element_type=F32)
                  + lax.dot_general(w_hi, h_mid, nt, preferred_element_type=F32)
                  + lax.dot_general(w_mid, h_hi, nt, preferred_element_type=F32)) + rb_ref[...]
        eid = lax.broadcasted_iota(jnp.int32, (N_EXPERTS, tm), 0)
        vals, idxs = [], []
        for _ in range(TOP_K):
            m = jnp.max(logits, axis=0, keepdims=True)
            ix = jnp.min(jnp.where(logits == m, eid, N_EXPERTS), axis=0, keepdims=True)
            vals.append(m)
            idxs.append(ix)
            logits = jnp.where(eid == ix, -jnp.inf, logits)
        exps = [jnp.exp(v - vals[0]) for v in vals]
        den = exps[0] + exps[1] + exps[2] + exps[3]

        onehot = jnp.zeros((N_EXPERTS, tm), F32)
        for ix in idxs:
            onehot = onehot + (eid == ix).astype(F32)
        rowi = lax.broadcasted_iota(jnp.int32, (tm, tm), 0)
        coli = lax.broadcasted_iota(jnp.int32, (tm, tm), 1)
        earlier = (rowi < coli).astype(BF16)
        prefix = _dot(onehot.astype(BF16), earlier)
        tile_cnt = jnp.sum(onehot, axis=1, keepdims=True)
        for k in range(TOP_K):
            lower_experts = jnp.sum(jnp.where(eid < idxs[k], tile_cnt, 0.0), axis=0, keepdims=True)
            rank = jnp.sum(jnp.where(eid == idxs[k], prefix, 0.0), axis=0, keepdims=True)
            slot_ref[0, :, k * tm:(k + 1) * tm] = (lower_experts + rank).astype(jnp.int32)
            gate_ref[0, :, k * tm:(k + 1) * tm] = exps[k] / den
        before_ref[0] = run_scr[...]
        run_scr[...] = run_scr[...] + tile_cnt
        cnt_ref[...] = run_scr[...]

    pl.when(i < n_first)(lambda: phase(up_ref, yp_ref, hgp_ref))
    pl.when(i >= n_first)(lambda: phase(us_ref, ys_ref, hgs_ref))


def post_mix(h0, u_pair, y_pair, hg_pair, d_skip, wglu, bglu, s5g, wout, g1, b1, rw_t, rb_col, tm):
    t = h0.shape[0]
    n_first = u_pair[0].shape[1] // tm
    row = lambda i: (i, 0)
    fixed = lambda i: (0, 0)
    full = lambda a: pl.BlockSpec(a.shape, fixed)
    weights = (d_skip, wglu, bglu, s5g, wout, g1, b1, rw_t, rb_col)
    return pl.pallas_call(
        functools.partial(_post_mix_kernel, tm=tm, n_first=n_first),
        grid=(t // tm,),
        in_specs=[pl.BlockSpec((tm, D_MODEL), row)]
                 + _two_phase_lane_block_specs(tm, n_first) * 2 + _two_phase_specs((tm, D_HG), n_first)
                 + [full(a) for a in weights],
        out_specs=[pl.BlockSpec((tm * ROW_TILES, LANES), row),
                   pl.BlockSpec((1, 1, TOP_K * tm), lambda i: (i, 0, 0)),
                   pl.BlockSpec((1, 1, TOP_K * tm), lambda i: (i, 0, 0)),
                   pl.BlockSpec((1, N_EXPERTS, 1), lambda i: (i, 0, 0)),
                   pl.BlockSpec((N_EXPERTS, 1), fixed)],
        out_shape=[jax.ShapeDtypeStruct((t * ROW_TILES, LANES), F32),
                   jax.ShapeDtypeStruct((t // tm, 1, TOP_K * tm), jnp.int32),
                   jax.ShapeDtypeStruct((t // tm, 1, TOP_K * tm), F32),
                   jax.ShapeDtypeStruct((t // tm, N_EXPERTS, 1), F32),
                   jax.ShapeDtypeStruct((N_EXPERTS, 1), F32)],
        scratch_shapes=[pltpu.VMEM((N_EXPERTS, 1), F32)],
        compiler_params=_cparams(),
        name="post_mix",
    )(h0, *u_pair, *y_pair, *hg_pair, *weights)


def _segment_copies(meta_ref, tile, tm, make_copy):
    for e in range(N_EXPERTS):
        sorted_row = meta_ref[tile, e]
        cnt = meta_ref[tile, N_EXPERTS + e]
        staged_row = meta_ref[tile, 2 * N_EXPERTS + e]
        for b in range(tm.bit_length()):
            done = cnt & ((1 << b) - 1)

            @pl.when(((cnt >> b) & 1) == 1)
            def _(b=b, done=done, e=e):
                make_copy(staged_row + done, sorted_row + done, 1 << b).start(priority=e % 2)


def _dispatch_kernel(meta_ref, pend_ref, slot_ref, h_ref, xs_ref, stage, zero_scr, zsem, sem, *, tm):
    n_rows = xs_ref.shape[0] // ROW_TILES
    i = pl.program_id(0)
    n = pl.num_programs(0)
    slot = i % 2

    def drain(s):
        pltpu.make_async_copy(stage.at[s], xs_ref.at[pl.ds(0, TOP_K * tm * ROW_TILES)], sem.at[s]).wait()

    @pl.when(i == 0)
    def _():
        zero_scr[...] = jnp.zeros_like(zero_scr)

        def last_block(e):
            prev = pend_ref[e - 1] if e > 0 else 0
            copy = pltpu.make_async_copy(
                zero_scr, xs_ref.at[pl.ds(pl.multiple_of(jnp.maximum(pend_ref[e] - MOE_ROWS, 0) * ROW_TILES,
                                                         ROW_TILES), MOE_ROWS * ROW_TILES)], zsem)
            return pend_ref[e] > prev, copy

        def tail_block(j):
            row0 = pend_ref[N_EXPERTS - 1] + j * MOE_ROWS
            copy = pltpu.make_async_copy(
                zero_scr, xs_ref.at[pl.ds(pl.multiple_of(jnp.minimum(row0, n_rows - MOE_ROWS) * ROW_TILES,
                                                         ROW_TILES), MOE_ROWS * ROW_TILES)], zsem)
            return row0 < n_rows, copy

        blocks = [last_block(e) for e in range(N_EXPERTS)] + [tail_block(j) for j in range(N_EXPERTS)]
        for used, copy in blocks:
            pl.when(used)(copy.start)
        for used, copy in blocks:
            pl.when(used)(copy.wait)

    pl.when(i >= 2)(lambda: drain(slot))

    def body(t, carry):
        row = h_ref[_row_tile(t), :]
        for k in range(TOP_K):
            stage[slot, _row_tile(slot_ref[k * tm + t]), :] = row
        return carry

    lax.fori_loop(0, tm, body, 0, unroll=8)
    _segment_copies(meta_ref, i, tm, lambda staged_row, sorted_row, rows: pltpu.make_async_copy(
        stage.at[slot, _row_tiles(staged_row, rows)], xs_ref.at[_row_tiles(sorted_row, rows)], sem.at[slot]))

    @pl.when(i == n - 1)
    def _():
        pl.when(n >= 2)(lambda: drain(1 - slot))
        drain(slot)


def dispatch(meta, pend, slots_flat, h1, n_rows, tm):
    t = h1.shape[0] // ROW_TILES
    grid_spec = pltpu.PrefetchScalarGridSpec(
        num_scalar_prefetch=2,
        grid=(t // tm,),
        in_specs=[pl.BlockSpec((TOP_K * tm,), lambda i, meta, pend: (i,), memory_space=pltpu.SMEM),
                  pl.BlockSpec((tm * ROW_TILES, LANES), lambda i, meta, pend: (i, 0))],
        out_specs=pl.BlockSpec(memory_space=pl.ANY),
        scratch_shapes=[pltpu.VMEM((2, TOP_K * tm * ROW_TILES, LANES), F32),
                        pltpu.VMEM((MOE_ROWS * ROW_TILES, LANES), F32),
                        pltpu.SemaphoreType.DMA(()), pltpu.SemaphoreType.DMA((2,))],
    )
    return pl.pallas_call(
        functools.partial(_dispatch_kernel, tm=tm),
        grid_spec=grid_spec,
        out_shape=jax.ShapeDtypeStruct((n_rows * ROW_TILES, LANES), F32),
        compiler_params=_cparams(),
        name="moe_dispatch",
    )(meta, pend, slots_flat, h1)


def _moe_ffn_kernel(be_ref, nu_ref, seg_ref, nxt_ref, x_ref, wg_ref, bg_ref, wu_ref, bu_ref, wd_ref, bd_ref,
                    y_ref, wbuf, wbf, sem):
    i = pl.program_id(0)
    hbm = (wg_ref, wu_ref, wd_ref)

    def weight_copies(expert, s):
        return [pltpu.make_async_copy(hbm[j].at[expert], wbuf.at[s, j], sem.at[s, j]) for j in range(3)]

    @pl.when((i == 0) | (be_ref[i] != be_ref[jnp.maximum(i - 1, 0)]))
    def _():
        s = seg_ref[i] % 2

        @pl.when(i == 0)
        def _():
            for c in weight_copies(be_ref[0], 0):
                c.start()

        for j, c in enumerate(weight_copies(be_ref[i], s)):
            c.wait()
            wbf[j] = wbuf[s, j].astype(BF16)

        @pl.when(nxt_ref[i] >= 0)
        def _():
            for c in weight_copies(nxt_ref[i], 1 - s):
                c.start()

    @pl.when(i < nu_ref[0])
    def _():
        x = _read_row_tiles(x_ref, MOE_ROWS).astype(BF16)
        gt = jnp.minimum(_dot(x, wbf[0]) + bg_ref[0], SWIGLU_LIMIT)
        up = jnp.clip(_dot(x, wbf[1]) + bu_ref[0], -SWIGLU_LIMIT, SWIGLU_LIMIT)
        hid = (up + 1.0) * (gt * jax.nn.sigmoid(SWIGLU_ALPHA * gt))
        _write_row_tiles(y_ref, _dot(hid.astype(BF16), wbf[2]) + bd_ref[0], MOE_ROWS)

    @pl.when(i >= nu_ref[0])
    def _():
        y_ref[...] = jnp.zeros_like(y_ref)


def moe_ffn(block_e, n_used, segment, next_e, xs, wg, bg, wu, bu, wd, bd):
    n_rows = xs.shape[0] // ROW_TILES
    n_blocks = n_rows // MOE_ROWS
    wsel = lambda i, be, nu, seg, nxt: (be[i], 0, 0)
    d_ff = wg.shape[-1]
    assert wg.shape[1:] == wu.shape[1:] == wd.shape[1:] == (D_MODEL, D_MODEL)
    anywhere = pl.BlockSpec(memory_space=pl.ANY)
    grid_spec = pltpu.PrefetchScalarGridSpec(
        num_scalar_prefetch=4,
        grid=(n_blocks,),
        in_specs=[pl.BlockSpec((MOE_ROWS * ROW_TILES, LANES),
                               lambda i, be, nu, seg, nxt: (jnp.minimum(i, nu[0] - 1), 0)),
                  anywhere, pl.BlockSpec((1, 1, d_ff), wsel),
                  anywhere, pl.BlockSpec((1, 1, d_ff), wsel),
                  anywhere, pl.BlockSpec((1, 1, D_MODEL), wsel)],
        out_specs=pl.BlockSpec((MOE_ROWS * ROW_TILES, LANES), lambda i, be, nu, seg, nxt: (i, 0)),
        scratch_shapes=[pltpu.VMEM((2, 3, D_MODEL, D_MODEL), F32), pltpu.VMEM((3, D_MODEL, D_MODEL), BF16),
                        pltpu.SemaphoreType.DMA((2, 3))],
    )
    return pl.pallas_call(
        _moe_ffn_kernel,
        grid_spec=grid_spec,
        out_shape=jax.ShapeDtypeStruct((n_rows * ROW_TILES, LANES), F32),
        compiler_params=_cparams(),
        name="moe_ffn",
    )(block_e, n_used, segment, next_e, xs, wg, bg, wu, bu, wd, bd)


def _combine_kernel(meta_ref, slot_ref, gate_ref, h_ref, pp_ref, ps_ref, yb_ref,
                    plew_ref, plegw_ref, g2_ref, b2_ref, outp_ref, outs_ref, buf, r_scr, sem,
                    *, tm, n_first):
    i = pl.program_id(0)
    n = pl.num_programs(0)
    slot = i % 2

    def fetch(tile, s):
        _segment_copies(meta_ref, tile, tm, lambda staged_row, sorted_row, rows: pltpu.make_async_copy(
            yb_ref.at[_row_tiles(sorted_row, rows)], buf.at[s, _row_tiles(staged_row, rows)], sem.at[s]))

    pl.when(i == 0)(lambda: fetch(0, 0))
    pl.when(i + 1 < n)(lambda: fetch(i + 1, 1 - slot))
    pltpu.make_async_copy(yb_ref.at[pl.ds(0, TOP_K * tm * ROW_TILES)], buf.at[slot], sem.at[slot]).wait()

    def body(t, carry):
        acc = DEEPNORM_ALPHA * h_ref[_row_tile(t), :]
        for k in range(TOP_K):
            acc = acc + gate_ref[k * tm + t] * buf[slot, _row_tile(slot_ref[k * tm + t]), :]
        r_scr[_row_tile(t), :] = acc
        return carry

    lax.fori_loop(0, tm, body, 0, unroll=8)
    r = _read_row_tiles(r_scr, tm)
    gate = jax.nn.sigmoid(_dot(r.astype(BF16), plegw_ref[...]))

    def finish(p, store):
        e = _dot(p.astype(BF16), plew_ref[...]) * gate
        store(_layernorm(r + e, g2_ref[...], b2_ref[...]))

    def store_prompt(v):
        outp_ref[0] = v

    def store_sample(v):
        outs_ref[...] = v

    pl.when(i < n_first)(lambda: finish(pp_ref[0], store_prompt))
    pl.when(i >= n_first)(lambda: finish(ps_ref[...], store_sample))


def combine(meta, slots_flat, gates_flat, h1, p_prompt, p_sample, yb, plew, plegw, g2, b2, tm):
    t = h1.shape[0] // ROW_TILES
    nb, seq, _ = p_prompt.shape
    ts = p_sample.shape[0]
    n_first = nb * seq // tm
    fixed = lambda i, meta: (0, 0)
    flat = pl.BlockSpec((TOP_K * tm,), lambda i, meta: (i,), memory_space=pltpu.SMEM)
    sample = lambda width: pl.BlockSpec((tm, width), lambda i, meta: (jnp.maximum(i - n_first, 0), 0))
    prompt = lambda width: pl.BlockSpec((1, tm, width), _prompt_spec(tm, seq, width, n_first).index_map)
    with_meta = lambda spec: pl.BlockSpec(spec.block_shape, lambda i, meta: spec.index_map(i))
    grid_spec = pltpu.PrefetchScalarGridSpec(
        num_scalar_prefetch=1,
        grid=(t // tm,),
        in_specs=[flat, flat,
                  pl.BlockSpec((tm * ROW_TILES, LANES), lambda i, meta: (i, 0)),
                  with_meta(prompt(PLE_DIM)), sample(PLE_DIM),
                  pl.BlockSpec(memory_space=pl.ANY),
                  pl.BlockSpec(plew.shape, fixed), pl.BlockSpec(plegw.shape, fixed),
                  pl.BlockSpec((1, D_MODEL), fixed), pl.BlockSpec((1, D_MODEL), fixed)],
        out_specs=[with_meta(prompt(D_MODEL)), sample(D_MODEL)],
        scratch_shapes=[pltpu.VMEM((2, TOP_K * tm * ROW_TILES, LANES), F32),
                        pltpu.VMEM((tm * ROW_TILES, LANES), F32),
                        pltpu.SemaphoreType.DMA((2,))],
    )
    return pl.pallas_call(
        functools.partial(_combine_kernel, tm=tm, n_first=n_first),
        grid_spec=grid_spec,
        out_shape=[jax.ShapeDtypeStruct((nb, seq, D_MODEL), F32), jax.ShapeDtypeStruct((ts, D_MODEL), F32)],
        compiler_params=_cparams(),
        name="moe_combine",
    )(meta, slots_flat, gates_flat, h1, p_prompt, p_sample, yb, plew, plegw, g2, b2)


def _row(v):
    return v.reshape(1, -1)


def kernel(x_prompt, x_sample, state_s5_re, state_s5_im, state_hgrn, p_prompt, p_sample, ln_in_g, ln_in_b, w_in, s5_lambda_re, s5_lambda_im, s5_log_step, s5_b_re, s5_b_im, s5_c_re, s5_c_im, s5_d, s5_w_glu, s5_b_glu, s5_norm_g, hg_lb, hg_norm_g, w_out, ln1_g, ln1_b, router_w, router_b, w_gate, b_gate, w_up, b_up, w_down, b_down, ple_w, ple_gate_w, ln2_g, ln2_b):
    nb, seq, _ = x_prompt.shape
    ns, dseq, _ = x_sample.shape
    assert dseq == CHUNK and seq % CHUNK == 0 and w_in.shape[0] == 1
    nc = seq // CHUNK
    tp, ts = nb * seq, ns * dseq
    t = tp + ts
    tm = 512 if (tp % 512 == 0 and ts % 512 == 0) else 256
    assert tp % tm == 0 and ts % tm == 0 and seq % tm == 0 and ns % nb == 0

    w_cols = jnp.split(w_in[0], [D_S5, D_S5 + D_HG, D_S5 + 2 * D_HG], axis=1)
    h0, u_p, u_s, f_p, f_s, z_p, z_s = ln_in_proj(
        x_prompt, x_sample.reshape(ts, D_MODEL), _row(ln_in_g), _row(ln_in_b),
        jnp.concatenate([w_cols[0], w_cols[2], w_cols[1], w_cols[3]], axis=1).astype(BF16), tm)

    m, w, wc, a = s5_prep(s5_lambda_re[0], s5_lambda_im[0], s5_log_step[0],
                          s5_b_re[0], s5_b_im[0], s5_c_re[0], s5_c_im[0])
    rb = 64 if (nb * nc) % 64 == 0 else nc
    yp_rows, ys_rows, fpr, fpi, fsr, fsi = s5_main(s5_rows(u_p, rb), s5_rows(u_s, ns), m, w, wc, a,
                                                   jnp.swapaxes(state_s5_re[0], 0, 1),
                                                   jnp.swapaxes(state_s5_im[0], 0, 1), nb, nc)
    y_pair = (s5_tokens(yp_rows, rb), s5_tokens(ys_rows, ns))

    zero_state = jnp.zeros((nb, HG_HEADS, HG_D, HG_D), F32)
    ng = _row(hg_norm_g[0])
    o_p, st_p = hgrn(f_p.reshape(nb, seq, D_HG), z_p.reshape(nb, seq, 3 * D_HG), zero_state, hg_lb, ng, nb,
                     HG_CHUNK if seq % HG_CHUNK == 0 else CHUNK)
    o_s, st_s = hgrn(f_s.reshape(ns, dseq, D_HG), z_s.reshape(ns, dseq, 3 * D_HG), jnp.swapaxes(state_hgrn[0], 2, 3),
                     hg_lb, ng, nb, dseq)

    h1, slots, gates, before, counts = post_mix(
        h0, (u_p, u_s), y_pair, (o_p.reshape(tp, D_HG), o_s.reshape(ts, D_HG)),
        _row(s5_d[0]), s5_w_glu[0].astype(BF16), _row(s5_b_glu[0]),
        _row(s5_norm_g[0]), w_out[0].astype(BF16), _row(ln1_g[0]), _row(ln1_b[0]),
        router_w[0].T, router_b[0].reshape(N_EXPERTS, 1), tm)

    counts = counts[:, 0].astype(jnp.int32)
    before = before[:, :, 0].astype(jnp.int32)
    cnt = jnp.concatenate([before[1:], counts[None]], axis=0) - before
    padded = (counts + MOE_ROWS - 1) // MOE_ROWS * MOE_ROWS
    pend = jnp.cumsum(padded)
    staged = jnp.cumsum(cnt, axis=1) - cnt
    meta = jnp.concatenate([pend - padded + before, cnt, staged, jnp.zeros_like(cnt)], axis=1)
    experts = jnp.arange(N_EXPERTS, dtype=jnp.int32)

    n_blocks = -(-t * TOP_K // MOE_ROWS) + N_EXPERTS
    n_used = (pend[-1] // MOE_ROWS).astype(jnp.int32)
    blk = jnp.arange(n_blocks, dtype=jnp.int32)
    blk = jnp.minimum(blk, n_used - 1)
    block_e = jnp.sum((pend[None, :] <= (blk * MOE_ROWS)[:, None]).astype(jnp.int32), axis=1)
    block_e = jnp.minimum(block_e, N_EXPERTS - 1)
    owns_rows = padded > 0
    owner = owns_rows[None, :]
    segment = jnp.sum((owner & (experts[None, :] <= block_e[:, None])).astype(jnp.int32), axis=1) - 1
    later = jnp.where(owner & (experts[None, :] > block_e[:, None]), experts[None, :], N_EXPERTS)
    next_e = jnp.min(later, axis=1)
    next_e = jnp.where(next_e < N_EXPERTS, next_e, -1).astype(jnp.int32)

    slots_flat = slots.reshape(-1)
    xs = dispatch(meta, pend, slots_flat, h1, n_blocks * MOE_ROWS, tm)
    yb = moe_ffn(block_e, n_used.reshape(1), segment, next_e, xs,
                 w_gate[0], b_gate[0][:, None, :], w_up[0], b_up[0][:, None, :],
                 w_down[0], b_down[0][:, None, :])
    out_p, out_s = combine(meta, slots_flat, gates.reshape(-1), h1, p_prompt[0], p_sample[0].reshape(ts, PLE_DIM),
                           yb, ple_w[0].astype(BF16), ple_gate_w[0].astype(BF16),
                           _row(ln2_g[0]), _row(ln2_b[0]), tm)

    def s5_state(f, n):
        return jnp.swapaxes(f, 0, 1).reshape(1, n, S5_GROUPS, S5_STATE)

    return (out_p, out_s.reshape(ns, dseq, D_MODEL),
            s5_state(fpr, nb), s5_state(fpi, nb), jnp.swapaxes(st_p, 2, 3)[None],
            s5_state(fsr, ns), s5_state(fsi, ns), jnp.swapaxes(st_s, 2, 3)[None])
```

```python
import functools

import jax
import jax.numpy as jnp
from jax import lax
from jax.experimental import pallas as pl
from jax.experimental.pallas import tpu as pltpu

F32 = jnp.float32
BF16 = jnp.bfloat16
HIGHEST = lax.Precision.HIGHEST

D_MODEL = 1024
CHUNK = 64
PLE_DIM = 256
D_S5 = 512
S5_GROUP = 16
S5_GROUPS = 32
S5_STATE = 64
D_HG = 512
HG_HEADS = 4
HG_D = 128
D_IN = D_S5 + 4 * D_HG
N_EXPERTS = 32
TOP_K = 4
SWIGLU_LIMIT = 7.0
SWIGLU_ALPHA = 1.702
DEEPNORM_ALPHA = 2.0 ** 0.25
LN_EPS = 1e-5
RMS_EPS = 1e-6

LANES = 128
SUBLANES = 8
ROW_TILES = D_MODEL // LANES
S5_CONV = CHUNK * S5_GROUP
HG_CHUNK = 128
MOE_ROWS = 512
VMEM_LIMIT = 56 * 1024 * 1024

assert ROW_TILES == SUBLANES


def _cparams(n_axes=1, flags=None):
    return pltpu.CompilerParams(dimension_semantics=("arbitrary",) * n_axes,
                                vmem_limit_bytes=VMEM_LIMIT, flags=flags)


def _dot(a, b, precision=None):
    return jnp.dot(a, b, preferred_element_type=F32, precision=precision)


def _split3(w):
    hi = w.astype(BF16)
    r1 = w - hi.astype(F32)
    mid = r1.astype(BF16)
    lo = (r1 - mid.astype(F32)).astype(BF16)
    return hi, mid, lo


def _spread_cols(table, onehot):
    return _dot(jnp.concatenate(_split3(table), axis=1), jnp.concatenate([onehot.astype(BF16)] * 3, axis=0))


def _spread_rows(onehot, table):
    return _dot(jnp.concatenate([onehot.astype(BF16)] * 3, axis=1), jnp.concatenate(_split3(table), axis=0))


def _layernorm(x, g, b):
    mu = jnp.mean(x, axis=-1, keepdims=True)
    xc = x - mu
    var = jnp.mean(xc * xc, axis=-1, keepdims=True)
    return xc * lax.rsqrt(var + LN_EPS) * g + b


def _two_phase_specs(block, n_first):
    nd = len(block)
    first = pl.BlockSpec(block, lambda i: (jnp.minimum(i, n_first - 1),) + (0,) * (nd - 1))
    second = pl.BlockSpec(block, lambda i: (jnp.maximum(i - n_first, 0),) + (0,) * (nd - 1))
    return [first, second]


def _two_phase_lane_block_specs(tm, n_first):
    nblk = D_S5 // LANES
    first = pl.BlockSpec((nblk, tm, LANES), lambda i: (0, jnp.minimum(i, n_first - 1), 0))
    second = pl.BlockSpec((nblk, tm, LANES), lambda i: (0, jnp.maximum(i - n_first, 0), 0))
    return [first, second]


def _read_lane_blocks(ref):
    return jnp.concatenate([ref[j] for j in range(D_S5 // LANES)], axis=1)


def _prompt_spec(tm, seq, width, n_first):
    per_seq = seq // tm

    def index(i):
        ic = jnp.minimum(i, n_first - 1)
        return (ic // per_seq, ic % per_seq, 0)

    return pl.BlockSpec((1, tm, width), index)


def _chunk(rows, j):
    return pl.ds(j, rows, stride=ROW_TILES)


def _read_row_tiles(ref, rows):
    return jnp.concatenate([ref[_chunk(rows, j), :] for j in range(ROW_TILES)], axis=1)


def _write_row_tiles(ref, val, rows):
    for j in range(ROW_TILES):
        ref[_chunk(rows, j), :] = val[:, j * LANES:(j + 1) * LANES]


def _row_tiles(r, n=1):
    return pl.ds(pl.multiple_of(r * ROW_TILES, ROW_TILES), n * ROW_TILES)


def _row_tile(r):
    return _row_tiles(r)


def _ln_in_proj_kernel(xp_ref, xs_ref, g_ref, b_ref, w_ref, h_ref, up_ref, us_ref, fp_ref, fs_ref, zp_ref, zs_ref,
                       *, n_first):
    def phase(x, u_ref, f_ref, z_ref):
        h = _layernorm(x, g_ref[...], b_ref[...])
        h_ref[...] = h
        hb = h.astype(BF16)
        u = _dot(hb, w_ref[:, :D_S5])
        for j in range(D_S5 // LANES):
            u_ref[j] = u[:, j * LANES:(j + 1) * LANES]
        f_ref[...] = _dot(hb, w_ref[:, D_S5:D_S5 + D_HG])
        z_ref[...] = _dot(hb, w_ref[:, D_S5 + D_HG:]).astype(BF16)

    i = pl.program_id(0)
    pl.when(i < n_first)(lambda: phase(xp_ref[0], up_ref, fp_ref, zp_ref))
    pl.when(i >= n_first)(lambda: phase(xs_ref[...], us_ref, fs_ref, zs_ref))


def ln_in_proj(xp, xs, g, b, w_bf16, tm):
    nb, seq, _ = xp.shape
    tp, ts = nb * seq, xs.shape[0]
    n_first = tp // tm
    fixed = lambda i: (0, 0)
    return pl.pallas_call(
        functools.partial(_ln_in_proj_kernel, n_first=n_first),
        grid=((tp + ts) // tm,),
        in_specs=[_prompt_spec(tm, seq, D_MODEL, n_first), _two_phase_specs((tm, D_MODEL), n_first)[1]]
                 + [pl.BlockSpec((1, D_MODEL), fixed), pl.BlockSpec((1, D_MODEL), fixed),
                    pl.BlockSpec((D_MODEL, D_IN), fixed)],
        out_specs=[pl.BlockSpec((tm, D_MODEL), lambda i: (i, 0))]
                  + _two_phase_lane_block_specs(tm, n_first)
                  + _two_phase_specs((tm, D_HG), n_first)
                  + _two_phase_specs((tm, 3 * D_HG), n_first),
        out_shape=[jax.ShapeDtypeStruct((tp + ts, D_MODEL), F32),
                   jax.ShapeDtypeStruct((D_S5 // LANES, tp, LANES), F32),
                   jax.ShapeDtypeStruct((D_S5 // LANES, ts, LANES), F32),
                   jax.ShapeDtypeStruct((tp, D_HG), F32), jax.ShapeDtypeStruct((ts, D_HG), F32),
                   jax.ShapeDtypeStruct((tp, 3 * D_HG), BF16), jax.ShapeDtypeStruct((ts, 3 * D_HG), BF16)],
        compiler_params=_cparams(),
        name="ln_in_proj",
    )(xp, xs, g, b, w_bf16)


def _lane_block(j):
    return slice(j * LANES, (j + 1) * LANES)


def _granule_transpose(slabs):
    per_block = LANES // S5_GROUP
    granule = lax.broadcasted_iota(jnp.int32, (1, LANES), 1) // S5_GROUP
    x = list(slabs)
    for d in (4, 2, 1):
        keep = (granule & d) == 0
        y = [None] * per_block
        for i in range(per_block):
            if i & d == 0:
                y[i] = jnp.where(keep, x[i], pltpu.roll(x[i + d], d * S5_GROUP, axis=1))
                y[i + d] = jnp.where(keep, pltpu.roll(x[i], LANES - d * S5_GROUP, axis=1), x[i + d])
        x = y
    return x


def _s5_rows_kernel(u_ref, o_ref, *, chunks):
    per_block = LANES // S5_GROUP
    for gcol in range(D_S5 // LANES):
        for j in range(S5_CONV // LANES):
            by_time = [u_ref[gcol, pl.ds(per_block * j + sl, chunks, stride=CHUNK), :] for sl in range(per_block)]
            for gl, rows in enumerate(_granule_transpose(by_time)):
                o_ref[gcol * per_block + gl, :, _lane_block(j)] = rows.astype(BF16)


def s5_rows(u, chunks):
    nblk, t, _ = u.shape
    r = t // CHUNK
    return pl.pallas_call(
        functools.partial(_s5_rows_kernel, chunks=chunks),
        grid=(r // chunks,),
        in_specs=[pl.BlockSpec((nblk, chunks * CHUNK, LANES), lambda i: (0, i, 0))],
        out_specs=pl.BlockSpec((S5_GROUPS, chunks, S5_CONV), lambda i: (0, i, 0)),
        out_shape=jax.ShapeDtypeStruct((S5_GROUPS, r, S5_CONV), BF16),
        compiler_params=_cparams(),
        name="s5_rows",
    )(u)


def _s5_tokens_kernel(y_ref, o_ref, *, chunks):
    per_block = LANES // S5_GROUP
    for gcol in range(D_S5 // LANES):
        for j in range(S5_CONV // LANES):
            by_group = [y_ref[gcol * per_block + gl, :, _lane_block(j)].astype(F32) for gl in range(per_block)]
            for sl, rows in enumerate(_granule_transpose(by_group)):
                o_ref[gcol, pl.ds(per_block * j + sl, chunks, stride=CHUNK), :] = rows


def s5_tokens(y_rows, chunks):
    _, r, _ = y_rows.shape
    return pl.pallas_call(
        functools.partial(_s5_tokens_kernel, chunks=chunks),
        grid=(r // chunks,),
        in_specs=[pl.BlockSpec((S5_GROUPS, chunks, S5_CONV), lambda i: (0, i, 0))],
        out_specs=pl.BlockSpec((D_S5 // LANES, chunks * CHUNK, LANES), lambda i: (0, i, 0)),
        out_shape=jax.ShapeDtypeStruct((D_S5 // LANES, r * CHUNK, LANES), F32),
        compiler_params=_cparams(),
        name="s5_tokens",
    )(y_rows)


def _s5_prep_kernel(lrc_ref, lic_ref, lrr_ref, lir_ref, ls_ref, brt_ref, bit_ref, ctr_ref, cti_ref,
                    m_ref, w_ref, wc_ref, a_ref):
    step = jnp.exp(ls_ref[0])

    def discretise(lr_raw, li):
        lr = jnp.minimum(lr_raw, -1e-4)
        dr, di = lr * step, li * step
        mag = jnp.exp(dr)
        a_re, a_im = mag * jnp.cos(di), mag * jnp.sin(di)
        den = lr * lr + li * li
        nr = a_re - 1.0
        fr = (nr * lr + a_im * li) / den
        fi = (a_im * lr - nr * li) / den
        return dr, di, fr, fi

    dr_c, di_c, _, _ = discretise(lrc_ref[0], lic_ref[0])
    dr_r, di_r, fr_r, fi_r = discretise(lrr_ref[0], lir_ref[0])

    lane = lax.broadcasted_iota(jnp.int32, (1, S5_CONV), 1)
    t_row = lax.broadcasted_iota(jnp.int32, (1, CHUNK), 1).astype(F32)
    t_col = lax.broadcasted_iota(jnp.int32, (CHUNK, 1), 0).astype(F32)
    lag_of_lane = (lax.broadcasted_iota(jnp.int32, (CHUNK, S5_CONV), 1) // S5_GROUP
                   == lax.broadcasted_iota(jnp.int32, (CHUNK, S5_CONV), 0)).astype(F32)
    time_of_row = (lax.broadcasted_iota(jnp.int32, (S5_CONV, CHUNK), 0) // S5_GROUP
                   == lax.broadcasted_iota(jnp.int32, (S5_CONV, CHUNK), 1)).astype(F32)
    chan_of_lane = (lax.broadcasted_iota(jnp.int32, (S5_GROUP, S5_CONV), 1) % S5_GROUP
                    == lax.broadcasted_iota(jnp.int32, (S5_GROUP, S5_CONV), 0)).astype(F32)
    chan_of_row = (lax.broadcasted_iota(jnp.int32, (S5_CONV, S5_GROUP), 0) % S5_GROUP
                   == lax.broadcasted_iota(jnp.int32, (S5_CONV, S5_GROUP), 1)).astype(F32)
    ctr = _spread_cols(ctr_ref[0], chan_of_lane)
    cti = _spread_cols(cti_ref[0], chan_of_lane)

    def c_times_power(tf):
        mag = jnp.exp(dr_c * tf)
        ang = di_c * tf
        pr = _spread_cols(mag * jnp.cos(ang), lag_of_lane)
        pi = _spread_cols(mag * jnp.sin(ang), lag_of_lane)
        return ctr * pr - cti * pi, ctr * pi + cti * pr

    cpr, cpi = c_times_power(t_row)
    bbr = fr_r * brt_ref[0] - fi_r * bit_ref[0]
    bbi = fr_r * bit_ref[0] + fi_r * brt_ref[0]
    kt = _dot(bbr, cpr, HIGHEST) - _dot(bbi, cpi, HIGHEST)
    for s in range(CHUNK):
        shifted = kt if s == 0 else pltpu.roll(kt, S5_GROUP * s, axis=1)
        m_ref[0, S5_GROUP * s:S5_GROUP * (s + 1), :] = jnp.where(
            lane >= S5_GROUP * s, shifted, 0.0).astype(BF16)

    rem = CHUNK - 1.0 - t_col
    magw = jnp.exp(dr_r * rem)
    angw = di_r * rem
    pwr = _spread_rows(time_of_row, magw * jnp.cos(angw))
    pwi = _spread_rows(time_of_row, magw * jnp.sin(angw))
    bbtr = _spread_rows(chan_of_row, bbr)
    bbti = _spread_rows(chan_of_row, bbi)
    w_ref[0, :, :S5_STATE] = pwr * bbtr - pwi * bbti
    w_ref[0, :, S5_STATE:] = pwr * bbti + pwi * bbtr

    c1r, c1i = c_times_power(t_row + 1.0)
    wc_ref[0, :S5_STATE, :] = c1r.astype(BF16)
    wc_ref[0, S5_STATE:, :] = (-c1i).astype(BF16)

    full = float(CHUNK)
    mag_c = jnp.exp(dr_r * full)
    a_ref[0, 0:1, :] = mag_c * jnp.cos(di_r * full)
    a_ref[0, 1:2, :] = mag_c * jnp.sin(di_r * full)


def s5_prep(lam_re, lam_im, log_step, b_re, b_im, c_re, c_im):
    g, p = lam_re.shape
    brt = jnp.swapaxes(b_re, 1, 2)
    bit = jnp.swapaxes(b_im, 1, 2)
    args = (lam_re.reshape(g, p, 1), lam_im.reshape(g, p, 1),
            lam_re.reshape(g, 1, p), lam_im.reshape(g, 1, p), log_step.reshape(g, 1, 1),
            brt, bit, jnp.swapaxes(c_re, 1, 2), jnp.swapaxes(c_im, 1, 2))
    spec = lambda a: pl.BlockSpec((1,) + a.shape[1:], lambda i: (i, 0, 0))
    out_shape = [jax.ShapeDtypeStruct((g, S5_CONV, S5_CONV), BF16),
                 jax.ShapeDtypeStruct((g, S5_CONV, 2 * S5_STATE), F32),
                 jax.ShapeDtypeStruct((g, 2 * S5_STATE, S5_CONV), BF16),
                 jax.ShapeDtypeStruct((g, 2, S5_STATE), F32)]
    return pl.pallas_call(
        _s5_prep_kernel,
        grid=(g,),
        in_specs=[spec(a) for a in args],
        out_specs=[spec(o) for o in out_shape],
        out_shape=out_shape,
        compiler_params=_cparams(),
        name="s5_prep",
    )(*args)


def _s5_main_kernel(up_ref, us_ref, m_ref, w_ref, wc_ref, a_ref, xsr_ref, xsi_ref,
                    yp_ref, ys_ref, fpr_ref, fpi_ref, fsr_ref, fsi_ref, fin_scr, *, n_prompt, n_chunks):
    w3 = jnp.concatenate(_split3(w_ref[0]), axis=1)
    ar = a_ref[0, 0:1, :]
    ai = a_ref[0, 1:2, :]
    width = 2 * S5_STATE

    def local(u):
        h3 = _dot(u, w3)
        return _dot(u, m_ref[0]), h3[:, :width] + h3[:, width:2 * width] + h3[:, 2 * width:]

    def times(pr, pi, x):
        return (jnp.concatenate([pr, pr], axis=1) * x
                + jnp.concatenate([-pi, pi], axis=1) * pltpu.roll(x, S5_STATE, axis=1))

    y_local, x = local(up_ref[0])
    chunk_of_row = lax.broadcasted_iota(jnp.int32, (n_prompt * n_chunks, 1), 0) % n_chunks
    pr, pi = ar, ai
    d = 1
    while d < n_chunks:
        x = x + times(pr, pi, jnp.where(chunk_of_row >= d, pltpu.roll(x, d, axis=0), 0.0))
        pr, pi = pr * pr - pi * pi, 2.0 * pr * pi
        d *= 2
    fin_scr[...] = x
    last = fin_scr[pl.ds(n_chunks - 1, n_prompt, stride=n_chunks), :]
    fpr_ref[0] = last[:, :S5_STATE]
    fpi_ref[0] = last[:, S5_STATE:]
    x0 = jnp.where(chunk_of_row >= 1, pltpu.roll(x, 1, axis=0), 0.0)
    yp_ref[0] = (y_local + _dot(x0.astype(BF16), wc_ref[0])).astype(BF16)

    y_local, hend = local(us_ref[0])
    x0 = jnp.concatenate([xsr_ref[0], xsi_ref[0]], axis=1)
    fin = times(ar, ai, x0) + hend
    fsr_ref[0] = fin[:, :S5_STATE]
    fsi_ref[0] = fin[:, S5_STATE:]
    ys_ref[0] = (y_local + _dot(x0.astype(BF16), wc_ref[0])).astype(BF16)


def s5_main(up_rows, us_rows, m, w, wc, a, xs_re, xs_im, n_prompt, n_chunks):
    g, r, _ = up_rows.shape
    n_sample = xs_re.shape[1]
    spec = lambda shape: pl.BlockSpec((1,) + tuple(shape[1:]), lambda i: (i, 0, 0))
    args = (up_rows, us_rows, m, w, wc, a, xs_re, xs_im)
    out_shape = [jax.ShapeDtypeStruct((g, r, S5_CONV), BF16),
                 jax.ShapeDtypeStruct((g, n_sample, S5_CONV), BF16),
                 jax.ShapeDtypeStruct((g, n_prompt, S5_STATE), F32),
                 jax.ShapeDtypeStruct((g, n_prompt, S5_STATE), F32),
                 jax.ShapeDtypeStruct((g, n_sample, S5_STATE), F32),
                 jax.ShapeDtypeStruct((g, n_sample, S5_STATE), F32)]
    return pl.pallas_call(
        functools.partial(_s5_main_kernel, n_prompt=n_prompt, n_chunks=n_chunks),
        grid=(g,),
        in_specs=[spec(x.shape) for x in args],
        out_specs=[spec(o.shape) for o in out_shape],
        out_shape=out_shape,
        scratch_shapes=[pltpu.VMEM((r, 2 * S5_STATE), F32)],
        compiler_params=_cparams(),
        name="s5_main",
    )(*args)


def _hgrn_kernel(f_ref, z_ref, s0_ref, lb_ref, ng_ref, o_ref, sfin_ref, st_scr, *, n_seq, chunk):
    c = pl.program_id(1)

    @pl.when(c == 0)
    def _():
        st_scr[...] = s0_ref[...]

    lbw = lb_ref[...]
    lbe = jnp.exp(lbw - jnp.max(lbw, axis=0, keepdims=True))
    lb_all = lbe[0:1, :] / jnp.sum(lbe, axis=0, keepdims=True)

    levels = [chunk >> (i + 1) for i in range(chunk.bit_length() - 1)]
    rowi = lax.broadcasted_iota(jnp.int32, (chunk, chunk), 0)
    coli = lax.broadcasted_iota(jnp.int32, (chunk, chunk), 1)
    rowk = lax.broadcasted_iota(jnp.int32, (chunk, HG_D), 0)
    sign, valid = [], []
    for m in levels:
        sign.append(jnp.where((rowk % (2 * m)) >= m, 1.0, -1.0))
        valid.append(((rowi // (2 * m)) == (coli // (2 * m)))
                     & ((rowi % (2 * m)) >= m) & ((coli % (2 * m)) < m))
    cum_mat = (coli <= rowi).astype(BF16)
    cum_mat3 = jnp.concatenate([cum_mat] * 3, axis=1)
    diag = rowi == coli
    nt = (((1,), (1,)), ((), ()))

    def body(n, carry):
        zf = f_ref[n]
        fg_all = lb_all + (1.0 - lb_all) * jax.nn.sigmoid(zf)
        cums = _dot(cum_mat3, jnp.concatenate(_split3(jnp.log2(fg_all)), axis=0))
        for hd in range(HG_HEADS):
            cols = slice(hd * HG_D, (hd + 1) * HG_D)
            zq = z_ref[n, :, hd * HG_D:(hd + 1) * HG_D].astype(F32)
            vb = z_ref[n, :, D_HG + hd * HG_D:D_HG + (hd + 1) * HG_D]
            zg = z_ref[n, :, 2 * D_HG + hd * HG_D:2 * D_HG + (hd + 1) * HG_D].astype(F32)
            q = zq * jax.nn.sigmoid(zq)
            kk = 1.0 - fg_all[:, cols]
            bcum = cums[:chunk, cols]
            b_last = bcum[chunk - 1:chunk, :]
            qb = q.astype(BF16)
            kb = kk.astype(BF16)
            st = st_scr[n, hd]

            scores = jnp.where(diag, lax.dot_general(qb, kb, nt, preferred_element_type=F32), 0.0)
            for lvl, m in enumerate(levels):
                if 2 * m >= SUBLANES:
                    bref = jnp.concatenate(
                        [jnp.broadcast_to(bcum[b * 2 * m + m - 1:b * 2 * m + m, :], (2 * m, HG_D))
                         for b in range(chunk // (2 * m))], axis=0)
                else:
                    offs = rowk % (2 * m) - (m - 1)
                    bref = bcum
                    for o in range(-(m - 1), m + 1):
                        if o != 0:
                            bref = jnp.where(offs == o, pltpu.roll(bcum, o % chunk, axis=0), bref)
                dec = jnp.exp2((bcum - bref) * sign[lvl]).astype(BF16)
                sc = lax.dot_general(qb * dec, kb * dec, nt, preferred_element_type=F32)
                scores = jnp.where(valid[lvl], sc, scores)

            qd = (q * jnp.exp2(bcum)).astype(BF16)
            o = lax.dot_general(qd, st.astype(BF16), nt, preferred_element_type=F32)
            o = o + _dot(scores.astype(BF16), vb)
            kdec = (kk * jnp.exp2(b_last - bcum)).astype(BF16)
            st_scr[n, hd] = jnp.exp2(b_last) * st + lax.dot_general(
                vb, kdec, (((0,), (0,)), ((), ())), preferred_element_type=F32)

            on = o * lax.rsqrt(jnp.mean(o * o, axis=-1, keepdims=True) + RMS_EPS) * ng_ref[:, cols]
            o_ref[n, :, hd * HG_D:(hd + 1) * HG_D] = on * (zg * jax.nn.sigmoid(zg))
        return carry

    lax.fori_loop(0, n_seq, body, 0, unroll=True)

    @pl.when(c == pl.num_programs(1) - 1)
    def _():
        sfin_ref[...] = st_scr[...]


def hgrn(f, z, s0_t, hg_lb, norm_g, n_seq, chunk):
    n, length, _ = z.shape
    return pl.pallas_call(
        functools.partial(_hgrn_kernel, n_seq=n_seq, chunk=chunk),
        grid=(n // n_seq, length // chunk),
        in_specs=[pl.BlockSpec((n_seq, chunk, D_HG), lambda g, c: (g, c, 0)),
                  pl.BlockSpec((n_seq, chunk, 3 * D_HG), lambda g, c: (g, c, 0)),
                  pl.BlockSpec((n_seq, HG_HEADS, HG_D, HG_D), lambda g, c: (g, 0, 0, 0)),
                  pl.BlockSpec(hg_lb.shape, lambda g, c: (0, 0)),
                  pl.BlockSpec((1, D_HG), lambda g, c: (0, 0))],
        out_specs=[pl.BlockSpec((n_seq, chunk, D_HG), lambda g, c: (g, c, 0)),
                   pl.BlockSpec((n_seq, HG_HEADS, HG_D, HG_D), lambda g, c: (g, 0, 0, 0))],
        out_shape=[jax.ShapeDtypeStruct((n, length, D_HG), F32),
                   jax.ShapeDtypeStruct((n, HG_HEADS, HG_D, HG_D), F32)],
        scratch_shapes=[pltpu.VMEM((n_seq, HG_HEADS, HG_D, HG_D), F32)],
        compiler_params=_cparams(2),
        name="hgrn",
    )(f, z, s0_t, hg_lb, norm_g)


def _post_mix_kernel(h_ref, up_ref, us_ref, yp_ref, ys_ref, hgp_ref, hgs_ref,
                     d_ref, wglu_ref, bglu_ref, s5g_ref, wout_ref, g1_ref, b1_ref, rwt_ref, rb_ref,
                     h1_ref, slot_ref, gate_ref, before_ref, cnt_ref, run_scr, *, tm, n_first):
    i = pl.program_id(0)

    @pl.when(i == 0)
    def _():
        run_scr[...] = jnp.zeros_like(run_scr)

    def phase(u_ref, y_ref, hg_ref):
        ys = _read_lane_blocks(y_ref) + d_ref[...] * _read_lane_blocks(u_ref)
        gl = 0.5 * ys * (1.0 + lax.erf(ys * (2.0 ** -0.5)))
        s5o = gl * jax.nn.sigmoid(_dot(gl.astype(BF16), wglu_ref[...]) + bglu_ref[...])
        s5o = s5o * lax.rsqrt(jnp.mean(s5o * s5o, axis=-1, keepdims=True) + RMS_EPS) * s5g_ref[...]
        mix = (_dot(s5o.astype(BF16), wout_ref[:D_S5, :])
               + _dot(hg_ref[...].astype(BF16), wout_ref[D_S5:, :]))
        h1 = _layernorm(DEEPNORM_ALPHA * h_ref[...] + mix, g1_ref[...], b1_ref[...])
        _write_row_tiles(h1_ref, h1, tm)

        h_hi, h_mid, _ = _split3(h1)
        w_hi, w_mid, _ = _split3(rwt_ref[...])
        nt = (((1,), (1,)), ((), ()))
        logits = (lax.dot_general(w_hi, h_hi, nt, preferred_element_type=F32)
                  + lax.dot_general(w_hi, h_mid, nt, preferred_element_type=F32)
                  + lax.dot_general(w_mid, h_hi, nt, preferred_element_type=F32)) + rb_ref[...]
        eid = lax.broadcasted_iota(jnp.int32, (N_EXPERTS, tm), 0)
        vals, idxs = [], []
        for _ in range(TOP_K):
            m = jnp.max(logits, axis=0, keepdims=True)
            ix = jnp.min(jnp.where(logits == m, eid, N_EXPERTS), axis=0, keepdims=True)
            vals.append(m)
            idxs.append(ix)
            logits = jnp.where(eid == ix, -jnp.inf, logits)
        exps = [jnp.exp(v - vals[0]) for v in vals]
        den = exps[0] + exps[1] + exps[2] + exps[3]

        onehot = jnp.zeros((N_EXPERTS, tm), F32)
        for ix in idxs:
            onehot = onehot + (eid == ix).astype(F32)
        rowi = lax.broadcasted_iota(jnp.int32, (tm, tm), 0)
        coli = lax.broadcasted_iota(jnp.int32, (tm, tm), 1)
        earlier = (rowi < coli).astype(BF16)
        prefix = _dot(onehot.astype(BF16), earlier)
        tile_cnt = jnp.sum(onehot, axis=1, keepdims=True)
        for k in range(TOP_K):
            lower_experts = jnp.sum(jnp.where(eid < idxs[k], tile_cnt, 0.0), axis=0, keepdims=True)
            rank = jnp.sum(jnp.where(eid == idxs[k], prefix, 0.0), axis=0, keepdims=True)
            slot_ref[0, :, k * tm:(k + 1) * tm] = (lower_experts + rank).astype(jnp.int32)
            gate_ref[0, :, k * tm:(k + 1) * tm] = exps[k] / den
        before_ref[0] = run_scr[...]
        run_scr[...] = run_scr[...] + tile_cnt
        cnt_ref[...] = run_scr[...]

    pl.when(i < n_first)(lambda: phase(up_ref, yp_ref, hgp_ref))
    pl.when(i >= n_first)(lambda: phase(us_ref, ys_ref, hgs_ref))


def post_mix(h0, u_pair, y_pair, hg_pair, d_skip, wglu, bglu, s5g, wout, g1, b1, rw_t, rb_col, tm):
    t = h0.shape[0]
    n_first = u_pair[0].shape[1] // tm
    row = lambda i: (i, 0)
    fixed = lambda i: (0, 0)
    full = lambda a: pl.BlockSpec(a.shape, fixed)
    weights = (d_skip, wglu, bglu, s5g, wout, g1, b1, rw_t, rb_col)
    return pl.pallas_call(
        functools.partial(_post_mix_kernel, tm=tm, n_first=n_first),
        grid=(t // tm,),
        in_specs=[pl.BlockSpec((tm, D_MODEL), row)]
                 + _two_phase_lane_block_specs(tm, n_first) * 2 + _two_phase_specs((tm, D_HG), n_first)
                 + [full(a) for a in weights],
        out_specs=[pl.BlockSpec((tm * ROW_TILES, LANES), row),
                   pl.BlockSpec((1, 1, TOP_K * tm), lambda i: (i, 0, 0)),
                   pl.BlockSpec((1, 1, TOP_K * tm), lambda i: (i, 0, 0)),
                   pl.BlockSpec((1, N_EXPERTS, 1), lambda i: (i, 0, 0)),
                   pl.BlockSpec((N_EXPERTS, 1), fixed)],
        out_shape=[jax.ShapeDtypeStruct((t * ROW_TILES, LANES), F32),
                   jax.ShapeDtypeStruct((t // tm, 1, TOP_K * tm), jnp.int32),
                   jax.ShapeDtypeStruct((t // tm, 1, TOP_K * tm), F32),
                   jax.ShapeDtypeStruct((t // tm, N_EXPERTS, 1), F32),
                   jax.ShapeDtypeStruct((N_EXPERTS, 1), F32)],
        scratch_shapes=[pltpu.VMEM((N_EXPERTS, 1), F32)],
        compiler_params=_cparams(),
        name="post_mix",
    )(h0, *u_pair, *y_pair, *hg_pair, *weights)


def _segment_copies(meta_ref, tile, tm, make_copy):
    for e in range(N_EXPERTS):
        sorted_row = meta_ref[tile, e]
        cnt = meta_ref[tile, N_EXPERTS + e]
        staged_row = meta_ref[tile, 2 * N_EXPERTS + e]
        for b in range(tm.bit_length()):
            done = cnt & ((1 << b) - 1)

            @pl.when(((cnt >> b) & 1) == 1)
            def _(b=b, done=done, e=e):
                make_copy(staged_row + done, sorted_row + done, 1 << b).start(priority=e % 2)


def _dispatch_kernel(meta_ref, pend_ref, slot_ref, h_ref, xs_ref, stage, zero_scr, zsem, sem, *, tm):
    n_rows = xs_ref.shape[0] // ROW_TILES
    i = pl.program_id(0)
    n = pl.num_programs(0)
    slot = i % 2

    def drain(s):
        pltpu.make_async_copy(stage.at[s], xs_ref.at[pl.ds(0, TOP_K * tm * ROW_TILES)], sem.at[s]).wait()

    @pl.when(i == 0)
    def _():
        zero_scr[...] = jnp.zeros_like(zero_scr)

        def last_block(e):
            prev = pend_ref[e - 1] if e > 0 else 0
            copy = pltpu.make_async_copy(
                zero_scr, xs_ref.at[pl.ds(pl.multiple_of(jnp.maximum(pend_ref[e] - MOE_ROWS, 0) * ROW_TILES,
                                                         ROW_TILES), MOE_ROWS * ROW_TILES)], zsem)
            return pend_ref[e] > prev, copy

        def tail_block(j):
            row0 = pend_ref[N_EXPERTS - 1] + j * MOE_ROWS
            copy = pltpu.make_async_copy(
                zero_scr, xs_ref.at[pl.ds(pl.multiple_of(jnp.minimum(row0, n_rows - MOE_ROWS) * ROW_TILES,
                                                         ROW_TILES), MOE_ROWS * ROW_TILES)], zsem)
            return row0 < n_rows, copy

        blocks = [last_block(e) for e in range(N_EXPERTS)] + [tail_block(j) for j in range(N_EXPERTS)]
        for used, copy in blocks:
            pl.when(used)(copy.start)
        for used, copy in blocks:
            pl.when(used)(copy.wait)

    pl.when(i >= 2)(lambda: drain(slot))

    def body(t, carry):
        row = h_ref[_row_tile(t), :]
        for k in range(TOP_K):
            stage[slot, _row_tile(slot_ref[k * tm + t]), :] = row
        return carry

    lax.fori_loop(0, tm, body, 0, unroll=8)
    _segment_copies(meta_ref, i, tm, lambda staged_row, sorted_row, rows: pltpu.make_async_copy(
        stage.at[slot, _row_tiles(staged_row, rows)], xs_ref.at[_row_tiles(sorted_row, rows)], sem.at[slot]))

    @pl.when(i == n - 1)
    def _():
        pl.when(n >= 2)(lambda: drain(1 - slot))
        drain(slot)


def dispatch(meta, pend, slots_flat, h1, n_rows, tm):
    t = h1.shape[0] // ROW_TILES
    grid_spec = pltpu.PrefetchScalarGridSpec(
        num_scalar_prefetch=2,
        grid=(t // tm,),
        in_specs=[pl.BlockSpec((TOP_K * tm,), lambda i, meta, pend: (i,), memory_space=pltpu.SMEM),
                  pl.BlockSpec((tm * ROW_TILES, LANES), lambda i, meta, pend: (i, 0))],
        out_specs=pl.BlockSpec(memory_space=pl.ANY),
        scratch_shapes=[pltpu.VMEM((2, TOP_K * tm * ROW_TILES, LANES), F32),
                        pltpu.VMEM((MOE_ROWS * ROW_TILES, LANES), F32),
                        pltpu.SemaphoreType.DMA(()), pltpu.SemaphoreType.DMA((2,))],
    )
    return pl.pallas_call(
        functools.partial(_dispatch_kernel, tm=tm),
        grid_spec=grid_spec,
        out_shape=jax.ShapeDtypeStruct((n_rows * ROW_TILES, LANES), F32),
        compiler_params=_cparams(),
        name="moe_dispatch",
    )(meta, pend, slots_flat, h1)


def _moe_ffn_kernel(be_ref, nu_ref, seg_ref, nxt_ref, x_ref, wg_ref, bg_ref, wu_ref, bu_ref, wd_ref, bd_ref,
                    y_ref, wbuf, wbf, sem):
    i = pl.program_id(0)
    hbm = (wg_ref, wu_ref, wd_ref)

    def weight_copies(expert, s):
        return [pltpu.make_async_copy(hbm[j].at[expert], wbuf.at[s, j], sem.at[s, j]) for j in range(3)]

    @pl.when((i == 0) | (be_ref[i] != be_ref[jnp.maximum(i - 1, 0)]))
    def _():
        s = seg_ref[i] % 2

        @pl.when(i == 0)
        def _():
            for c in weight_copies(be_ref[0], 0):
                c.start()

        for j, c in enumerate(weight_copies(be_ref[i], s)):
            c.wait()
            wbf[j] = wbuf[s, j].astype(BF16)

        @pl.when(nxt_ref[i] >= 0)
        def _():
            for c in weight_copies(nxt_ref[i], 1 - s):
                c.start()

    @pl.when(i < nu_ref[0])
    def _():
        x = _read_row_tiles(x_ref, MOE_ROWS).astype(BF16)
        gt = jnp.minimum(_dot(x, wbf[0]) + bg_ref[0], SWIGLU_LIMIT)
        up = jnp.clip(_dot(x, wbf[1]) + bu_ref[0], -SWIGLU_LIMIT, SWIGLU_LIMIT)
        hid = (up + 1.0) * (gt * jax.nn.sigmoid(SWIGLU_ALPHA * gt))
        _write_row_tiles(y_ref, _dot(hid.astype(BF16), wbf[2]) + bd_ref[0], MOE_ROWS)

    @pl.when(i >= nu_ref[0])
    def _():
        y_ref[...] = jnp.zeros_like(y_ref)


def moe_ffn(block_e, n_used, segment, next_e, xs, wg, bg, wu, bu, wd, bd):
    n_rows = xs.shape[0] // ROW_TILES
    n_blocks = n_rows // MOE_ROWS
    wsel = lambda i, be, nu, seg, nxt: (be[i], 0, 0)
    d_ff = wg.shape[-1]
    assert wg.shape[1:] == wu.shape[1:] == wd.shape[1:] == (D_MODEL, D_MODEL)
    anywhere = pl.BlockSpec(memory_space=pl.ANY)
    grid_spec = pltpu.PrefetchScalarGridSpec(
        num_scalar_prefetch=4,
        grid=(n_blocks,),
        in_specs=[pl.BlockSpec((MOE_ROWS * ROW_TILES, LANES),
                               lambda i, be, nu, seg, nxt: (jnp.minimum(i, nu[0] - 1), 0)),
                  anywhere, pl.BlockSpec((1, 1, d_ff), wsel),
                  anywhere, pl.BlockSpec((1, 1, d_ff), wsel),
                  anywhere, pl.BlockSpec((1, 1, D_MODEL), wsel)],
        out_specs=pl.BlockSpec((MOE_ROWS * ROW_TILES, LANES), lambda i, be, nu, seg, nxt: (i, 0)),
        scratch_shapes=[pltpu.VMEM((2, 3, D_MODEL, D_MODEL), F32), pltpu.VMEM((3, D_MODEL, D_MODEL), BF16),
                        pltpu.SemaphoreType.DMA((2, 3))],
    )
    return pl.pallas_call(
        _moe_ffn_kernel,
        grid_spec=grid_spec,
        out_shape=jax.ShapeDtypeStruct((n_rows * ROW_TILES, LANES), F32),
        compiler_params=_cparams(),
        name="moe_ffn",
    )(block_e, n_used, segment, next_e, xs, wg, bg, wu, bu, wd, bd)


def _combine_kernel(meta_ref, slot_ref, gate_ref, h_ref, pp_ref, ps_ref, yb_ref,
                    plew_ref, plegw_ref, g2_ref, b2_ref, outp_ref, outs_ref, buf, r_scr, sem,
                    *, tm, n_first):
    i = pl.program_id(0)
    n = pl.num_programs(0)
    slot = i % 2

    def fetch(tile, s):
        _segment_copies(meta_ref, tile, tm, lambda staged_row, sorted_row, rows: pltpu.make_async_copy(
            yb_ref.at[_row_tiles(sorted_row, rows)], buf.at[s, _row_tiles(staged_row, rows)], sem.at[s]))

    pl.when(i == 0)(lambda: fetch(0, 0))
    pl.when(i + 1 < n)(lambda: fetch(i + 1, 1 - slot))
    pltpu.make_async_copy(yb_ref.at[pl.ds(0, TOP_K * tm * ROW_TILES)], buf.at[slot], sem.at[slot]).wait()

    def body(t, carry):
        acc = DEEPNORM_ALPHA * h_ref[_row_tile(t), :]
        for k in range(TOP_K):
            acc = acc + gate_ref[k * tm + t] * buf[slot, _row_tile(slot_ref[k * tm + t]), :]
        r_scr[_row_tile(t), :] = acc
        return carry

    lax.fori_loop(0, tm, body, 0, unroll=8)
    r = _read_row_tiles(r_scr, tm)
    gate = jax.nn.sigmoid(_dot(r.astype(BF16), plegw_ref[...]))

    def finish(p, store):
        e = _dot(p.astype(BF16), plew_ref[...]) * gate
        store(_layernorm(r + e, g2_ref[...], b2_ref[...]))

    def store_prompt(v):
        outp_ref[0] = v

    def store_sample(v):
        outs_ref[...] = v

    pl.when(i < n_first)(lambda: finish(pp_ref[0], store_prompt))
    pl.when(i >= n_first)(lambda: finish(ps_ref[...], store_sample))


def combine(meta, slots_flat, gates_flat, h1, p_prompt, p_sample, yb, plew, plegw, g2, b2, tm):
    t = h1.shape[0] // ROW_TILES
    nb, seq, _ = p_prompt.shape
    ts = p_sample.shape[0]
    n_first = nb * seq // tm
    fixed = lambda i, meta: (0, 0)
    flat = pl.BlockSpec((TOP_K * tm,), lambda i, meta: (i,), memory_space=pltpu.SMEM)
    sample = lambda width: pl.BlockSpec((tm, width), lambda i, meta: (jnp.maximum(i - n_first, 0), 0))
    prompt = lambda width: pl.BlockSpec((1, tm, width), _prompt_spec(tm, seq, width, n_first).index_map)
    with_meta = lambda spec: pl.BlockSpec(spec.block_shape, lambda i, meta: spec.index_map(i))
    grid_spec = pltpu.PrefetchScalarGridSpec(
        num_scalar_prefetch=1,
        grid=(t // tm,),
        in_specs=[flat, flat,
                  pl.BlockSpec((tm * ROW_TILES, LANES), lambda i, meta: (i, 0)),
                  with_meta(prompt(PLE_DIM)), sample(PLE_DIM),
                  pl.BlockSpec(memory_space=pl.ANY),
                  pl.BlockSpec(plew.shape, fixed), pl.BlockSpec(plegw.shape, fixed),
                  pl.BlockSpec((1, D_MODEL), fixed), pl.BlockSpec((1, D_MODEL), fixed)],
        out_specs=[with_meta(prompt(D_MODEL)), sample(D_MODEL)],
        scratch_shapes=[pltpu.VMEM((2, TOP_K * tm * ROW_TILES, LANES), F32),
                        pltpu.VMEM((tm * ROW_TILES, LANES), F32),
                        pltpu.SemaphoreType.DMA((2,))],
    )
    return pl.pallas_call(
        functools.partial(_combine_kernel, tm=tm, n_first=n_first),
        grid_spec=grid_spec,
        out_shape=[jax.ShapeDtypeStruct((nb, seq, D_MODEL), F32), jax.ShapeDtypeStruct((ts, D_MODEL), F32)],
        compiler_params=_cparams(),
        name="moe_combine",
    )(meta, slots_flat, gates_flat, h1, p_prompt, p_sample, yb, plew, plegw, g2, b2)


def _row(v):
    return v.reshape(1, -1)


def kernel(x_prompt, x_sample, state_s5_re, state_s5_im, state_hgrn, p_prompt, p_sample, ln_in_g, ln_in_b, w_in, s5_lambda_re, s5_lambda_im, s5_log_step, s5_b_re, s5_b_im, s5_c_re, s5_c_im, s5_d, s5_w_glu, s5_b_glu, s5_norm_g, hg_lb, hg_norm_g, w_out, ln1_g, ln1_b, router_w, router_b, w_gate, b_gate, w_up, b_up, w_down, b_down, ple_w, ple_gate_w, ln2_g, ln2_b):
    nb, seq, _ = x_prompt.shape
    ns, dseq, _ = x_sample.shape
    assert dseq == CHUNK and seq % CHUNK == 0 and w_in.shape[0] == 1
    nc = seq // CHUNK
    tp, ts = nb * seq, ns * dseq
    t = tp + ts
    tm = 512 if (tp % 512 == 0 and ts % 512 == 0) else 256
    assert tp % tm == 0 and ts % tm == 0 and seq % tm == 0 and ns % nb == 0

    w_cols = jnp.split(w_in[0], [D_S5, D_S5 + D_HG, D_S5 + 2 * D_HG], axis=1)
    h0, u_p, u_s, f_p, f_s, z_p, z_s = ln_in_proj(
        x_prompt, x_sample.reshape(ts, D_MODEL), _row(ln_in_g), _row(ln_in_b),
        jnp.concatenate([w_cols[0], w_cols[2], w_cols[1], w_cols[3]], axis=1).astype(BF16), tm)

    m, w, wc, a = s5_prep(s5_lambda_re[0], s5_lambda_im[0], s5_log_step[0],
                          s5_b_re[0], s5_b_im[0], s5_c_re[0], s5_c_im[0])
    rb = 64 if (nb * nc) % 64 == 0 else nc
    yp_rows, ys_rows, fpr, fpi, fsr, fsi = s5_main(s5_rows(u_p, rb), s5_rows(u_s, ns), m, w, wc, a,
                                                   jnp.swapaxes(state_s5_re[0], 0, 1),
                                                   jnp.swapaxes(state_s5_im[0], 0, 1), nb, nc)
    y_pair = (s5_tokens(yp_rows, rb), s5_tokens(ys_rows, ns))

    zero_state = jnp.zeros((nb, HG_HEADS, HG_D, HG_D), F32)
    ng = _row(hg_norm_g[0])
    o_p, st_p = hgrn(f_p.reshape(nb, seq, D_HG), z_p.reshape(nb, seq, 3 * D_HG), zero_state, hg_lb, ng, nb,
                     HG_CHUNK if seq % HG_CHUNK == 0 else CHUNK)
    o_s, st_s = hgrn(f_s.reshape(ns, dseq, D_HG), z_s.reshape(ns, dseq, 3 * D_HG), jnp.swapaxes(state_hgrn[0], 2, 3),
                     hg_lb, ng, nb, dseq)

    h1, slots, gates, before, counts = post_mix(
        h0, (u_p, u_s), y_pair, (o_p.reshape(tp, D_HG), o_s.reshape(ts, D_HG)),
        _row(s5_d[0]), s5_w_glu[0].astype(BF16), _row(s5_b_glu[0]),
        _row(s5_norm_g[0]), w_out[0].astype(BF16), _row(ln1_g[0]), _row(ln1_b[0]),
        router_w[0].T, router_b[0].reshape(N_EXPERTS, 1), tm)

    counts = counts[:, 0].astype(jnp.int32)
    before = before[:, :, 0].astype(jnp.int32)
    cnt = jnp.concatenate([before[1:], counts[None]], axis=0) - before
    padded = (counts + MOE_ROWS - 1) // MOE_ROWS * MOE_ROWS
    pend = jnp.cumsum(padded)
    staged = jnp.cumsum(cnt, axis=1) - cnt
    meta = jnp.concatenate([pend - padded + before, cnt, staged, jnp.zeros_like(cnt)], axis=1)
    experts = jnp.arange(N_EXPERTS, dtype=jnp.int32)

    n_blocks = -(-t * TOP_K // MOE_ROWS) + N_EXPERTS
    n_used = (pend[-1] // MOE_ROWS).astype(jnp.int32)
    blk = jnp.arange(n_blocks, dtype=jnp.int32)
    blk = jnp.minimum(blk, n_used - 1)
    block_e = jnp.sum((pend[None, :] <= (blk * MOE_ROWS)[:, None]).astype(jnp.int32), axis=1)
    block_e = jnp.minimum(block_e, N_EXPERTS - 1)
    owns_rows = padded > 0
    owner = owns_rows[None, :]
    segment = jnp.sum((owner & (experts[None, :] <= block_e[:, None])).astype(jnp.int32), axis=1) - 1
    later = jnp.where(owner & (experts[None, :] > block_e[:, None]), experts[None, :], N_EXPERTS)
    next_e = jnp.min(later, axis=1)
    next_e = jnp.where(next_e < N_EXPERTS, next_e, -1).astype(jnp.int32)

    slots_flat = slots.reshape(-1)
    xs = dispatch(meta, pend, slots_flat, h1, n_blocks * MOE_ROWS, tm)
    yb = moe_ffn(block_e, n_used.reshape(1), segment, next_e, xs,
                 w_gate[0], b_gate[0][:, None, :], w_up[0], b_up[0][:, None, :],
                 w_down[0], b_down[0][:, None, :])
    out_p, out_s = combine(meta, slots_flat, gates.reshape(-1), h1, p_prompt[0], p_sample[0].reshape(ts, PLE_DIM),
                           yb, ple_w[0].astype(BF16), ple_gate_w[0].astype(BF16),
                           _row(ln2_g[0]), _row(ln2_b[0]), tm)

    def s5_state(f, n):
        return jnp.swapaxes(f, 0, 1).reshape(1, n, S5_GROUPS, S5_STATE)

    return (out_p, out_s.reshape(ns, dseq, D_MODEL),
            s5_state(fpr, nb), s5_state(fpi, nb), jnp.swapaxes(st_p, 2, 3)[None],
            s5_state(fsr, ns), s5_state(fsi, ns), jnp.swapaxes(st_s, 2, 3)[None])
```

```python
import functools

import jax
import jax.numpy as jnp
from jax import lax
from jax.experimental import pallas as pl
from jax.experimental.pallas import tpu as pltpu

F32 = jnp.float32
BF16 = jnp.bfloat16
HIGHEST = lax.Precision.HIGHEST

D_MODEL = 1024
CHUNK = 64
PLE_DIM = 256
D_S5 = 512
S5_GROUP = 16
S5_GROUPS = 32
S5_STATE = 64
D_HG = 512
HG_HEADS = 4
HG_D = 128
D_IN = D_S5 + 4 * D_HG
N_EXPERTS = 32
TOP_K = 4
SWIGLU_LIMIT = 7.0
SWIGLU_ALPHA = 1.702
DEEPNORM_ALPHA = 2.0 ** 0.25
LN_EPS = 1e-5
RMS_EPS = 1e-6

LANES = 128
SUBLANES = 8
ROW_TILES = D_MODEL // LANES
S5_CONV = CHUNK * S5_GROUP
HG_CHUNK = 128
MOE_ROWS = 512
VMEM_LIMIT = 56 * 1024 * 1024

assert ROW_TILES == SUBLANES


def _cparams(n_axes=1, flags=None):
    return pltpu.CompilerParams(dimension_semantics=("arbitrary",) * n_axes,
                                vmem_limit_bytes=VMEM_LIMIT, flags=flags)


def _dot(a, b, precision=None):
    return jnp.dot(a, b, preferred_element_type=F32, precision=precision)


def _split3(w):
    hi = w.astype(BF16)
    r1 = w - hi.astype(F32)
    mid = r1.astype(BF16)
    lo = (r1 - mid.astype(F32)).astype(BF16)
    return hi, mid, lo


def _spread_cols(table, onehot):
    return _dot(jnp.concatenate(_split3(table), axis=1), jnp.concatenate([onehot.astype(BF16)] * 3, axis=0))


def _spread_rows(onehot, table):
    return _dot(jnp.concatenate([onehot.astype(BF16)] * 3, axis=1), jnp.concatenate(_split3(table), axis=0))


def _layernorm(x, g, b):
    mu = jnp.mean(x, axis=-1, keepdims=True)
    xc = x - mu
    var = jnp.mean(xc * xc, axis=-1, keepdims=True)
    return xc * lax.rsqrt(var + LN_EPS) * g + b


def _two_phase_specs(block, n_first):
    nd = len(block)
    first = pl.BlockSpec(block, lambda i: (jnp.minimum(i, n_first - 1),) + (0,) * (nd - 1))
    second = pl.BlockSpec(block, lambda i: (jnp.maximum(i - n_first, 0),) + (0,) * (nd - 1))
    return [first, second]


def _two_phase_lane_block_specs(tm, n_first):
    nblk = D_S5 // LANES
    first = pl.BlockSpec((nblk, tm, LANES), lambda i: (0, jnp.minimum(i, n_first - 1), 0))
    second = pl.BlockSpec((nblk, tm, LANES), lambda i: (0, jnp.maximum(i - n_first, 0), 0))
    return [first, second]


def _read_lane_blocks(ref):
    return jnp.concatenate([ref[j] for j in range(D_S5 // LANES)], axis=1)


def _prompt_spec(tm, seq, width, n_first):
    per_seq = seq // tm

    def index(i):
        ic = jnp.minimum(i, n_first - 1)
        return (ic // per_seq, ic % per_seq, 0)

    return pl.BlockSpec((1, tm, width), index)


def _chunk(rows, j):
    return pl.ds(j, rows, stride=ROW_TILES)


def _read_row_tiles(ref, rows):
    return jnp.concatenate([ref[_chunk(rows, j), :] for j in range(ROW_TILES)], axis=1)


def _write_row_tiles(ref, val, rows):
    for j in range(ROW_TILES):
        ref[_chunk(rows, j), :] = val[:, j * LANES:(j + 1) * LANES]


def _row_tiles(r, n=1):
    return pl.ds(pl.multiple_of(r * ROW_TILES, ROW_TILES), n * ROW_TILES)


def _row_tile(r):
    return _row_tiles(r)


def _ln_in_proj_kernel(xp_ref, xs_ref, g_ref, b_ref, w_ref, h_ref, up_ref, us_ref, fp_ref, fs_ref, zp_ref, zs_ref,
                       *, n_first):
    def phase(x, u_ref, f_ref, z_ref):
        h = _layernorm(x, g_ref[...], b_ref[...])
        h_ref[...] = h
        hb = h.astype(BF16)
        u = _dot(hb, w_ref[:, :D_S5])
        for j in range(D_S5 // LANES):
            u_ref[j] = u[:, j * LANES:(j + 1) * LANES]
        f_ref[...] = _dot(hb, w_ref[:, D_S5:D_S5 + D_HG])
        z_ref[...] = _dot(hb, w_ref[:, D_S5 + D_HG:]).astype(BF16)

    i = pl.program_id(0)
    pl.when(i < n_first)(lambda: phase(xp_ref[0], up_ref, fp_ref, zp_ref))
    pl.when(i >= n_first)(lambda: phase(xs_ref[...], us_ref, fs_ref, zs_ref))


def ln_in_proj(xp, xs, g, b, w_bf16, tm):
    nb, seq, _ = xp.shape
    tp, ts = nb * seq, xs.shape[0]
    n_first = tp // tm
    fixed = lambda i: (0, 0)
    return pl.pallas_call(
        functools.partial(_ln_in_proj_kernel, n_first=n_first),
        grid=((tp + ts) // tm,),
        in_specs=[_prompt_spec(tm, seq, D_MODEL, n_first), _two_phase_specs((tm, D_MODEL), n_first)[1]]
                 + [pl.BlockSpec((1, D_MODEL), fixed), pl.BlockSpec((1, D_MODEL), fixed),
                    pl.BlockSpec((D_MODEL, D_IN), fixed)],
        out_specs=[pl.BlockSpec((tm, D_MODEL), lambda i: (i, 0))]
                  + _two_phase_lane_block_specs(tm, n_first)
                  + _two_phase_specs((tm, D_HG), n_first)
                  + _two_phase_specs((tm, 3 * D_HG), n_first),
        out_shape=[jax.ShapeDtypeStruct((tp + ts, D_MODEL), F32),
                   jax.ShapeDtypeStruct((D_S5 // LANES, tp, LANES), F32),
                   jax.ShapeDtypeStruct((D_S5 // LANES, ts, LANES), F32),
                   jax.ShapeDtypeStruct((tp, D_HG), F32), jax.ShapeDtypeStruct((ts, D_HG), F32),
                   jax.ShapeDtypeStruct((tp, 3 * D_HG), BF16), jax.ShapeDtypeStruct((ts, 3 * D_HG), BF16)],
        compiler_params=_cparams(),
        name="ln_in_proj",
    )(xp, xs, g, b, w_bf16)


def _lane_block(j):
    return slice(j * LANES, (j + 1) * LANES)


def _granule_transpose(slabs):
    per_block = LANES // S5_GROUP
    granule = lax.broadcasted_iota(jnp.int32, (1, LANES), 1) // S5_GROUP
    x = list(slabs)
    for d in (4, 2, 1):
        keep = (granule & d) == 0
        y = [None] * per_block
        for i in range(per_block):
            if i & d == 0:
                y[i] = jnp.where(keep, x[i], pltpu.roll(x[i + d], d * S5_GROUP, axis=1))
                y[i + d] = jnp.where(keep, pltpu.roll(x[i], LANES - d * S5_GROUP, axis=1), x[i + d])
        x = y
    return x


def _sublane_transpose(tiles):
    sub = lax.broadcasted_iota(jnp.int32, (SUBLANES, 1), 0)
    x = list(tiles)
    for d in (4, 2, 1):
        keep = (sub & d) == 0
        y = [None] * SUBLANES
        for i in range(SUBLANES):
            if i & d == 0:
                y[i] = jnp.where(keep, x[i], pltpu.roll(x[i + d], d, axis=0))
                y[i + d] = jnp.where(keep, pltpu.roll(x[i], SUBLANES - d, axis=0), x[i + d])
        x = y
    return x


def _s5_rows_kernel(u_ref, o_ref, *, chunks):
    per_block = LANES // S5_GROUP
    for gcol in range(D_S5 // LANES):
        for j in range(S5_CONV // LANES):
            groups = [_sublane_transpose([u_ref[gcol, pl.ds((cg * SUBLANES + c) * CHUNK + per_block * j, SUBLANES), :]
                                          for c in range(SUBLANES)]) for cg in range(chunks // SUBLANES)]
            by_time = [jnp.concatenate([grp[sl] for grp in groups], axis=0) for sl in range(per_block)]
            for gl, rows in enumerate(_granule_transpose(by_time)):
                o_ref[gcol * per_block + gl, :, _lane_block(j)] = rows.astype(BF16)


def s5_rows(u, chunks):
    nblk, t, _ = u.shape
    assert chunks % SUBLANES == 0
    r = t // CHUNK
    return pl.pallas_call(
        functools.partial(_s5_rows_kernel, chunks=chunks),
        grid=(r // chunks,),
        in_specs=[pl.BlockSpec((nblk, chunks * CHUNK, LANES), lambda i: (0, i, 0))],
        out_specs=pl.BlockSpec((S5_GROUPS, chunks, S5_CONV), lambda i: (0, i, 0)),
        out_shape=jax.ShapeDtypeStruct((S5_GROUPS, r, S5_CONV), BF16),
        compiler_params=_cparams(),
        name="s5_rows",
    )(u)


def _s5_tokens_kernel(y_ref, o_ref, *, chunks):
    per_block = LANES // S5_GROUP
    for gcol in range(D_S5 // LANES):
        for j in range(S5_CONV // LANES):
            by_group = [y_ref[gcol * per_block + gl, :, _lane_block(j)].astype(F32) for gl in range(per_block)]
            by_time = _granule_transpose(by_group)
            for cg in range(chunks // SUBLANES):
                tiles = _sublane_transpose([rows[cg * SUBLANES:(cg + 1) * SUBLANES, :] for rows in by_time])
                for c in range(SUBLANES):
                    o_ref[gcol, pl.ds((cg * SUBLANES + c) * CHUNK + per_block * j, SUBLANES), :] = tiles[c]


def s5_tokens(y_rows, chunks):
    _, r, _ = y_rows.shape
    assert chunks % SUBLANES == 0
    return pl.pallas_call(
        functools.partial(_s5_tokens_kernel, chunks=chunks),
        grid=(r // chunks,),
        in_specs=[pl.BlockSpec((S5_GROUPS, chunks, S5_CONV), lambda i: (0, i, 0))],
        out_specs=pl.BlockSpec((D_S5 // LANES, chunks * CHUNK, LANES), lambda i: (0, i, 0)),
        out_shape=jax.ShapeDtypeStruct((D_S5 // LANES, r * CHUNK, LANES), F32),
        compiler_params=_cparams(),
        name="s5_tokens",
    )(y_rows)


def _s5_prep_kernel(lrc_ref, lic_ref, lrr_ref, lir_ref, ls_ref, brt_ref, bit_ref, ctr_ref, cti_ref,
                    m_ref, w_ref, wc_ref, a_ref):
    step = jnp.exp(ls_ref[0])

    def discretise(lr_raw, li):
        lr = jnp.minimum(lr_raw, -1e-4)
        dr, di = lr * step, li * step
        mag = jnp.exp(dr)
        a_re, a_im = mag * jnp.cos(di), mag * jnp.sin(di)
        den = lr * lr + li * li
        nr = a_re - 1.0
        fr = (nr * lr + a_im * li) / den
        fi = (a_im * lr - nr * li) / den
        return dr, di, fr, fi

    dr_c, di_c, _, _ = discretise(lrc_ref[0], lic_ref[0])
    dr_r, di_r, fr_r, fi_r = discretise(lrr_ref[0], lir_ref[0])

    lane = lax.broadcasted_iota(jnp.int32, (1, S5_CONV), 1)
    t_row = lax.broadcasted_iota(jnp.int32, (1, CHUNK), 1).astype(F32)
    t_col = lax.broadcasted_iota(jnp.int32, (CHUNK, 1), 0).astype(F32)
    lag_of_lane = (lax.broadcasted_iota(jnp.int32, (CHUNK, S5_CONV), 1) // S5_GROUP
                   == lax.broadcasted_iota(jnp.int32, (CHUNK, S5_CONV), 0)).astype(F32)
    time_of_row = (lax.broadcasted_iota(jnp.int32, (S5_CONV, CHUNK), 0) // S5_GROUP
                   == lax.broadcasted_iota(jnp.int32, (S5_CONV, CHUNK), 1)).astype(F32)
    chan_of_lane = (lax.broadcasted_iota(jnp.int32, (S5_GROUP, S5_CONV), 1) % S5_GROUP
                    == lax.broadcasted_iota(jnp.int32, (S5_GROUP, S5_CONV), 0)).astype(F32)
    chan_of_row = (lax.broadcasted_iota(jnp.int32, (S5_CONV, S5_GROUP), 0) % S5_GROUP
                   == lax.broadcasted_iota(jnp.int32, (S5_CONV, S5_GROUP), 1)).astype(F32)
    ctr = _spread_cols(ctr_ref[0], chan_of_lane)
    cti = _spread_cols(cti_ref[0], chan_of_lane)

    def c_times_power(tf):
        mag = jnp.exp(dr_c * tf)
        ang = di_c * tf
        pr = _spread_cols(mag * jnp.cos(ang), lag_of_lane)
        pi = _spread_cols(mag * jnp.sin(ang), lag_of_lane)
        return ctr * pr - cti * pi, ctr * pi + cti * pr

    cpr, cpi = c_times_power(t_row)
    bbr = fr_r * brt_ref[0] - fi_r * bit_ref[0]
    bbi = fr_r * bit_ref[0] + fi_r * brt_ref[0]
    kt = _dot(bbr, cpr, HIGHEST) - _dot(bbi, cpi, HIGHEST)
    for s in range(CHUNK):
        shifted = kt if s == 0 else pltpu.roll(kt, S5_GROUP * s, axis=1)
        m_ref[0, S5_GROUP * s:S5_GROUP * (s + 1), :] = jnp.where(
            lane >= S5_GROUP * s, shifted, 0.0).astype(BF16)

    rem = CHUNK - 1.0 - t_col
    magw = jnp.exp(dr_r * rem)
    angw = di_r * rem
    pwr = _spread_rows(time_of_row, magw * jnp.cos(angw))
    pwi = _spread_rows(time_of_row, magw * jnp.sin(angw))
    bbtr = _spread_rows(chan_of_row, bbr)
    bbti = _spread_rows(chan_of_row, bbi)
    w_ref[0, :, :S5_STATE] = pwr * bbtr - pwi * bbti
    w_ref[0, :, S5_STATE:] = pwr * bbti + pwi * bbtr

    c1r, c1i = c_times_power(t_row + 1.0)
    wc_ref[0, :S5_STATE, :] = c1r.astype(BF16)
    wc_ref[0, S5_STATE:, :] = (-c1i).astype(BF16)

    full = float(CHUNK)
    mag_c = jnp.exp(dr_r * full)
    a_ref[0, 0:1, :] = mag_c * jnp.cos(di_r * full)
    a_ref[0, 1:2, :] = mag_c * jnp.sin(di_r * full)


def s5_prep(lam_re, lam_im, log_step, b_re, b_im, c_re, c_im):
    g, p = lam_re.shape
    brt = jnp.swapaxes(b_re, 1, 2)
    bit = jnp.swapaxes(b_im, 1, 2)
    args = (lam_re.reshape(g, p, 1), lam_im.reshape(g, p, 1),
            lam_re.reshape(g, 1, p), lam_im.reshape(g, 1, p), log_step.reshape(g, 1, 1),
            brt, bit, jnp.swapaxes(c_re, 1, 2), jnp.swapaxes(c_im, 1, 2))
    spec = lambda a: pl.BlockSpec((1,) + a.shape[1:], lambda i: (i, 0, 0))
    out_shape = [jax.ShapeDtypeStruct((g, S5_CONV, S5_CONV), BF16),
                 jax.ShapeDtypeStruct((g, S5_CONV, 2 * S5_STATE), F32),
                 jax.ShapeDtypeStruct((g, 2 * S5_STATE, S5_CONV), BF16),
                 jax.ShapeDtypeStruct((g, 2, S5_STATE), F32)]
    return pl.pallas_call(
        _s5_prep_kernel,
        grid=(g,),
        in_specs=[spec(a) for a in args],
        out_specs=[spec(o) for o in out_shape],
        out_shape=out_shape,
        compiler_params=_cparams(),
        name="s5_prep",
    )(*args)


def _s5_main_kernel(up_ref, us_ref, m_ref, w_ref, wc_ref, a_ref, xsr_ref, xsi_ref,
                    yp_ref, ys_ref, fpr_ref, fpi_ref, fsr_ref, fsi_ref, fin_scr, *, n_prompt, n_chunks):
    w3 = jnp.concatenate(_split3(w_ref[0]), axis=1)
    ar = a_ref[0, 0:1, :]
    ai = a_ref[0, 1:2, :]
    width = 2 * S5_STATE

    def local(u):
        h3 = _dot(u, w3)
        return _dot(u, m_ref[0]), h3[:, :width] + h3[:, width:2 * width] + h3[:, 2 * width:]

    def times(pr, pi, x):
        return (jnp.concatenate([pr, pr], axis=1) * x
                + jnp.concatenate([-pi, pi], axis=1) * pltpu.roll(x, S5_STATE, axis=1))

    y_local, x = local(up_ref[0])
    chunk_of_row = lax.broadcasted_iota(jnp.int32, (n_prompt * n_chunks, 1), 0) % n_chunks
    pr, pi = ar, ai
    d = 1
    while d < n_chunks:
        x = x + times(pr, pi, jnp.where(chunk_of_row >= d, pltpu.roll(x, d, axis=0), 0.0))
        pr, pi = pr * pr - pi * pi, 2.0 * pr * pi
        d *= 2
    fin_scr[...] = x
    last = fin_scr[pl.ds(n_chunks - 1, n_prompt, stride=n_chunks), :]
    fpr_ref[0] = last[:, :S5_STATE]
    fpi_ref[0] = last[:, S5_STATE:]
    x0 = jnp.where(chunk_of_row >= 1, pltpu.roll(x, 1, axis=0), 0.0)
    yp_ref[0] = (y_local + _dot(x0.astype(BF16), wc_ref[0])).astype(BF16)

    y_local, hend = local(us_ref[0])
    x0 = jnp.concatenate([xsr_ref[0], xsi_ref[0]], axis=1)
    fin = times(ar, ai, x0) + hend
    fsr_ref[0] = fin[:, :S5_STATE]
    fsi_ref[0] = fin[:, S5_STATE:]
    ys_ref[0] = (y_local + _dot(x0.astype(BF16), wc_ref[0])).astype(BF16)


def s5_main(up_rows, us_rows, m, w, wc, a, xs_re, xs_im, n_prompt, n_chunks):
    g, r, _ = up_rows.shape
    n_sample = xs_re.shape[1]
    spec = lambda shape: pl.BlockSpec((1,) + tuple(shape[1:]), lambda i: (i, 0, 0))
    args = (up_rows, us_rows, m, w, wc, a, xs_re, xs_im)
    out_shape = [jax.ShapeDtypeStruct((g, r, S5_CONV), BF16),
                 jax.ShapeDtypeStruct((g, n_sample, S5_CONV), BF16),
                 jax.ShapeDtypeStruct((g, n_prompt, S5_STATE), F32),
                 jax.ShapeDtypeStruct((g, n_prompt, S5_STATE), F32),
                 jax.ShapeDtypeStruct((g, n_sample, S5_STATE), F32),
                 jax.ShapeDtypeStruct((g, n_sample, S5_STATE), F32)]
    return pl.pallas_call(
        functools.partial(_s5_main_kernel, n_prompt=n_prompt, n_chunks=n_chunks),
        grid=(g,),
        in_specs=[spec(x.shape) for x in args],
        out_specs=[spec(o.shape) for o in out_shape],
        out_shape=out_shape,
        scratch_shapes=[pltpu.VMEM((r, 2 * S5_STATE), F32)],
        compiler_params=_cparams(),
        name="s5_main",
    )(*args)


def _hgrn_kernel(f_ref, z_ref, s0_ref, lb_ref, ng_ref, o_ref, sfin_ref, st_scr, *, n_seq, chunk):
    c = pl.program_id(1)

    @pl.when(c == 0)
    def _():
        st_scr[...] = s0_ref[...]

    lbw = lb_ref[...]
    lbe = jnp.exp(lbw - jnp.max(lbw, axis=0, keepdims=True))
    lb_all = lbe[0:1, :] / jnp.sum(lbe, axis=0, keepdims=True)

    levels = [chunk >> (i + 1) for i in range(chunk.bit_length() - 1)]
    rowi = lax.broadcasted_iota(jnp.int32, (chunk, chunk), 0)
    coli = lax.broadcasted_iota(jnp.int32, (chunk, chunk), 1)
    rowk = lax.broadcasted_iota(jnp.int32, (chunk, HG_D), 0)
    sign, valid = [], []
    for m in levels:
        sign.append(jnp.where((rowk % (2 * m)) >= m, 1.0, -1.0))
        valid.append(((rowi // (2 * m)) == (coli // (2 * m)))
                     & ((rowi % (2 * m)) >= m) & ((coli % (2 * m)) < m))
    cum_mat = (coli <= rowi).astype(BF16)
    cum_mat3 = jnp.concatenate([cum_mat] * 3, axis=1)
    diag = rowi == coli
    nt = (((1,), (1,)), ((), ()))

    def body(n, carry):
        zf = f_ref[n]
        fg_all = lb_all + (1.0 - lb_all) * jax.nn.sigmoid(zf)
        cums = _dot(cum_mat3, jnp.concatenate(_split3(jnp.log2(fg_all)), axis=0))
        for hd in range(HG_HEADS):
            cols = slice(hd * HG_D, (hd + 1) * HG_D)
            zq = z_ref[n, :, hd * HG_D:(hd + 1) * HG_D].astype(F32)
            vb = z_ref[n, :, D_HG + hd * HG_D:D_HG + (hd + 1) * HG_D]
            zg = z_ref[n, :, 2 * D_HG + hd * HG_D:2 * D_HG + (hd + 1) * HG_D].astype(F32)
            q = zq * jax.nn.sigmoid(zq)
            kk = 1.0 - fg_all[:, cols]
            bcum = cums[:chunk, cols]
            b_last = bcum[chunk - 1:chunk, :]
            qb = q.astype(BF16)
            kb = kk.astype(BF16)
            st = st_scr[n, hd]

            scores = jnp.where(diag, lax.dot_general(qb, kb, nt, preferred_element_type=F32), 0.0)
            for lvl, m in enumerate(levels):
                if 2 * m >= SUBLANES:
                    bref = jnp.concatenate(
                        [jnp.broadcast_to(bcum[b * 2 * m + m - 1:b * 2 * m + m, :], (2 * m, HG_D))
                         for b in range(chunk // (2 * m))], axis=0)
                else:
                    offs = rowk % (2 * m) - (m - 1)
                    bref = bcum
                    for o in range(-(m - 1), m + 1):
                        if o != 0:
                            bref = jnp.where(offs == o, pltpu.roll(bcum, o % chunk, axis=0), bref)
                dec = jnp.exp2((bcum - bref) * sign[lvl]).astype(BF16)
                sc = lax.dot_general(qb * dec, kb * dec, nt, preferred_element_type=F32)
                scores = jnp.where(valid[lvl], sc, scores)

            qd = (q * jnp.exp2(bcum)).astype(BF16)
            o = lax.dot_general(qd, st.astype(BF16), nt, preferred_element_type=F32)
            o = o + _dot(scores.astype(BF16), vb)
            kdec = (kk * jnp.exp2(b_last - bcum)).astype(BF16)
            st_scr[n, hd] = jnp.exp2(b_last) * st + lax.dot_general(
                vb, kdec, (((0,), (0,)), ((), ())), preferred_element_type=F32)

            on = o * lax.rsqrt(jnp.mean(o * o, axis=-1, keepdims=True) + RMS_EPS) * ng_ref[:, cols]
            o_ref[n, :, hd * HG_D:(hd + 1) * HG_D] = on * (zg * jax.nn.sigmoid(zg))
        return carry

    lax.fori_loop(0, n_seq, body, 0, unroll=True)

    @pl.when(c == pl.num_programs(1) - 1)
    def _():
        sfin_ref[...] = st_scr[...]


def hgrn(f, z, s0_t, hg_lb, norm_g, n_seq, chunk):
    n, length, _ = z.shape
    return pl.pallas_call(
        functools.partial(_hgrn_kernel, n_seq=n_seq, chunk=chunk),
        grid=(n // n_seq, length // chunk),
        in_specs=[pl.BlockSpec((n_seq, chunk, D_HG), lambda g, c: (g, c, 0)),
                  pl.BlockSpec((n_seq, chunk, 3 * D_HG), lambda g, c: (g, c, 0)),
                  pl.BlockSpec((n_seq, HG_HEADS, HG_D, HG_D), lambda g, c: (g, 0, 0, 0)),
                  pl.BlockSpec(hg_lb.shape, lambda g, c: (0, 0)),
                  pl.BlockSpec((1, D_HG), lambda g, c: (0, 0))],
        out_specs=[pl.BlockSpec((n_seq, chunk, D_HG), lambda g, c: (g, c, 0)),
                   pl.BlockSpec((n_seq, HG_HEADS, HG_D, HG_D), lambda g, c: (g, 0, 0, 0))],
        out_shape=[jax.ShapeDtypeStruct((n, length, D_HG), F32),
                   jax.ShapeDtypeStruct((n, HG_HEADS, HG_D, HG_D), F32)],
        scratch_shapes=[pltpu.VMEM((n_seq, HG_HEADS, HG_D, HG_D), F32)],
        compiler_params=_cparams(2),
        name="hgrn",
    )(f, z, s0_t, hg_lb, norm_g)


def _post_mix_kernel(h_ref, up_ref, us_ref, yp_ref, ys_ref, hgp_ref, hgs_ref,
                     d_ref, wglu_ref, bglu_ref, s5g_ref, wout_ref, g1_ref, b1_ref, rwt_ref, rb_ref,
                     h1_ref, slot_ref, gate_ref, before_ref, cnt_ref, run_scr, *, tm, n_first):
    i = pl.program_id(0)

    @pl.when(i == 0)
    def _():
        run_scr[...] = jnp.zeros_like(run_scr)

    def phase(u_ref, y_ref, hg_ref):
        ys = _read_lane_blocks(y_ref) + d_ref[...] * _read_lane_blocks(u_ref)
        gl = 0.5 * ys * (1.0 + lax.erf(ys * (2.0 ** -0.5)))
        s5o = gl * jax.nn.sigmoid(_dot(gl.astype(BF16), wglu_ref[...]) + bglu_ref[...])
        s5o = s5o * lax.rsqrt(jnp.mean(s5o * s5o, axis=-1, keepdims=True) + RMS_EPS) * s5g_ref[...]
        mix = (_dot(s5o.astype(BF16), wout_ref[:D_S5, :])
               + _dot(hg_ref[...].astype(BF16), wout_ref[D_S5:, :]))
        h1 = _layernorm(DEEPNORM_ALPHA * h_ref[...] + mix, g1_ref[...], b1_ref[...])
        _write_row_tiles(h1_ref, h1, tm)

        h_hi, h_mid, _ = _split3(h1)
        w_hi, w_mid, _ = _split3(rwt_ref[...])
        nt = (((1,), (1,)), ((), ()))
        logits = (lax.dot_general(w_hi, h_hi, nt, preferred_element_type=F32)
                  + lax.dot_general(w_hi, h_mid, nt, preferred_element_type=F32)
                  + lax.dot_general(w_mid, h_hi, nt, preferred_element_type=F32)) + rb_ref[...]
        eid = lax.broadcasted_iota(jnp.int32, (N_EXPERTS, tm), 0)
        vals, idxs = [], []
        for _ in range(TOP_K):
            m = jnp.max(logits, axis=0, keepdims=True)
            ix = jnp.min(jnp.where(logits == m, eid, N_EXPERTS), axis=0, keepdims=True)
            vals.append(m)
            idxs.append(ix)
            logits = jnp.where(eid == ix, -jnp.inf, logits)
        exps = [jnp.exp(v - vals[0]) for v in vals]
        den = exps[0] + exps[1] + exps[2] + exps[3]

        onehot = jnp.zeros((N_EXPERTS, tm), F32)
        for ix in idxs:
            onehot = onehot + (eid == ix).astype(F32)
        rowi = lax.broadcasted_iota(jnp.int32, (tm, tm), 0)
        coli = lax.broadcasted_iota(jnp.int32, (tm, tm), 1)
        earlier = (rowi < coli).astype(BF16)
        prefix = _dot(onehot.astype(BF16), earlier)
        tile_cnt = jnp.sum(onehot, axis=1, keepdims=True)
        for k in range(TOP_K):
            lower_experts = jnp.sum(jnp.where(eid < idxs[k], tile_cnt, 0.0), axis=0, keepdims=True)
            rank = jnp.sum(jnp.where(eid == idxs[k], prefix, 0.0), axis=0, keepdims=True)
            slot_ref[0, :, k * tm:(k + 1) * tm] = (lower_experts + rank).astype(jnp.int32)
            gate_ref[0, :, k * tm:(k + 1) * tm] = exps[k] / den
        before_ref[0] = run_scr[...]
        run_scr[...] = run_scr[...] + tile_cnt
        cnt_ref[...] = run_scr[...]

    pl.when(i < n_first)(lambda: phase(up_ref, yp_ref, hgp_ref))
    pl.when(i >= n_first)(lambda: phase(us_ref, ys_ref, hgs_ref))


def post_mix(h0, u_pair, y_pair, hg_pair, d_skip, wglu, bglu, s5g, wout, g1, b1, rw_t, rb_col, tm):
    t = h0.shape[0]
    n_first = u_pair[0].shape[1] // tm
    row = lambda i: (i, 0)
    fixed = lambda i: (0, 0)
    full = lambda a: pl.BlockSpec(a.shape, fixed)
    weights = (d_skip, wglu, bglu, s5g, wout, g1, b1, rw_t, rb_col)
    return pl.pallas_call(
        functools.partial(_post_mix_kernel, tm=tm, n_first=n_first),
        grid=(t // tm,),
        in_specs=[pl.BlockSpec((tm, D_MODEL), row)]
                 + _two_phase_lane_block_specs(tm, n_first) * 2 + _two_phase_specs((tm, D_HG), n_first)
                 + [full(a) for a in weights],
        out_specs=[pl.BlockSpec((tm * ROW_TILES, LANES), row),
                   pl.BlockSpec((1, 1, TOP_K * tm), lambda i: (i, 0, 0)),
                   pl.BlockSpec((1, 1, TOP_K * tm), lambda i: (i, 0, 0)),
                   pl.BlockSpec((1, N_EXPERTS, 1), lambda i: (i, 0, 0)),
                   pl.BlockSpec((N_EXPERTS, 1), fixed)],
        out_shape=[jax.ShapeDtypeStruct((t * ROW_TILES, LANES), F32),
                   jax.ShapeDtypeStruct((t // tm, 1, TOP_K * tm), jnp.int32),
                   jax.ShapeDtypeStruct((t // tm, 1, TOP_K * tm), F32),
                   jax.ShapeDtypeStruct((t // tm, N_EXPERTS, 1), F32),
                   jax.ShapeDtypeStruct((N_EXPERTS, 1), F32)],
        scratch_shapes=[pltpu.VMEM((N_EXPERTS, 1), F32)],
        compiler_params=_cparams(),
        name="post_mix",
    )(h0, *u_pair, *y_pair, *hg_pair, *weights)


def _segment_copies(meta_ref, tile, tm, make_copy):
    for e in range(N_EXPERTS):
        sorted_row = meta_ref[tile, e]
        cnt = meta_ref[tile, N_EXPERTS + e]
        staged_row = meta_ref[tile, 2 * N_EXPERTS + e]
        for b in range(tm.bit_length()):
            done = cnt & ((1 << b) - 1)

            @pl.when(((cnt >> b) & 1) == 1)
            def _(b=b, done=done, e=e):
                make_copy(staged_row + done, sorted_row + done, 1 << b).start(priority=e % 2)


def _dispatch_kernel(meta_ref, pend_ref, slot_ref, h_ref, xs_ref, stage, zero_scr, zsem, sem, *, tm):
    n_rows = xs_ref.shape[0] // ROW_TILES
    i = pl.program_id(0)
    n = pl.num_programs(0)
    slot = i % 2

    def drain(s):
        pltpu.make_async_copy(stage.at[s], xs_ref.at[pl.ds(0, TOP_K * tm * ROW_TILES)], sem.at[s]).wait()

    @pl.when(i == 0)
    def _():
        zero_scr[...] = jnp.zeros_like(zero_scr)

        def last_block(e):
            prev = pend_ref[e - 1] if e > 0 else 0
            copy = pltpu.make_async_copy(
                zero_scr, xs_ref.at[pl.ds(pl.multiple_of(jnp.maximum(pend_ref[e] - MOE_ROWS, 0) * ROW_TILES,
                                                         ROW_TILES), MOE_ROWS * ROW_TILES)], zsem)
            return pend_ref[e] > prev, copy

        def tail_block(j):
            row0 = pend_ref[N_EXPERTS - 1] + j * MOE_ROWS
            copy = pltpu.make_async_copy(
                zero_scr, xs_ref.at[pl.ds(pl.multiple_of(jnp.minimum(row0, n_rows - MOE_ROWS) * ROW_TILES,
                                                         ROW_TILES), MOE_ROWS * ROW_TILES)], zsem)
            return row0 < n_rows, copy

        blocks = [last_block(e) for e in range(N_EXPERTS)] + [tail_block(j) for j in range(N_EXPERTS)]
        for used, copy in blocks:
            pl.when(used)(copy.start)
        for used, copy in blocks:
            pl.when(used)(copy.wait)

    pl.when(i >= 2)(lambda: drain(slot))

    def body(t, carry):
        row = h_ref[_row_tile(t), :]
        for k in range(TOP_K):
            stage[slot, _row_tile(slot_ref[k * tm + t]), :] = row
        return carry

    lax.fori_loop(0, tm, body, 0, unroll=8)
    _segment_copies(meta_ref, i, tm, lambda staged_row, sorted_row, rows: pltpu.make_async_copy(
        stage.at[slot, _row_tiles(staged_row, rows)], xs_ref.at[_row_tiles(sorted_row, rows)], sem.at[slot]))

    @pl.when(i == n - 1)
    def _():
        pl.when(n >= 2)(lambda: drain(1 - slot))
        drain(slot)


def dispatch(meta, pend, slots_flat, h1, n_rows, tm):
    t = h1.shape[0] // ROW_TILES
    grid_spec = pltpu.PrefetchScalarGridSpec(
        num_scalar_prefetch=2,
        grid=(t // tm,),
        in_specs=[pl.BlockSpec((TOP_K * tm,), lambda i, meta, pend: (i,), memory_space=pltpu.SMEM),
                  pl.BlockSpec((tm * ROW_TILES, LANES), lambda i, meta, pend: (i, 0))],
        out_specs=pl.BlockSpec(memory_space=pl.ANY),
        scratch_shapes=[pltpu.VMEM((2, TOP_K * tm * ROW_TILES, LANES), F32),
                        pltpu.VMEM((MOE_ROWS * ROW_TILES, LANES), F32),
                        pltpu.SemaphoreType.DMA(()), pltpu.SemaphoreType.DMA((2,))],
    )
    return pl.pallas_call(
        functools.partial(_dispatch_kernel, tm=tm),
        grid_spec=grid_spec,
        out_shape=jax.ShapeDtypeStruct((n_rows * ROW_TILES, LANES), F32),
        compiler_params=_cparams(),
        name="moe_dispatch",
    )(meta, pend, slots_flat, h1)


def _moe_ffn_kernel(be_ref, nu_ref, seg_ref, nxt_ref, x_ref, wg_ref, bg_ref, wu_ref, bu_ref, wd_ref, bd_ref,
                    y_ref, wbuf, wbf, sem):
    i = pl.program_id(0)
    hbm = (wg_ref, wu_ref, wd_ref)

    def weight_copies(expert, s):
        return [pltpu.make_async_copy(hbm[j].at[expert], wbuf.at[s, j], sem.at[s, j]) for j in range(3)]

    @pl.when((i == 0) | (be_ref[i] != be_ref[jnp.maximum(i - 1, 0)]))
    def _():
        s = seg_ref[i] % 2

        @pl.when(i == 0)
        def _():
            for c in weight_copies(be_ref[0], 0):
                c.start()

        for j, c in enumerate(weight_copies(be_ref[i], s)):
            c.wait()
            wbf[j] = wbuf[s, j].astype(BF16)

        @pl.when(nxt_ref[i] >= 0)
        def _():
            for c in weight_copies(nxt_ref[i], 1 - s):
                c.start()

    @pl.when(i < nu_ref[0])
    def _():
        x = _read_row_tiles(x_ref, MOE_ROWS).astype(BF16)
        gt = jnp.minimum(_dot(x, wbf[0]) + bg_ref[0], SWIGLU_LIMIT)
        up = jnp.clip(_dot(x, wbf[1]) + bu_ref[0], -SWIGLU_LIMIT, SWIGLU_LIMIT)
        hid = (up + 1.0) * (gt * jax.nn.sigmoid(SWIGLU_ALPHA * gt))
        _write_row_tiles(y_ref, _dot(hid.astype(BF16), wbf[2]) + bd_ref[0], MOE_ROWS)

    @pl.when(i >= nu_ref[0])
    def _():
        y_ref[...] = jnp.zeros_like(y_ref)


def moe_ffn(block_e, n_used, segment, next_e, xs, wg, bg, wu, bu, wd, bd):
    n_rows = xs.shape[0] // ROW_TILES
    n_blocks = n_rows // MOE_ROWS
    wsel = lambda i, be, nu, seg, nxt: (be[i], 0, 0)
    d_ff = wg.shape[-1]
    assert wg.shape[1:] == wu.shape[1:] == wd.shape[1:] == (D_MODEL, D_MODEL)
    anywhere = pl.BlockSpec(memory_space=pl.ANY)
    grid_spec = pltpu.PrefetchScalarGridSpec(
        num_scalar_prefetch=4,
        grid=(n_blocks,),
        in_specs=[pl.BlockSpec((MOE_ROWS * ROW_TILES, LANES),
                               lambda i, be, nu, seg, nxt: (jnp.minimum(i, nu[0] - 1), 0)),
                  anywhere, pl.BlockSpec((1, 1, d_ff), wsel),
                  anywhere, pl.BlockSpec((1, 1, d_ff), wsel),
                  anywhere, pl.BlockSpec((1, 1, D_MODEL), wsel)],
        out_specs=pl.BlockSpec((MOE_ROWS * ROW_TILES, LANES), lambda i, be, nu, seg, nxt: (i, 0)),
        scratch_shapes=[pltpu.VMEM((2, 3, D_MODEL, D_MODEL), F32), pltpu.VMEM((3, D_MODEL, D_MODEL), BF16),
                        pltpu.SemaphoreType.DMA((2, 3))],
    )
    return pl.pallas_call(
        _moe_ffn_kernel,
        grid_spec=grid_spec,
        out_shape=jax.ShapeDtypeStruct((n_rows * ROW_TILES, LANES), F32),
        compiler_params=_cparams(),
        name="moe_ffn",
    )(block_e, n_used, segment, next_e, xs, wg, bg, wu, bu, wd, bd)


def _combine_kernel(meta_ref, slot_ref, gate_ref, h_ref, pp_ref, ps_ref, yb_ref,
                    plew_ref, plegw_ref, g2_ref, b2_ref, outp_ref, outs_ref, buf, r_scr, sem,
                    *, tm, n_first):
    i = pl.program_id(0)
    n = pl.num_programs(0)
    slot = i % 2

    def fetch(tile, s):
        _segment_copies(meta_ref, tile, tm, lambda staged_row, sorted_row, rows: pltpu.make_async_copy(
            yb_ref.at[_row_tiles(sorted_row, rows)], buf.at[s, _row_tiles(staged_row, rows)], sem.at[s]))

    pl.when(i == 0)(lambda: fetch(0, 0))
    pl.when(i + 1 < n)(lambda: fetch(i + 1, 1 - slot))
    pltpu.make_async_copy(yb_ref.at[pl.ds(0, TOP_K * tm * ROW_TILES)], buf.at[slot], sem.at[slot]).wait()

    def body(t, carry):
        acc = DEEPNORM_ALPHA * h_ref[_row_tile(t), :]
        for k in range(TOP_K):
            acc = acc + gate_ref[k * tm + t] * buf[slot, _row_tile(slot_ref[k * tm + t]), :]
        r_scr[_row_tile(t), :] = acc
        return carry

    lax.fori_loop(0, tm, body, 0, unroll=8)
    r = _read_row_tiles(r_scr, tm)
    gate = jax.nn.sigmoid(_dot(r.astype(BF16), plegw_ref[...]))

    def finish(p, store):
        e = _dot(p.astype(BF16), plew_ref[...]) * gate
        store(_layernorm(r + e, g2_ref[...], b2_ref[...]))

    def store_prompt(v):
        outp_ref[0] = v

    def store_sample(v):
        outs_ref[...] = v

    pl.when(i < n_first)(lambda: finish(pp_ref[0], store_prompt))
    pl.when(i >= n_first)(lambda: finish(ps_ref[...], store_sample))


def combine(meta, slots_flat, gates_flat, h1, p_prompt, p_sample, yb, plew, plegw, g2, b2, tm):
    t = h1.shape[0] // ROW_TILES
    nb, seq, _ = p_prompt.shape
    ts = p_sample.shape[0]
    n_first = nb * seq // tm
    fixed = lambda i, meta: (0, 0)
    flat = pl.BlockSpec((TOP_K * tm,), lambda i, meta: (i,), memory_space=pltpu.SMEM)
    sample = lambda width: pl.BlockSpec((tm, width), lambda i, meta: (jnp.maximum(i - n_first, 0), 0))
    prompt = lambda width: pl.BlockSpec((1, tm, width), _prompt_spec(tm, seq, width, n_first).index_map)
    with_meta = lambda spec: pl.BlockSpec(spec.block_shape, lambda i, meta: spec.index_map(i))
    grid_spec = pltpu.PrefetchScalarGridSpec(
        num_scalar_prefetch=1,
        grid=(t // tm,),
        in_specs=[flat, flat,
                  pl.BlockSpec((tm * ROW_TILES, LANES), lambda i, meta: (i, 0)),
                  with_meta(prompt(PLE_DIM)), sample(PLE_DIM),
                  pl.BlockSpec(memory_space=pl.ANY),
                  pl.BlockSpec(plew.shape, fixed), pl.BlockSpec(plegw.shape, fixed),
                  pl.BlockSpec((1, D_MODEL), fixed), pl.BlockSpec((1, D_MODEL), fixed)],
        out_specs=[with_meta(prompt(D_MODEL)), sample(D_MODEL)],
        scratch_shapes=[pltpu.VMEM((2, TOP_K * tm * ROW_TILES, LANES), F32),
                        pltpu.VMEM((tm * ROW_TILES, LANES), F32),
                        pltpu.SemaphoreType.DMA((2,))],
    )
    return pl.pallas_call(
        functools.partial(_combine_kernel, tm=tm, n_first=n_first),
        grid_spec=grid_spec,
        out_shape=[jax.ShapeDtypeStruct((nb, seq, D_MODEL), F32), jax.ShapeDtypeStruct((ts, D_MODEL), F32)],
        compiler_params=_cparams(),
        name="moe_combine",
    )(meta, slots_flat, gates_flat, h1, p_prompt, p_sample, yb, plew, plegw, g2, b2)


def _row(v):
    return v.reshape(1, -1)


def kernel(x_prompt, x_sample, state_s5_re, state_s5_im, state_hgrn, p_prompt, p_sample, ln_in_g, ln_in_b, w_in, s5_lambda_re, s5_lambda_im, s5_log_step, s5_b_re, s5_b_im, s5_c_re, s5_c_im, s5_d, s5_w_glu, s5_b_glu, s5_norm_g, hg_lb, hg_norm_g, w_out, ln1_g, ln1_b, router_w, router_b, w_gate, b_gate, w_up, b_up, w_down, b_down, ple_w, ple_gate_w, ln2_g, ln2_b):
    nb, seq, _ = x_prompt.shape
    ns, dseq, _ = x_sample.shape
    assert dseq == CHUNK and seq % CHUNK == 0 and w_in.shape[0] == 1
    nc = seq // CHUNK
    tp, ts = nb * seq, ns * dseq
    t = tp + ts
    tm = 512 if (tp % 512 == 0 and ts % 512 == 0) else 256
    assert tp % tm == 0 and ts % tm == 0 and seq % tm == 0 and ns % nb == 0

    w_cols = jnp.split(w_in[0], [D_S5, D_S5 + D_HG, D_S5 + 2 * D_HG], axis=1)
    h0, u_p, u_s, f_p, f_s, z_p, z_s = ln_in_proj(
        x_prompt, x_sample.reshape(ts, D_MODEL), _row(ln_in_g), _row(ln_in_b),
        jnp.concatenate([w_cols[0], w_cols[2], w_cols[1], w_cols[3]], axis=1).astype(BF16), tm)

    m, w, wc, a = s5_prep(s5_lambda_re[0], s5_lambda_im[0], s5_log_step[0],
                          s5_b_re[0], s5_b_im[0], s5_c_re[0], s5_c_im[0])
    rb = 64 if (nb * nc) % 64 == 0 else nc
    yp_rows, ys_rows, fpr, fpi, fsr, fsi = s5_main(s5_rows(u_p, rb), s5_rows(u_s, ns), m, w, wc, a,
                                                   jnp.swapaxes(state_s5_re[0], 0, 1),
                                                   jnp.swapaxes(state_s5_im[0], 0, 1), nb, nc)
    y_pair = (s5_tokens(yp_rows, rb), s5_tokens(ys_rows, ns))

    zero_state = jnp.zeros((nb, HG_HEADS, HG_D, HG_D), F32)
    ng = _row(hg_norm_g[0])
    o_p, st_p = hgrn(f_p.reshape(nb, seq, D_HG), z_p.reshape(nb, seq, 3 * D_HG), zero_state, hg_lb, ng, nb,
                     HG_CHUNK if seq % HG_CHUNK == 0 else CHUNK)
    o_s, st_s = hgrn(f_s.reshape(ns, dseq, D_HG), z_s.reshape(ns, dseq, 3 * D_HG), jnp.swapaxes(state_hgrn[0], 2, 3),
                     hg_lb, ng, nb, dseq)

    h1, slots, gates, before, counts = post_mix(
        h0, (u_p, u_s), y_pair, (o_p.reshape(tp, D_HG), o_s.reshape(ts, D_HG)),
        _row(s5_d[0]), s5_w_glu[0].astype(BF16), _row(s5_b_glu[0]),
        _row(s5_norm_g[0]), w_out[0].astype(BF16), _row(ln1_g[0]), _row(ln1_b[0]),
        router_w[0].T, router_b[0].reshape(N_EXPERTS, 1), tm)

    counts = counts[:, 0].astype(jnp.int32)
    before = before[:, :, 0].astype(jnp.int32)
    cnt = jnp.concatenate([before[1:], counts[None]], axis=0) - before
    padded = (counts + MOE_ROWS - 1) // MOE_ROWS * MOE_ROWS
    pend = jnp.cumsum(padded)
    staged = jnp.cumsum(cnt, axis=1) - cnt
    meta = jnp.concatenate([pend - padded + before, cnt, staged, jnp.zeros_like(cnt)], axis=1)
    experts = jnp.arange(N_EXPERTS, dtype=jnp.int32)

    n_blocks = -(-t * TOP_K // MOE_ROWS) + N_EXPERTS
    n_used = (pend[-1] // MOE_ROWS).astype(jnp.int32)
    blk = jnp.arange(n_blocks, dtype=jnp.int32)
    blk = jnp.minimum(blk, n_used - 1)
    block_e = jnp.sum((pend[None, :] <= (blk * MOE_ROWS)[:, None]).astype(jnp.int32), axis=1)
    block_e = jnp.minimum(block_e, N_EXPERTS - 1)
    owns_rows = padded > 0
    owner = owns_rows[None, :]
    segment = jnp.sum((owner & (experts[None, :] <= block_e[:, None])).astype(jnp.int32), axis=1) - 1
    later = jnp.where(owner & (experts[None, :] > block_e[:, None]), experts[None, :], N_EXPERTS)
    next_e = jnp.min(later, axis=1)
    next_e = jnp.where(next_e < N_EXPERTS, next_e, -1).astype(jnp.int32)

    slots_flat = slots.reshape(-1)
    xs = dispatch(meta, pend, slots_flat, h1, n_blocks * MOE_ROWS, tm)
    yb = moe_ffn(block_e, n_used.reshape(1), segment, next_e, xs,
                 w_gate[0], b_gate[0][:, None, :], w_up[0], b_up[0][:, None, :],
                 w_down[0], b_down[0][:, None, :])
    out_p, out_s = combine(meta, slots_flat, gates.reshape(-1), h1, p_prompt[0], p_sample[0].reshape(ts, PLE_DIM),
                           yb, ple_w[0].astype(BF16), ple_gate_w[0].astype(BF16),
                           _row(ln2_g[0]), _row(ln2_b[0]), tm)

    def s5_state(f, n):
        return jnp.swapaxes(f, 0, 1).reshape(1, n, S5_GROUPS, S5_STATE)

    return (out_p, out_s.reshape(ns, dseq, D_MODEL),
            s5_state(fpr, nb), s5_state(fpi, nb), jnp.swapaxes(st_p, 2, 3)[None],
            s5_state(fsr, ns), s5_state(fsi, ns), jnp.swapaxes(st_s, 2, 3)[None])
```

```python
import functools

import jax
import jax.numpy as jnp
from jax import lax
from jax.experimental import pallas as pl
from jax.experimental.pallas import tpu as pltpu

F32 = jnp.float32
BF16 = jnp.bfloat16
HIGHEST = lax.Precision.HIGHEST

D_MODEL = 1024
CHUNK = 64
PLE_DIM = 256
D_S5 = 512
S5_GROUP = 16
S5_GROUPS = 32
S5_STATE = 64
D_HG = 512
HG_HEADS = 4
HG_D = 128
D_IN = D_S5 + 4 * D_HG
N_EXPERTS = 32
TOP_K = 4
SWIGLU_LIMIT = 7.0
SWIGLU_ALPHA = 1.702
DEEPNORM_ALPHA = 2.0 ** 0.25
LN_EPS = 1e-5
RMS_EPS = 1e-6

LANES = 128
SUBLANES = 8
ROW_TILES = D_MODEL // LANES
S5_CONV = CHUNK * S5_GROUP
HG_CHUNK = 128
MOE_ROWS = 512
VMEM_LIMIT = 56 * 1024 * 1024

assert ROW_TILES == SUBLANES


def _cparams(n_axes=1, flags=None):
    return pltpu.CompilerParams(dimension_semantics=("arbitrary",) * n_axes,
                                vmem_limit_bytes=VMEM_LIMIT, flags=flags)


def _dot(a, b, precision=None):
    return jnp.dot(a, b, preferred_element_type=F32, precision=precision)


def _split3(w):
    hi = w.astype(BF16)
    r1 = w - hi.astype(F32)
    mid = r1.astype(BF16)
    lo = (r1 - mid.astype(F32)).astype(BF16)
    return hi, mid, lo


def _spread_cols(table, onehot):
    return _dot(jnp.concatenate(_split3(table), axis=1), jnp.concatenate([onehot.astype(BF16)] * 3, axis=0))


def _spread_rows(onehot, table):
    return _dot(jnp.concatenate([onehot.astype(BF16)] * 3, axis=1), jnp.concatenate(_split3(table), axis=0))


def _layernorm(x, g, b):
    mu = jnp.mean(x, axis=-1, keepdims=True)
    xc = x - mu
    var = jnp.mean(xc * xc, axis=-1, keepdims=True)
    return xc * lax.rsqrt(var + LN_EPS) * g + b


def _two_phase_specs(block, n_first):
    nd = len(block)
    first = pl.BlockSpec(block, lambda i: (jnp.minimum(i, n_first - 1),) + (0,) * (nd - 1))
    second = pl.BlockSpec(block, lambda i: (jnp.maximum(i - n_first, 0),) + (0,) * (nd - 1))
    return [first, second]


def _two_phase_lane_block_specs(tm, n_first):
    nblk = D_S5 // LANES
    first = pl.BlockSpec((nblk, tm, LANES), lambda i: (0, jnp.minimum(i, n_first - 1), 0))
    second = pl.BlockSpec((nblk, tm, LANES), lambda i: (0, jnp.maximum(i - n_first, 0), 0))
    return [first, second]


def _read_lane_blocks(ref):
    return jnp.concatenate([ref[j] for j in range(D_S5 // LANES)], axis=1)


def _prompt_spec(tm, seq, width, n_first):
    per_seq = seq // tm

    def index(i):
        ic = jnp.minimum(i, n_first - 1)
        return (ic // per_seq, ic % per_seq, 0)

    return pl.BlockSpec((1, tm, width), index)


def _chunk(rows, j):
    return pl.ds(j, rows, stride=ROW_TILES)


def _read_row_tiles(ref, rows):
    return jnp.concatenate([ref[_chunk(rows, j), :] for j in range(ROW_TILES)], axis=1)


def _write_row_tiles(ref, val, rows):
    for j in range(ROW_TILES):
        ref[_chunk(rows, j), :] = val[:, j * LANES:(j + 1) * LANES]


def _row_tiles(r, n=1):
    return pl.ds(pl.multiple_of(r * ROW_TILES, ROW_TILES), n * ROW_TILES)


def _row_tile(r):
    return _row_tiles(r)


def _ln_in_proj_kernel(xp_ref, xs_ref, g_ref, b_ref, w_ref, h_ref, up_ref, us_ref, fp_ref, fs_ref, zp_ref, zs_ref,
                       *, n_first):
    def phase(x, u_ref, f_ref, z_ref):
        h = _layernorm(x, g_ref[...], b_ref[...])
        h_ref[...] = h
        hb = h.astype(BF16)
        u = _dot(hb, w_ref[:, :D_S5])
        for j in range(D_S5 // LANES):
            u_ref[j] = u[:, j * LANES:(j + 1) * LANES]
        f_ref[...] = _dot(hb, w_ref[:, D_S5:D_S5 + D_HG])
        z_ref[...] = _dot(hb, w_ref[:, D_S5 + D_HG:]).astype(BF16)

    i = pl.program_id(0)
    pl.when(i < n_first)(lambda: phase(xp_ref[0], up_ref, fp_ref, zp_ref))
    pl.when(i >= n_first)(lambda: phase(xs_ref[...], us_ref, fs_ref, zs_ref))


def ln_in_proj(xp, xs, g, b, w_bf16, tm):
    nb, seq, _ = xp.shape
    tp, ts = nb * seq, xs.shape[0]
    n_first = tp // tm
    fixed = lambda i: (0, 0)
    return pl.pallas_call(
        functools.partial(_ln_in_proj_kernel, n_first=n_first),
        grid=((tp + ts) // tm,),
        in_specs=[_prompt_spec(tm, seq, D_MODEL, n_first), _two_phase_specs((tm, D_MODEL), n_first)[1]]
                 + [pl.BlockSpec((1, D_MODEL), fixed), pl.BlockSpec((1, D_MODEL), fixed),
                    pl.BlockSpec((D_MODEL, D_IN), fixed)],
        out_specs=[pl.BlockSpec((tm, D_MODEL), lambda i: (i, 0))]
                  + _two_phase_lane_block_specs(tm, n_first)
                  + _two_phase_specs((tm, D_HG), n_first)
                  + _two_phase_specs((tm, 3 * D_HG), n_first),
        out_shape=[jax.ShapeDtypeStruct((tp + ts, D_MODEL), F32),
                   jax.ShapeDtypeStruct((D_S5 // LANES, tp, LANES), F32),
                   jax.ShapeDtypeStruct((D_S5 // LANES, ts, LANES), F32),
                   jax.ShapeDtypeStruct((tp, D_HG), F32), jax.ShapeDtypeStruct((ts, D_HG), F32),
                   jax.ShapeDtypeStruct((tp, 3 * D_HG), BF16), jax.ShapeDtypeStruct((ts, 3 * D_HG), BF16)],
        compiler_params=_cparams(),
        name="ln_in_proj",
    )(xp, xs, g, b, w_bf16)


def _lane_block(j):
    return slice(j * LANES, (j + 1) * LANES)


def _granule_transpose(slabs):
    per_block = LANES // S5_GROUP
    granule = lax.broadcasted_iota(jnp.int32, (1, LANES), 1) // S5_GROUP
    x = list(slabs)
    for d in (4, 2, 1):
        keep = (granule & d) == 0
        y = [None] * per_block
        for i in range(per_block):
            if i & d == 0:
                y[i] = jnp.where(keep, x[i], pltpu.roll(x[i + d], d * S5_GROUP, axis=1))
                y[i + d] = jnp.where(keep, pltpu.roll(x[i], LANES - d * S5_GROUP, axis=1), x[i + d])
        x = y
    return x


def _sublane_transpose(tiles):
    sub = lax.broadcasted_iota(jnp.int32, (SUBLANES, 1), 0)
    x = list(tiles)
    for d in (4, 2, 1):
        keep = (sub & d) == 0
        y = [None] * SUBLANES
        for i in range(SUBLANES):
            if i & d == 0:
                y[i] = jnp.where(keep, x[i], pltpu.roll(x[i + d], d, axis=0))
                y[i + d] = jnp.where(keep, pltpu.roll(x[i], SUBLANES - d, axis=0), x[i + d])
        x = y
    return x


def _s5_rows_kernel(u_ref, o_ref, *, chunks):
    per_block = LANES // S5_GROUP
    for gcol in range(D_S5 // LANES):
        for j in range(S5_CONV // LANES):
            groups = [_sublane_transpose([u_ref[gcol, pl.ds((cg * SUBLANES + c) * CHUNK + per_block * j, SUBLANES), :]
                                          for c in range(SUBLANES)]) for cg in range(chunks // SUBLANES)]
            by_time = [jnp.concatenate([grp[sl] for grp in groups], axis=0) for sl in range(per_block)]
            for gl, rows in enumerate(_granule_transpose(by_time)):
                o_ref[gcol * per_block + gl, :, _lane_block(j)] = rows.astype(BF16)


def s5_rows(u, chunks):
    nblk, t, _ = u.shape
    assert chunks % SUBLANES == 0
    r = t // CHUNK
    return pl.pallas_call(
        functools.partial(_s5_rows_kernel, chunks=chunks),
        grid=(r // chunks,),
        in_specs=[pl.BlockSpec((nblk, chunks * CHUNK, LANES), lambda i: (0, i, 0))],
        out_specs=pl.BlockSpec((S5_GROUPS, chunks, S5_CONV), lambda i: (0, i, 0)),
        out_shape=jax.ShapeDtypeStruct((S5_GROUPS, r, S5_CONV), BF16),
        compiler_params=_cparams(),
        name="s5_rows",
    )(u)


def _s5_tokens_kernel(y_ref, o_ref, *, chunks):
    per_block = LANES // S5_GROUP
    for gcol in range(D_S5 // LANES):
        for j in range(S5_CONV // LANES):
            by_group = [y_ref[gcol * per_block + gl, :, _lane_block(j)].astype(F32) for gl in range(per_block)]
            by_time = _granule_transpose(by_group)
            for cg in range(chunks // SUBLANES):
                tiles = _sublane_transpose([rows[cg * SUBLANES:(cg + 1) * SUBLANES, :] for rows in by_time])
                for c in range(SUBLANES):
                    o_ref[gcol, pl.ds((cg * SUBLANES + c) * CHUNK + per_block * j, SUBLANES), :] = tiles[c]


def s5_tokens(y_rows, chunks):
    _, r, _ = y_rows.shape
    assert chunks % SUBLANES == 0
    return pl.pallas_call(
        functools.partial(_s5_tokens_kernel, chunks=chunks),
        grid=(r // chunks,),
        in_specs=[pl.BlockSpec((S5_GROUPS, chunks, S5_CONV), lambda i: (0, i, 0))],
        out_specs=pl.BlockSpec((D_S5 // LANES, chunks * CHUNK, LANES), lambda i: (0, i, 0)),
        out_shape=jax.ShapeDtypeStruct((D_S5 // LANES, r * CHUNK, LANES), F32),
        compiler_params=_cparams(),
        name="s5_tokens",
    )(y_rows)


def _s5_prep_kernel(lrc_ref, lic_ref, lrr_ref, lir_ref, ls_ref, brt_ref, bit_ref, ctr_ref, cti_ref,
                    m_ref, w_ref, wc_ref, a_ref):
    step = jnp.exp(ls_ref[0])

    def discretise(lr_raw, li):
        lr = jnp.minimum(lr_raw, -1e-4)
        dr, di = lr * step, li * step
        mag = jnp.exp(dr)
        a_re, a_im = mag * jnp.cos(di), mag * jnp.sin(di)
        den = lr * lr + li * li
        nr = a_re - 1.0
        fr = (nr * lr + a_im * li) / den
        fi = (a_im * lr - nr * li) / den
        return dr, di, fr, fi

    dr_c, di_c, _, _ = discretise(lrc_ref[0], lic_ref[0])
    dr_r, di_r, fr_r, fi_r = discretise(lrr_ref[0], lir_ref[0])

    lane = lax.broadcasted_iota(jnp.int32, (1, S5_CONV), 1)
    t_row = lax.broadcasted_iota(jnp.int32, (1, CHUNK), 1).astype(F32)
    t_col = lax.broadcasted_iota(jnp.int32, (CHUNK, 1), 0).astype(F32)
    lag_of_lane = (lax.broadcasted_iota(jnp.int32, (CHUNK, S5_CONV), 1) // S5_GROUP
                   == lax.broadcasted_iota(jnp.int32, (CHUNK, S5_CONV), 0)).astype(F32)
    time_of_row = (lax.broadcasted_iota(jnp.int32, (S5_CONV, CHUNK), 0) // S5_GROUP
                   == lax.broadcasted_iota(jnp.int32, (S5_CONV, CHUNK), 1)).astype(F32)
    chan_of_lane = (lax.broadcasted_iota(jnp.int32, (S5_GROUP, S5_CONV), 1) % S5_GROUP
                    == lax.broadcasted_iota(jnp.int32, (S5_GROUP, S5_CONV), 0)).astype(F32)
    chan_of_row = (lax.broadcasted_iota(jnp.int32, (S5_CONV, S5_GROUP), 0) % S5_GROUP
                   == lax.broadcasted_iota(jnp.int32, (S5_CONV, S5_GROUP), 1)).astype(F32)
    ctr = _spread_cols(ctr_ref[0], chan_of_lane)
    cti = _spread_cols(cti_ref[0], chan_of_lane)

    def c_times_power(tf):
        mag = jnp.exp(dr_c * tf)
        ang = di_c * tf
        pr = _spread_cols(mag * jnp.cos(ang), lag_of_lane)
        pi = _spread_cols(mag * jnp.sin(ang), lag_of_lane)
        return ctr * pr - cti * pi, ctr * pi + cti * pr

    cpr, cpi = c_times_power(t_row)
    bbr = fr_r * brt_ref[0] - fi_r * bit_ref[0]
    bbi = fr_r * bit_ref[0] + fi_r * brt_ref[0]
    kt = _dot(bbr, cpr, HIGHEST) - _dot(bbi, cpi, HIGHEST)
    for s in range(CHUNK):
        shifted = kt if s == 0 else pltpu.roll(kt, S5_GROUP * s, axis=1)
        m_ref[0, S5_GROUP * s:S5_GROUP * (s + 1), :] = jnp.where(
            lane >= S5_GROUP * s, shifted, 0.0).astype(BF16)

    rem = CHUNK - 1.0 - t_col
    magw = jnp.exp(dr_r * rem)
    angw = di_r * rem
    pwr = _spread_rows(time_of_row, magw * jnp.cos(angw))
    pwi = _spread_rows(time_of_row, magw * jnp.sin(angw))
    bbtr = _spread_rows(chan_of_row, bbr)
    bbti = _spread_rows(chan_of_row, bbi)
    w_ref[0, :, :S5_STATE] = pwr * bbtr - pwi * bbti
    w_ref[0, :, S5_STATE:] = pwr * bbti + pwi * bbtr

    c1r, c1i = c_times_power(t_row + 1.0)
    wc_ref[0, :S5_STATE, :] = c1r.astype(BF16)
    wc_ref[0, S5_STATE:, :] = (-c1i).astype(BF16)

    full = float(CHUNK)
    mag_c = jnp.exp(dr_r * full)
    a_ref[0, 0:1, :] = mag_c * jnp.cos(di_r * full)
    a_ref[0, 1:2, :] = mag_c * jnp.sin(di_r * full)


def s5_prep(lam_re, lam_im, log_step, b_re, b_im, c_re, c_im):
    g, p = lam_re.shape
    brt = jnp.swapaxes(b_re, 1, 2)
    bit = jnp.swapaxes(b_im, 1, 2)
    args = (lam_re.reshape(g, p, 1), lam_im.reshape(g, p, 1),
            lam_re.reshape(g, 1, p), lam_im.reshape(g, 1, p), log_step.reshape(g, 1, 1),
            brt, bit, jnp.swapaxes(c_re, 1, 2), jnp.swapaxes(c_im, 1, 2))
    spec = lambda a: pl.BlockSpec((1,) + a.shape[1:], lambda i: (i, 0, 0))
    out_shape = [jax.ShapeDtypeStruct((g, S5_CONV, S5_CONV), BF16),
                 jax.ShapeDtypeStruct((g, S5_CONV, 2 * S5_STATE), F32),
                 jax.ShapeDtypeStruct((g, 2 * S5_STATE, S5_CONV), BF16),
                 jax.ShapeDtypeStruct((g, 2, S5_STATE), F32)]
    return pl.pallas_call(
        _s5_prep_kernel,
        grid=(g,),
        in_specs=[spec(a) for a in args],
        out_specs=[spec(o) for o in out_shape],
        out_shape=out_shape,
        compiler_params=_cparams(),
        name="s5_prep",
    )(*args)


def _s5_main_kernel(up_ref, us_ref, m_ref, w_ref, wc_ref, a_ref, xsr_ref, xsi_ref,
                    yp_ref, ys_ref, fpr_ref, fpi_ref, fsr_ref, fsi_ref, fin_scr, *, n_prompt, n_chunks):
    w3 = jnp.concatenate(_split3(w_ref[0]), axis=1)
    ar = a_ref[0, 0:1, :]
    ai = a_ref[0, 1:2, :]
    width = 2 * S5_STATE

    def local(u):
        h3 = _dot(u, w3)
        return _dot(u, m_ref[0]), h3[:, :width] + h3[:, width:2 * width] + h3[:, 2 * width:]

    def times(pr, pi, x):
        return (jnp.concatenate([pr, pr], axis=1) * x
                + jnp.concatenate([-pi, pi], axis=1) * pltpu.roll(x, S5_STATE, axis=1))

    y_local, x = local(up_ref[0])
    chunk_of_row = lax.broadcasted_iota(jnp.int32, (n_prompt * n_chunks, 1), 0) % n_chunks
    pr, pi = ar, ai
    d = 1
    while d < n_chunks:
        x = x + times(pr, pi, jnp.where(chunk_of_row >= d, pltpu.roll(x, d, axis=0), 0.0))
        pr, pi = pr * pr - pi * pi, 2.0 * pr * pi
        d *= 2
    fin_scr[...] = x
    last = fin_scr[pl.ds(n_chunks - 1, n_prompt, stride=n_chunks), :]
    fpr_ref[0] = last[:, :S5_STATE]
    fpi_ref[0] = last[:, S5_STATE:]
    x0 = jnp.where(chunk_of_row >= 1, pltpu.roll(x, 1, axis=0), 0.0)
    yp_ref[0] = (y_local + _dot(x0.astype(BF16), wc_ref[0])).astype(BF16)

    y_local, hend = local(us_ref[0])
    x0 = jnp.concatenate([xsr_ref[0], xsi_ref[0]], axis=1)
    fin = times(ar, ai, x0) + hend
    fsr_ref[0] = fin[:, :S5_STATE]
    fsi_ref[0] = fin[:, S5_STATE:]
    ys_ref[0] = (y_local + _dot(x0.astype(BF16), wc_ref[0])).astype(BF16)


def s5_main(up_rows, us_rows, m, w, wc, a, xs_re, xs_im, n_prompt, n_chunks):
    g, r, _ = up_rows.shape
    n_sample = xs_re.shape[1]
    spec = lambda shape: pl.BlockSpec((1,) + tuple(shape[1:]), lambda i: (i, 0, 0))
    args = (up_rows, us_rows, m, w, wc, a, xs_re, xs_im)
    out_shape = [jax.ShapeDtypeStruct((g, r, S5_CONV), BF16),
                 jax.ShapeDtypeStruct((g, n_sample, S5_CONV), BF16),
                 jax.ShapeDtypeStruct((g, n_prompt, S5_STATE), F32),
                 jax.ShapeDtypeStruct((g, n_prompt, S5_STATE), F32),
                 jax.ShapeDtypeStruct((g, n_sample, S5_STATE), F32),
                 jax.ShapeDtypeStruct((g, n_sample, S5_STATE), F32)]
    return pl.pallas_call(
        functools.partial(_s5_main_kernel, n_prompt=n_prompt, n_chunks=n_chunks),
        grid=(g,),
        in_specs=[spec(x.shape) for x in args],
        out_specs=[spec(o.shape) for o in out_shape],
        out_shape=out_shape,
        scratch_shapes=[pltpu.VMEM((r, 2 * S5_STATE), F32)],
        compiler_params=_cparams(),
        name="s5_main",
    )(*args)


def _hgrn_kernel(f_ref, z_ref, s0_ref, lb_ref, ng_ref, o_ref, sfin_ref, st_scr, *, n_seq, chunk):
    c = pl.program_id(1)

    @pl.when(c == 0)
    def _():
        st_scr[...] = s0_ref[...]

    lbw = lb_ref[...]
    lbe = jnp.exp(lbw - jnp.max(lbw, axis=0, keepdims=True))
    lb_all = lbe[0:1, :] / jnp.sum(lbe, axis=0, keepdims=True)

    levels = [chunk >> (i + 1) for i in range(chunk.bit_length() - 1)]
    rowi = lax.broadcasted_iota(jnp.int32, (chunk, chunk), 0)
    coli = lax.broadcasted_iota(jnp.int32, (chunk, chunk), 1)
    rowk = lax.broadcasted_iota(jnp.int32, (chunk, HG_D), 0)
    sign, valid = [], []
    for m in levels:
        sign.append(jnp.where((rowk % (2 * m)) >= m, 1.0, -1.0))
        valid.append(((rowi // (2 * m)) == (coli // (2 * m)))
                     & ((rowi % (2 * m)) >= m) & ((coli % (2 * m)) < m))
    cum_mat = (coli <= rowi).astype(BF16)
    cum_mat3 = jnp.concatenate([cum_mat] * 3, axis=1)
    diag = rowi == coli
    nt = (((1,), (1,)), ((), ()))

    def body(n, carry):
        zf = f_ref[n]
        fg_all = lb_all + (1.0 - lb_all) * jax.nn.sigmoid(zf)
        cums = _dot(cum_mat3, jnp.concatenate(_split3(jnp.log2(fg_all)), axis=0))
        for hd in range(HG_HEADS):
            cols = slice(hd * HG_D, (hd + 1) * HG_D)
            zq = z_ref[n, :, hd * HG_D:(hd + 1) * HG_D].astype(F32)
            vb = z_ref[n, :, D_HG + hd * HG_D:D_HG + (hd + 1) * HG_D]
            zg = z_ref[n, :, 2 * D_HG + hd * HG_D:2 * D_HG + (hd + 1) * HG_D].astype(F32)
            q = zq * jax.nn.sigmoid(zq)
            kk = 1.0 - fg_all[:, cols]
            bcum = cums[:chunk, cols]
            b_last = bcum[chunk - 1:chunk, :]
            qb = q.astype(BF16)
            kb = kk.astype(BF16)
            st = st_scr[n, hd]

            scores = jnp.where(diag, lax.dot_general(qb, kb, nt, preferred_element_type=F32), 0.0)
            for lvl, m in enumerate(levels):
                if 2 * m >= SUBLANES:
                    bref = jnp.concatenate(
                        [jnp.broadcast_to(bcum[b * 2 * m + m - 1:b * 2 * m + m, :], (2 * m, HG_D))
                         for b in range(chunk // (2 * m))], axis=0)
                else:
                    offs = rowk % (2 * m) - (m - 1)
                    bref = bcum
                    for o in range(-(m - 1), m + 1):
                        if o != 0:
                            bref = jnp.where(offs == o, pltpu.roll(bcum, o % chunk, axis=0), bref)
                dec = jnp.exp2((bcum - bref) * sign[lvl]).astype(BF16)
                sc = lax.dot_general(qb * dec, kb * dec, nt, preferred_element_type=F32)
                scores = jnp.where(valid[lvl], sc, scores)

            qd = (q * jnp.exp2(bcum)).astype(BF16)
            o = lax.dot_general(qd, st.astype(BF16), nt, preferred_element_type=F32)
            o = o + _dot(scores.astype(BF16), vb)
            kdec = (kk * jnp.exp2(b_last - bcum)).astype(BF16)
            st_scr[n, hd] = jnp.exp2(b_last) * st + lax.dot_general(
                vb, kdec, (((0,), (0,)), ((), ())), preferred_element_type=F32)

            on = o * lax.rsqrt(jnp.mean(o * o, axis=-1, keepdims=True) + RMS_EPS) * ng_ref[:, cols]
            o_ref[n, :, hd * HG_D:(hd + 1) * HG_D] = on * (zg * jax.nn.sigmoid(zg))
        return carry

    lax.fori_loop(0, n_seq, body, 0, unroll=True)

    @pl.when(c == pl.num_programs(1) - 1)
    def _():
        sfin_ref[...] = st_scr[...]


def hgrn(f, z, s0_t, hg_lb, norm_g, n_seq, chunk):
    n, length, _ = z.shape
    return pl.pallas_call(
        functools.partial(_hgrn_kernel, n_seq=n_seq, chunk=chunk),
        grid=(n // n_seq, length // chunk),
        in_specs=[pl.BlockSpec((n_seq, chunk, D_HG), lambda g, c: (g, c, 0)),
                  pl.BlockSpec((n_seq, chunk, 3 * D_HG), lambda g, c: (g, c, 0)),
                  pl.BlockSpec((n_seq, HG_HEADS, HG_D, HG_D), lambda g, c: (g, 0, 0, 0)),
                  pl.BlockSpec(hg_lb.shape, lambda g, c: (0, 0)),
                  pl.BlockSpec((1, D_HG), lambda g, c: (0, 0))],
        out_specs=[pl.BlockSpec((n_seq, chunk, D_HG), lambda g, c: (g, c, 0)),
                   pl.BlockSpec((n_seq, HG_HEADS, HG_D, HG_D), lambda g, c: (g, 0, 0, 0))],
        out_shape=[jax.ShapeDtypeStruct((n, length, D_HG), F32),
                   jax.ShapeDtypeStruct((n, HG_HEADS, HG_D, HG_D), F32)],
        scratch_shapes=[pltpu.VMEM((n_seq, HG_HEADS, HG_D, HG_D), F32)],
        compiler_params=_cparams(2),
        name="hgrn",
    )(f, z, s0_t, hg_lb, norm_g)


def _post_mix_kernel(h_ref, up_ref, us_ref, yp_ref, ys_ref, hgp_ref, hgs_ref,
                     d_ref, wglu_ref, bglu_ref, s5g_ref, wout_ref, g1_ref, b1_ref, rwt_ref, rb_ref,
                     h1_ref, slot_ref, gate_ref, before_ref, cnt_ref, run_scr, *, tm, n_first):
    i = pl.program_id(0)

    @pl.when(i == 0)
    def _():
        run_scr[...] = jnp.zeros_like(run_scr)

    def phase(u_ref, y_ref, hg_ref):
        ys = _read_lane_blocks(y_ref) + d_ref[...] * _read_lane_blocks(u_ref)
        gl = 0.5 * ys * (1.0 + lax.erf(ys * (2.0 ** -0.5)))
        s5o = gl * jax.nn.sigmoid(_dot(gl.astype(BF16), wglu_ref[...]) + bglu_ref[...])
        s5o = s5o * lax.rsqrt(jnp.mean(s5o * s5o, axis=-1, keepdims=True) + RMS_EPS) * s5g_ref[...]
        mix = (_dot(s5o.astype(BF16), wout_ref[:D_S5, :])
               + _dot(hg_ref[...].astype(BF16), wout_ref[D_S5:, :]))
        h1 = _layernorm(DEEPNORM_ALPHA * h_ref[...] + mix, g1_ref[...], b1_ref[...])
        _write_row_tiles(h1_ref, h1, tm)

        h_hi, h_mid, _ = _split3(h1)
        w_hi, w_mid, _ = _split3(rwt_ref[...])
        nt = (((1,), (1,)), ((), ()))
        logits = (lax.dot_general(w_hi, h_hi, nt, preferred_element_type=F32)
                  + lax.dot_general(w_hi, h_mid, nt, preferred_element_type=F32)
                  + lax.dot_general(w_mid, h_hi, nt, preferred_element_type=F32)) + rb_ref[...]
        eid = lax.broadcasted_iota(jnp.int32, (N_EXPERTS, tm), 0)
        vals, idxs = [], []
        for _ in range(TOP_K):
            m = jnp.max(logits, axis=0, keepdims=True)
            ix = jnp.min(jnp.where(logits == m, eid, N_EXPERTS), axis=0, keepdims=True)
            vals.append(m)
            idxs.append(ix)
            logits = jnp.where(eid == ix, -jnp.inf, logits)
        exps = [jnp.exp(v - vals[0]) for v in vals]
        den = exps[0] + exps[1] + exps[2] + exps[3]

        onehot = jnp.zeros((N_EXPERTS, tm), F32)
        for ix in idxs:
            onehot = onehot + (eid == ix).astype(F32)
        rowi = lax.broadcasted_iota(jnp.int32, (tm, tm), 0)
        coli = lax.broadcasted_iota(jnp.int32, (tm, tm), 1)
        earlier = (rowi < coli).astype(BF16)
        prefix = _dot(onehot.astype(BF16), earlier)
        tile_cnt = jnp.sum(onehot, axis=1, keepdims=True)
        for k in range(TOP_K):
            lower_experts = jnp.sum(jnp.where(eid < idxs[k], tile_cnt, 0.0), axis=0, keepdims=True)
            rank = jnp.sum(jnp.where(eid == idxs[k], prefix, 0.0), axis=0, keepdims=True)
            slot_ref[0, :, k * tm:(k + 1) * tm] = (lower_experts + rank).astype(jnp.int32) * ROW_TILES
            gate_ref[0, :, k * tm:(k + 1) * tm] = exps[k] / den
        before_ref[0] = run_scr[...]
        run_scr[...] = run_scr[...] + tile_cnt
        cnt_ref[...] = run_scr[...]

    pl.when(i < n_first)(lambda: phase(up_ref, yp_ref, hgp_ref))
    pl.when(i >= n_first)(lambda: phase(us_ref, ys_ref, hgs_ref))


def post_mix(h0, u_pair, y_pair, hg_pair, d_skip, wglu, bglu, s5g, wout, g1, b1, rw_t, rb_col, tm):
    t = h0.shape[0]
    n_first = u_pair[0].shape[1] // tm
    row = lambda i: (i, 0)
    fixed = lambda i: (0, 0)
    full = lambda a: pl.BlockSpec(a.shape, fixed)
    weights = (d_skip, wglu, bglu, s5g, wout, g1, b1, rw_t, rb_col)
    return pl.pallas_call(
        functools.partial(_post_mix_kernel, tm=tm, n_first=n_first),
        grid=(t // tm,),
        in_specs=[pl.BlockSpec((tm, D_MODEL), row)]
                 + _two_phase_lane_block_specs(tm, n_first) * 2 + _two_phase_specs((tm, D_HG), n_first)
                 + [full(a) for a in weights],
        out_specs=[pl.BlockSpec((tm * ROW_TILES, LANES), row),
                   pl.BlockSpec((1, 1, TOP_K * tm), lambda i: (i, 0, 0)),
                   pl.BlockSpec((1, 1, TOP_K * tm), lambda i: (i, 0, 0)),
                   pl.BlockSpec((1, N_EXPERTS, 1), lambda i: (i, 0, 0)),
                   pl.BlockSpec((N_EXPERTS, 1), fixed)],
        out_shape=[jax.ShapeDtypeStruct((t * ROW_TILES, LANES), F32),
                   jax.ShapeDtypeStruct((t // tm, 1, TOP_K * tm), jnp.int32),
                   jax.ShapeDtypeStruct((t // tm, 1, TOP_K * tm), F32),
                   jax.ShapeDtypeStruct((t // tm, N_EXPERTS, 1), F32),
                   jax.ShapeDtypeStruct((N_EXPERTS, 1), F32)],
        scratch_shapes=[pltpu.VMEM((N_EXPERTS, 1), F32)],
        compiler_params=_cparams(),
        name="post_mix",
    )(h0, *u_pair, *y_pair, *hg_pair, *weights)


def _segment_copies(meta_ref, tile, tm, make_copy):
    for e in range(N_EXPERTS):
        sorted_row = meta_ref[tile, e]
        cnt = meta_ref[tile, N_EXPERTS + e]
        staged_row = meta_ref[tile, 2 * N_EXPERTS + e]
        for b in range(tm.bit_length()):
            done = cnt & ((1 << b) - 1)

            @pl.when(((cnt >> b) & 1) == 1)
            def _(b=b, done=done, e=e):
                make_copy(staged_row + done, sorted_row + done, 1 << b).start(priority=e % 2)


def _dispatch_kernel(meta_ref, pend_ref, slot_ref, h_ref, xs_ref, stage, zero_scr, zsem, sem, *, tm):
    n_rows = xs_ref.shape[0] // ROW_TILES
    i = pl.program_id(0)
    n = pl.num_programs(0)
    slot = i % 2

    def drain(s):
        pltpu.make_async_copy(stage.at[s], xs_ref.at[pl.ds(0, TOP_K * tm * ROW_TILES)], sem.at[s]).wait()

    @pl.when(i == 0)
    def _():
        zero_scr[...] = jnp.zeros_like(zero_scr)

        def last_block(e):
            prev = pend_ref[e - 1] if e > 0 else 0
            copy = pltpu.make_async_copy(
                zero_scr, xs_ref.at[pl.ds(pl.multiple_of(jnp.maximum(pend_ref[e] - MOE_ROWS, 0) * ROW_TILES,
                                                         ROW_TILES), MOE_ROWS * ROW_TILES)], zsem)
            return pend_ref[e] > prev, copy

        def tail_block(j):
            row0 = pend_ref[N_EXPERTS - 1] + j * MOE_ROWS
            copy = pltpu.make_async_copy(
                zero_scr, xs_ref.at[pl.ds(pl.multiple_of(jnp.minimum(row0, n_rows - MOE_ROWS) * ROW_TILES,
                                                         ROW_TILES), MOE_ROWS * ROW_TILES)], zsem)
            return row0 < n_rows, copy

        blocks = [last_block(e) for e in range(N_EXPERTS)] + [tail_block(j) for j in range(N_EXPERTS)]
        for used, copy in blocks:
            pl.when(used)(copy.start)
        for used, copy in blocks:
            pl.when(used)(copy.wait)

    pl.when(i >= 2)(lambda: drain(slot))

    def regroup(s):
        def body(t, carry):
            row = h_ref[_row_tile(t), :]
            for k in range(TOP_K):
                stage[s, pl.ds(pl.multiple_of(slot_ref[k * tm + t], ROW_TILES), ROW_TILES), :] = row
            return carry

        lax.fori_loop(0, tm, body, 0, unroll=8)

    pl.when(slot == 0)(lambda: regroup(0))
    pl.when(slot == 1)(lambda: regroup(1))
    _segment_copies(meta_ref, i, tm, lambda staged_row, sorted_row, rows: pltpu.make_async_copy(
        stage.at[slot, _row_tiles(staged_row, rows)], xs_ref.at[_row_tiles(sorted_row, rows)], sem.at[slot]))

    @pl.when(i == n - 1)
    def _():
        pl.when(n >= 2)(lambda: drain(1 - slot))
        drain(slot)


def dispatch(meta, pend, slots_flat, h1, n_rows, tm):
    t = h1.shape[0] // ROW_TILES
    grid_spec = pltpu.PrefetchScalarGridSpec(
        num_scalar_prefetch=2,
        grid=(t // tm,),
        in_specs=[pl.BlockSpec((TOP_K * tm,), lambda i, meta, pend: (i,), memory_space=pltpu.SMEM),
                  pl.BlockSpec((tm * ROW_TILES, LANES), lambda i, meta, pend: (i, 0))],
        out_specs=pl.BlockSpec(memory_space=pl.ANY),
        scratch_shapes=[pltpu.VMEM((2, TOP_K * tm * ROW_TILES, LANES), F32),
                        pltpu.VMEM((MOE_ROWS * ROW_TILES, LANES), F32),
                        pltpu.SemaphoreType.DMA(()), pltpu.SemaphoreType.DMA((2,))],
    )
    return pl.pallas_call(
        functools.partial(_dispatch_kernel, tm=tm),
        grid_spec=grid_spec,
        out_shape=jax.ShapeDtypeStruct((n_rows * ROW_TILES, LANES), F32),
        compiler_params=_cparams(),
        name="moe_dispatch",
    )(meta, pend, slots_flat, h1)


def _moe_ffn_kernel(be_ref, nu_ref, seg_ref, nxt_ref, x_ref, wg_ref, bg_ref, wu_ref, bu_ref, wd_ref, bd_ref,
                    y_ref, wbuf, wbf, sem):
    i = pl.program_id(0)
    hbm = (wg_ref, wu_ref, wd_ref)

    def weight_copies(expert, s):
        return [pltpu.make_async_copy(hbm[j].at[expert], wbuf.at[s, j], sem.at[s, j]) for j in range(3)]

    @pl.when((i == 0) | (be_ref[i] != be_ref[jnp.maximum(i - 1, 0)]))
    def _():
        s = seg_ref[i] % 2

        @pl.when(i == 0)
        def _():
            for c in weight_copies(be_ref[0], 0):
                c.start()

        for j, c in enumerate(weight_copies(be_ref[i], s)):
            c.wait()
            wbf[j] = wbuf[s, j].astype(BF16)

        @pl.when(nxt_ref[i] >= 0)
        def _():
            for c in weight_copies(nxt_ref[i], 1 - s):
                c.start()

    @pl.when(i < nu_ref[0])
    def _():
        x = _read_row_tiles(x_ref, MOE_ROWS).astype(BF16)
        gt = jnp.minimum(_dot(x, wbf[0]) + bg_ref[0], SWIGLU_LIMIT)
        up = jnp.clip(_dot(x, wbf[1]) + bu_ref[0], -SWIGLU_LIMIT, SWIGLU_LIMIT)
        hid = (up + 1.0) * (gt * jax.nn.sigmoid(SWIGLU_ALPHA * gt))
        _write_row_tiles(y_ref, _dot(hid.astype(BF16), wbf[2]) + bd_ref[0], MOE_ROWS)

    @pl.when(i >= nu_ref[0])
    def _():
        y_ref[...] = jnp.zeros_like(y_ref)


def moe_ffn(block_e, n_used, segment, next_e, xs, wg, bg, wu, bu, wd, bd):
    n_rows = xs.shape[0] // ROW_TILES
    n_blocks = n_rows // MOE_ROWS
    wsel = lambda i, be, nu, seg, nxt: (be[i], 0, 0)
    d_ff = wg.shape[-1]
    assert wg.shape[1:] == wu.shape[1:] == wd.shape[1:] == (D_MODEL, D_MODEL)
    anywhere = pl.BlockSpec(memory_space=pl.ANY)
    grid_spec = pltpu.PrefetchScalarGridSpec(
        num_scalar_prefetch=4,
        grid=(n_blocks,),
        in_specs=[pl.BlockSpec((MOE_ROWS * ROW_TILES, LANES),
                               lambda i, be, nu, seg, nxt: (jnp.minimum(i, nu[0] - 1), 0)),
                  anywhere, pl.BlockSpec((1, 1, d_ff), wsel),
                  anywhere, pl.BlockSpec((1, 1, d_ff), wsel),
                  anywhere, pl.BlockSpec((1, 1, D_MODEL), wsel)],
        out_specs=pl.BlockSpec((MOE_ROWS * ROW_TILES, LANES), lambda i, be, nu, seg, nxt: (i, 0)),
        scratch_shapes=[pltpu.VMEM((2, 3, D_MODEL, D_MODEL), F32), pltpu.VMEM((3, D_MODEL, D_MODEL), BF16),
                        pltpu.SemaphoreType.DMA((2, 3))],
    )
    return pl.pallas_call(
        _moe_ffn_kernel,
        grid_spec=grid_spec,
        out_shape=jax.ShapeDtypeStruct((n_rows * ROW_TILES, LANES), F32),
        compiler_params=_cparams(),
        name="moe_ffn",
    )(block_e, n_used, segment, next_e, xs, wg, bg, wu, bu, wd, bd)


def _combine_kernel(meta_ref, slot_ref, gate_ref, h_ref, pp_ref, ps_ref, yb_ref,
                    plew_ref, plegw_ref, g2_ref, b2_ref, outp_ref, outs_ref, buf, r_scr, sem,
                    *, tm, n_first):
    i = pl.program_id(0)
    n = pl.num_programs(0)
    slot = i % 2

    def fetch(tile, s):
        _segment_copies(meta_ref, tile, tm, lambda staged_row, sorted_row, rows: pltpu.make_async_copy(
            yb_ref.at[_row_tiles(sorted_row, rows)], buf.at[s, _row_tiles(staged_row, rows)], sem.at[s]))

    pl.when(i == 0)(lambda: fetch(0, 0))
    pl.when(i + 1 < n)(lambda: fetch(i + 1, 1 - slot))
    pltpu.make_async_copy(yb_ref.at[pl.ds(0, TOP_K * tm * ROW_TILES)], buf.at[slot], sem.at[slot]).wait()

    def weighted_sum(s):
        def body(t, carry):
            acc = DEEPNORM_ALPHA * h_ref[_row_tile(t), :]
            for k in range(TOP_K):
                acc = acc + gate_ref[k * tm + t] * buf[s, pl.ds(pl.multiple_of(slot_ref[k * tm + t], ROW_TILES),
                                                               ROW_TILES), :]
            r_scr[_row_tile(t), :] = acc
            return carry

        lax.fori_loop(0, tm, body, 0, unroll=8)

    pl.when(slot == 0)(lambda: weighted_sum(0))
    pl.when(slot == 1)(lambda: weighted_sum(1))
    r = _read_row_tiles(r_scr, tm)
    gate = jax.nn.sigmoid(_dot(r.astype(BF16), plegw_ref[...]))

    def finish(p, store):
        e = _dot(p.astype(BF16), plew_ref[...]) * gate
        store(_layernorm(r + e, g2_ref[...], b2_ref[...]))

    def store_prompt(v):
        outp_ref[0] = v

    def store_sample(v):
        outs_ref[...] = v

    pl.when(i < n_first)(lambda: finish(pp_ref[0], store_prompt))
    pl.when(i >= n_first)(lambda: finish(ps_ref[...], store_sample))


def combine(meta, slots_flat, gates_flat, h1, p_prompt, p_sample, yb, plew, plegw, g2, b2, tm):
    t = h1.shape[0] // ROW_TILES
    nb, seq, _ = p_prompt.shape
    ts = p_sample.shape[0]
    n_first = nb * seq // tm
    fixed = lambda i, meta: (0, 0)
    flat = pl.BlockSpec((TOP_K * tm,), lambda i, meta: (i,), memory_space=pltpu.SMEM)
    sample = lambda width: pl.BlockSpec((tm, width), lambda i, meta: (jnp.maximum(i - n_first, 0), 0))
    prompt = lambda width: pl.BlockSpec((1, tm, width), _prompt_spec(tm, seq, width, n_first).index_map)
    with_meta = lambda spec: pl.BlockSpec(spec.block_shape, lambda i, meta: spec.index_map(i))
    grid_spec = pltpu.PrefetchScalarGridSpec(
        num_scalar_prefetch=1,
        grid=(t // tm,),
        in_specs=[flat, flat,
                  pl.BlockSpec((tm * ROW_TILES, LANES), lambda i, meta: (i, 0)),
                  with_meta(prompt(PLE_DIM)), sample(PLE_DIM),
                  pl.BlockSpec(memory_space=pl.ANY),
                  pl.BlockSpec(plew.shape, fixed), pl.BlockSpec(plegw.shape, fixed),
                  pl.BlockSpec((1, D_MODEL), fixed), pl.BlockSpec((1, D_MODEL), fixed)],
        out_specs=[with_meta(prompt(D_MODEL)), sample(D_MODEL)],
        scratch_shapes=[pltpu.VMEM((2, TOP_K * tm * ROW_TILES, LANES), F32),
                        pltpu.VMEM((tm * ROW_TILES, LANES), F32),
                        pltpu.SemaphoreType.DMA((2,))],
    )
    return pl.pallas_call(
        functools.partial(_combine_kernel, tm=tm, n_first=n_first),
        grid_spec=grid_spec,
        out_shape=[jax.ShapeDtypeStruct((nb, seq, D_MODEL), F32), jax.ShapeDtypeStruct((ts, D_MODEL), F32)],
        compiler_params=_cparams(),
        name="moe_combine",
    )(meta, slots_flat, gates_flat, h1, p_prompt, p_sample, yb, plew, plegw, g2, b2)


def _row(v):
    return v.reshape(1, -1)


def kernel(x_prompt, x_sample, state_s5_re, state_s5_im, state_hgrn, p_prompt, p_sample, ln_in_g, ln_in_b, w_in, s5_lambda_re, s5_lambda_im, s5_log_step, s5_b_re, s5_b_im, s5_c_re, s5_c_im, s5_d, s5_w_glu, s5_b_glu, s5_norm_g, hg_lb, hg_norm_g, w_out, ln1_g, ln1_b, router_w, router_b, w_gate, b_gate, w_up, b_up, w_down, b_down, ple_w, ple_gate_w, ln2_g, ln2_b):
    nb, seq, _ = x_prompt.shape
    ns, dseq, _ = x_sample.shape
    assert dseq == CHUNK and seq % CHUNK == 0 and w_in.shape[0] == 1
    nc = seq // CHUNK
    tp, ts = nb * seq, ns * dseq
    t = tp + ts
    tm = 512 if (tp % 512 == 0 and ts % 512 == 0) else 256
    assert tp % tm == 0 and ts % tm == 0 and seq % tm == 0 and ns % nb == 0

    w_cols = jnp.split(w_in[0], [D_S5, D_S5 + D_HG, D_S5 + 2 * D_HG], axis=1)
    h0, u_p, u_s, f_p, f_s, z_p, z_s = ln_in_proj(
        x_prompt, x_sample.reshape(ts, D_MODEL), _row(ln_in_g), _row(ln_in_b),
        jnp.concatenate([w_cols[0], w_cols[2], w_cols[1], w_cols[3]], axis=1).astype(BF16), tm)

    m, w, wc, a = s5_prep(s5_lambda_re[0], s5_lambda_im[0], s5_log_step[0],
                          s5_b_re[0], s5_b_im[0], s5_c_re[0], s5_c_im[0])
    rb = 64 if (nb * nc) % 64 == 0 else nc
    yp_rows, ys_rows, fpr, fpi, fsr, fsi = s5_main(s5_rows(u_p, rb), s5_rows(u_s, ns), m, w, wc, a,
                                                   jnp.swapaxes(state_s5_re[0], 0, 1),
                                                   jnp.swapaxes(state_s5_im[0], 0, 1), nb, nc)
    y_pair = (s5_tokens(yp_rows, rb), s5_tokens(ys_rows, ns))

    zero_state = jnp.zeros((nb, HG_HEADS, HG_D, HG_D), F32)
    ng = _row(hg_norm_g[0])
    o_p, st_p = hgrn(f_p.reshape(nb, seq, D_HG), z_p.reshape(nb, seq, 3 * D_HG), zero_state, hg_lb, ng, nb,
                     HG_CHUNK if seq % HG_CHUNK == 0 else CHUNK)
    o_s, st_s = hgrn(f_s.reshape(ns, dseq, D_HG), z_s.reshape(ns, dseq, 3 * D_HG), jnp.swapaxes(state_hgrn[0], 2, 3),
                     hg_lb, ng, nb, dseq)

    h1, slots, gates, before, counts = post_mix(
        h0, (u_p, u_s), y_pair, (o_p.reshape(tp, D_HG), o_s.reshape(ts, D_HG)),
        _row(s5_d[0]), s5_w_glu[0].astype(BF16), _row(s5_b_glu[0]),
        _row(s5_norm_g[0]), w_out[0].astype(BF16), _row(ln1_g[0]), _row(ln1_b[0]),
        router_w[0].T, router_b[0].reshape(N_EXPERTS, 1), tm)

    counts = counts[:, 0].astype(jnp.int32)
    before = before[:, :, 0].astype(jnp.int32)
    cnt = jnp.concatenate([before[1:], counts[None]], axis=0) - before
    padded = (counts + MOE_ROWS - 1) // MOE_ROWS * MOE_ROWS
    pend = jnp.cumsum(padded)
    staged = jnp.cumsum(cnt, axis=1) - cnt
    meta = jnp.concatenate([pend - padded + before, cnt, staged, jnp.zeros_like(cnt)], axis=1)
    experts = jnp.arange(N_EXPERTS, dtype=jnp.int32)

    n_blocks = -(-t * TOP_K // MOE_ROWS) + N_EXPERTS
    n_used = (pend[-1] // MOE_ROWS).astype(jnp.int32)
    blk = jnp.arange(n_blocks, dtype=jnp.int32)
    blk = jnp.minimum(blk, n_used - 1)
    block_e = jnp.sum((pend[None, :] <= (blk * MOE_ROWS)[:, None]).astype(jnp.int32), axis=1)
    block_e = jnp.minimum(block_e, N_EXPERTS - 1)
    owns_rows = padded > 0
    owner = owns_rows[None, :]
    segment = jnp.sum((owner & (experts[None, :] <= block_e[:, None])).astype(jnp.int32), axis=1) - 1
    later = jnp.where(owner & (experts[None, :] > block_e[:, None]), experts[None, :], N_EXPERTS)
    next_e = jnp.min(later, axis=1)
    next_e = jnp.where(next_e < N_EXPERTS, next_e, -1).astype(jnp.int32)

    slots_flat = slots.reshape(-1)
    xs = dispatch(meta, pend, slots_flat, h1, n_blocks * MOE_ROWS, tm)
    yb = moe_ffn(block_e, n_used.reshape(1), segment, next_e, xs,
                 w_gate[0], b_gate[0][:, None, :], w_up[0], b_up[0][:, None, :],
                 w_down[0], b_down[0][:, None, :])
    out_p, out_s = combine(meta, slots_flat, gates.reshape(-1), h1, p_prompt[0], p_sample[0].reshape(ts, PLE_DIM),
                           yb, ple_w[0].astype(BF16), ple_gate_w[0].astype(BF16),
                           _row(ln2_g[0]), _row(ln2_b[0]), tm)

    def s5_state(f, n):
        return jnp.swapaxes(f, 0, 1).reshape(1, n, S5_GROUPS, S5_STATE)

    return (out_p, out_s.reshape(ns, dseq, D_MODEL),
            s5_state(fpr, nb), s5_state(fpi, nb), jnp.swapaxes(st_p, 2, 3)[None],
            s5_state(fsr, ns), s5_state(fsi, ns), jnp.swapaxes(st_s, 2, 3)[None])
```

```python
import functools

import jax
import jax.numpy as jnp
from jax import lax
from jax.experimental import pallas as pl
from jax.experimental.pallas import tpu as pltpu

F32 = jnp.float32
BF16 = jnp.bfloat16
HIGHEST = lax.Precision.HIGHEST

D_MODEL = 1024
CHUNK = 64
PLE_DIM = 256
D_S5 = 512
S5_GROUP = 16
S5_GROUPS = 32
S5_STATE = 64
D_HG = 512
HG_HEADS = 4
HG_D = 128
D_IN = D_S5 + 4 * D_HG
N_EXPERTS = 32
TOP_K = 4
SWIGLU_LIMIT = 7.0
SWIGLU_ALPHA = 1.702
DEEPNORM_ALPHA = 2.0 ** 0.25
LN_EPS = 1e-5
RMS_EPS = 1e-6

LANES = 128
SUBLANES = 8
ROW_TILES = D_MODEL // LANES
S5_CONV = CHUNK * S5_GROUP
HG_CHUNK = 128
MOE_ROWS = 512
VMEM_LIMIT = 56 * 1024 * 1024

assert ROW_TILES == SUBLANES


def _cparams(n_axes=1, flags=None):
    return pltpu.CompilerParams(dimension_semantics=("arbitrary",) * n_axes,
                                vmem_limit_bytes=VMEM_LIMIT, flags=flags)


def _dot(a, b, precision=None):
    return jnp.dot(a, b, preferred_element_type=F32, precision=precision)


def _split3(w):
    hi = w.astype(BF16)
    r1 = w - hi.astype(F32)
    mid = r1.astype(BF16)
    lo = (r1 - mid.astype(F32)).astype(BF16)
    return hi, mid, lo


def _spread_cols(table, onehot):
    return _dot(jnp.concatenate(_split3(table), axis=1), jnp.concatenate([onehot.astype(BF16)] * 3, axis=0))


def _spread_rows(onehot, table):
    return _dot(jnp.concatenate([onehot.astype(BF16)] * 3, axis=1), jnp.concatenate(_split3(table), axis=0))


def _layernorm(x, g, b):
    mu = jnp.mean(x, axis=-1, keepdims=True)
    xc = x - mu
    var = jnp.mean(xc * xc, axis=-1, keepdims=True)
    return xc * lax.rsqrt(var + LN_EPS) * g + b


def _two_phase_specs(block, n_first):
    nd = len(block)
    first = pl.BlockSpec(block, lambda i: (jnp.minimum(i, n_first - 1),) + (0,) * (nd - 1))
    second = pl.BlockSpec(block, lambda i: (jnp.maximum(i - n_first, 0),) + (0,) * (nd - 1))
    return [first, second]


def _two_phase_lane_block_specs(tm, n_first):
    nblk = D_S5 // LANES
    first = pl.BlockSpec((nblk, tm, LANES), lambda i: (0, jnp.minimum(i, n_first - 1), 0))
    second = pl.BlockSpec((nblk, tm, LANES), lambda i: (0, jnp.maximum(i - n_first, 0), 0))
    return [first, second]


def _read_lane_blocks(ref):
    return jnp.concatenate([ref[j] for j in range(D_S5 // LANES)], axis=1)


def _prompt_spec(tm, seq, width, n_first):
    per_seq = seq // tm

    def index(i):
        ic = jnp.minimum(i, n_first - 1)
        return (ic // per_seq, ic % per_seq, 0)

    return pl.BlockSpec((1, tm, width), index)


def _chunk(rows, j):
    return pl.ds(j, rows, stride=ROW_TILES)


def _read_row_tiles(ref, rows):
    return jnp.concatenate([ref[_chunk(rows, j), :] for j in range(ROW_TILES)], axis=1)


def _write_row_tiles(ref, val, rows):
    for j in range(ROW_TILES):
        ref[_chunk(rows, j), :] = val[:, j * LANES:(j + 1) * LANES]


def _row_tiles(r, n=1):
    return pl.ds(pl.multiple_of(r * ROW_TILES, ROW_TILES), n * ROW_TILES)


def _row_tile(r):
    return _row_tiles(r)


def _ln_in_proj_kernel(xp_ref, xs_ref, g_ref, b_ref, w_ref, h_ref, up_ref, us_ref, fp_ref, fs_ref, zp_ref, zs_ref,
                       *, n_first):
    def phase(x, u_ref, f_ref, z_ref):
        h = _layernorm(x, g_ref[...], b_ref[...])
        h_ref[...] = h
        hb = h.astype(BF16)
        u = _dot(hb, w_ref[:, :D_S5])
        for j in range(D_S5 // LANES):
            u_ref[j] = u[:, j * LANES:(j + 1) * LANES]
        f_ref[...] = _dot(hb, w_ref[:, D_S5:D_S5 + D_HG])
        z_ref[...] = _dot(hb, w_ref[:, D_S5 + D_HG:]).astype(BF16)

    i = pl.program_id(0)
    pl.when(i < n_first)(lambda: phase(xp_ref[0], up_ref, fp_ref, zp_ref))
    pl.when(i >= n_first)(lambda: phase(xs_ref[...], us_ref, fs_ref, zs_ref))


def ln_in_proj(xp, xs, g, b, w_bf16, tm):
    nb, seq, _ = xp.shape
    tp, ts = nb * seq, xs.shape[0]
    n_first = tp // tm
    fixed = lambda i: (0, 0)
    return pl.pallas_call(
        functools.partial(_ln_in_proj_kernel, n_first=n_first),
        grid=((tp + ts) // tm,),
        in_specs=[_prompt_spec(tm, seq, D_MODEL, n_first), _two_phase_specs((tm, D_MODEL), n_first)[1]]
                 + [pl.BlockSpec((1, D_MODEL), fixed), pl.BlockSpec((1, D_MODEL), fixed),
                    pl.BlockSpec((D_MODEL, D_IN), fixed)],
        out_specs=[pl.BlockSpec((tm, D_MODEL), lambda i: (i, 0))]
                  + _two_phase_lane_block_specs(tm, n_first)
                  + _two_phase_specs((tm, D_HG), n_first)
                  + _two_phase_specs((tm, 3 * D_HG), n_first),
        out_shape=[jax.ShapeDtypeStruct((tp + ts, D_MODEL), F32),
                   jax.ShapeDtypeStruct((D_S5 // LANES, tp, LANES), F32),
                   jax.ShapeDtypeStruct((D_S5 // LANES, ts, LANES), F32),
                   jax.ShapeDtypeStruct((tp, D_HG), F32), jax.ShapeDtypeStruct((ts, D_HG), F32),
                   jax.ShapeDtypeStruct((tp, 3 * D_HG), BF16), jax.ShapeDtypeStruct((ts, 3 * D_HG), BF16)],
        compiler_params=_cparams(),
        name="ln_in_proj",
    )(xp, xs, g, b, w_bf16)


def _lane_block(j):
    return slice(j * LANES, (j + 1) * LANES)


def _granule_transpose(slabs):
    per_block = LANES // S5_GROUP
    granule = lax.broadcasted_iota(jnp.int32, (1, LANES), 1) // S5_GROUP
    x = list(slabs)
    for d in (4, 2, 1):
        keep = (granule & d) == 0
        y = [None] * per_block
        for i in range(per_block):
            if i & d == 0:
                y[i] = jnp.where(keep, x[i], pltpu.roll(x[i + d], d * S5_GROUP, axis=1))
                y[i + d] = jnp.where(keep, pltpu.roll(x[i], LANES - d * S5_GROUP, axis=1), x[i + d])
        x = y
    return x


def _sublane_transpose(tiles):
    sub = lax.broadcasted_iota(jnp.int32, (SUBLANES, 1), 0)
    x = list(tiles)
    for d in (4, 2, 1):
        keep = (sub & d) == 0
        y = [None] * SUBLANES
        for i in range(SUBLANES):
            if i & d == 0:
                y[i] = jnp.where(keep, x[i], pltpu.roll(x[i + d], d, axis=0))
                y[i + d] = jnp.where(keep, pltpu.roll(x[i], SUBLANES - d, axis=0), x[i + d])
        x = y
    return x


def _s5_rows_kernel(u_ref, o_ref, *, chunks):
    per_block = LANES // S5_GROUP
    for gcol in range(D_S5 // LANES):
        for j in range(S5_CONV // LANES):
            groups = [_sublane_transpose([u_ref[gcol, pl.ds((cg * SUBLANES + c) * CHUNK + per_block * j, SUBLANES), :]
                                          for c in range(SUBLANES)]) for cg in range(chunks // SUBLANES)]
            by_time = [jnp.concatenate([grp[sl] for grp in groups], axis=0) for sl in range(per_block)]
            for gl, rows in enumerate(_granule_transpose(by_time)):
                o_ref[gcol * per_block + gl, :, _lane_block(j)] = rows.astype(BF16)


def s5_rows(u, chunks):
    nblk, t, _ = u.shape
    assert chunks % SUBLANES == 0
    r = t // CHUNK
    return pl.pallas_call(
        functools.partial(_s5_rows_kernel, chunks=chunks),
        grid=(r // chunks,),
        in_specs=[pl.BlockSpec((nblk, chunks * CHUNK, LANES), lambda i: (0, i, 0))],
        out_specs=pl.BlockSpec((S5_GROUPS, chunks, S5_CONV), lambda i: (0, i, 0)),
        out_shape=jax.ShapeDtypeStruct((S5_GROUPS, r, S5_CONV), BF16),
        compiler_params=_cparams(),
        name="s5_rows",
    )(u)


def _s5_tokens_kernel(y_ref, o_ref, *, chunks):
    per_block = LANES // S5_GROUP
    for gcol in range(D_S5 // LANES):
        for j in range(S5_CONV // LANES):
            by_group = [y_ref[gcol * per_block + gl, :, _lane_block(j)].astype(F32) for gl in range(per_block)]
            by_time = _granule_transpose(by_group)
            for cg in range(chunks // SUBLANES):
                tiles = _sublane_transpose([rows[cg * SUBLANES:(cg + 1) * SUBLANES, :] for rows in by_time])
                for c in range(SUBLANES):
                    o_ref[gcol, pl.ds((cg * SUBLANES + c) * CHUNK + per_block * j, SUBLANES), :] = tiles[c]


def s5_tokens(y_rows, chunks):
    _, r, _ = y_rows.shape
    assert chunks % SUBLANES == 0
    return pl.pallas_call(
        functools.partial(_s5_tokens_kernel, chunks=chunks),
        grid=(r // chunks,),
        in_specs=[pl.BlockSpec((S5_GROUPS, chunks, S5_CONV), lambda i: (0, i, 0))],
        out_specs=pl.BlockSpec((D_S5 // LANES, chunks * CHUNK, LANES), lambda i: (0, i, 0)),
        out_shape=jax.ShapeDtypeStruct((D_S5 // LANES, r * CHUNK, LANES), F32),
        compiler_params=_cparams(),
        name="s5_tokens",
    )(y_rows)


def _s5_prep_kernel(lrc_ref, lic_ref, lrr_ref, lir_ref, ls_ref, brt_ref, bit_ref, ctr_ref, cti_ref,
                    m_ref, w_ref, wc_ref, a_ref):
    step = jnp.exp(ls_ref[0])

    def discretise(lr_raw, li):
        lr = jnp.minimum(lr_raw, -1e-4)
        dr, di = lr * step, li * step
        mag = jnp.exp(dr)
        a_re, a_im = mag * jnp.cos(di), mag * jnp.sin(di)
        den = lr * lr + li * li
        nr = a_re - 1.0
        fr = (nr * lr + a_im * li) / den
        fi = (a_im * lr - nr * li) / den
        return dr, di, fr, fi

    dr_c, di_c, _, _ = discretise(lrc_ref[0], lic_ref[0])
    dr_r, di_r, fr_r, fi_r = discretise(lrr_ref[0], lir_ref[0])

    lane = lax.broadcasted_iota(jnp.int32, (1, S5_CONV), 1)
    t_row = lax.broadcasted_iota(jnp.int32, (1, CHUNK), 1).astype(F32)
    t_col = lax.broadcasted_iota(jnp.int32, (CHUNK, 1), 0).astype(F32)
    lag_of_lane = (lax.broadcasted_iota(jnp.int32, (CHUNK, S5_CONV), 1) // S5_GROUP
                   == lax.broadcasted_iota(jnp.int32, (CHUNK, S5_CONV), 0)).astype(F32)
    time_of_row = (lax.broadcasted_iota(jnp.int32, (S5_CONV, CHUNK), 0) // S5_GROUP
                   == lax.broadcasted_iota(jnp.int32, (S5_CONV, CHUNK), 1)).astype(F32)
    chan_of_lane = (lax.broadcasted_iota(jnp.int32, (S5_GROUP, S5_CONV), 1) % S5_GROUP
                    == lax.broadcasted_iota(jnp.int32, (S5_GROUP, S5_CONV), 0)).astype(F32)
    chan_of_row = (lax.broadcasted_iota(jnp.int32, (S5_CONV, S5_GROUP), 0) % S5_GROUP
                   == lax.broadcasted_iota(jnp.int32, (S5_CONV, S5_GROUP), 1)).astype(F32)
    ctr = _spread_cols(ctr_ref[0], chan_of_lane)
    cti = _spread_cols(cti_ref[0], chan_of_lane)

    def c_times_power(tf):
        mag = jnp.exp(dr_c * tf)
        ang = di_c * tf
        pr = _spread_cols(mag * jnp.cos(ang), lag_of_lane)
        pi = _spread_cols(mag * jnp.sin(ang), lag_of_lane)
        return ctr * pr - cti * pi, ctr * pi + cti * pr

    cpr, cpi = c_times_power(t_row)
    bbr = fr_r * brt_ref[0] - fi_r * bit_ref[0]
    bbi = fr_r * bit_ref[0] + fi_r * brt_ref[0]
    kt = _dot(bbr, cpr, HIGHEST) - _dot(bbi, cpi, HIGHEST)
    for s in range(CHUNK):
        shifted = kt if s == 0 else pltpu.roll(kt, S5_GROUP * s, axis=1)
        m_ref[0, S5_GROUP * s:S5_GROUP * (s + 1), :] = jnp.where(
            lane >= S5_GROUP * s, shifted, 0.0).astype(BF16)

    rem = CHUNK - 1.0 - t_col
    magw = jnp.exp(dr_r * rem)
    angw = di_r * rem
    pwr = _spread_rows(time_of_row, magw * jnp.cos(angw))
    pwi = _spread_rows(time_of_row, magw * jnp.sin(angw))
    bbtr = _spread_rows(chan_of_row, bbr)
    bbti = _spread_rows(chan_of_row, bbi)
    w_ref[0, :, :S5_STATE] = pwr * bbtr - pwi * bbti
    w_ref[0, :, S5_STATE:] = pwr * bbti + pwi * bbtr

    c1r, c1i = c_times_power(t_row + 1.0)
    wc_ref[0, :S5_STATE, :] = c1r.astype(BF16)
    wc_ref[0, S5_STATE:, :] = (-c1i).astype(BF16)

    full = float(CHUNK)
    mag_c = jnp.exp(dr_r * full)
    a_ref[0, 0:1, :] = mag_c * jnp.cos(di_r * full)
    a_ref[0, 1:2, :] = mag_c * jnp.sin(di_r * full)


def s5_prep(lam_re, lam_im, log_step, b_re, b_im, c_re, c_im):
    g, p = lam_re.shape
    brt = jnp.swapaxes(b_re, 1, 2)
    bit = jnp.swapaxes(b_im, 1, 2)
    args = (lam_re.reshape(g, p, 1), lam_im.reshape(g, p, 1),
            lam_re.reshape(g, 1, p), lam_im.reshape(g, 1, p), log_step.reshape(g, 1, 1),
            brt, bit, jnp.swapaxes(c_re, 1, 2), jnp.swapaxes(c_im, 1, 2))
    spec = lambda a: pl.BlockSpec((1,) + a.shape[1:], lambda i: (i, 0, 0))
    out_shape = [jax.ShapeDtypeStruct((g, S5_CONV, S5_CONV), BF16),
                 jax.ShapeDtypeStruct((g, S5_CONV, 2 * S5_STATE), F32),
                 jax.ShapeDtypeStruct((g, 2 * S5_STATE, S5_CONV), BF16),
                 jax.ShapeDtypeStruct((g, 2, S5_STATE), F32)]
    return pl.pallas_call(
        _s5_prep_kernel,
        grid=(g,),
        in_specs=[spec(a) for a in args],
        out_specs=[spec(o) for o in out_shape],
        out_shape=out_shape,
        compiler_params=_cparams(),
        name="s5_prep",
    )(*args)


def _s5_main_kernel(up_ref, us_ref, m_ref, w_ref, wc_ref, a_ref, xsr_ref, xsi_ref,
                    yp_ref, ys_ref, fpr_ref, fpi_ref, fsr_ref, fsi_ref, fin_scr, *, n_prompt, n_chunks):
    w3 = jnp.concatenate(_split3(w_ref[0]), axis=1)
    ar = a_ref[0, 0:1, :]
    ai = a_ref[0, 1:2, :]
    width = 2 * S5_STATE

    def local(u):
        h3 = _dot(u, w3)
        return _dot(u, m_ref[0]), h3[:, :width] + h3[:, width:2 * width] + h3[:, 2 * width:]

    def times(pr, pi, x):
        return (jnp.concatenate([pr, pr], axis=1) * x
                + jnp.concatenate([-pi, pi], axis=1) * pltpu.roll(x, S5_STATE, axis=1))

    y_local, x = local(up_ref[0])
    chunk_of_row = lax.broadcasted_iota(jnp.int32, (n_prompt * n_chunks, 1), 0) % n_chunks
    pr, pi = ar, ai
    d = 1
    while d < n_chunks:
        x = x + times(pr, pi, jnp.where(chunk_of_row >= d, pltpu.roll(x, d, axis=0), 0.0))
        pr, pi = pr * pr - pi * pi, 2.0 * pr * pi
        d *= 2
    fin_scr[...] = x
    last = fin_scr[pl.ds(n_chunks - 1, n_prompt, stride=n_chunks), :]
    fpr_ref[0] = last[:, :S5_STATE]
    fpi_ref[0] = last[:, S5_STATE:]
    x0 = jnp.where(chunk_of_row >= 1, pltpu.roll(x, 1, axis=0), 0.0)
    yp_ref[0] = (y_local + _dot(x0.astype(BF16), wc_ref[0])).astype(BF16)

    y_local, hend = local(us_ref[0])
    x0 = jnp.concatenate([xsr_ref[0], xsi_ref[0]], axis=1)
    fin = times(ar, ai, x0) + hend
    fsr_ref[0] = fin[:, :S5_STATE]
    fsi_ref[0] = fin[:, S5_STATE:]
    ys_ref[0] = (y_local + _dot(x0.astype(BF16), wc_ref[0])).astype(BF16)


def s5_main(up_rows, us_rows, m, w, wc, a, xs_re, xs_im, n_prompt, n_chunks):
    g, r, _ = up_rows.shape
    n_sample = xs_re.shape[1]
    spec = lambda shape: pl.BlockSpec((1,) + tuple(shape[1:]), lambda i: (i, 0, 0))
    args = (up_rows, us_rows, m, w, wc, a, xs_re, xs_im)
    out_shape = [jax.ShapeDtypeStruct((g, r, S5_CONV), BF16),
                 jax.ShapeDtypeStruct((g, n_sample, S5_CONV), BF16),
                 jax.ShapeDtypeStruct((g, n_prompt, S5_STATE), F32),
                 jax.ShapeDtypeStruct((g, n_prompt, S5_STATE), F32),
                 jax.ShapeDtypeStruct((g, n_sample, S5_STATE), F32),
                 jax.ShapeDtypeStruct((g, n_sample, S5_STATE), F32)]
    return pl.pallas_call(
        functools.partial(_s5_main_kernel, n_prompt=n_prompt, n_chunks=n_chunks),
        grid=(g,),
        in_specs=[spec(x.shape) for x in args],
        out_specs=[spec(o.shape) for o in out_shape],
        out_shape=out_shape,
        scratch_shapes=[pltpu.VMEM((r, 2 * S5_STATE), F32)],
        compiler_params=_cparams(),
        name="s5_main",
    )(*args)


def _hgrn_kernel(f_ref, z_ref, s0_ref, lb_ref, ng_ref, o_ref, sfin_ref, st_scr, *, n_seq, chunk):
    c = pl.program_id(1)

    @pl.when(c == 0)
    def _():
        st_scr[...] = s0_ref[...]

    lbw = lb_ref[...]
    lbe = jnp.exp(lbw - jnp.max(lbw, axis=0, keepdims=True))
    lb_all = lbe[0:1, :] / jnp.sum(lbe, axis=0, keepdims=True)

    levels = [chunk >> (i + 1) for i in range(chunk.bit_length() - 1)]
    rowi = lax.broadcasted_iota(jnp.int32, (chunk, chunk), 0)
    coli = lax.broadcasted_iota(jnp.int32, (chunk, chunk), 1)
    rowk = lax.broadcasted_iota(jnp.int32, (chunk, HG_D), 0)
    sign, valid = [], []
    for m in levels:
        sign.append(jnp.where((rowk % (2 * m)) >= m, 1.0, -1.0))
        valid.append(((rowi // (2 * m)) == (coli // (2 * m)))
                     & ((rowi % (2 * m)) >= m) & ((coli % (2 * m)) < m))
    cum_mat = (coli <= rowi).astype(BF16)
    cum_mat3 = jnp.concatenate([cum_mat] * 3, axis=1)
    diag = rowi == coli
    nt = (((1,), (1,)), ((), ()))

    def body(n, carry):
        zf = f_ref[n]
        fg_all = lb_all + (1.0 - lb_all) * jax.nn.sigmoid(zf)
        cums = _dot(cum_mat3, jnp.concatenate(_split3(jnp.log2(fg_all)), axis=0))
        for hd in range(HG_HEADS):
            cols = slice(hd * HG_D, (hd + 1) * HG_D)
            zq = z_ref[n, :, hd * HG_D:(hd + 1) * HG_D].astype(F32)
            vb = z_ref[n, :, D_HG + hd * HG_D:D_HG + (hd + 1) * HG_D]
            zg = z_ref[n, :, 2 * D_HG + hd * HG_D:2 * D_HG + (hd + 1) * HG_D].astype(F32)
            q = zq * jax.nn.sigmoid(zq)
            kk = 1.0 - fg_all[:, cols]
            bcum = cums[:chunk, cols]
            b_last = bcum[chunk - 1:chunk, :]
            qb = q.astype(BF16)
            kb = kk.astype(BF16)
            st = st_scr[n, hd]

            scores = jnp.where(diag, lax.dot_general(qb, kb, nt, preferred_element_type=F32), 0.0)
            for lvl, m in enumerate(levels):
                if 2 * m >= SUBLANES:
                    bref = jnp.concatenate(
                        [jnp.broadcast_to(bcum[b * 2 * m + m - 1:b * 2 * m + m, :], (2 * m, HG_D))
                         for b in range(chunk // (2 * m))], axis=0)
                else:
                    offs = rowk % (2 * m) - (m - 1)
                    bref = bcum
                    for o in range(-(m - 1), m + 1):
                        if o != 0:
                            bref = jnp.where(offs == o, pltpu.roll(bcum, o % chunk, axis=0), bref)
                dec = jnp.exp2((bcum - bref) * sign[lvl]).astype(BF16)
                sc = lax.dot_general(qb * dec, kb * dec, nt, preferred_element_type=F32)
                scores = jnp.where(valid[lvl], sc, scores)

            qd = (q * jnp.exp2(bcum)).astype(BF16)
            o = lax.dot_general(qd, st.astype(BF16), nt, preferred_element_type=F32)
            o = o + _dot(scores.astype(BF16), vb)
            kdec = (kk * jnp.exp2(b_last - bcum)).astype(BF16)
            st_scr[n, hd] = jnp.exp2(b_last) * st + lax.dot_general(
                vb, kdec, (((0,), (0,)), ((), ())), preferred_element_type=F32)

            on = o * lax.rsqrt(jnp.mean(o * o, axis=-1, keepdims=True) + RMS_EPS) * ng_ref[:, cols]
            o_ref[n, :, hd * HG_D:(hd + 1) * HG_D] = on * (zg * jax.nn.sigmoid(zg))
        return carry

    lax.fori_loop(0, n_seq, body, 0, unroll=True)

    @pl.when(c == pl.num_programs(1) - 1)
    def _():
        sfin_ref[...] = st_scr[...]


def hgrn(f, z, s0_t, hg_lb, norm_g, n_seq, chunk):
    n, length, _ = z.shape
    return pl.pallas_call(
        functools.partial(_hgrn_kernel, n_seq=n_seq, chunk=chunk),
        grid=(n // n_seq, length // chunk),
        in_specs=[pl.BlockSpec((n_seq, chunk, D_HG), lambda g, c: (g, c, 0)),
                  pl.BlockSpec((n_seq, chunk, 3 * D_HG), lambda g, c: (g, c, 0)),
                  pl.BlockSpec((n_seq, HG_HEADS, HG_D, HG_D), lambda g, c: (g, 0, 0, 0)),
                  pl.BlockSpec(hg_lb.shape, lambda g, c: (0, 0)),
                  pl.BlockSpec((1, D_HG), lambda g, c: (0, 0))],
        out_specs=[pl.BlockSpec((n_seq, chunk, D_HG), lambda g, c: (g, c, 0)),
                   pl.BlockSpec((n_seq, HG_HEADS, HG_D, HG_D), lambda g, c: (g, 0, 0, 0))],
        out_shape=[jax.ShapeDtypeStruct((n, length, D_HG), F32),
                   jax.ShapeDtypeStruct((n, HG_HEADS, HG_D, HG_D), F32)],
        scratch_shapes=[pltpu.VMEM((n_seq, HG_HEADS, HG_D, HG_D), F32)],
        compiler_params=_cparams(2),
        name="hgrn",
    )(f, z, s0_t, hg_lb, norm_g)


def _post_mix_kernel(h_ref, up_ref, us_ref, yp_ref, ys_ref, hgp_ref, hgs_ref,
                     d_ref, wglu_ref, bglu_ref, s5g_ref, wout_ref, g1_ref, b1_ref, rwt_ref, rb_ref,
                     h1_ref, slot_ref, gate_ref, before_ref, cnt_ref, run_scr, *, tm, n_first):
    i = pl.program_id(0)

    @pl.when(i == 0)
    def _():
        run_scr[...] = jnp.zeros_like(run_scr)

    def phase(u_ref, y_ref, hg_ref):
        ys = _read_lane_blocks(y_ref) + d_ref[...] * _read_lane_blocks(u_ref)
        gl = 0.5 * ys * (1.0 + lax.erf(ys * (2.0 ** -0.5)))
        s5o = gl * jax.nn.sigmoid(_dot(gl.astype(BF16), wglu_ref[...]) + bglu_ref[...])
        s5o = s5o * lax.rsqrt(jnp.mean(s5o * s5o, axis=-1, keepdims=True) + RMS_EPS) * s5g_ref[...]
        mix = (_dot(s5o.astype(BF16), wout_ref[:D_S5, :])
               + _dot(hg_ref[...].astype(BF16), wout_ref[D_S5:, :]))
        h1 = _layernorm(DEEPNORM_ALPHA * h_ref[...] + mix, g1_ref[...], b1_ref[...])
        _write_row_tiles(h1_ref, h1, tm)

        h_hi, h_mid, _ = _split3(h1)
        w_hi, w_mid, _ = _split3(rwt_ref[...])
        nt = (((1,), (1,)), ((), ()))
        logits = (lax.dot_general(w_hi, h_hi, nt, preferred_element_type=F32)
                  + lax.dot_general(w_hi, h_mid, nt, preferred_element_type=F32)
                  + lax.dot_general(w_mid, h_hi, nt, preferred_element_type=F32)) + rb_ref[...]
        eid = lax.broadcasted_iota(jnp.int32, (N_EXPERTS, tm), 0)
        vals, idxs = [], []
        for _ in range(TOP_K):
            m = jnp.max(logits, axis=0, keepdims=True)
            ix = jnp.min(jnp.where(logits == m, eid, N_EXPERTS), axis=0, keepdims=True)
            vals.append(m)
            idxs.append(ix)
            logits = jnp.where(eid == ix, -jnp.inf, logits)
        exps = [jnp.exp(v - vals[0]) for v in vals]
        den = exps[0] + exps[1] + exps[2] + exps[3]

        onehot = jnp.zeros((N_EXPERTS, tm), F32)
        for ix in idxs:
            onehot = onehot + (eid == ix).astype(F32)
        rowi = lax.broadcasted_iota(jnp.int32, (tm, tm), 0)
        coli = lax.broadcasted_iota(jnp.int32, (tm, tm), 1)
        earlier = (rowi < coli).astype(BF16)
        prefix = _dot(onehot.astype(BF16), earlier)
        tile_cnt = jnp.sum(onehot, axis=1, keepdims=True)
        for k in range(TOP_K):
            lower_experts = jnp.sum(jnp.where(eid < idxs[k], tile_cnt, 0.0), axis=0, keepdims=True)
            rank = jnp.sum(jnp.where(eid == idxs[k], prefix, 0.0), axis=0, keepdims=True)
            slot_ref[0, :, k * tm:(k + 1) * tm] = (lower_experts + rank).astype(jnp.int32) * ROW_TILES
            gate_ref[0, :, k * tm:(k + 1) * tm] = exps[k] / den
        before_ref[0] = run_scr[...]
        run_scr[...] = run_scr[...] + tile_cnt
        cnt_ref[...] = run_scr[...]

    pl.when(i < n_first)(lambda: phase(up_ref, yp_ref, hgp_ref))
    pl.when(i >= n_first)(lambda: phase(us_ref, ys_ref, hgs_ref))


def post_mix(h0, u_pair, y_pair, hg_pair, d_skip, wglu, bglu, s5g, wout, g1, b1, rw_t, rb_col, tm):
    t = h0.shape[0]
    n_first = u_pair[0].shape[1] // tm
    row = lambda i: (i, 0)
    fixed = lambda i: (0, 0)
    full = lambda a: pl.BlockSpec(a.shape, fixed)
    weights = (d_skip, wglu, bglu, s5g, wout, g1, b1, rw_t, rb_col)
    return pl.pallas_call(
        functools.partial(_post_mix_kernel, tm=tm, n_first=n_first),
        grid=(t // tm,),
        in_specs=[pl.BlockSpec((tm, D_MODEL), row)]
                 + _two_phase_lane_block_specs(tm, n_first) * 2 + _two_phase_specs((tm, D_HG), n_first)
                 + [full(a) for a in weights],
        out_specs=[pl.BlockSpec((tm * ROW_TILES, LANES), row),
                   pl.BlockSpec((1, 1, TOP_K * tm), lambda i: (i, 0, 0)),
                   pl.BlockSpec((1, 1, TOP_K * tm), lambda i: (i, 0, 0)),
                   pl.BlockSpec((1, N_EXPERTS, 1), lambda i: (i, 0, 0)),
                   pl.BlockSpec((N_EXPERTS, 1), fixed)],
        out_shape=[jax.ShapeDtypeStruct((t * ROW_TILES, LANES), F32),
                   jax.ShapeDtypeStruct((t // tm, 1, TOP_K * tm), jnp.int32),
                   jax.ShapeDtypeStruct((t // tm, 1, TOP_K * tm), F32),
                   jax.ShapeDtypeStruct((t // tm, N_EXPERTS, 1), F32),
                   jax.ShapeDtypeStruct((N_EXPERTS, 1), F32)],
        scratch_shapes=[pltpu.VMEM((N_EXPERTS, 1), F32)],
        compiler_params=_cparams(),
        name="post_mix",
    )(h0, *u_pair, *y_pair, *hg_pair, *weights)


def _segment_copies(meta_ref, tile, tm, make_copy):
    for e in range(N_EXPERTS):
        sorted_row = meta_ref[tile, e]
        cnt = meta_ref[tile, N_EXPERTS + e]
        staged_row = meta_ref[tile, 2 * N_EXPERTS + e]

        def move_bits(bits, e=e, cnt=cnt, sorted_row=sorted_row, staged_row=staged_row):
            for b in bits:
                done = cnt & ((1 << b) - 1)

                @pl.when(((cnt >> b) & 1) == 1)
                def _(b=b, done=done):
                    make_copy(staged_row + done, sorted_row + done, 1 << b).start(priority=e % 2)

        n_bits = tm.bit_length()
        n_low = min(n_bits, 7)
        move_bits(range(n_low))
        if n_bits > n_low:
            pl.when(cnt >= (1 << n_low))(lambda move_bits=move_bits: move_bits(range(n_low, n_bits)))


def _dispatch_kernel(meta_ref, pend_ref, slot_ref, h_ref, xs_ref, stage, zero_scr, zsem, sem, *, tm):
    n_rows = xs_ref.shape[0] // ROW_TILES
    i = pl.program_id(0)
    n = pl.num_programs(0)
    slot = i % 2

    def drain(s):
        pltpu.make_async_copy(stage.at[s], xs_ref.at[pl.ds(0, TOP_K * tm * ROW_TILES)], sem.at[s]).wait()

    @pl.when(i == 0)
    def _():
        zero_scr[...] = jnp.zeros_like(zero_scr)

        def last_block(e):
            prev = pend_ref[e - 1] if e > 0 else 0
            copy = pltpu.make_async_copy(
                zero_scr, xs_ref.at[pl.ds(pl.multiple_of(jnp.maximum(pend_ref[e] - MOE_ROWS, 0) * ROW_TILES,
                                                         ROW_TILES), MOE_ROWS * ROW_TILES)], zsem)
            return pend_ref[e] > prev, copy

        def tail_block(j):
            row0 = pend_ref[N_EXPERTS - 1] + j * MOE_ROWS
            copy = pltpu.make_async_copy(
                zero_scr, xs_ref.at[pl.ds(pl.multiple_of(jnp.minimum(row0, n_rows - MOE_ROWS) * ROW_TILES,
                                                         ROW_TILES), MOE_ROWS * ROW_TILES)], zsem)
            return row0 < n_rows, copy

        blocks = [last_block(e) for e in range(N_EXPERTS)] + [tail_block(j) for j in range(N_EXPERTS)]
        for used, copy in blocks:
            pl.when(used)(copy.start)
        for used, copy in blocks:
            pl.when(used)(copy.wait)

    pl.when(i >= 2)(lambda: drain(slot))

    def regroup(s):
        def body(t, carry):
            row = h_ref[_row_tile(t), :]
            for k in range(TOP_K):
                stage[s, pl.ds(pl.multiple_of(slot_ref[k * tm + t], ROW_TILES), ROW_TILES), :] = row
            return carry

        lax.fori_loop(0, tm, body, 0, unroll=8)

    pl.when(slot == 0)(lambda: regroup(0))
    pl.when(slot == 1)(lambda: regroup(1))
    _segment_copies(meta_ref, i, tm, lambda staged_row, sorted_row, rows: pltpu.make_async_copy(
        stage.at[slot, _row_tiles(staged_row, rows)], xs_ref.at[_row_tiles(sorted_row, rows)], sem.at[slot]))

    @pl.when(i == n - 1)
    def _():
        pl.when(n >= 2)(lambda: drain(1 - slot))
        drain(slot)


def dispatch(meta, pend, slots_flat, h1, n_rows, tm):
    t = h1.shape[0] // ROW_TILES
    grid_spec = pltpu.PrefetchScalarGridSpec(
        num_scalar_prefetch=2,
        grid=(t // tm,),
        in_specs=[pl.BlockSpec((TOP_K * tm,), lambda i, meta, pend: (i,), memory_space=pltpu.SMEM),
                  pl.BlockSpec((tm * ROW_TILES, LANES), lambda i, meta, pend: (i, 0))],
        out_specs=pl.BlockSpec(memory_space=pl.ANY),
        scratch_shapes=[pltpu.VMEM((2, TOP_K * tm * ROW_TILES, LANES), F32),
                        pltpu.VMEM((MOE_ROWS * ROW_TILES, LANES), F32),
                        pltpu.SemaphoreType.DMA(()), pltpu.SemaphoreType.DMA((2,))],
    )
    return pl.pallas_call(
        functools.partial(_dispatch_kernel, tm=tm),
        grid_spec=grid_spec,
        out_shape=jax.ShapeDtypeStruct((n_rows * ROW_TILES, LANES), F32),
        compiler_params=_cparams(),
        name="moe_dispatch",
    )(meta, pend, slots_flat, h1)


def _moe_ffn_kernel(be_ref, nu_ref, seg_ref, nxt_ref, x_ref, wg_ref, bg_ref, wu_ref, bu_ref, wd_ref, bd_ref,
                    y_ref, wbuf, wbf, sem):
    i = pl.program_id(0)
    hbm = (wg_ref, wu_ref, wd_ref)

    def weight_copies(expert, s):
        return [pltpu.make_async_copy(hbm[j].at[expert], wbuf.at[s, j], sem.at[s, j]) for j in range(3)]

    @pl.when((i == 0) | (be_ref[i] != be_ref[jnp.maximum(i - 1, 0)]))
    def _():
        s = seg_ref[i] % 2

        @pl.when(i == 0)
        def _():
            for c in weight_copies(be_ref[0], 0):
                c.start()

        for j, c in enumerate(weight_copies(be_ref[i], s)):
            c.wait()
            wbf[j] = wbuf[s, j].astype(BF16)

        @pl.when(nxt_ref[i] >= 0)
        def _():
            for c in weight_copies(nxt_ref[i], 1 - s):
                c.start()

    @pl.when(i < nu_ref[0])
    def _():
        x = _read_row_tiles(x_ref, MOE_ROWS).astype(BF16)
        gt = jnp.minimum(_dot(x, wbf[0]) + bg_ref[0], SWIGLU_LIMIT)
        up = jnp.clip(_dot(x, wbf[1]) + bu_ref[0], -SWIGLU_LIMIT, SWIGLU_LIMIT)
        hid = (up + 1.0) * (gt * jax.nn.sigmoid(SWIGLU_ALPHA * gt))
        _write_row_tiles(y_ref, _dot(hid.astype(BF16), wbf[2]) + bd_ref[0], MOE_ROWS)

    @pl.when(i >= nu_ref[0])
    def _():
        y_ref[...] = jnp.zeros_like(y_ref)


def moe_ffn(block_e, n_used, segment, next_e, xs, wg, bg, wu, bu, wd, bd):
    n_rows = xs.shape[0] // ROW_TILES
    n_blocks = n_rows // MOE_ROWS
    wsel = lambda i, be, nu, seg, nxt: (be[i], 0, 0)
    d_ff = wg.shape[-1]
    assert wg.shape[1:] == wu.shape[1:] == wd.shape[1:] == (D_MODEL, D_MODEL)
    anywhere = pl.BlockSpec(memory_space=pl.ANY)
    grid_spec = pltpu.PrefetchScalarGridSpec(
        num_scalar_prefetch=4,
        grid=(n_blocks,),
        in_specs=[pl.BlockSpec((MOE_ROWS * ROW_TILES, LANES),
                               lambda i, be, nu, seg, nxt: (jnp.minimum(i, nu[0] - 1), 0)),
                  anywhere, pl.BlockSpec((1, 1, d_ff), wsel),
                  anywhere, pl.BlockSpec((1, 1, d_ff), wsel),
                  anywhere, pl.BlockSpec((1, 1, D_MODEL), wsel)],
        out_specs=pl.BlockSpec((MOE_ROWS * ROW_TILES, LANES), lambda i, be, nu, seg, nxt: (i, 0)),
        scratch_shapes=[pltpu.VMEM((2, 3, D_MODEL, D_MODEL), F32), pltpu.VMEM((3, D_MODEL, D_MODEL), BF16),
                        pltpu.SemaphoreType.DMA((2, 3))],
    )
    return pl.pallas_call(
        _moe_ffn_kernel,
        grid_spec=grid_spec,
        out_shape=jax.ShapeDtypeStruct((n_rows * ROW_TILES, LANES), F32),
        compiler_params=_cparams(),
        name="moe_ffn",
    )(block_e, n_used, segment, next_e, xs, wg, bg, wu, bu, wd, bd)


def _combine_kernel(meta_ref, slot_ref, gate_ref, h_ref, pp_ref, ps_ref, yb_ref,
                    plew_ref, plegw_ref, g2_ref, b2_ref, outp_ref, outs_ref, buf, r_scr, sem,
                    *, tm, n_first):
    i = pl.program_id(0)
    n = pl.num_programs(0)
    slot = i % 2

    def fetch(tile, s):
        _segment_copies(meta_ref, tile, tm, lambda staged_row, sorted_row, rows: pltpu.make_async_copy(
            yb_ref.at[_row_tiles(sorted_row, rows)], buf.at[s, _row_tiles(staged_row, rows)], sem.at[s]))

    pl.when(i == 0)(lambda: fetch(0, 0))
    pl.when(i + 1 < n)(lambda: fetch(i + 1, 1 - slot))
    pltpu.make_async_copy(yb_ref.at[pl.ds(0, TOP_K * tm * ROW_TILES)], buf.at[slot], sem.at[slot]).wait()

    def weighted_sum(s):
        def body(t, carry):
            acc = DEEPNORM_ALPHA * h_ref[_row_tile(t), :]
            for k in range(TOP_K):
                acc = acc + gate_ref[k * tm + t] * buf[s, pl.ds(pl.multiple_of(slot_ref[k * tm + t], ROW_TILES),
                                                               ROW_TILES), :]
            r_scr[_row_tile(t), :] = acc
            return carry

        lax.fori_loop(0, tm, body, 0, unroll=8)

    pl.when(slot == 0)(lambda: weighted_sum(0))
    pl.when(slot == 1)(lambda: weighted_sum(1))
    r = _read_row_tiles(r_scr, tm)
    gate = jax.nn.sigmoid(_dot(r.astype(BF16), plegw_ref[...]))

    def finish(p, store):
        e = _dot(p.astype(BF16), plew_ref[...]) * gate
        store(_layernorm(r + e, g2_ref[...], b2_ref[...]))

    def store_prompt(v):
        outp_ref[0] = v

    def store_sample(v):
        outs_ref[...] = v

    pl.when(i < n_first)(lambda: finish(pp_ref[0], store_prompt))
    pl.when(i >= n_first)(lambda: finish(ps_ref[...], store_sample))


def combine(meta, slots_flat, gates_flat, h1, p_prompt, p_sample, yb, plew, plegw, g2, b2, tm):
    t = h1.shape[0] // ROW_TILES
    nb, seq, _ = p_prompt.shape
    ts = p_sample.shape[0]
    n_first = nb * seq // tm
    fixed = lambda i, meta: (0, 0)
    flat = pl.BlockSpec((TOP_K * tm,), lambda i, meta: (i,), memory_space=pltpu.SMEM)
    sample = lambda width: pl.BlockSpec((tm, width), lambda i, meta: (jnp.maximum(i - n_first, 0), 0))
    prompt = lambda width: pl.BlockSpec((1, tm, width), _prompt_spec(tm, seq, width, n_first).index_map)
    with_meta = lambda spec: pl.BlockSpec(spec.block_shape, lambda i, meta: spec.index_map(i))
    grid_spec = pltpu.PrefetchScalarGridSpec(
        num_scalar_prefetch=1,
        grid=(t // tm,),
        in_specs=[flat, flat,
                  pl.BlockSpec((tm * ROW_TILES, LANES), lambda i, meta: (i, 0)),
                  with_meta(prompt(PLE_DIM)), sample(PLE_DIM),
                  pl.BlockSpec(memory_space=pl.ANY),
                  pl.BlockSpec(plew.shape, fixed), pl.BlockSpec(plegw.shape, fixed),
                  pl.BlockSpec((1, D_MODEL), fixed), pl.BlockSpec((1, D_MODEL), fixed)],
        out_specs=[with_meta(prompt(D_MODEL)), sample(D_MODEL)],
        scratch_shapes=[pltpu.VMEM((2, TOP_K * tm * ROW_TILES, LANES), F32),
                        pltpu.VMEM((tm * ROW_TILES, LANES), F32),
                        pltpu.SemaphoreType.DMA((2,))],
    )
    return pl.pallas_call(
        functools.partial(_combine_kernel, tm=tm, n_first=n_first),
        grid_spec=grid_spec,
        out_shape=[jax.ShapeDtypeStruct((nb, seq, D_MODEL), F32), jax.ShapeDtypeStruct((ts, D_MODEL), F32)],
        compiler_params=_cparams(),
        name="moe_combine",
    )(meta, slots_flat, gates_flat, h1, p_prompt, p_sample, yb, plew, plegw, g2, b2)


def _row(v):
    return v.reshape(1, -1)


def kernel(x_prompt, x_sample, state_s5_re, state_s5_im, state_hgrn, p_prompt, p_sample, ln_in_g, ln_in_b, w_in, s5_lambda_re, s5_lambda_im, s5_log_step, s5_b_re, s5_b_im, s5_c_re, s5_c_im, s5_d, s5_w_glu, s5_b_glu, s5_norm_g, hg_lb, hg_norm_g, w_out, ln1_g, ln1_b, router_w, router_b, w_gate, b_gate, w_up, b_up, w_down, b_down, ple_w, ple_gate_w, ln2_g, ln2_b):
    nb, seq, _ = x_prompt.shape
    ns, dseq, _ = x_sample.shape
    assert dseq == CHUNK and seq % CHUNK == 0 and w_in.shape[0] == 1
    nc = seq // CHUNK
    tp, ts = nb * seq, ns * dseq
    t = tp + ts
    tm = 512 if (tp % 512 == 0 and ts % 512 == 0) else 256
    assert tp % tm == 0 and ts % tm == 0 and seq % tm == 0 and ns % nb == 0

    w_cols = jnp.split(w_in[0], [D_S5, D_S5 + D_HG, D_S5 + 2 * D_HG], axis=1)
    h0, u_p, u_s, f_p, f_s, z_p, z_s = ln_in_proj(
        x_prompt, x_sample.reshape(ts, D_MODEL), _row(ln_in_g), _row(ln_in_b),
        jnp.concatenate([w_cols[0], w_cols[2], w_cols[1], w_cols[3]], axis=1).astype(BF16), tm)

    m, w, wc, a = s5_prep(s5_lambda_re[0], s5_lambda_im[0], s5_log_step[0],
                          s5_b_re[0], s5_b_im[0], s5_c_re[0], s5_c_im[0])
    rb = 64 if (nb * nc) % 64 == 0 else nc
    yp_rows, ys_rows, fpr, fpi, fsr, fsi = s5_main(s5_rows(u_p, rb), s5_rows(u_s, ns), m, w, wc, a,
                                                   jnp.swapaxes(state_s5_re[0], 0, 1),
                                                   jnp.swapaxes(state_s5_im[0], 0, 1), nb, nc)
    y_pair = (s5_tokens(yp_rows, rb), s5_tokens(ys_rows, ns))

    zero_state = jnp.zeros((nb, HG_HEADS, HG_D, HG_D), F32)
    ng = _row(hg_norm_g[0])
    o_p, st_p = hgrn(f_p.reshape(nb, seq, D_HG), z_p.reshape(nb, seq, 3 * D_HG), zero_state, hg_lb, ng, nb,
                     HG_CHUNK if seq % HG_CHUNK == 0 else CHUNK)
    o_s, st_s = hgrn(f_s.reshape(ns, dseq, D_HG), z_s.reshape(ns, dseq, 3 * D_HG), jnp.swapaxes(state_hgrn[0], 2, 3),
                     hg_lb, ng, nb, dseq)

    h1, slots, gates, before, counts = post_mix(
        h0, (u_p, u_s), y_pair, (o_p.reshape(tp, D_HG), o_s.reshape(ts, D_HG)),
        _row(s5_d[0]), s5_w_glu[0].astype(BF16), _row(s5_b_glu[0]),
        _row(s5_norm_g[0]), w_out[0].astype(BF16), _row(ln1_g[0]), _row(ln1_b[0]),
        router_w[0].T, router_b[0].reshape(N_EXPERTS, 1), tm)

    counts = counts[:, 0].astype(jnp.int32)
    before = before[:, :, 0].astype(jnp.int32)
    cnt = jnp.concatenate([before[1:], counts[None]], axis=0) - before
    padded = (counts + MOE_ROWS - 1) // MOE_ROWS * MOE_ROWS
    pend = jnp.cumsum(padded)
    staged = jnp.cumsum(cnt, axis=1) - cnt
    meta = jnp.concatenate([pend - padded + before, cnt, staged, jnp.zeros_like(cnt)], axis=1)
    experts = jnp.arange(N_EXPERTS, dtype=jnp.int32)

    n_blocks = -(-t * TOP_K // MOE_ROWS) + N_EXPERTS
    n_used = (pend[-1] // MOE_ROWS).astype(jnp.int32)
    blk = jnp.arange(n_blocks, dtype=jnp.int32)
    blk = jnp.minimum(blk, n_used - 1)
    block_e = jnp.sum((pend[None, :] <= (blk * MOE_ROWS)[:, None]).astype(jnp.int32), axis=1)
    block_e = jnp.minimum(block_e, N_EXPERTS - 1)
    owns_rows = padded > 0
    owner = owns_rows[None, :]
    segment = jnp.sum((owner & (experts[None, :] <= block_e[:, None])).astype(jnp.int32), axis=1) - 1
    later = jnp.where(owner & (experts[None, :] > block_e[:, None]), experts[None, :], N_EXPERTS)
    next_e = jnp.min(later, axis=1)
    next_e = jnp.where(next_e < N_EXPERTS, next_e, -1).astype(jnp.int32)

    slots_flat = slots.reshape(-1)
    xs = dispatch(meta, pend, slots_flat, h1, n_blocks * MOE_ROWS, tm)
    yb = moe_ffn(block_e, n_used.reshape(1), segment, next_e, xs,
                 w_gate[0], b_gate[0][:, None, :], w_up[0], b_up[0][:, None, :],
                 w_down[0], b_down[0][:, None, :])
    out_p, out_s = combine(meta, slots_flat, gates.reshape(-1), h1, p_prompt[0], p_sample[0].reshape(ts, PLE_DIM),
                           yb, ple_w[0].astype(BF16), ple_gate_w[0].astype(BF16),
                           _row(ln2_g[0]), _row(ln2_b[0]), tm)

    def s5_state(f, n):
        return jnp.swapaxes(f, 0, 1).reshape(1, n, S5_GROUPS, S5_STATE)

    return (out_p, out_s.reshape(ns, dseq, D_MODEL),
            s5_state(fpr, nb), s5_state(fpi, nb), jnp.swapaxes(st_p, 2, 3)[None],
            s5_state(fsr, ns), s5_state(fsi, ns), jnp.swapaxes(st_s, 2, 3)[None])
```

```python
import functools

import jax
import jax.numpy as jnp
from jax import lax
from jax.experimental import pallas as pl
from jax.experimental.pallas import tpu as pltpu

F32 = jnp.float32
BF16 = jnp.bfloat16
HIGHEST = lax.Precision.HIGHEST

D_MODEL = 1024
CHUNK = 64
PLE_DIM = 256
D_S5 = 512
S5_GROUP = 16
S5_GROUPS = 32
S5_STATE = 64
D_HG = 512
HG_HEADS = 4
HG_D = 128
D_IN = D_S5 + 4 * D_HG
N_EXPERTS = 32
TOP_K = 4
SWIGLU_LIMIT = 7.0
SWIGLU_ALPHA = 1.702
DEEPNORM_ALPHA = 2.0 ** 0.25
LN_EPS = 1e-5
RMS_EPS = 1e-6

LANES = 128
SUBLANES = 8
ROW_TILES = D_MODEL // LANES
S5_CONV = CHUNK * S5_GROUP
HG_CHUNK = 128
MOE_ROWS = 512
VMEM_LIMIT = 56 * 1024 * 1024

assert ROW_TILES == SUBLANES


def _cparams(n_axes=1):
    return pltpu.CompilerParams(dimension_semantics=("arbitrary",) * n_axes,
                                vmem_limit_bytes=VMEM_LIMIT)


def _dot(a, b, precision=None):
    return jnp.dot(a, b, preferred_element_type=F32, precision=precision)


def _split3(w):
    hi = w.astype(BF16)
    r1 = w - hi.astype(F32)
    mid = r1.astype(BF16)
    lo = (r1 - mid.astype(F32)).astype(BF16)
    return hi, mid, lo


def _spread_cols(table, onehot):
    return _dot(jnp.concatenate(_split3(table), axis=1), jnp.concatenate([onehot.astype(BF16)] * 3, axis=0))


def _spread_rows(onehot, table):
    return _dot(jnp.concatenate([onehot.astype(BF16)] * 3, axis=1), jnp.concatenate(_split3(table), axis=0))


def _layernorm(x, g, b):
    mu = jnp.mean(x, axis=-1, keepdims=True)
    xc = x - mu
    var = jnp.mean(xc * xc, axis=-1, keepdims=True)
    return xc * lax.rsqrt(var + LN_EPS) * g + b


def _two_phase_specs(block, n_first):
    nd = len(block)
    first = pl.BlockSpec(block, lambda i: (jnp.minimum(i, n_first - 1),) + (0,) * (nd - 1))
    second = pl.BlockSpec(block, lambda i: (jnp.maximum(i - n_first, 0),) + (0,) * (nd - 1))
    return [first, second]


def _two_phase_lane_block_specs(tm, n_first):
    nblk = D_S5 // LANES
    first = pl.BlockSpec((nblk, tm, LANES), lambda i: (0, jnp.minimum(i, n_first - 1), 0))
    second = pl.BlockSpec((nblk, tm, LANES), lambda i: (0, jnp.maximum(i - n_first, 0), 0))
    return [first, second]


def _read_lane_blocks(ref):
    return jnp.concatenate([ref[j] for j in range(D_S5 // LANES)], axis=1)


def _prompt_spec(tm, seq, width, n_first):
    per_seq = seq // tm

    def index(i):
        ic = jnp.minimum(i, n_first - 1)
        return (ic // per_seq, ic % per_seq, 0)

    return pl.BlockSpec((1, tm, width), index)


def _chunk(rows, j):
    return pl.ds(j, rows, stride=ROW_TILES)


def _read_row_tiles(ref, rows):
    return jnp.concatenate([ref[_chunk(rows, j), :] for j in range(ROW_TILES)], axis=1)


def _write_row_tiles(ref, val, rows):
    for j in range(ROW_TILES):
        ref[_chunk(rows, j), :] = val[:, j * LANES:(j + 1) * LANES]


def _row_tiles(r, n=1):
    return pl.ds(pl.multiple_of(r * ROW_TILES, ROW_TILES), n * ROW_TILES)


def _row_tile(r):
    return _row_tiles(r)


def _ln_in_proj_kernel(xp_ref, xs_ref, g_ref, b_ref, w_ref, h_ref, up_ref, us_ref, fp_ref, fs_ref, zp_ref, zs_ref,
                       *, n_first):
    def phase(x, u_ref, f_ref, z_ref):
        h = _layernorm(x, g_ref[...], b_ref[...])
        h_ref[...] = h
        hb = h.astype(BF16)
        u = _dot(hb, w_ref[:, :D_S5])
        for j in range(D_S5 // LANES):
            u_ref[j] = u[:, j * LANES:(j + 1) * LANES]
        f_ref[...] = _dot(hb, w_ref[:, D_S5:D_S5 + D_HG])
        z_ref[...] = _dot(hb, w_ref[:, D_S5 + D_HG:]).astype(BF16)

    i = pl.program_id(0)
    pl.when(i < n_first)(lambda: phase(xp_ref[0], up_ref, fp_ref, zp_ref))
    pl.when(i >= n_first)(lambda: phase(xs_ref[...], us_ref, fs_ref, zs_ref))


def ln_in_proj(xp, xs, g, b, w_bf16, tm):
    nb, seq, _ = xp.shape
    tp, ts = nb * seq, xs.shape[0]
    n_first = tp // tm
    fixed = lambda i: (0, 0)
    return pl.pallas_call(
        functools.partial(_ln_in_proj_kernel, n_first=n_first),
        grid=((tp + ts) // tm,),
        in_specs=[_prompt_spec(tm, seq, D_MODEL, n_first), _two_phase_specs((tm, D_MODEL), n_first)[1]]
                 + [pl.BlockSpec((1, D_MODEL), fixed), pl.BlockSpec((1, D_MODEL), fixed),
                    pl.BlockSpec((D_MODEL, D_IN), fixed)],
        out_specs=[pl.BlockSpec((tm, D_MODEL), lambda i: (i, 0))]
                  + _two_phase_lane_block_specs(tm, n_first)
                  + _two_phase_specs((tm, D_HG), n_first)
                  + _two_phase_specs((tm, 3 * D_HG), n_first),
        out_shape=[jax.ShapeDtypeStruct((tp + ts, D_MODEL), F32),
                   jax.ShapeDtypeStruct((D_S5 // LANES, tp, LANES), F32),
                   jax.ShapeDtypeStruct((D_S5 // LANES, ts, LANES), F32),
                   jax.ShapeDtypeStruct((tp, D_HG), F32), jax.ShapeDtypeStruct((ts, D_HG), F32),
                   jax.ShapeDtypeStruct((tp, 3 * D_HG), BF16), jax.ShapeDtypeStruct((ts, 3 * D_HG), BF16)],
        compiler_params=_cparams(),
        name="ln_in_proj",
    )(xp, xs, g, b, w_bf16)


def _lane_block(j):
    return slice(j * LANES, (j + 1) * LANES)


def _granule_transpose(slabs):
    per_block = LANES // S5_GROUP
    granule = lax.broadcasted_iota(jnp.int32, (1, LANES), 1) // S5_GROUP
    x = list(slabs)
    for d in (4, 2, 1):
        keep = (granule & d) == 0
        y = [None] * per_block
        for i in range(per_block):
            if i & d == 0:
                y[i] = jnp.where(keep, x[i], pltpu.roll(x[i + d], d * S5_GROUP, axis=1))
                y[i + d] = jnp.where(keep, pltpu.roll(x[i], LANES - d * S5_GROUP, axis=1), x[i + d])
        x = y
    return x


def _sublane_transpose(tiles):
    sub = lax.broadcasted_iota(jnp.int32, (SUBLANES, 1), 0)
    x = list(tiles)
    for d in (4, 2, 1):
        keep = (sub & d) == 0
        y = [None] * SUBLANES
        for i in range(SUBLANES):
            if i & d == 0:
                y[i] = jnp.where(keep, x[i], pltpu.roll(x[i + d], d, axis=0))
                y[i + d] = jnp.where(keep, pltpu.roll(x[i], SUBLANES - d, axis=0), x[i + d])
        x = y
    return x


def _s5_rows_kernel(u_ref, o_ref, *, chunks):
    per_block = LANES // S5_GROUP
    for gcol in range(D_S5 // LANES):
        for j in range(S5_CONV // LANES):
            groups = [_sublane_transpose([u_ref[gcol, pl.ds((cg * SUBLANES + c) * CHUNK + per_block * j, SUBLANES), :]
                                          for c in range(SUBLANES)]) for cg in range(chunks // SUBLANES)]
            by_time = [jnp.concatenate([grp[sl] for grp in groups], axis=0) for sl in range(per_block)]
            for gl, rows in enumerate(_granule_transpose(by_time)):
                o_ref[gcol * per_block + gl, :, _lane_block(j)] = rows.astype(BF16)


def s5_rows(u, chunks):
    nblk, t, _ = u.shape
    assert chunks % SUBLANES == 0
    r = t // CHUNK
    return pl.pallas_call(
        functools.partial(_s5_rows_kernel, chunks=chunks),
        grid=(r // chunks,),
        in_specs=[pl.BlockSpec((nblk, chunks * CHUNK, LANES), lambda i: (0, i, 0))],
        out_specs=pl.BlockSpec((S5_GROUPS, chunks, S5_CONV), lambda i: (0, i, 0)),
        out_shape=jax.ShapeDtypeStruct((S5_GROUPS, r, S5_CONV), BF16),
        compiler_params=_cparams(),
        name="s5_rows",
    )(u)


def _s5_tokens_kernel(y_ref, o_ref, *, chunks):
    per_block = LANES // S5_GROUP
    for gcol in range(D_S5 // LANES):
        for j in range(S5_CONV // LANES):
            by_group = [y_ref[gcol * per_block + gl, :, _lane_block(j)].astype(F32) for gl in range(per_block)]
            by_time = _granule_transpose(by_group)
            for cg in range(chunks // SUBLANES):
                tiles = _sublane_transpose([rows[cg * SUBLANES:(cg + 1) * SUBLANES, :] for rows in by_time])
                for c in range(SUBLANES):
                    o_ref[gcol, pl.ds((cg * SUBLANES + c) * CHUNK + per_block * j, SUBLANES), :] = tiles[c]


def s5_tokens(y_rows, chunks):
    _, r, _ = y_rows.shape
    assert chunks % SUBLANES == 0
    return pl.pallas_call(
        functools.partial(_s5_tokens_kernel, chunks=chunks),
        grid=(r // chunks,),
        in_specs=[pl.BlockSpec((S5_GROUPS, chunks, S5_CONV), lambda i: (0, i, 0))],
        out_specs=pl.BlockSpec((D_S5 // LANES, chunks * CHUNK, LANES), lambda i: (0, i, 0)),
        out_shape=jax.ShapeDtypeStruct((D_S5 // LANES, r * CHUNK, LANES), F32),
        compiler_params=_cparams(),
        name="s5_tokens",
    )(y_rows)


def _s5_prep_kernel(lrc_ref, lic_ref, lrr_ref, lir_ref, ls_ref, brt_ref, bit_ref, ctr_ref, cti_ref,
                    m_ref, w_ref, wc_ref, a_ref):
    step = jnp.exp(ls_ref[0])

    def discretise(lr_raw, li):
        lr = jnp.minimum(lr_raw, -1e-4)
        dr, di = lr * step, li * step
        mag = jnp.exp(dr)
        a_re, a_im = mag * jnp.cos(di), mag * jnp.sin(di)
        den = lr * lr + li * li
        nr = a_re - 1.0
        fr = (nr * lr + a_im * li) / den
        fi = (a_im * lr - nr * li) / den
        return dr, di, fr, fi

    dr_c, di_c, _, _ = discretise(lrc_ref[0], lic_ref[0])
    dr_r, di_r, fr_r, fi_r = discretise(lrr_ref[0], lir_ref[0])

    lane = lax.broadcasted_iota(jnp.int32, (1, S5_CONV), 1)
    t_row = lax.broadcasted_iota(jnp.int32, (1, CHUNK), 1).astype(F32)
    t_col = lax.broadcasted_iota(jnp.int32, (CHUNK, 1), 0).astype(F32)
    lag_of_lane = (lax.broadcasted_iota(jnp.int32, (CHUNK, S5_CONV), 1) // S5_GROUP
                   == lax.broadcasted_iota(jnp.int32, (CHUNK, S5_CONV), 0)).astype(F32)
    time_of_row = (lax.broadcasted_iota(jnp.int32, (S5_CONV, CHUNK), 0) // S5_GROUP
                   == lax.broadcasted_iota(jnp.int32, (S5_CONV, CHUNK), 1)).astype(F32)
    chan_of_lane = (lax.broadcasted_iota(jnp.int32, (S5_GROUP, S5_CONV), 1) % S5_GROUP
                    == lax.broadcasted_iota(jnp.int32, (S5_GROUP, S5_CONV), 0)).astype(F32)
    chan_of_row = (lax.broadcasted_iota(jnp.int32, (S5_CONV, S5_GROUP), 0) % S5_GROUP
                   == lax.broadcasted_iota(jnp.int32, (S5_CONV, S5_GROUP), 1)).astype(F32)
    ctr = _spread_cols(ctr_ref[0], chan_of_lane)
    cti = _spread_cols(cti_ref[0], chan_of_lane)

    def c_times_power(tf):
        mag = jnp.exp(dr_c * tf)
        ang = di_c * tf
        pr = _spread_cols(mag * jnp.cos(ang), lag_of_lane)
        pi = _spread_cols(mag * jnp.sin(ang), lag_of_lane)
        return ctr * pr - cti * pi, ctr * pi + cti * pr

    cpr, cpi = c_times_power(t_row)
    bbr = fr_r * brt_ref[0] - fi_r * bit_ref[0]
    bbi = fr_r * bit_ref[0] + fi_r * brt_ref[0]
    kt = _dot(bbr, cpr, HIGHEST) - _dot(bbi, cpi, HIGHEST)
    for s in range(CHUNK):
        shifted = kt if s == 0 else pltpu.roll(kt, S5_GROUP * s, axis=1)
        m_ref[0, S5_GROUP * s:S5_GROUP * (s + 1), :] = jnp.where(
            lane >= S5_GROUP * s, shifted, 0.0).astype(BF16)

    rem = CHUNK - 1.0 - t_col
    magw = jnp.exp(dr_r * rem)
    angw = di_r * rem
    pwr = _spread_rows(time_of_row, magw * jnp.cos(angw))
    pwi = _spread_rows(time_of_row, magw * jnp.sin(angw))
    bbtr = _spread_rows(chan_of_row, bbr)
    bbti = _spread_rows(chan_of_row, bbi)
    w_ref[0, :, :S5_STATE] = pwr * bbtr - pwi * bbti
    w_ref[0, :, S5_STATE:] = pwr * bbti + pwi * bbtr

    c1r, c1i = c_times_power(t_row + 1.0)
    wc_ref[0, :S5_STATE, :] = c1r.astype(BF16)
    wc_ref[0, S5_STATE:, :] = (-c1i).astype(BF16)

    full = float(CHUNK)
    mag_c = jnp.exp(dr_r * full)
    a_ref[0, 0:1, :] = mag_c * jnp.cos(di_r * full)
    a_ref[0, 1:2, :] = mag_c * jnp.sin(di_r * full)


def s5_prep(lam_re, lam_im, log_step, b_re, b_im, c_re, c_im):
    g, p = lam_re.shape
    brt = jnp.swapaxes(b_re, 1, 2)
    bit = jnp.swapaxes(b_im, 1, 2)
    args = (lam_re.reshape(g, p, 1), lam_im.reshape(g, p, 1),
            lam_re.reshape(g, 1, p), lam_im.reshape(g, 1, p), log_step.reshape(g, 1, 1),
            brt, bit, jnp.swapaxes(c_re, 1, 2), jnp.swapaxes(c_im, 1, 2))
    spec = lambda a: pl.BlockSpec((1,) + a.shape[1:], lambda i: (i, 0, 0))
    out_shape = [jax.ShapeDtypeStruct((g, S5_CONV, S5_CONV), BF16),
                 jax.ShapeDtypeStruct((g, S5_CONV, 2 * S5_STATE), F32),
                 jax.ShapeDtypeStruct((g, 2 * S5_STATE, S5_CONV), BF16),
                 jax.ShapeDtypeStruct((g, 2, S5_STATE), F32)]
    return pl.pallas_call(
        _s5_prep_kernel,
        grid=(g,),
        in_specs=[spec(a) for a in args],
        out_specs=[spec(o) for o in out_shape],
        out_shape=out_shape,
        compiler_params=_cparams(),
        name="s5_prep",
    )(*args)


def _s5_main_kernel(up_ref, us_ref, m_ref, w_ref, wc_ref, a_ref, xsr_ref, xsi_ref,
                    yp_ref, ys_ref, fpr_ref, fpi_ref, fsr_ref, fsi_ref, fin_scr, *, n_prompt, n_chunks):
    w3 = jnp.concatenate(_split3(w_ref[0]), axis=1)
    ar = a_ref[0, 0:1, :]
    ai = a_ref[0, 1:2, :]
    width = 2 * S5_STATE

    def local(u):
        h3 = _dot(u, w3)
        return _dot(u, m_ref[0]), h3[:, :width] + h3[:, width:2 * width] + h3[:, 2 * width:]

    def times(pr, pi, x):
        return (jnp.concatenate([pr, pr], axis=1) * x
                + jnp.concatenate([-pi, pi], axis=1) * pltpu.roll(x, S5_STATE, axis=1))

    y_local, x = local(up_ref[0])
    chunk_of_row = lax.broadcasted_iota(jnp.int32, (n_prompt * n_chunks, 1), 0) % n_chunks
    pr, pi = ar, ai
    d = 1
    while d < n_chunks:
        x = x + times(pr, pi, jnp.where(chunk_of_row >= d, pltpu.roll(x, d, axis=0), 0.0))
        pr, pi = pr * pr - pi * pi, 2.0 * pr * pi
        d *= 2
    fin_scr[...] = x
    last = fin_scr[pl.ds(n_chunks - 1, n_prompt, stride=n_chunks), :]
    fpr_ref[0] = last[:, :S5_STATE]
    fpi_ref[0] = last[:, S5_STATE:]
    x0 = jnp.where(chunk_of_row >= 1, pltpu.roll(x, 1, axis=0), 0.0)
    yp_ref[0] = (y_local + _dot(x0.astype(BF16), wc_ref[0])).astype(BF16)

    y_local, hend = local(us_ref[0])
    x0 = jnp.concatenate([xsr_ref[0], xsi_ref[0]], axis=1)
    fin = times(ar, ai, x0) + hend
    fsr_ref[0] = fin[:, :S5_STATE]
    fsi_ref[0] = fin[:, S5_STATE:]
    ys_ref[0] = (y_local + _dot(x0.astype(BF16), wc_ref[0])).astype(BF16)


def s5_main(up_rows, us_rows, m, w, wc, a, xs_re, xs_im, n_prompt, n_chunks):
    g, r, _ = up_rows.shape
    n_sample = xs_re.shape[1]
    spec = lambda shape: pl.BlockSpec((1,) + tuple(shape[1:]), lambda i: (i, 0, 0))
    args = (up_rows, us_rows, m, w, wc, a, xs_re, xs_im)
    out_shape = [jax.ShapeDtypeStruct((g, r, S5_CONV), BF16),
                 jax.ShapeDtypeStruct((g, n_sample, S5_CONV), BF16),
                 jax.ShapeDtypeStruct((g, n_prompt, S5_STATE), F32),
                 jax.ShapeDtypeStruct((g, n_prompt, S5_STATE), F32),
                 jax.ShapeDtypeStruct((g, n_sample, S5_STATE), F32),
                 jax.ShapeDtypeStruct((g, n_sample, S5_STATE), F32)]
    return pl.pallas_call(
        functools.partial(_s5_main_kernel, n_prompt=n_prompt, n_chunks=n_chunks),
        grid=(g,),
        in_specs=[spec(x.shape) for x in args],
        out_specs=[spec(o.shape) for o in out_shape],
        out_shape=out_shape,
        scratch_shapes=[pltpu.VMEM((r, 2 * S5_STATE), F32)],
        compiler_params=_cparams(),
        name="s5_main",
    )(*args)


def _hgrn_kernel(f_ref, z_ref, s0_ref, lb_ref, ng_ref, o_ref, sfin_ref, st_scr, *, n_seq, chunk):
    c = pl.program_id(1)

    @pl.when(c == 0)
    def _():
        st_scr[...] = s0_ref[...]

    lbw = lb_ref[...]
    lbe = jnp.exp(lbw - jnp.max(lbw, axis=0, keepdims=True))
    lb_all = lbe[0:1, :] / jnp.sum(lbe, axis=0, keepdims=True)

    levels = [chunk >> (i + 1) for i in range(chunk.bit_length() - 1)]
    rowi = lax.broadcasted_iota(jnp.int32, (chunk, chunk), 0)
    coli = lax.broadcasted_iota(jnp.int32, (chunk, chunk), 1)
    rowk = lax.broadcasted_iota(jnp.int32, (chunk, HG_D), 0)
    sign, valid = [], []
    for m in levels:
        sign.append(jnp.where((rowk % (2 * m)) >= m, 1.0, -1.0))
        valid.append(((rowi // (2 * m)) == (coli // (2 * m)))
                     & ((rowi % (2 * m)) >= m) & ((coli % (2 * m)) < m))
    cum_mat = (coli <= rowi).astype(BF16)
    cum_mat3 = jnp.concatenate([cum_mat] * 3, axis=1)
    diag = rowi == coli
    nt = (((1,), (1,)), ((), ()))

    def body(n, carry):
        zf = f_ref[n]
        fg_all = lb_all + (1.0 - lb_all) * jax.nn.sigmoid(zf)
        cums = _dot(cum_mat3, jnp.concatenate(_split3(jnp.log2(fg_all)), axis=0))
        for hd in range(HG_HEADS):
            cols = slice(hd * HG_D, (hd + 1) * HG_D)
            zq = z_ref[n, :, hd * HG_D:(hd + 1) * HG_D].astype(F32)
            vb = z_ref[n, :, D_HG + hd * HG_D:D_HG + (hd + 1) * HG_D]
            zg = z_ref[n, :, 2 * D_HG + hd * HG_D:2 * D_HG + (hd + 1) * HG_D].astype(F32)
            q = zq * jax.nn.sigmoid(zq)
            kk = 1.0 - fg_all[:, cols]
            bcum = cums[:chunk, cols]
            b_last = bcum[chunk - 1:chunk, :]
            qb = q.astype(BF16)
            kb = kk.astype(BF16)
            st = st_scr[n, hd]

            scores = jnp.where(diag, lax.dot_general(qb, kb, nt, preferred_element_type=F32), 0.0)
            for lvl, m in enumerate(levels):
                if 2 * m >= SUBLANES:
                    bref = jnp.concatenate(
                        [jnp.broadcast_to(bcum[b * 2 * m + m - 1:b * 2 * m + m, :], (2 * m, HG_D))
                         for b in range(chunk // (2 * m))], axis=0)
                else:
                    offs = rowk % (2 * m) - (m - 1)
                    bref = bcum
                    for o in range(-(m - 1), m + 1):
                        if o != 0:
                            bref = jnp.where(offs == o, pltpu.roll(bcum, o % chunk, axis=0), bref)
                dec = jnp.exp2((bcum - bref) * sign[lvl]).astype(BF16)
                sc = lax.dot_general(qb * dec, kb * dec, nt, preferred_element_type=F32)
                scores = jnp.where(valid[lvl], sc, scores)

            qd = (q * jnp.exp2(bcum)).astype(BF16)
            o = lax.dot_general(qd, st.astype(BF16), nt, preferred_element_type=F32)
            o = o + _dot(scores.astype(BF16), vb)
            kdec = (kk * jnp.exp2(b_last - bcum)).astype(BF16)
            st_scr[n, hd] = jnp.exp2(b_last) * st + lax.dot_general(
                vb, kdec, (((0,), (0,)), ((), ())), preferred_element_type=F32)

            on = o * lax.rsqrt(jnp.mean(o * o, axis=-1, keepdims=True) + RMS_EPS) * ng_ref[:, cols]
            o_ref[n, :, hd * HG_D:(hd + 1) * HG_D] = on * (zg * jax.nn.sigmoid(zg))
        return carry

    lax.fori_loop(0, n_seq, body, 0, unroll=True)

    @pl.when(c == pl.num_programs(1) - 1)
    def _():
        sfin_ref[...] = st_scr[...]


def hgrn(f, z, s0_t, hg_lb, norm_g, n_seq, chunk):
    n, length, _ = z.shape
    return pl.pallas_call(
        functools.partial(_hgrn_kernel, n_seq=n_seq, chunk=chunk),
        grid=(n // n_seq, length // chunk),
        in_specs=[pl.BlockSpec((n_seq, chunk, D_HG), lambda g, c: (g, c, 0)),
                  pl.BlockSpec((n_seq, chunk, 3 * D_HG), lambda g, c: (g, c, 0)),
                  pl.BlockSpec((n_seq, HG_HEADS, HG_D, HG_D), lambda g, c: (g, 0, 0, 0)),
                  pl.BlockSpec(hg_lb.shape, lambda g, c: (0, 0)),
                  pl.BlockSpec((1, D_HG), lambda g, c: (0, 0))],
        out_specs=[pl.BlockSpec((n_seq, chunk, D_HG), lambda g, c: (g, c, 0)),
                   pl.BlockSpec((n_seq, HG_HEADS, HG_D, HG_D), lambda g, c: (g, 0, 0, 0))],
        out_shape=[jax.ShapeDtypeStruct((n, length, D_HG), F32),
                   jax.ShapeDtypeStruct((n, HG_HEADS, HG_D, HG_D), F32)],
        scratch_shapes=[pltpu.VMEM((n_seq, HG_HEADS, HG_D, HG_D), F32)],
        compiler_params=_cparams(2),
        name="hgrn",
    )(f, z, s0_t, hg_lb, norm_g)


def _post_mix_kernel(h_ref, up_ref, us_ref, yp_ref, ys_ref, hgp_ref, hgs_ref,
                     d_ref, wglu_ref, bglu_ref, s5g_ref, wout_ref, g1_ref, b1_ref, rwt_ref, rb_ref,
                     h1_ref, slot_ref, gate_ref, before_ref, cnt_ref, run_scr, *, tm, n_first):
    i = pl.program_id(0)

    @pl.when(i == 0)
    def _():
        run_scr[...] = jnp.zeros_like(run_scr)

    def phase(u_ref, y_ref, hg_ref):
        ys = _read_lane_blocks(y_ref) + d_ref[...] * _read_lane_blocks(u_ref)
        gl = 0.5 * ys * (1.0 + lax.erf(ys * (2.0 ** -0.5)))
        s5o = gl * jax.nn.sigmoid(_dot(gl.astype(BF16), wglu_ref[...]) + bglu_ref[...])
        s5o = s5o * lax.rsqrt(jnp.mean(s5o * s5o, axis=-1, keepdims=True) + RMS_EPS) * s5g_ref[...]
        mix = (_dot(s5o.astype(BF16), wout_ref[:D_S5, :])
               + _dot(hg_ref[...].astype(BF16), wout_ref[D_S5:, :]))
        h1 = _layernorm(DEEPNORM_ALPHA * h_ref[...] + mix, g1_ref[...], b1_ref[...])
        _write_row_tiles(h1_ref, h1, tm)

        h_hi, h_mid, _ = _split3(h1)
        w_hi, w_mid, _ = _split3(rwt_ref[...])
        nt = (((1,), (1,)), ((), ()))
        logits = (lax.dot_general(w_hi, h_hi, nt, preferred_element_type=F32)
                  + lax.dot_general(w_hi, h_mid, nt, preferred_element_type=F32)
                  + lax.dot_general(w_mid, h_hi, nt, preferred_element_type=F32)) + rb_ref[...]
        eid = lax.broadcasted_iota(jnp.int32, (N_EXPERTS, tm), 0)
        vals, idxs = [], []
        for _ in range(TOP_K):
            m = jnp.max(logits, axis=0, keepdims=True)
            ix = jnp.min(jnp.where(logits == m, eid, N_EXPERTS), axis=0, keepdims=True)
            vals.append(m)
            idxs.append(ix)
            logits = jnp.where(eid == ix, -jnp.inf, logits)
        exps = [jnp.exp(v - vals[0]) for v in vals]
        den = exps[0] + exps[1] + exps[2] + exps[3]

        onehot = jnp.zeros((N_EXPERTS, tm), F32)
        for ix in idxs:
            onehot = onehot + (eid == ix).astype(F32)
        rowi = lax.broadcasted_iota(jnp.int32, (tm, tm), 0)
        coli = lax.broadcasted_iota(jnp.int32, (tm, tm), 1)
        earlier = (rowi < coli).astype(BF16)
        prefix = _dot(onehot.astype(BF16), earlier)
        tile_cnt = jnp.sum(onehot, axis=1, keepdims=True)
        for k in range(TOP_K):
            lower_experts = jnp.sum(jnp.where(eid < idxs[k], tile_cnt, 0.0), axis=0, keepdims=True)
            rank = jnp.sum(jnp.where(eid == idxs[k], prefix, 0.0), axis=0, keepdims=True)
            slot_ref[0, :, k * tm:(k + 1) * tm] = (lower_experts + rank).astype(jnp.int32) * ROW_TILES
            gate_ref[0, :, k * tm:(k + 1) * tm] = exps[k] / den
        before_ref[0] = run_scr[...]
        run_scr[...] = run_scr[...] + tile_cnt
        cnt_ref[...] = run_scr[...]

    pl.when(i < n_first)(lambda: phase(up_ref, yp_ref, hgp_ref))
    pl.when(i >= n_first)(lambda: phase(us_ref, ys_ref, hgs_ref))


def post_mix(h0, u_pair, y_pair, hg_pair, d_skip, wglu, bglu, s5g, wout, g1, b1, rw_t, rb_col, tm):
    t = h0.shape[0]
    n_first = u_pair[0].shape[1] // tm
    row = lambda i: (i, 0)
    fixed = lambda i: (0, 0)
    full = lambda a: pl.BlockSpec(a.shape, fixed)
    weights = (d_skip, wglu, bglu, s5g, wout, g1, b1, rw_t, rb_col)
    return pl.pallas_call(
        functools.partial(_post_mix_kernel, tm=tm, n_first=n_first),
        grid=(t // tm,),
        in_specs=[pl.BlockSpec((tm, D_MODEL), row)]
                 + _two_phase_lane_block_specs(tm, n_first) * 2 + _two_phase_specs((tm, D_HG), n_first)
                 + [full(a) for a in weights],
        out_specs=[pl.BlockSpec((tm * ROW_TILES, LANES), row),
                   pl.BlockSpec((1, 1, TOP_K * tm), lambda i: (i, 0, 0)),
                   pl.BlockSpec((1, 1, TOP_K * tm), lambda i: (i, 0, 0)),
                   pl.BlockSpec((1, N_EXPERTS, 1), lambda i: (i, 0, 0)),
                   pl.BlockSpec((N_EXPERTS, 1), fixed)],
        out_shape=[jax.ShapeDtypeStruct((t * ROW_TILES, LANES), F32),
                   jax.ShapeDtypeStruct((t // tm, 1, TOP_K * tm), jnp.int32),
                   jax.ShapeDtypeStruct((t // tm, 1, TOP_K * tm), F32),
                   jax.ShapeDtypeStruct((t // tm, N_EXPERTS, 1), F32),
                   jax.ShapeDtypeStruct((N_EXPERTS, 1), F32)],
        scratch_shapes=[pltpu.VMEM((N_EXPERTS, 1), F32)],
        compiler_params=_cparams(),
        name="post_mix",
    )(h0, *u_pair, *y_pair, *hg_pair, *weights)


def _segment_copies(meta_ref, tile, tm, make_copy):
    for e in range(N_EXPERTS):
        sorted_row = meta_ref[tile, e]
        cnt = meta_ref[tile, N_EXPERTS + e]
        staged_row = meta_ref[tile, 2 * N_EXPERTS + e]
        for b in range(tm.bit_length()):
            done = cnt & ((1 << b) - 1)

            @pl.when(((cnt >> b) & 1) == 1)
            def _(b=b, done=done, e=e):
                make_copy(staged_row + done, sorted_row + done, 1 << b).start(priority=e % 2)


def _dispatch_kernel(meta_ref, pend_ref, slot_ref, h_ref, xs_ref, stage, zero_scr, zsem, sem, *, tm):
    n_rows = xs_ref.shape[0] // ROW_TILES
    i = pl.program_id(0)
    n = pl.num_programs(0)
    slot = i % 2

    def drain(s):
        pltpu.make_async_copy(stage.at[s], xs_ref.at[pl.ds(0, TOP_K * tm * ROW_TILES)], sem.at[s]).wait()

    @pl.when(i == 0)
    def _():
        zero_scr[...] = jnp.zeros_like(zero_scr)

        def last_block(e):
            prev = pend_ref[e - 1] if e > 0 else 0
            copy = pltpu.make_async_copy(
                zero_scr, xs_ref.at[pl.ds(pl.multiple_of(jnp.maximum(pend_ref[e] - MOE_ROWS, 0) * ROW_TILES,
                                                         ROW_TILES), MOE_ROWS * ROW_TILES)], zsem)
            return pend_ref[e] > prev, copy

        def tail_block(j):
            row0 = pend_ref[N_EXPERTS - 1] + j * MOE_ROWS
            copy = pltpu.make_async_copy(
                zero_scr, xs_ref.at[pl.ds(pl.multiple_of(jnp.minimum(row0, n_rows - MOE_ROWS) * ROW_TILES,
                                                         ROW_TILES), MOE_ROWS * ROW_TILES)], zsem)
            return row0 < n_rows, copy

        blocks = [last_block(e) for e in range(N_EXPERTS)] + [tail_block(j) for j in range(N_EXPERTS)]
        for used, copy in blocks:
            pl.when(used)(copy.start)
        for used, copy in blocks:
            pl.when(used)(copy.wait)

    pl.when(i >= 2)(lambda: drain(slot))

    def regroup(s):
        def body(t, carry):
            row = h_ref[_row_tile(t), :]
            for k in range(TOP_K):
                stage[s, pl.ds(pl.multiple_of(slot_ref[k * tm + t], ROW_TILES), ROW_TILES), :] = row
            return carry

        lax.fori_loop(0, tm, body, 0, unroll=8)

    pl.when(slot == 0)(lambda: regroup(0))
    pl.when(slot == 1)(lambda: regroup(1))
    _segment_copies(meta_ref, i, tm, lambda staged_row, sorted_row, rows: pltpu.make_async_copy(
        stage.at[slot, _row_tiles(staged_row, rows)], xs_ref.at[_row_tiles(sorted_row, rows)], sem.at[slot]))

    @pl.when(i == n - 1)
    def _():
        pl.when(n >= 2)(lambda: drain(1 - slot))
        drain(slot)


def dispatch(meta, pend, slots_flat, h1, n_rows, tm):
    t = h1.shape[0] // ROW_TILES
    grid_spec = pltpu.PrefetchScalarGridSpec(
        num_scalar_prefetch=2,
        grid=(t // tm,),
        in_specs=[pl.BlockSpec((TOP_K * tm,), lambda i, meta, pend: (i,), memory_space=pltpu.SMEM),
                  pl.BlockSpec((tm * ROW_TILES, LANES), lambda i, meta, pend: (i, 0))],
        out_specs=pl.BlockSpec(memory_space=pl.ANY),
        scratch_shapes=[pltpu.VMEM((2, TOP_K * tm * ROW_TILES, LANES), F32),
                        pltpu.VMEM((MOE_ROWS * ROW_TILES, LANES), F32),
                        pltpu.SemaphoreType.DMA(()), pltpu.SemaphoreType.DMA((2,))],
    )
    return pl.pallas_call(
        functools.partial(_dispatch_kernel, tm=tm),
        grid_spec=grid_spec,
        out_shape=jax.ShapeDtypeStruct((n_rows * ROW_TILES, LANES), F32),
        compiler_params=_cparams(),
        name="moe_dispatch",
    )(meta, pend, slots_flat, h1)


def _moe_ffn_kernel(be_ref, nu_ref, seg_ref, nxt_ref, x_ref, wg_ref, bg_ref, wu_ref, bu_ref, wd_ref, bd_ref,
                    y_ref, wbuf, wbf, sem):
    i = pl.program_id(0)
    hbm = (wg_ref, wu_ref, wd_ref)

    def weight_copies(expert, s):
        return [pltpu.make_async_copy(hbm[j].at[expert], wbuf.at[s, j], sem.at[s, j]) for j in range(3)]

    @pl.when((i == 0) | (be_ref[i] != be_ref[jnp.maximum(i - 1, 0)]))
    def _():
        s = seg_ref[i] % 2

        @pl.when(i == 0)
        def _():
            for c in weight_copies(be_ref[0], 0):
                c.start()

        for j, c in enumerate(weight_copies(be_ref[i], s)):
            c.wait()
            wbf[j] = wbuf[s, j].astype(BF16)

        @pl.when(nxt_ref[i] >= 0)
        def _():
            for c in weight_copies(nxt_ref[i], 1 - s):
                c.start()

    @pl.when(i < nu_ref[0])
    def _():
        x = _read_row_tiles(x_ref, MOE_ROWS).astype(BF16)
        gt = jnp.minimum(_dot(x, wbf[0]) + bg_ref[0], SWIGLU_LIMIT)
        up = jnp.clip(_dot(x, wbf[1]) + bu_ref[0], -SWIGLU_LIMIT, SWIGLU_LIMIT)
        hid = (up + 1.0) * (gt * jax.nn.sigmoid(SWIGLU_ALPHA * gt))
        _write_row_tiles(y_ref, _dot(hid.astype(BF16), wbf[2]) + bd_ref[0], MOE_ROWS)

    @pl.when(i >= nu_ref[0])
    def _():
        y_ref[...] = jnp.zeros_like(y_ref)


def moe_ffn(block_e, n_used, segment, next_e, xs, wg, bg, wu, bu, wd, bd):
    n_rows = xs.shape[0] // ROW_TILES
    n_blocks = n_rows // MOE_ROWS
    wsel = lambda i, be, nu, seg, nxt: (be[i], 0, 0)
    d_ff = wg.shape[-1]
    assert wg.shape[1:] == wu.shape[1:] == wd.shape[1:] == (D_MODEL, D_MODEL)
    anywhere = pl.BlockSpec(memory_space=pl.ANY)
    grid_spec = pltpu.PrefetchScalarGridSpec(
        num_scalar_prefetch=4,
        grid=(n_blocks,),
        in_specs=[pl.BlockSpec((MOE_ROWS * ROW_TILES, LANES),
                               lambda i, be, nu, seg, nxt: (jnp.minimum(i, nu[0] - 1), 0)),
                  anywhere, pl.BlockSpec((1, 1, d_ff), wsel),
                  anywhere, pl.BlockSpec((1, 1, d_ff), wsel),
                  anywhere, pl.BlockSpec((1, 1, D_MODEL), wsel)],
        out_specs=pl.BlockSpec((MOE_ROWS * ROW_TILES, LANES), lambda i, be, nu, seg, nxt: (i, 0)),
        scratch_shapes=[pltpu.VMEM((2, 3, D_MODEL, D_MODEL), F32), pltpu.VMEM((3, D_MODEL, D_MODEL), BF16),
                        pltpu.SemaphoreType.DMA((2, 3))],
    )
    return pl.pallas_call(
        _moe_ffn_kernel,
        grid_spec=grid_spec,
        out_shape=jax.ShapeDtypeStruct((n_rows * ROW_TILES, LANES), F32),
        compiler_params=_cparams(),
        name="moe_ffn",
    )(block_e, n_used, segment, next_e, xs, wg, bg, wu, bu, wd, bd)


def _combine_kernel(meta_ref, slot_ref, gate_ref, h_ref, pp_ref, ps_ref, yb_ref,
                    plew_ref, plegw_ref, g2_ref, b2_ref, outp_ref, outs_ref, buf, r_scr, sem,
                    *, tm, n_first):
    i = pl.program_id(0)
    n = pl.num_programs(0)
    slot = i % 2

    def fetch(tile, s):
        _segment_copies(meta_ref, tile, tm, lambda staged_row, sorted_row, rows: pltpu.make_async_copy(
            yb_ref.at[_row_tiles(sorted_row, rows)], buf.at[s, _row_tiles(staged_row, rows)], sem.at[s]))

    pl.when(i == 0)(lambda: fetch(0, 0))
    pl.when(i + 1 < n)(lambda: fetch(i + 1, 1 - slot))
    pltpu.make_async_copy(yb_ref.at[pl.ds(0, TOP_K * tm * ROW_TILES)], buf.at[slot], sem.at[slot]).wait()

    def weighted_sum(s):
        def body(t, carry):
            acc = DEEPNORM_ALPHA * h_ref[_row_tile(t), :]
            for k in range(TOP_K):
                acc = acc + gate_ref[k * tm + t] * buf[s, pl.ds(pl.multiple_of(slot_ref[k * tm + t], ROW_TILES),
                                                               ROW_TILES), :]
            r_scr[_row_tile(t), :] = acc
            return carry

        lax.fori_loop(0, tm, body, 0, unroll=8)

    pl.when(slot == 0)(lambda: weighted_sum(0))
    pl.when(slot == 1)(lambda: weighted_sum(1))
    r = _read_row_tiles(r_scr, tm)
    gate = jax.nn.sigmoid(_dot(r.astype(BF16), plegw_ref[...]))

    def finish(p, store):
        e = _dot(p.astype(BF16), plew_ref[...]) * gate
        store(_layernorm(r + e, g2_ref[...], b2_ref[...]))

    def store_prompt(v):
        outp_ref[0] = v

    def store_sample(v):
        outs_ref[...] = v

    pl.when(i < n_first)(lambda: finish(pp_ref[0], store_prompt))
    pl.when(i >= n_first)(lambda: finish(ps_ref[...], store_sample))


def combine(meta, slots_flat, gates_flat, h1, p_prompt, p_sample, yb, plew, plegw, g2, b2, tm):
    t = h1.shape[0] // ROW_TILES
    nb, seq, _ = p_prompt.shape
    ts = p_sample.shape[0]
    n_first = nb * seq // tm
    fixed = lambda i, meta: (0, 0)
    flat = pl.BlockSpec((TOP_K * tm,), lambda i, meta: (i,), memory_space=pltpu.SMEM)
    sample = lambda width: pl.BlockSpec((tm, width), lambda i, meta: (jnp.maximum(i - n_first, 0), 0))
    prompt = lambda width: pl.BlockSpec((1, tm, width), _prompt_spec(tm, seq, width, n_first).index_map)
    with_meta = lambda spec: pl.BlockSpec(spec.block_shape, lambda i, meta: spec.index_map(i))
    grid_spec = pltpu.PrefetchScalarGridSpec(
        num_scalar_prefetch=1,
        grid=(t // tm,),
        in_specs=[flat, flat,
                  pl.BlockSpec((tm * ROW_TILES, LANES), lambda i, meta: (i, 0)),
                  with_meta(prompt(PLE_DIM)), sample(PLE_DIM),
                  pl.BlockSpec(memory_space=pl.ANY),
                  pl.BlockSpec(plew.shape, fixed), pl.BlockSpec(plegw.shape, fixed),
                  pl.BlockSpec((1, D_MODEL), fixed), pl.BlockSpec((1, D_MODEL), fixed)],
        out_specs=[with_meta(prompt(D_MODEL)), sample(D_MODEL)],
        scratch_shapes=[pltpu.VMEM((2, TOP_K * tm * ROW_TILES, LANES), F32),
                        pltpu.VMEM((tm * ROW_TILES, LANES), F32),
                        pltpu.SemaphoreType.DMA((2,))],
    )
    return pl.pallas_call(
        functools.partial(_combine_kernel, tm=tm, n_first=n_first),
        grid_spec=grid_spec,
        out_shape=[jax.ShapeDtypeStruct((nb, seq, D_MODEL), F32), jax.ShapeDtypeStruct((ts, D_MODEL), F32)],
        compiler_params=_cparams(),
        name="moe_combine",
    )(meta, slots_flat, gates_flat, h1, p_prompt, p_sample, yb, plew, plegw, g2, b2)


def _row(v):
    return v.reshape(1, -1)


def kernel(x_prompt, x_sample, state_s5_re, state_s5_im, state_hgrn, p_prompt, p_sample, ln_in_g, ln_in_b, w_in, s5_lambda_re, s5_lambda_im, s5_log_step, s5_b_re, s5_b_im, s5_c_re, s5_c_im, s5_d, s5_w_glu, s5_b_glu, s5_norm_g, hg_lb, hg_norm_g, w_out, ln1_g, ln1_b, router_w, router_b, w_gate, b_gate, w_up, b_up, w_down, b_down, ple_w, ple_gate_w, ln2_g, ln2_b):
    nb, seq, _ = x_prompt.shape
    ns, dseq, _ = x_sample.shape
    assert dseq == CHUNK and seq % CHUNK == 0 and w_in.shape[0] == 1
    nc = seq // CHUNK
    tp, ts = nb * seq, ns * dseq
    t = tp + ts
    tm = 512 if (tp % 512 == 0 and ts % 512 == 0) else 256
    assert tp % tm == 0 and ts % tm == 0 and seq % tm == 0 and ns % nb == 0

    w_cols = jnp.split(w_in[0], [D_S5, D_S5 + D_HG, D_S5 + 2 * D_HG], axis=1)
    h0, u_p, u_s, f_p, f_s, z_p, z_s = ln_in_proj(
        x_prompt, x_sample.reshape(ts, D_MODEL), _row(ln_in_g), _row(ln_in_b),
        jnp.concatenate([w_cols[0], w_cols[2], w_cols[1], w_cols[3]], axis=1).astype(BF16), tm)

    m, w, wc, a = s5_prep(s5_lambda_re[0], s5_lambda_im[0], s5_log_step[0],
                          s5_b_re[0], s5_b_im[0], s5_c_re[0], s5_c_im[0])
    rb = 64 if (nb * nc) % 64 == 0 else nc
    yp_rows, ys_rows, fpr, fpi, fsr, fsi = s5_main(s5_rows(u_p, rb), s5_rows(u_s, ns), m, w, wc, a,
                                                   jnp.swapaxes(state_s5_re[0], 0, 1),
                                                   jnp.swapaxes(state_s5_im[0], 0, 1), nb, nc)
    y_pair = (s5_tokens(yp_rows, rb), s5_tokens(ys_rows, ns))

    zero_state = jnp.zeros((nb, HG_HEADS, HG_D, HG_D), F32)
    ng = _row(hg_norm_g[0])
    o_p, st_p = hgrn(f_p.reshape(nb, seq, D_HG), z_p.reshape(nb, seq, 3 * D_HG), zero_state, hg_lb, ng, nb,
                     HG_CHUNK if seq % HG_CHUNK == 0 else CHUNK)
    o_s, st_s = hgrn(f_s.reshape(ns, dseq, D_HG), z_s.reshape(ns, dseq, 3 * D_HG), jnp.swapaxes(state_hgrn[0], 2, 3),
                     hg_lb, ng, nb, dseq)

    h1, slots, gates, before, counts = post_mix(
        h0, (u_p, u_s), y_pair, (o_p.reshape(tp, D_HG), o_s.reshape(ts, D_HG)),
        _row(s5_d[0]), s5_w_glu[0].astype(BF16), _row(s5_b_glu[0]),
        _row(s5_norm_g[0]), w_out[0].astype(BF16), _row(ln1_g[0]), _row(ln1_b[0]),
        router_w[0].T, router_b[0].reshape(N_EXPERTS, 1), tm)

    counts = counts[:, 0].astype(jnp.int32)
    before = before[:, :, 0].astype(jnp.int32)
    cnt = jnp.concatenate([before[1:], counts[None]], axis=0) - before
    padded = (counts + MOE_ROWS - 1) // MOE_ROWS * MOE_ROWS
    pend = jnp.cumsum(padded)
    staged = jnp.cumsum(cnt, axis=1) - cnt
    meta = jnp.concatenate([pend - padded + before, cnt, staged, jnp.zeros_like(cnt)], axis=1)
    experts = jnp.arange(N_EXPERTS, dtype=jnp.int32)

    n_blocks = -(-t * TOP_K // MOE_ROWS) + N_EXPERTS
    n_used = (pend[-1] // MOE_ROWS).astype(jnp.int32)
    blk = jnp.arange(n_blocks, dtype=jnp.int32)
    blk = jnp.minimum(blk, n_used - 1)
    block_e = jnp.sum((pend[None, :] <= (blk * MOE_ROWS)[:, None]).astype(jnp.int32), axis=1)
    block_e = jnp.minimum(block_e, N_EXPERTS - 1)
    owns_rows = padded > 0
    owner = owns_rows[None, :]
    segment = jnp.sum((owner & (experts[None, :] <= block_e[:, None])).astype(jnp.int32), axis=1) - 1
    later = jnp.where(owner & (experts[None, :] > block_e[:, None]), experts[None, :], N_EXPERTS)
    next_e = jnp.min(later, axis=1)
    next_e = jnp.where(next_e < N_EXPERTS, next_e, -1).astype(jnp.int32)

    slots_flat = slots.reshape(-1)
    xs = dispatch(meta, pend, slots_flat, h1, n_blocks * MOE_ROWS, tm)
    yb = moe_ffn(block_e, n_used.reshape(1), segment, next_e, xs,
                 w_gate[0], b_gate[0][:, None, :], w_up[0], b_up[0][:, None, :],
                 w_down[0], b_down[0][:, None, :])
    out_p, out_s = combine(meta, slots_flat, gates.reshape(-1), h1, p_prompt[0], p_sample[0].reshape(ts, PLE_DIM),
                           yb, ple_w[0].astype(BF16), ple_gate_w[0].astype(BF16),
                           _row(ln2_g[0]), _row(ln2_b[0]), tm)

    def s5_state(f, n):
        return jnp.swapaxes(f, 0, 1).reshape(1, n, S5_GROUPS, S5_STATE)

    return (out_p, out_s.reshape(ns, dseq, D_MODEL),
            s5_state(fpr, nb), s5_state(fpi, nb), jnp.swapaxes(st_p, 2, 3)[None],
            s5_state(fsr, ns), s5_state(fsi, ns), jnp.swapaxes(st_s, 2, 3)[None])
```

```python
import functools

import jax
import jax.numpy as jnp
from jax import lax
from jax.experimental import pallas as pl
from jax.experimental.pallas import tpu as pltpu

F32 = jnp.float32
BF16 = jnp.bfloat16
HIGHEST = lax.Precision.HIGHEST

D_MODEL = 1024
CHUNK = 64
PLE_DIM = 256
D_S5 = 512
S5_GROUP = 16
S5_GROUPS = 32
S5_STATE = 64
D_HG = 512
HG_HEADS = 4
HG_D = 128
D_IN = D_S5 + 4 * D_HG
N_EXPERTS = 32
TOP_K = 4
SWIGLU_LIMIT = 7.0
SWIGLU_ALPHA = 1.702
DEEPNORM_ALPHA = 2.0 ** 0.25
LN_EPS = 1e-5
RMS_EPS = 1e-6

LANES = 128
SUBLANES = 8
ROW_TILES = D_MODEL // LANES
S5_CONV = CHUNK * S5_GROUP
HG_CHUNK = 128
MOE_ROWS = 512
VMEM_LIMIT = 56 * 1024 * 1024

assert ROW_TILES == SUBLANES


def _cparams(n_axes=1):
    return pltpu.CompilerParams(dimension_semantics=("arbitrary",) * n_axes,
                                vmem_limit_bytes=VMEM_LIMIT)


def _dot(a, b, precision=None):
    return jnp.dot(a, b, preferred_element_type=F32, precision=precision)


def _split3(w):
    hi = w.astype(BF16)
    r1 = w - hi.astype(F32)
    mid = r1.astype(BF16)
    lo = (r1 - mid.astype(F32)).astype(BF16)
    return hi, mid, lo


def _spread_cols(table, onehot):
    return _dot(jnp.concatenate(_split3(table), axis=1), jnp.concatenate([onehot.astype(BF16)] * 3, axis=0))


def _spread_rows(onehot, table):
    return _dot(jnp.concatenate([onehot.astype(BF16)] * 3, axis=1), jnp.concatenate(_split3(table), axis=0))


def _layernorm(x, g, b):
    mu = jnp.mean(x, axis=-1, keepdims=True)
    xc = x - mu
    var = jnp.mean(xc * xc, axis=-1, keepdims=True)
    return xc * lax.rsqrt(var + LN_EPS) * g + b


def _two_phase_specs(block, n_first):
    nd = len(block)
    first = pl.BlockSpec(block, lambda i: (jnp.minimum(i, n_first - 1),) + (0,) * (nd - 1))
    second = pl.BlockSpec(block, lambda i: (jnp.maximum(i - n_first, 0),) + (0,) * (nd - 1))
    return [first, second]


def _two_phase_lane_block_specs(tm, n_first):
    nblk = D_S5 // LANES
    first = pl.BlockSpec((nblk, tm, LANES), lambda i: (0, jnp.minimum(i, n_first - 1), 0))
    second = pl.BlockSpec((nblk, tm, LANES), lambda i: (0, jnp.maximum(i - n_first, 0), 0))
    return [first, second]


def _read_lane_blocks(ref):
    return jnp.concatenate([ref[j] for j in range(D_S5 // LANES)], axis=1)


def _prompt_spec(tm, seq, width, n_first):
    per_seq = seq // tm

    def index(i):
        ic = jnp.minimum(i, n_first - 1)
        return (ic // per_seq, ic % per_seq, 0)

    return pl.BlockSpec((1, tm, width), index)


def _chunk(rows, j):
    return pl.ds(j, rows, stride=ROW_TILES)


def _read_row_tiles(ref, rows):
    return jnp.concatenate([ref[_chunk(rows, j), :] for j in range(ROW_TILES)], axis=1)


def _write_row_tiles(ref, val, rows):
    for j in range(ROW_TILES):
        ref[_chunk(rows, j), :] = val[:, j * LANES:(j + 1) * LANES]


def _row_tiles(r, n=1):
    return pl.ds(pl.multiple_of(r * ROW_TILES, ROW_TILES), n * ROW_TILES)


def _row_tile(r):
    return _row_tiles(r)


def _ln_in_proj_kernel(xp_ref, xs_ref, g_ref, b_ref, w_ref, h_ref, up_ref, us_ref, fp_ref, fs_ref, zp_ref, zs_ref,
                       *, n_first):
    def phase(x, u_ref, f_ref, z_ref):
        h = _layernorm(x, g_ref[...], b_ref[...])
        h_ref[...] = h
        hb = h.astype(BF16)
        u = _dot(hb, w_ref[:, :D_S5])
        for j in range(D_S5 // LANES):
            u_ref[j] = u[:, j * LANES:(j + 1) * LANES]
        f_ref[...] = _dot(hb, w_ref[:, D_S5:D_S5 + D_HG])
        z_ref[...] = _dot(hb, w_ref[:, D_S5 + D_HG:]).astype(BF16)

    i = pl.program_id(0)
    pl.when(i < n_first)(lambda: phase(xp_ref[0], up_ref, fp_ref, zp_ref))
    pl.when(i >= n_first)(lambda: phase(xs_ref[...], us_ref, fs_ref, zs_ref))


def ln_in_proj(xp, xs, g, b, w_bf16, tm):
    nb, seq, _ = xp.shape
    tp, ts = nb * seq, xs.shape[0]
    n_first = tp // tm
    fixed = lambda i: (0, 0)
    return pl.pallas_call(
        functools.partial(_ln_in_proj_kernel, n_first=n_first),
        grid=((tp + ts) // tm,),
        in_specs=[_prompt_spec(tm, seq, D_MODEL, n_first), _two_phase_specs((tm, D_MODEL), n_first)[1]]
                 + [pl.BlockSpec((1, D_MODEL), fixed), pl.BlockSpec((1, D_MODEL), fixed),
                    pl.BlockSpec((D_MODEL, D_IN), fixed)],
        out_specs=[pl.BlockSpec((tm, D_MODEL), lambda i: (i, 0))]
                  + _two_phase_lane_block_specs(tm, n_first)
                  + _two_phase_specs((tm, D_HG), n_first)
                  + _two_phase_specs((tm, 3 * D_HG), n_first),
        out_shape=[jax.ShapeDtypeStruct((tp + ts, D_MODEL), F32),
                   jax.ShapeDtypeStruct((D_S5 // LANES, tp, LANES), F32),
                   jax.ShapeDtypeStruct((D_S5 // LANES, ts, LANES), F32),
                   jax.ShapeDtypeStruct((tp, D_HG), F32), jax.ShapeDtypeStruct((ts, D_HG), F32),
                   jax.ShapeDtypeStruct((tp, 3 * D_HG), BF16), jax.ShapeDtypeStruct((ts, 3 * D_HG), BF16)],
        compiler_params=_cparams(),
        name="ln_in_proj",
    )(xp, xs, g, b, w_bf16)


def _lane_block(j):
    return slice(j * LANES, (j + 1) * LANES)


def _granule_transpose(slabs):
    per_block = LANES // S5_GROUP
    granule = lax.broadcasted_iota(jnp.int32, (1, LANES), 1) // S5_GROUP
    x = list(slabs)
    for d in (4, 2, 1):
        keep = (granule & d) == 0
        y = [None] * per_block
        for i in range(per_block):
            if i & d == 0:
                y[i] = jnp.where(keep, x[i], pltpu.roll(x[i + d], d * S5_GROUP, axis=1))
                y[i + d] = jnp.where(keep, pltpu.roll(x[i], LANES - d * S5_GROUP, axis=1), x[i + d])
        x = y
    return x


def _sublane_transpose(tiles):
    sub = lax.broadcasted_iota(jnp.int32, (SUBLANES, 1), 0)
    x = list(tiles)
    for d in (4, 2, 1):
        keep = (sub & d) == 0
        y = [None] * SUBLANES
        for i in range(SUBLANES):
            if i & d == 0:
                y[i] = jnp.where(keep, x[i], pltpu.roll(x[i + d], d, axis=0))
                y[i + d] = jnp.where(keep, pltpu.roll(x[i], SUBLANES - d, axis=0), x[i + d])
        x = y
    return x


def _s5_rows_kernel(u_ref, o_ref, *, chunks):
    per_block = LANES // S5_GROUP
    for gcol in range(D_S5 // LANES):
        for j in range(S5_CONV // LANES):
            groups = [_sublane_transpose([u_ref[gcol, pl.ds((cg * SUBLANES + c) * CHUNK + per_block * j, SUBLANES), :]
                                          for c in range(SUBLANES)]) for cg in range(chunks // SUBLANES)]
            by_time = [jnp.concatenate([grp[sl] for grp in groups], axis=0) for sl in range(per_block)]
            for gl, rows in enumerate(_granule_transpose(by_time)):
                o_ref[gcol * per_block + gl, :, _lane_block(j)] = rows.astype(BF16)


def s5_rows(u, chunks):
    nblk, t, _ = u.shape
    assert chunks % SUBLANES == 0
    r = t // CHUNK
    return pl.pallas_call(
        functools.partial(_s5_rows_kernel, chunks=chunks),
        grid=(r // chunks,),
        in_specs=[pl.BlockSpec((nblk, chunks * CHUNK, LANES), lambda i: (0, i, 0))],
        out_specs=pl.BlockSpec((S5_GROUPS, chunks, S5_CONV), lambda i: (0, i, 0)),
        out_shape=jax.ShapeDtypeStruct((S5_GROUPS, r, S5_CONV), BF16),
        compiler_params=_cparams(),
        name="s5_rows",
    )(u)


def _s5_tokens_kernel(y_ref, o_ref, *, chunks):
    per_block = LANES // S5_GROUP
    for gcol in range(D_S5 // LANES):
        for j in range(S5_CONV // LANES):
            by_group = [y_ref[gcol * per_block + gl, :, _lane_block(j)].astype(F32) for gl in range(per_block)]
            by_time = _granule_transpose(by_group)
            for cg in range(chunks // SUBLANES):
                tiles = _sublane_transpose([rows[cg * SUBLANES:(cg + 1) * SUBLANES, :] for rows in by_time])
                for c in range(SUBLANES):
                    o_ref[gcol, pl.ds((cg * SUBLANES + c) * CHUNK + per_block * j, SUBLANES), :] = tiles[c]


def s5_tokens(y_rows, chunks):
    _, r, _ = y_rows.shape
    assert chunks % SUBLANES == 0
    return pl.pallas_call(
        functools.partial(_s5_tokens_kernel, chunks=chunks),
        grid=(r // chunks,),
        in_specs=[pl.BlockSpec((S5_GROUPS, chunks, S5_CONV), lambda i: (0, i, 0))],
        out_specs=pl.BlockSpec((D_S5 // LANES, chunks * CHUNK, LANES), lambda i: (0, i, 0)),
        out_shape=jax.ShapeDtypeStruct((D_S5 // LANES, r * CHUNK, LANES), F32),
        compiler_params=_cparams(),
        name="s5_tokens",
    )(y_rows)


def _s5_prep_kernel(lrc_ref, lic_ref, lrr_ref, lir_ref, ls_ref, brt_ref, bit_ref, ctr_ref, cti_ref,
                    m_ref, w_ref, wc_ref, a_ref):
    step = jnp.exp(ls_ref[0])

    def discretise(lr_raw, li):
        lr = jnp.minimum(lr_raw, -1e-4)
        dr, di = lr * step, li * step
        mag = jnp.exp(dr)
        a_re, a_im = mag * jnp.cos(di), mag * jnp.sin(di)
        den = lr * lr + li * li
        nr = a_re - 1.0
        fr = (nr * lr + a_im * li) / den
        fi = (a_im * lr - nr * li) / den
        return dr, di, fr, fi

    dr_c, di_c, _, _ = discretise(lrc_ref[0], lic_ref[0])
    dr_r, di_r, fr_r, fi_r = discretise(lrr_ref[0], lir_ref[0])

    lane = lax.broadcasted_iota(jnp.int32, (1, S5_CONV), 1)
    t_row = lax.broadcasted_iota(jnp.int32, (1, CHUNK), 1).astype(F32)
    t_col = lax.broadcasted_iota(jnp.int32, (CHUNK, 1), 0).astype(F32)
    lag_of_lane = (lax.broadcasted_iota(jnp.int32, (CHUNK, S5_CONV), 1) // S5_GROUP
                   == lax.broadcasted_iota(jnp.int32, (CHUNK, S5_CONV), 0)).astype(F32)
    time_of_row = (lax.broadcasted_iota(jnp.int32, (S5_CONV, CHUNK), 0) // S5_GROUP
                   == lax.broadcasted_iota(jnp.int32, (S5_CONV, CHUNK), 1)).astype(F32)
    chan_of_lane = (lax.broadcasted_iota(jnp.int32, (S5_GROUP, S5_CONV), 1) % S5_GROUP
                    == lax.broadcasted_iota(jnp.int32, (S5_GROUP, S5_CONV), 0)).astype(F32)
    chan_of_row = (lax.broadcasted_iota(jnp.int32, (S5_CONV, S5_GROUP), 0) % S5_GROUP
                   == lax.broadcasted_iota(jnp.int32, (S5_CONV, S5_GROUP), 1)).astype(F32)
    ctr = _spread_cols(ctr_ref[0], chan_of_lane)
    cti = _spread_cols(cti_ref[0], chan_of_lane)

    def c_times_power(tf):
        mag = jnp.exp(dr_c * tf)
        ang = di_c * tf
        pr = _spread_cols(mag * jnp.cos(ang), lag_of_lane)
        pi = _spread_cols(mag * jnp.sin(ang), lag_of_lane)
        return ctr * pr - cti * pi, ctr * pi + cti * pr

    cpr, cpi = c_times_power(t_row)
    bbr = fr_r * brt_ref[0] - fi_r * bit_ref[0]
    bbi = fr_r * bit_ref[0] + fi_r * brt_ref[0]
    kt = _dot(bbr, cpr, HIGHEST) - _dot(bbi, cpi, HIGHEST)
    for s in range(CHUNK):
        shifted = kt if s == 0 else pltpu.roll(kt, S5_GROUP * s, axis=1)
        m_ref[0, S5_GROUP * s:S5_GROUP * (s + 1), :] = jnp.where(
            lane >= S5_GROUP * s, shifted, 0.0).astype(BF16)

    rem = CHUNK - 1.0 - t_col
    magw = jnp.exp(dr_r * rem)
    angw = di_r * rem
    pwr = _spread_rows(time_of_row, magw * jnp.cos(angw))
    pwi = _spread_rows(time_of_row, magw * jnp.sin(angw))
    bbtr = _spread_rows(chan_of_row, bbr)
    bbti = _spread_rows(chan_of_row, bbi)
    w_ref[0, :, :S5_STATE] = pwr * bbtr - pwi * bbti
    w_ref[0, :, S5_STATE:] = pwr * bbti + pwi * bbtr

    c1r, c1i = c_times_power(t_row + 1.0)
    wc_ref[0, :S5_STATE, :] = c1r.astype(BF16)
    wc_ref[0, S5_STATE:, :] = (-c1i).astype(BF16)

    full = float(CHUNK)
    mag_c = jnp.exp(dr_r * full)
    a_ref[0, 0:1, :] = mag_c * jnp.cos(di_r * full)
    a_ref[0, 1:2, :] = mag_c * jnp.sin(di_r * full)


def s5_prep(lam_re, lam_im, log_step, b_re, b_im, c_re, c_im):
    g, p = lam_re.shape
    brt = jnp.swapaxes(b_re, 1, 2)
    bit = jnp.swapaxes(b_im, 1, 2)
    args = (lam_re.reshape(g, p, 1), lam_im.reshape(g, p, 1),
            lam_re.reshape(g, 1, p), lam_im.reshape(g, 1, p), log_step.reshape(g, 1, 1),
            brt, bit, jnp.swapaxes(c_re, 1, 2), jnp.swapaxes(c_im, 1, 2))
    spec = lambda a: pl.BlockSpec((1,) + a.shape[1:], lambda i: (i, 0, 0))
    out_shape = [jax.ShapeDtypeStruct((g, S5_CONV, S5_CONV), BF16),
                 jax.ShapeDtypeStruct((g, S5_CONV, 2 * S5_STATE), F32),
                 jax.ShapeDtypeStruct((g, 2 * S5_STATE, S5_CONV), BF16),
                 jax.ShapeDtypeStruct((g, 2, S5_STATE), F32)]
    return pl.pallas_call(
        _s5_prep_kernel,
        grid=(g,),
        in_specs=[spec(a) for a in args],
        out_specs=[spec(o) for o in out_shape],
        out_shape=out_shape,
        compiler_params=_cparams(),
        name="s5_prep",
    )(*args)


def _s5_main_kernel(up_ref, us_ref, m_ref, w_ref, wc_ref, a_ref, xsr_ref, xsi_ref,
                    yp_ref, ys_ref, fpr_ref, fpi_ref, fsr_ref, fsi_ref, fin_scr, *, n_prompt, n_chunks):
    w3 = jnp.concatenate(_split3(w_ref[0]), axis=1)
    ar = a_ref[0, 0:1, :]
    ai = a_ref[0, 1:2, :]
    width = 2 * S5_STATE

    def local(u):
        h3 = _dot(u, w3)
        nblk = 4
        blk = S5_CONV // nblk
        cols = []
        for tb in range(nblk):
            acc = _dot(u[:, :blk], m_ref[0, :blk, tb * blk:(tb + 1) * blk])
            for sb in range(1, tb + 1):
                acc = acc + _dot(u[:, sb * blk:(sb + 1) * blk], m_ref[0, sb * blk:(sb + 1) * blk, tb * blk:(tb + 1) * blk])
            cols.append(acc)
        return jnp.concatenate(cols, axis=1), h3[:, :width] + h3[:, width:2 * width] + h3[:, 2 * width:]

    def times(pr, pi, x):
        return (jnp.concatenate([pr, pr], axis=1) * x
                + jnp.concatenate([-pi, pi], axis=1) * pltpu.roll(x, S5_STATE, axis=1))

    y_local, x = local(up_ref[0])
    chunk_of_row = lax.broadcasted_iota(jnp.int32, (n_prompt * n_chunks, 1), 0) % n_chunks
    pr, pi = ar, ai
    d = 1
    while d < n_chunks:
        x = x + times(pr, pi, jnp.where(chunk_of_row >= d, pltpu.roll(x, d, axis=0), 0.0))
        pr, pi = pr * pr - pi * pi, 2.0 * pr * pi
        d *= 2
    fin_scr[...] = x
    last = fin_scr[pl.ds(n_chunks - 1, n_prompt, stride=n_chunks), :]
    fpr_ref[0] = last[:, :S5_STATE]
    fpi_ref[0] = last[:, S5_STATE:]
    x0 = jnp.where(chunk_of_row >= 1, pltpu.roll(x, 1, axis=0), 0.0)
    yp_ref[0] = (y_local + _dot(x0.astype(BF16), wc_ref[0])).astype(BF16)

    y_local, hend = local(us_ref[0])
    x0 = jnp.concatenate([xsr_ref[0], xsi_ref[0]], axis=1)
    fin = times(ar, ai, x0) + hend
    fsr_ref[0] = fin[:, :S5_STATE]
    fsi_ref[0] = fin[:, S5_STATE:]
    ys_ref[0] = (y_local + _dot(x0.astype(BF16), wc_ref[0])).astype(BF16)


def s5_main(up_rows, us_rows, m, w, wc, a, xs_re, xs_im, n_prompt, n_chunks):
    g, r, _ = up_rows.shape
    n_sample = xs_re.shape[1]
    spec = lambda shape: pl.BlockSpec((1,) + tuple(shape[1:]), lambda i: (i, 0, 0))
    args = (up_rows, us_rows, m, w, wc, a, xs_re, xs_im)
    out_shape = [jax.ShapeDtypeStruct((g, r, S5_CONV), BF16),
                 jax.ShapeDtypeStruct((g, n_sample, S5_CONV), BF16),
                 jax.ShapeDtypeStruct((g, n_prompt, S5_STATE), F32),
                 jax.ShapeDtypeStruct((g, n_prompt, S5_STATE), F32),
                 jax.ShapeDtypeStruct((g, n_sample, S5_STATE), F32),
                 jax.ShapeDtypeStruct((g, n_sample, S5_STATE), F32)]
    return pl.pallas_call(
        functools.partial(_s5_main_kernel, n_prompt=n_prompt, n_chunks=n_chunks),
        grid=(g,),
        in_specs=[spec(x.shape) for x in args],
        out_specs=[spec(o.shape) for o in out_shape],
        out_shape=out_shape,
        scratch_shapes=[pltpu.VMEM((r, 2 * S5_STATE), F32)],
        compiler_params=_cparams(),
        name="s5_main",
    )(*args)


def _hgrn_kernel(f_ref, z_ref, s0_ref, lb_ref, ng_ref, o_ref, sfin_ref, st_scr, *, n_seq, chunk):
    c = pl.program_id(1)

    @pl.when(c == 0)
    def _():
        st_scr[...] = s0_ref[...]

    lbw = lb_ref[...]
    lbe = jnp.exp(lbw - jnp.max(lbw, axis=0, keepdims=True))
    lb_all = lbe[0:1, :] / jnp.sum(lbe, axis=0, keepdims=True)

    levels = [chunk >> (i + 1) for i in range(chunk.bit_length() - 1)]
    rowi = lax.broadcasted_iota(jnp.int32, (chunk, chunk), 0)
    coli = lax.broadcasted_iota(jnp.int32, (chunk, chunk), 1)
    rowk = lax.broadcasted_iota(jnp.int32, (chunk, HG_D), 0)
    sign, valid = [], []
    for m in levels:
        sign.append(jnp.where((rowk % (2 * m)) >= m, 1.0, -1.0))
        valid.append(((rowi // (2 * m)) == (coli // (2 * m)))
                     & ((rowi % (2 * m)) >= m) & ((coli % (2 * m)) < m))
    cum_mat = (coli <= rowi).astype(BF16)
    cum_mat3 = jnp.concatenate([cum_mat] * 3, axis=1)
    diag = rowi == coli
    nt = (((1,), (1,)), ((), ()))

    def body(n, carry):
        zf = f_ref[n]
        fg_all = lb_all + (1.0 - lb_all) * jax.nn.sigmoid(zf)
        cums = _dot(cum_mat3, jnp.concatenate(_split3(jnp.log2(fg_all)), axis=0))
        for hd in range(HG_HEADS):
            cols = slice(hd * HG_D, (hd + 1) * HG_D)
            zq = z_ref[n, :, hd * HG_D:(hd + 1) * HG_D].astype(F32)
            vb = z_ref[n, :, D_HG + hd * HG_D:D_HG + (hd + 1) * HG_D]
            zg = z_ref[n, :, 2 * D_HG + hd * HG_D:2 * D_HG + (hd + 1) * HG_D].astype(F32)
            q = zq * jax.nn.sigmoid(zq)
            kk = 1.0 - fg_all[:, cols]
            bcum = cums[:chunk, cols]
            b_last = bcum[chunk - 1:chunk, :]
            qb = q.astype(BF16)
            kb = kk.astype(BF16)
            st = st_scr[n, hd]

            scores = jnp.where(diag, lax.dot_general(qb, kb, nt, preferred_element_type=F32), 0.0)
            for lvl, m in enumerate(levels):
                if 2 * m >= SUBLANES:
                    bref = jnp.concatenate(
                        [jnp.broadcast_to(bcum[b * 2 * m + m - 1:b * 2 * m + m, :], (2 * m, HG_D))
                         for b in range(chunk // (2 * m))], axis=0)
                else:
                    offs = rowk % (2 * m) - (m - 1)
                    bref = bcum
                    for o in range(-(m - 1), m + 1):
                        if o != 0:
                            bref = jnp.where(offs == o, pltpu.roll(bcum, o % chunk, axis=0), bref)
                dec = jnp.exp2((bcum - bref) * sign[lvl]).astype(BF16)
                sc = lax.dot_general(qb * dec, kb * dec, nt, preferred_element_type=F32)
                scores = jnp.where(valid[lvl], sc, scores)

            qd = (q * jnp.exp2(bcum)).astype(BF16)
            o = lax.dot_general(qd, st.astype(BF16), nt, preferred_element_type=F32)
            o = o + _dot(scores.astype(BF16), vb)
            kdec = (kk * jnp.exp2(b_last - bcum)).astype(BF16)
            st_scr[n, hd] = jnp.exp2(b_last) * st + lax.dot_general(
                vb, kdec, (((0,), (0,)), ((), ())), preferred_element_type=F32)

            on = o * lax.rsqrt(jnp.mean(o * o, axis=-1, keepdims=True) + RMS_EPS) * ng_ref[:, cols]
            o_ref[n, :, hd * HG_D:(hd + 1) * HG_D] = on * (zg * jax.nn.sigmoid(zg))
        return carry

    lax.fori_loop(0, n_seq, body, 0, unroll=True)

    @pl.when(c == pl.num_programs(1) - 1)
    def _():
        sfin_ref[...] = st_scr[...]


def hgrn(f, z, s0_t, hg_lb, norm_g, n_seq, chunk):
    n, length, _ = z.shape
    return pl.pallas_call(
        functools.partial(_hgrn_kernel, n_seq=n_seq, chunk=chunk),
        grid=(n // n_seq, length // chunk),
        in_specs=[pl.BlockSpec((n_seq, chunk, D_HG), lambda g, c: (g, c, 0)),
                  pl.BlockSpec((n_seq, chunk, 3 * D_HG), lambda g, c: (g, c, 0)),
                  pl.BlockSpec((n_seq, HG_HEADS, HG_D, HG_D), lambda g, c: (g, 0, 0, 0)),
                  pl.BlockSpec(hg_lb.shape, lambda g, c: (0, 0)),
                  pl.BlockSpec((1, D_HG), lambda g, c: (0, 0))],
        out_specs=[pl.BlockSpec((n_seq, chunk, D_HG), lambda g, c: (g, c, 0)),
                   pl.BlockSpec((n_seq, HG_HEADS, HG_D, HG_D), lambda g, c: (g, 0, 0, 0))],
        out_shape=[jax.ShapeDtypeStruct((n, length, D_HG), F32),
                   jax.ShapeDtypeStruct((n, HG_HEADS, HG_D, HG_D), F32)],
        scratch_shapes=[pltpu.VMEM((n_seq, HG_HEADS, HG_D, HG_D), F32)],
        compiler_params=_cparams(2),
        name="hgrn",
    )(f, z, s0_t, hg_lb, norm_g)


def _post_mix_kernel(h_ref, up_ref, us_ref, yp_ref, ys_ref, hgp_ref, hgs_ref,
                     d_ref, wglu_ref, bglu_ref, s5g_ref, wout_ref, g1_ref, b1_ref, rwt_ref, rb_ref,
                     h1_ref, slot_ref, gate_ref, before_ref, cnt_ref, run_scr, *, tm, n_first):
    i = pl.program_id(0)

    @pl.when(i == 0)
    def _():
        run_scr[...] = jnp.zeros_like(run_scr)

    def phase(u_ref, y_ref, hg_ref):
        ys = _read_lane_blocks(y_ref) + d_ref[...] * _read_lane_blocks(u_ref)
        gl = 0.5 * ys * (1.0 + lax.erf(ys * (2.0 ** -0.5)))
        s5o = gl * jax.nn.sigmoid(_dot(gl.astype(BF16), wglu_ref[...]) + bglu_ref[...])
        s5o = s5o * lax.rsqrt(jnp.mean(s5o * s5o, axis=-1, keepdims=True) + RMS_EPS) * s5g_ref[...]
        mix = (_dot(s5o.astype(BF16), wout_ref[:D_S5, :])
               + _dot(hg_ref[...].astype(BF16), wout_ref[D_S5:, :]))
        h1 = _layernorm(DEEPNORM_ALPHA * h_ref[...] + mix, g1_ref[...], b1_ref[...])
        _write_row_tiles(h1_ref, h1, tm)

        h_hi, h_mid, _ = _split3(h1)
        w_hi, w_mid, _ = _split3(rwt_ref[...])
        nt = (((1,), (1,)), ((), ()))
        logits = (lax.dot_general(w_hi, h_hi, nt, preferred_element_type=F32)
                  + lax.dot_general(w_hi, h_mid, nt, preferred_element_type=F32)
                  + lax.dot_general(w_mid, h_hi, nt, preferred_element_type=F32)) + rb_ref[...]
        eid = lax.broadcasted_iota(jnp.int32, (N_EXPERTS, tm), 0)
        vals, idxs = [], []
        for _ in range(TOP_K):
            m = jnp.max(logits, axis=0, keepdims=True)
            ix = jnp.min(jnp.where(logits == m, eid, N_EXPERTS), axis=0, keepdims=True)
            vals.append(m)
            idxs.append(ix)
            logits = jnp.where(eid == ix, -jnp.inf, logits)
        exps = [jnp.exp(v - vals[0]) for v in vals]
        den = exps[0] + exps[1] + exps[2] + exps[3]

        onehot = jnp.zeros((N_EXPERTS, tm), F32)
        for ix in idxs:
            onehot = onehot + (eid == ix).astype(F32)
        rowi = lax.broadcasted_iota(jnp.int32, (tm, tm), 0)
        coli = lax.broadcasted_iota(jnp.int32, (tm, tm), 1)
        earlier = (rowi < coli).astype(BF16)
        prefix = _dot(onehot.astype(BF16), earlier)
        tile_cnt = jnp.sum(onehot, axis=1, keepdims=True)
        for k in range(TOP_K):
            lower_experts = jnp.sum(jnp.where(eid < idxs[k], tile_cnt, 0.0), axis=0, keepdims=True)
            rank = jnp.sum(jnp.where(eid == idxs[k], prefix, 0.0), axis=0, keepdims=True)
            slot_ref[0, :, k * tm:(k + 1) * tm] = (lower_experts + rank).astype(jnp.int32) * ROW_TILES
            gate_ref[0, :, k * tm:(k + 1) * tm] = exps[k] / den
        before_ref[0] = run_scr[...]
        run_scr[...] = run_scr[...] + tile_cnt
        cnt_ref[...] = run_scr[...]

    pl.when(i < n_first)(lambda: phase(up_ref, yp_ref, hgp_ref))
    pl.when(i >= n_first)(lambda: phase(us_ref, ys_ref, hgs_ref))


def post_mix(h0, u_pair, y_pair, hg_pair, d_skip, wglu, bglu, s5g, wout, g1, b1, rw_t, rb_col, tm):
    t = h0.shape[0]
    n_first = u_pair[0].shape[1] // tm
    row = lambda i: (i, 0)
    fixed = lambda i: (0, 0)
    full = lambda a: pl.BlockSpec(a.shape, fixed)
    weights = (d_skip, wglu, bglu, s5g, wout, g1, b1, rw_t, rb_col)
    return pl.pallas_call(
        functools.partial(_post_mix_kernel, tm=tm, n_first=n_first),
        grid=(t // tm,),
        in_specs=[pl.BlockSpec((tm, D_MODEL), row)]
                 + _two_phase_lane_block_specs(tm, n_first) * 2 + _two_phase_specs((tm, D_HG), n_first)
                 + [full(a) for a in weights],
        out_specs=[pl.BlockSpec((tm * ROW_TILES, LANES), row),
                   pl.BlockSpec((1, 1, TOP_K * tm), lambda i: (i, 0, 0)),
                   pl.BlockSpec((1, 1, TOP_K * tm), lambda i: (i, 0, 0)),
                   pl.BlockSpec((1, N_EXPERTS, 1), lambda i: (i, 0, 0)),
                   pl.BlockSpec((N_EXPERTS, 1), fixed)],
        out_shape=[jax.ShapeDtypeStruct((t * ROW_TILES, LANES), F32),
                   jax.ShapeDtypeStruct((t // tm, 1, TOP_K * tm), jnp.int32),
                   jax.ShapeDtypeStruct((t // tm, 1, TOP_K * tm), F32),
                   jax.ShapeDtypeStruct((t // tm, N_EXPERTS, 1), F32),
                   jax.ShapeDtypeStruct((N_EXPERTS, 1), F32)],
        scratch_shapes=[pltpu.VMEM((N_EXPERTS, 1), F32)],
        compiler_params=_cparams(),
        name="post_mix",
    )(h0, *u_pair, *y_pair, *hg_pair, *weights)


def _segment_copies(meta_ref, tile, tm, make_copy):
    for e in range(N_EXPERTS):
        sorted_row = meta_ref[tile, e]
        cnt = meta_ref[tile, N_EXPERTS + e]
        staged_row = meta_ref[tile, 2 * N_EXPERTS + e]
        for b in range(tm.bit_length()):
            done = cnt & ((1 << b) - 1)

            @pl.when(((cnt >> b) & 1) == 1)
            def _(b=b, done=done, e=e):
                make_copy(staged_row + done, sorted_row + done, 1 << b).start(priority=e % 2)


def _dispatch_kernel(meta_ref, pend_ref, slot_ref, h_ref, xs_ref, stage, zero_scr, zsem, sem, *, tm):
    n_rows = xs_ref.shape[0] // ROW_TILES
    i = pl.program_id(0)
    n = pl.num_programs(0)
    slot = i % 2

    def drain(s):
        pltpu.make_async_copy(stage.at[s], xs_ref.at[pl.ds(0, TOP_K * tm * ROW_TILES)], sem.at[s]).wait()

    @pl.when(i == 0)
    def _():
        zero_scr[...] = jnp.zeros_like(zero_scr)

        def last_block(e):
            prev = pend_ref[e - 1] if e > 0 else 0
            copy = pltpu.make_async_copy(
                zero_scr, xs_ref.at[pl.ds(pl.multiple_of(jnp.maximum(pend_ref[e] - MOE_ROWS, 0) * ROW_TILES,
                                                         ROW_TILES), MOE_ROWS * ROW_TILES)], zsem)
            return pend_ref[e] > prev, copy

        def tail_block(j):
            row0 = pend_ref[N_EXPERTS - 1] + j * MOE_ROWS
            copy = pltpu.make_async_copy(
                zero_scr, xs_ref.at[pl.ds(pl.multiple_of(jnp.minimum(row0, n_rows - MOE_ROWS) * ROW_TILES,
                                                         ROW_TILES), MOE_ROWS * ROW_TILES)], zsem)
            return row0 < n_rows, copy

        blocks = [last_block(e) for e in range(N_EXPERTS)] + [tail_block(j) for j in range(N_EXPERTS)]
        for used, copy in blocks:
            pl.when(used)(copy.start)
        for used, copy in blocks:
            pl.when(used)(copy.wait)

    pl.when(i >= 2)(lambda: drain(slot))

    def regroup(s):
        def body(t, carry):
            row = h_ref[_row_tile(t), :]
            for k in range(TOP_K):
                stage[s, pl.ds(pl.multiple_of(slot_ref[k * tm + t], ROW_TILES), ROW_TILES), :] = row
            return carry

        lax.fori_loop(0, tm, body, 0, unroll=8)

    pl.when(slot == 0)(lambda: regroup(0))
    pl.when(slot == 1)(lambda: regroup(1))
    _segment_copies(meta_ref, i, tm, lambda staged_row, sorted_row, rows: pltpu.make_async_copy(
        stage.at[slot, _row_tiles(staged_row, rows)], xs_ref.at[_row_tiles(sorted_row, rows)], sem.at[slot]))

    @pl.when(i == n - 1)
    def _():
        pl.when(n >= 2)(lambda: drain(1 - slot))
        drain(slot)


def dispatch(meta, pend, slots_flat, h1, n_rows, tm):
    t = h1.shape[0] // ROW_TILES
    grid_spec = pltpu.PrefetchScalarGridSpec(
        num_scalar_prefetch=2,
        grid=(t // tm,),
        in_specs=[pl.BlockSpec((TOP_K * tm,), lambda i, meta, pend: (i,), memory_space=pltpu.SMEM),
                  pl.BlockSpec((tm * ROW_TILES, LANES), lambda i, meta, pend: (i, 0))],
        out_specs=pl.BlockSpec(memory_space=pl.ANY),
        scratch_shapes=[pltpu.VMEM((2, TOP_K * tm * ROW_TILES, LANES), F32),
                        pltpu.VMEM((MOE_ROWS * ROW_TILES, LANES), F32),
                        pltpu.SemaphoreType.DMA(()), pltpu.SemaphoreType.DMA((2,))],
    )
    return pl.pallas_call(
        functools.partial(_dispatch_kernel, tm=tm),
        grid_spec=grid_spec,
        out_shape=jax.ShapeDtypeStruct((n_rows * ROW_TILES, LANES), F32),
        compiler_params=_cparams(),
        name="moe_dispatch",
    )(meta, pend, slots_flat, h1)


def _moe_ffn_kernel(be_ref, nu_ref, seg_ref, nxt_ref, x_ref, wg_ref, bg_ref, wu_ref, bu_ref, wd_ref, bd_ref,
                    y_ref, wbuf, wbf, sem):
    i = pl.program_id(0)
    hbm = (wg_ref, wu_ref, wd_ref)

    def weight_copies(expert, s):
        return [pltpu.make_async_copy(hbm[j].at[expert], wbuf.at[s, j], sem.at[s, j]) for j in range(3)]

    @pl.when((i == 0) | (be_ref[i] != be_ref[jnp.maximum(i - 1, 0)]))
    def _():
        s = seg_ref[i] % 2

        @pl.when(i == 0)
        def _():
            for c in weight_copies(be_ref[0], 0):
                c.start()

        for j, c in enumerate(weight_copies(be_ref[i], s)):
            c.wait()
            wbf[j] = wbuf[s, j].astype(BF16)

        @pl.when(nxt_ref[i] >= 0)
        def _():
            for c in weight_copies(nxt_ref[i], 1 - s):
                c.start()

    @pl.when(i < nu_ref[0])
    def _():
        x = _read_row_tiles(x_ref, MOE_ROWS).astype(BF16)
        gt = jnp.minimum(_dot(x, wbf[0]) + bg_ref[0], SWIGLU_LIMIT)
        up = jnp.clip(_dot(x, wbf[1]) + bu_ref[0], -SWIGLU_LIMIT, SWIGLU_LIMIT)
        hid = (up + 1.0) * (gt * jax.nn.sigmoid(SWIGLU_ALPHA * gt))
        _write_row_tiles(y_ref, _dot(hid.astype(BF16), wbf[2]) + bd_ref[0], MOE_ROWS)

    @pl.when(i >= nu_ref[0])
    def _():
        y_ref[...] = jnp.zeros_like(y_ref)


def moe_ffn(block_e, n_used, segment, next_e, xs, wg, bg, wu, bu, wd, bd):
    n_rows = xs.shape[0] // ROW_TILES
    n_blocks = n_rows // MOE_ROWS
    wsel = lambda i, be, nu, seg, nxt: (be[i], 0, 0)
    d_ff = wg.shape[-1]
    assert wg.shape[1:] == wu.shape[1:] == wd.shape[1:] == (D_MODEL, D_MODEL)
    anywhere = pl.BlockSpec(memory_space=pl.ANY)
    grid_spec = pltpu.PrefetchScalarGridSpec(
        num_scalar_prefetch=4,
        grid=(n_blocks,),
        in_specs=[pl.BlockSpec((MOE_ROWS * ROW_TILES, LANES),
                               lambda i, be, nu, seg, nxt: (jnp.minimum(i, nu[0] - 1), 0)),
                  anywhere, pl.BlockSpec((1, 1, d_ff), wsel),
                  anywhere, pl.BlockSpec((1, 1, d_ff), wsel),
                  anywhere, pl.BlockSpec((1, 1, D_MODEL), wsel)],
        out_specs=pl.BlockSpec((MOE_ROWS * ROW_TILES, LANES), lambda i, be, nu, seg, nxt: (i, 0)),
        scratch_shapes=[pltpu.VMEM((2, 3, D_MODEL, D_MODEL), F32), pltpu.VMEM((3, D_MODEL, D_MODEL), BF16),
                        pltpu.SemaphoreType.DMA((2, 3))],
    )
    return pl.pallas_call(
        _moe_ffn_kernel,
        grid_spec=grid_spec,
        out_shape=jax.ShapeDtypeStruct((n_rows * ROW_TILES, LANES), F32),
        compiler_params=_cparams(),
        name="moe_ffn",
    )(block_e, n_used, segment, next_e, xs, wg, bg, wu, bu, wd, bd)


def _combine_kernel(meta_ref, slot_ref, gate_ref, h_ref, pp_ref, ps_ref, yb_ref,
                    plew_ref, plegw_ref, g2_ref, b2_ref, outp_ref, outs_ref, buf, r_scr, sem,
                    *, tm, n_first):
    i = pl.program_id(0)
    n = pl.num_programs(0)
    slot = i % 2

    def fetch(tile, s):
        _segment_copies(meta_ref, tile, tm, lambda staged_row, sorted_row, rows: pltpu.make_async_copy(
            yb_ref.at[_row_tiles(sorted_row, rows)], buf.at[s, _row_tiles(staged_row, rows)], sem.at[s]))

    pl.when(i == 0)(lambda: fetch(0, 0))
    pl.when(i + 1 < n)(lambda: fetch(i + 1, 1 - slot))
    pltpu.make_async_copy(yb_ref.at[pl.ds(0, TOP_K * tm * ROW_TILES)], buf.at[slot], sem.at[slot]).wait()

    def weighted_sum(s):
        def body(t, carry):
            acc = DEEPNORM_ALPHA * h_ref[_row_tile(t), :]
            for k in range(TOP_K):
                acc = acc + gate_ref[k * tm + t] * buf[s, pl.ds(pl.multiple_of(slot_ref[k * tm + t], ROW_TILES),
                                                               ROW_TILES), :]
            r_scr[_row_tile(t), :] = acc
            return carry

        lax.fori_loop(0, tm, body, 0, unroll=8)

    pl.when(slot == 0)(lambda: weighted_sum(0))
    pl.when(slot == 1)(lambda: weighted_sum(1))
    r = _read_row_tiles(r_scr, tm)
    gate = jax.nn.sigmoid(_dot(r.astype(BF16), plegw_ref[...]))

    def finish(p, store):
        e = _dot(p.astype(BF16), plew_ref[...]) * gate
        store(_layernorm(r + e, g2_ref[...], b2_ref[...]))

    def store_prompt(v):
        outp_ref[0] = v

    def store_sample(v):
        outs_ref[...] = v

    pl.when(i < n_first)(lambda: finish(pp_ref[0], store_prompt))
    pl.when(i >= n_first)(lambda: finish(ps_ref[...], store_sample))


def combine(meta, slots_flat, gates_flat, h1, p_prompt, p_sample, yb, plew, plegw, g2, b2, tm):
    t = h1.shape[0] // ROW_TILES
    nb, seq, _ = p_prompt.shape
    ts = p_sample.shape[0]
    n_first = nb * seq // tm
    fixed = lambda i, meta: (0, 0)
    flat = pl.BlockSpec((TOP_K * tm,), lambda i, meta: (i,), memory_space=pltpu.SMEM)
    sample = lambda width: pl.BlockSpec((tm, width), lambda i, meta: (jnp.maximum(i - n_first, 0), 0))
    prompt = lambda width: pl.BlockSpec((1, tm, width), _prompt_spec(tm, seq, width, n_first).index_map)
    with_meta = lambda spec: pl.BlockSpec(spec.block_shape, lambda i, meta: spec.index_map(i))
    grid_spec = pltpu.PrefetchScalarGridSpec(
        num_scalar_prefetch=1,
        grid=(t // tm,),
        in_specs=[flat, flat,
                  pl.BlockSpec((tm * ROW_TILES, LANES), lambda i, meta: (i, 0)),
                  with_meta(prompt(PLE_DIM)), sample(PLE_DIM),
                  pl.BlockSpec(memory_space=pl.ANY),
                  pl.BlockSpec(plew.shape, fixed), pl.BlockSpec(plegw.shape, fixed),
                  pl.BlockSpec((1, D_MODEL), fixed), pl.BlockSpec((1, D_MODEL), fixed)],
        out_specs=[with_meta(prompt(D_MODEL)), sample(D_MODEL)],
        scratch_shapes=[pltpu.VMEM((2, TOP_K * tm * ROW_TILES, LANES), F32),
                        pltpu.VMEM((tm * ROW_TILES, LANES), F32),
                        pltpu.SemaphoreType.DMA((2,))],
    )
    return pl.pallas_call(
        functools.partial(_combine_kernel, tm=tm, n_first=n_first),
        grid_spec=grid_spec,
        out_shape=[jax.ShapeDtypeStruct((nb, seq, D_MODEL), F32), jax.ShapeDtypeStruct((ts, D_MODEL), F32)],
        compiler_params=_cparams(),
        name="moe_combine",
    )(meta, slots_flat, gates_flat, h1, p_prompt, p_sample, yb, plew, plegw, g2, b2)


def _row(v):
    return v.reshape(1, -1)


def kernel(x_prompt, x_sample, state_s5_re, state_s5_im, state_hgrn, p_prompt, p_sample, ln_in_g, ln_in_b, w_in, s5_lambda_re, s5_lambda_im, s5_log_step, s5_b_re, s5_b_im, s5_c_re, s5_c_im, s5_d, s5_w_glu, s5_b_glu, s5_norm_g, hg_lb, hg_norm_g, w_out, ln1_g, ln1_b, router_w, router_b, w_gate, b_gate, w_up, b_up, w_down, b_down, ple_w, ple_gate_w, ln2_g, ln2_b):
    nb, seq, _ = x_prompt.shape
    ns, dseq, _ = x_sample.shape
    assert dseq == CHUNK and seq % CHUNK == 0 and w_in.shape[0] == 1
    nc = seq // CHUNK
    tp, ts = nb * seq, ns * dseq
    t = tp + ts
    tm = 512 if (tp % 512 == 0 and ts % 512 == 0) else 256
    assert tp % tm == 0 and ts % tm == 0 and seq % tm == 0 and ns % nb == 0

    w_cols = jnp.split(w_in[0], [D_S5, D_S5 + D_HG, D_S5 + 2 * D_HG], axis=1)
    h0, u_p, u_s, f_p, f_s, z_p, z_s = ln_in_proj(
        x_prompt, x_sample.reshape(ts, D_MODEL), _row(ln_in_g), _row(ln_in_b),
        jnp.concatenate([w_cols[0], w_cols[2], w_cols[1], w_cols[3]], axis=1).astype(BF16), tm)

    m, w, wc, a = s5_prep(s5_lambda_re[0], s5_lambda_im[0], s5_log_step[0],
                          s5_b_re[0], s5_b_im[0], s5_c_re[0], s5_c_im[0])
    rb = 64 if (nb * nc) % 64 == 0 else nc
    yp_rows, ys_rows, fpr, fpi, fsr, fsi = s5_main(s5_rows(u_p, rb), s5_rows(u_s, ns), m, w, wc, a,
                                                   jnp.swapaxes(state_s5_re[0], 0, 1),
                                                   jnp.swapaxes(state_s5_im[0], 0, 1), nb, nc)
    y_pair = (s5_tokens(yp_rows, rb), s5_tokens(ys_rows, ns))

    zero_state = jnp.zeros((nb, HG_HEADS, HG_D, HG_D), F32)
    ng = _row(hg_norm_g[0])
    o_p, st_p = hgrn(f_p.reshape(nb, seq, D_HG), z_p.reshape(nb, seq, 3 * D_HG), zero_state, hg_lb, ng, nb,
                     HG_CHUNK if seq % HG_CHUNK == 0 else CHUNK)
    o_s, st_s = hgrn(f_s.reshape(ns, dseq, D_HG), z_s.reshape(ns, dseq, 3 * D_HG), jnp.swapaxes(state_hgrn[0], 2, 3),
                     hg_lb, ng, nb, dseq)

    h1, slots, gates, before, counts = post_mix(
        h0, (u_p, u_s), y_pair, (o_p.reshape(tp, D_HG), o_s.reshape(ts, D_HG)),
        _row(s5_d[0]), s5_w_glu[0].astype(BF16), _row(s5_b_glu[0]),
        _row(s5_norm_g[0]), w_out[0].astype(BF16), _row(ln1_g[0]), _row(ln1_b[0]),
        router_w[0].T, router_b[0].reshape(N_EXPERTS, 1), tm)

    counts = counts[:, 0].astype(jnp.int32)
    before = before[:, :, 0].astype(jnp.int32)
    cnt = jnp.concatenate([before[1:], counts[None]], axis=0) - before
    padded = (counts + MOE_ROWS - 1) // MOE_ROWS * MOE_ROWS
    pend = jnp.cumsum(padded)
    staged = jnp.cumsum(cnt, axis=1) - cnt
    meta = jnp.concatenate([pend - padded + before, cnt, staged, jnp.zeros_like(cnt)], axis=1)
    experts = jnp.arange(N_EXPERTS, dtype=jnp.int32)

    n_blocks = -(-t * TOP_K // MOE_ROWS) + N_EXPERTS
    n_used = (pend[-1] // MOE_ROWS).astype(jnp.int32)
    blk = jnp.arange(n_blocks, dtype=jnp.int32)
    blk = jnp.minimum(blk, n_used - 1)
    block_e = jnp.sum((pend[None, :] <= (blk * MOE_ROWS)[:, None]).astype(jnp.int32), axis=1)
    block_e = jnp.minimum(block_e, N_EXPERTS - 1)
    owns_rows = padded > 0
    owner = owns_rows[None, :]
    segment = jnp.sum((owner & (experts[None, :] <= block_e[:, None])).astype(jnp.int32), axis=1) - 1
    later = jnp.where(owner & (experts[None, :] > block_e[:, None]), experts[None, :], N_EXPERTS)
    next_e = jnp.min(later, axis=1)
    next_e = jnp.where(next_e < N_EXPERTS, next_e, -1).astype(jnp.int32)

    slots_flat = slots.reshape(-1)
    xs = dispatch(meta, pend, slots_flat, h1, n_blocks * MOE_ROWS, tm)
    yb = moe_ffn(block_e, n_used.reshape(1), segment, next_e, xs,
                 w_gate[0], b_gate[0][:, None, :], w_up[0], b_up[0][:, None, :],
                 w_down[0], b_down[0][:, None, :])
    out_p, out_s = combine(meta, slots_flat, gates.reshape(-1), h1, p_prompt[0], p_sample[0].reshape(ts, PLE_DIM),
                           yb, ple_w[0].astype(BF16), ple_gate_w[0].astype(BF16),
                           _row(ln2_g[0]), _row(ln2_b[0]), tm)

    def s5_state(f, n):
        return jnp.swapaxes(f, 0, 1).reshape(1, n, S5_GROUPS, S5_STATE)

    return (out_p, out_s.reshape(ns, dseq, D_MODEL),
            s5_state(fpr, nb), s5_state(fpi, nb), jnp.swapaxes(st_p, 2, 3)[None],
            s5_state(fsr, ns), s5_state(fsi, ns), jnp.swapaxes(st_s, 2, 3)[None])
```

```python
import functools

import jax
import jax.numpy as jnp
from jax import lax
from jax.experimental import pallas as pl
from jax.experimental.pallas import tpu as pltpu

F32 = jnp.float32
BF16 = jnp.bfloat16
HIGHEST = lax.Precision.HIGHEST

D_MODEL = 1024
CHUNK = 64
PLE_DIM = 256
D_S5 = 512
S5_GROUP = 16
S5_GROUPS = 32
S5_STATE = 64
D_HG = 512
HG_HEADS = 4
HG_D = 128
D_IN = D_S5 + 4 * D_HG
N_EXPERTS = 32
TOP_K = 4
SWIGLU_LIMIT = 7.0
SWIGLU_ALPHA = 1.702
DEEPNORM_ALPHA = 2.0 ** 0.25
LN_EPS = 1e-5
RMS_EPS = 1e-6

LANES = 128
SUBLANES = 8
ROW_TILES = D_MODEL // LANES
S5_CONV = CHUNK * S5_GROUP
HG_CHUNK = 128
MOE_ROWS = 512
VMEM_LIMIT = 56 * 1024 * 1024

assert ROW_TILES == SUBLANES


def _cparams(n_axes=1):
    return pltpu.CompilerParams(dimension_semantics=("arbitrary",) * n_axes,
                                vmem_limit_bytes=VMEM_LIMIT)


def _dot(a, b, precision=None):
    return jnp.dot(a, b, preferred_element_type=F32, precision=precision)


def _split3(w):
    hi = w.astype(BF16)
    r1 = w - hi.astype(F32)
    mid = r1.astype(BF16)
    lo = (r1 - mid.astype(F32)).astype(BF16)
    return hi, mid, lo


def _spread_cols(table, onehot):
    return _dot(jnp.concatenate(_split3(table), axis=1), jnp.concatenate([onehot.astype(BF16)] * 3, axis=0))


def _spread_rows(onehot, table):
    return _dot(jnp.concatenate([onehot.astype(BF16)] * 3, axis=1), jnp.concatenate(_split3(table), axis=0))


def _sigmoid(x):
    return 0.5 * jnp.tanh(0.5 * x) + 0.5


def _layernorm(x, g, b):
    mu = jnp.mean(x, axis=-1, keepdims=True)
    xc = x - mu
    var = jnp.mean(xc * xc, axis=-1, keepdims=True)
    return xc * lax.rsqrt(var + LN_EPS) * g + b


def _two_phase_specs(block, n_first):
    nd = len(block)
    first = pl.BlockSpec(block, lambda i: (jnp.minimum(i, n_first - 1),) + (0,) * (nd - 1))
    second = pl.BlockSpec(block, lambda i: (jnp.maximum(i - n_first, 0),) + (0,) * (nd - 1))
    return [first, second]


def _two_phase_lane_block_specs(tm, n_first):
    nblk = D_S5 // LANES
    first = pl.BlockSpec((nblk, tm, LANES), lambda i: (0, jnp.minimum(i, n_first - 1), 0))
    second = pl.BlockSpec((nblk, tm, LANES), lambda i: (0, jnp.maximum(i - n_first, 0), 0))
    return [first, second]


def _read_lane_blocks(ref):
    return jnp.concatenate([ref[j] for j in range(D_S5 // LANES)], axis=1)


def _prompt_spec(tm, seq, width, n_first):
    per_seq = seq // tm

    def index(i):
        ic = jnp.minimum(i, n_first - 1)
        return (ic // per_seq, ic % per_seq, 0)

    return pl.BlockSpec((1, tm, width), index)


def _chunk(rows, j):
    return pl.ds(j, rows, stride=ROW_TILES)


def _read_row_tiles(ref, rows):
    return jnp.concatenate([ref[_chunk(rows, j), :] for j in range(ROW_TILES)], axis=1)


def _write_row_tiles(ref, val, rows):
    for j in range(ROW_TILES):
        ref[_chunk(rows, j), :] = val[:, j * LANES:(j + 1) * LANES]


def _row_tiles(r, n=1):
    return pl.ds(pl.multiple_of(r * ROW_TILES, ROW_TILES), n * ROW_TILES)


def _row_tile(r):
    return _row_tiles(r)


def _ln_in_proj_kernel(xp_ref, xs_ref, g_ref, b_ref, w_ref, h_ref, up_ref, us_ref, fp_ref, fs_ref, zp_ref, zs_ref,
                       *, n_first):
    def phase(x, u_ref, f_ref, z_ref):
        h = _layernorm(x, g_ref[...], b_ref[...])
        h_ref[...] = h
        hb = h.astype(BF16)
        u = _dot(hb, w_ref[:, :D_S5])
        for j in range(D_S5 // LANES):
            u_ref[j] = u[:, j * LANES:(j + 1) * LANES]
        f_ref[...] = _dot(hb, w_ref[:, D_S5:D_S5 + D_HG])
        z_ref[...] = _dot(hb, w_ref[:, D_S5 + D_HG:]).astype(BF16)

    i = pl.program_id(0)
    pl.when(i < n_first)(lambda: phase(xp_ref[0], up_ref, fp_ref, zp_ref))
    pl.when(i >= n_first)(lambda: phase(xs_ref[...], us_ref, fs_ref, zs_ref))


def ln_in_proj(xp, xs, g, b, w_bf16, tm):
    nb, seq, _ = xp.shape
    tp, ts = nb * seq, xs.shape[0]
    n_first = tp // tm
    fixed = lambda i: (0, 0)
    return pl.pallas_call(
        functools.partial(_ln_in_proj_kernel, n_first=n_first),
        grid=((tp + ts) // tm,),
        in_specs=[_prompt_spec(tm, seq, D_MODEL, n_first), _two_phase_specs((tm, D_MODEL), n_first)[1]]
                 + [pl.BlockSpec((1, D_MODEL), fixed), pl.BlockSpec((1, D_MODEL), fixed),
                    pl.BlockSpec((D_MODEL, D_IN), fixed)],
        out_specs=[pl.BlockSpec((tm, D_MODEL), lambda i: (i, 0))]
                  + _two_phase_lane_block_specs(tm, n_first)
                  + _two_phase_specs((tm, D_HG), n_first)
                  + _two_phase_specs((tm, 3 * D_HG), n_first),
        out_shape=[jax.ShapeDtypeStruct((tp + ts, D_MODEL), F32),
                   jax.ShapeDtypeStruct((D_S5 // LANES, tp, LANES), F32),
                   jax.ShapeDtypeStruct((D_S5 // LANES, ts, LANES), F32),
                   jax.ShapeDtypeStruct((tp, D_HG), F32), jax.ShapeDtypeStruct((ts, D_HG), F32),
                   jax.ShapeDtypeStruct((tp, 3 * D_HG), BF16), jax.ShapeDtypeStruct((ts, 3 * D_HG), BF16)],
        compiler_params=_cparams(),
        name="ln_in_proj",
    )(xp, xs, g, b, w_bf16)


def _lane_block(j):
    return slice(j * LANES, (j + 1) * LANES)


def _granule_transpose(slabs):
    per_block = LANES // S5_GROUP
    granule = lax.broadcasted_iota(jnp.int32, (1, LANES), 1) // S5_GROUP
    x = list(slabs)
    for d in (4, 2, 1):
        keep = (granule & d) == 0
        y = [None] * per_block
        for i in range(per_block):
            if i & d == 0:
                y[i] = jnp.where(keep, x[i], pltpu.roll(x[i + d], d * S5_GROUP, axis=1))
                y[i + d] = jnp.where(keep, pltpu.roll(x[i], LANES - d * S5_GROUP, axis=1), x[i + d])
        x = y
    return x


def _sublane_transpose(tiles):
    sub = lax.broadcasted_iota(jnp.int32, (SUBLANES, 1), 0)
    x = list(tiles)
    for d in (4, 2, 1):
        keep = (sub & d) == 0
        y = [None] * SUBLANES
        for i in range(SUBLANES):
            if i & d == 0:
                y[i] = jnp.where(keep, x[i], pltpu.roll(x[i + d], d, axis=0))
                y[i + d] = jnp.where(keep, pltpu.roll(x[i], SUBLANES - d, axis=0), x[i + d])
        x = y
    return x


def _s5_rows_kernel(u_ref, o_ref, *, chunks):
    per_block = LANES // S5_GROUP
    for gcol in range(D_S5 // LANES):
        for j in range(S5_CONV // LANES):
            groups = [_sublane_transpose([u_ref[gcol, pl.ds((cg * SUBLANES + c) * CHUNK + per_block * j, SUBLANES), :]
                                          for c in range(SUBLANES)]) for cg in range(chunks // SUBLANES)]
            by_time = [jnp.concatenate([grp[sl] for grp in groups], axis=0) for sl in range(per_block)]
            for gl, rows in enumerate(_granule_transpose(by_time)):
                o_ref[gcol * per_block + gl, :, _lane_block(j)] = rows.astype(BF16)


def s5_rows(u, chunks):
    nblk, t, _ = u.shape
    assert chunks % SUBLANES == 0
    r = t // CHUNK
    return pl.pallas_call(
        functools.partial(_s5_rows_kernel, chunks=chunks),
        grid=(r // chunks,),
        in_specs=[pl.BlockSpec((nblk, chunks * CHUNK, LANES), lambda i: (0, i, 0))],
        out_specs=pl.BlockSpec((S5_GROUPS, chunks, S5_CONV), lambda i: (0, i, 0)),
        out_shape=jax.ShapeDtypeStruct((S5_GROUPS, r, S5_CONV), BF16),
        compiler_params=_cparams(),
        name="s5_rows",
    )(u)


def _s5_tokens_kernel(y_ref, o_ref, *, chunks):
    per_block = LANES // S5_GROUP
    for gcol in range(D_S5 // LANES):
        for j in range(S5_CONV // LANES):
            by_group = [y_ref[gcol * per_block + gl, :, _lane_block(j)].astype(F32) for gl in range(per_block)]
            by_time = _granule_transpose(by_group)
            for cg in range(chunks // SUBLANES):
                tiles = _sublane_transpose([rows[cg * SUBLANES:(cg + 1) * SUBLANES, :] for rows in by_time])
                for c in range(SUBLANES):
                    o_ref[gcol, pl.ds((cg * SUBLANES + c) * CHUNK + per_block * j, SUBLANES), :] = tiles[c]


def s5_tokens(y_rows, chunks):
    _, r, _ = y_rows.shape
    assert chunks % SUBLANES == 0
    return pl.pallas_call(
        functools.partial(_s5_tokens_kernel, chunks=chunks),
        grid=(r // chunks,),
        in_specs=[pl.BlockSpec((S5_GROUPS, chunks, S5_CONV), lambda i: (0, i, 0))],
        out_specs=pl.BlockSpec((D_S5 // LANES, chunks * CHUNK, LANES), lambda i: (0, i, 0)),
        out_shape=jax.ShapeDtypeStruct((D_S5 // LANES, r * CHUNK, LANES), F32),
        compiler_params=_cparams(),
        name="s5_tokens",
    )(y_rows)


def _s5_prep_kernel(lrc_ref, lic_ref, lrr_ref, lir_ref, ls_ref, brt_ref, bit_ref, ctr_ref, cti_ref,
                    m_ref, w_ref, wc_ref, a_ref):
    step = jnp.exp(ls_ref[0])

    def discretise(lr_raw, li):
        lr = jnp.minimum(lr_raw, -1e-4)
        dr, di = lr * step, li * step
        mag = jnp.exp(dr)
        a_re, a_im = mag * jnp.cos(di), mag * jnp.sin(di)
        den = lr * lr + li * li
        nr = a_re - 1.0
        fr = (nr * lr + a_im * li) / den
        fi = (a_im * lr - nr * li) / den
        return dr, di, fr, fi

    dr_c, di_c, _, _ = discretise(lrc_ref[0], lic_ref[0])
    dr_r, di_r, fr_r, fi_r = discretise(lrr_ref[0], lir_ref[0])

    lane = lax.broadcasted_iota(jnp.int32, (1, S5_CONV), 1)
    t_row = lax.broadcasted_iota(jnp.int32, (1, CHUNK), 1).astype(F32)
    t_col = lax.broadcasted_iota(jnp.int32, (CHUNK, 1), 0).astype(F32)
    lag_of_lane = (lax.broadcasted_iota(jnp.int32, (CHUNK, S5_CONV), 1) // S5_GROUP
                   == lax.broadcasted_iota(jnp.int32, (CHUNK, S5_CONV), 0)).astype(F32)
    time_of_row = (lax.broadcasted_iota(jnp.int32, (S5_CONV, CHUNK), 0) // S5_GROUP
                   == lax.broadcasted_iota(jnp.int32, (S5_CONV, CHUNK), 1)).astype(F32)
    chan_of_lane = (lax.broadcasted_iota(jnp.int32, (S5_GROUP, S5_CONV), 1) % S5_GROUP
                    == lax.broadcasted_iota(jnp.int32, (S5_GROUP, S5_CONV), 0)).astype(F32)
    chan_of_row = (lax.broadcasted_iota(jnp.int32, (S5_CONV, S5_GROUP), 0) % S5_GROUP
                   == lax.broadcasted_iota(jnp.int32, (S5_CONV, S5_GROUP), 1)).astype(F32)
    ctr = _spread_cols(ctr_ref[0], chan_of_lane)
    cti = _spread_cols(cti_ref[0], chan_of_lane)

    def c_times_power(tf):
        mag = jnp.exp(dr_c * tf)
        ang = di_c * tf
        pr = _spread_cols(mag * jnp.cos(ang), lag_of_lane)
        pi = _spread_cols(mag * jnp.sin(ang), lag_of_lane)
        return ctr * pr - cti * pi, ctr * pi + cti * pr

    cpr, cpi = c_times_power(t_row)
    bbr = fr_r * brt_ref[0] - fi_r * bit_ref[0]
    bbi = fr_r * bit_ref[0] + fi_r * brt_ref[0]
    kt = _dot(bbr, cpr, HIGHEST) - _dot(bbi, cpi, HIGHEST)
    for s in range(CHUNK):
        shifted = kt if s == 0 else pltpu.roll(kt, S5_GROUP * s, axis=1)
        m_ref[0, S5_GROUP * s:S5_GROUP * (s + 1), :] = jnp.where(
            lane >= S5_GROUP * s, shifted, 0.0).astype(BF16)

    rem = CHUNK - 1.0 - t_col
    magw = jnp.exp(dr_r * rem)
    angw = di_r * rem
    pwr = _spread_rows(time_of_row, magw * jnp.cos(angw))
    pwi = _spread_rows(time_of_row, magw * jnp.sin(angw))
    bbtr = _spread_rows(chan_of_row, bbr)
    bbti = _spread_rows(chan_of_row, bbi)
    w_ref[0, :, :S5_STATE] = pwr * bbtr - pwi * bbti
    w_ref[0, :, S5_STATE:] = pwr * bbti + pwi * bbtr

    c1r, c1i = c_times_power(t_row + 1.0)
    wc_ref[0, :S5_STATE, :] = c1r.astype(BF16)
    wc_ref[0, S5_STATE:, :] = (-c1i).astype(BF16)

    full = float(CHUNK)
    mag_c = jnp.exp(dr_r * full)
    a_ref[0, 0:1, :] = mag_c * jnp.cos(di_r * full)
    a_ref[0, 1:2, :] = mag_c * jnp.sin(di_r * full)


def s5_prep(lam_re, lam_im, log_step, b_re, b_im, c_re, c_im):
    g, p = lam_re.shape
    brt = jnp.swapaxes(b_re, 1, 2)
    bit = jnp.swapaxes(b_im, 1, 2)
    args = (lam_re.reshape(g, p, 1), lam_im.reshape(g, p, 1),
            lam_re.reshape(g, 1, p), lam_im.reshape(g, 1, p), log_step.reshape(g, 1, 1),
            brt, bit, jnp.swapaxes(c_re, 1, 2), jnp.swapaxes(c_im, 1, 2))
    spec = lambda a: pl.BlockSpec((1,) + a.shape[1:], lambda i: (i, 0, 0))
    out_shape = [jax.ShapeDtypeStruct((g, S5_CONV, S5_CONV), BF16),
                 jax.ShapeDtypeStruct((g, S5_CONV, 2 * S5_STATE), F32),
                 jax.ShapeDtypeStruct((g, 2 * S5_STATE, S5_CONV), BF16),
                 jax.ShapeDtypeStruct((g, 2, S5_STATE), F32)]
    return pl.pallas_call(
        _s5_prep_kernel,
        grid=(g,),
        in_specs=[spec(a) for a in args],
        out_specs=[spec(o) for o in out_shape],
        out_shape=out_shape,
        compiler_params=_cparams(),
        name="s5_prep",
    )(*args)


def _s5_main_kernel(up_ref, us_ref, m_ref, w_ref, wc_ref, a_ref, xsr_ref, xsi_ref,
                    yp_ref, ys_ref, fpr_ref, fpi_ref, fsr_ref, fsi_ref, fin_scr, *, n_prompt, n_chunks):
    w3 = jnp.concatenate(_split3(w_ref[0]), axis=1)
    ar = a_ref[0, 0:1, :]
    ai = a_ref[0, 1:2, :]
    width = 2 * S5_STATE

    def local(u):
        h3 = _dot(u, w3)
        nblk = 4
        blk = S5_CONV // nblk
        cols = []
        for tb in range(nblk):
            acc = _dot(u[:, :blk], m_ref[0, :blk, tb * blk:(tb + 1) * blk])
            for sb in range(1, tb + 1):
                acc = acc + _dot(u[:, sb * blk:(sb + 1) * blk], m_ref[0, sb * blk:(sb + 1) * blk, tb * blk:(tb + 1) * blk])
            cols.append(acc)
        return jnp.concatenate(cols, axis=1), h3[:, :width] + h3[:, width:2 * width] + h3[:, 2 * width:]

    def times(pr, pi, x):
        return (jnp.concatenate([pr, pr], axis=1) * x
                + jnp.concatenate([-pi, pi], axis=1) * pltpu.roll(x, S5_STATE, axis=1))

    y_local, x = local(up_ref[0])
    chunk_of_row = lax.broadcasted_iota(jnp.int32, (n_prompt * n_chunks, 1), 0) % n_chunks
    pr, pi = ar, ai
    d = 1
    while d < n_chunks:
        x = x + times(pr, pi, jnp.where(chunk_of_row >= d, pltpu.roll(x, d, axis=0), 0.0))
        pr, pi = pr * pr - pi * pi, 2.0 * pr * pi
        d *= 2
    fin_scr[...] = x
    last = fin_scr[pl.ds(n_chunks - 1, n_prompt, stride=n_chunks), :]
    fpr_ref[0] = last[:, :S5_STATE]
    fpi_ref[0] = last[:, S5_STATE:]
    x0 = jnp.where(chunk_of_row >= 1, pltpu.roll(x, 1, axis=0), 0.0)
    yp_ref[0] = (y_local + _dot(x0.astype(BF16), wc_ref[0])).astype(BF16)

    y_local, hend = local(us_ref[0])
    x0 = jnp.concatenate([xsr_ref[0], xsi_ref[0]], axis=1)
    fin = times(ar, ai, x0) + hend
    fsr_ref[0] = fin[:, :S5_STATE]
    fsi_ref[0] = fin[:, S5_STATE:]
    ys_ref[0] = (y_local + _dot(x0.astype(BF16), wc_ref[0])).astype(BF16)


def s5_main(up_rows, us_rows, m, w, wc, a, xs_re, xs_im, n_prompt, n_chunks):
    g, r, _ = up_rows.shape
    n_sample = xs_re.shape[1]
    spec = lambda shape: pl.BlockSpec((1,) + tuple(shape[1:]), lambda i: (i, 0, 0))
    args = (up_rows, us_rows, m, w, wc, a, xs_re, xs_im)
    out_shape = [jax.ShapeDtypeStruct((g, r, S5_CONV), BF16),
                 jax.ShapeDtypeStruct((g, n_sample, S5_CONV), BF16),
                 jax.ShapeDtypeStruct((g, n_prompt, S5_STATE), F32),
                 jax.ShapeDtypeStruct((g, n_prompt, S5_STATE), F32),
                 jax.ShapeDtypeStruct((g, n_sample, S5_STATE), F32),
                 jax.ShapeDtypeStruct((g, n_sample, S5_STATE), F32)]
    return pl.pallas_call(
        functools.partial(_s5_main_kernel, n_prompt=n_prompt, n_chunks=n_chunks),
        grid=(g,),
        in_specs=[spec(x.shape) for x in args],
        out_specs=[spec(o.shape) for o in out_shape],
        out_shape=out_shape,
        scratch_shapes=[pltpu.VMEM((r, 2 * S5_STATE), F32)],
        compiler_params=_cparams(),
        name="s5_main",
    )(*args)


def _hgrn_kernel(f_ref, z_ref, s0_ref, lb_ref, ng_ref, o_ref, sfin_ref, st_scr, *, n_seq, chunk):
    c = pl.program_id(1)

    @pl.when(c == 0)
    def _():
        st_scr[...] = s0_ref[...]

    lbw = lb_ref[...]
    lbe = jnp.exp(lbw - jnp.max(lbw, axis=0, keepdims=True))
    lb_all = lbe[0:1, :] / jnp.sum(lbe, axis=0, keepdims=True)

    levels = [chunk >> (i + 1) for i in range(chunk.bit_length() - 1)]
    rowi = lax.broadcasted_iota(jnp.int32, (chunk, chunk), 0)
    coli = lax.broadcasted_iota(jnp.int32, (chunk, chunk), 1)
    rowk = lax.broadcasted_iota(jnp.int32, (chunk, HG_D), 0)
    sign, valid = [], []
    for m in levels:
        sign.append(jnp.where((rowk % (2 * m)) >= m, 1.0, -1.0))
        valid.append(((rowi // (2 * m)) == (coli // (2 * m)))
                     & ((rowi % (2 * m)) >= m) & ((coli % (2 * m)) < m))
    cum_mat = (coli <= rowi).astype(BF16)
    cum_mat3 = jnp.concatenate([cum_mat] * 3, axis=1)
    diag = rowi == coli
    nt = (((1,), (1,)), ((), ()))

    def body(n, carry):
        zf = f_ref[n]
        fg_all = lb_all + (1.0 - lb_all) * _sigmoid(zf)
        cums = _dot(cum_mat3, jnp.concatenate(_split3(jnp.log2(fg_all)), axis=0))
        for hd in range(HG_HEADS):
            cols = slice(hd * HG_D, (hd + 1) * HG_D)
            zq = z_ref[n, :, hd * HG_D:(hd + 1) * HG_D].astype(F32)
            vb = z_ref[n, :, D_HG + hd * HG_D:D_HG + (hd + 1) * HG_D]
            zg = z_ref[n, :, 2 * D_HG + hd * HG_D:2 * D_HG + (hd + 1) * HG_D].astype(F32)
            q = zq * _sigmoid(zq)
            kk = 1.0 - fg_all[:, cols]
            bcum = cums[:chunk, cols]
            b_last = bcum[chunk - 1:chunk, :]
            qb = q.astype(BF16)
            kb = kk.astype(BF16)
            st = st_scr[n, hd]

            scores = jnp.where(diag, lax.dot_general(qb, kb, nt, preferred_element_type=F32), 0.0)
            for lvl, m in enumerate(levels):
                if 2 * m >= SUBLANES:
                    bref = jnp.concatenate(
                        [jnp.broadcast_to(bcum[b * 2 * m + m - 1:b * 2 * m + m, :], (2 * m, HG_D))
                         for b in range(chunk // (2 * m))], axis=0)
                else:
                    offs = rowk % (2 * m) - (m - 1)
                    bref = bcum
                    for o in range(-(m - 1), m + 1):
                        if o != 0:
                            bref = jnp.where(offs == o, pltpu.roll(bcum, o % chunk, axis=0), bref)
                dec = jnp.exp2((bcum - bref) * sign[lvl]).astype(BF16)
                sc = lax.dot_general(qb * dec, kb * dec, nt, preferred_element_type=F32)
                scores = jnp.where(valid[lvl], sc, scores)

            qd = (q * jnp.exp2(bcum)).astype(BF16)
            o = lax.dot_general(qd, st.astype(BF16), nt, preferred_element_type=F32)
            o = o + _dot(scores.astype(BF16), vb)
            kdec = (kk * jnp.exp2(b_last - bcum)).astype(BF16)
            st_scr[n, hd] = jnp.exp2(b_last) * st + lax.dot_general(
                vb, kdec, (((0,), (0,)), ((), ())), preferred_element_type=F32)

            on = o * lax.rsqrt(jnp.mean(o * o, axis=-1, keepdims=True) + RMS_EPS) * ng_ref[:, cols]
            o_ref[n, :, hd * HG_D:(hd + 1) * HG_D] = on * (zg * _sigmoid(zg))
        return carry

    lax.fori_loop(0, n_seq, body, 0, unroll=True)

    @pl.when(c == pl.num_programs(1) - 1)
    def _():
        sfin_ref[...] = st_scr[...]


def hgrn(f, z, s0_t, hg_lb, norm_g, n_seq, chunk):
    n, length, _ = z.shape
    return pl.pallas_call(
        functools.partial(_hgrn_kernel, n_seq=n_seq, chunk=chunk),
        grid=(n // n_seq, length // chunk),
        in_specs=[pl.BlockSpec((n_seq, chunk, D_HG), lambda g, c: (g, c, 0)),
                  pl.BlockSpec((n_seq, chunk, 3 * D_HG), lambda g, c: (g, c, 0)),
                  pl.BlockSpec((n_seq, HG_HEADS, HG_D, HG_D), lambda g, c: (g, 0, 0, 0)),
                  pl.BlockSpec(hg_lb.shape, lambda g, c: (0, 0)),
                  pl.BlockSpec((1, D_HG), lambda g, c: (0, 0))],
        out_specs=[pl.BlockSpec((n_seq, chunk, D_HG), lambda g, c: (g, c, 0)),
                   pl.BlockSpec((n_seq, HG_HEADS, HG_D, HG_D), lambda g, c: (g, 0, 0, 0))],
        out_shape=[jax.ShapeDtypeStruct((n, length, D_HG), F32),
                   jax.ShapeDtypeStruct((n, HG_HEADS, HG_D, HG_D), F32)],
        scratch_shapes=[pltpu.VMEM((n_seq, HG_HEADS, HG_D, HG_D), F32)],
        compiler_params=_cparams(2),
        name="hgrn",
    )(f, z, s0_t, hg_lb, norm_g)


def _post_mix_kernel(h_ref, up_ref, us_ref, yp_ref, ys_ref, hgp_ref, hgs_ref,
                     d_ref, wglu_ref, bglu_ref, s5g_ref, wout_ref, g1_ref, b1_ref, rwt_ref, rb_ref,
                     h1_ref, slot_ref, gate_ref, before_ref, cnt_ref, run_scr, *, tm, n_first):
    i = pl.program_id(0)

    @pl.when(i == 0)
    def _():
        run_scr[...] = jnp.zeros_like(run_scr)

    def phase(u_ref, y_ref, hg_ref):
        ys = _read_lane_blocks(y_ref) + d_ref[...] * _read_lane_blocks(u_ref)
        gl = 0.5 * ys * (1.0 + lax.erf(ys * (2.0 ** -0.5)))
        s5o = gl * _sigmoid(_dot(gl.astype(BF16), wglu_ref[...]) + bglu_ref[...])
        s5o = s5o * lax.rsqrt(jnp.mean(s5o * s5o, axis=-1, keepdims=True) + RMS_EPS) * s5g_ref[...]
        mix = (_dot(s5o.astype(BF16), wout_ref[:D_S5, :])
               + _dot(hg_ref[...].astype(BF16), wout_ref[D_S5:, :]))
        h1 = _layernorm(DEEPNORM_ALPHA * h_ref[...] + mix, g1_ref[...], b1_ref[...])
        _write_row_tiles(h1_ref, h1, tm)

        h_hi, h_mid, _ = _split3(h1)
        w_hi, w_mid, _ = _split3(rwt_ref[...])
        nt = (((1,), (1,)), ((), ()))
        logits = (lax.dot_general(w_hi, h_hi, nt, preferred_element_type=F32)
                  + lax.dot_general(w_hi, h_mid, nt, preferred_element_type=F32)
                  + lax.dot_general(w_mid, h_hi, nt, preferred_element_type=F32)) + rb_ref[...]
        eid = lax.broadcasted_iota(jnp.int32, (N_EXPERTS, tm), 0)
        vals, idxs = [], []
        for _ in range(TOP_K):
            m = jnp.max(logits, axis=0, keepdims=True)
            ix = jnp.min(jnp.where(logits == m, eid, N_EXPERTS), axis=0, keepdims=True)
            vals.append(m)
            idxs.append(ix)
            logits = jnp.where(eid == ix, -jnp.inf, logits)
        exps = [jnp.exp(v - vals[0]) for v in vals]
        den = exps[0] + exps[1] + exps[2] + exps[3]

        onehot = jnp.zeros((N_EXPERTS, tm), F32)
        for ix in idxs:
            onehot = onehot + (eid == ix).astype(F32)
        rowi = lax.broadcasted_iota(jnp.int32, (tm, tm), 0)
        coli = lax.broadcasted_iota(jnp.int32, (tm, tm), 1)
        earlier = (rowi < coli).astype(BF16)
        prefix = _dot(onehot.astype(BF16), earlier)
        tile_cnt = jnp.sum(onehot, axis=1, keepdims=True)
        for k in range(TOP_K):
            lower_experts = jnp.sum(jnp.where(eid < idxs[k], tile_cnt, 0.0), axis=0, keepdims=True)
            rank = jnp.sum(jnp.where(eid == idxs[k], prefix, 0.0), axis=0, keepdims=True)
            slot_ref[0, :, k * tm:(k + 1) * tm] = (lower_experts + rank).astype(jnp.int32) * ROW_TILES
            gate_ref[0, :, k * tm:(k + 1) * tm] = exps[k] / den
        before_ref[0] = run_scr[...]
        run_scr[...] = run_scr[...] + tile_cnt
        cnt_ref[...] = run_scr[...]

    pl.when(i < n_first)(lambda: phase(up_ref, yp_ref, hgp_ref))
    pl.when(i >= n_first)(lambda: phase(us_ref, ys_ref, hgs_ref))


def post_mix(h0, u_pair, y_pair, hg_pair, d_skip, wglu, bglu, s5g, wout, g1, b1, rw_t, rb_col, tm):
    t = h0.shape[0]
    n_first = u_pair[0].shape[1] // tm
    row = lambda i: (i, 0)
    fixed = lambda i: (0, 0)
    full = lambda a: pl.BlockSpec(a.shape, fixed)
    weights = (d_skip, wglu, bglu, s5g, wout, g1, b1, rw_t, rb_col)
    return pl.pallas_call(
        functools.partial(_post_mix_kernel, tm=tm, n_first=n_first),
        grid=(t // tm,),
        in_specs=[pl.BlockSpec((tm, D_MODEL), row)]
                 + _two_phase_lane_block_specs(tm, n_first) * 2 + _two_phase_specs((tm, D_HG), n_first)
                 + [full(a) for a in weights],
        out_specs=[pl.BlockSpec((tm * ROW_TILES, LANES), row),
                   pl.BlockSpec((1, 1, TOP_K * tm), lambda i: (i, 0, 0)),
                   pl.BlockSpec((1, 1, TOP_K * tm), lambda i: (i, 0, 0)),
                   pl.BlockSpec((1, N_EXPERTS, 1), lambda i: (i, 0, 0)),
                   pl.BlockSpec((N_EXPERTS, 1), fixed)],
        out_shape=[jax.ShapeDtypeStruct((t * ROW_TILES, LANES), F32),
                   jax.ShapeDtypeStruct((t // tm, 1, TOP_K * tm), jnp.int32),
                   jax.ShapeDtypeStruct((t // tm, 1, TOP_K * tm), F32),
                   jax.ShapeDtypeStruct((t // tm, N_EXPERTS, 1), F32),
                   jax.ShapeDtypeStruct((N_EXPERTS, 1), F32)],
        scratch_shapes=[pltpu.VMEM((N_EXPERTS, 1), F32)],
        compiler_params=_cparams(),
        name="post_mix",
    )(h0, *u_pair, *y_pair, *hg_pair, *weights)


def _segment_copies(meta_ref, tile, tm, make_copy):
    for e in range(N_EXPERTS):
        sorted_row = meta_ref[tile, e]
        cnt = meta_ref[tile, N_EXPERTS + e]
        staged_row = meta_ref[tile, 2 * N_EXPERTS + e]
        for b in range(tm.bit_length()):
            done = cnt & ((1 << b) - 1)

            @pl.when(((cnt >> b) & 1) == 1)
            def _(b=b, done=done, e=e):
                make_copy(staged_row + done, sorted_row + done, 1 << b).start(priority=e % 2)


def _dispatch_kernel(meta_ref, pend_ref, slot_ref, h_ref, xs_ref, stage, zero_scr, zsem, sem, *, tm):
    n_rows = xs_ref.shape[0] // ROW_TILES
    i = pl.program_id(0)
    n = pl.num_programs(0)
    slot = i % 2

    def drain(s):
        pltpu.make_async_copy(stage.at[s], xs_ref.at[pl.ds(0, TOP_K * tm * ROW_TILES)], sem.at[s]).wait()

    @pl.when(i == 0)
    def _():
        zero_scr[...] = jnp.zeros_like(zero_scr)

        def last_block(e):
            prev = pend_ref[e - 1] if e > 0 else 0
            copy = pltpu.make_async_copy(
                zero_scr, xs_ref.at[pl.ds(pl.multiple_of(jnp.maximum(pend_ref[e] - MOE_ROWS, 0) * ROW_TILES,
                                                         ROW_TILES), MOE_ROWS * ROW_TILES)], zsem)
            return pend_ref[e] > prev, copy

        def tail_block(j):
            row0 = pend_ref[N_EXPERTS - 1] + j * MOE_ROWS
            copy = pltpu.make_async_copy(
                zero_scr, xs_ref.at[pl.ds(pl.multiple_of(jnp.minimum(row0, n_rows - MOE_ROWS) * ROW_TILES,
                                                         ROW_TILES), MOE_ROWS * ROW_TILES)], zsem)
            return row0 < n_rows, copy

        blocks = [last_block(e) for e in range(N_EXPERTS)] + [tail_block(j) for j in range(N_EXPERTS)]
        for used, copy in blocks:
            pl.when(used)(copy.start)
        for used, copy in blocks:
            pl.when(used)(copy.wait)

    pl.when(i >= 2)(lambda: drain(slot))

    def regroup(s):
        def body(t, carry):
            row = h_ref[_row_tile(t), :]
            for k in range(TOP_K):
                stage[s, pl.ds(pl.multiple_of(slot_ref[k * tm + t], ROW_TILES), ROW_TILES), :] = row
            return carry

        lax.fori_loop(0, tm, body, 0, unroll=8)

    pl.when(slot == 0)(lambda: regroup(0))
    pl.when(slot == 1)(lambda: regroup(1))
    _segment_copies(meta_ref, i, tm, lambda staged_row, sorted_row, rows: pltpu.make_async_copy(
        stage.at[slot, _row_tiles(staged_row, rows)], xs_ref.at[_row_tiles(sorted_row, rows)], sem.at[slot]))

    @pl.when(i == n - 1)
    def _():
        pl.when(n >= 2)(lambda: drain(1 - slot))
        drain(slot)


def dispatch(meta, pend, slots_flat, h1, n_rows, tm):
    t = h1.shape[0] // ROW_TILES
    grid_spec = pltpu.PrefetchScalarGridSpec(
        num_scalar_prefetch=2,
        grid=(t // tm,),
        in_specs=[pl.BlockSpec((TOP_K * tm,), lambda i, meta, pend: (i,), memory_space=pltpu.SMEM),
                  pl.BlockSpec((tm * ROW_TILES, LANES), lambda i, meta, pend: (i, 0))],
        out_specs=pl.BlockSpec(memory_space=pl.ANY),
        scratch_shapes=[pltpu.VMEM((2, TOP_K * tm * ROW_TILES, LANES), F32),
                        pltpu.VMEM((MOE_ROWS * ROW_TILES, LANES), F32),
                        pltpu.SemaphoreType.DMA(()), pltpu.SemaphoreType.DMA((2,))],
    )
    return pl.pallas_call(
        functools.partial(_dispatch_kernel, tm=tm),
        grid_spec=grid_spec,
        out_shape=jax.ShapeDtypeStruct((n_rows * ROW_TILES, LANES), F32),
        compiler_params=_cparams(),
        name="moe_dispatch",
    )(meta, pend, slots_flat, h1)


def _moe_ffn_kernel(be_ref, nu_ref, seg_ref, nxt_ref, x_ref, wg_ref, bg_ref, wu_ref, bu_ref, wd_ref, bd_ref,
                    y_ref, wbuf, wbf, sem):
    i = pl.program_id(0)
    hbm = (wg_ref, wu_ref, wd_ref)

    def weight_copies(expert, s):
        return [pltpu.make_async_copy(hbm[j].at[expert], wbuf.at[s, j], sem.at[s, j]) for j in range(3)]

    @pl.when((i == 0) | (be_ref[i] != be_ref[jnp.maximum(i - 1, 0)]))
    def _():
        s = seg_ref[i] % 2

        @pl.when(i == 0)
        def _():
            for c in weight_copies(be_ref[0], 0):
                c.start()

        for j, c in enumerate(weight_copies(be_ref[i], s)):
            c.wait()
            wbf[j] = wbuf[s, j].astype(BF16)

        @pl.when(nxt_ref[i] >= 0)
        def _():
            for c in weight_copies(nxt_ref[i], 1 - s):
                c.start()

    @pl.when(i < nu_ref[0])
    def _():
        x = _read_row_tiles(x_ref, MOE_ROWS).astype(BF16)
        gt = jnp.minimum(_dot(x, wbf[0]) + bg_ref[0], SWIGLU_LIMIT)
        up = jnp.clip(_dot(x, wbf[1]) + bu_ref[0], -SWIGLU_LIMIT, SWIGLU_LIMIT)
        hid = (up + 1.0) * (gt * _sigmoid(SWIGLU_ALPHA * gt))
        _write_row_tiles(y_ref, _dot(hid.astype(BF16), wbf[2]) + bd_ref[0], MOE_ROWS)

    @pl.when(i >= nu_ref[0])
    def _():
        y_ref[...] = jnp.zeros_like(y_ref)


def moe_ffn(block_e, n_used, segment, next_e, xs, wg, bg, wu, bu, wd, bd):
    n_rows = xs.shape[0] // ROW_TILES
    n_blocks = n_rows // MOE_ROWS
    wsel = lambda i, be, nu, seg, nxt: (be[i], 0, 0)
    d_ff = wg.shape[-1]
    assert wg.shape[1:] == wu.shape[1:] == wd.shape[1:] == (D_MODEL, D_MODEL)
    anywhere = pl.BlockSpec(memory_space=pl.ANY)
    grid_spec = pltpu.PrefetchScalarGridSpec(
        num_scalar_prefetch=4,
        grid=(n_blocks,),
        in_specs=[pl.BlockSpec((MOE_ROWS * ROW_TILES, LANES),
                               lambda i, be, nu, seg, nxt: (jnp.minimum(i, nu[0] - 1), 0)),
                  anywhere, pl.BlockSpec((1, 1, d_ff), wsel),
                  anywhere, pl.BlockSpec((1, 1, d_ff), wsel),
                  anywhere, pl.BlockSpec((1, 1, D_MODEL), wsel)],
        out_specs=pl.BlockSpec((MOE_ROWS * ROW_TILES, LANES), lambda i, be, nu, seg, nxt: (i, 0)),
        scratch_shapes=[pltpu.VMEM((2, 3, D_MODEL, D_MODEL), F32), pltpu.VMEM((3, D_MODEL, D_MODEL), BF16),
                        pltpu.SemaphoreType.DMA((2, 3))],
    )
    return pl.pallas_call(
        _moe_ffn_kernel,
        grid_spec=grid_spec,
        out_shape=jax.ShapeDtypeStruct((n_rows * ROW_TILES, LANES), F32),
        compiler_params=_cparams(),
        name="moe_ffn",
    )(block_e, n_used, segment, next_e, xs, wg, bg, wu, bu, wd, bd)


def _combine_kernel(meta_ref, slot_ref, gate_ref, h_ref, pp_ref, ps_ref, yb_ref,
                    plew_ref, plegw_ref, g2_ref, b2_ref, outp_ref, outs_ref, buf, r_scr, sem,
                    *, tm, n_first):
    i = pl.program_id(0)
    n = pl.num_programs(0)
    slot = i % 2

    def fetch(tile, s):
        _segment_copies(meta_ref, tile, tm, lambda staged_row, sorted_row, rows: pltpu.make_async_copy(
            yb_ref.at[_row_tiles(sorted_row, rows)], buf.at[s, _row_tiles(staged_row, rows)], sem.at[s]))

    pl.when(i == 0)(lambda: fetch(0, 0))
    pl.when(i + 1 < n)(lambda: fetch(i + 1, 1 - slot))
    pltpu.make_async_copy(yb_ref.at[pl.ds(0, TOP_K * tm * ROW_TILES)], buf.at[slot], sem.at[slot]).wait()

    def weighted_sum(s):
        def body(t, carry):
            acc = DEEPNORM_ALPHA * h_ref[_row_tile(t), :]
            for k in range(TOP_K):
                acc = acc + gate_ref[k * tm + t] * buf[s, pl.ds(pl.multiple_of(slot_ref[k * tm + t], ROW_TILES),
                                                               ROW_TILES), :]
            r_scr[_row_tile(t), :] = acc
            return carry

        lax.fori_loop(0, tm, body, 0, unroll=8)

    pl.when(slot == 0)(lambda: weighted_sum(0))
    pl.when(slot == 1)(lambda: weighted_sum(1))
    r = _read_row_tiles(r_scr, tm)
    gate = _sigmoid(_dot(r.astype(BF16), plegw_ref[...]))

    def finish(p, store):
        e = _dot(p.astype(BF16), plew_ref[...]) * gate
        store(_layernorm(r + e, g2_ref[...], b2_ref[...]))

    def store_prompt(v):
        outp_ref[0] = v

    def store_sample(v):
        outs_ref[...] = v

    pl.when(i < n_first)(lambda: finish(pp_ref[0], store_prompt))
    pl.when(i >= n_first)(lambda: finish(ps_ref[...], store_sample))


def combine(meta, slots_flat, gates_flat, h1, p_prompt, p_sample, yb, plew, plegw, g2, b2, tm):
    t = h1.shape[0] // ROW_TILES
    nb, seq, _ = p_prompt.shape
    ts = p_sample.shape[0]
    n_first = nb * seq // tm
    fixed = lambda i, meta: (0, 0)
    flat = pl.BlockSpec((TOP_K * tm,), lambda i, meta: (i,), memory_space=pltpu.SMEM)
    sample = lambda width: pl.BlockSpec((tm, width), lambda i, meta: (jnp.maximum(i - n_first, 0), 0))
    prompt = lambda width: pl.BlockSpec((1, tm, width), _prompt_spec(tm, seq, width, n_first).index_map)
    with_meta = lambda spec: pl.BlockSpec(spec.block_shape, lambda i, meta: spec.index_map(i))
    grid_spec = pltpu.PrefetchScalarGridSpec(
        num_scalar_prefetch=1,
        grid=(t // tm,),
        in_specs=[flat, flat,
                  pl.BlockSpec((tm * ROW_TILES, LANES), lambda i, meta: (i, 0)),
                  with_meta(prompt(PLE_DIM)), sample(PLE_DIM),
                  pl.BlockSpec(memory_space=pl.ANY),
                  pl.BlockSpec(plew.shape, fixed), pl.BlockSpec(plegw.shape, fixed),
                  pl.BlockSpec((1, D_MODEL), fixed), pl.BlockSpec((1, D_MODEL), fixed)],
        out_specs=[with_meta(prompt(D_MODEL)), sample(D_MODEL)],
        scratch_shapes=[pltpu.VMEM((2, TOP_K * tm * ROW_TILES, LANES), F32),
                        pltpu.VMEM((tm * ROW_TILES, LANES), F32),
                        pltpu.SemaphoreType.DMA((2,))],
    )
    return pl.pallas_call(
        functools.partial(_combine_kernel, tm=tm, n_first=n_first),
        grid_spec=grid_spec,
        out_shape=[jax.ShapeDtypeStruct((nb, seq, D_MODEL), F32), jax.ShapeDtypeStruct((ts, D_MODEL), F32)],
        compiler_params=_cparams(),
        name="moe_combine",
    )(meta, slots_flat, gates_flat, h1, p_prompt, p_sample, yb, plew, plegw, g2, b2)


def _row(v):
    return v.reshape(1, -1)


def kernel(x_prompt, x_sample, state_s5_re, state_s5_im, state_hgrn, p_prompt, p_sample, ln_in_g, ln_in_b, w_in, s5_lambda_re, s5_lambda_im, s5_log_step, s5_b_re, s5_b_im, s5_c_re, s5_c_im, s5_d, s5_w_glu, s5_b_glu, s5_norm_g, hg_lb, hg_norm_g, w_out, ln1_g, ln1_b, router_w, router_b, w_gate, b_gate, w_up, b_up, w_down, b_down, ple_w, ple_gate_w, ln2_g, ln2_b):
    nb, seq, _ = x_prompt.shape
    ns, dseq, _ = x_sample.shape
    assert dseq == CHUNK and seq % CHUNK == 0 and w_in.shape[0] == 1
    nc = seq // CHUNK
    tp, ts = nb * seq, ns * dseq
    t = tp + ts
    tm = 512 if (tp % 512 == 0 and ts % 512 == 0) else 256
    assert tp % tm == 0 and ts % tm == 0 and seq % tm == 0 and ns % nb == 0

    w_cols = jnp.split(w_in[0], [D_S5, D_S5 + D_HG, D_S5 + 2 * D_HG], axis=1)
    h0, u_p, u_s, f_p, f_s, z_p, z_s = ln_in_proj(
        x_prompt, x_sample.reshape(ts, D_MODEL), _row(ln_in_g), _row(ln_in_b),
        jnp.concatenate([w_cols[0], w_cols[2], w_cols[1], w_cols[3]], axis=1).astype(BF16), tm)

    m, w, wc, a = s5_prep(s5_lambda_re[0], s5_lambda_im[0], s5_log_step[0],
                          s5_b_re[0], s5_b_im[0], s5_c_re[0], s5_c_im[0])
    rb = 64 if (nb * nc) % 64 == 0 else nc
    yp_rows, ys_rows, fpr, fpi, fsr, fsi = s5_main(s5_rows(u_p, rb), s5_rows(u_s, ns), m, w, wc, a,
                                                   jnp.swapaxes(state_s5_re[0], 0, 1),
                                                   jnp.swapaxes(state_s5_im[0], 0, 1), nb, nc)
    y_pair = (s5_tokens(yp_rows, rb), s5_tokens(ys_rows, ns))

    zero_state = jnp.zeros((nb, HG_HEADS, HG_D, HG_D), F32)
    ng = _row(hg_norm_g[0])
    o_p, st_p = hgrn(f_p.reshape(nb, seq, D_HG), z_p.reshape(nb, seq, 3 * D_HG), zero_state, hg_lb, ng, nb,
                     HG_CHUNK if seq % HG_CHUNK == 0 else CHUNK)
    o_s, st_s = hgrn(f_s.reshape(ns, dseq, D_HG), z_s.reshape(ns, dseq, 3 * D_HG), jnp.swapaxes(state_hgrn[0], 2, 3),
                     hg_lb, ng, nb, dseq)

    h1, slots, gates, before, counts = post_mix(
        h0, (u_p, u_s), y_pair, (o_p.reshape(tp, D_HG), o_s.reshape(ts, D_HG)),
        _row(s5_d[0]), s5_w_glu[0].astype(BF16), _row(s5_b_glu[0]),
        _row(s5_norm_g[0]), w_out[0].astype(BF16), _row(ln1_g[0]), _row(ln1_b[0]),
        router_w[0].T, router_b[0].reshape(N_EXPERTS, 1), tm)

    counts = counts[:, 0].astype(jnp.int32)
    before = before[:, :, 0].astype(jnp.int32)
    cnt = jnp.concatenate([before[1:], counts[None]], axis=0) - before
    padded = (counts + MOE_ROWS - 1) // MOE_ROWS * MOE_ROWS
    pend = jnp.cumsum(padded)
    staged = jnp.cumsum(cnt, axis=1) - cnt
    meta = jnp.concatenate([pend - padded + before, cnt, staged, jnp.zeros_like(cnt)], axis=1)
    experts = jnp.arange(N_EXPERTS, dtype=jnp.int32)

    n_blocks = -(-t * TOP_K // MOE_ROWS) + N_EXPERTS
    n_used = (pend[-1] // MOE_ROWS).astype(jnp.int32)
    blk = jnp.arange(n_blocks, dtype=jnp.int32)
    blk = jnp.minimum(blk, n_used - 1)
    block_e = jnp.sum((pend[None, :] <= (blk * MOE_ROWS)[:, None]).astype(jnp.int32), axis=1)
    block_e = jnp.minimum(block_e, N_EXPERTS - 1)
    owns_rows = padded > 0
    owner = owns_rows[None, :]
    segment = jnp.sum((owner & (experts[None, :] <= block_e[:, None])).astype(jnp.int32), axis=1) - 1
    later = jnp.where(owner & (experts[None, :] > block_e[:, None]), experts[None, :], N_EXPERTS)
    next_e = jnp.min(later, axis=1)
    next_e = jnp.where(next_e < N_EXPERTS, next_e, -1).astype(jnp.int32)

    slots_flat = slots.reshape(-1)
    xs = dispatch(meta, pend, slots_flat, h1, n_blocks * MOE_ROWS, tm)
    yb = moe_ffn(block_e, n_used.reshape(1), segment, next_e, xs,
                 w_gate[0], b_gate[0][:, None, :], w_up[0], b_up[0][:, None, :],
                 w_down[0], b_down[0][:, None, :])
    out_p, out_s = combine(meta, slots_flat, gates.reshape(-1), h1, p_prompt[0], p_sample[0].reshape(ts, PLE_DIM),
                           yb, ple_w[0].astype(BF16), ple_gate_w[0].astype(BF16),
                           _row(ln2_g[0]), _row(ln2_b[0]), tm)

    def s5_state(f, n):
        return jnp.swapaxes(f, 0, 1).reshape(1, n, S5_GROUPS, S5_STATE)

    return (out_p, out_s.reshape(ns, dseq, D_MODEL),
            s5_state(fpr, nb), s5_state(fpi, nb), jnp.swapaxes(st_p, 2, 3)[None],
            s5_state(fsr, ns), s5_state(fsi, ns), jnp.swapaxes(st_s, 2, 3)[None])
```
